```python
import math
import jax, jax.numpy as jnp
from jax import lax
import numpy as np

D_MODEL = 1024
BATCH = 8
SEQ = 2048
DEPTH = 2

N_A_LAYERS = DEPTH // 2
N_B_LAYERS = DEPTH - N_A_LAYERS
DN_ALPHA = (2 * DEPTH) ** 0.25
DN_BETA = (8 * DEPTH) ** -0.25
LN_EPS = 1e-5

SSM_EXPAND = 2
SSM_D_INNER = SSM_EXPAND * D_MODEL
SSM_HEAD_DIM = 64
SSM_HEADS = SSM_D_INNER // SSM_HEAD_DIM
SSM_GROUPS = 4
SSM_STATE = 128
SSM_CONV = 4
SSM_CHUNK = 128
SSM_CONV_DIM = SSM_D_INNER + 2 * SSM_GROUPS * SSM_STATE
SSM_IN_DIM = SSM_D_INNER + SSM_CONV_DIM + SSM_HEADS

ATTN_HEAD_DIM = 64
ATTN_GROUPS = ((128, 1, 6), (512, 4, 5), (2048, 16, 5))
ATTN_HEADS = sum(g[2] for g in ATTN_GROUPS)
ATTN_WIDTH = ATTN_HEADS * ATTN_HEAD_DIM
ATTN_BLOCK = 128
N_BUCKETS = 32
MAX_DISTANCE = 2048

MOE_GROUPS = 4
MOE_EXPERTS_PER_GROUP = 8
MOE_TOP_K = 2
MOE_D_FF = 512

kernel_name = 'yoco_ssd_dilated_attn_hmoe'


def layer_norm(x, g, b):
    xf = x.astype(jnp.float32)
    mu = jnp.mean(xf, -1, keepdims=True)
    var = jnp.mean(jnp.square(xf - mu), -1, keepdims=True)
    return ((xf - mu) * lax.rsqrt(var + LN_EPS) * g.astype(jnp.float32) + b.astype(jnp.float32)).astype(x.dtype)


def causal_depthwise_conv(u, w, b):
    k, c = w.shape
    out = lax.conv_general_dilated(u, w[:, None, :], window_strides=(1,), padding=[(k - 1, 0)],
                                   dimension_numbers=('NWC', 'WIO', 'NWC'), feature_group_count=c)
    return out + b


def ssd_chunked(x, dt, a, bm, cm):
    b, s, h, p = x.shape
    g, n = bm.shape[2], bm.shape[3]
    r = h // g
    c = s // SSM_CHUNK
    xc = (x * dt[..., None].astype(x.dtype)).reshape(b, c, SSM_CHUNK, g, r, p)
    adt = (dt * a).reshape(b, c, SSM_CHUNK, g, r).transpose(0, 1, 3, 4, 2)
    acs = jnp.cumsum(adt, axis=-1)
    bc = bm.reshape(b, c, SSM_CHUNK, g, n)
    cc = cm.reshape(b, c, SSM_CHUNK, g, n)
    tri = np.tril(np.ones((SSM_CHUNK, SSM_CHUNK), dtype=bool))
    decay_in = jnp.exp(jnp.where(tri, acs[..., :, None] - acs[..., None, :], -jnp.inf))
    cb = jnp.einsum('bcign,bcjgn->bcgij', cc, bc)
    y_diag = jnp.einsum('bcgrij,bcjgrp->bcigrp', cb[:, :, :, None] * decay_in, xc)
    decay_states = jnp.exp(acs[..., -1:] - acs)
    states = jnp.einsum('bcjgn,bcgrj,bcjgrp->bcgrpn', bc, decay_states, xc)
    chunk_decay = jnp.exp(acs[..., -1])

    def step(carry, inp):
        st, dec = inp
        return carry * dec[..., None, None] + st, carry

    _, prev = lax.scan(step, jnp.zeros_like(states[:, 0]),
                       (jnp.swapaxes(states, 0, 1), jnp.swapaxes(chunk_decay, 0, 1)))
    prev = jnp.swapaxes(prev, 0, 1)
    y_off = jnp.einsum('bcign,bcgrpn,bcgri->bcigrp', cc, prev, jnp.exp(acs))
    return (y_diag + y_off).reshape(b, s, h, p)


def mamba2_mixer(h, in_w, conv_w, conv_b, dt_bias, a_log, d_skip, norm_w, out_w):
    bsz, s, _ = h.shape
    gn = SSM_GROUPS * SSM_STATE
    zxbcdt = h @ in_w
    z = zxbcdt[..., :SSM_D_INNER]
    xbc = zxbcdt[..., SSM_D_INNER:SSM_D_INNER + SSM_CONV_DIM]
    dt_raw = zxbcdt[..., SSM_D_INNER + SSM_CONV_DIM:]
    xbc = jax.nn.silu(causal_depthwise_conv(xbc, conv_w, conv_b))
    xs = xbc[..., :SSM_D_INNER].reshape(bsz, s, SSM_HEADS, SSM_HEAD_DIM)
    bm = xbc[..., SSM_D_INNER:SSM_D_INNER + gn].reshape(bsz, s, SSM_GROUPS, SSM_STATE)
    cm = xbc[..., SSM_D_INNER + gn:].reshape(bsz, s, SSM_GROUPS, SSM_STATE)
    dt = jax.nn.softplus(dt_raw.astype(jnp.float32) + dt_bias.astype(jnp.float32))
    a = -jnp.exp(a_log.astype(jnp.float32))
    y = ssd_chunked(xs, dt, a, bm, cm) + d_skip[:, None] * xs
    y = y.reshape(bsz, s, SSM_D_INNER) * jax.nn.silu(z)
    yg = y.astype(jnp.float32).reshape(bsz, s, SSM_GROUPS, -1)
    yg = yg * lax.rsqrt(jnp.mean(jnp.square(yg), -1, keepdims=True) + LN_EPS)
    y = (yg.reshape(bsz, s, SSM_D_INNER) * norm_w).astype(h.dtype)
    return y @ out_w


def t5_bucket(dist):
    max_exact = N_BUCKETS // 2
    n = np.maximum(dist, 1).astype(np.float64)
    large = max_exact + (np.log(n / max_exact) / np.log(MAX_DISTANCE / max_exact)
                         * (N_BUCKETS - max_exact)).astype(np.int32)
    large = np.minimum(large, N_BUCKETS - 1)
    return np.where(dist < max_exact, dist, large).astype(np.int32)


def dilated_group(q, k, v, bias_tab, window, dil):
    bsz, s, hg, dh = q.shape
    w_sub = window // dil
    l = s // dil
    nb = -(-l // ATTN_BLOCK)
    lp = nb * ATTN_BLOCK

    def split(t):
        return t.reshape(bsz, l, dil, hg, dh)

    qb = jnp.pad(split(q), ((0, 0), (0, lp - l), (0, 0), (0, 0), (0, 0))).reshape(bsz, nb, ATTN_BLOCK, dil, hg, dh)

    def kblocks(t):
        tp = jnp.pad(split(t), ((0, 0), (ATTN_BLOCK, lp - l), (0, 0), (0, 0), (0, 0)))
        tp = tp.reshape(bsz, nb + 1, ATTN_BLOCK, dil, hg, dh)
        return jnp.concatenate([tp[:, :-1], tp[:, 1:]], axis=2)

    kb, vb = kblocks(k), kblocks(v)
    sc = jnp.einsum('bnqrhe,bnkrhe->bnrhqk', qb, kb).astype(jnp.float32) * (ATTN_HEAD_DIM ** -0.5)
    delta = (np.arange(ATTN_BLOCK)[:, None] + ATTN_BLOCK) - np.arange(2 * ATTN_BLOCK)[None, :]
    band = (delta >= 0) & (delta <= w_sub)
    kpos = np.arange(nb)[:, None] * ATTN_BLOCK + np.arange(2 * ATTN_BLOCK)[None, :] - ATTN_BLOCK
    valid = band[None] & (kpos >= 0)[:, None, :]
    bucket = t5_bucket(np.clip(delta, 0, None) * dil)
    bias = jnp.transpose(bias_tab.astype(jnp.float32)[bucket], (2, 0, 1))
    sc = jnp.where(valid[None, :, None, None], sc + bias[None, None, None], -jnp.inf)
    m = jnp.max(sc, -1, keepdims=True)
    pr = jnp.exp(sc - m)
    den = jnp.sum(pr, -1)
    o = jnp.einsum('bnrhqk,bnkrhe->bnqrhe', pr.astype(v.dtype), vb)
    o = o / jnp.transpose(den, (0, 1, 4, 2, 3))[..., None]
    lse = jnp.transpose(m[..., 0] + jnp.log(den), (0, 1, 4, 2, 3))
    o = o.reshape(bsz, lp, dil, hg, dh)[:, :l].reshape(bsz, s, hg, dh)
    lse = lse.reshape(bsz, lp, dil, hg)[:, :l].reshape(bsz, s, hg)
    return o, lse


def dilated_attention(h, q_w, k_sh, v_sh, o_w, rel_bias):
    bsz, s, _ = h.shape
    q = (h @ q_w).reshape(bsz, s, ATTN_HEADS, ATTN_HEAD_DIM)
    outs, lses = [], []
    h0 = 0
    for window, dil, nh in ATTN_GROUPS:
        o, lse = dilated_group(q[:, :, h0:h0 + nh], k_sh[:, :, h0:h0 + nh], v_sh[:, :, h0:h0 + nh],
                               rel_bias[:, h0:h0 + nh], window, dil)
        outs.append(o)
        lses.append(jnp.mean(lse, -1))
        h0 += nh
    n_g = len(ATTN_GROUPS)
    wts = jax.nn.softmax(jnp.stack(lses, -1), -1) * n_g
    o = jnp.concatenate([outs[g] * wts[..., g, None, None] for g in range(n_g)], axis=2)
    return o.reshape(bsz, s, ATTN_WIDTH).astype(h.dtype) @ o_w


def hier_moe(h, group_w, group_b, expert_w, expert_b, gate_w, up_w, down_w):
    bsz, s, d = h.shape
    t = h.reshape(-1, d)
    g_logits = (t @ group_w).astype(jnp.float32) + group_b
    g_prob = jax.nn.softmax(g_logits, -1)
    g_idx = jnp.argmax(g_logits, -1)
    g_val = jnp.take_along_axis(g_prob, g_idx[:, None], -1)
    e_logits = jnp.einsum('nd,gde->nge', t, expert_w).astype(jnp.float32) + expert_b
    e_sel = jnp.take_along_axis(e_logits, g_idx[:, None, None], axis=1)[:, 0]
    top_v, top_i = lax.top_k(e_sel, MOE_TOP_K)
    top_p = jax.nn.softmax(top_v, -1) * g_val
    e_gate = jnp.sum(jax.nn.one_hot(top_i, MOE_EXPERTS_PER_GROUP, dtype=jnp.float32) * top_p[..., None], axis=1)
    gates = jax.nn.one_hot(g_idx, MOE_GROUPS, dtype=jnp.float32)[:, :, None] * e_gate[:, None, :]
    out = jnp.zeros((t.shape[0], d), jnp.float32)
    for g in range(MOE_GROUPS):
        hid = jax.nn.silu(jnp.einsum('nd,edf->nef', t, gate_w[g])) * jnp.einsum('nd,edf->nef', t, up_w[g])
        out = out + jnp.einsum('nef,efd->nd', hid * gates[:, g, :, None].astype(hid.dtype), down_w[g])
    return out.reshape(bsz, s, d).astype(h.dtype)


def setup_inputs(seed: int = 0) -> dict:
    key = jax.random.key(seed)
    ks = jax.random.split(key, 24)

    def nrm(k, shape, scale):
        return jax.random.normal(k, shape, jnp.float32) * scale

    na, nbl, L = N_A_LAYERS, N_B_LAYERS, DEPTH
    G, E, F = MOE_GROUPS, MOE_EXPERTS_PER_GROUP, MOE_D_FF
    x = nrm(ks[0], (BATCH, SEQ, D_MODEL), 1.0)
    ssm_in_w = nrm(ks[1], (na, D_MODEL, SSM_IN_DIM), D_MODEL ** -0.5)
    ssm_conv_w = nrm(ks[2], (na, SSM_CONV, SSM_CONV_DIM), SSM_CONV ** -0.5)
    ssm_conv_b = nrm(ks[3], (na, SSM_CONV_DIM), 0.02)
    dt0 = jnp.exp(jax.random.uniform(ks[4], (na, SSM_HEADS), jnp.float32, math.log(1e-3), math.log(1e-1)))
    ssm_dt_bias = dt0 + jnp.log(-jnp.expm1(-dt0))
    ssm_a_log = jnp.log(jax.random.uniform(ks[5], (na, SSM_HEADS), jnp.float32, 1.0, 16.0))
    ssm_d = 1.0 + nrm(ks[6], (na, SSM_HEADS), 0.1)
    ssm_norm_w = 1.0 + nrm(ks[7], (na, SSM_D_INNER), 0.05)
    ssm_out_w = nrm(ks[8], (na, SSM_D_INNER, D_MODEL), SSM_D_INNER ** -0.5 * DN_BETA)
    kv_w = jnp.concatenate([nrm(ks[9], (D_MODEL, ATTN_WIDTH), D_MODEL ** -0.5),
                            nrm(ks[10], (D_MODEL, ATTN_WIDTH), D_MODEL ** -0.5 * DN_BETA)], axis=1)
    attn_q_w = nrm(ks[11], (nbl, D_MODEL, ATTN_WIDTH), D_MODEL ** -0.5)
    attn_o_w = nrm(ks[12], (nbl, ATTN_WIDTH, D_MODEL), ATTN_WIDTH ** -0.5 * DN_BETA)
    rel_bias = nrm(ks[13], (N_BUCKETS, ATTN_HEADS), 0.5)
    moe_group_w = nrm(ks[14], (L, D_MODEL, G), D_MODEL ** -0.5)
    moe_group_b = nrm(ks[15], (L, G), 0.01)
    moe_expert_w = nrm(ks[16], (L, G, D_MODEL, E), D_MODEL ** -0.5)
    moe_expert_b = nrm(ks[17], (L, G, E), 0.01)
    moe_gate_w = nrm(ks[18], (L, G, E, D_MODEL, F), D_MODEL ** -0.5)
    moe_up_w = nrm(ks[19], (L, G, E, D_MODEL, F), D_MODEL ** -0.5 * DN_BETA)
    moe_down_w = nrm(ks[20], (L, G, E, F, D_MODEL), F ** -0.5 * DN_BETA)
    ln_g = 1.0 + nrm(ks[21], (L, 2, D_MODEL), 0.05)
    ln_b = nrm(ks[22], (L, 2, D_MODEL), 0.02)
    return {'x': x, 'ssm_in_w': ssm_in_w, 'ssm_conv_w': ssm_conv_w, 'ssm_conv_b': ssm_conv_b,
            'ssm_dt_bias': ssm_dt_bias, 'ssm_a_log': ssm_a_log, 'ssm_d': ssm_d, 'ssm_norm_w': ssm_norm_w,
            'ssm_out_w': ssm_out_w, 'kv_w': kv_w, 'attn_q_w': attn_q_w, 'attn_o_w': attn_o_w,
            'rel_bias': rel_bias, 'moe_group_w': moe_group_w, 'moe_group_b': moe_group_b,
            'moe_expert_w': moe_expert_w, 'moe_expert_b': moe_expert_b, 'moe_gate_w': moe_gate_w,
            'moe_up_w': moe_up_w, 'moe_down_w': moe_down_w, 'ln_g': ln_g, 'ln_b': ln_b}


def reference(x, ssm_in_w, ssm_conv_w, ssm_conv_b, ssm_dt_bias, ssm_a_log, ssm_d, ssm_norm_w,
              ssm_out_w, kv_w, attn_q_w, attn_o_w, rel_bias, moe_group_w, moe_group_b,
              moe_expert_w, moe_expert_b, moe_gate_w, moe_up_w, moe_down_w, ln_g, ln_b):
    bsz, s, _ = x.shape
    h = x
    k_sh = v_sh = None
    for i in range(DEPTH):
        if i < N_A_LAYERS:
            mix = mamba2_mixer(h, ssm_in_w[i], ssm_conv_w[i], ssm_conv_b[i], ssm_dt_bias[i],
                               ssm_a_log[i], ssm_d[i], ssm_norm_w[i], ssm_out_w[i])
        else:
            if i == N_A_LAYERS:
                kv = (h @ kv_w).reshape(bsz, s, 2, ATTN_HEADS, ATTN_HEAD_DIM)
                k_sh, v_sh = kv[:, :, 0], kv[:, :, 1]
            j = i - N_A_LAYERS
            mix = dilated_attention(h, attn_q_w[j], k_sh, v_sh, attn_o_w[j], rel_bias)
        h = layer_norm(DN_ALPHA * h + mix, ln_g[i, 0], ln_b[i, 0])
        ffn = hier_moe(h, moe_group_w[i], moe_group_b[i], moe_expert_w[i], moe_expert_b[i],
                       moe_gate_w[i], moe_up_w[i], moe_down_w[i])
        h = layer_norm(DN_ALPHA * h + ffn, ln_g[i, 1], ln_b[i, 1])
    return h
```

```python
import functools
import math

import numpy as np
import jax
import jax.numpy as jnp
from jax import lax
from jax.experimental import pallas as pl
from jax.experimental.pallas import tpu as pltpu

F32 = jnp.float32
BF16 = jnp.bfloat16
I32 = jnp.int32

D_MODEL = 1024
DEPTH = 2
DN_ALPHA = (2 * DEPTH) ** 0.25
LN_EPS = 1e-5

SSM_D_INNER = 2048
SSM_HEAD_DIM = 64
SSM_HEADS = 32
SSM_GROUPS = 4
SSM_STATE = 128
SSM_CONV = 4
SSM_CHUNK = 128
SSM_CONV_DIM = SSM_D_INNER + 2 * SSM_GROUPS * SSM_STATE

ATTN_HEAD_DIM = 64
ATTN_GROUPS = ((128, 1, 6), (512, 4, 5), (2048, 16, 5))
ATTN_HEADS = 16
ATTN_BLOCK = 128
N_BUCKETS = 32
MAX_DISTANCE = 2048
ATTN_SLAB = 384
NEG_BIG = -1e30

MOE_GROUPS = 4
MOE_EPG = 8
MOE_EXPERTS = MOE_GROUPS * MOE_EPG
MOE_D_FF = 512
MOE_ROW_TILE = 256

LANES = 128
VMEM_LIMIT = 48 * 1024 * 1024


def _cparams(*sem):
    return pltpu.CompilerParams(dimension_semantics=sem, vmem_limit_bytes=VMEM_LIMIT)


def _layer_norm(x, g, b):
    mu = jnp.mean(x, -1, keepdims=True)
    xc = x - mu
    var = jnp.mean(xc * xc, -1, keepdims=True)
    return xc * lax.rsqrt(var + LN_EPS) * g + b


def _split2(x):
    hi = x.astype(BF16)
    lo = (x - hi.astype(F32)).astype(BF16)
    return hi, lo


def _dot(a, b):
    return jnp.dot(a, b, preferred_element_type=F32)


def _dot_f32ish(a, b):
    ah, al = _split2(a)
    bh, bl = _split2(b)
    return _dot(ah, bh) + _dot(al, bh) + _dot(ah, bl)


def _mm_kernel(a_ref, b_ref, o_ref):
    o_ref[...] = _dot(a_ref[...].astype(BF16), b_ref[...]).astype(o_ref.dtype)


def _matmul(a, b, out_dtype, tm, tn):
    m, k = a.shape
    nc = b.shape[1]
    return pl.pallas_call(
        _mm_kernel,
        grid=(m // tm, nc // tn),
        in_specs=[pl.BlockSpec((tm, k), lambda i, j: (i, 0)),
                  pl.BlockSpec((k, tn), lambda i, j: (0, j))],
        out_specs=pl.BlockSpec((tm, tn), lambda i, j: (i, j)),
        out_shape=jax.ShapeDtypeStruct((m, nc), out_dtype),
        compiler_params=_cparams("parallel", "parallel"),
        name="matmul",
    )(a, b)


def _mm3_kernel(a_ref, b_ref, o_ref):
    o_ref[...] = _dot_f32ish(a_ref[...], b_ref[...])


def _matmul_f32ish(a, b, tm):
    m, k = a.shape
    nc = b.shape[1]
    return pl.pallas_call(
        _mm3_kernel,
        grid=(m // tm,),
        in_specs=[pl.BlockSpec((tm, k), lambda i: (i, 0)),
                  pl.BlockSpec((k, nc), lambda i: (0, 0))],
        out_specs=pl.BlockSpec((tm, nc), lambda i: (i, 0)),
        out_shape=jax.ShapeDtypeStruct((m, nc), F32),
        compiler_params=_cparams("parallel"),
        name="matmul_f32ish",
    )(a, b)


def _mm_ln_kernel(a_ref, w_ref, h_ref, g_ref, b_ref, o_ref):
    acc = _dot(a_ref[...], w_ref[...])
    o_ref[...] = _layer_norm(DN_ALPHA * h_ref[...] + acc, g_ref[...], b_ref[...])


def _matmul_res_ln(a, w, h, g, b, tm):
    m, k = a.shape
    d = w.shape[1]
    return pl.pallas_call(
        _mm_ln_kernel,
        grid=(m // tm,),
        in_specs=[pl.BlockSpec((tm, k), lambda i: (i, 0)),
                  pl.BlockSpec((k, d), lambda i: (0, 0)),
                  pl.BlockSpec((tm, d), lambda i: (i, 0)),
                  pl.BlockSpec((1, d), lambda i: (0, 0)),
                  pl.BlockSpec((1, d), lambda i: (0, 0))],
        out_specs=pl.BlockSpec((tm, d), lambda i: (i, 0)),
        out_shape=jax.ShapeDtypeStruct((m, d), F32),
        compiler_params=_cparams("parallel"),
        name="matmul_res_ln",
    )(a, w, h, g, b)


def _ssd_kernel(z_ref, xbc_ref, dt_ref, cw_ref, cb_ref, dtb_ref, alog_ref, dsk_ref, nw_ref,
                y_ref, xe_ref, st_ref):
    q = SSM_CHUNK

    @pl.when(pl.program_id(1) == 0)
    def _():
        xe_ref[0:8, :] = jnp.zeros((8, SSM_CONV_DIM), F32)
        st_ref[...] = jnp.zeros_like(st_ref)

    u = xbc_ref[...]
    xe_ref[8:8 + q, :] = u
    w = cw_ref[...]
    conv = (cb_ref[...] + w[3:4] * u + w[2:3] * xe_ref[7:7 + q, :]
            + w[1:2] * xe_ref[6:6 + q, :] + w[0:1] * xe_ref[5:5 + q, :])
    xe_ref[0:8, :] = xe_ref[q:q + 8, :]
    act = conv * jax.nn.sigmoid(conv)

    pre = dt_ref[...] + dtb_ref[...]
    dt = jnp.maximum(pre, 0.0) + jnp.log(1.0 + jnp.exp(-jnp.abs(pre)))
    adt = dt * (-jnp.exp(alog_ref[...]))

    row = lax.broadcasted_iota(I32, (q, q), 0)
    col = lax.broadcasted_iota(I32, (q, q), 1)
    tril = row >= col
    tri_b = jnp.where(tril, 1.0, 0.0).astype(BF16)
    a_hi = adt.astype(BF16)
    r1 = adt - a_hi.astype(F32)
    a_mid = r1.astype(BF16)
    a_lo = (r1 - a_mid.astype(F32)).astype(BF16)
    acs = _dot(tri_b, a_hi) + _dot(tri_b, a_mid) + _dot(tri_b, a_lo)
    acs_t = acs.T
    eacs = jnp.exp(acs)
    left = col < SSM_HEAD_DIM

    for g in range(SSM_GROUPS):
        b0 = SSM_D_INNER + g * SSM_STATE
        c0 = SSM_D_INNER + SSM_GROUPS * SSM_STATE + g * SSM_STATE
        bm = act[:, b0:b0 + SSM_STATE]
        cm = act[:, c0:c0 + SSM_STATE].astype(BF16)
        cb = lax.dot_general(cm, bm.astype(BF16), (((1,), (1,)), ((), ())),
                             preferred_element_type=F32)
        bm_t = bm.T
        gs = g * 512
        y_off = _dot(cm, st_ref[:, gs:gs + 512].astype(BF16))
        slabs = []
        for pr in range(4):
            ha = g * 8 + pr * 2
            hb = ha + 1
            cs = gs + pr * LANES
            x2 = act[:, cs:cs + LANES]
            dt2 = jnp.where(left, dt[:, ha:ha + 1], dt[:, hb:hb + 1])
            xdt = (x2 * dt2).astype(BF16)
            ys, ups = [], []
            for h in (ha, hb):
                a_col = acs[:, h:h + 1]
                a_row = acs_t[h:h + 1, :]
                decay = jnp.where(tril, jnp.exp(a_col - a_row), 0.0)
                ys.append(_dot((cb * decay).astype(BF16), xdt))
                to_end = jnp.exp(a_row[:, q - 1:q] - a_row)
                ups.append(_dot((bm_t * to_end).astype(BF16), xdt))
            y_diag = jnp.where(left, ys[0], ys[1])
            upd = jnp.where(left, ups[0], ups[1])
            e2 = jnp.where(left, eacs[:, ha:ha + 1], eacs[:, hb:hb + 1])
            cd = jnp.where(left[0:1, :], eacs[q - 1:q, ha:ha + 1], eacs[q - 1:q, hb:hb + 1])
            y2 = y_diag + y_off[:, pr * LANES:(pr + 1) * LANES] * e2 + dsk_ref[:, cs:cs + LANES] * x2
            st_ref[:, cs:cs + LANES] = st_ref[:, cs:cs + LANES] * cd + upd
            slabs.append(y2)
        yg = jnp.concatenate(slabs, axis=1)
        zg = z_ref[:, gs:gs + 512]
        yg = yg * (zg * jax.nn.sigmoid(zg))
        ms = jnp.mean(yg * yg, -1, keepdims=True)
        y_ref[:, gs:gs + 512] = (yg * lax.rsqrt(ms + LN_EPS) * nw_ref[:, gs:gs + 512]).astype(y_ref.dtype)


def _ssd(z, xbc, dt_raw, conv_w, conv_b, dt_bias, a_log, d_rep, norm_w, bsz, seq):
    n = z.shape[0]
    nchunk = seq // SSM_CHUNK
    q = SSM_CHUNK
    tok = lambda b, c: (b * nchunk + c, 0)
    fixed = lambda b, c: (0, 0)
    return pl.pallas_call(
        _ssd_kernel,
        grid=(bsz, nchunk),
        in_specs=[pl.BlockSpec((q, SSM_D_INNER), tok),
                  pl.BlockSpec((q, SSM_CONV_DIM), tok),
                  pl.BlockSpec((q, LANES), tok),
                  pl.BlockSpec((SSM_CONV, SSM_CONV_DIM), fixed),
                  pl.BlockSpec((1, SSM_CONV_DIM), fixed),
                  pl.BlockSpec((1, LANES), fixed),
                  pl.BlockSpec((1, LANES), fixed),
                  pl.BlockSpec((1, SSM_D_INNER), fixed),
                  pl.BlockSpec((1, SSM_D_INNER), fixed)],
        out_specs=pl.BlockSpec((q, SSM_D_INNER), tok),
        out_shape=jax.ShapeDtypeStruct((n, SSM_D_INNER), BF16),
        scratch_shapes=[pltpu.VMEM((q + 8, SSM_CONV_DIM), F32),
                        pltpu.VMEM((SSM_STATE, SSM_D_INNER), F32)],
        compiler_params=_cparams("parallel", "arbitrary"),
        name="ssd_chunk",
    )(z, xbc, dt_raw, conv_w, conv_b, dt_bias, a_log, d_rep, norm_w)


def _attn_kernel(q_ref, kp_ref, kc_ref, vp_ref, vc_ref, bp_ref, bc_ref, o_ref, lse_ref, *, nh):
    has_prev = pl.program_id(1) > 0
    qb = q_ref[0]
    kp, kc, vp, vc = kp_ref[0], kc_ref[0], vp_ref[0], vc_ref[0]
    lane = lax.broadcasted_iota(I32, (ATTN_BLOCK, LANES), 1)
    left = lane < ATTN_HEAD_DIM
    zero = jnp.zeros((), BF16)
    nt = (((1,), (1,)), ((), ()))
    lse_sum = jnp.zeros((ATTN_BLOCK, 1), F32)
    outs = []
    for pr in range(ATTN_SLAB // LANES):
        sl = slice(pr * LANES, (pr + 1) * LANES)
        q2, kp2, kc2, vp2, vc2 = qb[:, sl], kp[:, sl], kc[:, sl], vp[:, sl], vc[:, sl]
        halves = []
        for side in range(2):
            hh = pr * 2 + side
            if hh >= nh:
                halves.append(jnp.zeros((ATTN_BLOCK, LANES), F32))
                continue
            qm = jnp.where(left if side == 0 else ~left, q2, zero)
            s_p = lax.dot_general(qm, kp2, nt, preferred_element_type=F32) * (ATTN_HEAD_DIM ** -0.5)
            s_c = lax.dot_general(qm, kc2, nt, preferred_element_type=F32) * (ATTN_HEAD_DIM ** -0.5)
            s_p = jnp.where(has_prev, s_p + bp_ref[hh], NEG_BIG)
            s_c = s_c + bc_ref[hh]
            m = jnp.maximum(jnp.max(s_p, -1, keepdims=True), jnp.max(s_c, -1, keepdims=True))
            p_p = jnp.exp(s_p - m)
            p_c = jnp.exp(s_c - m)
            den = jnp.sum(p_p, -1, keepdims=True) + jnp.sum(p_c, -1, keepdims=True)
            o = _dot(p_p.astype(BF16), vp2) + _dot(p_c.astype(BF16), vc2)
            halves.append(o / den)
            lse_sum = lse_sum + (m + jnp.log(den))
        outs.append(jnp.where(left, halves[0], halves[1]))
    o_ref[0] = jnp.concatenate(outs, axis=1)
    lse_ref[0] = jnp.broadcast_to(lse_sum * (1.0 / nh), (ATTN_BLOCK, LANES))


def _attention_group(qkv, bias_p, bias_c, gi, dil, nh, bsz, seq):
    l = seq // dil
    nb = l // ATTN_BLOCK
    nslab = qkv.shape[1] // ATTN_SLAB
    view = qkv.reshape(bsz, l, dil * qkv.shape[1])
    blk = (1, ATTN_BLOCK, ATTN_SLAB)

    def cur(which):
        return lambda b, n, r: (b, n, r * nslab + which * 3 + gi)

    def prev(which):
        return lambda b, n, r: (b, jnp.maximum(n - 1, 0), r * nslab + which * 3 + gi)

    fixed = lambda b, n, r: (0, 0, 0)
    o, lse = pl.pallas_call(
        functools.partial(_attn_kernel, nh=nh),
        grid=(bsz, nb, dil),
        in_specs=[pl.BlockSpec(blk, cur(0)),
                  pl.BlockSpec(blk, prev(1)), pl.BlockSpec(blk, cur(1)),
                  pl.BlockSpec(blk, prev(2)), pl.BlockSpec(blk, cur(2)),
                  pl.BlockSpec((nh, ATTN_BLOCK, ATTN_BLOCK), fixed),
                  pl.BlockSpec((nh, ATTN_BLOCK, ATTN_BLOCK), fixed)],
        out_specs=[pl.BlockSpec(blk, lambda b, n, r: (b, n, r)),
                   pl.BlockSpec((1, ATTN_BLOCK, LANES), lambda b, n, r: (b, n, r))],
        out_shape=[jax.ShapeDtypeStruct((bsz, l, dil * ATTN_SLAB), F32),
                   jax.ShapeDtypeStruct((bsz, l, dil * LANES), F32)],
        compiler_params=_cparams("parallel", "parallel", "parallel"),
        name=f"dilated_attn_g{gi}",
    )(view, view, view, view, view, bias_p, bias_c)
    return o.reshape(bsz * seq, ATTN_SLAB), lse.reshape(bsz * seq, LANES)


def _attn_out_kernel(o0_ref, o1_ref, o2_ref, l0_ref, l1_ref, l2_ref, w_ref, h_ref, g_ref, b_ref, out_ref):
    ls = [r[:, 0:1] for r in (l0_ref, l1_ref, l2_ref)]
    m = jnp.maximum(jnp.maximum(ls[0], ls[1]), ls[2])
    es = [jnp.exp(v - m) for v in ls]
    scale = len(ATTN_GROUPS) / (es[0] + es[1] + es[2])
    acc = None
    for gi, o_ref in enumerate((o0_ref, o1_ref, o2_ref)):
        part = _dot((o_ref[...] * (es[gi] * scale)).astype(BF16), w_ref[gi])
        acc = part if acc is None else acc + part
    out_ref[...] = _layer_norm(DN_ALPHA * h_ref[...] + acc, g_ref[...], b_ref[...])


def _attn_out(outs, lses, w, h, g, b, tm):
    n = h.shape[0]
    tok = lambda i: (i, 0)
    fixed2 = lambda i: (0, 0)
    return pl.pallas_call(
        _attn_out_kernel,
        grid=(n // tm,),
        in_specs=[pl.BlockSpec((tm, ATTN_SLAB), tok)] * 3 + [pl.BlockSpec((tm, LANES), tok)] * 3
                 + [pl.BlockSpec((3, ATTN_SLAB, D_MODEL), lambda i: (0, 0, 0)),
                    pl.BlockSpec((tm, D_MODEL), tok),
                    pl.BlockSpec((1, D_MODEL), fixed2), pl.BlockSpec((1, D_MODEL), fixed2)],
        out_specs=pl.BlockSpec((tm, D_MODEL), tok),
        out_shape=jax.ShapeDtypeStruct((n, D_MODEL), F32),
        compiler_params=_cparams("parallel"),
        name="attn_out_ln",
    )(*outs, *lses, w, h, g, b)


def _t5_bucket(dist):
    max_exact = N_BUCKETS // 2
    n = np.maximum(dist, 1).astype(np.float64)
    large = max_exact + (np.log(n / max_exact) / np.log(MAX_DISTANCE / max_exact)
                         * (N_BUCKETS - max_exact)).astype(np.int32)
    large = np.minimum(large, N_BUCKETS - 1)
    return np.where(dist < max_exact, dist, large).astype(np.int32)


def _group_bias(rel_bias, h0, nh, dil):
    qi = np.arange(ATTN_BLOCK)[:, None]
    ki = np.arange(ATTN_BLOCK)[None, :]
    tabs = []
    for delta, band in ((qi + ATTN_BLOCK - ki, ki >= qi), (qi - ki, ki <= qi)):
        bucket = _t5_bucket(np.clip(delta, 0, None) * dil)
        t = jnp.transpose(rel_bias[:, h0:h0 + nh][bucket], (2, 0, 1))
        tabs.append(jnp.where(band[None], t, NEG_BIG).astype(F32))
    return tabs


def _router_kernel(h_ref, w_ref, b_ref, ints_ref, flts_ref, cnt_ref, carry_ref, *, tm):
    @pl.when(pl.program_id(0) == 0)
    def _():
        carry_ref[...] = jnp.zeros_like(carry_ref)

    lt = _dot_f32ish(h_ref[...], w_ref[...]).T + b_ref[...]
    gl = lt[0:MOE_GROUPS]
    r4 = lax.broadcasted_iota(I32, (MOE_GROUPS, tm), 0)
    gmax = jnp.max(gl, 0, keepdims=True)
    gidx = jnp.min(jnp.where(gl == gmax, r4, MOE_GROUPS), 0, keepdims=True)
    gval = 1.0 / jnp.sum(jnp.exp(gl - gmax), 0, keepdims=True)

    esel = jnp.zeros((MOE_EPG, tm), F32)
    for g in range(MOE_GROUPS):
        esel = jnp.where(gidx == g, lt[8 + g * MOE_EPG:8 + (g + 1) * MOE_EPG], esel)
    r8 = lax.broadcasted_iota(I32, (MOE_EPG, tm), 0)
    v1 = jnp.max(esel, 0, keepdims=True)
    i1 = jnp.min(jnp.where(esel == v1, r8, MOE_EPG), 0, keepdims=True)
    rest = jnp.where(r8 == i1, -jnp.inf, esel)
    v2 = jnp.max(rest, 0, keepdims=True)
    i2 = jnp.min(jnp.where(rest == v2, r8, MOE_EPG), 0, keepdims=True)
    t = jnp.exp(v2 - v1)
    p1 = gval / (1.0 + t)
    p2 = p1 * t
    e1 = gidx * MOE_EPG + i1
    e2 = gidx * MOE_EPG + i2

    r32 = lax.broadcasted_iota(I32, (MOE_EXPERTS, tm), 0)
    oh1 = r32 == e1
    oh2 = r32 == e2
    oh = jnp.where(oh1 | oh2, 1.0, 0.0)
    tr = lax.broadcasted_iota(I32, (tm, tm), 0)
    tc = lax.broadcasted_iota(I32, (tm, tm), 1)
    before = jnp.where(tr < tc, 1.0, 0.0).astype(BF16)
    base = _dot(oh.astype(BF16), before) + carry_ref[:, 0:1]
    rank1 = jnp.sum(jnp.where(oh1, base, 0.0), 0, keepdims=True).astype(I32)
    rank2 = jnp.sum(jnp.where(oh2, base, 0.0), 0, keepdims=True).astype(I32)
    carry_ref[...] = carry_ref[...] + jnp.sum(oh, 1, keepdims=True)
    cnt_ref[...] = carry_ref[...]

    ints_ref[...] = jnp.where(r8 == 0, e1, jnp.where(r8 == 1, e2, jnp.where(r8 == 2, rank1,
                              jnp.where(r8 == 3, rank2, 0))))
    flts_ref[...] = jnp.where(r8 == 0, p1, jnp.where(r8 == 1, p2, 0.0))


def _router(h, w_r, b_r, tm):
    n = h.shape[0]
    return pl.pallas_call(
        functools.partial(_router_kernel, tm=tm),
        grid=(n // tm,),
        in_specs=[pl.BlockSpec((tm, D_MODEL), lambda i: (i, 0)),
                  pl.BlockSpec((D_MODEL, LANES), lambda i: (0, 0)),
                  pl.BlockSpec((LANES, 1), lambda i: (0, 0))],
        out_specs=[pl.BlockSpec((8, tm), lambda i: (0, i)),
                   pl.BlockSpec((8, tm), lambda i: (0, i)),
                   pl.BlockSpec((MOE_EXPERTS, LANES), lambda i: (0, 0))],
        out_shape=[jax.ShapeDtypeStruct((8, n), I32),
                   jax.ShapeDtypeStruct((8, n), F32),
                   jax.ShapeDtypeStruct((MOE_EXPERTS, LANES), F32)],
        scratch_shapes=[pltpu.VMEM((MOE_EXPERTS, LANES), F32)],
        compiler_params=_cparams("arbitrary"),
        name="moe_router",
    )(h, w_r, b_r)


def _row_copy(src_ref, s, dst_ref, d, sem):
    return pltpu.make_async_copy(src_ref.at[pl.ds(s, 1), :], dst_ref.at[pl.ds(d, 1), :], sem)


def _experts_kernel(tok_ref, te_ref, na_ref, h_ref, wg_ref, wu_ref, wd_ref, y_ref,
                    x_s, wg_s, wu_s, wd_s, sem):
    i = pl.program_id(0)
    tr = MOE_ROW_TILE
    active = i < na_ref[0]
    changed = jnp.logical_or(i == 0, te_ref[i] != te_ref[jnp.maximum(i - 1, 0)])

    @pl.when(active)
    def _():
        def issue(r, carry):
            _row_copy(h_ref, tok_ref[i * tr + r], x_s, r, sem).start()
            return carry

        lax.fori_loop(0, tr, issue, 0)

    @pl.when(jnp.logical_and(active, changed))
    def _():
        wg_s[...] = wg_ref[0].astype(BF16)
        wu_s[...] = wu_ref[0].astype(BF16)
        wd_s[...] = wd_ref[0].astype(BF16)

    @pl.when(active)
    def _():
        def drain(r, carry):
            _row_copy(h_ref, 0, x_s, 0, sem).wait()
            return carry

        lax.fori_loop(0, tr, drain, 0)
        x = x_s[...].astype(BF16)
        gate = _dot(x, wg_s[...])
        up = _dot(x, wu_s[...])
        hid = (gate * jax.nn.sigmoid(gate) * up).astype(BF16)
        y_ref[...] = _dot(hid, wd_s[...])

    @pl.when(jnp.logical_not(active))
    def _():
        y_ref[...] = jnp.zeros_like(y_ref)


def _experts(h, row_token, tile_expert, n_active, wg, wu, wd):
    rows = row_token.shape[0]
    tr = MOE_ROW_TILE
    wmap = lambda i, tok, te, na: (te[i], 0, 0)
    return pl.pallas_call(
        _experts_kernel,
        grid_spec=pltpu.PrefetchScalarGridSpec(
            num_scalar_prefetch=3,
            grid=(rows // tr,),
            in_specs=[pl.BlockSpec(memory_space=pl.ANY),
                      pl.BlockSpec((1, D_MODEL, MOE_D_FF), wmap),
                      pl.BlockSpec((1, D_MODEL, MOE_D_FF), wmap),
                      pl.BlockSpec((1, MOE_D_FF, D_MODEL), wmap)],
            out_specs=pl.BlockSpec((tr, D_MODEL), lambda i, tok, te, na: (i, 0)),
            scratch_shapes=[pltpu.VMEM((tr, D_MODEL), F32),
                            pltpu.VMEM((D_MODEL, MOE_D_FF), BF16),
                            pltpu.VMEM((D_MODEL, MOE_D_FF), BF16),
                            pltpu.VMEM((MOE_D_FF, D_MODEL), BF16),
                            pltpu.SemaphoreType.DMA(())]),
        out_shape=jax.ShapeDtypeStruct((rows, D_MODEL), F32),
        compiler_params=_cparams("arbitrary"),
        name="moe_experts",
    )(row_token, tile_expert, n_active, h, wg, wu, wd)


def _combine_kernel(pos1_ref, pos2_ref, y_ref, h_ref, p1_ref, p2_ref, g_ref, b_ref, out_ref,
                    buf1, buf2, sem, *, tm):
    base = pl.program_id(0) * tm

    def issue(t, carry):
        _row_copy(y_ref, pos1_ref[base + t], buf1, t, sem).start()
        _row_copy(y_ref, pos2_ref[base + t], buf2, t, sem).start()
        return carry

    lax.fori_loop(0, tm, issue, 0)

    def drain(t, carry):
        _row_copy(y_ref, 0, buf1, 0, sem).wait()
        _row_copy(y_ref, 0, buf2, 0, sem).wait()
        return carry

    lax.fori_loop(0, tm, drain, 0)
    ffn = p1_ref[...] * buf1[...] + p2_ref[...] * buf2[...]
    out_ref[...] = _layer_norm(DN_ALPHA * h_ref[...] + ffn, g_ref[...], b_ref[...])


def _combine(y, h, pos1, pos2, p1, p2, g, b, tm):
    n = h.shape[0]
    tok = lambda i, a, c: (i, 0)
    fixed = lambda i, a, c: (0, 0)
    return pl.pallas_call(
        functools.partial(_combine_kernel, tm=tm),
        grid_spec=pltpu.PrefetchScalarGridSpec(
            num_scalar_prefetch=2,
            grid=(n // tm,),
            in_specs=[pl.BlockSpec(memory_space=pl.ANY),
                      pl.BlockSpec((tm, D_MODEL), tok),
                      pl.BlockSpec((tm, 1), tok), pl.BlockSpec((tm, 1), tok),
                      pl.BlockSpec((1, D_MODEL), fixed), pl.BlockSpec((1, D_MODEL), fixed)],
            out_specs=pl.BlockSpec((tm, D_MODEL), tok),
            scratch_shapes=[pltpu.VMEM((tm, D_MODEL), F32), pltpu.VMEM((tm, D_MODEL), F32),
                            pltpu.SemaphoreType.DMA(())]),
        out_shape=jax.ShapeDtypeStruct((n, D_MODEL), F32),
        compiler_params=_cparams("arbitrary"),
        name="moe_combine_ln",
    )(pos1, pos2, y, h, p1, p2, g, b)


def _moe_layer(h, group_w, group_b, expert_w, expert_b, gate_w, up_w, down_w, ln_g, ln_b):
    n = h.shape[0]
    ew = jnp.transpose(expert_w, (1, 0, 2)).reshape(D_MODEL, MOE_EXPERTS)
    w_r = jnp.zeros((D_MODEL, LANES), F32).at[:, 0:MOE_GROUPS].set(group_w).at[:, 8:8 + MOE_EXPERTS].set(ew)
    b_r = jnp.zeros((LANES,), F32).at[0:MOE_GROUPS].set(group_b).at[8:8 + MOE_EXPERTS].set(expert_b.reshape(-1))
    ints, flts, cnt = _router(h, w_r, b_r.reshape(LANES, 1), 512)
    e1, e2, rank1, rank2 = ints[0], ints[1], ints[2], ints[3]

    tr = MOE_ROW_TILE
    n_tiles = (2 * n) // tr + MOE_EXPERTS
    counts = cnt[:, 0].astype(I32)
    padded = ((counts + tr - 1) // tr) * tr
    ends = jnp.cumsum(padded)
    starts = ends - padded
    pos1 = starts[e1] + rank1
    pos2 = starts[e2] + rank2
    n_active = ends[-1] // tr
    tile_ids = jnp.arange(n_tiles, dtype=I32)
    tile_expert = jnp.searchsorted(ends, tile_ids * tr, side="right").astype(I32)
    tile_expert = jnp.minimum(tile_expert, MOE_EXPERTS - 1)
    tile_expert = jnp.where(tile_ids < n_active, tile_expert, tile_expert[jnp.maximum(n_active - 1, 0)])

    tok_ids = jnp.arange(n, dtype=I32)
    row_token = jnp.zeros((n_tiles * tr,), I32).at[pos1].set(tok_ids).at[pos2].set(tok_ids)
    y = _experts(h, row_token, tile_expert, n_active.reshape(1).astype(I32),
                 gate_w.reshape(MOE_EXPERTS, D_MODEL, MOE_D_FF),
                 up_w.reshape(MOE_EXPERTS, D_MODEL, MOE_D_FF),
                 down_w.reshape(MOE_EXPERTS, MOE_D_FF, D_MODEL))
    return _combine(y, h, pos1, pos2, flts[0].reshape(n, 1), flts[1].reshape(n, 1),
                    ln_g.reshape(1, -1), ln_b.reshape(1, -1), 256)


def _pad_heads(w, axis):
    parts = []
    h0 = 0
    for _, _, nh in ATTN_GROUPS:
        sl = [slice(None)] * w.ndim
        sl[axis] = slice(h0 * ATTN_HEAD_DIM, (h0 + nh) * ATTN_HEAD_DIM)
        part = w[tuple(sl)]
        pad = [(0, 0)] * w.ndim
        pad[axis] = (0, ATTN_SLAB - nh * ATTN_HEAD_DIM)
        parts.append(jnp.pad(part, pad))
        h0 += nh
    return parts


def _ssd_layer(h, in_w, conv_w, conv_b, dt_bias, a_log, d_skip, norm_w, out_w, ln_g, ln_b, bsz, seq):
    in_b = in_w.astype(BF16)
    z = _matmul(h, in_b[:, :SSM_D_INNER], F32, 512, 512)
    xbc = _matmul(h, in_b[:, SSM_D_INNER:SSM_D_INNER + SSM_CONV_DIM], F32, 512, 512)
    dt_w = jnp.pad(in_w[:, SSM_D_INNER + SSM_CONV_DIM:], ((0, 0), (0, LANES - SSM_HEADS)))
    dt_raw = _matmul_f32ish(h, dt_w, 512)
    pad32 = lambda v: jnp.pad(v, (0, LANES - SSM_HEADS)).reshape(1, LANES)
    y = _ssd(z, xbc, dt_raw, conv_w, conv_b.reshape(1, -1), pad32(dt_bias), pad32(a_log),
             jnp.repeat(d_skip, SSM_HEAD_DIM).reshape(1, -1), norm_w.reshape(1, -1), bsz, seq)
    return _matmul_res_ln(y, out_w.astype(BF16), h, ln_g.reshape(1, -1), ln_b.reshape(1, -1), 512)


def _attn_layer(h, kv_w, q_w, o_w, rel_bias, ln_g, ln_b, bsz, seq):
    width = ATTN_HEADS * ATTN_HEAD_DIM
    w_all = jnp.concatenate(_pad_heads(q_w, 1) + _pad_heads(kv_w[:, :width], 1)
                            + _pad_heads(kv_w[:, width:], 1), axis=1).astype(BF16)
    qkv = _matmul(h, w_all, BF16, 512, 3 * ATTN_SLAB)
    outs, lses = [], []
    h0 = 0
    for gi, (_, dil, nh) in enumerate(ATTN_GROUPS):
        bias_p, bias_c = _group_bias(rel_bias, h0, nh, dil)
        o, lse = _attention_group(qkv, bias_p, bias_c, gi, dil, nh, bsz, seq)
        outs.append(o)
        lses.append(lse)
        h0 += nh
    w_o = jnp.stack(_pad_heads(o_w, 0)).astype(BF16)
    return _attn_out(outs, lses, w_o, h, ln_g.reshape(1, -1), ln_b.reshape(1, -1), 512)


def kernel(x, ssm_in_w, ssm_conv_w, ssm_conv_b, ssm_dt_bias, ssm_a_log, ssm_d, ssm_norm_w, ssm_out_w,
           kv_w, attn_q_w, attn_o_w, rel_bias, moe_group_w, moe_group_b, moe_expert_w, moe_expert_b,
           moe_gate_w, moe_up_w, moe_down_w, ln_g, ln_b):
    bsz, seq, d = x.shape
    h = x.reshape(bsz * seq, d)
    n_ssd = DEPTH // 2
    for i in range(DEPTH):
        if i < n_ssd:
            h = _ssd_layer(h, ssm_in_w[i], ssm_conv_w[i], ssm_conv_b[i], ssm_dt_bias[i], ssm_a_log[i],
                           ssm_d[i], ssm_norm_w[i], ssm_out_w[i], ln_g[i, 0], ln_b[i, 0], bsz, seq)
        else:
            j = i - n_ssd
            h = _attn_layer(h, kv_w, attn_q_w[j], attn_o_w[j], rel_bias, ln_g[i, 0], ln_b[i, 0], bsz, seq)
        h = _moe_layer(h, moe_group_w[i], moe_group_b[i], moe_expert_w[i], moe_expert_b[i],
                       moe_gate_w[i], moe_up_w[i], moe_down_w[i], ln_g[i, 1], ln_b[i, 1])
    return h.reshape(bsz, seq, d)
```

```python
import functools
import math

import numpy as np
import jax
import jax.numpy as jnp
from jax import lax
from jax.experimental import pallas as pl
from jax.experimental.pallas import tpu as pltpu

F32 = jnp.float32
BF16 = jnp.bfloat16
I32 = jnp.int32

D_MODEL = 1024
DEPTH = 2
DN_ALPHA = (2 * DEPTH) ** 0.25
LN_EPS = 1e-5

SSM_D_INNER = 2048
SSM_HEAD_DIM = 64
SSM_HEADS = 32
SSM_GROUPS = 4
SSM_STATE = 128
SSM_CONV = 4
SSM_CHUNK = 128
SSM_CONV_DIM = SSM_D_INNER + 2 * SSM_GROUPS * SSM_STATE

ATTN_HEAD_DIM = 64
ATTN_GROUPS = ((128, 1, 6), (512, 4, 5), (2048, 16, 5))
ATTN_HEADS = 16
ATTN_BLOCK = 128
N_BUCKETS = 32
MAX_DISTANCE = 2048
ATTN_SLAB = 384
NEG_BIG = -1e30

MOE_GROUPS = 4
MOE_EPG = 8
MOE_EXPERTS = MOE_GROUPS * MOE_EPG
MOE_D_FF = 512
MOE_ROW_TILE = 256

LANES = 128
DMA_QUEUES = 2
VMEM_LIMIT = 48 * 1024 * 1024


def _cparams(*sem):
    return pltpu.CompilerParams(dimension_semantics=sem, vmem_limit_bytes=VMEM_LIMIT)


def _layer_norm(x, g, b):
    mu = jnp.mean(x, -1, keepdims=True)
    xc = x - mu
    var = jnp.mean(xc * xc, -1, keepdims=True)
    return xc * lax.rsqrt(var + LN_EPS) * g + b


def _split2(x):
    hi = x.astype(BF16)
    lo = (x - hi.astype(F32)).astype(BF16)
    return hi, lo


def _dot(a, b):
    return jnp.dot(a, b, preferred_element_type=F32)


def _dot_f32ish(a, b):
    ah, al = _split2(a)
    bh, bl = _split2(b)
    return _dot(ah, bh) + _dot(al, bh) + _dot(ah, bl)


def _mm_kernel(a_ref, b_ref, o_ref):
    o_ref[...] = _dot(a_ref[...].astype(BF16), b_ref[...]).astype(o_ref.dtype)


def _matmul(a, b, out_dtype, tm, tn):
    m, k = a.shape
    nc = b.shape[1]
    return pl.pallas_call(
        _mm_kernel,
        grid=(m // tm, nc // tn),
        in_specs=[pl.BlockSpec((tm, k), lambda i, j: (i, 0)),
                  pl.BlockSpec((k, tn), lambda i, j: (0, j))],
        out_specs=pl.BlockSpec((tm, tn), lambda i, j: (i, j)),
        out_shape=jax.ShapeDtypeStruct((m, nc), out_dtype),
        compiler_params=_cparams("parallel", "parallel"),
        name="matmul",
    )(a, b)


def _mm3_kernel(a_ref, b_ref, o_ref):
    o_ref[...] = _dot_f32ish(a_ref[...], b_ref[...])


def _matmul_f32ish(a, b, tm):
    m, k = a.shape
    nc = b.shape[1]
    return pl.pallas_call(
        _mm3_kernel,
        grid=(m // tm,),
        in_specs=[pl.BlockSpec((tm, k), lambda i: (i, 0)),
                  pl.BlockSpec((k, nc), lambda i: (0, 0))],
        out_specs=pl.BlockSpec((tm, nc), lambda i: (i, 0)),
        out_shape=jax.ShapeDtypeStruct((m, nc), F32),
        compiler_params=_cparams("parallel"),
        name="matmul_f32ish",
    )(a, b)


ROW_SPLIT = D_MODEL // LANES


def _store_row_tiled(ref, val):
    rows = val.shape[0]
    for c in range(ROW_SPLIT):
        ref[pl.ds(c, rows, stride=ROW_SPLIT), :] = val[:, c * LANES:(c + 1) * LANES]


def _load_row_tiled(ref):
    rows = ref.shape[0] // ROW_SPLIT
    return jnp.concatenate([ref[pl.ds(c, rows, stride=ROW_SPLIT), :] for c in range(ROW_SPLIT)], axis=1)


def _mm_ln_kernel(a_ref, w_ref, h_ref, g_ref, b_ref, o_ref, o8_ref):
    acc = _dot(a_ref[...], w_ref[...])
    out = _layer_norm(DN_ALPHA * h_ref[...] + acc, g_ref[...], b_ref[...])
    o_ref[...] = out
    _store_row_tiled(o8_ref, out)


def _matmul_res_ln(a, w, h, g, b, tm):
    m, k = a.shape
    d = w.shape[1]
    return pl.pallas_call(
        _mm_ln_kernel,
        grid=(m // tm,),
        in_specs=[pl.BlockSpec((tm, k), lambda i: (i, 0)),
                  pl.BlockSpec((k, d), lambda i: (0, 0)),
                  pl.BlockSpec((tm, d), lambda i: (i, 0)),
                  pl.BlockSpec((1, d), lambda i: (0, 0)),
                  pl.BlockSpec((1, d), lambda i: (0, 0))],
        out_specs=[pl.BlockSpec((tm, d), lambda i: (i, 0)),
                   pl.BlockSpec((tm * ROW_SPLIT, LANES), lambda i: (i, 0))],
        out_shape=[jax.ShapeDtypeStruct((m, d), F32),
                   jax.ShapeDtypeStruct((m * ROW_SPLIT, LANES), F32)],
        compiler_params=_cparams("parallel"),
        name="matmul_res_ln",
    )(a, w, h, g, b)


def _ssd_kernel(z_ref, xbc_ref, dt_ref, cw_ref, cb_ref, dtb_ref, alog_ref, dsk_ref, nw_ref,
                y_ref, xe_ref, st_ref):
    q = SSM_CHUNK

    @pl.when(pl.program_id(1) == 0)
    def _():
        xe_ref[0:8, :] = jnp.zeros((8, SSM_CONV_DIM), F32)
        st_ref[...] = jnp.zeros_like(st_ref)

    u = xbc_ref[...]
    xe_ref[8:8 + q, :] = u
    w = cw_ref[...]
    conv = (cb_ref[...] + w[3:4] * u + w[2:3] * xe_ref[7:7 + q, :]
            + w[1:2] * xe_ref[6:6 + q, :] + w[0:1] * xe_ref[5:5 + q, :])
    xe_ref[0:8, :] = xe_ref[q:q + 8, :]
    act = conv * jax.nn.sigmoid(conv)

    pre = dt_ref[...] + dtb_ref[...]
    dt = jnp.maximum(pre, 0.0) + jnp.log(1.0 + jnp.exp(-jnp.abs(pre)))
    adt = dt * (-jnp.exp(alog_ref[...]))

    row = lax.broadcasted_iota(I32, (q, q), 0)
    col = lax.broadcasted_iota(I32, (q, q), 1)
    tril = row >= col
    tri_b = jnp.where(tril, 1.0, 0.0).astype(BF16)
    a_hi = adt.astype(BF16)
    r1 = adt - a_hi.astype(F32)
    a_mid = r1.astype(BF16)
    a_lo = (r1 - a_mid.astype(F32)).astype(BF16)
    acs = _dot(tri_b, a_hi) + _dot(tri_b, a_mid) + _dot(tri_b, a_lo)
    acs_t = acs.T
    eacs = jnp.exp(acs)
    left = col < SSM_HEAD_DIM

    for g in range(SSM_GROUPS):
        b0 = SSM_D_INNER + g * SSM_STATE
        c0 = SSM_D_INNER + SSM_GROUPS * SSM_STATE + g * SSM_STATE
        bm = act[:, b0:b0 + SSM_STATE]
        cm = act[:, c0:c0 + SSM_STATE].astype(BF16)
        cb = lax.dot_general(cm, bm.astype(BF16), (((1,), (1,)), ((), ())),
                             preferred_element_type=F32)
        bm_t = bm.T
        gs = g * 512
        y_off = _dot(cm, st_ref[:, gs:gs + 512].astype(BF16))
        slabs = []
        for pr in range(4):
            ha = g * 8 + pr * 2
            hb = ha + 1
            cs = gs + pr * LANES
            x2 = act[:, cs:cs + LANES]
            dt2 = jnp.where(left, dt[:, ha:ha + 1], dt[:, hb:hb + 1])
            xdt = (x2 * dt2).astype(BF16)
            ys, ups = [], []
            for h in (ha, hb):
                a_col = acs[:, h:h + 1]
                a_row = acs_t[h:h + 1, :]
                decay = jnp.where(tril, jnp.exp(a_col - a_row), 0.0)
                ys.append(_dot((cb * decay).astype(BF16), xdt))
                to_end = jnp.exp(a_row[:, q - 1:q] - a_row)
                ups.append(_dot((bm_t * to_end).astype(BF16), xdt))
            y_diag = jnp.where(left, ys[0], ys[1])
            upd = jnp.where(left, ups[0], ups[1])
            e2 = jnp.where(left, eacs[:, ha:ha + 1], eacs[:, hb:hb + 1])
            cd = jnp.where(left[0:1, :], eacs[q - 1:q, ha:ha + 1], eacs[q - 1:q, hb:hb + 1])
            y2 = y_diag + y_off[:, pr * LANES:(pr + 1) * LANES] * e2 + dsk_ref[:, cs:cs + LANES] * x2
            st_ref[:, cs:cs + LANES] = st_ref[:, cs:cs + LANES] * cd + upd
            slabs.append(y2)
        yg = jnp.concatenate(slabs, axis=1)
        zg = z_ref[:, gs:gs + 512]
        yg = yg * (zg * jax.nn.sigmoid(zg))
        ms = jnp.mean(yg * yg, -1, keepdims=True)
        y_ref[:, gs:gs + 512] = (yg * lax.rsqrt(ms + LN_EPS) * nw_ref[:, gs:gs + 512]).astype(y_ref.dtype)


def _ssd(z, xbc, dt_raw, conv_w, conv_b, dt_bias, a_log, d_rep, norm_w, bsz, seq):
    n = z.shape[0]
    nchunk = seq // SSM_CHUNK
    q = SSM_CHUNK
    tok = lambda b, c: (b * nchunk + c, 0)
    fixed = lambda b, c: (0, 0)
    return pl.pallas_call(
        _ssd_kernel,
        grid=(bsz, nchunk),
        in_specs=[pl.BlockSpec((q, SSM_D_INNER), tok),
                  pl.BlockSpec((q, SSM_CONV_DIM), tok),
                  pl.BlockSpec((q, LANES), tok),
                  pl.BlockSpec((SSM_CONV, SSM_CONV_DIM), fixed),
                  pl.BlockSpec((1, SSM_CONV_DIM), fixed),
                  pl.BlockSpec((1, LANES), fixed),
                  pl.BlockSpec((1, LANES), fixed),
                  pl.BlockSpec((1, SSM_D_INNER), fixed),
                  pl.BlockSpec((1, SSM_D_INNER), fixed)],
        out_specs=pl.BlockSpec((q, SSM_D_INNER), tok),
        out_shape=jax.ShapeDtypeStruct((n, SSM_D_INNER), BF16),
        scratch_shapes=[pltpu.VMEM((q + 8, SSM_CONV_DIM), F32),
                        pltpu.VMEM((SSM_STATE, SSM_D_INNER), F32)],
        compiler_params=_cparams("parallel", "arbitrary"),
        name="ssd_chunk",
    )(z, xbc, dt_raw, conv_w, conv_b, dt_bias, a_log, d_rep, norm_w)


def _attn_kernel(q_ref, kp_ref, kc_ref, vp_ref, vc_ref, bp_ref, bc_ref, o_ref, lse_ref, *, nh):
    has_prev = pl.program_id(1) > 0
    qb = q_ref[0]
    kp, kc, vp, vc = kp_ref[0], kc_ref[0], vp_ref[0], vc_ref[0]
    lane = lax.broadcasted_iota(I32, (ATTN_BLOCK, LANES), 1)
    left = lane < ATTN_HEAD_DIM
    zero = jnp.zeros((), BF16)
    nt = (((1,), (1,)), ((), ()))
    lse_sum = jnp.zeros((ATTN_BLOCK, 1), F32)
    outs = []
    for pr in range(ATTN_SLAB // LANES):
        sl = slice(pr * LANES, (pr + 1) * LANES)
        q2, kp2, kc2, vp2, vc2 = qb[:, sl], kp[:, sl], kc[:, sl], vp[:, sl], vc[:, sl]
        halves = []
        for side in range(2):
            hh = pr * 2 + side
            if hh >= nh:
                halves.append(jnp.zeros((ATTN_BLOCK, LANES), F32))
                continue
            qm = jnp.where(left if side == 0 else ~left, q2, zero)
            s_p = lax.dot_general(qm, kp2, nt, preferred_element_type=F32) * (ATTN_HEAD_DIM ** -0.5)
            s_c = lax.dot_general(qm, kc2, nt, preferred_element_type=F32) * (ATTN_HEAD_DIM ** -0.5)
            s_p = jnp.where(has_prev, s_p + bp_ref[hh], NEG_BIG)
            s_c = s_c + bc_ref[hh]
            m = jnp.maximum(jnp.max(s_p, -1, keepdims=True), jnp.max(s_c, -1, keepdims=True))
            p_p = jnp.exp(s_p - m)
            p_c = jnp.exp(s_c - m)
            den = jnp.sum(p_p, -1, keepdims=True) + jnp.sum(p_c, -1, keepdims=True)
            o = _dot(p_p.astype(BF16), vp2) + _dot(p_c.astype(BF16), vc2)
            halves.append(o / den)
            lse_sum = lse_sum + (m + jnp.log(den))
        outs.append(jnp.where(left, halves[0], halves[1]))
    o_ref[0] = jnp.concatenate(outs, axis=1)
    lse_ref[0] = jnp.broadcast_to(lse_sum * (1.0 / nh), (ATTN_BLOCK, LANES))


def _attention_group(qkv, bias_p, bias_c, gi, dil, nh, bsz, seq):
    l = seq // dil
    nb = l // ATTN_BLOCK
    nslab = qkv.shape[1] // ATTN_SLAB
    view = qkv.reshape(bsz, l, dil * qkv.shape[1])
    blk = (1, ATTN_BLOCK, ATTN_SLAB)

    def cur(which):
        return lambda b, n, r: (b, n, r * nslab + which * 3 + gi)

    def prev(which):
        return lambda b, n, r: (b, jnp.maximum(n - 1, 0), r * nslab + which * 3 + gi)

    fixed = lambda b, n, r: (0, 0, 0)
    o, lse = pl.pallas_call(
        functools.partial(_attn_kernel, nh=nh),
        grid=(bsz, nb, dil),
        in_specs=[pl.BlockSpec(blk, cur(0)),
                  pl.BlockSpec(blk, prev(1)), pl.BlockSpec(blk, cur(1)),
                  pl.BlockSpec(blk, prev(2)), pl.BlockSpec(blk, cur(2)),
                  pl.BlockSpec((nh, ATTN_BLOCK, ATTN_BLOCK), fixed),
                  pl.BlockSpec((nh, ATTN_BLOCK, ATTN_BLOCK), fixed)],
        out_specs=[pl.BlockSpec(blk, lambda b, n, r: (b, n, r)),
                   pl.BlockSpec((1, ATTN_BLOCK, LANES), lambda b, n, r: (b, n, r))],
        out_shape=[jax.ShapeDtypeStruct((bsz, l, dil * ATTN_SLAB), F32),
                   jax.ShapeDtypeStruct((bsz, l, dil * LANES), F32)],
        compiler_params=_cparams("parallel", "parallel", "parallel"),
        name=f"dilated_attn_g{gi}",
    )(view, view, view, view, view, bias_p, bias_c)
    return o.reshape(bsz * seq, ATTN_SLAB), lse.reshape(bsz * seq, LANES)


def _attn_out_kernel(o0_ref, o1_ref, o2_ref, l0_ref, l1_ref, l2_ref, w_ref, h_ref, g_ref, b_ref,
                     out_ref, out8_ref):
    ls = [r[:, 0:1] for r in (l0_ref, l1_ref, l2_ref)]
    m = jnp.maximum(jnp.maximum(ls[0], ls[1]), ls[2])
    es = [jnp.exp(v - m) for v in ls]
    scale = len(ATTN_GROUPS) / (es[0] + es[1] + es[2])
    acc = None
    for gi, o_ref in enumerate((o0_ref, o1_ref, o2_ref)):
        part = _dot((o_ref[...] * (es[gi] * scale)).astype(BF16), w_ref[gi])
        acc = part if acc is None else acc + part
    out = _layer_norm(DN_ALPHA * h_ref[...] + acc, g_ref[...], b_ref[...])
    out_ref[...] = out
    _store_row_tiled(out8_ref, out)


def _attn_out(outs, lses, w, h, g, b, tm):
    n = h.shape[0]
    tok = lambda i: (i, 0)
    fixed2 = lambda i: (0, 0)
    return pl.pallas_call(
        _attn_out_kernel,
        grid=(n // tm,),
        in_specs=[pl.BlockSpec((tm, ATTN_SLAB), tok)] * 3 + [pl.BlockSpec((tm, LANES), tok)] * 3
                 + [pl.BlockSpec((3, ATTN_SLAB, D_MODEL), lambda i: (0, 0, 0)),
                    pl.BlockSpec((tm, D_MODEL), tok),
                    pl.BlockSpec((1, D_MODEL), fixed2), pl.BlockSpec((1, D_MODEL), fixed2)],
        out_specs=[pl.BlockSpec((tm, D_MODEL), tok), pl.BlockSpec((tm * ROW_SPLIT, LANES), tok)],
        out_shape=[jax.ShapeDtypeStruct((n, D_MODEL), F32),
                   jax.ShapeDtypeStruct((n * ROW_SPLIT, LANES), F32)],
        compiler_params=_cparams("parallel"),
        name="attn_out_ln",
    )(*outs, *lses, w, h, g, b)


def _t5_bucket(dist):
    max_exact = N_BUCKETS // 2
    n = np.maximum(dist, 1).astype(np.float64)
    large = max_exact + (np.log(n / max_exact) / np.log(MAX_DISTANCE / max_exact)
                         * (N_BUCKETS - max_exact)).astype(np.int32)
    large = np.minimum(large, N_BUCKETS - 1)
    return np.where(dist < max_exact, dist, large).astype(np.int32)


def _group_bias(rel_bias, h0, nh, dil):
    qi = np.arange(ATTN_BLOCK)[:, None]
    ki = np.arange(ATTN_BLOCK)[None, :]
    tabs = []
    for delta, band in ((qi + ATTN_BLOCK - ki, ki >= qi), (qi - ki, ki <= qi)):
        bucket = _t5_bucket(np.clip(delta, 0, None) * dil)
        onehot = (bucket[..., None] == np.arange(N_BUCKETS)).astype(np.float32)
        t = jnp.einsum("qkb,bh->hqk", onehot, rel_bias[:, h0:h0 + nh], precision=lax.Precision.HIGHEST)
        tabs.append(jnp.where(band[None], t, NEG_BIG).astype(F32))
    return tabs


def _router_kernel(h_ref, w_ref, b_ref, ints_ref, flts_ref, cnt_ref, carry_ref, *, tm):
    @pl.when(pl.program_id(0) == 0)
    def _():
        carry_ref[...] = jnp.zeros_like(carry_ref)

    lt = _dot_f32ish(h_ref[...], w_ref[...]).T + b_ref[...]
    gl = lt[0:MOE_GROUPS]
    r4 = lax.broadcasted_iota(I32, (MOE_GROUPS, tm), 0)
    gmax = jnp.max(gl, 0, keepdims=True)
    gidx = jnp.min(jnp.where(gl == gmax, r4, MOE_GROUPS), 0, keepdims=True)
    gval = 1.0 / jnp.sum(jnp.exp(gl - gmax), 0, keepdims=True)

    esel = jnp.zeros((MOE_EPG, tm), F32)
    for g in range(MOE_GROUPS):
        esel = jnp.where(gidx == g, lt[8 + g * MOE_EPG:8 + (g + 1) * MOE_EPG], esel)
    r8 = lax.broadcasted_iota(I32, (MOE_EPG, tm), 0)
    v1 = jnp.max(esel, 0, keepdims=True)
    i1 = jnp.min(jnp.where(esel == v1, r8, MOE_EPG), 0, keepdims=True)
    rest = jnp.where(r8 == i1, -jnp.inf, esel)
    v2 = jnp.max(rest, 0, keepdims=True)
    i2 = jnp.min(jnp.where(rest == v2, r8, MOE_EPG), 0, keepdims=True)
    t = jnp.exp(v2 - v1)
    p1 = gval / (1.0 + t)
    p2 = p1 * t
    e1 = gidx * MOE_EPG + i1
    e2 = gidx * MOE_EPG + i2

    r32 = lax.broadcasted_iota(I32, (MOE_EXPERTS, tm), 0)
    oh1 = r32 == e1
    oh2 = r32 == e2
    oh = jnp.where(oh1 | oh2, 1.0, 0.0)
    tr = lax.broadcasted_iota(I32, (tm, tm), 0)
    tc = lax.broadcasted_iota(I32, (tm, tm), 1)
    before = jnp.where(tr < tc, 1.0, 0.0).astype(BF16)
    base = _dot(oh.astype(BF16), before) + carry_ref[:, 0:1]
    rank1 = jnp.sum(jnp.where(oh1, base, 0.0), 0, keepdims=True).astype(I32)
    rank2 = jnp.sum(jnp.where(oh2, base, 0.0), 0, keepdims=True).astype(I32)
    carry_ref[...] = carry_ref[...] + jnp.sum(oh, 1, keepdims=True)
    cnt_ref[...] = carry_ref[...]

    ints_ref[...] = jnp.where(r8 == 0, e1, jnp.where(r8 == 1, e2, jnp.where(r8 == 2, rank1,
                              jnp.where(r8 == 3, rank2, 0))))
    flts_ref[...] = jnp.where(r8 == 0, p1, jnp.where(r8 == 1, p2, 0.0))


def _router(h, w_r, b_r, tm):
    n = h.shape[0]
    return pl.pallas_call(
        functools.partial(_router_kernel, tm=tm),
        grid=(n // tm,),
        in_specs=[pl.BlockSpec((tm, D_MODEL), lambda i: (i, 0)),
                  pl.BlockSpec((D_MODEL, LANES), lambda i: (0, 0)),
                  pl.BlockSpec((LANES, 1), lambda i: (0, 0))],
        out_specs=[pl.BlockSpec((8, tm), lambda i: (0, i)),
                   pl.BlockSpec((8, tm), lambda i: (0, i)),
                   pl.BlockSpec((MOE_EXPERTS, LANES), lambda i: (0, 0))],
        out_shape=[jax.ShapeDtypeStruct((8, n), I32),
                   jax.ShapeDtypeStruct((8, n), F32),
                   jax.ShapeDtypeStruct((MOE_EXPERTS, LANES), F32)],
        scratch_shapes=[pltpu.VMEM((MOE_EXPERTS, LANES), F32)],
        compiler_params=_cparams("arbitrary"),
        name="moe_router",
    )(h, w_r, b_r)


def _row_copy(src_ref, s, dst_ref, d, sem):
    def first(r):
        return r * ROW_SPLIT if isinstance(r, int) else pl.multiple_of(r * ROW_SPLIT, ROW_SPLIT)

    return pltpu.make_async_copy(src_ref.at[pl.ds(first(s), ROW_SPLIT), :],
                                 dst_ref.at[pl.ds(first(d), ROW_SPLIT), :], sem)


def _tile_wait(src_ref, dst_ref, sem):
    pltpu.make_async_copy(src_ref.at[pl.ds(0, dst_ref.shape[0]), :], dst_ref, sem).wait()


def _experts_kernel(tok_ref, te_ref, na_ref, h_ref, wg_ref, wu_ref, wd_ref, y_ref,
                    xa, xb, wg_s, wu_s, wd_s, sem):
    i = pl.program_id(0)
    n_tiles = pl.num_programs(0)
    tr = MOE_ROW_TILE
    na = na_ref[0]
    active = i < na
    changed = jnp.logical_or(i == 0, te_ref[i] != te_ref[jnp.maximum(i - 1, 0)])

    @pl.when(i == 0)
    def _():
        def issue(r, carry):
            _row_copy(h_ref, tok_ref[r], xa, r, sem.at[0]).start()
            return carry

        lax.fori_loop(0, tr, issue, 0)

    @pl.when(jnp.logical_and(active, changed))
    def _():
        wg_s[...] = wg_ref[0].astype(BF16)
        wu_s[...] = wu_ref[0].astype(BF16)
        wd_s[...] = wd_ref[0].astype(BF16)

    def step(cur, cur_sem, nxt, nxt_sem):
        _tile_wait(h_ref, cur, cur_sem)
        base = jnp.minimum(i + 1, n_tiles - 1) * tr
        for r in range(tr):
            _row_copy(h_ref, tok_ref[base + r], nxt, r, nxt_sem).start(priority=r % DMA_QUEUES)
        x = _load_row_tiled(cur).astype(BF16)
        gate = _dot(x, wg_s[...])
        up = _dot(x, wu_s[...])
        hid = (gate * jax.nn.sigmoid(gate) * up).astype(BF16)
        _store_row_tiled(y_ref, _dot(hid, wd_s[...]))

        @pl.when(i == n_tiles - 1)
        def _():
            _tile_wait(h_ref, nxt, nxt_sem)

    even = i % 2 == 0

    @pl.when(jnp.logical_and(active, even))
    def _():
        step(xa, sem.at[0], xb, sem.at[1])

    @pl.when(jnp.logical_and(active, jnp.logical_not(even)))
    def _():
        step(xb, sem.at[1], xa, sem.at[0])

    @pl.when(jnp.logical_not(active))
    def _():
        y_ref[...] = jnp.zeros_like(y_ref)

    @pl.when(jnp.logical_and(i == na, even))
    def _():
        _tile_wait(h_ref, xa, sem.at[0])

    @pl.when(jnp.logical_and(i == na, jnp.logical_not(even)))
    def _():
        _tile_wait(h_ref, xb, sem.at[1])


def _experts(h, row_token, tile_expert, n_active, wg, wu, wd):
    rows = row_token.shape[0]
    tr = MOE_ROW_TILE
    wmap = lambda i, tok, te, na: (te[i], 0, 0)
    return pl.pallas_call(
        _experts_kernel,
        grid_spec=pltpu.PrefetchScalarGridSpec(
            num_scalar_prefetch=3,
            grid=(rows // tr,),
            in_specs=[pl.BlockSpec(memory_space=pl.ANY),
                      pl.BlockSpec((1, D_MODEL, MOE_D_FF), wmap),
                      pl.BlockSpec((1, D_MODEL, MOE_D_FF), wmap),
                      pl.BlockSpec((1, MOE_D_FF, D_MODEL), wmap)],
            out_specs=pl.BlockSpec((tr * ROW_SPLIT, LANES), lambda i, tok, te, na: (i, 0)),
            scratch_shapes=[pltpu.VMEM((tr * ROW_SPLIT, LANES), F32),
                            pltpu.VMEM((tr * ROW_SPLIT, LANES), F32),
                            pltpu.VMEM((D_MODEL, MOE_D_FF), BF16),
                            pltpu.VMEM((D_MODEL, MOE_D_FF), BF16),
                            pltpu.VMEM((MOE_D_FF, D_MODEL), BF16),
                            pltpu.SemaphoreType.DMA((2,))]),
        out_shape=jax.ShapeDtypeStruct((rows * ROW_SPLIT, LANES), F32),
        compiler_params=_cparams("arbitrary"),
        name="moe_experts",
    )(row_token, tile_expert, n_active, h, wg, wu, wd)


def _combine_kernel(pos1_ref, pos2_ref, y_ref, h_ref, p1_ref, p2_ref, g_ref, b_ref, out_ref,
                    a1, a2, b1, b2, sem, *, tm):
    i = pl.program_id(0)
    last = pl.num_programs(0) - 1

    @pl.when(i == 0)
    def _():
        def issue(t, carry):
            _row_copy(y_ref, pos1_ref[t], a1, t, sem.at[0]).start()
            _row_copy(y_ref, pos2_ref[t], a2, t, sem.at[0]).start()
            return carry

        lax.fori_loop(0, tm, issue, 0)

    def step(c1, c2, cur_sem, n1, n2, nxt_sem):
        _tile_wait(y_ref, c1, cur_sem)
        _tile_wait(y_ref, c2, cur_sem)
        base = jnp.minimum(i + 1, last) * tm
        for t in range(tm):
            _row_copy(y_ref, pos1_ref[base + t], n1, t, nxt_sem).start(priority=0)
            _row_copy(y_ref, pos2_ref[base + t], n2, t, nxt_sem).start(priority=DMA_QUEUES - 1)
        ffn = p1_ref[...] * _load_row_tiled(c1) + p2_ref[...] * _load_row_tiled(c2)
        out_ref[...] = _layer_norm(DN_ALPHA * h_ref[...] + ffn, g_ref[...], b_ref[...])

        @pl.when(i == last)
        def _():
            _tile_wait(y_ref, n1, nxt_sem)
            _tile_wait(y_ref, n2, nxt_sem)

    @pl.when(i % 2 == 0)
    def _():
        step(a1, a2, sem.at[0], b1, b2, sem.at[1])

    @pl.when(i % 2 == 1)
    def _():
        step(b1, b2, sem.at[1], a1, a2, sem.at[0])


def _combine(y, h, pos1, pos2, p1, p2, g, b, tm):
    n = h.shape[0]
    tok = lambda i, a, c: (i, 0)
    fixed = lambda i, a, c: (0, 0)
    return pl.pallas_call(
        functools.partial(_combine_kernel, tm=tm),
        grid_spec=pltpu.PrefetchScalarGridSpec(
            num_scalar_prefetch=2,
            grid=(n // tm,),
            in_specs=[pl.BlockSpec(memory_space=pl.ANY),
                      pl.BlockSpec((tm, D_MODEL), tok),
                      pl.BlockSpec((tm, 1), tok), pl.BlockSpec((tm, 1), tok),
                      pl.BlockSpec((1, D_MODEL), fixed), pl.BlockSpec((1, D_MODEL), fixed)],
            out_specs=pl.BlockSpec((tm, D_MODEL), tok),
            scratch_shapes=[pltpu.VMEM((tm * ROW_SPLIT, LANES), F32)] * 4 + [pltpu.SemaphoreType.DMA((2,))]),
        out_shape=jax.ShapeDtypeStruct((n, D_MODEL), F32),
        compiler_params=_cparams("arbitrary"),
        name="moe_combine_ln",
    )(pos1, pos2, y, h, p1, p2, g, b)


def _plan_kernel(ints_ref, cnt_ref, pos_ref, meta_ref, *, layer, chunk):
    tr = MOE_ROW_TILE
    n = ints_ref.shape[1]
    ne = MOE_EXPERTS
    tiles = (cnt_ref[...] + (tr - 1.0)) * (1.0 / tr)
    tiles = tiles.astype(I32).astype(F32)
    lower = lax.broadcasted_iota(I32, (ne, ne), 0) >= lax.broadcasted_iota(I32, (ne, ne), 1)
    ends = _dot(jnp.where(lower, 1.0, 0.0).astype(BF16), tiles.astype(BF16))
    start_col = ((ends - tiles) * tr).astype(I32)[:, 0:1]
    r8 = lax.broadcasted_iota(I32, (8, chunk), 0)
    re = lax.broadcasted_iota(I32, (ne, chunk), 0)
    for c in range(n // chunk):
        blk = ints_ref[:, c * chunk:(c + 1) * chunk]
        s1 = jnp.sum(jnp.where(re == blk[0:1], start_col, 0), 0, keepdims=True)
        s2 = jnp.sum(jnp.where(re == blk[1:2], start_col, 0), 0, keepdims=True)
        pos_ref[:, c * chunk:(c + 1) * chunk] = jnp.where(
            r8 == 0, s1 + blk[2:3], jnp.where(r8 == 1, s2 + blk[3:4], 0))
    width = meta_ref.shape[1]
    tile_id = lax.broadcasted_iota(I32, (ne, width), 1).astype(F32)
    te = jnp.sum(jnp.where(ends[:, 0:1] <= tile_id, 1, 0), 0, keepdims=True)
    te = jnp.minimum(te, ne - 1) + layer * ne
    n_used = ends[ne - 1:ne, 0:1].astype(I32)
    rm = lax.broadcasted_iota(I32, (8, width), 0)
    meta_ref[...] = jnp.where(rm == 0, te, jnp.where(rm == 1, n_used, 0))


def _plan(ints, cnt, layer, n_tiles):
    n = ints.shape[1]
    width = -(-n_tiles // LANES) * LANES
    return pl.pallas_call(
        functools.partial(_plan_kernel, layer=layer, chunk=2048),
        out_shape=[jax.ShapeDtypeStruct((8, n), I32), jax.ShapeDtypeStruct((8, width), I32)],
        compiler_params=pltpu.CompilerParams(vmem_limit_bytes=VMEM_LIMIT),
        name="moe_plan",
    )(ints, cnt)


def _moe_layer(h, h8, layer, group_w, group_b, expert_w, expert_b, gate_w, up_w, down_w, ln_g, ln_b):
    n = h.shape[0]
    ew = jnp.transpose(expert_w, (1, 0, 2)).reshape(D_MODEL, MOE_EXPERTS)
    w_r = jnp.zeros((D_MODEL, LANES), F32).at[:, 0:MOE_GROUPS].set(group_w).at[:, 8:8 + MOE_EXPERTS].set(ew)
    b_r = jnp.zeros((LANES,), F32).at[0:MOE_GROUPS].set(group_b).at[8:8 + MOE_EXPERTS].set(expert_b.reshape(-1))
    ints, flts, cnt = _router(h, w_r, b_r.reshape(LANES, 1), 512)
    tr = MOE_ROW_TILE
    n_tiles = (2 * n) // tr + MOE_EXPERTS
    pos, meta = _plan(ints, cnt, layer, n_tiles)
    pos1, pos2 = pos[0], pos[1]

    tok_ids = jnp.arange(n, dtype=I32)
    row_token = jnp.zeros((n_tiles * tr,), I32).at[jnp.concatenate([pos1, pos2])].set(
        jnp.concatenate([tok_ids, tok_ids]))
    y = _experts(h8, row_token, meta[0, :n_tiles], meta[1, :1],
                 gate_w.reshape(-1, D_MODEL, MOE_D_FF),
                 up_w.reshape(-1, D_MODEL, MOE_D_FF),
                 down_w.reshape(-1, MOE_D_FF, D_MODEL))
    return _combine(y, h, pos1, pos2, flts[0].reshape(n, 1), flts[1].reshape(n, 1),
                    ln_g.reshape(1, -1), ln_b.reshape(1, -1), 256)


def _pad_heads(w, axis):
    parts = []
    h0 = 0
    for _, _, nh in ATTN_GROUPS:
        sl = [slice(None)] * w.ndim
        sl[axis] = slice(h0 * ATTN_HEAD_DIM, (h0 + nh) * ATTN_HEAD_DIM)
        part = w[tuple(sl)]
        pad = [(0, 0)] * w.ndim
        pad[axis] = (0, ATTN_SLAB - nh * ATTN_HEAD_DIM)
        parts.append(jnp.pad(part, pad))
        h0 += nh
    return parts


def _ssd_layer(h, in_w, conv_w, conv_b, dt_bias, a_log, d_skip, norm_w, out_w, ln_g, ln_b, bsz, seq):
    in_b = in_w.astype(BF16)
    z = _matmul(h, in_b[:, :SSM_D_INNER], F32, 1024, 1024)
    xbc = _matmul(h, in_b[:, SSM_D_INNER:SSM_D_INNER + SSM_CONV_DIM], F32, 1024, 1024)
    dt_w = jnp.pad(in_w[:, SSM_D_INNER + SSM_CONV_DIM:], ((0, 0), (0, LANES - SSM_HEADS)))
    dt_raw = _matmul_f32ish(h, dt_w, 512)
    pad32 = lambda v: jnp.pad(v, (0, LANES - SSM_HEADS)).reshape(1, LANES)
    y = _ssd(z, xbc, dt_raw, conv_w, conv_b.reshape(1, -1), pad32(dt_bias), pad32(a_log),
             jnp.repeat(d_skip, SSM_HEAD_DIM).reshape(1, -1), norm_w.reshape(1, -1), bsz, seq)
    return _matmul_res_ln(y, out_w.astype(BF16), h, ln_g.reshape(1, -1), ln_b.reshape(1, -1), 512)


def _attn_layer(h, kv_w, q_w, o_w, rel_bias, ln_g, ln_b, bsz, seq):
    width = ATTN_HEADS * ATTN_HEAD_DIM
    w_all = jnp.concatenate(_pad_heads(q_w, 1) + _pad_heads(kv_w[:, :width], 1)
                            + _pad_heads(kv_w[:, width:], 1), axis=1).astype(BF16)
    qkv = _matmul(h, w_all, BF16, 1024, 3 * ATTN_SLAB)
    outs, lses = [], []
    h0 = 0
    for gi, (_, dil, nh) in enumerate(ATTN_GROUPS):
        bias_p, bias_c = _group_bias(rel_bias, h0, nh, dil)
        o, lse = _attention_group(qkv, bias_p, bias_c, gi, dil, nh, bsz, seq)
        outs.append(o)
        lses.append(lse)
        h0 += nh
    w_o = jnp.stack(_pad_heads(o_w, 0)).astype(BF16)
    return _attn_out(outs, lses, w_o, h, ln_g.reshape(1, -1), ln_b.reshape(1, -1), 512)


def kernel(x, ssm_in_w, ssm_conv_w, ssm_conv_b, ssm_dt_bias, ssm_a_log, ssm_d, ssm_norm_w, ssm_out_w,
           kv_w, attn_q_w, attn_o_w, rel_bias, moe_group_w, moe_group_b, moe_expert_w, moe_expert_b,
           moe_gate_w, moe_up_w, moe_down_w, ln_g, ln_b):
    bsz, seq, d = x.shape
    h = x.reshape(bsz * seq, d)
    n_ssd = DEPTH // 2
    for i in range(DEPTH):
        if i < n_ssd:
            h, h8 = _ssd_layer(h, ssm_in_w[i], ssm_conv_w[i], ssm_conv_b[i], ssm_dt_bias[i], ssm_a_log[i],
                               ssm_d[i], ssm_norm_w[i], ssm_out_w[i], ln_g[i, 0], ln_b[i, 0], bsz, seq)
        else:
            j = i - n_ssd
            h, h8 = _attn_layer(h, kv_w, attn_q_w[j], attn_o_w[j], rel_bias, ln_g[i, 0], ln_b[i, 0], bsz, seq)
        h = _moe_layer(h, h8, i, moe_group_w[i], moe_group_b[i], moe_expert_w[i], moe_expert_b[i],
                       moe_gate_w, moe_up_w, moe_down_w, ln_g[i, 1], ln_b[i, 1])
    return h.reshape(bsz, seq, d)
```

```python
import functools
import math

import numpy as np
import jax
import jax.numpy as jnp
from jax import lax
from jax.experimental import pallas as pl
from jax.experimental.pallas import tpu as pltpu

F32 = jnp.float32
BF16 = jnp.bfloat16
I32 = jnp.int32

D_MODEL = 1024
DEPTH = 2
DN_ALPHA = (2 * DEPTH) ** 0.25
LN_EPS = 1e-5

SSM_D_INNER = 2048
SSM_HEAD_DIM = 64
SSM_HEADS = 32
SSM_GROUPS = 4
SSM_STATE = 128
SSM_CONV = 4
SSM_CHUNK = 128
SSM_CONV_DIM = SSM_D_INNER + 2 * SSM_GROUPS * SSM_STATE

ATTN_HEAD_DIM = 64
ATTN_GROUPS = ((128, 1, 6), (512, 4, 5), (2048, 16, 5))
ATTN_HEADS = 16
ATTN_BLOCK = 128
N_BUCKETS = 32
MAX_DISTANCE = 2048
ATTN_SLAB = 384
NEG_BIG = -1e30

MOE_GROUPS = 4
MOE_EPG = 8
MOE_EXPERTS = MOE_GROUPS * MOE_EPG
MOE_D_FF = 512
MOE_ROW_TILE = 512

LANES = 128
DMA_QUEUES = 2
VMEM_LIMIT = 48 * 1024 * 1024


def _cparams(*sem):
    return pltpu.CompilerParams(dimension_semantics=sem, vmem_limit_bytes=VMEM_LIMIT)


def _layer_norm(x, g, b):
    mu = jnp.mean(x, -1, keepdims=True)
    xc = x - mu
    var = jnp.mean(xc * xc, -1, keepdims=True)
    return xc * lax.rsqrt(var + LN_EPS) * g + b


def _split2(x):
    hi = x.astype(BF16)
    lo = (x - hi.astype(F32)).astype(BF16)
    return hi, lo


def _dot(a, b):
    return jnp.dot(a, b, preferred_element_type=F32)


def _dot_f32ish(a, b):
    ah, al = _split2(a)
    bh, bl = _split2(b)
    return _dot(ah, bh) + _dot(al, bh) + _dot(ah, bl)


def _mm_kernel(a_ref, b_ref, o_ref):
    o_ref[...] = _dot(a_ref[...].astype(BF16), b_ref[...]).astype(o_ref.dtype)


def _matmul(a, b, out_dtype, tm, tn):
    m, k = a.shape
    nc = b.shape[1]
    return pl.pallas_call(
        _mm_kernel,
        grid=(m // tm, nc // tn),
        in_specs=[pl.BlockSpec((tm, k), lambda i, j: (i, 0)),
                  pl.BlockSpec((k, tn), lambda i, j: (0, j))],
        out_specs=pl.BlockSpec((tm, tn), lambda i, j: (i, j)),
        out_shape=jax.ShapeDtypeStruct((m, nc), out_dtype),
        compiler_params=_cparams("parallel", "parallel"),
        name="matmul",
    )(a, b)


def _mm3_kernel(a_ref, b_ref, o_ref):
    o_ref[...] = _dot_f32ish(a_ref[...], b_ref[...])


def _matmul_f32ish(a, b, tm):
    m, k = a.shape
    nc = b.shape[1]
    return pl.pallas_call(
        _mm3_kernel,
        grid=(m // tm,),
        in_specs=[pl.BlockSpec((tm, k), lambda i: (i, 0)),
                  pl.BlockSpec((k, nc), lambda i: (0, 0))],
        out_specs=pl.BlockSpec((tm, nc), lambda i: (i, 0)),
        out_shape=jax.ShapeDtypeStruct((m, nc), F32),
        compiler_params=_cparams("parallel"),
        name="matmul_f32ish",
    )(a, b)


ROW_SPLIT = D_MODEL // LANES


def _store_row_tiled(ref, val):
    rows = val.shape[0]
    for c in range(ROW_SPLIT):
        ref[pl.ds(c, rows, stride=ROW_SPLIT), :] = val[:, c * LANES:(c + 1) * LANES]


def _load_row_tiled(ref):
    rows = ref.shape[0] // ROW_SPLIT
    return jnp.concatenate([ref[pl.ds(c, rows, stride=ROW_SPLIT), :] for c in range(ROW_SPLIT)], axis=1)


def _mm_ln_kernel(a_ref, w_ref, h_ref, g_ref, b_ref, o_ref, o8_ref):
    acc = _dot(a_ref[...], w_ref[...])
    out = _layer_norm(DN_ALPHA * h_ref[...] + acc, g_ref[...], b_ref[...])
    o_ref[...] = out
    _store_row_tiled(o8_ref, out)


def _matmul_res_ln(a, w, h, g, b, tm):
    m, k = a.shape
    d = w.shape[1]
    return pl.pallas_call(
        _mm_ln_kernel,
        grid=(m // tm,),
        in_specs=[pl.BlockSpec((tm, k), lambda i: (i, 0)),
                  pl.BlockSpec((k, d), lambda i: (0, 0)),
                  pl.BlockSpec((tm, d), lambda i: (i, 0)),
                  pl.BlockSpec((1, d), lambda i: (0, 0)),
                  pl.BlockSpec((1, d), lambda i: (0, 0))],
        out_specs=[pl.BlockSpec((tm, d), lambda i: (i, 0)),
                   pl.BlockSpec((tm * ROW_SPLIT, LANES), lambda i: (i, 0))],
        out_shape=[jax.ShapeDtypeStruct((m, d), F32),
                   jax.ShapeDtypeStruct((m * ROW_SPLIT, LANES), F32)],
        compiler_params=_cparams("parallel"),
        name="matmul_res_ln",
    )(a, w, h, g, b)


def _ssd_kernel(z_ref, xbc_ref, dt_ref, cw_ref, cb_ref, dtb_ref, alog_ref, dsk_ref, nw_ref,
                y_ref, xe_ref, st_ref):
    q = SSM_CHUNK

    @pl.when(pl.program_id(1) == 0)
    def _():
        xe_ref[0:8, :] = jnp.zeros((8, SSM_CONV_DIM), F32)
        st_ref[...] = jnp.zeros_like(st_ref)

    u = xbc_ref[...]
    xe_ref[8:8 + q, :] = u
    w = cw_ref[...]
    conv = (cb_ref[...] + w[3:4] * u + w[2:3] * xe_ref[7:7 + q, :]
            + w[1:2] * xe_ref[6:6 + q, :] + w[0:1] * xe_ref[5:5 + q, :])
    xe_ref[0:8, :] = xe_ref[q:q + 8, :]
    act = conv * jax.nn.sigmoid(conv)

    pre = dt_ref[...] + dtb_ref[...]
    dt = jnp.maximum(pre, 0.0) + jnp.log(1.0 + jnp.exp(-jnp.abs(pre)))
    adt = dt * (-jnp.exp(alog_ref[...]))

    row = lax.broadcasted_iota(I32, (q, q), 0)
    col = lax.broadcasted_iota(I32, (q, q), 1)
    tril = row >= col
    tri_b = jnp.where(tril, 1.0, 0.0).astype(BF16)
    a_hi = adt.astype(BF16)
    r1 = adt - a_hi.astype(F32)
    a_mid = r1.astype(BF16)
    a_lo = (r1 - a_mid.astype(F32)).astype(BF16)
    acs = _dot(tri_b, a_hi) + _dot(tri_b, a_mid) + _dot(tri_b, a_lo)
    acs_t = acs.T
    eacs = jnp.exp(acs)
    left = col < SSM_HEAD_DIM

    for g in range(SSM_GROUPS):
        b0 = SSM_D_INNER + g * SSM_STATE
        c0 = SSM_D_INNER + SSM_GROUPS * SSM_STATE + g * SSM_STATE
        bm = act[:, b0:b0 + SSM_STATE]
        cm = act[:, c0:c0 + SSM_STATE].astype(BF16)
        cb = lax.dot_general(cm, bm.astype(BF16), (((1,), (1,)), ((), ())),
                             preferred_element_type=F32)
        bm_t = bm.T
        gs = g * 512
        y_off = _dot(cm, st_ref[:, gs:gs + 512].astype(BF16))
        slabs = []
        for pr in range(4):
            ha = g * 8 + pr * 2
            hb = ha + 1
            cs = gs + pr * LANES
            x2 = act[:, cs:cs + LANES]
            dt2 = jnp.where(left, dt[:, ha:ha + 1], dt[:, hb:hb + 1])
            xdt = (x2 * dt2).astype(BF16)
            ys, ups = [], []
            for h in (ha, hb):
                a_col = acs[:, h:h + 1]
                a_row = acs_t[h:h + 1, :]
                decay = jnp.where(tril, jnp.exp(a_col - a_row), 0.0)
                ys.append(_dot((cb * decay).astype(BF16), xdt))
                to_end = jnp.exp(a_row[:, q - 1:q] - a_row)
                ups.append(_dot((bm_t * to_end).astype(BF16), xdt))
            y_diag = jnp.where(left, ys[0], ys[1])
            upd = jnp.where(left, ups[0], ups[1])
            e2 = jnp.where(left, eacs[:, ha:ha + 1], eacs[:, hb:hb + 1])
            cd = jnp.where(left[0:1, :], eacs[q - 1:q, ha:ha + 1], eacs[q - 1:q, hb:hb + 1])
            y2 = y_diag + y_off[:, pr * LANES:(pr + 1) * LANES] * e2 + dsk_ref[:, cs:cs + LANES] * x2
            st_ref[:, cs:cs + LANES] = st_ref[:, cs:cs + LANES] * cd + upd
            slabs.append(y2)
        yg = jnp.concatenate(slabs, axis=1)
        zg = z_ref[:, gs:gs + 512]
        yg = yg * (zg * jax.nn.sigmoid(zg))
        ms = jnp.mean(yg * yg, -1, keepdims=True)
        y_ref[:, gs:gs + 512] = (yg * lax.rsqrt(ms + LN_EPS) * nw_ref[:, gs:gs + 512]).astype(y_ref.dtype)


def _ssd(z, xbc, dt_raw, conv_w, conv_b, dt_bias, a_log, d_rep, norm_w, bsz, seq):
    n = z.shape[0]
    nchunk = seq // SSM_CHUNK
    q = SSM_CHUNK
    tok = lambda b, c: (b * nchunk + c, 0)
    fixed = lambda b, c: (0, 0)
    return pl.pallas_call(
        _ssd_kernel,
        grid=(bsz, nchunk),
        in_specs=[pl.BlockSpec((q, SSM_D_INNER), tok),
                  pl.BlockSpec((q, SSM_CONV_DIM), tok),
                  pl.BlockSpec((q, LANES), tok),
                  pl.BlockSpec((SSM_CONV, SSM_CONV_DIM), fixed),
                  pl.BlockSpec((1, SSM_CONV_DIM), fixed),
                  pl.BlockSpec((1, LANES), fixed),
                  pl.BlockSpec((1, LANES), fixed),
                  pl.BlockSpec((1, SSM_D_INNER), fixed),
                  pl.BlockSpec((1, SSM_D_INNER), fixed)],
        out_specs=pl.BlockSpec((q, SSM_D_INNER), tok),
        out_shape=jax.ShapeDtypeStruct((n, SSM_D_INNER), BF16),
        scratch_shapes=[pltpu.VMEM((q + 8, SSM_CONV_DIM), F32),
                        pltpu.VMEM((SSM_STATE, SSM_D_INNER), F32)],
        compiler_params=_cparams("parallel", "arbitrary"),
        name="ssd_chunk",
    )(z, xbc, dt_raw, conv_w, conv_b, dt_bias, a_log, d_rep, norm_w)


def _attn_kernel(q_ref, kp_ref, kc_ref, vp_ref, vc_ref, bp_ref, bc_ref, o_ref, lse_ref, *, nh):
    has_prev = pl.program_id(1) > 0
    qb = q_ref[0]
    kp, kc, vp, vc = kp_ref[0], kc_ref[0], vp_ref[0], vc_ref[0]
    lane = lax.broadcasted_iota(I32, (ATTN_BLOCK, LANES), 1)
    left = lane < ATTN_HEAD_DIM
    zero = jnp.zeros((), BF16)
    nt = (((1,), (1,)), ((), ()))
    lse_sum = jnp.zeros((ATTN_BLOCK, 1), F32)
    outs = []
    for pr in range(ATTN_SLAB // LANES):
        sl = slice(pr * LANES, (pr + 1) * LANES)
        q2, kp2, kc2, vp2, vc2 = qb[:, sl], kp[:, sl], kc[:, sl], vp[:, sl], vc[:, sl]
        halves = []
        for side in range(2):
            hh = pr * 2 + side
            if hh >= nh:
                halves.append(jnp.zeros((ATTN_BLOCK, LANES), F32))
                continue
            qm = jnp.where(left if side == 0 else ~left, q2, zero)
            s_p = lax.dot_general(qm, kp2, nt, preferred_element_type=F32) * (ATTN_HEAD_DIM ** -0.5)
            s_c = lax.dot_general(qm, kc2, nt, preferred_element_type=F32) * (ATTN_HEAD_DIM ** -0.5)
            s_p = jnp.where(has_prev, s_p + bp_ref[hh], NEG_BIG)
            s_c = s_c + bc_ref[hh]
            m = jnp.maximum(jnp.max(s_p, -1, keepdims=True), jnp.max(s_c, -1, keepdims=True))
            p_p = jnp.exp(s_p - m)
            p_c = jnp.exp(s_c - m)
            den = jnp.sum(p_p, -1, keepdims=True) + jnp.sum(p_c, -1, keepdims=True)
            o = _dot(p_p.astype(BF16), vp2) + _dot(p_c.astype(BF16), vc2)
            halves.append(o / den)
            lse_sum = lse_sum + (m + jnp.log(den))
        outs.append(jnp.where(left, halves[0], halves[1]))
    o_ref[0] = jnp.concatenate(outs, axis=1)
    lse_ref[0] = jnp.broadcast_to(lse_sum * (1.0 / nh), (ATTN_BLOCK, LANES))


def _attention_group(qkv, bias_p, bias_c, gi, dil, nh, bsz, seq):
    l = seq // dil
    nb = l // ATTN_BLOCK
    nslab = qkv.shape[1] // ATTN_SLAB
    view = qkv.reshape(bsz, l, dil * qkv.shape[1])
    blk = (1, ATTN_BLOCK, ATTN_SLAB)

    def cur(which):
        return lambda b, n, r: (b, n, r * nslab + which * 3 + gi)

    def prev(which):
        return lambda b, n, r: (b, jnp.maximum(n - 1, 0), r * nslab + which * 3 + gi)

    fixed = lambda b, n, r: (0, 0, 0)
    o, lse = pl.pallas_call(
        functools.partial(_attn_kernel, nh=nh),
        grid=(bsz, nb, dil),
        in_specs=[pl.BlockSpec(blk, cur(0)),
                  pl.BlockSpec(blk, prev(1)), pl.BlockSpec(blk, cur(1)),
                  pl.BlockSpec(blk, prev(2)), pl.BlockSpec(blk, cur(2)),
                  pl.BlockSpec((nh, ATTN_BLOCK, ATTN_BLOCK), fixed),
                  pl.BlockSpec((nh, ATTN_BLOCK, ATTN_BLOCK), fixed)],
        out_specs=[pl.BlockSpec(blk, lambda b, n, r: (b, n, r)),
                   pl.BlockSpec((1, ATTN_BLOCK, LANES), lambda b, n, r: (b, n, r))],
        out_shape=[jax.ShapeDtypeStruct((bsz, l, dil * ATTN_SLAB), F32),
                   jax.ShapeDtypeStruct((bsz, l, dil * LANES), F32)],
        compiler_params=_cparams("parallel", "parallel", "parallel"),
        name=f"dilated_attn_g{gi}",
    )(view, view, view, view, view, bias_p, bias_c)
    return o.reshape(bsz * seq, ATTN_SLAB), lse.reshape(bsz * seq, LANES)


def _attn_out_kernel(o0_ref, o1_ref, o2_ref, l0_ref, l1_ref, l2_ref, w_ref, h_ref, g_ref, b_ref,
                     out_ref, out8_ref):
    ls = [r[:, 0:1] for r in (l0_ref, l1_ref, l2_ref)]
    m = jnp.maximum(jnp.maximum(ls[0], ls[1]), ls[2])
    es = [jnp.exp(v - m) for v in ls]
    scale = len(ATTN_GROUPS) / (es[0] + es[1] + es[2])
    acc = None
    for gi, o_ref in enumerate((o0_ref, o1_ref, o2_ref)):
        part = _dot((o_ref[...] * (es[gi] * scale)).astype(BF16), w_ref[gi])
        acc = part if acc is None else acc + part
    out = _layer_norm(DN_ALPHA * h_ref[...] + acc, g_ref[...], b_ref[...])
    out_ref[...] = out
    _store_row_tiled(out8_ref, out)


def _attn_out(outs, lses, w, h, g, b, tm):
    n = h.shape[0]
    tok = lambda i: (i, 0)
    fixed2 = lambda i: (0, 0)
    return pl.pallas_call(
        _attn_out_kernel,
        grid=(n // tm,),
        in_specs=[pl.BlockSpec((tm, ATTN_SLAB), tok)] * 3 + [pl.BlockSpec((tm, LANES), tok)] * 3
                 + [pl.BlockSpec((3, ATTN_SLAB, D_MODEL), lambda i: (0, 0, 0)),
                    pl.BlockSpec((tm, D_MODEL), tok),
                    pl.BlockSpec((1, D_MODEL), fixed2), pl.BlockSpec((1, D_MODEL), fixed2)],
        out_specs=[pl.BlockSpec((tm, D_MODEL), tok), pl.BlockSpec((tm * ROW_SPLIT, LANES), tok)],
        out_shape=[jax.ShapeDtypeStruct((n, D_MODEL), F32),
                   jax.ShapeDtypeStruct((n * ROW_SPLIT, LANES), F32)],
        compiler_params=_cparams("parallel"),
        name="attn_out_ln",
    )(*outs, *lses, w, h, g, b)


def _t5_bucket(dist):
    max_exact = N_BUCKETS // 2
    n = np.maximum(dist, 1).astype(np.float64)
    large = max_exact + (np.log(n / max_exact) / np.log(MAX_DISTANCE / max_exact)
                         * (N_BUCKETS - max_exact)).astype(np.int32)
    large = np.minimum(large, N_BUCKETS - 1)
    return np.where(dist < max_exact, dist, large).astype(np.int32)


def _group_bias(rel_bias, h0, nh, dil):
    qi = np.arange(ATTN_BLOCK)[:, None]
    ki = np.arange(ATTN_BLOCK)[None, :]
    tabs = []
    for delta, band in ((qi + ATTN_BLOCK - ki, ki >= qi), (qi - ki, ki <= qi)):
        bucket = _t5_bucket(np.clip(delta, 0, None) * dil)
        onehot = (bucket[..., None] == np.arange(N_BUCKETS)).astype(np.float32)
        t = jnp.einsum("qkb,bh->hqk", onehot, rel_bias[:, h0:h0 + nh], precision=lax.Precision.HIGHEST)
        tabs.append(jnp.where(band[None], t, NEG_BIG).astype(F32))
    return tabs


def _router_kernel(h_ref, w_ref, b_ref, ints_ref, flts_ref, cnt_ref, carry_ref, *, tm):
    @pl.when(pl.program_id(0) == 0)
    def _():
        carry_ref[...] = jnp.zeros_like(carry_ref)

    lt = _dot_f32ish(h_ref[...], w_ref[...]).T + b_ref[...]
    gl = lt[0:MOE_GROUPS]
    r4 = lax.broadcasted_iota(I32, (MOE_GROUPS, tm), 0)
    gmax = jnp.max(gl, 0, keepdims=True)
    gidx = jnp.min(jnp.where(gl == gmax, r4, MOE_GROUPS), 0, keepdims=True)
    gval = 1.0 / jnp.sum(jnp.exp(gl - gmax), 0, keepdims=True)

    esel = jnp.zeros((MOE_EPG, tm), F32)
    for g in range(MOE_GROUPS):
        esel = jnp.where(gidx == g, lt[8 + g * MOE_EPG:8 + (g + 1) * MOE_EPG], esel)
    r8 = lax.broadcasted_iota(I32, (MOE_EPG, tm), 0)
    v1 = jnp.max(esel, 0, keepdims=True)
    i1 = jnp.min(jnp.where(esel == v1, r8, MOE_EPG), 0, keepdims=True)
    rest = jnp.where(r8 == i1, -jnp.inf, esel)
    v2 = jnp.max(rest, 0, keepdims=True)
    i2 = jnp.min(jnp.where(rest == v2, r8, MOE_EPG), 0, keepdims=True)
    t = jnp.exp(v2 - v1)
    p1 = gval / (1.0 + t)
    p2 = p1 * t
    e1 = gidx * MOE_EPG + i1
    e2 = gidx * MOE_EPG + i2

    r32 = lax.broadcasted_iota(I32, (MOE_EXPERTS, tm), 0)
    oh1 = r32 == e1
    oh2 = r32 == e2
    oh = jnp.where(oh1 | oh2, 1.0, 0.0)
    tr = lax.broadcasted_iota(I32, (tm, tm), 0)
    tc = lax.broadcasted_iota(I32, (tm, tm), 1)
    before = jnp.where(tr < tc, 1.0, 0.0).astype(BF16)
    base = _dot(oh.astype(BF16), before) + carry_ref[:, 0:1]
    rank1 = jnp.sum(jnp.where(oh1, base, 0.0), 0, keepdims=True).astype(I32)
    rank2 = jnp.sum(jnp.where(oh2, base, 0.0), 0, keepdims=True).astype(I32)
    carry_ref[...] = carry_ref[...] + jnp.sum(oh, 1, keepdims=True)
    cnt_ref[...] = carry_ref[...]

    ints_ref[...] = jnp.where(r8 == 0, e1, jnp.where(r8 == 1, e2, jnp.where(r8 == 2, rank1,
                              jnp.where(r8 == 3, rank2, 0))))
    flts_ref[...] = jnp.where(r8 == 0, p1, jnp.where(r8 == 1, p2, 0.0))


def _router(h, w_r, b_r, tm):
    n = h.shape[0]
    return pl.pallas_call(
        functools.partial(_router_kernel, tm=tm),
        grid=(n // tm,),
        in_specs=[pl.BlockSpec((tm, D_MODEL), lambda i: (i, 0)),
                  pl.BlockSpec((D_MODEL, LANES), lambda i: (0, 0)),
                  pl.BlockSpec((LANES, 1), lambda i: (0, 0))],
        out_specs=[pl.BlockSpec((8, tm), lambda i: (0, i)),
                   pl.BlockSpec((8, tm), lambda i: (0, i)),
                   pl.BlockSpec((MOE_EXPERTS, LANES), lambda i: (0, 0))],
        out_shape=[jax.ShapeDtypeStruct((8, n), I32),
                   jax.ShapeDtypeStruct((8, n), F32),
                   jax.ShapeDtypeStruct((MOE_EXPERTS, LANES), F32)],
        scratch_shapes=[pltpu.VMEM((MOE_EXPERTS, LANES), F32)],
        compiler_params=_cparams("arbitrary"),
        name="moe_router",
    )(h, w_r, b_r)


def _row_copy(src_ref, s, dst_ref, d, sem):
    def first(r):
        return r * ROW_SPLIT if isinstance(r, int) else pl.multiple_of(r * ROW_SPLIT, ROW_SPLIT)

    return pltpu.make_async_copy(src_ref.at[pl.ds(first(s), ROW_SPLIT), :],
                                 dst_ref.at[pl.ds(first(d), ROW_SPLIT), :], sem)


def _tile_wait(src_ref, dst_ref, sem):
    pltpu.make_async_copy(src_ref.at[pl.ds(0, dst_ref.shape[0]), :], dst_ref, sem).wait()


def _experts_kernel(tok_ref, te_ref, na_ref, h_ref, wg_ref, wu_ref, wd_ref, y_ref,
                    xa, xb, wg_s, wu_s, wd_s, sem):
    i = pl.program_id(0)
    n_tiles = pl.num_programs(0)
    tr = MOE_ROW_TILE
    na = na_ref[0]
    active = i < na
    changed = jnp.logical_or(i == 0, te_ref[i] != te_ref[jnp.maximum(i - 1, 0)])

    @pl.when(i == 0)
    def _():
        def issue(r, carry):
            _row_copy(h_ref, tok_ref[r], xa, r, sem.at[0]).start()
            return carry

        lax.fori_loop(0, tr, issue, 0)

    @pl.when(jnp.logical_and(active, changed))
    def _():
        wg_s[...] = wg_ref[0].astype(BF16)
        wu_s[...] = wu_ref[0].astype(BF16)
        wd_s[...] = wd_ref[0].astype(BF16)

    def step(cur, cur_sem, nxt, nxt_sem):
        _tile_wait(h_ref, cur, cur_sem)
        base = jnp.minimum(i + 1, n_tiles - 1) * tr
        for r in range(tr):
            _row_copy(h_ref, tok_ref[base + r], nxt, r, nxt_sem).start(priority=r % DMA_QUEUES)
        x = _load_row_tiled(cur).astype(BF16)
        gate = _dot(x, wg_s[...])
        up = _dot(x, wu_s[...])
        hid = (gate * jax.nn.sigmoid(gate) * up).astype(BF16)
        _store_row_tiled(y_ref, _dot(hid, wd_s[...]))

        @pl.when(i == n_tiles - 1)
        def _():
            _tile_wait(h_ref, nxt, nxt_sem)

    even = i % 2 == 0

    @pl.when(jnp.logical_and(active, even))
    def _():
        step(xa, sem.at[0], xb, sem.at[1])

    @pl.when(jnp.logical_and(active, jnp.logical_not(even)))
    def _():
        step(xb, sem.at[1], xa, sem.at[0])

    @pl.when(jnp.logical_not(active))
    def _():
        y_ref[...] = jnp.zeros_like(y_ref)

    @pl.when(jnp.logical_and(i == na, even))
    def _():
        _tile_wait(h_ref, xa, sem.at[0])

    @pl.when(jnp.logical_and(i == na, jnp.logical_not(even)))
    def _():
        _tile_wait(h_ref, xb, sem.at[1])


def _experts(h, row_token, tile_expert, n_active, wg, wu, wd):
    rows = row_token.shape[0]
    tr = MOE_ROW_TILE
    wmap = lambda i, tok, te, na: (te[i], 0, 0)
    return pl.pallas_call(
        _experts_kernel,
        grid_spec=pltpu.PrefetchScalarGridSpec(
            num_scalar_prefetch=3,
            grid=(rows // tr,),
            in_specs=[pl.BlockSpec(memory_space=pl.ANY),
                      pl.BlockSpec((1, D_MODEL, MOE_D_FF), wmap),
                      pl.BlockSpec((1, D_MODEL, MOE_D_FF), wmap),
                      pl.BlockSpec((1, MOE_D_FF, D_MODEL), wmap)],
            out_specs=pl.BlockSpec((tr * ROW_SPLIT, LANES), lambda i, tok, te, na: (i, 0)),
            scratch_shapes=[pltpu.VMEM((tr * ROW_SPLIT, LANES), F32),
                            pltpu.VMEM((tr * ROW_SPLIT, LANES), F32),
                            pltpu.VMEM((D_MODEL, MOE_D_FF), BF16),
                            pltpu.VMEM((D_MODEL, MOE_D_FF), BF16),
                            pltpu.VMEM((MOE_D_FF, D_MODEL), BF16),
                            pltpu.SemaphoreType.DMA((2,))]),
        out_shape=jax.ShapeDtypeStruct((rows * ROW_SPLIT, LANES), F32),
        compiler_params=_cparams("arbitrary"),
        name="moe_experts",
    )(row_token, tile_expert, n_active, h, wg, wu, wd)


def _combine_kernel(pos1_ref, pos2_ref, y_ref, h_ref, p1_ref, p2_ref, g_ref, b_ref, out_ref,
                    a1, a2, b1, b2, sem, *, tm):
    i = pl.program_id(0)
    last = pl.num_programs(0) - 1

    @pl.when(i == 0)
    def _():
        def issue(t, carry):
            _row_copy(y_ref, pos1_ref[t], a1, t, sem.at[0]).start()
            _row_copy(y_ref, pos2_ref[t], a2, t, sem.at[0]).start()
            return carry

        lax.fori_loop(0, tm, issue, 0)

    def step(c1, c2, cur_sem, n1, n2, nxt_sem):
        _tile_wait(y_ref, c1, cur_sem)
        _tile_wait(y_ref, c2, cur_sem)
        base = jnp.minimum(i + 1, last) * tm
        for t in range(tm):
            _row_copy(y_ref, pos1_ref[base + t], n1, t, nxt_sem).start(priority=0)
            _row_copy(y_ref, pos2_ref[base + t], n2, t, nxt_sem).start(priority=DMA_QUEUES - 1)
        ffn = p1_ref[...] * _load_row_tiled(c1) + p2_ref[...] * _load_row_tiled(c2)
        out_ref[...] = _layer_norm(DN_ALPHA * h_ref[...] + ffn, g_ref[...], b_ref[...])

        @pl.when(i == last)
        def _():
            _tile_wait(y_ref, n1, nxt_sem)
            _tile_wait(y_ref, n2, nxt_sem)

    @pl.when(i % 2 == 0)
    def _():
        step(a1, a2, sem.at[0], b1, b2, sem.at[1])

    @pl.when(i % 2 == 1)
    def _():
        step(b1, b2, sem.at[1], a1, a2, sem.at[0])


def _combine(y, h, pos1, pos2, p1, p2, g, b, tm):
    n = h.shape[0]
    tok = lambda i, a, c: (i, 0)
    fixed = lambda i, a, c: (0, 0)
    return pl.pallas_call(
        functools.partial(_combine_kernel, tm=tm),
        grid_spec=pltpu.PrefetchScalarGridSpec(
            num_scalar_prefetch=2,
            grid=(n // tm,),
            in_specs=[pl.BlockSpec(memory_space=pl.ANY),
                      pl.BlockSpec((tm, D_MODEL), tok),
                      pl.BlockSpec((tm, 1), tok), pl.BlockSpec((tm, 1), tok),
                      pl.BlockSpec((1, D_MODEL), fixed), pl.BlockSpec((1, D_MODEL), fixed)],
            out_specs=pl.BlockSpec((tm, D_MODEL), tok),
            scratch_shapes=[pltpu.VMEM((tm * ROW_SPLIT, LANES), F32)] * 4 + [pltpu.SemaphoreType.DMA((2,))]),
        out_shape=jax.ShapeDtypeStruct((n, D_MODEL), F32),
        compiler_params=_cparams("arbitrary"),
        name="moe_combine_ln",
    )(pos1, pos2, y, h, p1, p2, g, b)


def _plan_kernel(ints_ref, cnt_ref, pos_ref, meta_ref, *, layer, chunk):
    tr = MOE_ROW_TILE
    n = ints_ref.shape[1]
    ne = MOE_EXPERTS
    tiles = (cnt_ref[...] + (tr - 1.0)) * (1.0 / tr)
    tiles = tiles.astype(I32).astype(F32)
    lower = lax.broadcasted_iota(I32, (ne, ne), 0) >= lax.broadcasted_iota(I32, (ne, ne), 1)
    ends = _dot(jnp.where(lower, 1.0, 0.0).astype(BF16), tiles.astype(BF16))
    start_col = ((ends - tiles) * tr).astype(I32)[:, 0:1]
    r8 = lax.broadcasted_iota(I32, (8, chunk), 0)
    re = lax.broadcasted_iota(I32, (ne, chunk), 0)
    for c in range(n // chunk):
        blk = ints_ref[:, c * chunk:(c + 1) * chunk]
        s1 = jnp.sum(jnp.where(re == blk[0:1], start_col, 0), 0, keepdims=True)
        s2 = jnp.sum(jnp.where(re == blk[1:2], start_col, 0), 0, keepdims=True)
        pos_ref[:, c * chunk:(c + 1) * chunk] = jnp.where(
            r8 == 0, s1 + blk[2:3], jnp.where(r8 == 1, s2 + blk[3:4], 0))
    width = meta_ref.shape[1]
    tile_id = lax.broadcasted_iota(I32, (ne, width), 1).astype(F32)
    te = jnp.sum(jnp.where(ends[:, 0:1] <= tile_id, 1, 0), 0, keepdims=True)
    te = jnp.minimum(te, ne - 1) + layer * ne
    n_used = ends[ne - 1:ne, 0:1].astype(I32)
    rm = lax.broadcasted_iota(I32, (8, width), 0)
    meta_ref[...] = jnp.where(rm == 0, te, jnp.where(rm == 1, n_used, 0))


def _plan(ints, cnt, layer, n_tiles):
    n = ints.shape[1]
    width = -(-n_tiles // LANES) * LANES
    return pl.pallas_call(
        functools.partial(_plan_kernel, layer=layer, chunk=2048),
        out_shape=[jax.ShapeDtypeStruct((8, n), I32), jax.ShapeDtypeStruct((8, width), I32)],
        compiler_params=pltpu.CompilerParams(vmem_limit_bytes=VMEM_LIMIT),
        name="moe_plan",
    )(ints, cnt)


def _moe_layer(h, h8, layer, group_w, group_b, expert_w, expert_b, gate_w, up_w, down_w, ln_g, ln_b):
    n = h.shape[0]
    ew = jnp.transpose(expert_w, (1, 0, 2)).reshape(D_MODEL, MOE_EXPERTS)
    w_r = jnp.zeros((D_MODEL, LANES), F32).at[:, 0:MOE_GROUPS].set(group_w).at[:, 8:8 + MOE_EXPERTS].set(ew)
    b_r = jnp.zeros((LANES,), F32).at[0:MOE_GROUPS].set(group_b).at[8:8 + MOE_EXPERTS].set(expert_b.reshape(-1))
    ints, flts, cnt = _router(h, w_r, b_r.reshape(LANES, 1), 512)
    tr = MOE_ROW_TILE
    n_tiles = (2 * n) // tr + MOE_EXPERTS
    pos, meta = _plan(ints, cnt, layer, n_tiles)
    pos1, pos2 = pos[0], pos[1]

    tok_ids = jnp.arange(n, dtype=I32)
    row_token = jnp.zeros((n_tiles * tr,), I32).at[jnp.concatenate([pos1, pos2])].set(
        jnp.concatenate([tok_ids, tok_ids]))
    y = _experts(h8, row_token, meta[0, :n_tiles], meta[1, :1],
                 gate_w.reshape(-1, D_MODEL, MOE_D_FF),
                 up_w.reshape(-1, D_MODEL, MOE_D_FF),
                 down_w.reshape(-1, MOE_D_FF, D_MODEL))
    return _combine(y, h, pos1, pos2, flts[0].reshape(n, 1), flts[1].reshape(n, 1),
                    ln_g.reshape(1, -1), ln_b.reshape(1, -1), 256)


def _pad_heads(w, axis):
    parts = []
    h0 = 0
    for _, _, nh in ATTN_GROUPS:
        sl = [slice(None)] * w.ndim
        sl[axis] = slice(h0 * ATTN_HEAD_DIM, (h0 + nh) * ATTN_HEAD_DIM)
        part = w[tuple(sl)]
        pad = [(0, 0)] * w.ndim
        pad[axis] = (0, ATTN_SLAB - nh * ATTN_HEAD_DIM)
        parts.append(jnp.pad(part, pad))
        h0 += nh
    return parts


def _ssd_layer(h, in_w, conv_w, conv_b, dt_bias, a_log, d_skip, norm_w, out_w, ln_g, ln_b, bsz, seq):
    in_b = in_w.astype(BF16)
    z = _matmul(h, in_b[:, :SSM_D_INNER], F32, 1024, 1024)
    xbc = _matmul(h, in_b[:, SSM_D_INNER:SSM_D_INNER + SSM_CONV_DIM], F32, 1024, 1024)
    dt_w = jnp.pad(in_w[:, SSM_D_INNER + SSM_CONV_DIM:], ((0, 0), (0, LANES - SSM_HEADS)))
    dt_raw = _matmul_f32ish(h, dt_w, 512)
    pad32 = lambda v: jnp.pad(v, (0, LANES - SSM_HEADS)).reshape(1, LANES)
    y = _ssd(z, xbc, dt_raw, conv_w, conv_b.reshape(1, -1), pad32(dt_bias), pad32(a_log),
             jnp.repeat(d_skip, SSM_HEAD_DIM).reshape(1, -1), norm_w.reshape(1, -1), bsz, seq)
    return _matmul_res_ln(y, out_w.astype(BF16), h, ln_g.reshape(1, -1), ln_b.reshape(1, -1), 512)


def _attn_layer(h, kv_w, q_w, o_w, rel_bias, ln_g, ln_b, bsz, seq):
    width = ATTN_HEADS * ATTN_HEAD_DIM
    w_all = jnp.concatenate(_pad_heads(q_w, 1) + _pad_heads(kv_w[:, :width], 1)
                            + _pad_heads(kv_w[:, width:], 1), axis=1).astype(BF16)
    qkv = _matmul(h, w_all, BF16, 1024, 3 * ATTN_SLAB)
    outs, lses = [], []
    h0 = 0
    for gi, (_, dil, nh) in enumerate(ATTN_GROUPS):
        bias_p, bias_c = _group_bias(rel_bias, h0, nh, dil)
        o, lse = _attention_group(qkv, bias_p, bias_c, gi, dil, nh, bsz, seq)
        outs.append(o)
        lses.append(lse)
        h0 += nh
    w_o = jnp.stack(_pad_heads(o_w, 0)).astype(BF16)
    return _attn_out(outs, lses, w_o, h, ln_g.reshape(1, -1), ln_b.reshape(1, -1), 512)


def kernel(x, ssm_in_w, ssm_conv_w, ssm_conv_b, ssm_dt_bias, ssm_a_log, ssm_d, ssm_norm_w, ssm_out_w,
           kv_w, attn_q_w, attn_o_w, rel_bias, moe_group_w, moe_group_b, moe_expert_w, moe_expert_b,
           moe_gate_w, moe_up_w, moe_down_w, ln_g, ln_b):
    bsz, seq, d = x.shape
    h = x.reshape(bsz * seq, d)
    n_ssd = DEPTH // 2
    for i in range(DEPTH):
        if i < n_ssd:
            h, h8 = _ssd_layer(h, ssm_in_w[i], ssm_conv_w[i], ssm_conv_b[i], ssm_dt_bias[i], ssm_a_log[i],
                               ssm_d[i], ssm_norm_w[i], ssm_out_w[i], ln_g[i, 0], ln_b[i, 0], bsz, seq)
        else:
            j = i - n_ssd
            h, h8 = _attn_layer(h, kv_w, attn_q_w[j], attn_o_w[j], rel_bias, ln_g[i, 0], ln_b[i, 0], bsz, seq)
        h = _moe_layer(h, h8, i, moe_group_w[i], moe_group_b[i], moe_expert_w[i], moe_expert_b[i],
                       moe_gate_w, moe_up_w, moe_down_w, ln_g[i, 1], ln_b[i, 1])
    return h.reshape(bsz, seq, d)
```

```python
import functools
import math

import numpy as np
import jax
import jax.numpy as jnp
from jax import lax
from jax.experimental import pallas as pl
from jax.experimental.pallas import tpu as pltpu

F32 = jnp.float32
BF16 = jnp.bfloat16
I32 = jnp.int32

D_MODEL = 1024
DEPTH = 2
DN_ALPHA = (2 * DEPTH) ** 0.25
LN_EPS = 1e-5

SSM_D_INNER = 2048
SSM_HEAD_DIM = 64
SSM_HEADS = 32
SSM_GROUPS = 4
SSM_STATE = 128
SSM_CONV = 4
SSM_CHUNK = 128
SSM_CONV_DIM = SSM_D_INNER + 2 * SSM_GROUPS * SSM_STATE

ATTN_HEAD_DIM = 64
ATTN_GROUPS = ((128, 1, 6), (512, 4, 5), (2048, 16, 5))
ATTN_HEADS = 16
ATTN_BLOCK = 128
N_BUCKETS = 32
MAX_DISTANCE = 2048
ATTN_SLAB = 384
NEG_BIG = -1e30

MOE_GROUPS = 4
MOE_EPG = 8
MOE_EXPERTS = MOE_GROUPS * MOE_EPG
MOE_D_FF = 512
MOE_ROW_TILE = 256

LANES = 128
DMA_QUEUES = 2
VMEM_LIMIT = 48 * 1024 * 1024
EXPERTS_VMEM_LIMIT = 56 * 1024 * 1024


def _cparams(*sem):
    return pltpu.CompilerParams(dimension_semantics=sem, vmem_limit_bytes=VMEM_LIMIT)


def _layer_norm(x, g, b):
    mu = jnp.mean(x, -1, keepdims=True)
    xc = x - mu
    var = jnp.mean(xc * xc, -1, keepdims=True)
    return xc * lax.rsqrt(var + LN_EPS) * g + b


def _split2(x):
    hi = x.astype(BF16)
    lo = (x - hi.astype(F32)).astype(BF16)
    return hi, lo


def _dot(a, b):
    return jnp.dot(a, b, preferred_element_type=F32)


def _dot_f32ish(a, b):
    ah, al = _split2(a)
    bh, bl = _split2(b)
    return _dot(ah, bh) + _dot(al, bh) + _dot(ah, bl)


def _mm_kernel(a_ref, b_ref, o_ref):
    o_ref[...] = _dot(a_ref[...].astype(BF16), b_ref[...]).astype(o_ref.dtype)


def _matmul(a, b, out_dtype, tm, tn):
    m, k = a.shape
    nc = b.shape[1]
    return pl.pallas_call(
        _mm_kernel,
        grid=(m // tm, nc // tn),
        in_specs=[pl.BlockSpec((tm, k), lambda i, j: (i, 0)),
                  pl.BlockSpec((k, tn), lambda i, j: (0, j))],
        out_specs=pl.BlockSpec((tm, tn), lambda i, j: (i, j)),
        out_shape=jax.ShapeDtypeStruct((m, nc), out_dtype),
        compiler_params=_cparams("parallel", "parallel"),
        name="matmul",
    )(a, b)


def _mm3_kernel(a_ref, b_ref, o_ref):
    o_ref[...] = _dot_f32ish(a_ref[...], b_ref[...])


def _matmul_f32ish(a, b, tm):
    m, k = a.shape
    nc = b.shape[1]
    return pl.pallas_call(
        _mm3_kernel,
        grid=(m // tm,),
        in_specs=[pl.BlockSpec((tm, k), lambda i: (i, 0)),
                  pl.BlockSpec((k, nc), lambda i: (0, 0))],
        out_specs=pl.BlockSpec((tm, nc), lambda i: (i, 0)),
        out_shape=jax.ShapeDtypeStruct((m, nc), F32),
        compiler_params=_cparams("parallel"),
        name="matmul_f32ish",
    )(a, b)


ROW_SPLIT = D_MODEL // LANES


def _store_row_tiled(ref, val):
    rows = val.shape[0]
    for c in range(ROW_SPLIT):
        ref[pl.ds(c, rows, stride=ROW_SPLIT), :] = val[:, c * LANES:(c + 1) * LANES]


def _load_row_tiled(ref):
    rows = ref.shape[0] // ROW_SPLIT
    return jnp.concatenate([ref[pl.ds(c, rows, stride=ROW_SPLIT), :] for c in range(ROW_SPLIT)], axis=1)


HALF = D_MODEL // 2
U32 = jnp.uint32
HI16 = 0xFFFF0000


def _pack_bf16_pairs(x):
    bits = lax.bitcast_convert_type(x.astype(BF16).astype(F32), U32)
    return (bits[:, :HALF] >> 16) | (bits[:, HALF:] & U32(HI16))


def _unpack_bf16_pairs(w):
    lo = lax.bitcast_convert_type(w << 16, F32).astype(BF16)
    hi = lax.bitcast_convert_type(w & U32(HI16), F32).astype(BF16)
    return lo, hi


def _mm_ln_kernel(a_ref, w_ref, h_ref, g_ref, b_ref, o_ref, op_ref):
    acc = _dot(a_ref[...], w_ref[...])
    out = _layer_norm(DN_ALPHA * h_ref[...] + acc, g_ref[...], b_ref[...])
    o_ref[...] = out
    op_ref[...] = _pack_bf16_pairs(out)


def _matmul_res_ln(a, w, h, g, b, tm):
    m, k = a.shape
    d = w.shape[1]
    return pl.pallas_call(
        _mm_ln_kernel,
        grid=(m // tm,),
        in_specs=[pl.BlockSpec((tm, k), lambda i: (i, 0)),
                  pl.BlockSpec((k, d), lambda i: (0, 0)),
                  pl.BlockSpec((tm, d), lambda i: (i, 0)),
                  pl.BlockSpec((1, d), lambda i: (0, 0)),
                  pl.BlockSpec((1, d), lambda i: (0, 0))],
        out_specs=[pl.BlockSpec((tm, d), lambda i: (i, 0)),
                   pl.BlockSpec((tm, HALF), lambda i: (i, 0))],
        out_shape=[jax.ShapeDtypeStruct((m, d), F32),
                   jax.ShapeDtypeStruct((m, HALF), U32)],
        compiler_params=_cparams("parallel"),
        name="matmul_res_ln",
    )(a, w, h, g, b)


def _ssd_kernel(z_ref, xbc_ref, dt_ref, cw_ref, cb_ref, dtb_ref, alog_ref, dsk_ref, nw_ref,
                y_ref, xe_ref, st_ref):
    q = SSM_CHUNK

    @pl.when(pl.program_id(1) == 0)
    def _():
        xe_ref[0:8, :] = jnp.zeros((8, SSM_CONV_DIM), F32)
        st_ref[...] = jnp.zeros_like(st_ref)

    u = xbc_ref[...]
    xe_ref[8:8 + q, :] = u
    w = cw_ref[...]
    conv = (cb_ref[...] + w[3:4] * u + w[2:3] * xe_ref[7:7 + q, :]
            + w[1:2] * xe_ref[6:6 + q, :] + w[0:1] * xe_ref[5:5 + q, :])
    xe_ref[0:8, :] = xe_ref[q:q + 8, :]
    act = conv * jax.nn.sigmoid(conv)

    pre = dt_ref[...] + dtb_ref[...]
    dt = jnp.maximum(pre, 0.0) + jnp.log(1.0 + jnp.exp(-jnp.abs(pre)))
    adt = dt * (-jnp.exp(alog_ref[...]))

    row = lax.broadcasted_iota(I32, (q, q), 0)
    col = lax.broadcasted_iota(I32, (q, q), 1)
    tril = row >= col
    tri_b = jnp.where(tril, 1.0, 0.0).astype(BF16)
    a_hi = adt.astype(BF16)
    r1 = adt - a_hi.astype(F32)
    a_mid = r1.astype(BF16)
    a_lo = (r1 - a_mid.astype(F32)).astype(BF16)
    acs = _dot(tri_b, a_hi) + _dot(tri_b, a_mid) + _dot(tri_b, a_lo)
    acs_t = acs.T
    eacs = jnp.exp(acs)
    left = col < SSM_HEAD_DIM

    for g in range(SSM_GROUPS):
        b0 = SSM_D_INNER + g * SSM_STATE
        c0 = SSM_D_INNER + SSM_GROUPS * SSM_STATE + g * SSM_STATE
        bm = act[:, b0:b0 + SSM_STATE]
        cm = act[:, c0:c0 + SSM_STATE].astype(BF16)
        cb = lax.dot_general(cm, bm.astype(BF16), (((1,), (1,)), ((), ())),
                             preferred_element_type=F32)
        bm_t = bm.T
        gs = g * 512
        y_off = _dot(cm, st_ref[:, gs:gs + 512].astype(BF16))
        slabs = []
        for pr in range(4):
            ha = g * 8 + pr * 2
            hb = ha + 1
            cs = gs + pr * LANES
            x2 = act[:, cs:cs + LANES]
            dt2 = jnp.where(left, dt[:, ha:ha + 1], dt[:, hb:hb + 1])
            xdt = (x2 * dt2).astype(BF16)
            ys, ups = [], []
            for h in (ha, hb):
                a_col = acs[:, h:h + 1]
                a_row = acs_t[h:h + 1, :]
                decay = jnp.where(tril, jnp.exp(a_col - a_row), 0.0)
                ys.append(_dot((cb * decay).astype(BF16), xdt))
                to_end = jnp.exp(a_row[:, q - 1:q] - a_row)
                ups.append(_dot((bm_t * to_end).astype(BF16), xdt))
            y_diag = jnp.where(left, ys[0], ys[1])
            upd = jnp.where(left, ups[0], ups[1])
            e2 = jnp.where(left, eacs[:, ha:ha + 1], eacs[:, hb:hb + 1])
            cd = jnp.where(left[0:1, :], eacs[q - 1:q, ha:ha + 1], eacs[q - 1:q, hb:hb + 1])
            y2 = y_diag + y_off[:, pr * LANES:(pr + 1) * LANES] * e2 + dsk_ref[:, cs:cs + LANES] * x2
            st_ref[:, cs:cs + LANES] = st_ref[:, cs:cs + LANES] * cd + upd
            slabs.append(y2)
        yg = jnp.concatenate(slabs, axis=1)
        zg = z_ref[:, gs:gs + 512]
        yg = yg * (zg * jax.nn.sigmoid(zg))
        ms = jnp.mean(yg * yg, -1, keepdims=True)
        y_ref[:, gs:gs + 512] = (yg * lax.rsqrt(ms + LN_EPS) * nw_ref[:, gs:gs + 512]).astype(y_ref.dtype)


def _ssd(z, xbc, dt_raw, conv_w, conv_b, dt_bias, a_log, d_rep, norm_w, bsz, seq):
    n = z.shape[0]
    nchunk = seq // SSM_CHUNK
    q = SSM_CHUNK
    tok = lambda b, c: (b * nchunk + c, 0)
    fixed = lambda b, c: (0, 0)
    return pl.pallas_call(
        _ssd_kernel,
        grid=(bsz, nchunk),
        in_specs=[pl.BlockSpec((q, SSM_D_INNER), tok),
                  pl.BlockSpec((q, SSM_CONV_DIM), tok),
                  pl.BlockSpec((q, LANES), tok),
                  pl.BlockSpec((SSM_CONV, SSM_CONV_DIM), fixed),
                  pl.BlockSpec((1, SSM_CONV_DIM), fixed),
                  pl.BlockSpec((1, LANES), fixed),
                  pl.BlockSpec((1, LANES), fixed),
                  pl.BlockSpec((1, SSM_D_INNER), fixed),
                  pl.BlockSpec((1, SSM_D_INNER), fixed)],
        out_specs=pl.BlockSpec((q, SSM_D_INNER), tok),
        out_shape=jax.ShapeDtypeStruct((n, SSM_D_INNER), BF16),
        scratch_shapes=[pltpu.VMEM((q + 8, SSM_CONV_DIM), F32),
                        pltpu.VMEM((SSM_STATE, SSM_D_INNER), F32)],
        compiler_params=_cparams("parallel", "arbitrary"),
        name="ssd_chunk",
    )(z, xbc, dt_raw, conv_w, conv_b, dt_bias, a_log, d_rep, norm_w)


def _attn_kernel(q_ref, kp_ref, kc_ref, vp_ref, vc_ref, bp_ref, bc_ref, o_ref, lse_ref, *, nh):
    has_prev = pl.program_id(1) > 0
    qb = q_ref[0]
    kp, kc, vp, vc = kp_ref[0], kc_ref[0], vp_ref[0], vc_ref[0]
    lane = lax.broadcasted_iota(I32, (ATTN_BLOCK, LANES), 1)
    left = lane < ATTN_HEAD_DIM
    zero = jnp.zeros((), BF16)
    nt = (((1,), (1,)), ((), ()))
    lse_sum = jnp.zeros((ATTN_BLOCK, 1), F32)
    outs = []
    for pr in range(ATTN_SLAB // LANES):
        sl = slice(pr * LANES, (pr + 1) * LANES)
        q2, kp2, kc2, vp2, vc2 = qb[:, sl], kp[:, sl], kc[:, sl], vp[:, sl], vc[:, sl]
        halves = []
        for side in range(2):
            hh = pr * 2 + side
            if hh >= nh:
                halves.append(jnp.zeros((ATTN_BLOCK, LANES), F32))
                continue
            qm = jnp.where(left if side == 0 else ~left, q2, zero)
            s_p = lax.dot_general(qm, kp2, nt, preferred_element_type=F32) * (ATTN_HEAD_DIM ** -0.5)
            s_c = lax.dot_general(qm, kc2, nt, preferred_element_type=F32) * (ATTN_HEAD_DIM ** -0.5)
            s_p = jnp.where(has_prev, s_p + bp_ref[hh], NEG_BIG)
            s_c = s_c + bc_ref[hh]
            m = jnp.maximum(jnp.max(s_p, -1, keepdims=True), jnp.max(s_c, -1, keepdims=True))
            p_p = jnp.exp(s_p - m)
            p_c = jnp.exp(s_c - m)
            den = jnp.sum(p_p, -1, keepdims=True) + jnp.sum(p_c, -1, keepdims=True)
            o = _dot(p_p.astype(BF16), vp2) + _dot(p_c.astype(BF16), vc2)
            halves.append(o / den)
            lse_sum = lse_sum + (m + jnp.log(den))
        outs.append(jnp.where(left, halves[0], halves[1]))
    o_ref[0] = jnp.concatenate(outs, axis=1)
    lse_ref[0] = jnp.broadcast_to(lse_sum * (1.0 / nh), (ATTN_BLOCK, LANES))


def _attention_group(qkv, bias_p, bias_c, gi, dil, nh, bsz, seq):
    l = seq // dil
    nb = l // ATTN_BLOCK
    nslab = qkv.shape[1] // ATTN_SLAB
    view = qkv.reshape(bsz, l, dil * qkv.shape[1])
    blk = (1, ATTN_BLOCK, ATTN_SLAB)

    def cur(which):
        return lambda b, n, r: (b, n, r * nslab + which * 3 + gi)

    def prev(which):
        return lambda b, n, r: (b, jnp.maximum(n - 1, 0), r * nslab + which * 3 + gi)

    fixed = lambda b, n, r: (0, 0, 0)
    o, lse = pl.pallas_call(
        functools.partial(_attn_kernel, nh=nh),
        grid=(bsz, nb, dil),
        in_specs=[pl.BlockSpec(blk, cur(0)),
                  pl.BlockSpec(blk, prev(1)), pl.BlockSpec(blk, cur(1)),
                  pl.BlockSpec(blk, prev(2)), pl.BlockSpec(blk, cur(2)),
                  pl.BlockSpec((nh, ATTN_BLOCK, ATTN_BLOCK), fixed),
                  pl.BlockSpec((nh, ATTN_BLOCK, ATTN_BLOCK), fixed)],
        out_specs=[pl.BlockSpec(blk, lambda b, n, r: (b, n, r)),
                   pl.BlockSpec((1, ATTN_BLOCK, LANES), lambda b, n, r: (b, n, r))],
        out_shape=[jax.ShapeDtypeStruct((bsz, l, dil * ATTN_SLAB), F32),
                   jax.ShapeDtypeStruct((bsz, l, dil * LANES), F32)],
        compiler_params=_cparams("parallel", "parallel", "parallel"),
        name=f"dilated_attn_g{gi}",
    )(view, view, view, view, view, bias_p, bias_c)
    return o.reshape(bsz * seq, ATTN_SLAB), lse.reshape(bsz * seq, LANES)


def _attn_out_kernel(o0_ref, o1_ref, o2_ref, l0_ref, l1_ref, l2_ref, w_ref, h_ref, g_ref, b_ref,
                     out_ref, outp_ref):
    ls = [r[:, 0:1] for r in (l0_ref, l1_ref, l2_ref)]
    m = jnp.maximum(jnp.maximum(ls[0], ls[1]), ls[2])
    es = [jnp.exp(v - m) for v in ls]
    scale = len(ATTN_GROUPS) / (es[0] + es[1] + es[2])
    acc = None
    for gi, o_ref in enumerate((o0_ref, o1_ref, o2_ref)):
        part = _dot((o_ref[...] * (es[gi] * scale)).astype(BF16), w_ref[gi])
        acc = part if acc is None else acc + part
    out = _layer_norm(DN_ALPHA * h_ref[...] + acc, g_ref[...], b_ref[...])
    out_ref[...] = out
    outp_ref[...] = _pack_bf16_pairs(out)


def _attn_out(outs, lses, w, h, g, b, tm):
    n = h.shape[0]
    tok = lambda i: (i, 0)
    fixed2 = lambda i: (0, 0)
    return pl.pallas_call(
        _attn_out_kernel,
        grid=(n // tm,),
        in_specs=[pl.BlockSpec((tm, ATTN_SLAB), tok)] * 3 + [pl.BlockSpec((tm, LANES), tok)] * 3
                 + [pl.BlockSpec((3, ATTN_SLAB, D_MODEL), lambda i: (0, 0, 0)),
                    pl.BlockSpec((tm, D_MODEL), tok),
                    pl.BlockSpec((1, D_MODEL), fixed2), pl.BlockSpec((1, D_MODEL), fixed2)],
        out_specs=[pl.BlockSpec((tm, D_MODEL), tok), pl.BlockSpec((tm, HALF), tok)],
        out_shape=[jax.ShapeDtypeStruct((n, D_MODEL), F32),
                   jax.ShapeDtypeStruct((n, HALF), U32)],
        compiler_params=_cparams("parallel"),
        name="attn_out_ln",
    )(*outs, *lses, w, h, g, b)


def _t5_bucket(dist):
    max_exact = N_BUCKETS // 2
    n = np.maximum(dist, 1).astype(np.float64)
    large = max_exact + (np.log(n / max_exact) / np.log(MAX_DISTANCE / max_exact)
                         * (N_BUCKETS - max_exact)).astype(np.int32)
    large = np.minimum(large, N_BUCKETS - 1)
    return np.where(dist < max_exact, dist, large).astype(np.int32)


def _group_bias(rel_bias, h0, nh, dil):
    qi = np.arange(ATTN_BLOCK)[:, None]
    ki = np.arange(ATTN_BLOCK)[None, :]
    tabs = []
    for delta, band in ((qi + ATTN_BLOCK - ki, ki >= qi), (qi - ki, ki <= qi)):
        bucket = _t5_bucket(np.clip(delta, 0, None) * dil)
        onehot = (bucket[..., None] == np.arange(N_BUCKETS)).astype(np.float32)
        t = jnp.einsum("qkb,bh->hqk", onehot, rel_bias[:, h0:h0 + nh], precision=lax.Precision.HIGHEST)
        tabs.append(jnp.where(band[None], t, NEG_BIG).astype(F32))
    return tabs


def _router_kernel(h_ref, w_ref, b_ref, ints_ref, flts_ref, cnt_ref, carry_ref, *, tm):
    @pl.when(pl.program_id(0) == 0)
    def _():
        carry_ref[...] = jnp.zeros_like(carry_ref)

    lt = _dot_f32ish(h_ref[...], w_ref[...]).T + b_ref[...]
    gl = lt[0:MOE_GROUPS]
    r4 = lax.broadcasted_iota(I32, (MOE_GROUPS, tm), 0)
    gmax = jnp.max(gl, 0, keepdims=True)
    gidx = jnp.min(jnp.where(gl == gmax, r4, MOE_GROUPS), 0, keepdims=True)
    gval = 1.0 / jnp.sum(jnp.exp(gl - gmax), 0, keepdims=True)

    esel = jnp.zeros((MOE_EPG, tm), F32)
    for g in range(MOE_GROUPS):
        esel = jnp.where(gidx == g, lt[8 + g * MOE_EPG:8 + (g + 1) * MOE_EPG], esel)
    r8 = lax.broadcasted_iota(I32, (MOE_EPG, tm), 0)
    v1 = jnp.max(esel, 0, keepdims=True)
    i1 = jnp.min(jnp.where(esel == v1, r8, MOE_EPG), 0, keepdims=True)
    rest = jnp.where(r8 == i1, -jnp.inf, esel)
    v2 = jnp.max(rest, 0, keepdims=True)
    i2 = jnp.min(jnp.where(rest == v2, r8, MOE_EPG), 0, keepdims=True)
    t = jnp.exp(v2 - v1)
    p1 = gval / (1.0 + t)
    p2 = p1 * t
    e1 = gidx * MOE_EPG + i1
    e2 = gidx * MOE_EPG + i2

    r32 = lax.broadcasted_iota(I32, (MOE_EXPERTS, tm), 0)
    oh1 = r32 == e1
    oh2 = r32 == e2
    oh = jnp.where(oh1 | oh2, 1.0, 0.0)
    tr = lax.broadcasted_iota(I32, (tm, tm), 0)
    tc = lax.broadcasted_iota(I32, (tm, tm), 1)
    before = jnp.where(tr < tc, 1.0, 0.0).astype(BF16)
    base = _dot(oh.astype(BF16), before) + carry_ref[:, 0:1]
    rank1 = jnp.sum(jnp.where(oh1, base, 0.0), 0, keepdims=True).astype(I32)
    rank2 = jnp.sum(jnp.where(oh2, base, 0.0), 0, keepdims=True).astype(I32)
    carry_ref[...] = carry_ref[...] + jnp.sum(oh, 1, keepdims=True)
    cnt_ref[...] = carry_ref[...]

    ints_ref[...] = jnp.where(r8 == 0, e1, jnp.where(r8 == 1, e2, jnp.where(r8 == 2, rank1,
                              jnp.where(r8 == 3, rank2, 0))))
    flts_ref[...] = jnp.where(r8 == 0, p1, jnp.where(r8 == 1, p2, 0.0))


def _router(h, w_r, b_r, tm):
    n = h.shape[0]
    return pl.pallas_call(
        functools.partial(_router_kernel, tm=tm),
        grid=(n // tm,),
        in_specs=[pl.BlockSpec((tm, D_MODEL), lambda i: (i, 0)),
                  pl.BlockSpec((D_MODEL, LANES), lambda i: (0, 0)),
                  pl.BlockSpec((LANES, 1), lambda i: (0, 0))],
        out_specs=[pl.BlockSpec((8, tm), lambda i: (0, i)),
                   pl.BlockSpec((8, tm), lambda i: (0, i)),
                   pl.BlockSpec((MOE_EXPERTS, LANES), lambda i: (0, 0))],
        out_shape=[jax.ShapeDtypeStruct((8, n), I32),
                   jax.ShapeDtypeStruct((8, n), F32),
                   jax.ShapeDtypeStruct((MOE_EXPERTS, LANES), F32)],
        scratch_shapes=[pltpu.VMEM((MOE_EXPERTS, LANES), F32)],
        compiler_params=_cparams("arbitrary"),
        name="moe_router",
    )(h, w_r, b_r)


def _row_copy(src_ref, s, dst_ref, d, sem):
    def first(r):
        return r * ROW_SPLIT if isinstance(r, int) else pl.multiple_of(r * ROW_SPLIT, ROW_SPLIT)

    return pltpu.make_async_copy(src_ref.at[pl.ds(first(s), ROW_SPLIT), :],
                                 dst_ref.at[pl.ds(first(d), ROW_SPLIT), :], sem)


def _tile_wait(src_ref, dst_ref, sem):
    pltpu.make_async_copy(src_ref.at[pl.ds(0, dst_ref.shape[0]), :], dst_ref, sem).wait()


def _experts_kernel(pos1_ref, pos2_ref, te_ref, na_ref, hp_ref, wg_ref, wu_ref, wd_ref, y_ref,
                    xa, xb, wg_s, wu_s, wd_s, row_tok):
    i = pl.program_id(0)
    n_tiles = pl.num_programs(0)
    tr = MOE_ROW_TILE
    active = i < na_ref[0]
    changed = jnp.logical_or(i == 0, te_ref[i] != te_ref[jnp.maximum(i - 1, 0)])

    @pl.when(i == 0)
    def _():
        def clear(r, carry):
            row_tok[r] = 0
            return carry

        lax.fori_loop(0, row_tok.shape[0], clear, 0, unroll=8)

        def invert(t, carry):
            row_tok[pos1_ref[t]] = t
            row_tok[pos2_ref[t]] = t
            return carry

        lax.fori_loop(0, pos1_ref.shape[0], invert, 0, unroll=8)

        def pick(r, carry):
            xa[pl.ds(r, 1), :] = hp_ref[pl.ds(row_tok[r], 1), :]
            return carry

        lax.fori_loop(0, tr, pick, 0, unroll=8)

    @pl.when(jnp.logical_and(active, changed))
    def _():
        wg_s[...] = wg_ref[0].astype(BF16)
        wu_s[...] = wu_ref[0].astype(BF16)
        wd_s[...] = wd_ref[0].astype(BF16)

    def step(cur, nxt):
        base = jnp.minimum(i + 1, n_tiles - 1) * tr
        for r in range(tr):
            nxt[r:r + 1, :] = hp_ref[pl.ds(row_tok[base + r], 1), :]
        lo, hi = _unpack_bf16_pairs(cur[...])
        gate = _dot(lo, wg_s[0:HALF, :]) + _dot(hi, wg_s[HALF:, :])
        up = _dot(lo, wu_s[0:HALF, :]) + _dot(hi, wu_s[HALF:, :])
        hid = (gate * jax.nn.sigmoid(gate) * up).astype(BF16)
        _store_row_tiled(y_ref, _dot(hid, wd_s[...]))

    even = i % 2 == 0

    @pl.when(jnp.logical_and(active, even))
    def _():
        step(xa, xb)

    @pl.when(jnp.logical_and(active, jnp.logical_not(even)))
    def _():
        step(xb, xa)

    @pl.when(jnp.logical_not(active))
    def _():
        y_ref[...] = jnp.zeros_like(y_ref)


def _experts(hp, pos1, pos2, tile_expert, n_active, wg, wu, wd):
    tr = MOE_ROW_TILE
    n_tiles = tile_expert.shape[0]
    rows = n_tiles * tr
    wmap = lambda i, p1, p2, te, na: (te[i], 0, 0)
    return pl.pallas_call(
        _experts_kernel,
        grid_spec=pltpu.PrefetchScalarGridSpec(
            num_scalar_prefetch=4,
            grid=(n_tiles,),
            in_specs=[pl.BlockSpec(memory_space=pltpu.VMEM),
                      pl.BlockSpec((1, D_MODEL, MOE_D_FF), wmap),
                      pl.BlockSpec((1, D_MODEL, MOE_D_FF), wmap),
                      pl.BlockSpec((1, MOE_D_FF, D_MODEL), wmap)],
            out_specs=pl.BlockSpec((tr * ROW_SPLIT, LANES), lambda i, p1, p2, te, na: (i, 0)),
            scratch_shapes=[pltpu.VMEM((tr, HALF), U32),
                            pltpu.VMEM((tr, HALF), U32),
                            pltpu.VMEM((D_MODEL, MOE_D_FF), BF16),
                            pltpu.VMEM((D_MODEL, MOE_D_FF), BF16),
                            pltpu.VMEM((MOE_D_FF, D_MODEL), BF16),
                            pltpu.SMEM((rows,), I32)]),
        out_shape=jax.ShapeDtypeStruct((rows * ROW_SPLIT, LANES), F32),
        compiler_params=pltpu.CompilerParams(dimension_semantics=("arbitrary",),
                                             vmem_limit_bytes=EXPERTS_VMEM_LIMIT),
        name="moe_experts",
    )(pos1, pos2, tile_expert, n_active, hp, wg, wu, wd)


def _combine_kernel(pos1_ref, pos2_ref, y_ref, h_ref, p1_ref, p2_ref, g_ref, b_ref, out_ref,
                    a1, a2, b1, b2, sem, *, tm):
    i = pl.program_id(0)
    last = pl.num_programs(0) - 1

    @pl.when(i == 0)
    def _():
        def issue(t, carry):
            _row_copy(y_ref, pos1_ref[t], a1, t, sem.at[0]).start()
            _row_copy(y_ref, pos2_ref[t], a2, t, sem.at[0]).start()
            return carry

        lax.fori_loop(0, tm, issue, 0)

    def step(c1, c2, cur_sem, n1, n2, nxt_sem):
        _tile_wait(y_ref, c1, cur_sem)
        _tile_wait(y_ref, c2, cur_sem)
        base = jnp.minimum(i + 1, last) * tm
        for t in range(tm):
            _row_copy(y_ref, pos1_ref[base + t], n1, t, nxt_sem).start(priority=0)
            _row_copy(y_ref, pos2_ref[base + t], n2, t, nxt_sem).start(priority=DMA_QUEUES - 1)
        ffn = p1_ref[...] * _load_row_tiled(c1) + p2_ref[...] * _load_row_tiled(c2)
        out_ref[...] = _layer_norm(DN_ALPHA * h_ref[...] + ffn, g_ref[...], b_ref[...])

        @pl.when(i == last)
        def _():
            _tile_wait(y_ref, n1, nxt_sem)
            _tile_wait(y_ref, n2, nxt_sem)

    @pl.when(i % 2 == 0)
    def _():
        step(a1, a2, sem.at[0], b1, b2, sem.at[1])

    @pl.when(i % 2 == 1)
    def _():
        step(b1, b2, sem.at[1], a1, a2, sem.at[0])


def _combine(y, h, pos1, pos2, p1, p2, g, b, tm):
    n = h.shape[0]
    tok = lambda i, a, c: (i, 0)
    fixed = lambda i, a, c: (0, 0)
    return pl.pallas_call(
        functools.partial(_combine_kernel, tm=tm),
        grid_spec=pltpu.PrefetchScalarGridSpec(
            num_scalar_prefetch=2,
            grid=(n // tm,),
            in_specs=[pl.BlockSpec(memory_space=pl.ANY),
                      pl.BlockSpec((tm, D_MODEL), tok),
                      pl.BlockSpec((tm, 1), tok), pl.BlockSpec((tm, 1), tok),
                      pl.BlockSpec((1, D_MODEL), fixed), pl.BlockSpec((1, D_MODEL), fixed)],
            out_specs=pl.BlockSpec((tm, D_MODEL), tok),
            scratch_shapes=[pltpu.VMEM((tm * ROW_SPLIT, LANES), F32)] * 4 + [pltpu.SemaphoreType.DMA((2,))]),
        out_shape=jax.ShapeDtypeStruct((n, D_MODEL), F32),
        compiler_params=_cparams("arbitrary"),
        name="moe_combine_ln",
    )(pos1, pos2, y, h, p1, p2, g, b)


def _plan_kernel(ints_ref, cnt_ref, pos_ref, meta_ref, *, layer, chunk):
    tr = MOE_ROW_TILE
    n = ints_ref.shape[1]
    ne = MOE_EXPERTS
    tiles = (cnt_ref[...] + (tr - 1.0)) * (1.0 / tr)
    tiles = tiles.astype(I32).astype(F32)
    lower = lax.broadcasted_iota(I32, (ne, ne), 0) >= lax.broadcasted_iota(I32, (ne, ne), 1)
    ends = _dot(jnp.where(lower, 1.0, 0.0).astype(BF16), tiles.astype(BF16))
    start_col = ((ends - tiles) * tr).astype(I32)[:, 0:1]
    r8 = lax.broadcasted_iota(I32, (8, chunk), 0)
    re = lax.broadcasted_iota(I32, (ne, chunk), 0)
    for c in range(n // chunk):
        blk = ints_ref[:, c * chunk:(c + 1) * chunk]
        s1 = jnp.sum(jnp.where(re == blk[0:1], start_col, 0), 0, keepdims=True)
        s2 = jnp.sum(jnp.where(re == blk[1:2], start_col, 0), 0, keepdims=True)
        pos_ref[:, c * chunk:(c + 1) * chunk] = jnp.where(
            r8 == 0, s1 + blk[2:3], jnp.where(r8 == 1, s2 + blk[3:4], 0))
    width = meta_ref.shape[1]
    tile_id = lax.broadcasted_iota(I32, (ne, width), 1).astype(F32)
    te = jnp.sum(jnp.where(ends[:, 0:1] <= tile_id, 1, 0), 0, keepdims=True)
    te = jnp.minimum(te, ne - 1) + layer * ne
    n_used = ends[ne - 1:ne, 0:1].astype(I32)
    rm = lax.broadcasted_iota(I32, (8, width), 0)
    meta_ref[...] = jnp.where(rm == 0, te, jnp.where(rm == 1, n_used, 0))


def _plan(ints, cnt, layer, n_tiles):
    n = ints.shape[1]
    width = -(-n_tiles // LANES) * LANES
    return pl.pallas_call(
        functools.partial(_plan_kernel, layer=layer, chunk=2048),
        out_shape=[jax.ShapeDtypeStruct((8, n), I32), jax.ShapeDtypeStruct((8, width), I32)],
        compiler_params=pltpu.CompilerParams(vmem_limit_bytes=VMEM_LIMIT),
        name="moe_plan",
    )(ints, cnt)


def _moe_layer(h, hp, layer, group_w, group_b, expert_w, expert_b, gate_w, up_w, down_w, ln_g, ln_b):
    n = h.shape[0]
    ew = jnp.transpose(expert_w, (1, 0, 2)).reshape(D_MODEL, MOE_EXPERTS)
    w_r = jnp.zeros((D_MODEL, LANES), F32).at[:, 0:MOE_GROUPS].set(group_w).at[:, 8:8 + MOE_EXPERTS].set(ew)
    b_r = jnp.zeros((LANES,), F32).at[0:MOE_GROUPS].set(group_b).at[8:8 + MOE_EXPERTS].set(expert_b.reshape(-1))
    ints, flts, cnt = _router(h, w_r, b_r.reshape(LANES, 1), 512)
    tr = MOE_ROW_TILE
    n_tiles = (2 * n) // tr + MOE_EXPERTS
    pos, meta = _plan(ints, cnt, layer, n_tiles)
    pos1, pos2 = pos[0], pos[1]

    y = _experts(hp, pos1, pos2, meta[0, :n_tiles], meta[1, :1],
                 gate_w.reshape(-1, D_MODEL, MOE_D_FF),
                 up_w.reshape(-1, D_MODEL, MOE_D_FF),
                 down_w.reshape(-1, MOE_D_FF, D_MODEL))
    return _combine(y, h, pos1, pos2, flts[0].reshape(n, 1), flts[1].reshape(n, 1),
                    ln_g.reshape(1, -1), ln_b.reshape(1, -1), 256)


def _pad_heads(w, axis):
    parts = []
    h0 = 0
    for _, _, nh in ATTN_GROUPS:
        sl = [slice(None)] * w.ndim
        sl[axis] = slice(h0 * ATTN_HEAD_DIM, (h0 + nh) * ATTN_HEAD_DIM)
        part = w[tuple(sl)]
        pad = [(0, 0)] * w.ndim
        pad[axis] = (0, ATTN_SLAB - nh * ATTN_HEAD_DIM)
        parts.append(jnp.pad(part, pad))
        h0 += nh
    return parts


def _ssd_layer(h, in_w, conv_w, conv_b, dt_bias, a_log, d_skip, norm_w, out_w, ln_g, ln_b, bsz, seq):
    in_b = in_w.astype(BF16)
    z = _matmul(h, in_b[:, :SSM_D_INNER], F32, 1024, 1024)
    xbc = _matmul(h, in_b[:, SSM_D_INNER:SSM_D_INNER + SSM_CONV_DIM], F32, 1024, 1024)
    dt_w = jnp.pad(in_w[:, SSM_D_INNER + SSM_CONV_DIM:], ((0, 0), (0, LANES - SSM_HEADS)))
    dt_raw = _matmul_f32ish(h, dt_w, 512)
    pad32 = lambda v: jnp.pad(v, (0, LANES - SSM_HEADS)).reshape(1, LANES)
    y = _ssd(z, xbc, dt_raw, conv_w, conv_b.reshape(1, -1), pad32(dt_bias), pad32(a_log),
             jnp.repeat(d_skip, SSM_HEAD_DIM).reshape(1, -1), norm_w.reshape(1, -1), bsz, seq)
    return _matmul_res_ln(y, out_w.astype(BF16), h, ln_g.reshape(1, -1), ln_b.reshape(1, -1), 512)


def _attn_layer(h, kv_w, q_w, o_w, rel_bias, ln_g, ln_b, bsz, seq):
    width = ATTN_HEADS * ATTN_HEAD_DIM
    w_all = jnp.concatenate(_pad_heads(q_w, 1) + _pad_heads(kv_w[:, :width], 1)
                            + _pad_heads(kv_w[:, width:], 1), axis=1).astype(BF16)
    qkv = _matmul(h, w_all, BF16, 1024, 3 * ATTN_SLAB)
    outs, lses = [], []
    h0 = 0
    for gi, (_, dil, nh) in enumerate(ATTN_GROUPS):
        bias_p, bias_c = _group_bias(rel_bias, h0, nh, dil)
        o, lse = _attention_group(qkv, bias_p, bias_c, gi, dil, nh, bsz, seq)
        outs.append(o)
        lses.append(lse)
        h0 += nh
    w_o = jnp.stack(_pad_heads(o_w, 0)).astype(BF16)
    return _attn_out(outs, lses, w_o, h, ln_g.reshape(1, -1), ln_b.reshape(1, -1), 512)


def kernel(x, ssm_in_w, ssm_conv_w, ssm_conv_b, ssm_dt_bias, ssm_a_log, ssm_d, ssm_norm_w, ssm_out_w,
           kv_w, attn_q_w, attn_o_w, rel_bias, moe_group_w, moe_group_b, moe_expert_w, moe_expert_b,
           moe_gate_w, moe_up_w, moe_down_w, ln_g, ln_b):
    bsz, seq, d = x.shape
    h = x.reshape(bsz * seq, d)
    n_ssd = DEPTH // 2
    for i in range(DEPTH):
        if i < n_ssd:
            h, hp = _ssd_layer(h, ssm_in_w[i], ssm_conv_w[i], ssm_conv_b[i], ssm_dt_bias[i], ssm_a_log[i],
                               ssm_d[i], ssm_norm_w[i], ssm_out_w[i], ln_g[i, 0], ln_b[i, 0], bsz, seq)
        else:
            j = i - n_ssd
            h, hp = _attn_layer(h, kv_w, attn_q_w[j], attn_o_w[j], rel_bias, ln_g[i, 0], ln_b[i, 0], bsz, seq)
        h = _moe_layer(h, hp, i, moe_group_w[i], moe_group_b[i], moe_expert_w[i], moe_expert_b[i],
                       moe_gate_w, moe_up_w, moe_down_w, ln_g[i, 1], ln_b[i, 1])
    return h.reshape(bsz, seq, d)
```

```python
import functools
import math

import numpy as np
import jax
import jax.numpy as jnp
from jax import lax
from jax.experimental import pallas as pl
from jax.experimental.pallas import tpu as pltpu

F32 = jnp.float32
BF16 = jnp.bfloat16
I32 = jnp.int32

D_MODEL = 1024
DEPTH = 2
DN_ALPHA = (2 * DEPTH) ** 0.25
LN_EPS = 1e-5

SSM_D_INNER = 2048
SSM_HEAD_DIM = 64
SSM_HEADS = 32
SSM_GROUPS = 4
SSM_STATE = 128
SSM_CONV = 4
SSM_CHUNK = 128
SSM_CONV_DIM = SSM_D_INNER + 2 * SSM_GROUPS * SSM_STATE

ATTN_HEAD_DIM = 64
ATTN_GROUPS = ((128, 1, 6), (512, 4, 5), (2048, 16, 5))
ATTN_HEADS = 16
ATTN_BLOCK = 128
N_BUCKETS = 32
MAX_DISTANCE = 2048
ATTN_SLAB = 384
NEG_BIG = -1e30

MOE_GROUPS = 4
MOE_EPG = 8
MOE_EXPERTS = MOE_GROUPS * MOE_EPG
MOE_D_FF = 512
MOE_ROW_TILE = 256

LANES = 128
DMA_QUEUES = 2
VMEM_LIMIT = 48 * 1024 * 1024
EXPERTS_VMEM_LIMIT = 56 * 1024 * 1024


def _cparams(*sem):
    return pltpu.CompilerParams(dimension_semantics=sem, vmem_limit_bytes=VMEM_LIMIT)


def _layer_norm(x, g, b):
    mu = jnp.mean(x, -1, keepdims=True)
    xc = x - mu
    var = jnp.mean(xc * xc, -1, keepdims=True)
    return xc * lax.rsqrt(var + LN_EPS) * g + b


def _split2(x):
    hi = x.astype(BF16)
    lo = (x - hi.astype(F32)).astype(BF16)
    return hi, lo


def _dot(a, b):
    return jnp.dot(a, b, preferred_element_type=F32)


def _dot_f32ish(a, b):
    ah, al = _split2(a)
    bh, bl = _split2(b)
    return _dot(ah, bh) + _dot(al, bh) + _dot(ah, bl)


def _mm_kernel(a_ref, b_ref, o_ref):
    o_ref[...] = _dot(a_ref[...].astype(BF16), b_ref[...]).astype(o_ref.dtype)


def _matmul(a, b, out_dtype, tm, tn):
    m, k = a.shape
    nc = b.shape[1]
    return pl.pallas_call(
        _mm_kernel,
        grid=(m // tm, nc // tn),
        in_specs=[pl.BlockSpec((tm, k), lambda i, j: (i, 0)),
                  pl.BlockSpec((k, tn), lambda i, j: (0, j))],
        out_specs=pl.BlockSpec((tm, tn), lambda i, j: (i, j)),
        out_shape=jax.ShapeDtypeStruct((m, nc), out_dtype),
        compiler_params=_cparams("parallel", "parallel"),
        name="matmul",
    )(a, b)


def _mm3_kernel(a_ref, b_ref, o_ref):
    o_ref[...] = _dot_f32ish(a_ref[...], b_ref[...])


def _matmul_f32ish(a, b, tm):
    m, k = a.shape
    nc = b.shape[1]
    return pl.pallas_call(
        _mm3_kernel,
        grid=(m // tm,),
        in_specs=[pl.BlockSpec((tm, k), lambda i: (i, 0)),
                  pl.BlockSpec((k, nc), lambda i: (0, 0))],
        out_specs=pl.BlockSpec((tm, nc), lambda i: (i, 0)),
        out_shape=jax.ShapeDtypeStruct((m, nc), F32),
        compiler_params=_cparams("parallel"),
        name="matmul_f32ish",
    )(a, b)


ROW_SPLIT = D_MODEL // LANES


def _store_row_tiled(ref, val):
    rows = val.shape[0]
    for c in range(ROW_SPLIT):
        ref[pl.ds(c, rows, stride=ROW_SPLIT), :] = val[:, c * LANES:(c + 1) * LANES]


def _load_row_tiled(ref):
    rows = ref.shape[0] // ROW_SPLIT
    return jnp.concatenate([ref[pl.ds(c, rows, stride=ROW_SPLIT), :] for c in range(ROW_SPLIT)], axis=1)


HALF = D_MODEL // 2
U32 = jnp.uint32
HI16 = 0xFFFF0000


def _pack_bf16_pairs(x):
    bits = lax.bitcast_convert_type(x.astype(BF16).astype(F32), U32)
    return (bits[:, :HALF] >> 16) | (bits[:, HALF:] & U32(HI16))


def _unpack_bf16_pairs(w):
    lo = lax.bitcast_convert_type(w << 16, F32).astype(BF16)
    hi = lax.bitcast_convert_type(w & U32(HI16), F32).astype(BF16)
    return lo, hi


def _mm_ln_kernel(a_ref, w_ref, h_ref, g_ref, b_ref, o_ref, op_ref):
    acc = _dot(a_ref[...], w_ref[...])
    out = _layer_norm(DN_ALPHA * h_ref[...] + acc, g_ref[...], b_ref[...])
    o_ref[...] = out
    op_ref[...] = _pack_bf16_pairs(out)


def _matmul_res_ln(a, w, h, g, b, tm):
    m, k = a.shape
    d = w.shape[1]
    return pl.pallas_call(
        _mm_ln_kernel,
        grid=(m // tm,),
        in_specs=[pl.BlockSpec((tm, k), lambda i: (i, 0)),
                  pl.BlockSpec((k, d), lambda i: (0, 0)),
                  pl.BlockSpec((tm, d), lambda i: (i, 0)),
                  pl.BlockSpec((1, d), lambda i: (0, 0)),
                  pl.BlockSpec((1, d), lambda i: (0, 0))],
        out_specs=[pl.BlockSpec((tm, d), lambda i: (i, 0)),
                   pl.BlockSpec((tm, HALF), lambda i: (i, 0))],
        out_shape=[jax.ShapeDtypeStruct((m, d), F32),
                   jax.ShapeDtypeStruct((m, HALF), U32)],
        compiler_params=_cparams("parallel"),
        name="matmul_res_ln",
    )(a, w, h, g, b)


def _ssd_kernel(z_ref, xbc_ref, dt_ref, cw_ref, cb_ref, dtb_ref, alog_ref, dsk_ref, nw_ref,
                y_ref, xe_ref, st_ref):
    q = SSM_CHUNK

    @pl.when(pl.program_id(1) == 0)
    def _():
        xe_ref[0:8, :] = jnp.zeros((8, SSM_CONV_DIM), F32)
        st_ref[...] = jnp.zeros_like(st_ref)

    u = xbc_ref[...]
    xe_ref[8:8 + q, :] = u
    w = cw_ref[...]
    conv = (cb_ref[...] + w[3:4] * u + w[2:3] * xe_ref[7:7 + q, :]
            + w[1:2] * xe_ref[6:6 + q, :] + w[0:1] * xe_ref[5:5 + q, :])
    xe_ref[0:8, :] = xe_ref[q:q + 8, :]
    act = conv * jax.nn.sigmoid(conv)

    pre = dt_ref[...] + dtb_ref[...]
    dt = jnp.maximum(pre, 0.0) + jnp.log(1.0 + jnp.exp(-jnp.abs(pre)))
    adt = dt * (-jnp.exp(alog_ref[...]))

    row = lax.broadcasted_iota(I32, (q, q), 0)
    col = lax.broadcasted_iota(I32, (q, q), 1)
    tril = row >= col
    tri_b = jnp.where(tril, 1.0, 0.0).astype(BF16)
    a_hi = adt.astype(BF16)
    r1 = adt - a_hi.astype(F32)
    a_mid = r1.astype(BF16)
    a_lo = (r1 - a_mid.astype(F32)).astype(BF16)
    acs = _dot(tri_b, a_hi) + _dot(tri_b, a_mid) + _dot(tri_b, a_lo)
    acs_t = acs.T
    eacs = jnp.exp(acs)
    left = col < SSM_HEAD_DIM

    for g in range(SSM_GROUPS):
        b0 = SSM_D_INNER + g * SSM_STATE
        c0 = SSM_D_INNER + SSM_GROUPS * SSM_STATE + g * SSM_STATE
        bm = act[:, b0:b0 + SSM_STATE]
        cm = act[:, c0:c0 + SSM_STATE].astype(BF16)
        cb = lax.dot_general(cm, bm.astype(BF16), (((1,), (1,)), ((), ())),
                             preferred_element_type=F32)
        bm_t = bm.T
        gs = g * 512
        y_off = _dot(cm, st_ref[:, gs:gs + 512].astype(BF16))
        slabs = []
        for pr in range(4):
            ha = g * 8 + pr * 2
            hb = ha + 1
            cs = gs + pr * LANES
            x2 = act[:, cs:cs + LANES]
            dt2 = jnp.where(left, dt[:, ha:ha + 1], dt[:, hb:hb + 1])
            xdt = (x2 * dt2).astype(BF16)
            ys, ups = [], []
            for h in (ha, hb):
                a_col = acs[:, h:h + 1]
                a_row = acs_t[h:h + 1, :]
                decay = jnp.where(tril, jnp.exp(a_col - a_row), 0.0)
                ys.append(_dot((cb * decay).astype(BF16), xdt))
                to_end = jnp.exp(a_row[:, q - 1:q] - a_row)
                ups.append(_dot((bm_t * to_end).astype(BF16), xdt))
            y_diag = jnp.where(left, ys[0], ys[1])
            upd = jnp.where(left, ups[0], ups[1])
            e2 = jnp.where(left, eacs[:, ha:ha + 1], eacs[:, hb:hb + 1])
            cd = jnp.where(left[0:1, :], eacs[q - 1:q, ha:ha + 1], eacs[q - 1:q, hb:hb + 1])
            y2 = y_diag + y_off[:, pr * LANES:(pr + 1) * LANES] * e2 + dsk_ref[:, cs:cs + LANES] * x2
            st_ref[:, cs:cs + LANES] = st_ref[:, cs:cs + LANES] * cd + upd
            slabs.append(y2)
        yg = jnp.concatenate(slabs, axis=1)
        zg = z_ref[:, gs:gs + 512]
        yg = yg * (zg * jax.nn.sigmoid(zg))
        ms = jnp.mean(yg * yg, -1, keepdims=True)
        y_ref[:, gs:gs + 512] = (yg * lax.rsqrt(ms + LN_EPS) * nw_ref[:, gs:gs + 512]).astype(y_ref.dtype)


def _ssd(z, xbc, dt_raw, conv_w, conv_b, dt_bias, a_log, d_rep, norm_w, bsz, seq):
    n = z.shape[0]
    nchunk = seq // SSM_CHUNK
    q = SSM_CHUNK
    tok = lambda b, c: (b * nchunk + c, 0)
    fixed = lambda b, c: (0, 0)
    return pl.pallas_call(
        _ssd_kernel,
        grid=(bsz, nchunk),
        in_specs=[pl.BlockSpec((q, SSM_D_INNER), tok),
                  pl.BlockSpec((q, SSM_CONV_DIM), tok),
                  pl.BlockSpec((q, LANES), tok),
                  pl.BlockSpec((SSM_CONV, SSM_CONV_DIM), fixed),
                  pl.BlockSpec((1, SSM_CONV_DIM), fixed),
                  pl.BlockSpec((1, LANES), fixed),
                  pl.BlockSpec((1, LANES), fixed),
                  pl.BlockSpec((1, SSM_D_INNER), fixed),
                  pl.BlockSpec((1, SSM_D_INNER), fixed)],
        out_specs=pl.BlockSpec((q, SSM_D_INNER), tok),
        out_shape=jax.ShapeDtypeStruct((n, SSM_D_INNER), BF16),
        scratch_shapes=[pltpu.VMEM((q + 8, SSM_CONV_DIM), F32),
                        pltpu.VMEM((SSM_STATE, SSM_D_INNER), F32)],
        compiler_params=_cparams("parallel", "arbitrary"),
        name="ssd_chunk",
    )(z, xbc, dt_raw, conv_w, conv_b, dt_bias, a_log, d_rep, norm_w)


def _qkv_dilated_kernel(h8_ref, w_ref, o_ref, *, dil):
    span = ATTN_BLOCK * dil
    rows = []
    for blk in range(o_ref.shape[0] // span):
        for r in range(dil):
            first = (blk * span + r) * ROW_SPLIT
            rows.append(jnp.concatenate(
                [h8_ref[pl.ds(first + c, ATTN_BLOCK, stride=ROW_SPLIT * dil), :] for c in range(ROW_SPLIT)],
                axis=1))
    x = jnp.concatenate(rows, axis=0).astype(BF16)
    o_ref[...] = _dot(x, w_ref[...]).astype(o_ref.dtype)


def _qkv_dilated(h8, w, dil, tm):
    n = h8.shape[0] // ROW_SPLIT
    nc = w.shape[1]
    return pl.pallas_call(
        functools.partial(_qkv_dilated_kernel, dil=dil),
        grid=(n // tm,),
        in_specs=[pl.BlockSpec((tm * ROW_SPLIT, LANES), lambda i: (i, 0)),
                  pl.BlockSpec((D_MODEL, nc), lambda i: (0, 0))],
        out_specs=pl.BlockSpec((tm, nc), lambda i: (i, 0)),
        out_shape=jax.ShapeDtypeStruct((n, nc), BF16),
        compiler_params=_cparams("parallel"),
        name=f"qkv_dil{dil}",
    )(h8, w)


def _attn_kernel(*refs, nh, dil, has_prev):
    if has_prev:
        q_ref, kp_ref, kc_ref, vp_ref, vc_ref, bp_ref, bc_ref = refs[:7]
        out_refs = refs[7:]
        first_pen = jnp.where(pl.program_id(1) > 0, 0.0, NEG_BIG)
    else:
        q_ref, kc_ref, vc_ref, bc_ref = refs[:4]
        out_refs = refs[4:]
    s_scr, p_scr, inv_scr = out_refs[-3:]
    o_refs, lse_ref = out_refs[:-4], out_refs[-4]
    lane = lax.broadcasted_iota(I32, (ATTN_BLOCK, LANES), 1)
    left = lane < ATTN_HEAD_DIM
    zero = jnp.zeros((), BF16)
    nt = (((1,), (1,)), ((), ()))

    def residue(r, carry):
        rows = pl.ds(pl.multiple_of(r * ATTN_BLOCK, ATTN_BLOCK), ATTN_BLOCK)
        dst = pl.ds(r, ATTN_BLOCK, stride=dil)
        for hh in range(nh):
            cols = pl.ds(hh // 2 * LANES, LANES)
            qm = jnp.where(left if hh % 2 == 0 else ~left, q_ref[rows, cols], zero)
            s_c = lax.dot_general(qm, kc_ref[rows, cols], nt, preferred_element_type=F32) + bc_ref[hh]
            if has_prev:
                s_p = lax.dot_general(qm, kp_ref[rows, cols], nt, preferred_element_type=F32)
                s_scr[hh, :, 0:ATTN_BLOCK] = s_p + (bp_ref[hh] + first_pen)
                s_scr[hh, :, ATTN_BLOCK:] = s_c
            else:
                s_scr[hh] = s_c
        lse_sum = jnp.zeros((ATTN_BLOCK, 1), F32)
        for hh in range(nh):
            s = s_scr[hh]
            m = jnp.max(s, -1, keepdims=True)
            p = jnp.exp(s - m)
            den = jnp.sum(p, -1, keepdims=True)
            p_scr[hh] = p.astype(BF16)
            inv_scr[hh] = 1.0 / den
            lse_sum = lse_sum + (m + jnp.log(den))
        for pr in range(ATTN_SLAB // LANES):
            cols = pl.ds(pr * LANES, LANES)
            halves = []
            for hh in (pr * 2, pr * 2 + 1):
                if hh >= nh:
                    halves.append(jnp.zeros((ATTN_BLOCK, LANES), F32))
                    continue
                if has_prev:
                    o = (_dot(p_scr[hh, :, 0:ATTN_BLOCK], vp_ref[rows, cols])
                         + _dot(p_scr[hh, :, ATTN_BLOCK:], vc_ref[rows, cols]))
                else:
                    o = _dot(p_scr[hh], vc_ref[rows, cols])
                halves.append(o * inv_scr[hh])
            o_refs[pr][dst, :] = jnp.where(left, halves[0], halves[1])
        lse_ref[dst, :] = jnp.broadcast_to(lse_sum * (1.0 / nh), (ATTN_BLOCK, LANES))
        return carry

    lax.fori_loop(0, dil, residue, 0)


def _attention_group(qkv, bias_p, bias_c, gi, dil, nh, bsz, seq):
    span = ATTN_BLOCK * dil
    nb = seq // span
    has_prev = nb > 1
    blk = (span, ATTN_SLAB)
    cur = lambda which: (lambda b, n: (b * nb + n, which))
    prev = lambda which: (lambda b, n: (b * nb + jnp.maximum(n - 1, 0), which))
    fixed = lambda b, n: (0, 0, 0)
    tab = pl.BlockSpec((nh, ATTN_BLOCK, ATTN_BLOCK), fixed)
    if has_prev:
        in_specs = [pl.BlockSpec(blk, cur(0)), pl.BlockSpec(blk, prev(1)), pl.BlockSpec(blk, cur(1)),
                    pl.BlockSpec(blk, prev(2)), pl.BlockSpec(blk, cur(2)), tab, tab]
        args = (qkv, qkv, qkv, qkv, qkv, bias_p, bias_c)
    else:
        in_specs = [pl.BlockSpec(blk, cur(0)), pl.BlockSpec(blk, cur(1)), pl.BlockSpec(blk, cur(2)), tab]
        args = (qkv, qkv, qkv, bias_c)
    n_out = ATTN_SLAB // LANES + 1
    keys = 2 * ATTN_BLOCK if has_prev else ATTN_BLOCK
    outs = pl.pallas_call(
        functools.partial(_attn_kernel, nh=nh, dil=dil, has_prev=has_prev),
        grid=(bsz, nb),
        in_specs=in_specs,
        out_specs=[pl.BlockSpec((span, LANES), lambda b, n: (b * nb + n, 0))] * n_out,
        out_shape=[jax.ShapeDtypeStruct((bsz * seq, LANES), F32)] * n_out,
        scratch_shapes=[pltpu.VMEM((nh, ATTN_BLOCK, keys), F32),
                        pltpu.VMEM((nh, ATTN_BLOCK, keys), BF16),
                        pltpu.VMEM((nh, ATTN_BLOCK, 1), F32)],
        compiler_params=_cparams("parallel", "parallel"),
        name=f"dilated_attn_g{gi}",
    )(*args)
    return outs[:-1], outs[-1]


def _attn_out_kernel(*refs):
    n_pairs = ATTN_SLAB // LANES
    n_groups = len(ATTN_GROUPS)
    o_refs = refs[:n_groups * n_pairs]
    l_refs = refs[n_groups * n_pairs:n_groups * (n_pairs + 1)]
    w_ref, h_ref, g_ref, b_ref, out_ref, outp_ref = refs[n_groups * (n_pairs + 1):]
    ls = [r[:, 0:1] for r in l_refs]
    m = jnp.maximum(jnp.maximum(ls[0], ls[1]), ls[2])
    es = [jnp.exp(v - m) for v in ls]
    scale = n_groups / (es[0] + es[1] + es[2])
    acc = None
    for gi in range(n_groups):
        o = jnp.concatenate([r[...] for r in o_refs[gi * n_pairs:(gi + 1) * n_pairs]], axis=1)
        part = _dot((o * (es[gi] * scale)).astype(BF16), w_ref[gi])
        acc = part if acc is None else acc + part
    out = _layer_norm(DN_ALPHA * h_ref[...] + acc, g_ref[...], b_ref[...])
    out_ref[...] = out
    outp_ref[...] = _pack_bf16_pairs(out)


def _attn_out(outs, lses, w, h, g, b, tm):
    n = h.shape[0]
    tok = lambda i: (i, 0)
    fixed2 = lambda i: (0, 0)
    return pl.pallas_call(
        _attn_out_kernel,
        grid=(n // tm,),
        in_specs=[pl.BlockSpec((tm, LANES), tok)] * (len(outs) + len(lses))
                 + [pl.BlockSpec((3, ATTN_SLAB, D_MODEL), lambda i: (0, 0, 0)),
                    pl.BlockSpec((tm, D_MODEL), tok),
                    pl.BlockSpec((1, D_MODEL), fixed2), pl.BlockSpec((1, D_MODEL), fixed2)],
        out_specs=[pl.BlockSpec((tm, D_MODEL), tok), pl.BlockSpec((tm, HALF), tok)],
        out_shape=[jax.ShapeDtypeStruct((n, D_MODEL), F32),
                   jax.ShapeDtypeStruct((n, HALF), U32)],
        compiler_params=_cparams("parallel"),
        name="attn_out_ln",
    )(*outs, *lses, w, h, g, b)


def _t5_bucket(dist):
    max_exact = N_BUCKETS // 2
    n = np.maximum(dist, 1).astype(np.float64)
    large = max_exact + (np.log(n / max_exact) / np.log(MAX_DISTANCE / max_exact)
                         * (N_BUCKETS - max_exact)).astype(np.int32)
    large = np.minimum(large, N_BUCKETS - 1)
    return np.where(dist < max_exact, dist, large).astype(np.int32)


def _group_bias(rel_bias, h0, nh, dil):
    qi = np.arange(ATTN_BLOCK)[:, None]
    ki = np.arange(ATTN_BLOCK)[None, :]
    tabs = []
    for delta, band in ((qi + ATTN_BLOCK - ki, ki >= qi), (qi - ki, ki <= qi)):
        bucket = _t5_bucket(np.clip(delta, 0, None) * dil)
        onehot = (bucket[..., None] == np.arange(N_BUCKETS)).astype(np.float32)
        t = jnp.einsum("qkb,bh->hqk", onehot, rel_bias[:, h0:h0 + nh], precision=lax.Precision.HIGHEST)
        tabs.append(jnp.where(band[None], t, NEG_BIG).astype(F32))
    return tabs


def _router_kernel(h_ref, w_ref, b_ref, ints_ref, flts_ref, cnt_ref, carry_ref, *, tm):
    @pl.when(pl.program_id(0) == 0)
    def _():
        carry_ref[...] = jnp.zeros_like(carry_ref)

    lt = _dot_f32ish(h_ref[...], w_ref[...]).T + b_ref[...]
    gl = lt[0:MOE_GROUPS]
    r4 = lax.broadcasted_iota(I32, (MOE_GROUPS, tm), 0)
    gmax = jnp.max(gl, 0, keepdims=True)
    gidx = jnp.min(jnp.where(gl == gmax, r4, MOE_GROUPS), 0, keepdims=True)
    gval = 1.0 / jnp.sum(jnp.exp(gl - gmax), 0, keepdims=True)

    esel = jnp.zeros((MOE_EPG, tm), F32)
    for g in range(MOE_GROUPS):
        esel = jnp.where(gidx == g, lt[8 + g * MOE_EPG:8 + (g + 1) * MOE_EPG], esel)
    r8 = lax.broadcasted_iota(I32, (MOE_EPG, tm), 0)
    v1 = jnp.max(esel, 0, keepdims=True)
    i1 = jnp.min(jnp.where(esel == v1, r8, MOE_EPG), 0, keepdims=True)
    rest = jnp.where(r8 == i1, -jnp.inf, esel)
    v2 = jnp.max(rest, 0, keepdims=True)
    i2 = jnp.min(jnp.where(rest == v2, r8, MOE_EPG), 0, keepdims=True)
    t = jnp.exp(v2 - v1)
    p1 = gval / (1.0 + t)
    p2 = p1 * t
    e1 = gidx * MOE_EPG + i1
    e2 = gidx * MOE_EPG + i2

    r32 = lax.broadcasted_iota(I32, (MOE_EXPERTS, tm), 0)
    oh1 = r32 == e1
    oh2 = r32 == e2
    oh = jnp.where(oh1 | oh2, 1.0, 0.0)
    tr = lax.broadcasted_iota(I32, (tm, tm), 0)
    tc = lax.broadcasted_iota(I32, (tm, tm), 1)
    before = jnp.where(tr < tc, 1.0, 0.0).astype(BF16)
    base = _dot(oh.astype(BF16), before) + carry_ref[:, 0:1]
    rank1 = jnp.sum(jnp.where(oh1, base, 0.0), 0, keepdims=True).astype(I32)
    rank2 = jnp.sum(jnp.where(oh2, base, 0.0), 0, keepdims=True).astype(I32)
    carry_ref[...] = carry_ref[...] + jnp.sum(oh, 1, keepdims=True)
    cnt_ref[...] = carry_ref[...]

    ints_ref[...] = jnp.where(r8 == 0, e1, jnp.where(r8 == 1, e2, jnp.where(r8 == 2, rank1,
                              jnp.where(r8 == 3, rank2, 0))))
    flts_ref[...] = jnp.where(r8 == 0, p1, jnp.where(r8 == 1, p2, 0.0))


def _router(h, w_r, b_r, tm):
    n = h.shape[0]
    return pl.pallas_call(
        functools.partial(_router_kernel, tm=tm),
        grid=(n // tm,),
        in_specs=[pl.BlockSpec((tm, D_MODEL), lambda i: (i, 0)),
                  pl.BlockSpec((D_MODEL, LANES), lambda i: (0, 0)),
                  pl.BlockSpec((LANES, 1), lambda i: (0, 0))],
        out_specs=[pl.BlockSpec((8, tm), lambda i: (0, i)),
                   pl.BlockSpec((8, tm), lambda i: (0, i)),
                   pl.BlockSpec((MOE_EXPERTS, LANES), lambda i: (0, 0))],
        out_shape=[jax.ShapeDtypeStruct((8, n), I32),
                   jax.ShapeDtypeStruct((8, n), F32),
                   jax.ShapeDtypeStruct((MOE_EXPERTS, LANES), F32)],
        scratch_shapes=[pltpu.VMEM((MOE_EXPERTS, LANES), F32)],
        compiler_params=_cparams("arbitrary"),
        name="moe_router",
    )(h, w_r, b_r)


def _row_copy(src_ref, s, dst_ref, d, sem):
    def first(r):
        return r * ROW_SPLIT if isinstance(r, int) else pl.multiple_of(r * ROW_SPLIT, ROW_SPLIT)

    return pltpu.make_async_copy(src_ref.at[pl.ds(first(s), ROW_SPLIT), :],
                                 dst_ref.at[pl.ds(first(d), ROW_SPLIT), :], sem)


def _tile_wait(src_ref, dst_ref, sem):
    pltpu.make_async_copy(src_ref.at[pl.ds(0, dst_ref.shape[0]), :], dst_ref, sem).wait()


def _experts_kernel(pos1_ref, pos2_ref, te_ref, na_ref, hp_ref, wg_ref, wu_ref, wd_ref, y_ref,
                    xa, xb, wg_s, wu_s, wd_s, row_tok):
    i = pl.program_id(0)
    n_tiles = pl.num_programs(0)
    tr = MOE_ROW_TILE
    active = i < na_ref[0]
    changed = jnp.logical_or(i == 0, te_ref[i] != te_ref[jnp.maximum(i - 1, 0)])

    @pl.when(i == 0)
    def _():
        def clear(r, carry):
            row_tok[r] = 0
            return carry

        lax.fori_loop(0, row_tok.shape[0], clear, 0, unroll=8)

        def invert(t, carry):
            row_tok[pos1_ref[t]] = t
            row_tok[pos2_ref[t]] = t
            return carry

        lax.fori_loop(0, pos1_ref.shape[0], invert, 0, unroll=8)

        def pick(r, carry):
            xa[pl.ds(r, 1), :] = hp_ref[pl.ds(row_tok[r], 1), :]
            return carry

        lax.fori_loop(0, tr, pick, 0, unroll=8)

    @pl.when(jnp.logical_and(active, changed))
    def _():
        wg_s[...] = wg_ref[0].astype(BF16)
        wu_s[...] = wu_ref[0].astype(BF16)
        wd_s[...] = wd_ref[0].astype(BF16)

    def step(cur, nxt):
        base = jnp.minimum(i + 1, n_tiles - 1) * tr
        for r in range(tr):
            nxt[r:r + 1, :] = hp_ref[pl.ds(row_tok[base + r], 1), :]
        lo, hi = _unpack_bf16_pairs(cur[...])
        gate = _dot(lo, wg_s[0:HALF, :]) + _dot(hi, wg_s[HALF:, :])
        up = _dot(lo, wu_s[0:HALF, :]) + _dot(hi, wu_s[HALF:, :])
        hid = (gate * jax.nn.sigmoid(gate) * up).astype(BF16)
        _store_row_tiled(y_ref, _dot(hid, wd_s[...]))

    even = i % 2 == 0

    @pl.when(jnp.logical_and(active, even))
    def _():
        step(xa, xb)

    @pl.when(jnp.logical_and(active, jnp.logical_not(even)))
    def _():
        step(xb, xa)

    @pl.when(jnp.logical_not(active))
    def _():
        y_ref[...] = jnp.zeros_like(y_ref)


def _experts(hp, pos1, pos2, tile_expert, n_active, wg, wu, wd):
    tr = MOE_ROW_TILE
    n_tiles = tile_expert.shape[0]
    rows = n_tiles * tr
    wmap = lambda i, p1, p2, te, na: (te[i], 0, 0)
    return pl.pallas_call(
        _experts_kernel,
        grid_spec=pltpu.PrefetchScalarGridSpec(
            num_scalar_prefetch=4,
            grid=(n_tiles,),
            in_specs=[pl.BlockSpec(memory_space=pltpu.VMEM),
                      pl.BlockSpec((1, D_MODEL, MOE_D_FF), wmap),
                      pl.BlockSpec((1, D_MODEL, MOE_D_FF), wmap),
                      pl.BlockSpec((1, MOE_D_FF, D_MODEL), wmap)],
            out_specs=pl.BlockSpec((tr * ROW_SPLIT, LANES), lambda i, p1, p2, te, na: (i, 0)),
            scratch_shapes=[pltpu.VMEM((tr, HALF), U32),
                            pltpu.VMEM((tr, HALF), U32),
                            pltpu.VMEM((D_MODEL, MOE_D_FF), BF16),
                            pltpu.VMEM((D_MODEL, MOE_D_FF), BF16),
                            pltpu.VMEM((MOE_D_FF, D_MODEL), BF16),
                            pltpu.SMEM((rows,), I32)]),
        out_shape=jax.ShapeDtypeStruct((rows * ROW_SPLIT, LANES), F32),
        compiler_params=pltpu.CompilerParams(dimension_semantics=("arbitrary",),
                                             vmem_limit_bytes=EXPERTS_VMEM_LIMIT),
        name="moe_experts",
    )(pos1, pos2, tile_expert, n_active, hp, wg, wu, wd)


def _combine_kernel(pos1_ref, pos2_ref, y_ref, h_ref, p1_ref, p2_ref, g_ref, b_ref, *rest, tm, tiled_copy):
    out_ref = rest[0]
    a1, a2, b1, b2, sem = rest[-5:]
    i = pl.program_id(0)
    last = pl.num_programs(0) - 1

    @pl.when(i == 0)
    def _():
        def issue(t, carry):
            _row_copy(y_ref, pos1_ref[t], a1, t, sem.at[0]).start()
            _row_copy(y_ref, pos2_ref[t], a2, t, sem.at[0]).start()
            return carry

        lax.fori_loop(0, tm, issue, 0)

    def step(c1, c2, cur_sem, n1, n2, nxt_sem):
        _tile_wait(y_ref, c1, cur_sem)
        _tile_wait(y_ref, c2, cur_sem)
        base = jnp.minimum(i + 1, last) * tm
        for t in range(tm):
            _row_copy(y_ref, pos1_ref[base + t], n1, t, nxt_sem).start(priority=0)
            _row_copy(y_ref, pos2_ref[base + t], n2, t, nxt_sem).start(priority=DMA_QUEUES - 1)
        ffn = p1_ref[...] * _load_row_tiled(c1) + p2_ref[...] * _load_row_tiled(c2)
        out = _layer_norm(DN_ALPHA * h_ref[...] + ffn, g_ref[...], b_ref[...])
        out_ref[...] = out
        if tiled_copy:
            _store_row_tiled(rest[1], out)

        @pl.when(i == last)
        def _():
            _tile_wait(y_ref, n1, nxt_sem)
            _tile_wait(y_ref, n2, nxt_sem)

    @pl.when(i % 2 == 0)
    def _():
        step(a1, a2, sem.at[0], b1, b2, sem.at[1])

    @pl.when(i % 2 == 1)
    def _():
        step(b1, b2, sem.at[1], a1, a2, sem.at[0])


def _combine(y, h, pos1, pos2, p1, p2, g, b, tm, tiled_copy):
    n = h.shape[0]
    tok = lambda i, a, c: (i, 0)
    fixed = lambda i, a, c: (0, 0)
    out_specs = [pl.BlockSpec((tm, D_MODEL), tok)]
    out_shape = [jax.ShapeDtypeStruct((n, D_MODEL), F32)]
    if tiled_copy:
        out_specs.append(pl.BlockSpec((tm * ROW_SPLIT, LANES), tok))
        out_shape.append(jax.ShapeDtypeStruct((n * ROW_SPLIT, LANES), F32))
    return pl.pallas_call(
        functools.partial(_combine_kernel, tm=tm, tiled_copy=tiled_copy),
        grid_spec=pltpu.PrefetchScalarGridSpec(
            num_scalar_prefetch=2,
            grid=(n // tm,),
            in_specs=[pl.BlockSpec(memory_space=pl.ANY),
                      pl.BlockSpec((tm, D_MODEL), tok),
                      pl.BlockSpec((tm, 1), tok), pl.BlockSpec((tm, 1), tok),
                      pl.BlockSpec((1, D_MODEL), fixed), pl.BlockSpec((1, D_MODEL), fixed)],
            out_specs=out_specs,
            scratch_shapes=[pltpu.VMEM((tm * ROW_SPLIT, LANES), F32)] * 4 + [pltpu.SemaphoreType.DMA((2,))]),
        out_shape=out_shape,
        compiler_params=_cparams("arbitrary"),
        name="moe_combine_ln",
    )(pos1, pos2, y, h, p1, p2, g, b)


def _plan_kernel(ints_ref, cnt_ref, pos_ref, meta_ref, *, layer, chunk):
    tr = MOE_ROW_TILE
    n = ints_ref.shape[1]
    ne = MOE_EXPERTS
    tiles = (cnt_ref[...] + (tr - 1.0)) * (1.0 / tr)
    tiles = tiles.astype(I32).astype(F32)
    lower = lax.broadcasted_iota(I32, (ne, ne), 0) >= lax.broadcasted_iota(I32, (ne, ne), 1)
    ends = _dot(jnp.where(lower, 1.0, 0.0).astype(BF16), tiles.astype(BF16))
    start_col = ((ends - tiles) * tr).astype(I32)[:, 0:1]
    r8 = lax.broadcasted_iota(I32, (8, chunk), 0)
    re = lax.broadcasted_iota(I32, (ne, chunk), 0)
    for c in range(n // chunk):
        blk = ints_ref[:, c * chunk:(c + 1) * chunk]
        s1 = jnp.sum(jnp.where(re == blk[0:1], start_col, 0), 0, keepdims=True)
        s2 = jnp.sum(jnp.where(re == blk[1:2], start_col, 0), 0, keepdims=True)
        pos_ref[:, c * chunk:(c + 1) * chunk] = jnp.where(
            r8 == 0, s1 + blk[2:3], jnp.where(r8 == 1, s2 + blk[3:4], 0))
    width = meta_ref.shape[1]
    tile_id = lax.broadcasted_iota(I32, (ne, width), 1).astype(F32)
    te = jnp.sum(jnp.where(ends[:, 0:1] <= tile_id, 1, 0), 0, keepdims=True)
    te = jnp.minimum(te, ne - 1) + layer * ne
    n_used = ends[ne - 1:ne, 0:1].astype(I32)
    rm = lax.broadcasted_iota(I32, (8, width), 0)
    meta_ref[...] = jnp.where(rm == 0, te, jnp.where(rm == 1, n_used, 0))


def _plan(ints, cnt, layer, n_tiles):
    n = ints.shape[1]
    width = -(-n_tiles // LANES) * LANES
    return pl.pallas_call(
        functools.partial(_plan_kernel, layer=layer, chunk=2048),
        out_shape=[jax.ShapeDtypeStruct((8, n), I32), jax.ShapeDtypeStruct((8, width), I32)],
        compiler_params=pltpu.CompilerParams(vmem_limit_bytes=VMEM_LIMIT),
        name="moe_plan",
    )(ints, cnt)


def _moe_layer(h, hp, layer, group_w, group_b, expert_w, expert_b, gate_w, up_w, down_w, ln_g, ln_b,
               tiled_copy):
    n = h.shape[0]
    ew = jnp.transpose(expert_w, (1, 0, 2)).reshape(D_MODEL, MOE_EXPERTS)
    w_r = jnp.zeros((D_MODEL, LANES), F32).at[:, 0:MOE_GROUPS].set(group_w).at[:, 8:8 + MOE_EXPERTS].set(ew)
    b_r = jnp.zeros((LANES,), F32).at[0:MOE_GROUPS].set(group_b).at[8:8 + MOE_EXPERTS].set(expert_b.reshape(-1))
    ints, flts, cnt = _router(h, w_r, b_r.reshape(LANES, 1), 512)
    tr = MOE_ROW_TILE
    n_tiles = (2 * n) // tr + MOE_EXPERTS
    pos, meta = _plan(ints, cnt, layer, n_tiles)
    pos1, pos2 = pos[0], pos[1]

    y = _experts(hp, pos1, pos2, meta[0, :n_tiles], meta[1, :1],
                 gate_w.reshape(-1, D_MODEL, MOE_D_FF),
                 up_w.reshape(-1, D_MODEL, MOE_D_FF),
                 down_w.reshape(-1, MOE_D_FF, D_MODEL))
    return _combine(y, h, pos1, pos2, flts[0].reshape(n, 1), flts[1].reshape(n, 1),
                    ln_g.reshape(1, -1), ln_b.reshape(1, -1), 256, tiled_copy)


def _pad_heads(w, axis):
    parts = []
    h0 = 0
    for _, _, nh in ATTN_GROUPS:
        sl = [slice(None)] * w.ndim
        sl[axis] = slice(h0 * ATTN_HEAD_DIM, (h0 + nh) * ATTN_HEAD_DIM)
        part = w[tuple(sl)]
        pad = [(0, 0)] * w.ndim
        pad[axis] = (0, ATTN_SLAB - nh * ATTN_HEAD_DIM)
        parts.append(jnp.pad(part, pad))
        h0 += nh
    return parts


def _ssd_layer(h, in_w, conv_w, conv_b, dt_bias, a_log, d_skip, norm_w, out_w, ln_g, ln_b, bsz, seq):
    in_b = in_w.astype(BF16)
    z = _matmul(h, in_b[:, :SSM_D_INNER], F32, 1024, 1024)
    xbc = _matmul(h, in_b[:, SSM_D_INNER:SSM_D_INNER + SSM_CONV_DIM], F32, 1024, 1024)
    dt_w = jnp.pad(in_w[:, SSM_D_INNER + SSM_CONV_DIM:], ((0, 0), (0, LANES - SSM_HEADS)))
    dt_raw = _matmul_f32ish(h, dt_w, 512)
    pad32 = lambda v: jnp.pad(v, (0, LANES - SSM_HEADS)).reshape(1, LANES)
    y = _ssd(z, xbc, dt_raw, conv_w, conv_b.reshape(1, -1), pad32(dt_bias), pad32(a_log),
             jnp.repeat(d_skip, SSM_HEAD_DIM).reshape(1, -1), norm_w.reshape(1, -1), bsz, seq)
    return _matmul_res_ln(y, out_w.astype(BF16), h, ln_g.reshape(1, -1), ln_b.reshape(1, -1), 512)


def _attn_layer(h, h8, kv_w, q_w, o_w, rel_bias, ln_g, ln_b, bsz, seq):
    width = ATTN_HEADS * ATTN_HEAD_DIM
    w_q = _pad_heads(q_w * (ATTN_HEAD_DIM ** -0.5), 1)
    w_k = _pad_heads(kv_w[:, :width], 1)
    w_v = _pad_heads(kv_w[:, width:], 1)
    outs, lses = [], []
    h0 = 0
    for gi, (_, dil, nh) in enumerate(ATTN_GROUPS):
        w_g = jnp.concatenate([w_q[gi], w_k[gi], w_v[gi]], axis=1).astype(BF16)
        if dil == 1:
            qkv = _matmul(h, w_g, BF16, 1024, 3 * ATTN_SLAB)
        else:
            qkv = _qkv_dilated(h8, w_g, dil, max(1024, ATTN_BLOCK * dil))
        bias_p, bias_c = _group_bias(rel_bias, h0, nh, dil)
        o, lse = _attention_group(qkv, bias_p, bias_c, gi, dil, nh, bsz, seq)
        outs.extend(o)
        lses.append(lse)
        h0 += nh
    w_o = jnp.stack(_pad_heads(o_w, 0)).astype(BF16)
    return _attn_out(outs, lses, w_o, h, ln_g.reshape(1, -1), ln_b.reshape(1, -1), 512)


def kernel(x, ssm_in_w, ssm_conv_w, ssm_conv_b, ssm_dt_bias, ssm_a_log, ssm_d, ssm_norm_w, ssm_out_w,
           kv_w, attn_q_w, attn_o_w, rel_bias, moe_group_w, moe_group_b, moe_expert_w, moe_expert_b,
           moe_gate_w, moe_up_w, moe_down_w, ln_g, ln_b):
    bsz, seq, d = x.shape
    h = x.reshape(bsz * seq, d)
    n_ssd = DEPTH // 2
    h8 = None
    for i in range(DEPTH):
        if i < n_ssd:
            h, hp = _ssd_layer(h, ssm_in_w[i], ssm_conv_w[i], ssm_conv_b[i], ssm_dt_bias[i], ssm_a_log[i],
                               ssm_d[i], ssm_norm_w[i], ssm_out_w[i], ln_g[i, 0], ln_b[i, 0], bsz, seq)
        else:
            j = i - n_ssd
            h, hp = _attn_layer(h, h8, kv_w, attn_q_w[j], attn_o_w[j], rel_bias, ln_g[i, 0], ln_b[i, 0],
                                bsz, seq)
        feeds_attention = n_ssd <= i + 1 < DEPTH
        res = _moe_layer(h, hp, i, moe_group_w[i], moe_group_b[i], moe_expert_w[i], moe_expert_b[i],
                         moe_gate_w, moe_up_w, moe_down_w, ln_g[i, 1], ln_b[i, 1], feeds_attention)
        h = res[0]
        h8 = res[1] if feeds_attention else None
    return h.reshape(bsz, seq, d)
```

```python
import functools
import math

import numpy as np
import jax
import jax.numpy as jnp
from jax import lax
from jax.experimental import pallas as pl
from jax.experimental.pallas import tpu as pltpu

F32 = jnp.float32
BF16 = jnp.bfloat16
I32 = jnp.int32

D_MODEL = 1024
DEPTH = 2
DN_ALPHA = (2 * DEPTH) ** 0.25
LN_EPS = 1e-5
LOG2_E = math.log2(math.e)

SSM_D_INNER = 2048
SSM_HEAD_DIM = 64
SSM_HEADS = 32
SSM_GROUPS = 4
SSM_STATE = 128
SSM_CONV = 4
SSM_CHUNK = 128
SSM_CONV_DIM = SSM_D_INNER + 2 * SSM_GROUPS * SSM_STATE

ATTN_HEAD_DIM = 64
ATTN_GROUPS = ((128, 1, 6), (512, 4, 5), (2048, 16, 5))
ATTN_HEADS = 16
ATTN_BLOCK = 128
N_BUCKETS = 32
MAX_DISTANCE = 2048
ATTN_SLAB = 384
NEG_BIG = -1e30

MOE_GROUPS = 4
MOE_EPG = 8
MOE_EXPERTS = MOE_GROUPS * MOE_EPG
MOE_D_FF = 512
MOE_ROW_TILE = 256

LANES = 128
DMA_QUEUES = 2
VMEM_LIMIT = 48 * 1024 * 1024
EXPERTS_VMEM_LIMIT = 56 * 1024 * 1024


def _cparams(*sem):
    return pltpu.CompilerParams(dimension_semantics=sem, vmem_limit_bytes=VMEM_LIMIT)


def _layer_norm(x, g, b):
    mu = jnp.mean(x, -1, keepdims=True)
    xc = x - mu
    var = jnp.mean(xc * xc, -1, keepdims=True)
    return xc * lax.rsqrt(var + LN_EPS) * g + b


def _split2(x):
    hi = x.astype(BF16)
    lo = (x - hi.astype(F32)).astype(BF16)
    return hi, lo


def _dot(a, b):
    return jnp.dot(a, b, preferred_element_type=F32)


def _dot_f32ish(a, b):
    ah, al = _split2(a)
    bh, bl = _split2(b)
    return _dot(ah, bh) + _dot(al, bh) + _dot(ah, bl)


def _mm_kernel(a_ref, b_ref, o_ref):
    o_ref[...] = _dot(a_ref[...].astype(BF16), b_ref[...]).astype(o_ref.dtype)


def _matmul(a, b, out_dtype, tm, tn):
    m, k = a.shape
    nc = b.shape[1]
    return pl.pallas_call(
        _mm_kernel,
        grid=(m // tm, nc // tn),
        in_specs=[pl.BlockSpec((tm, k), lambda i, j: (i, 0)),
                  pl.BlockSpec((k, tn), lambda i, j: (0, j))],
        out_specs=pl.BlockSpec((tm, tn), lambda i, j: (i, j)),
        out_shape=jax.ShapeDtypeStruct((m, nc), out_dtype),
        compiler_params=_cparams("parallel", "parallel"),
        name="matmul",
    )(a, b)


def _mm3_kernel(a_ref, b_ref, o_ref):
    o_ref[...] = _dot_f32ish(a_ref[...], b_ref[...])


def _matmul_f32ish(a, b, tm):
    m, k = a.shape
    nc = b.shape[1]
    return pl.pallas_call(
        _mm3_kernel,
        grid=(m // tm,),
        in_specs=[pl.BlockSpec((tm, k), lambda i: (i, 0)),
                  pl.BlockSpec((k, nc), lambda i: (0, 0))],
        out_specs=pl.BlockSpec((tm, nc), lambda i: (i, 0)),
        out_shape=jax.ShapeDtypeStruct((m, nc), F32),
        compiler_params=_cparams("parallel"),
        name="matmul_f32ish",
    )(a, b)


ROW_SPLIT = D_MODEL // LANES


def _store_row_tiled(ref, val):
    rows = val.shape[0]
    for c in range(ROW_SPLIT):
        ref[pl.ds(c, rows, stride=ROW_SPLIT), :] = val[:, c * LANES:(c + 1) * LANES]


def _load_row_tiled(ref):
    rows = ref.shape[0] // ROW_SPLIT
    return jnp.concatenate([ref[pl.ds(c, rows, stride=ROW_SPLIT), :] for c in range(ROW_SPLIT)], axis=1)


HALF = D_MODEL // 2
U32 = jnp.uint32
HI16 = 0xFFFF0000


def _pack_bf16_pairs(x):
    bits = lax.bitcast_convert_type(x.astype(BF16).astype(F32), U32)
    return (bits[:, :HALF] >> 16) | (bits[:, HALF:] & U32(HI16))


def _unpack_bf16_pairs(w):
    lo = lax.bitcast_convert_type(w << 16, F32).astype(BF16)
    hi = lax.bitcast_convert_type(w & U32(HI16), F32).astype(BF16)
    return lo, hi


def _mm_ln_kernel(a_ref, w_ref, h_ref, g_ref, b_ref, o_ref, op_ref):
    acc = _dot(a_ref[...], w_ref[...])
    out = _layer_norm(DN_ALPHA * h_ref[...] + acc, g_ref[...], b_ref[...])
    o_ref[...] = out
    op_ref[...] = _pack_bf16_pairs(out)


def _matmul_res_ln(a, w, h, g, b, tm):
    m, k = a.shape
    d = w.shape[1]
    return pl.pallas_call(
        _mm_ln_kernel,
        grid=(m // tm,),
        in_specs=[pl.BlockSpec((tm, k), lambda i: (i, 0)),
                  pl.BlockSpec((k, d), lambda i: (0, 0)),
                  pl.BlockSpec((tm, d), lambda i: (i, 0)),
                  pl.BlockSpec((1, d), lambda i: (0, 0)),
                  pl.BlockSpec((1, d), lambda i: (0, 0))],
        out_specs=[pl.BlockSpec((tm, d), lambda i: (i, 0)),
                   pl.BlockSpec((tm, HALF), lambda i: (i, 0))],
        out_shape=[jax.ShapeDtypeStruct((m, d), F32),
                   jax.ShapeDtypeStruct((m, HALF), U32)],
        compiler_params=_cparams("parallel"),
        name="matmul_res_ln",
    )(a, w, h, g, b)


def _ssd_kernel(z_ref, xbc_ref, dt_ref, cw_ref, cb_ref, dtb_ref, alog_ref, dsk_ref, nw_ref,
                y_ref, xe_ref, st_ref):
    q = SSM_CHUNK

    @pl.when(pl.program_id(1) == 0)
    def _():
        xe_ref[0:8, :] = jnp.zeros((8, SSM_CONV_DIM), F32)
        st_ref[...] = jnp.zeros_like(st_ref)

    u = xbc_ref[...]
    xe_ref[8:8 + q, :] = u
    w = cw_ref[...]
    conv = (cb_ref[...] + w[3:4] * u + w[2:3] * xe_ref[7:7 + q, :]
            + w[1:2] * xe_ref[6:6 + q, :] + w[0:1] * xe_ref[5:5 + q, :])
    xe_ref[0:8, :] = xe_ref[q:q + 8, :]
    act = conv * jax.nn.sigmoid(conv)

    pre = dt_ref[...] + dtb_ref[...]
    dt = jnp.maximum(pre, 0.0) + jnp.log(1.0 + jnp.exp(-jnp.abs(pre)))
    adt = dt * (-jnp.exp(alog_ref[...]) * LOG2_E)

    row = lax.broadcasted_iota(I32, (q, q), 0)
    col = lax.broadcasted_iota(I32, (q, q), 1)
    tril = row >= col
    tri_b = jnp.where(tril, 1.0, 0.0).astype(BF16)
    a_hi = adt.astype(BF16)
    r1 = adt - a_hi.astype(F32)
    a_mid = r1.astype(BF16)
    a_lo = (r1 - a_mid.astype(F32)).astype(BF16)
    acs = _dot(tri_b, a_hi) + _dot(tri_b, a_mid) + _dot(tri_b, a_lo)
    acs_t = acs.T
    eacs = jnp.exp2(acs)
    left = col < SSM_HEAD_DIM

    for g in range(SSM_GROUPS):
        b0 = SSM_D_INNER + g * SSM_STATE
        c0 = SSM_D_INNER + SSM_GROUPS * SSM_STATE + g * SSM_STATE
        bm = act[:, b0:b0 + SSM_STATE]
        cm = act[:, c0:c0 + SSM_STATE].astype(BF16)
        cb = lax.dot_general(cm, bm.astype(BF16), (((1,), (1,)), ((), ())),
                             preferred_element_type=F32)
        bm_t = bm.T
        gs = g * 512
        y_off = _dot(cm, st_ref[:, gs:gs + 512].astype(BF16))
        slabs = []
        for pr in range(4):
            ha = g * 8 + pr * 2
            hb = ha + 1
            cs = gs + pr * LANES
            x2 = act[:, cs:cs + LANES]
            dt2 = jnp.where(left, dt[:, ha:ha + 1], dt[:, hb:hb + 1])
            xdt = (x2 * dt2).astype(BF16)
            ys, ups = [], []
            for h in (ha, hb):
                a_col = acs[:, h:h + 1]
                a_row = acs_t[h:h + 1, :]
                decay = jnp.where(tril, jnp.exp2(a_col - a_row), 0.0)
                ys.append(_dot((cb * decay).astype(BF16), xdt))
                to_end = jnp.exp2(a_row[:, q - 1:q] - a_row)
                ups.append(_dot((bm_t * to_end).astype(BF16), xdt))
            y_diag = jnp.where(left, ys[0], ys[1])
            upd = jnp.where(left, ups[0], ups[1])
            e2 = jnp.where(left, eacs[:, ha:ha + 1], eacs[:, hb:hb + 1])
            cd = jnp.where(left[0:1, :], eacs[q - 1:q, ha:ha + 1], eacs[q - 1:q, hb:hb + 1])
            y2 = y_diag + y_off[:, pr * LANES:(pr + 1) * LANES] * e2 + dsk_ref[:, cs:cs + LANES] * x2
            st_ref[:, cs:cs + LANES] = st_ref[:, cs:cs + LANES] * cd + upd
            slabs.append(y2)
        yg = jnp.concatenate(slabs, axis=1)
        zg = z_ref[:, gs:gs + 512]
        yg = yg * (zg * jax.nn.sigmoid(zg))
        ms = jnp.mean(yg * yg, -1, keepdims=True)
        y_ref[:, gs:gs + 512] = (yg * lax.rsqrt(ms + LN_EPS) * nw_ref[:, gs:gs + 512]).astype(y_ref.dtype)


def _ssd(z, xbc, dt_raw, conv_w, conv_b, dt_bias, a_log, d_rep, norm_w, bsz, seq):
    n = z.shape[0]
    nchunk = seq // SSM_CHUNK
    q = SSM_CHUNK
    tok = lambda b, c: (b * nchunk + c, 0)
    fixed = lambda b, c: (0, 0)
    return pl.pallas_call(
        _ssd_kernel,
        grid=(bsz, nchunk),
        in_specs=[pl.BlockSpec((q, SSM_D_INNER), tok),
                  pl.BlockSpec((q, SSM_CONV_DIM), tok),
                  pl.BlockSpec((q, LANES), tok),
                  pl.BlockSpec((SSM_CONV, SSM_CONV_DIM), fixed),
                  pl.BlockSpec((1, SSM_CONV_DIM), fixed),
                  pl.BlockSpec((1, LANES), fixed),
                  pl.BlockSpec((1, LANES), fixed),
                  pl.BlockSpec((1, SSM_D_INNER), fixed),
                  pl.BlockSpec((1, SSM_D_INNER), fixed)],
        out_specs=pl.BlockSpec((q, SSM_D_INNER), tok),
        out_shape=jax.ShapeDtypeStruct((n, SSM_D_INNER), BF16),
        scratch_shapes=[pltpu.VMEM((q + 8, SSM_CONV_DIM), F32),
                        pltpu.VMEM((SSM_STATE, SSM_D_INNER), F32)],
        compiler_params=_cparams("parallel", "arbitrary"),
        name="ssd_chunk",
    )(z, xbc, dt_raw, conv_w, conv_b, dt_bias, a_log, d_rep, norm_w)


def _qkv_dilated_kernel(h8_ref, w_ref, o_ref, *, dil):
    span = ATTN_BLOCK * dil
    rows = []
    for blk in range(o_ref.shape[0] // span):
        for r in range(dil):
            first = (blk * span + r) * ROW_SPLIT
            rows.append(jnp.concatenate(
                [h8_ref[pl.ds(first + c, ATTN_BLOCK, stride=ROW_SPLIT * dil), :] for c in range(ROW_SPLIT)],
                axis=1))
    x = jnp.concatenate(rows, axis=0).astype(BF16)
    o_ref[...] = _dot(x, w_ref[...]).astype(o_ref.dtype)


def _qkv_dilated(h8, w, dil, tm):
    n = h8.shape[0] // ROW_SPLIT
    nc = w.shape[1]
    return pl.pallas_call(
        functools.partial(_qkv_dilated_kernel, dil=dil),
        grid=(n // tm,),
        in_specs=[pl.BlockSpec((tm * ROW_SPLIT, LANES), lambda i: (i, 0)),
                  pl.BlockSpec((D_MODEL, nc), lambda i: (0, 0))],
        out_specs=pl.BlockSpec((tm, nc), lambda i: (i, 0)),
        out_shape=jax.ShapeDtypeStruct((n, nc), BF16),
        compiler_params=_cparams("parallel"),
        name=f"qkv_dil{dil}",
    )(h8, w)


def _attn_kernel(*refs, nh, dil, has_prev):
    if has_prev:
        q_ref, kp_ref, kc_ref, vp_ref, vc_ref, bp_ref, bc_ref = refs[:7]
        out_refs = refs[7:]
        first_pen = jnp.where(pl.program_id(1) > 0, 0.0, NEG_BIG)
    else:
        q_ref, kc_ref, vc_ref, bc_ref = refs[:4]
        out_refs = refs[4:]
    s_scr, p_scr, inv_scr = out_refs[-3:]
    o_refs, lse_ref = out_refs[:-4], out_refs[-4]
    lane = lax.broadcasted_iota(I32, (ATTN_BLOCK, LANES), 1)
    left = lane < ATTN_HEAD_DIM
    zero = jnp.zeros((), BF16)
    nt = (((1,), (1,)), ((), ()))

    def residue(r, carry):
        rows = pl.ds(pl.multiple_of(r * ATTN_BLOCK, ATTN_BLOCK), ATTN_BLOCK)
        dst = pl.ds(r, ATTN_BLOCK, stride=dil)
        for hh in range(nh):
            cols = pl.ds(hh // 2 * LANES, LANES)
            qm = jnp.where(left if hh % 2 == 0 else ~left, q_ref[rows, cols], zero)
            s_c = lax.dot_general(qm, kc_ref[rows, cols], nt, preferred_element_type=F32) + bc_ref[hh]
            if has_prev:
                s_p = lax.dot_general(qm, kp_ref[rows, cols], nt, preferred_element_type=F32)
                s_scr[hh, :, 0:ATTN_BLOCK] = s_p + (bp_ref[hh] + first_pen)
                s_scr[hh, :, ATTN_BLOCK:] = s_c
            else:
                s_scr[hh] = s_c
        lse_sum = jnp.zeros((ATTN_BLOCK, 1), F32)
        for hh in range(nh):
            s = s_scr[hh]
            m = jnp.max(s, -1, keepdims=True)
            p = jnp.exp(s - m)
            den = jnp.sum(p, -1, keepdims=True)
            p_scr[hh] = p.astype(BF16)
            inv_scr[hh] = 1.0 / den
            lse_sum = lse_sum + (m + jnp.log(den))
        for pr in range(ATTN_SLAB // LANES):
            cols = pl.ds(pr * LANES, LANES)
            halves = []
            for hh in (pr * 2, pr * 2 + 1):
                if hh >= nh:
                    halves.append(jnp.zeros((ATTN_BLOCK, LANES), F32))
                    continue
                if has_prev:
                    o = (_dot(p_scr[hh, :, 0:ATTN_BLOCK], vp_ref[rows, cols])
                         + _dot(p_scr[hh, :, ATTN_BLOCK:], vc_ref[rows, cols]))
                else:
                    o = _dot(p_scr[hh], vc_ref[rows, cols])
                halves.append(o * inv_scr[hh])
            o_refs[pr][dst, :] = jnp.where(left, halves[0], halves[1])
        lse_ref[dst, :] = jnp.broadcast_to(lse_sum * (1.0 / nh), (ATTN_BLOCK, LANES))
        return carry

    lax.fori_loop(0, dil, residue, 0)


def _attention_group(qkv, bias_p, bias_c, gi, dil, nh, bsz, seq):
    span = ATTN_BLOCK * dil
    nb = seq // span
    has_prev = nb > 1
    blk = (span, ATTN_SLAB)
    cur = lambda which: (lambda b, n: (b * nb + n, which))
    prev = lambda which: (lambda b, n: (b * nb + jnp.maximum(n - 1, 0), which))
    fixed = lambda b, n: (0, 0, 0)
    tab = pl.BlockSpec((nh, ATTN_BLOCK, ATTN_BLOCK), fixed)
    if has_prev:
        in_specs = [pl.BlockSpec(blk, cur(0)), pl.BlockSpec(blk, prev(1)), pl.BlockSpec(blk, cur(1)),
                    pl.BlockSpec(blk, prev(2)), pl.BlockSpec(blk, cur(2)), tab, tab]
        args = (qkv, qkv, qkv, qkv, qkv, bias_p, bias_c)
    else:
        in_specs = [pl.BlockSpec(blk, cur(0)), pl.BlockSpec(blk, cur(1)), pl.BlockSpec(blk, cur(2)), tab]
        args = (qkv, qkv, qkv, bias_c)
    n_out = ATTN_SLAB // LANES + 1
    keys = 2 * ATTN_BLOCK if has_prev else ATTN_BLOCK
    outs = pl.pallas_call(
        functools.partial(_attn_kernel, nh=nh, dil=dil, has_prev=has_prev),
        grid=(bsz, nb),
        in_specs=in_specs,
        out_specs=[pl.BlockSpec((span, LANES), lambda b, n: (b * nb + n, 0))] * n_out,
        out_shape=[jax.ShapeDtypeStruct((bsz * seq, LANES), F32)] * n_out,
        scratch_shapes=[pltpu.VMEM((nh, ATTN_BLOCK, keys), F32),
                        pltpu.VMEM((nh, ATTN_BLOCK, keys), BF16),
                        pltpu.VMEM((nh, ATTN_BLOCK, 1), F32)],
        compiler_params=_cparams("parallel", "parallel"),
        name=f"dilated_attn_g{gi}",
    )(*args)
    return outs[:-1], outs[-1]


def _attn_out_kernel(*refs):
    n_pairs = ATTN_SLAB // LANES
    n_groups = len(ATTN_GROUPS)
    o_refs = refs[:n_groups * n_pairs]
    l_refs = refs[n_groups * n_pairs:n_groups * (n_pairs + 1)]
    w_ref, h_ref, g_ref, b_ref, out_ref, outp_ref = refs[n_groups * (n_pairs + 1):]
    ls = [r[:, 0:1] for r in l_refs]
    m = jnp.maximum(jnp.maximum(ls[0], ls[1]), ls[2])
    es = [jnp.exp(v - m) for v in ls]
    scale = n_groups / (es[0] + es[1] + es[2])
    acc = None
    for gi in range(n_groups):
        o = jnp.concatenate([r[...] for r in o_refs[gi * n_pairs:(gi + 1) * n_pairs]], axis=1)
        part = _dot((o * (es[gi] * scale)).astype(BF16), w_ref[gi])
        acc = part if acc is None else acc + part
    out = _layer_norm(DN_ALPHA * h_ref[...] + acc, g_ref[...], b_ref[...])
    out_ref[...] = out
    outp_ref[...] = _pack_bf16_pairs(out)


def _attn_out(outs, lses, w, h, g, b, tm):
    n = h.shape[0]
    tok = lambda i: (i, 0)
    fixed2 = lambda i: (0, 0)
    return pl.pallas_call(
        _attn_out_kernel,
        grid=(n // tm,),
        in_specs=[pl.BlockSpec((tm, LANES), tok)] * (len(outs) + len(lses))
                 + [pl.BlockSpec((3, ATTN_SLAB, D_MODEL), lambda i: (0, 0, 0)),
                    pl.BlockSpec((tm, D_MODEL), tok),
                    pl.BlockSpec((1, D_MODEL), fixed2), pl.BlockSpec((1, D_MODEL), fixed2)],
        out_specs=[pl.BlockSpec((tm, D_MODEL), tok), pl.BlockSpec((tm, HALF), tok)],
        out_shape=[jax.ShapeDtypeStruct((n, D_MODEL), F32),
                   jax.ShapeDtypeStruct((n, HALF), U32)],
        compiler_params=_cparams("parallel"),
        name="attn_out_ln",
    )(*outs, *lses, w, h, g, b)


def _t5_bucket(dist):
    max_exact = N_BUCKETS // 2
    n = np.maximum(dist, 1).astype(np.float64)
    large = max_exact + (np.log(n / max_exact) / np.log(MAX_DISTANCE / max_exact)
                         * (N_BUCKETS - max_exact)).astype(np.int32)
    large = np.minimum(large, N_BUCKETS - 1)
    return np.where(dist < max_exact, dist, large).astype(np.int32)


def _group_bias(rel_bias, h0, nh, dil):
    qi = np.arange(ATTN_BLOCK)[:, None]
    ki = np.arange(ATTN_BLOCK)[None, :]
    tabs = []
    for delta, band in ((qi + ATTN_BLOCK - ki, ki >= qi), (qi - ki, ki <= qi)):
        bucket = _t5_bucket(np.clip(delta, 0, None) * dil)
        onehot = (bucket[..., None] == np.arange(N_BUCKETS)).astype(np.float32)
        t = jnp.einsum("qkb,bh->hqk", onehot, rel_bias[:, h0:h0 + nh], precision=lax.Precision.HIGHEST)
        tabs.append(jnp.where(band[None], t, NEG_BIG).astype(F32))
    return tabs


def _router_kernel(h_ref, w_ref, b_ref, ints_ref, flts_ref, cnt_ref, carry_ref, *, tm):
    @pl.when(pl.program_id(0) == 0)
    def _():
        carry_ref[...] = jnp.zeros_like(carry_ref)

    lt = _dot_f32ish(h_ref[...], w_ref[...]).T + b_ref[...]
    gl = lt[0:MOE_GROUPS]
    r4 = lax.broadcasted_iota(I32, (MOE_GROUPS, tm), 0)
    gmax = jnp.max(gl, 0, keepdims=True)
    gidx = jnp.min(jnp.where(gl == gmax, r4, MOE_GROUPS), 0, keepdims=True)
    gval = 1.0 / jnp.sum(jnp.exp(gl - gmax), 0, keepdims=True)

    esel = jnp.zeros((MOE_EPG, tm), F32)
    for g in range(MOE_GROUPS):
        esel = jnp.where(gidx == g, lt[8 + g * MOE_EPG:8 + (g + 1) * MOE_EPG], esel)
    r8 = lax.broadcasted_iota(I32, (MOE_EPG, tm), 0)
    v1 = jnp.max(esel, 0, keepdims=True)
    i1 = jnp.min(jnp.where(esel == v1, r8, MOE_EPG), 0, keepdims=True)
    rest = jnp.where(r8 == i1, -jnp.inf, esel)
    v2 = jnp.max(rest, 0, keepdims=True)
    i2 = jnp.min(jnp.where(rest == v2, r8, MOE_EPG), 0, keepdims=True)
    t = jnp.exp(v2 - v1)
    p1 = gval / (1.0 + t)
    p2 = p1 * t
    e1 = gidx * MOE_EPG + i1
    e2 = gidx * MOE_EPG + i2

    r32 = lax.broadcasted_iota(I32, (MOE_EXPERTS, tm), 0)
    oh1 = r32 == e1
    oh2 = r32 == e2
    oh = jnp.where(oh1 | oh2, 1.0, 0.0)
    tr = lax.broadcasted_iota(I32, (tm, tm), 0)
    tc = lax.broadcasted_iota(I32, (tm, tm), 1)
    before = jnp.where(tr < tc, 1.0, 0.0).astype(BF16)
    base = _dot(oh.astype(BF16), before) + carry_ref[:, 0:1]
    rank1 = jnp.sum(jnp.where(oh1, base, 0.0), 0, keepdims=True).astype(I32)
    rank2 = jnp.sum(jnp.where(oh2, base, 0.0), 0, keepdims=True).astype(I32)
    carry_ref[...] = carry_ref[...] + jnp.sum(oh, 1, keepdims=True)
    cnt_ref[...] = carry_ref[...]

    ints_ref[...] = jnp.where(r8 == 0, e1, jnp.where(r8 == 1, e2, jnp.where(r8 == 2, rank1,
                              jnp.where(r8 == 3, rank2, 0))))
    flts_ref[...] = jnp.where(r8 == 0, p1, jnp.where(r8 == 1, p2, 0.0))


def _router(h, w_r, b_r, tm):
    n = h.shape[0]
    return pl.pallas_call(
        functools.partial(_router_kernel, tm=tm),
        grid=(n // tm,),
        in_specs=[pl.BlockSpec((tm, D_MODEL), lambda i: (i, 0)),
                  pl.BlockSpec((D_MODEL, LANES), lambda i: (0, 0)),
                  pl.BlockSpec((LANES, 1), lambda i: (0, 0))],
        out_specs=[pl.BlockSpec((8, tm), lambda i: (0, i)),
                   pl.BlockSpec((8, tm), lambda i: (0, i)),
                   pl.BlockSpec((MOE_EXPERTS, LANES), lambda i: (0, 0))],
        out_shape=[jax.ShapeDtypeStruct((8, n), I32),
                   jax.ShapeDtypeStruct((8, n), F32),
                   jax.ShapeDtypeStruct((MOE_EXPERTS, LANES), F32)],
        scratch_shapes=[pltpu.VMEM((MOE_EXPERTS, LANES), F32)],
        compiler_params=_cparams("arbitrary"),
        name="moe_router",
    )(h, w_r, b_r)


def _row_copy(src_ref, s, dst_ref, d, sem):
    def first(r):
        return r * ROW_SPLIT if isinstance(r, int) else pl.multiple_of(r * ROW_SPLIT, ROW_SPLIT)

    return pltpu.make_async_copy(src_ref.at[pl.ds(first(s), ROW_SPLIT), :],
                                 dst_ref.at[pl.ds(first(d), ROW_SPLIT), :], sem)


def _tile_wait(src_ref, dst_ref, sem):
    pltpu.make_async_copy(src_ref.at[pl.ds(0, dst_ref.shape[0]), :], dst_ref, sem).wait()


def _experts_kernel(pos1_ref, pos2_ref, te_ref, na_ref, hp_ref, zeros_ref, wg_ref, wu_ref, wd_ref, y_ref,
                    xa, xb, wg_s, wu_s, wd_s, row_tok, sem):
    i = pl.program_id(0)
    n_tiles = pl.num_programs(0)
    tr = MOE_ROW_TILE
    active = i < na_ref[0]
    changed = jnp.logical_or(i == 0, te_ref[i] != te_ref[jnp.maximum(i - 1, 0)])

    @pl.when(i == 0)
    def _():
        clear = pltpu.make_async_copy(zeros_ref, row_tok, sem)
        clear.start()
        clear.wait()

        def invert(t, carry):
            row_tok[pos1_ref[t]] = t
            row_tok[pos2_ref[t]] = t
            return carry

        lax.fori_loop(0, pos1_ref.shape[0], invert, 0, unroll=16)

        def pick(r, carry):
            xa[pl.ds(r, 1), :] = hp_ref[pl.ds(row_tok[r], 1), :]
            return carry

        lax.fori_loop(0, tr, pick, 0, unroll=8)

    @pl.when(jnp.logical_and(active, changed))
    def _():
        wg_s[...] = wg_ref[0].astype(BF16)
        wu_s[...] = wu_ref[0].astype(BF16)
        wd_s[...] = wd_ref[0].astype(BF16)

    def step(cur, nxt):
        base = jnp.minimum(i + 1, n_tiles - 1) * tr
        for r in range(tr):
            nxt[r:r + 1, :] = hp_ref[pl.ds(row_tok[base + r], 1), :]
        lo, hi = _unpack_bf16_pairs(cur[...])
        gate = _dot(lo, wg_s[0:HALF, :]) + _dot(hi, wg_s[HALF:, :])
        up = _dot(lo, wu_s[0:HALF, :]) + _dot(hi, wu_s[HALF:, :])
        hid = (gate * jax.nn.sigmoid(gate) * up).astype(BF16)
        _store_row_tiled(y_ref, _dot(hid, wd_s[...]))

    even = i % 2 == 0

    @pl.when(jnp.logical_and(active, even))
    def _():
        step(xa, xb)

    @pl.when(jnp.logical_and(active, jnp.logical_not(even)))
    def _():
        step(xb, xa)

    @pl.when(jnp.logical_not(active))
    def _():
        y_ref[...] = jnp.zeros_like(y_ref)


def _experts(hp, pos1, pos2, tile_expert, n_active, wg, wu, wd):
    tr = MOE_ROW_TILE
    n_tiles = tile_expert.shape[0]
    rows = n_tiles * tr
    wmap = lambda i, p1, p2, te, na: (te[i], 0, 0)
    return pl.pallas_call(
        _experts_kernel,
        grid_spec=pltpu.PrefetchScalarGridSpec(
            num_scalar_prefetch=4,
            grid=(n_tiles,),
            in_specs=[pl.BlockSpec(memory_space=pltpu.VMEM),
                      pl.BlockSpec(memory_space=pl.ANY),
                      pl.BlockSpec((1, D_MODEL, MOE_D_FF), wmap),
                      pl.BlockSpec((1, D_MODEL, MOE_D_FF), wmap),
                      pl.BlockSpec((1, MOE_D_FF, D_MODEL), wmap)],
            out_specs=pl.BlockSpec((tr * ROW_SPLIT, LANES), lambda i, p1, p2, te, na: (i, 0)),
            scratch_shapes=[pltpu.VMEM((tr, HALF), U32),
                            pltpu.VMEM((tr, HALF), U32),
                            pltpu.VMEM((D_MODEL, MOE_D_FF), BF16),
                            pltpu.VMEM((D_MODEL, MOE_D_FF), BF16),
                            pltpu.VMEM((MOE_D_FF, D_MODEL), BF16),
                            pltpu.SMEM((rows,), I32),
                            pltpu.SemaphoreType.DMA(())]),
        out_shape=jax.ShapeDtypeStruct((rows * ROW_SPLIT, LANES), F32),
        compiler_params=pltpu.CompilerParams(dimension_semantics=("arbitrary",),
                                             vmem_limit_bytes=EXPERTS_VMEM_LIMIT),
        name="moe_experts",
    )(pos1, pos2, tile_expert, n_active, hp, jnp.zeros((rows,), I32), wg, wu, wd)


def _combine_kernel(pos1_ref, pos2_ref, y_ref, h_ref, p1_ref, p2_ref, g_ref, b_ref, *rest, tm, tiled_copy):
    out_ref = rest[0]
    a1, a2, b1, b2, sem = rest[-5:]
    i = pl.program_id(0)
    last = pl.num_programs(0) - 1

    @pl.when(i == 0)
    def _():
        def issue(t, carry):
            _row_copy(y_ref, pos1_ref[t], a1, t, sem.at[0]).start()
            _row_copy(y_ref, pos2_ref[t], a2, t, sem.at[0]).start()
            return carry

        lax.fori_loop(0, tm, issue, 0)

    def step(c1, c2, cur_sem, n1, n2, nxt_sem):
        _tile_wait(y_ref, c1, cur_sem)
        _tile_wait(y_ref, c2, cur_sem)
        base = jnp.minimum(i + 1, last) * tm
        for t in range(tm):
            _row_copy(y_ref, pos1_ref[base + t], n1, t, nxt_sem).start(priority=0)
            _row_copy(y_ref, pos2_ref[base + t], n2, t, nxt_sem).start(priority=DMA_QUEUES - 1)
        ffn = p1_ref[...] * _load_row_tiled(c1) + p2_ref[...] * _load_row_tiled(c2)
        out = _layer_norm(DN_ALPHA * h_ref[...] + ffn, g_ref[...], b_ref[...])
        out_ref[...] = out
        if tiled_copy:
            _store_row_tiled(rest[1], out)

        @pl.when(i == last)
        def _():
            _tile_wait(y_ref, n1, nxt_sem)
            _tile_wait(y_ref, n2, nxt_sem)

    @pl.when(i % 2 == 0)
    def _():
        step(a1, a2, sem.at[0], b1, b2, sem.at[1])

    @pl.when(i % 2 == 1)
    def _():
        step(b1, b2, sem.at[1], a1, a2, sem.at[0])


def _combine(y, h, pos1, pos2, p1, p2, g, b, tm, tiled_copy):
    n = h.shape[0]
    tok = lambda i, a, c: (i, 0)
    fixed = lambda i, a, c: (0, 0)
    out_specs = [pl.BlockSpec((tm, D_MODEL), tok)]
    out_shape = [jax.ShapeDtypeStruct((n, D_MODEL), F32)]
    if tiled_copy:
        out_specs.append(pl.BlockSpec((tm * ROW_SPLIT, LANES), tok))
        out_shape.append(jax.ShapeDtypeStruct((n * ROW_SPLIT, LANES), F32))
    return pl.pallas_call(
        functools.partial(_combine_kernel, tm=tm, tiled_copy=tiled_copy),
        grid_spec=pltpu.PrefetchScalarGridSpec(
            num_scalar_prefetch=2,
            grid=(n // tm,),
            in_specs=[pl.BlockSpec(memory_space=pl.ANY),
                      pl.BlockSpec((tm, D_MODEL), tok),
                      pl.BlockSpec((tm, 1), tok), pl.BlockSpec((tm, 1), tok),
                      pl.BlockSpec((1, D_MODEL), fixed), pl.BlockSpec((1, D_MODEL), fixed)],
            out_specs=out_specs,
            scratch_shapes=[pltpu.VMEM((tm * ROW_SPLIT, LANES), F32)] * 4 + [pltpu.SemaphoreType.DMA((2,))]),
        out_shape=out_shape,
        compiler_params=_cparams("arbitrary"),
        name="moe_combine_ln",
    )(pos1, pos2, y, h, p1, p2, g, b)


def _plan_kernel(ints_ref, cnt_ref, pos_ref, meta_ref, *, layer, chunk):
    tr = MOE_ROW_TILE
    n = ints_ref.shape[1]
    ne = MOE_EXPERTS
    tiles = (cnt_ref[...] + (tr - 1.0)) * (1.0 / tr)
    tiles = tiles.astype(I32).astype(F32)
    lower = lax.broadcasted_iota(I32, (ne, ne), 0) >= lax.broadcasted_iota(I32, (ne, ne), 1)
    ends = _dot(jnp.where(lower, 1.0, 0.0).astype(BF16), tiles.astype(BF16))
    start_col = ((ends - tiles) * tr).astype(I32)[:, 0:1]
    r8 = lax.broadcasted_iota(I32, (8, chunk), 0)
    re = lax.broadcasted_iota(I32, (ne, chunk), 0)
    for c in range(n // chunk):
        blk = ints_ref[:, c * chunk:(c + 1) * chunk]
        s1 = jnp.sum(jnp.where(re == blk[0:1], start_col, 0), 0, keepdims=True)
        s2 = jnp.sum(jnp.where(re == blk[1:2], start_col, 0), 0, keepdims=True)
        pos_ref[:, c * chunk:(c + 1) * chunk] = jnp.where(
            r8 == 0, s1 + blk[2:3], jnp.where(r8 == 1, s2 + blk[3:4], 0))
    width = meta_ref.shape[1]
    tile_id = lax.broadcasted_iota(I32, (ne, width), 1).astype(F32)
    te = jnp.sum(jnp.where(ends[:, 0:1] <= tile_id, 1, 0), 0, keepdims=True)
    te = jnp.minimum(te, ne - 1) + layer * ne
    n_used = ends[ne - 1:ne, 0:1].astype(I32)
    rm = lax.broadcasted_iota(I32, (8, width), 0)
    meta_ref[...] = jnp.where(rm == 0, te, jnp.where(rm == 1, n_used, 0))


def _plan(ints, cnt, layer, n_tiles):
    n = ints.shape[1]
    width = -(-n_tiles // LANES) * LANES
    return pl.pallas_call(
        functools.partial(_plan_kernel, layer=layer, chunk=2048),
        out_shape=[jax.ShapeDtypeStruct((8, n), I32), jax.ShapeDtypeStruct((8, width), I32)],
        compiler_params=pltpu.CompilerParams(vmem_limit_bytes=VMEM_LIMIT),
        name="moe_plan",
    )(ints, cnt)


def _moe_layer(h, hp, layer, group_w, group_b, expert_w, expert_b, gate_w, up_w, down_w, ln_g, ln_b,
               tiled_copy):
    n = h.shape[0]
    ew = jnp.transpose(expert_w, (1, 0, 2)).reshape(D_MODEL, MOE_EXPERTS)
    w_r = jnp.zeros((D_MODEL, LANES), F32).at[:, 0:MOE_GROUPS].set(group_w).at[:, 8:8 + MOE_EXPERTS].set(ew)
    b_r = jnp.zeros((LANES,), F32).at[0:MOE_GROUPS].set(group_b).at[8:8 + MOE_EXPERTS].set(expert_b.reshape(-1))
    ints, flts, cnt = _router(h, w_r, b_r.reshape(LANES, 1), 512)
    tr = MOE_ROW_TILE
    n_tiles = (2 * n) // tr + MOE_EXPERTS
    pos, meta = _plan(ints, cnt, layer, n_tiles)
    pos1, pos2 = pos[0], pos[1]

    y = _experts(hp, pos1, pos2, meta[0, :n_tiles], meta[1, :1],
                 gate_w.reshape(-1, D_MODEL, MOE_D_FF),
                 up_w.reshape(-1, D_MODEL, MOE_D_FF),
                 down_w.reshape(-1, MOE_D_FF, D_MODEL))
    return _combine(y, h, pos1, pos2, flts[0].reshape(n, 1), flts[1].reshape(n, 1),
                    ln_g.reshape(1, -1), ln_b.reshape(1, -1), 256, tiled_copy)


def _pad_heads(w, axis):
    parts = []
    h0 = 0
    for _, _, nh in ATTN_GROUPS:
        sl = [slice(None)] * w.ndim
        sl[axis] = slice(h0 * ATTN_HEAD_DIM, (h0 + nh) * ATTN_HEAD_DIM)
        part = w[tuple(sl)]
        pad = [(0, 0)] * w.ndim
        pad[axis] = (0, ATTN_SLAB - nh * ATTN_HEAD_DIM)
        parts.append(jnp.pad(part, pad))
        h0 += nh
    return parts


def _in_proj_kernel(x_ref, w_ref, wdt_ref, z_ref, xbc_ref, dt_ref, *, z_tiles):
    j = pl.program_id(1)
    acc = _dot(x_ref[...].astype(BF16), w_ref[...])

    @pl.when(j < z_tiles)
    def _():
        z_ref[...] = acc

    @pl.when(j >= z_tiles)
    def _():
        xbc_ref[...] = acc

    @pl.when(j == 0)
    def _():
        dt_ref[...] = _dot_f32ish(x_ref[...], wdt_ref[...])


def _in_proj(x, w_zx, w_dt, tm, tn):
    m, k = x.shape
    z_tiles = SSM_D_INNER // tn
    n_tiles = w_zx.shape[1] // tn
    return pl.pallas_call(
        functools.partial(_in_proj_kernel, z_tiles=z_tiles),
        grid=(m // tm, n_tiles),
        in_specs=[pl.BlockSpec((tm, k), lambda i, j: (i, 0)),
                  pl.BlockSpec((k, tn), lambda i, j: (0, j)),
                  pl.BlockSpec((k, LANES), lambda i, j: (0, 0))],
        out_specs=[pl.BlockSpec((tm, tn), lambda i, j: (i, jnp.minimum(j, z_tiles - 1))),
                   pl.BlockSpec((tm, tn), lambda i, j: (i, jnp.maximum(j - z_tiles, 0))),
                   pl.BlockSpec((tm, LANES), lambda i, j: (i, 0))],
        out_shape=[jax.ShapeDtypeStruct((m, SSM_D_INNER), F32),
                   jax.ShapeDtypeStruct((m, SSM_CONV_DIM), F32),
                   jax.ShapeDtypeStruct((m, LANES), F32)],
        compiler_params=_cparams("parallel", "arbitrary"),
        name="ssm_in_proj",
    )(x, w_zx, w_dt)


def _ssd_layer(h, in_w, conv_w, conv_b, dt_bias, a_log, d_skip, norm_w, out_w, ln_g, ln_b, bsz, seq):
    split = SSM_D_INNER + SSM_CONV_DIM
    dt_w = jnp.pad(in_w[:, split:], ((0, 0), (0, LANES - SSM_HEADS)))
    z, xbc, dt_raw = _in_proj(h, in_w[:, :split].astype(BF16), dt_w, 1024, 1024)
    pad32 = lambda v: jnp.pad(v, (0, LANES - SSM_HEADS)).reshape(1, LANES)
    y = _ssd(z, xbc, dt_raw, conv_w, conv_b.reshape(1, -1), pad32(dt_bias), pad32(a_log),
             jnp.repeat(d_skip, SSM_HEAD_DIM).reshape(1, -1), norm_w.reshape(1, -1), bsz, seq)
    return _matmul_res_ln(y, out_w.astype(BF16), h, ln_g.reshape(1, -1), ln_b.reshape(1, -1), 512)


def _attn_layer(h, h8, kv_w, q_w, o_w, rel_bias, ln_g, ln_b, bsz, seq):
    width = ATTN_HEADS * ATTN_HEAD_DIM
    w_q = _pad_heads(q_w * (ATTN_HEAD_DIM ** -0.5), 1)
    w_k = _pad_heads(kv_w[:, :width], 1)
    w_v = _pad_heads(kv_w[:, width:], 1)
    outs, lses = [], []
    h0 = 0
    for gi, (_, dil, nh) in enumerate(ATTN_GROUPS):
        w_g = jnp.concatenate([w_q[gi], w_k[gi], w_v[gi]], axis=1).astype(BF16)
        if dil == 1:
            qkv = _matmul(h, w_g, BF16, 1024, 3 * ATTN_SLAB)
        else:
            qkv = _qkv_dilated(h8, w_g, dil, max(1024, ATTN_BLOCK * dil))
        bias_p, bias_c = _group_bias(rel_bias, h0, nh, dil)
        o, lse = _attention_group(qkv, bias_p, bias_c, gi, dil, nh, bsz, seq)
        outs.extend(o)
        lses.append(lse)
        h0 += nh
    w_o = jnp.stack(_pad_heads(o_w, 0)).astype(BF16)
    return _attn_out(outs, lses, w_o, h, ln_g.reshape(1, -1), ln_b.reshape(1, -1), 512)


def kernel(x, ssm_in_w, ssm_conv_w, ssm_conv_b, ssm_dt_bias, ssm_a_log, ssm_d, ssm_norm_w, ssm_out_w,
           kv_w, attn_q_w, attn_o_w, rel_bias, moe_group_w, moe_group_b, moe_expert_w, moe_expert_b,
           moe_gate_w, moe_up_w, moe_down_w, ln_g, ln_b):
    bsz, seq, d = x.shape
    h = x.reshape(bsz * seq, d)
    n_ssd = DEPTH // 2
    h8 = None
    for i in range(DEPTH):
        if i < n_ssd:
            h, hp = _ssd_layer(h, ssm_in_w[i], ssm_conv_w[i], ssm_conv_b[i], ssm_dt_bias[i], ssm_a_log[i],
                               ssm_d[i], ssm_norm_w[i], ssm_out_w[i], ln_g[i, 0], ln_b[i, 0], bsz, seq)
        else:
            j = i - n_ssd
            h, hp = _attn_layer(h, h8, kv_w, attn_q_w[j], attn_o_w[j], rel_bias, ln_g[i, 0], ln_b[i, 0],
                                bsz, seq)
        feeds_attention = n_ssd <= i + 1 < DEPTH
        res = _moe_layer(h, hp, i, moe_group_w[i], moe_group_b[i], moe_expert_w[i], moe_expert_b[i],
                         moe_gate_w, moe_up_w, moe_down_w, ln_g[i, 1], ln_b[i, 1], feeds_attention)
        h = res[0]
        h8 = res[1] if feeds_attention else None
    return h.reshape(bsz, seq, d)
```

```python
import functools
import math

import numpy as np
import jax
import jax.numpy as jnp
from jax import lax
from jax.experimental import pallas as pl
from jax.experimental.pallas import tpu as pltpu

F32 = jnp.float32
BF16 = jnp.bfloat16
I32 = jnp.int32

D_MODEL = 1024
DEPTH = 2
DN_ALPHA = (2 * DEPTH) ** 0.25
LN_EPS = 1e-5
LOG2_E = math.log2(math.e)

SSM_D_INNER = 2048
SSM_HEAD_DIM = 64
SSM_HEADS = 32
SSM_GROUPS = 4
SSM_STATE = 128
SSM_CONV = 4
SSM_CHUNK = 128
SSM_CONV_DIM = SSM_D_INNER + 2 * SSM_GROUPS * SSM_STATE

ATTN_HEAD_DIM = 64
ATTN_GROUPS = ((128, 1, 6), (512, 4, 5), (2048, 16, 5))
ATTN_HEADS = 16
ATTN_BLOCK = 128
N_BUCKETS = 32
MAX_DISTANCE = 2048
ATTN_SLAB = 384
NEG_BIG = -1e30

MOE_GROUPS = 4
MOE_EPG = 8
MOE_EXPERTS = MOE_GROUPS * MOE_EPG
MOE_D_FF = 512
MOE_ROW_TILE = 256

LANES = 128
DMA_QUEUES = 2
VMEM_LIMIT = 48 * 1024 * 1024
EXPERTS_VMEM_LIMIT = 56 * 1024 * 1024


def _cparams(*sem):
    return pltpu.CompilerParams(dimension_semantics=sem, vmem_limit_bytes=VMEM_LIMIT)


def _layer_norm(x, g, b):
    mu = jnp.mean(x, -1, keepdims=True)
    xc = x - mu
    var = jnp.mean(xc * xc, -1, keepdims=True)
    return xc * lax.rsqrt(var + LN_EPS) * g + b


def _split2(x):
    hi = x.astype(BF16)
    lo = (x - hi.astype(F32)).astype(BF16)
    return hi, lo


def _dot(a, b):
    return jnp.dot(a, b, preferred_element_type=F32)


def _dot_f32ish(a, b):
    ah, al = _split2(a)
    bh, bl = _split2(b)
    return _dot(ah, bh) + _dot(al, bh) + _dot(ah, bl)


def _mm_kernel(a_ref, b_ref, o_ref):
    o_ref[...] = _dot(a_ref[...].astype(BF16), b_ref[...]).astype(o_ref.dtype)


def _matmul(a, b, out_dtype, tm, tn):
    m, k = a.shape
    nc = b.shape[1]
    return pl.pallas_call(
        _mm_kernel,
        grid=(m // tm, nc // tn),
        in_specs=[pl.BlockSpec((tm, k), lambda i, j: (i, 0)),
                  pl.BlockSpec((k, tn), lambda i, j: (0, j))],
        out_specs=pl.BlockSpec((tm, tn), lambda i, j: (i, j)),
        out_shape=jax.ShapeDtypeStruct((m, nc), out_dtype),
        compiler_params=_cparams("parallel", "parallel"),
        name="matmul",
    )(a, b)


def _mm3_kernel(a_ref, b_ref, o_ref):
    o_ref[...] = _dot_f32ish(a_ref[...], b_ref[...])


def _matmul_f32ish(a, b, tm):
    m, k = a.shape
    nc = b.shape[1]
    return pl.pallas_call(
        _mm3_kernel,
        grid=(m // tm,),
        in_specs=[pl.BlockSpec((tm, k), lambda i: (i, 0)),
                  pl.BlockSpec((k, nc), lambda i: (0, 0))],
        out_specs=pl.BlockSpec((tm, nc), lambda i: (i, 0)),
        out_shape=jax.ShapeDtypeStruct((m, nc), F32),
        compiler_params=_cparams("parallel"),
        name="matmul_f32ish",
    )(a, b)


ROW_SPLIT = D_MODEL // LANES


def _store_row_tiled(ref, val):
    rows = val.shape[0]
    for c in range(ROW_SPLIT):
        ref[pl.ds(c, rows, stride=ROW_SPLIT), :] = val[:, c * LANES:(c + 1) * LANES]


def _load_row_tiled(ref):
    rows = ref.shape[0] // ROW_SPLIT
    return jnp.concatenate([ref[pl.ds(c, rows, stride=ROW_SPLIT), :] for c in range(ROW_SPLIT)], axis=1)


HALF = D_MODEL // 2
U32 = jnp.uint32
HI16 = 0xFFFF0000


def _pack_bf16_pairs(x):
    bits = lax.bitcast_convert_type(x.astype(BF16).astype(F32), U32)
    return (bits[:, :HALF] >> 16) | (bits[:, HALF:] & U32(HI16))


def _unpack_bf16_pairs(w):
    lo = lax.bitcast_convert_type(w << 16, F32).astype(BF16)
    hi = lax.bitcast_convert_type(w & U32(HI16), F32).astype(BF16)
    return lo, hi


def _mm_ln_kernel(a_ref, w_ref, h_ref, g_ref, b_ref, o_ref, op_ref):
    acc = _dot(a_ref[...], w_ref[...])
    out = _layer_norm(DN_ALPHA * h_ref[...] + acc, g_ref[...], b_ref[...])
    o_ref[...] = out
    op_ref[...] = _pack_bf16_pairs(out)


def _matmul_res_ln(a, w, h, g, b, tm):
    m, k = a.shape
    d = w.shape[1]
    return pl.pallas_call(
        _mm_ln_kernel,
        grid=(m // tm,),
        in_specs=[pl.BlockSpec((tm, k), lambda i: (i, 0)),
                  pl.BlockSpec((k, d), lambda i: (0, 0)),
                  pl.BlockSpec((tm, d), lambda i: (i, 0)),
                  pl.BlockSpec((1, d), lambda i: (0, 0)),
                  pl.BlockSpec((1, d), lambda i: (0, 0))],
        out_specs=[pl.BlockSpec((tm, d), lambda i: (i, 0)),
                   pl.BlockSpec((tm, HALF), lambda i: (i, 0))],
        out_shape=[jax.ShapeDtypeStruct((m, d), F32),
                   jax.ShapeDtypeStruct((m, HALF), U32)],
        compiler_params=_cparams("parallel"),
        name="matmul_res_ln",
    )(a, w, h, g, b)


def _ssd_kernel(z_ref, xbc_ref, dt_ref, cw_ref, cb_ref, dtb_ref, alog_ref, dsk_ref, nw_ref,
                y_ref, xe_ref, st_ref):
    q = SSM_CHUNK

    @pl.when(pl.program_id(1) == 0)
    def _():
        xe_ref[0:8, :] = jnp.zeros((8, SSM_CONV_DIM), F32)
        st_ref[...] = jnp.zeros_like(st_ref)

    u = xbc_ref[...]
    xe_ref[8:8 + q, :] = u
    w = cw_ref[...]
    conv = (cb_ref[...] + w[3:4] * u + w[2:3] * xe_ref[7:7 + q, :]
            + w[1:2] * xe_ref[6:6 + q, :] + w[0:1] * xe_ref[5:5 + q, :])
    xe_ref[0:8, :] = xe_ref[q:q + 8, :]
    act = conv * jax.nn.sigmoid(conv)

    pre = dt_ref[...] + dtb_ref[...]
    dt = jnp.maximum(pre, 0.0) + jnp.log(1.0 + jnp.exp(-jnp.abs(pre)))
    adt = dt * (-jnp.exp(alog_ref[...]) * LOG2_E)

    row = lax.broadcasted_iota(I32, (q, q), 0)
    col = lax.broadcasted_iota(I32, (q, q), 1)
    tril = row >= col
    tri_b = jnp.where(tril, 1.0, 0.0).astype(BF16)
    a_hi = adt.astype(BF16)
    r1 = adt - a_hi.astype(F32)
    a_mid = r1.astype(BF16)
    a_lo = (r1 - a_mid.astype(F32)).astype(BF16)
    acs = _dot(tri_b, a_hi) + _dot(tri_b, a_mid) + _dot(tri_b, a_lo)
    acs_t = acs.T
    acs_dt_t = acs_t - jnp.log(dt.T) * LOG2_E
    eacs = jnp.exp2(acs)
    left = col < SSM_HEAD_DIM

    for g in range(SSM_GROUPS):
        b0 = SSM_D_INNER + g * SSM_STATE
        c0 = SSM_D_INNER + SSM_GROUPS * SSM_STATE + g * SSM_STATE
        bm = act[:, b0:b0 + SSM_STATE]
        cm = act[:, c0:c0 + SSM_STATE].astype(BF16)
        cb = lax.dot_general(cm, bm.astype(BF16), (((1,), (1,)), ((), ())),
                             preferred_element_type=F32)
        bm_t = bm.T
        gs = g * 512
        y_off = _dot(cm, st_ref[:, gs:gs + 512].astype(BF16))
        slabs = []
        for pr in range(4):
            ha = g * 8 + pr * 2
            hb = ha + 1
            cs = gs + pr * LANES
            x2 = act[:, cs:cs + LANES]
            x2b = x2.astype(BF16)
            ys, ups = [], []
            for h in (ha, hb):
                a_col = acs[:, h:h + 1]
                a_src = acs_dt_t[h:h + 1, :]
                decay = jnp.where(tril, jnp.exp2(a_col - a_src), 0.0)
                ys.append(_dot((cb * decay).astype(BF16), x2b))
                to_end = jnp.exp2(acs_t[h:h + 1, q - 1:q] - a_src)
                ups.append(_dot((bm_t * to_end).astype(BF16), x2b))
            y_diag = jnp.where(left, ys[0], ys[1])
            upd = jnp.where(left, ups[0], ups[1])
            e2 = jnp.where(left, eacs[:, ha:ha + 1], eacs[:, hb:hb + 1])
            cd = jnp.where(left[0:1, :], eacs[q - 1:q, ha:ha + 1], eacs[q - 1:q, hb:hb + 1])
            y2 = y_diag + y_off[:, pr * LANES:(pr + 1) * LANES] * e2 + dsk_ref[:, cs:cs + LANES] * x2
            st_ref[:, cs:cs + LANES] = st_ref[:, cs:cs + LANES] * cd + upd
            slabs.append(y2)
        yg = jnp.concatenate(slabs, axis=1)
        zg = z_ref[:, gs:gs + 512]
        yg = yg * (zg * jax.nn.sigmoid(zg))
        ms = jnp.mean(yg * yg, -1, keepdims=True)
        y_ref[:, gs:gs + 512] = (yg * lax.rsqrt(ms + LN_EPS) * nw_ref[:, gs:gs + 512]).astype(y_ref.dtype)


def _ssd(z, xbc, dt_raw, conv_w, conv_b, dt_bias, a_log, d_rep, norm_w, bsz, seq):
    n = z.shape[0]
    nchunk = seq // SSM_CHUNK
    q = SSM_CHUNK
    tok = lambda b, c: (b * nchunk + c, 0)
    fixed = lambda b, c: (0, 0)
    return pl.pallas_call(
        _ssd_kernel,
        grid=(bsz, nchunk),
        in_specs=[pl.BlockSpec((q, SSM_D_INNER), tok),
                  pl.BlockSpec((q, SSM_CONV_DIM), tok),
                  pl.BlockSpec((q, LANES), tok),
                  pl.BlockSpec((SSM_CONV, SSM_CONV_DIM), fixed),
                  pl.BlockSpec((1, SSM_CONV_DIM), fixed),
                  pl.BlockSpec((1, LANES), fixed),
                  pl.BlockSpec((1, LANES), fixed),
                  pl.BlockSpec((1, SSM_D_INNER), fixed),
                  pl.BlockSpec((1, SSM_D_INNER), fixed)],
        out_specs=pl.BlockSpec((q, SSM_D_INNER), tok),
        out_shape=jax.ShapeDtypeStruct((n, SSM_D_INNER), BF16),
        scratch_shapes=[pltpu.VMEM((q + 8, SSM_CONV_DIM), F32),
                        pltpu.VMEM((SSM_STATE, SSM_D_INNER), F32)],
        compiler_params=_cparams("parallel", "arbitrary"),
        name="ssd_chunk",
    )(z, xbc, dt_raw, conv_w, conv_b, dt_bias, a_log, d_rep, norm_w)


def _qkv_dilated_kernel(h8_ref, w_ref, o_ref, *, dil):
    span = ATTN_BLOCK * dil
    per_dot = 4
    rows = []
    for blk in range(o_ref.shape[0] // span):
        for r in range(dil):
            first = (blk * span + r) * ROW_SPLIT
            rows.append(jnp.concatenate(
                [h8_ref[pl.ds(first + c, ATTN_BLOCK, stride=ROW_SPLIT * dil), :] for c in range(ROW_SPLIT)],
                axis=1).astype(BF16))
    for k in range(0, len(rows), per_dot):
        x = jnp.concatenate(rows[k:k + per_dot], axis=0)
        o_ref[k * ATTN_BLOCK:(k + per_dot) * ATTN_BLOCK, :] = _dot(x, w_ref[...]).astype(o_ref.dtype)


def _qkv_dilated(h8, w, dil, tm):
    n = h8.shape[0] // ROW_SPLIT
    nc = w.shape[1]
    return pl.pallas_call(
        functools.partial(_qkv_dilated_kernel, dil=dil),
        grid=(n // tm,),
        in_specs=[pl.BlockSpec((tm * ROW_SPLIT, LANES), lambda i: (i, 0)),
                  pl.BlockSpec((D_MODEL, nc), lambda i: (0, 0))],
        out_specs=pl.BlockSpec((tm, nc), lambda i: (i, 0)),
        out_shape=jax.ShapeDtypeStruct((n, nc), BF16),
        compiler_params=_cparams("parallel"),
        name=f"qkv_dil{dil}",
    )(h8, w)


def _attn_kernel(*refs, nh, dil, has_prev):
    if has_prev:
        q_ref, kp_ref, kc_ref, vp_ref, vc_ref, bp_ref, bc_ref = refs[:7]
        out_refs = refs[7:]
        first_pen = jnp.where(pl.program_id(1) > 0, 0.0, NEG_BIG)
    else:
        q_ref, kc_ref, vc_ref, bc_ref = refs[:4]
        out_refs = refs[4:]
    s_scr, p_scr, max_scr, den_scr = out_refs[-4:]
    o_refs, lse_ref = out_refs[:-5], out_refs[-5]
    ones = jnp.ones((s_scr.shape[2], LANES), BF16)
    lane = lax.broadcasted_iota(I32, (ATTN_BLOCK, LANES), 1)
    left = lane < ATTN_HEAD_DIM
    zero = jnp.zeros((), BF16)
    nt = (((1,), (1,)), ((), ()))

    def residue(r, carry):
        rows = pl.ds(pl.multiple_of(r * ATTN_BLOCK, ATTN_BLOCK), ATTN_BLOCK)
        dst = pl.ds(r, ATTN_BLOCK, stride=dil)
        for hh in range(nh):
            cols = pl.ds(hh // 2 * LANES, LANES)
            qm = jnp.where(left if hh % 2 == 0 else ~left, q_ref[rows, cols], zero)
            s_c = lax.dot_general(qm, kc_ref[rows, cols], nt, preferred_element_type=F32) + bc_ref[hh]
            if has_prev:
                s_p = lax.dot_general(qm, kp_ref[rows, cols], nt, preferred_element_type=F32)
                s_scr[hh, :, 0:ATTN_BLOCK] = s_p + (bp_ref[hh] + first_pen)
                s_scr[hh, :, ATTN_BLOCK:] = s_c
            else:
                s_scr[hh] = s_c
        for hh in range(nh):
            s = s_scr[hh]
            m = jnp.max(s, -1, keepdims=True)
            p = jnp.exp(s - m)
            p_scr[hh] = p.astype(BF16)
            max_scr[hh] = m
            if not has_prev:
                den_scr[hh] = jnp.sum(p, -1, keepdims=True)
        lse_sum = jnp.zeros((ATTN_BLOCK, 1), F32)
        for pr in range(ATTN_SLAB // LANES):
            cols = pl.ds(pr * LANES, LANES)
            halves = []
            for hh in (pr * 2, pr * 2 + 1):
                if hh >= nh:
                    halves.append(jnp.zeros((ATTN_BLOCK, LANES), F32))
                    continue
                if has_prev:
                    den = _dot(p_scr[hh], ones)
                    den1 = den[:, 0:1]
                    o = (_dot(p_scr[hh, :, 0:ATTN_BLOCK], vp_ref[rows, cols])
                         + _dot(p_scr[hh, :, ATTN_BLOCK:], vc_ref[rows, cols]))
                else:
                    den = den1 = den_scr[hh]
                    o = _dot(p_scr[hh], vc_ref[rows, cols])
                halves.append(o * (1.0 / den))
                lse_sum = lse_sum + (max_scr[hh] + jnp.log(den1))
            o_refs[pr][dst, :] = jnp.where(left, halves[0], halves[1])
        lse_ref[dst, :] = jnp.broadcast_to(lse_sum * (1.0 / nh), (ATTN_BLOCK, LANES))
        return carry

    lax.fori_loop(0, dil, residue, 0)


def _attention_group(qkv, bias_p, bias_c, gi, dil, nh, bsz, seq):
    span = ATTN_BLOCK * dil
    nb = seq // span
    has_prev = nb > 1
    blk = (span, ATTN_SLAB)
    cur = lambda which: (lambda b, n: (b * nb + n, which))
    prev = lambda which: (lambda b, n: (b * nb + jnp.maximum(n - 1, 0), which))
    fixed = lambda b, n: (0, 0, 0)
    tab = pl.BlockSpec((nh, ATTN_BLOCK, ATTN_BLOCK), fixed)
    if has_prev:
        in_specs = [pl.BlockSpec(blk, cur(0)), pl.BlockSpec(blk, prev(1)), pl.BlockSpec(blk, cur(1)),
                    pl.BlockSpec(blk, prev(2)), pl.BlockSpec(blk, cur(2)), tab, tab]
        args = (qkv, qkv, qkv, qkv, qkv, bias_p, bias_c)
    else:
        in_specs = [pl.BlockSpec(blk, cur(0)), pl.BlockSpec(blk, cur(1)), pl.BlockSpec(blk, cur(2)), tab]
        args = (qkv, qkv, qkv, bias_c)
    n_out = ATTN_SLAB // LANES + 1
    keys = 2 * ATTN_BLOCK if has_prev else ATTN_BLOCK
    outs = pl.pallas_call(
        functools.partial(_attn_kernel, nh=nh, dil=dil, has_prev=has_prev),
        grid=(bsz, nb),
        in_specs=in_specs,
        out_specs=[pl.BlockSpec((span, LANES), lambda b, n: (b * nb + n, 0))] * n_out,
        out_shape=[jax.ShapeDtypeStruct((bsz * seq, LANES), F32)] * n_out,
        scratch_shapes=[pltpu.VMEM((nh, ATTN_BLOCK, keys), F32),
                        pltpu.VMEM((nh, ATTN_BLOCK, keys), BF16),
                        pltpu.VMEM((nh, ATTN_BLOCK, 1), F32),
                        pltpu.VMEM((nh, ATTN_BLOCK, 1), F32)],
        compiler_params=_cparams("parallel", "parallel"),
        name=f"dilated_attn_g{gi}",
    )(*args)
    return outs[:-1], outs[-1]


def _attn_out_kernel(*refs):
    n_pairs = ATTN_SLAB // LANES
    n_groups = len(ATTN_GROUPS)
    o_refs = refs[:n_groups * n_pairs]
    l_refs = refs[n_groups * n_pairs:n_groups * (n_pairs + 1)]
    w_ref, h_ref, g_ref, b_ref, out_ref, outp_ref = refs[n_groups * (n_pairs + 1):]
    ls = [r[:, 0:1] for r in l_refs]
    m = jnp.maximum(jnp.maximum(ls[0], ls[1]), ls[2])
    es = [jnp.exp(v - m) for v in ls]
    scale = n_groups / (es[0] + es[1] + es[2])
    slabs = []
    for gi in range(n_groups):
        wt = es[gi] * scale
        slabs.extend((r[...] * wt).astype(BF16) for r in o_refs[gi * n_pairs:(gi + 1) * n_pairs])
    acc = _dot(jnp.concatenate(slabs, axis=1), w_ref[...])
    out = _layer_norm(DN_ALPHA * h_ref[...] + acc, g_ref[...], b_ref[...])
    out_ref[...] = out
    outp_ref[...] = _pack_bf16_pairs(out)


def _attn_out(outs, lses, w, h, g, b, tm):
    n = h.shape[0]
    tok = lambda i: (i, 0)
    fixed2 = lambda i: (0, 0)
    return pl.pallas_call(
        _attn_out_kernel,
        grid=(n // tm,),
        in_specs=[pl.BlockSpec((tm, LANES), tok)] * (len(outs) + len(lses))
                 + [pl.BlockSpec((len(ATTN_GROUPS) * ATTN_SLAB, D_MODEL), lambda i: (0, 0)),
                    pl.BlockSpec((tm, D_MODEL), tok),
                    pl.BlockSpec((1, D_MODEL), fixed2), pl.BlockSpec((1, D_MODEL), fixed2)],
        out_specs=[pl.BlockSpec((tm, D_MODEL), tok), pl.BlockSpec((tm, HALF), tok)],
        out_shape=[jax.ShapeDtypeStruct((n, D_MODEL), F32),
                   jax.ShapeDtypeStruct((n, HALF), U32)],
        compiler_params=_cparams("parallel"),
        name="attn_out_ln",
    )(*outs, *lses, w, h, g, b)


def _t5_bucket(dist):
    max_exact = N_BUCKETS // 2
    n = np.maximum(dist, 1).astype(np.float64)
    large = max_exact + (np.log(n / max_exact) / np.log(MAX_DISTANCE / max_exact)
                         * (N_BUCKETS - max_exact)).astype(np.int32)
    large = np.minimum(large, N_BUCKETS - 1)
    return np.where(dist < max_exact, dist, large).astype(np.int32)


def _group_bias(rel_bias, h0, nh, dil):
    qi = np.arange(ATTN_BLOCK)[:, None]
    ki = np.arange(ATTN_BLOCK)[None, :]
    tabs = []
    for delta, band in ((qi + ATTN_BLOCK - ki, ki >= qi), (qi - ki, ki <= qi)):
        bucket = _t5_bucket(np.clip(delta, 0, None) * dil)
        onehot = (bucket[..., None] == np.arange(N_BUCKETS)).astype(np.float32)
        t = jnp.einsum("qkb,bh->hqk", onehot, rel_bias[:, h0:h0 + nh], precision=lax.Precision.HIGHEST)
        tabs.append(jnp.where(band[None], t, NEG_BIG).astype(F32))
    return tabs


def _router_kernel(h_ref, w_ref, b_ref, ints_ref, flts_ref, cnt_ref, carry_ref, *, tm):
    @pl.when(pl.program_id(0) == 0)
    def _():
        carry_ref[...] = jnp.zeros_like(carry_ref)

    lt = _dot_f32ish(h_ref[...], w_ref[...]).T + b_ref[...]
    gl = lt[0:MOE_GROUPS]
    r4 = lax.broadcasted_iota(I32, (MOE_GROUPS, tm), 0)
    gmax = jnp.max(gl, 0, keepdims=True)
    gidx = jnp.min(jnp.where(gl == gmax, r4, MOE_GROUPS), 0, keepdims=True)
    gval = 1.0 / jnp.sum(jnp.exp(gl - gmax), 0, keepdims=True)

    esel = jnp.zeros((MOE_EPG, tm), F32)
    for g in range(MOE_GROUPS):
        esel = jnp.where(gidx == g, lt[8 + g * MOE_EPG:8 + (g + 1) * MOE_EPG], esel)
    r8 = lax.broadcasted_iota(I32, (MOE_EPG, tm), 0)
    v1 = jnp.max(esel, 0, keepdims=True)
    i1 = jnp.min(jnp.where(esel == v1, r8, MOE_EPG), 0, keepdims=True)
    rest = jnp.where(r8 == i1, -jnp.inf, esel)
    v2 = jnp.max(rest, 0, keepdims=True)
    i2 = jnp.min(jnp.where(rest == v2, r8, MOE_EPG), 0, keepdims=True)
    t = jnp.exp(v2 - v1)
    p1 = gval / (1.0 + t)
    p2 = p1 * t
    e1 = gidx * MOE_EPG + i1
    e2 = gidx * MOE_EPG + i2

    r32 = lax.broadcasted_iota(I32, (MOE_EXPERTS, tm), 0)
    oh1 = r32 == e1
    oh2 = r32 == e2
    oh = jnp.where(oh1 | oh2, 1.0, 0.0)
    tr = lax.broadcasted_iota(I32, (tm, tm), 0)
    tc = lax.broadcasted_iota(I32, (tm, tm), 1)
    before = jnp.where(tr < tc, 1.0, 0.0).astype(BF16)
    base = _dot(oh.astype(BF16), before) + carry_ref[:, 0:1]
    rank1 = jnp.sum(jnp.where(oh1, base, 0.0), 0, keepdims=True).astype(I32)
    rank2 = jnp.sum(jnp.where(oh2, base, 0.0), 0, keepdims=True).astype(I32)
    carry_ref[...] = carry_ref[...] + jnp.sum(oh, 1, keepdims=True)
    cnt_ref[...] = carry_ref[...]

    ints_ref[...] = jnp.where(r8 == 0, e1, jnp.where(r8 == 1, e2, jnp.where(r8 == 2, rank1,
                              jnp.where(r8 == 3, rank2, 0))))
    flts_ref[...] = jnp.where(r8 == 0, p1, jnp.where(r8 == 1, p2, 0.0))


def _router(h, w_r, b_r, tm):
    n = h.shape[0]
    return pl.pallas_call(
        functools.partial(_router_kernel, tm=tm),
        grid=(n // tm,),
        in_specs=[pl.BlockSpec((tm, D_MODEL), lambda i: (i, 0)),
                  pl.BlockSpec((D_MODEL, LANES), lambda i: (0, 0)),
                  pl.BlockSpec((LANES, 1), lambda i: (0, 0))],
        out_specs=[pl.BlockSpec((8, tm), lambda i: (0, i)),
                   pl.BlockSpec((8, tm), lambda i: (0, i)),
                   pl.BlockSpec((MOE_EXPERTS, LANES), lambda i: (0, 0))],
        out_shape=[jax.ShapeDtypeStruct((8, n), I32),
                   jax.ShapeDtypeStruct((8, n), F32),
                   jax.ShapeDtypeStruct((MOE_EXPERTS, LANES), F32)],
        scratch_shapes=[pltpu.VMEM((MOE_EXPERTS, LANES), F32)],
        compiler_params=_cparams("arbitrary"),
        name="moe_router",
    )(h, w_r, b_r)


def _row_copy(src_ref, s, dst_ref, d, sem):
    def first(r):
        return r * ROW_SPLIT if isinstance(r, int) else pl.multiple_of(r * ROW_SPLIT, ROW_SPLIT)

    return pltpu.make_async_copy(src_ref.at[pl.ds(first(s), ROW_SPLIT), :],
                                 dst_ref.at[pl.ds(first(d), ROW_SPLIT), :], sem)


def _tile_wait(src_ref, dst_ref, sem):
    pltpu.make_async_copy(src_ref.at[pl.ds(0, dst_ref.shape[0]), :], dst_ref, sem).wait()


def _experts_kernel(pos1_ref, pos2_ref, te_ref, na_ref, hp_ref, zeros_ref, wg_ref, wu_ref, wd_ref, y_ref,
                    xa, xb, wg_s, wu_s, wd_s, row_tok, sem):
    i = pl.program_id(0)
    n_tiles = pl.num_programs(0)
    tr = MOE_ROW_TILE
    active = i < na_ref[0]
    changed = jnp.logical_or(i == 0, te_ref[i] != te_ref[jnp.maximum(i - 1, 0)])

    @pl.when(i == 0)
    def _():
        clear = pltpu.make_async_copy(zeros_ref, row_tok, sem)
        clear.start()
        clear.wait()

        def invert(t, carry):
            row_tok[pos1_ref[t]] = t
            row_tok[pos2_ref[t]] = t
            return carry

        lax.fori_loop(0, pos1_ref.shape[0], invert, 0, unroll=16)

        def pick(r, carry):
            xa[pl.ds(r, 1), :] = hp_ref[pl.ds(row_tok[r], 1), :]
            return carry

        lax.fori_loop(0, tr, pick, 0, unroll=8)

    @pl.when(jnp.logical_and(active, changed))
    def _():
        wg_s[...] = wg_ref[0].astype(BF16)
        wu_s[...] = wu_ref[0].astype(BF16)
        wd_s[...] = wd_ref[0].astype(BF16)

    def step(cur, nxt):
        base = jnp.minimum(i + 1, n_tiles - 1) * tr
        for r in range(tr):
            nxt[r:r + 1, :] = hp_ref[pl.ds(row_tok[base + r], 1), :]
        lo, hi = _unpack_bf16_pairs(cur[...])
        gate = _dot(lo, wg_s[0:HALF, :]) + _dot(hi, wg_s[HALF:, :])
        up = _dot(lo, wu_s[0:HALF, :]) + _dot(hi, wu_s[HALF:, :])
        hid = (gate * jax.nn.sigmoid(gate) * up).astype(BF16)
        _store_row_tiled(y_ref, _dot(hid, wd_s[...]))

    even = i % 2 == 0

    @pl.when(jnp.logical_and(active, even))
    def _():
        step(xa, xb)

    @pl.when(jnp.logical_and(active, jnp.logical_not(even)))
    def _():
        step(xb, xa)

    @pl.when(jnp.logical_not(active))
    def _():
        y_ref[...] = jnp.zeros_like(y_ref)


def _experts(hp, pos1, pos2, tile_expert, n_active, wg, wu, wd):
    tr = MOE_ROW_TILE
    n_tiles = tile_expert.shape[0]
    rows = n_tiles * tr
    wmap = lambda i, p1, p2, te, na: (te[i], 0, 0)
    return pl.pallas_call(
        _experts_kernel,
        grid_spec=pltpu.PrefetchScalarGridSpec(
            num_scalar_prefetch=4,
            grid=(n_tiles,),
            in_specs=[pl.BlockSpec(memory_space=pltpu.VMEM),
                      pl.BlockSpec(memory_space=pl.ANY),
                      pl.BlockSpec((1, D_MODEL, MOE_D_FF), wmap),
                      pl.BlockSpec((1, D_MODEL, MOE_D_FF), wmap),
                      pl.BlockSpec((1, MOE_D_FF, D_MODEL), wmap)],
            out_specs=pl.BlockSpec((tr * ROW_SPLIT, LANES), lambda i, p1, p2, te, na: (i, 0)),
            scratch_shapes=[pltpu.VMEM((tr, HALF), U32),
                            pltpu.VMEM((tr, HALF), U32),
                            pltpu.VMEM((D_MODEL, MOE_D_FF), BF16),
                            pltpu.VMEM((D_MODEL, MOE_D_FF), BF16),
                            pltpu.VMEM((MOE_D_FF, D_MODEL), BF16),
                            pltpu.SMEM((rows,), I32),
                            pltpu.SemaphoreType.DMA(())]),
        out_shape=jax.ShapeDtypeStruct((rows * ROW_SPLIT, LANES), F32),
        compiler_params=pltpu.CompilerParams(dimension_semantics=("arbitrary",),
                                             vmem_limit_bytes=EXPERTS_VMEM_LIMIT),
        name="moe_experts",
    )(pos1, pos2, tile_expert, n_active, hp, jnp.zeros((rows,), I32), wg, wu, wd)


def _combine_kernel(pos1_ref, pos2_ref, y_ref, h_ref, p1_ref, p2_ref, g_ref, b_ref, *rest, tm, tiled_copy):
    out_ref = rest[0]
    a1, a2, b1, b2, sem = rest[-5:]
    i = pl.program_id(0)
    last = pl.num_programs(0) - 1

    @pl.when(i == 0)
    def _():
        def issue(t, carry):
            _row_copy(y_ref, pos1_ref[t], a1, t, sem.at[0]).start()
            _row_copy(y_ref, pos2_ref[t], a2, t, sem.at[0]).start()
            return carry

        lax.fori_loop(0, tm, issue, 0)

    def step(c1, c2, cur_sem, n1, n2, nxt_sem):
        _tile_wait(y_ref, c1, cur_sem)
        _tile_wait(y_ref, c2, cur_sem)
        base = jnp.minimum(i + 1, last) * tm
        for t in range(tm):
            _row_copy(y_ref, pos1_ref[base + t], n1, t, nxt_sem).start(priority=0)
            _row_copy(y_ref, pos2_ref[base + t], n2, t, nxt_sem).start(priority=DMA_QUEUES - 1)
        ffn = p1_ref[...] * _load_row_tiled(c1) + p2_ref[...] * _load_row_tiled(c2)
        out = _layer_norm(DN_ALPHA * h_ref[...] + ffn, g_ref[...], b_ref[...])
        out_ref[...] = out
        if tiled_copy:
            _store_row_tiled(rest[1], out)

        @pl.when(i == last)
        def _():
            _tile_wait(y_ref, n1, nxt_sem)
            _tile_wait(y_ref, n2, nxt_sem)

    @pl.when(i % 2 == 0)
    def _():
        step(a1, a2, sem.at[0], b1, b2, sem.at[1])

    @pl.when(i % 2 == 1)
    def _():
        step(b1, b2, sem.at[1], a1, a2, sem.at[0])


def _combine(y, h, pos1, pos2, p1, p2, g, b, tm, tiled_copy):
    n = h.shape[0]
    tok = lambda i, a, c: (i, 0)
    fixed = lambda i, a, c: (0, 0)
    out_specs = [pl.BlockSpec((tm, D_MODEL), tok)]
    out_shape = [jax.ShapeDtypeStruct((n, D_MODEL), F32)]
    if tiled_copy:
        out_specs.append(pl.BlockSpec((tm * ROW_SPLIT, LANES), tok))
        out_shape.append(jax.ShapeDtypeStruct((n * ROW_SPLIT, LANES), F32))
    return pl.pallas_call(
        functools.partial(_combine_kernel, tm=tm, tiled_copy=tiled_copy),
        grid_spec=pltpu.PrefetchScalarGridSpec(
            num_scalar_prefetch=2,
            grid=(n // tm,),
            in_specs=[pl.BlockSpec(memory_space=pl.ANY),
                      pl.BlockSpec((tm, D_MODEL), tok),
                      pl.BlockSpec((tm, 1), tok), pl.BlockSpec((tm, 1), tok),
                      pl.BlockSpec((1, D_MODEL), fixed), pl.BlockSpec((1, D_MODEL), fixed)],
            out_specs=out_specs,
            scratch_shapes=[pltpu.VMEM((tm * ROW_SPLIT, LANES), F32)] * 4 + [pltpu.SemaphoreType.DMA((2,))]),
        out_shape=out_shape,
        compiler_params=_cparams("arbitrary"),
        name="moe_combine_ln",
    )(pos1, pos2, y, h, p1, p2, g, b)


def _plan_kernel(ints_ref, cnt_ref, pos_ref, meta_ref, *, layer, chunk):
    tr = MOE_ROW_TILE
    n = ints_ref.shape[1]
    ne = MOE_EXPERTS
    tiles = (cnt_ref[...] + (tr - 1.0)) * (1.0 / tr)
    tiles = tiles.astype(I32).astype(F32)
    lower = lax.broadcasted_iota(I32, (ne, ne), 0) >= lax.broadcasted_iota(I32, (ne, ne), 1)
    ends = _dot(jnp.where(lower, 1.0, 0.0).astype(BF16), tiles.astype(BF16))
    start_col = ((ends - tiles) * tr).astype(I32)[:, 0:1]
    r8 = lax.broadcasted_iota(I32, (8, chunk), 0)
    re = lax.broadcasted_iota(I32, (ne, chunk), 0)
    for c in range(n // chunk):
        blk = ints_ref[:, c * chunk:(c + 1) * chunk]
        s1 = jnp.sum(jnp.where(re == blk[0:1], start_col, 0), 0, keepdims=True)
        s2 = jnp.sum(jnp.where(re == blk[1:2], start_col, 0), 0, keepdims=True)
        pos_ref[:, c * chunk:(c + 1) * chunk] = jnp.where(
            r8 == 0, s1 + blk[2:3], jnp.where(r8 == 1, s2 + blk[3:4], 0))
    width = meta_ref.shape[1]
    tile_id = lax.broadcasted_iota(I32, (ne, width), 1).astype(F32)
    te = jnp.sum(jnp.where(ends[:, 0:1] <= tile_id, 1, 0), 0, keepdims=True)
    te = jnp.minimum(te, ne - 1) + layer * ne
    n_used = ends[ne - 1:ne, 0:1].astype(I32)
    rm = lax.broadcasted_iota(I32, (8, width), 0)
    meta_ref[...] = jnp.where(rm == 0, te, jnp.where(rm == 1, n_used, 0))


def _plan(ints, cnt, layer, n_tiles):
    n = ints.shape[1]
    width = -(-n_tiles // LANES) * LANES
    return pl.pallas_call(
        functools.partial(_plan_kernel, layer=layer, chunk=2048),
        out_shape=[jax.ShapeDtypeStruct((8, n), I32), jax.ShapeDtypeStruct((8, width), I32)],
        compiler_params=pltpu.CompilerParams(vmem_limit_bytes=VMEM_LIMIT),
        name="moe_plan",
    )(ints, cnt)


def _moe_layer(h, hp, layer, group_w, group_b, expert_w, expert_b, gate_w, up_w, down_w, ln_g, ln_b,
               tiled_copy):
    n = h.shape[0]
    ew = jnp.transpose(expert_w, (1, 0, 2)).reshape(D_MODEL, MOE_EXPERTS)
    w_r = jnp.zeros((D_MODEL, LANES), F32).at[:, 0:MOE_GROUPS].set(group_w).at[:, 8:8 + MOE_EXPERTS].set(ew)
    b_r = jnp.zeros((LANES,), F32).at[0:MOE_GROUPS].set(group_b).at[8:8 + MOE_EXPERTS].set(expert_b.reshape(-1))
    ints, flts, cnt = _router(h, w_r, b_r.reshape(LANES, 1), 512)
    tr = MOE_ROW_TILE
    n_tiles = (2 * n) // tr + MOE_EXPERTS
    pos, meta = _plan(ints, cnt, layer, n_tiles)
    pos1, pos2 = pos[0], pos[1]

    y = _experts(hp, pos1, pos2, meta[0, :n_tiles], meta[1, :1],
                 gate_w.reshape(-1, D_MODEL, MOE_D_FF),
                 up_w.reshape(-1, D_MODEL, MOE_D_FF),
                 down_w.reshape(-1, MOE_D_FF, D_MODEL))
    return _combine(y, h, pos1, pos2, flts[0].reshape(n, 1), flts[1].reshape(n, 1),
                    ln_g.reshape(1, -1), ln_b.reshape(1, -1), 256, tiled_copy)


def _pad_heads(w, axis):
    parts = []
    h0 = 0
    for _, _, nh in ATTN_GROUPS:
        sl = [slice(None)] * w.ndim
        sl[axis] = slice(h0 * ATTN_HEAD_DIM, (h0 + nh) * ATTN_HEAD_DIM)
        part = w[tuple(sl)]
        pad = [(0, 0)] * w.ndim
        pad[axis] = (0, ATTN_SLAB - nh * ATTN_HEAD_DIM)
        parts.append(jnp.pad(part, pad))
        h0 += nh
    return parts


def _in_proj_kernel(x_ref, w_ref, wdt_ref, z_ref, xbc_ref, dt_ref, *, z_tiles):
    j = pl.program_id(1)
    acc = _dot(x_ref[...].astype(BF16), w_ref[...])

    @pl.when(j < z_tiles)
    def _():
        z_ref[...] = acc

    @pl.when(j >= z_tiles)
    def _():
        xbc_ref[...] = acc

    @pl.when(j == 0)
    def _():
        dt_ref[...] = _dot_f32ish(x_ref[...], wdt_ref[...])


def _in_proj(x, w_zx, w_dt, tm, tn):
    m, k = x.shape
    z_tiles = SSM_D_INNER // tn
    n_tiles = w_zx.shape[1] // tn
    return pl.pallas_call(
        functools.partial(_in_proj_kernel, z_tiles=z_tiles),
        grid=(m // tm, n_tiles),
        in_specs=[pl.BlockSpec((tm, k), lambda i, j: (i, 0)),
                  pl.BlockSpec((k, tn), lambda i, j: (0, j)),
                  pl.BlockSpec((k, LANES), lambda i, j: (0, 0))],
        out_specs=[pl.BlockSpec((tm, tn), lambda i, j: (i, jnp.minimum(j, z_tiles - 1))),
                   pl.BlockSpec((tm, tn), lambda i, j: (i, jnp.maximum(j - z_tiles, 0))),
                   pl.BlockSpec((tm, LANES), lambda i, j: (i, 0))],
        out_shape=[jax.ShapeDtypeStruct((m, SSM_D_INNER), F32),
                   jax.ShapeDtypeStruct((m, SSM_CONV_DIM), F32),
                   jax.ShapeDtypeStruct((m, LANES), F32)],
        compiler_params=_cparams("parallel", "arbitrary"),
        name="ssm_in_proj",
    )(x, w_zx, w_dt)


def _ssd_layer(h, in_w, conv_w, conv_b, dt_bias, a_log, d_skip, norm_w, out_w, ln_g, ln_b, bsz, seq):
    split = SSM_D_INNER + SSM_CONV_DIM
    dt_w = jnp.pad(in_w[:, split:], ((0, 0), (0, LANES - SSM_HEADS)))
    z, xbc, dt_raw = _in_proj(h, in_w[:, :split].astype(BF16), dt_w, 1024, 1024)
    pad32 = lambda v: jnp.pad(v, (0, LANES - SSM_HEADS)).reshape(1, LANES)
    y = _ssd(z, xbc, dt_raw, conv_w, conv_b.reshape(1, -1), pad32(dt_bias), pad32(a_log),
             jnp.repeat(d_skip, SSM_HEAD_DIM).reshape(1, -1), norm_w.reshape(1, -1), bsz, seq)
    return _matmul_res_ln(y, out_w.astype(BF16), h, ln_g.reshape(1, -1), ln_b.reshape(1, -1), 512)


def _attn_layer(h, h8, kv_w, q_w, o_w, rel_bias, ln_g, ln_b, bsz, seq):
    width = ATTN_HEADS * ATTN_HEAD_DIM
    w_q = _pad_heads(q_w * (ATTN_HEAD_DIM ** -0.5), 1)
    w_k = _pad_heads(kv_w[:, :width], 1)
    w_v = _pad_heads(kv_w[:, width:], 1)
    outs, lses = [], []
    h0 = 0
    for gi, (_, dil, nh) in enumerate(ATTN_GROUPS):
        w_g = jnp.concatenate([w_q[gi], w_k[gi], w_v[gi]], axis=1).astype(BF16)
        if dil == 1:
            qkv = _matmul(h, w_g, BF16, 1024, 3 * ATTN_SLAB)
        else:
            qkv = _qkv_dilated(h8, w_g, dil, max(1024, ATTN_BLOCK * dil))
        bias_p, bias_c = _group_bias(rel_bias, h0, nh, dil)
        o, lse = _attention_group(qkv, bias_p, bias_c, gi, dil, nh, bsz, seq)
        outs.extend(o)
        lses.append(lse)
        h0 += nh
    w_o = jnp.concatenate(_pad_heads(o_w, 0), axis=0).astype(BF16)
    return _attn_out(outs, lses, w_o, h, ln_g.reshape(1, -1), ln_b.reshape(1, -1), 512)


def kernel(x, ssm_in_w, ssm_conv_w, ssm_conv_b, ssm_dt_bias, ssm_a_log, ssm_d, ssm_norm_w, ssm_out_w,
           kv_w, attn_q_w, attn_o_w, rel_bias, moe_group_w, moe_group_b, moe_expert_w, moe_expert_b,
           moe_gate_w, moe_up_w, moe_down_w, ln_g, ln_b):
    bsz, seq, d = x.shape
    h = x.reshape(bsz * seq, d)
    n_ssd = DEPTH // 2
    h8 = None
    for i in range(DEPTH):
        if i < n_ssd:
            h, hp = _ssd_layer(h, ssm_in_w[i], ssm_conv_w[i], ssm_conv_b[i], ssm_dt_bias[i], ssm_a_log[i],
                               ssm_d[i], ssm_norm_w[i], ssm_out_w[i], ln_g[i, 0], ln_b[i, 0], bsz, seq)
        else:
            j = i - n_ssd
            h, hp = _attn_layer(h, h8, kv_w, attn_q_w[j], attn_o_w[j], rel_bias, ln_g[i, 0], ln_b[i, 0],
                                bsz, seq)
        feeds_attention = n_ssd <= i + 1 < DEPTH
        res = _moe_layer(h, hp, i, moe_group_w[i], moe_group_b[i], moe_expert_w[i], moe_expert_b[i],
                         moe_gate_w, moe_up_w, moe_down_w, ln_g[i, 1], ln_b[i, 1], feeds_attention)
        h = res[0]
        h8 = res[1] if feeds_attention else None
    return h.reshape(bsz, seq, d)
```

```python
import functools
import math

import numpy as np
import jax
import jax.numpy as jnp
from jax import lax
from jax.experimental import pallas as pl
from jax.experimental.pallas import tpu as pltpu

F32 = jnp.float32
BF16 = jnp.bfloat16
I32 = jnp.int32

D_MODEL = 1024
DEPTH = 2
DN_ALPHA = (2 * DEPTH) ** 0.25
LN_EPS = 1e-5
LOG2_E = math.log2(math.e)

SSM_D_INNER = 2048
SSM_HEAD_DIM = 64
SSM_HEADS = 32
SSM_GROUPS = 4
SSM_STATE = 128
SSM_CONV = 4
SSM_CHUNK = 128
SSM_CONV_DIM = SSM_D_INNER + 2 * SSM_GROUPS * SSM_STATE

ATTN_HEAD_DIM = 64
ATTN_GROUPS = ((128, 1, 6), (512, 4, 5), (2048, 16, 5))
ATTN_HEADS = 16
ATTN_BLOCK = 128
N_BUCKETS = 32
MAX_DISTANCE = 2048
ATTN_SLAB = 384
NEG_BIG = -1e30

MOE_GROUPS = 4
MOE_EPG = 8
MOE_EXPERTS = MOE_GROUPS * MOE_EPG
MOE_D_FF = 512
MOE_ROW_TILE = 256

LANES = 128
DMA_QUEUES = 2
VMEM_LIMIT = 48 * 1024 * 1024
EXPERTS_VMEM_LIMIT = 56 * 1024 * 1024


def _cparams(*sem):
    return pltpu.CompilerParams(dimension_semantics=sem, vmem_limit_bytes=VMEM_LIMIT)


def _layer_norm(x, g, b):
    mu = jnp.mean(x, -1, keepdims=True)
    xc = x - mu
    var = jnp.mean(xc * xc, -1, keepdims=True)
    return xc * lax.rsqrt(var + LN_EPS) * g + b


def _split2(x):
    hi = x.astype(BF16)
    lo = (x - hi.astype(F32)).astype(BF16)
    return hi, lo


def _dot(a, b):
    return jnp.dot(a, b, preferred_element_type=F32)


def _dot_f32ish(a, b):
    ah, al = _split2(a)
    bh, bl = _split2(b)
    return _dot(ah, bh) + _dot(al, bh) + _dot(ah, bl)


def _mm_kernel(a_ref, b_ref, o_ref):
    o_ref[...] = _dot(a_ref[...].astype(BF16), b_ref[...]).astype(o_ref.dtype)


def _matmul(a, b, out_dtype, tm, tn):
    m, k = a.shape
    nc = b.shape[1]
    return pl.pallas_call(
        _mm_kernel,
        grid=(m // tm, nc // tn),
        in_specs=[pl.BlockSpec((tm, k), lambda i, j: (i, 0)),
                  pl.BlockSpec((k, tn), lambda i, j: (0, j))],
        out_specs=pl.BlockSpec((tm, tn), lambda i, j: (i, j)),
        out_shape=jax.ShapeDtypeStruct((m, nc), out_dtype),
        compiler_params=_cparams("parallel", "parallel"),
        name="matmul",
    )(a, b)


def _mm3_kernel(a_ref, b_ref, o_ref):
    o_ref[...] = _dot_f32ish(a_ref[...], b_ref[...])


def _matmul_f32ish(a, b, tm):
    m, k = a.shape
    nc = b.shape[1]
    return pl.pallas_call(
        _mm3_kernel,
        grid=(m // tm,),
        in_specs=[pl.BlockSpec((tm, k), lambda i: (i, 0)),
                  pl.BlockSpec((k, nc), lambda i: (0, 0))],
        out_specs=pl.BlockSpec((tm, nc), lambda i: (i, 0)),
        out_shape=jax.ShapeDtypeStruct((m, nc), F32),
        compiler_params=_cparams("parallel"),
        name="matmul_f32ish",
    )(a, b)


ROW_SPLIT = D_MODEL // LANES


def _store_row_tiled(ref, val):
    rows = val.shape[0]
    for c in range(ROW_SPLIT):
        ref[pl.ds(c, rows, stride=ROW_SPLIT), :] = val[:, c * LANES:(c + 1) * LANES]


def _load_row_tiled(ref):
    rows = ref.shape[0] // ROW_SPLIT
    return jnp.concatenate([ref[pl.ds(c, rows, stride=ROW_SPLIT), :] for c in range(ROW_SPLIT)], axis=1)


HALF = D_MODEL // 2
U32 = jnp.uint32
HI16 = 0xFFFF0000


def _pack_bf16_pairs(x):
    bits = lax.bitcast_convert_type(x.astype(BF16).astype(F32), U32)
    return (bits[:, :HALF] >> 16) | (bits[:, HALF:] & U32(HI16))


def _unpack_bf16_pairs(w):
    lo = lax.bitcast_convert_type(w << 16, F32).astype(BF16)
    hi = lax.bitcast_convert_type(w & U32(HI16), F32).astype(BF16)
    return lo, hi


def _mm_ln_kernel(a_ref, w_ref, h_ref, g_ref, b_ref, o_ref, op_ref):
    acc = _dot(a_ref[...], w_ref[...])
    out = _layer_norm(DN_ALPHA * h_ref[...] + acc, g_ref[...], b_ref[...])
    o_ref[...] = out
    op_ref[...] = _pack_bf16_pairs(out)


def _matmul_res_ln(a, w, h, g, b, tm):
    m, k = a.shape
    d = w.shape[1]
    return pl.pallas_call(
        _mm_ln_kernel,
        grid=(m // tm,),
        in_specs=[pl.BlockSpec((tm, k), lambda i: (i, 0)),
                  pl.BlockSpec((k, d), lambda i: (0, 0)),
                  pl.BlockSpec((tm, d), lambda i: (i, 0)),
                  pl.BlockSpec((1, d), lambda i: (0, 0)),
                  pl.BlockSpec((1, d), lambda i: (0, 0))],
        out_specs=[pl.BlockSpec((tm, d), lambda i: (i, 0)),
                   pl.BlockSpec((tm, HALF), lambda i: (i, 0))],
        out_shape=[jax.ShapeDtypeStruct((m, d), F32),
                   jax.ShapeDtypeStruct((m, HALF), U32)],
        compiler_params=_cparams("parallel"),
        name="matmul_res_ln",
    )(a, w, h, g, b)


def _ssd_kernel(z_ref, xbc_ref, dt_ref, cw_ref, cb_ref, dtb_ref, alog_ref, dsk_ref, nw_ref,
                y_ref, xe_ref, st_ref):
    q = SSM_CHUNK

    @pl.when(pl.program_id(1) == 0)
    def _():
        xe_ref[0:8, :] = jnp.zeros((8, SSM_CONV_DIM), F32)
        st_ref[...] = jnp.zeros_like(st_ref)

    u = xbc_ref[...]
    xe_ref[8:8 + q, :] = u
    w = cw_ref[...]
    conv = (cb_ref[...] + w[3:4] * u + w[2:3] * xe_ref[7:7 + q, :]
            + w[1:2] * xe_ref[6:6 + q, :] + w[0:1] * xe_ref[5:5 + q, :])
    xe_ref[0:8, :] = xe_ref[q:q + 8, :]
    act = conv * jax.nn.sigmoid(conv)

    pre = dt_ref[...] + dtb_ref[...]
    dt = jnp.maximum(pre, 0.0) + jnp.log(1.0 + jnp.exp(-jnp.abs(pre)))
    adt = dt * (-jnp.exp(alog_ref[...]) * LOG2_E)

    row = lax.broadcasted_iota(I32, (q, q), 0)
    col = lax.broadcasted_iota(I32, (q, q), 1)
    tril = row >= col
    tri_b = jnp.where(tril, 1.0, 0.0).astype(BF16)
    a_hi = adt.astype(BF16)
    r1 = adt - a_hi.astype(F32)
    a_mid = r1.astype(BF16)
    a_lo = (r1 - a_mid.astype(F32)).astype(BF16)
    acs = _dot(tri_b, a_hi) + _dot(tri_b, a_mid) + _dot(tri_b, a_lo)
    acs_t = acs.T
    acs_dt_t = acs_t - jnp.log(dt.T) * LOG2_E
    eacs = jnp.exp2(acs)
    left = col < SSM_HEAD_DIM

    for g in range(SSM_GROUPS):
        b0 = SSM_D_INNER + g * SSM_STATE
        c0 = SSM_D_INNER + SSM_GROUPS * SSM_STATE + g * SSM_STATE
        bm = act[:, b0:b0 + SSM_STATE]
        cm = act[:, c0:c0 + SSM_STATE].astype(BF16)
        cb = lax.dot_general(cm, bm.astype(BF16), (((1,), (1,)), ((), ())),
                             preferred_element_type=F32)
        bm_t = bm.T
        gs = g * 512
        y_off = _dot(cm, st_ref[:, gs:gs + 512].astype(BF16))
        slabs = []
        for pr in range(4):
            ha = g * 8 + pr * 2
            hb = ha + 1
            cs = gs + pr * LANES
            x2 = act[:, cs:cs + LANES]
            x2b = x2.astype(BF16)
            ys, ups = [], []
            for h in (ha, hb):
                a_col = acs[:, h:h + 1]
                a_src = acs_dt_t[h:h + 1, :]
                decay = jnp.where(tril, jnp.exp2(a_col - a_src), 0.0)
                ys.append(_dot((cb * decay).astype(BF16), x2b))
                to_end = jnp.exp2(acs_t[h:h + 1, q - 1:q] - a_src)
                ups.append(_dot((bm_t * to_end).astype(BF16), x2b))
            y_diag = jnp.where(left, ys[0], ys[1])
            upd = jnp.where(left, ups[0], ups[1])
            e2 = jnp.where(left, eacs[:, ha:ha + 1], eacs[:, hb:hb + 1])
            cd = jnp.where(left[0:1, :], eacs[q - 1:q, ha:ha + 1], eacs[q - 1:q, hb:hb + 1])
            y2 = y_diag + y_off[:, pr * LANES:(pr + 1) * LANES] * e2 + dsk_ref[:, cs:cs + LANES] * x2
            st_ref[:, cs:cs + LANES] = st_ref[:, cs:cs + LANES] * cd + upd
            slabs.append(y2)
        yg = jnp.concatenate(slabs, axis=1)
        zg = z_ref[:, gs:gs + 512]
        yg = yg * (zg * jax.nn.sigmoid(zg))
        ms = jnp.mean(yg * yg, -1, keepdims=True)
        y_ref[:, gs:gs + 512] = (yg * lax.rsqrt(ms + LN_EPS) * nw_ref[:, gs:gs + 512]).astype(y_ref.dtype)


def _ssd(z, xbc, dt_raw, conv_w, conv_b, dt_bias, a_log, d_rep, norm_w, bsz, seq):
    n = z.shape[0]
    nchunk = seq // SSM_CHUNK
    q = SSM_CHUNK
    tok = lambda b, c: (b * nchunk + c, 0)
    fixed = lambda b, c: (0, 0)
    return pl.pallas_call(
        _ssd_kernel,
        grid=(bsz, nchunk),
        in_specs=[pl.BlockSpec((q, SSM_D_INNER), tok),
                  pl.BlockSpec((q, SSM_CONV_DIM), tok),
                  pl.BlockSpec((q, LANES), tok),
                  pl.BlockSpec((SSM_CONV, SSM_CONV_DIM), fixed),
                  pl.BlockSpec((1, SSM_CONV_DIM), fixed),
                  pl.BlockSpec((1, LANES), fixed),
                  pl.BlockSpec((1, LANES), fixed),
                  pl.BlockSpec((1, SSM_D_INNER), fixed),
                  pl.BlockSpec((1, SSM_D_INNER), fixed)],
        out_specs=pl.BlockSpec((q, SSM_D_INNER), tok),
        out_shape=jax.ShapeDtypeStruct((n, SSM_D_INNER), BF16),
        scratch_shapes=[pltpu.VMEM((q + 8, SSM_CONV_DIM), F32),
                        pltpu.VMEM((SSM_STATE, SSM_D_INNER), F32)],
        compiler_params=_cparams("parallel", "arbitrary"),
        name="ssd_chunk",
    )(z, xbc, dt_raw, conv_w, conv_b, dt_bias, a_log, d_rep, norm_w)


def _qkv_dilated_kernel(h8_ref, w_ref, o_ref, *, dil):
    span = ATTN_BLOCK * dil
    per_dot = 4
    rows = []
    for blk in range(o_ref.shape[0] // span):
        for r in range(dil):
            first = (blk * span + r) * ROW_SPLIT
            rows.append(jnp.concatenate(
                [h8_ref[pl.ds(first + c, ATTN_BLOCK, stride=ROW_SPLIT * dil), :] for c in range(ROW_SPLIT)],
                axis=1).astype(BF16))
    for k in range(0, len(rows), per_dot):
        x = jnp.concatenate(rows[k:k + per_dot], axis=0)
        o_ref[k * ATTN_BLOCK:(k + per_dot) * ATTN_BLOCK, :] = _dot(x, w_ref[...]).astype(o_ref.dtype)


def _qkv_dilated(h8, w, dil, tm):
    n = h8.shape[0] // ROW_SPLIT
    nc = w.shape[1]
    return pl.pallas_call(
        functools.partial(_qkv_dilated_kernel, dil=dil),
        grid=(n // tm,),
        in_specs=[pl.BlockSpec((tm * ROW_SPLIT, LANES), lambda i: (i, 0)),
                  pl.BlockSpec((D_MODEL, nc), lambda i: (0, 0))],
        out_specs=pl.BlockSpec((tm, nc), lambda i: (i, 0)),
        out_shape=jax.ShapeDtypeStruct((n, nc), BF16),
        compiler_params=_cparams("parallel"),
        name=f"qkv_dil{dil}",
    )(h8, w)


def _attn_kernel(*refs, nh, dil, has_prev):
    if has_prev:
        q_ref, kp_ref, kc_ref, vp_ref, vc_ref, bp_ref, bc_ref = refs[:7]
        out_refs = refs[7:]
        first_pen = jnp.where(pl.program_id(1) > 0, 0.0, NEG_BIG)
    else:
        q_ref, kc_ref, vc_ref, bc_ref = refs[:4]
        out_refs = refs[4:]
    s_scr, p_scr, max_scr, den_scr = out_refs[-4:]
    o_refs, lse_ref = out_refs[:-5], out_refs[-5]
    ones = jnp.ones((s_scr.shape[2], LANES), BF16)
    lane = lax.broadcasted_iota(I32, (ATTN_BLOCK, LANES), 1)
    left = lane < ATTN_HEAD_DIM
    zero = jnp.zeros((), BF16)
    nt = (((1,), (1,)), ((), ()))

    def residue(r, carry):
        rows = pl.ds(pl.multiple_of(r * ATTN_BLOCK, ATTN_BLOCK), ATTN_BLOCK)
        dst = pl.ds(r, ATTN_BLOCK, stride=dil)
        for hh in range(nh):
            cols = pl.ds(hh // 2 * LANES, LANES)
            qm = jnp.where(left if hh % 2 == 0 else ~left, q_ref[rows, cols], zero)
            s_c = lax.dot_general(qm, kc_ref[rows, cols], nt, preferred_element_type=F32) + bc_ref[hh]
            if has_prev:
                s_p = lax.dot_general(qm, kp_ref[rows, cols], nt, preferred_element_type=F32)
                s_scr[hh, :, 0:ATTN_BLOCK] = s_p + (bp_ref[hh] + first_pen)
                s_scr[hh, :, ATTN_BLOCK:] = s_c
            else:
                s_scr[hh] = s_c
        for hh in range(nh):
            s = s_scr[hh]
            m = jnp.max(s, -1, keepdims=True)
            p = jnp.exp(s - m)
            p_scr[hh] = p.astype(BF16)
            max_scr[hh] = m
            if not has_prev:
                den_scr[hh] = jnp.sum(p, -1, keepdims=True)
        lse_sum = jnp.zeros((ATTN_BLOCK, 1), F32)
        for pr in range(ATTN_SLAB // LANES):
            cols = pl.ds(pr * LANES, LANES)
            halves = []
            for hh in (pr * 2, pr * 2 + 1):
                if hh >= nh:
                    halves.append(jnp.zeros((ATTN_BLOCK, LANES), F32))
                    continue
                if has_prev:
                    den = _dot(p_scr[hh], ones)
                    den1 = den[:, 0:1]
                    o = (_dot(p_scr[hh, :, 0:ATTN_BLOCK], vp_ref[rows, cols])
                         + _dot(p_scr[hh, :, ATTN_BLOCK:], vc_ref[rows, cols]))
                else:
                    den = den1 = den_scr[hh]
                    o = _dot(p_scr[hh], vc_ref[rows, cols])
                halves.append(o * (1.0 / den))
                lse_sum = lse_sum + (max_scr[hh] + jnp.log(den1))
            o_refs[pr][dst, :] = jnp.where(left, halves[0], halves[1])
        lse_ref[dst, :] = jnp.broadcast_to(lse_sum * (1.0 / nh), (ATTN_BLOCK, LANES))
        return carry

    lax.fori_loop(0, dil, residue, 0)


def _attention_group(qkv, bias_p, bias_c, gi, dil, nh, bsz, seq):
    span = ATTN_BLOCK * dil
    nb = seq // span
    has_prev = nb > 1
    blk = (span, ATTN_SLAB)
    cur = lambda which: (lambda b, n: (b * nb + n, which))
    prev = lambda which: (lambda b, n: (b * nb + jnp.maximum(n - 1, 0), which))
    fixed = lambda b, n: (0, 0, 0)
    tab = pl.BlockSpec((nh, ATTN_BLOCK, ATTN_BLOCK), fixed)
    if has_prev:
        in_specs = [pl.BlockSpec(blk, cur(0)), pl.BlockSpec(blk, prev(1)), pl.BlockSpec(blk, cur(1)),
                    pl.BlockSpec(blk, prev(2)), pl.BlockSpec(blk, cur(2)), tab, tab]
        args = (qkv, qkv, qkv, qkv, qkv, bias_p, bias_c)
    else:
        in_specs = [pl.BlockSpec(blk, cur(0)), pl.BlockSpec(blk, cur(1)), pl.BlockSpec(blk, cur(2)), tab]
        args = (qkv, qkv, qkv, bias_c)
    n_out = ATTN_SLAB // LANES + 1
    keys = 2 * ATTN_BLOCK if has_prev else ATTN_BLOCK
    outs = pl.pallas_call(
        functools.partial(_attn_kernel, nh=nh, dil=dil, has_prev=has_prev),
        grid=(bsz, nb),
        in_specs=in_specs,
        out_specs=[pl.BlockSpec((span, LANES), lambda b, n: (b * nb + n, 0))] * n_out,
        out_shape=[jax.ShapeDtypeStruct((bsz * seq, LANES), F32)] * n_out,
        scratch_shapes=[pltpu.VMEM((nh, ATTN_BLOCK, keys), F32),
                        pltpu.VMEM((nh, ATTN_BLOCK, keys), BF16),
                        pltpu.VMEM((nh, ATTN_BLOCK, 1), F32),
                        pltpu.VMEM((nh, ATTN_BLOCK, 1), F32)],
        compiler_params=_cparams("parallel", "parallel"),
        name=f"dilated_attn_g{gi}",
    )(*args)
    return outs[:-1], outs[-1]


def _attn_out_kernel(*refs):
    n_pairs = ATTN_SLAB // LANES
    n_groups = len(ATTN_GROUPS)
    o_refs = refs[:n_groups * n_pairs]
    l_refs = refs[n_groups * n_pairs:n_groups * (n_pairs + 1)]
    w_ref, h_ref, g_ref, b_ref, out_ref, outp_ref = refs[n_groups * (n_pairs + 1):]
    ls = [r[:, 0:1] for r in l_refs]
    m = jnp.maximum(jnp.maximum(ls[0], ls[1]), ls[2])
    es = [jnp.exp(v - m) for v in ls]
    scale = n_groups / (es[0] + es[1] + es[2])
    slabs = []
    for gi in range(n_groups):
        wt = es[gi] * scale
        slabs.extend((r[...] * wt).astype(BF16) for r in o_refs[gi * n_pairs:(gi + 1) * n_pairs])
    acc = _dot(jnp.concatenate(slabs, axis=1), w_ref[...])
    out = _layer_norm(DN_ALPHA * h_ref[...] + acc, g_ref[...], b_ref[...])
    out_ref[...] = out
    outp_ref[...] = _pack_bf16_pairs(out)


def _attn_out(outs, lses, w, h, g, b, tm):
    n = h.shape[0]
    tok = lambda i: (i, 0)
    fixed2 = lambda i: (0, 0)
    return pl.pallas_call(
        _attn_out_kernel,
        grid=(n // tm,),
        in_specs=[pl.BlockSpec((tm, LANES), tok)] * (len(outs) + len(lses))
                 + [pl.BlockSpec((len(ATTN_GROUPS) * ATTN_SLAB, D_MODEL), lambda i: (0, 0)),
                    pl.BlockSpec((tm, D_MODEL), tok),
                    pl.BlockSpec((1, D_MODEL), fixed2), pl.BlockSpec((1, D_MODEL), fixed2)],
        out_specs=[pl.BlockSpec((tm, D_MODEL), tok), pl.BlockSpec((tm, HALF), tok)],
        out_shape=[jax.ShapeDtypeStruct((n, D_MODEL), F32),
                   jax.ShapeDtypeStruct((n, HALF), U32)],
        compiler_params=_cparams("parallel"),
        name="attn_out_ln",
    )(*outs, *lses, w, h, g, b)


def _t5_bucket(dist):
    max_exact = N_BUCKETS // 2
    n = np.maximum(dist, 1).astype(np.float64)
    large = max_exact + (np.log(n / max_exact) / np.log(MAX_DISTANCE / max_exact)
                         * (N_BUCKETS - max_exact)).astype(np.int32)
    large = np.minimum(large, N_BUCKETS - 1)
    return np.where(dist < max_exact, dist, large).astype(np.int32)


def _group_bias(rel_bias, h0, nh, dil):
    qi = np.arange(ATTN_BLOCK)[:, None]
    ki = np.arange(ATTN_BLOCK)[None, :]
    tabs = []
    for delta, band in ((qi + ATTN_BLOCK - ki, ki >= qi), (qi - ki, ki <= qi)):
        bucket = _t5_bucket(np.clip(delta, 0, None) * dil)
        onehot = (bucket[..., None] == np.arange(N_BUCKETS)).astype(np.float32)
        t = jnp.einsum("qkb,bh->hqk", onehot, rel_bias[:, h0:h0 + nh], precision=lax.Precision.HIGHEST)
        tabs.append(jnp.where(band[None], t, NEG_BIG).astype(F32))
    return tabs


def _router_kernel(h_ref, w_ref, b_ref, ints_ref, flts_ref, cnt_ref, carry_ref, *, tm):
    @pl.when(pl.program_id(0) == 0)
    def _():
        carry_ref[...] = jnp.zeros_like(carry_ref)

    lt = _dot_f32ish(h_ref[...], w_ref[...]).T + b_ref[...]
    gl = lt[0:MOE_GROUPS]
    r4 = lax.broadcasted_iota(I32, (MOE_GROUPS, tm), 0)
    gmax = jnp.max(gl, 0, keepdims=True)
    gidx = jnp.min(jnp.where(gl == gmax, r4, MOE_GROUPS), 0, keepdims=True)
    gval = 1.0 / jnp.sum(jnp.exp(gl - gmax), 0, keepdims=True)

    esel = jnp.zeros((MOE_EPG, tm), F32)
    for g in range(MOE_GROUPS):
        esel = jnp.where(gidx == g, lt[8 + g * MOE_EPG:8 + (g + 1) * MOE_EPG], esel)
    r8 = lax.broadcasted_iota(I32, (MOE_EPG, tm), 0)
    v1 = jnp.max(esel, 0, keepdims=True)
    i1 = jnp.min(jnp.where(esel == v1, r8, MOE_EPG), 0, keepdims=True)
    rest = jnp.where(r8 == i1, -jnp.inf, esel)
    v2 = jnp.max(rest, 0, keepdims=True)
    i2 = jnp.min(jnp.where(rest == v2, r8, MOE_EPG), 0, keepdims=True)
    t = jnp.exp(v2 - v1)
    p1 = gval / (1.0 + t)
    p2 = p1 * t
    e1 = gidx * MOE_EPG + i1
    e2 = gidx * MOE_EPG + i2

    r32 = lax.broadcasted_iota(I32, (MOE_EXPERTS, tm), 0)
    oh1 = r32 == e1
    oh2 = r32 == e2
    oh = jnp.where(oh1 | oh2, 1.0, 0.0)
    tr = lax.broadcasted_iota(I32, (tm, tm), 0)
    tc = lax.broadcasted_iota(I32, (tm, tm), 1)
    before = jnp.where(tr < tc, 1.0, 0.0).astype(BF16)
    base = _dot(oh.astype(BF16), before) + carry_ref[:, 0:1]
    rank1 = jnp.sum(jnp.where(oh1, base, 0.0), 0, keepdims=True).astype(I32)
    rank2 = jnp.sum(jnp.where(oh2, base, 0.0), 0, keepdims=True).astype(I32)
    carry_ref[...] = carry_ref[...] + jnp.sum(oh, 1, keepdims=True)
    cnt_ref[...] = carry_ref[...]

    ints_ref[...] = jnp.where(r8 == 0, e1, jnp.where(r8 == 1, e2, jnp.where(r8 == 2, rank1,
                              jnp.where(r8 == 3, rank2, 0))))
    flts_ref[...] = jnp.where(r8 == 0, p1, jnp.where(r8 == 1, p2, 0.0))


def _router(h, w_r, b_r, tm):
    n = h.shape[0]
    return pl.pallas_call(
        functools.partial(_router_kernel, tm=tm),
        grid=(n // tm,),
        in_specs=[pl.BlockSpec((tm, D_MODEL), lambda i: (i, 0)),
                  pl.BlockSpec((D_MODEL, LANES), lambda i: (0, 0)),
                  pl.BlockSpec((LANES, 1), lambda i: (0, 0))],
        out_specs=[pl.BlockSpec((8, tm), lambda i: (0, i)),
                   pl.BlockSpec((8, tm), lambda i: (0, i)),
                   pl.BlockSpec((MOE_EXPERTS, LANES), lambda i: (0, 0))],
        out_shape=[jax.ShapeDtypeStruct((8, n), I32),
                   jax.ShapeDtypeStruct((8, n), F32),
                   jax.ShapeDtypeStruct((MOE_EXPERTS, LANES), F32)],
        scratch_shapes=[pltpu.VMEM((MOE_EXPERTS, LANES), F32)],
        compiler_params=_cparams("arbitrary"),
        name="moe_router",
    )(h, w_r, b_r)


def _row_copy(src_ref, s, dst_ref, d, sem):
    def first(r):
        return r * ROW_SPLIT if isinstance(r, int) else pl.multiple_of(r * ROW_SPLIT, ROW_SPLIT)

    return pltpu.make_async_copy(src_ref.at[pl.ds(first(s), ROW_SPLIT), :],
                                 dst_ref.at[pl.ds(first(d), ROW_SPLIT), :], sem)


def _tile_wait(src_ref, dst_ref, sem):
    pltpu.make_async_copy(src_ref.at[pl.ds(0, dst_ref.shape[0]), :], dst_ref, sem).wait()


def _experts_kernel(pos1_ref, pos2_ref, te_ref, nxt_ref, na_ref, hp_ref, zeros_ref, wg_ref, wu_ref, wd_ref,
                    y_ref, xa, xb, wg_f, wu_f, wd_f, wg_s, wu_s, wd_s, row_tok, slot_ref, sem, wsem):
    i = pl.program_id(0)
    n_tiles = pl.num_programs(0)
    tr = MOE_ROW_TILE
    active = i < na_ref[0]
    changed = jnp.logical_or(i == 0, te_ref[i] != te_ref[jnp.maximum(i - 1, 0)])

    def weight_copies(expert, slot):
        return [pltpu.make_async_copy(src.at[expert], dst.at[slot], wsem.at[slot])
                for src, dst in ((wg_ref, wg_f), (wu_ref, wu_f), (wd_ref, wd_f))]

    @pl.when(i == 0)
    def _():
        slot_ref[0] = 0
        for c in weight_copies(te_ref[0], 0):
            c.start()
        clear = pltpu.make_async_copy(zeros_ref, row_tok, sem)
        clear.start()
        clear.wait()

        def invert(t, carry):
            row_tok[pos1_ref[t]] = t
            row_tok[pos2_ref[t]] = t
            return carry

        lax.fori_loop(0, pos1_ref.shape[0], invert, 0, unroll=16)

        def pick(r, carry):
            xa[pl.ds(r, 1), :] = hp_ref[pl.ds(row_tok[r], 1), :]
            return carry

        lax.fori_loop(0, tr, pick, 0, unroll=8)

    @pl.when(jnp.logical_and(active, changed))
    def _():
        slot = slot_ref[0]
        for c in weight_copies(te_ref[i], slot):
            c.wait()
        following = nxt_ref[i]

        @pl.when(following >= 0)
        def _():
            for c in weight_copies(following, 1 - slot):
                c.start()

        wg_s[...] = wg_f[slot].astype(BF16)
        wu_s[...] = wu_f[slot].astype(BF16)
        wd_s[...] = wd_f[slot].astype(BF16)
        slot_ref[0] = 1 - slot

    def step(cur, nxt):
        base = jnp.minimum(i + 1, n_tiles - 1) * tr
        for r in range(tr):
            nxt[r:r + 1, :] = hp_ref[pl.ds(row_tok[base + r], 1), :]
        lo, hi = _unpack_bf16_pairs(cur[...])
        gate = _dot(lo, wg_s[0:HALF, :]) + _dot(hi, wg_s[HALF:, :])
        up = _dot(lo, wu_s[0:HALF, :]) + _dot(hi, wu_s[HALF:, :])
        hid = (gate * jax.nn.sigmoid(gate) * up).astype(BF16)
        _store_row_tiled(y_ref, _dot(hid, wd_s[...]))

    even = i % 2 == 0

    @pl.when(jnp.logical_and(active, even))
    def _():
        step(xa, xb)

    @pl.when(jnp.logical_and(active, jnp.logical_not(even)))
    def _():
        step(xb, xa)

    @pl.when(jnp.logical_not(active))
    def _():
        y_ref[...] = jnp.zeros_like(y_ref)


def _experts(hp, pos1, pos2, tile_expert, next_expert, n_active, wg, wu, wd):
    tr = MOE_ROW_TILE
    n_tiles = tile_expert.shape[0]
    rows = n_tiles * tr
    hbm = pl.BlockSpec(memory_space=pl.ANY)
    return pl.pallas_call(
        _experts_kernel,
        grid_spec=pltpu.PrefetchScalarGridSpec(
            num_scalar_prefetch=5,
            grid=(n_tiles,),
            in_specs=[pl.BlockSpec(memory_space=pltpu.VMEM), hbm, hbm, hbm, hbm],
            out_specs=pl.BlockSpec((tr * ROW_SPLIT, LANES), lambda i, *_: (i, 0)),
            scratch_shapes=[pltpu.VMEM((tr, HALF), U32),
                            pltpu.VMEM((tr, HALF), U32),
                            pltpu.VMEM((2, D_MODEL, MOE_D_FF), F32),
                            pltpu.VMEM((2, D_MODEL, MOE_D_FF), F32),
                            pltpu.VMEM((2, MOE_D_FF, D_MODEL), F32),
                            pltpu.VMEM((D_MODEL, MOE_D_FF), BF16),
                            pltpu.VMEM((D_MODEL, MOE_D_FF), BF16),
                            pltpu.VMEM((MOE_D_FF, D_MODEL), BF16),
                            pltpu.SMEM((rows,), I32),
                            pltpu.SMEM((1,), I32),
                            pltpu.SemaphoreType.DMA(()),
                            pltpu.SemaphoreType.DMA((2,))]),
        out_shape=jax.ShapeDtypeStruct((rows * ROW_SPLIT, LANES), F32),
        compiler_params=pltpu.CompilerParams(dimension_semantics=("arbitrary",),
                                             vmem_limit_bytes=EXPERTS_VMEM_LIMIT),
        name="moe_experts",
    )(pos1, pos2, tile_expert, next_expert, n_active, hp, jnp.zeros((rows,), I32), wg, wu, wd)


def _combine_kernel(pos1_ref, pos2_ref, y_ref, h_ref, p1_ref, p2_ref, g_ref, b_ref, *rest, tm, tiled_copy):
    out_ref = rest[0]
    a1, a2, b1, b2, sem = rest[-5:]
    i = pl.program_id(0)
    last = pl.num_programs(0) - 1

    @pl.when(i == 0)
    def _():
        def issue(t, carry):
            _row_copy(y_ref, pos1_ref[t], a1, t, sem.at[0]).start()
            _row_copy(y_ref, pos2_ref[t], a2, t, sem.at[0]).start()
            return carry

        lax.fori_loop(0, tm, issue, 0)

    def step(c1, c2, cur_sem, n1, n2, nxt_sem):
        _tile_wait(y_ref, c1, cur_sem)
        _tile_wait(y_ref, c2, cur_sem)
        base = jnp.minimum(i + 1, last) * tm
        for t in range(tm):
            _row_copy(y_ref, pos1_ref[base + t], n1, t, nxt_sem).start(priority=0)
            _row_copy(y_ref, pos2_ref[base + t], n2, t, nxt_sem).start(priority=DMA_QUEUES - 1)
        ffn = p1_ref[...] * _load_row_tiled(c1) + p2_ref[...] * _load_row_tiled(c2)
        out = _layer_norm(DN_ALPHA * h_ref[...] + ffn, g_ref[...], b_ref[...])
        out_ref[...] = out
        if tiled_copy:
            _store_row_tiled(rest[1], out)

        @pl.when(i == last)
        def _():
            _tile_wait(y_ref, n1, nxt_sem)
            _tile_wait(y_ref, n2, nxt_sem)

    @pl.when(i % 2 == 0)
    def _():
        step(a1, a2, sem.at[0], b1, b2, sem.at[1])

    @pl.when(i % 2 == 1)
    def _():
        step(b1, b2, sem.at[1], a1, a2, sem.at[0])


def _combine(y, h, pos1, pos2, p1, p2, g, b, tm, tiled_copy):
    n = h.shape[0]
    tok = lambda i, a, c: (i, 0)
    fixed = lambda i, a, c: (0, 0)
    out_specs = [pl.BlockSpec((tm, D_MODEL), tok)]
    out_shape = [jax.ShapeDtypeStruct((n, D_MODEL), F32)]
    if tiled_copy:
        out_specs.append(pl.BlockSpec((tm * ROW_SPLIT, LANES), tok))
        out_shape.append(jax.ShapeDtypeStruct((n * ROW_SPLIT, LANES), F32))
    return pl.pallas_call(
        functools.partial(_combine_kernel, tm=tm, tiled_copy=tiled_copy),
        grid_spec=pltpu.PrefetchScalarGridSpec(
            num_scalar_prefetch=2,
            grid=(n // tm,),
            in_specs=[pl.BlockSpec(memory_space=pl.ANY),
                      pl.BlockSpec((tm, D_MODEL), tok),
                      pl.BlockSpec((tm, 1), tok), pl.BlockSpec((tm, 1), tok),
                      pl.BlockSpec((1, D_MODEL), fixed), pl.BlockSpec((1, D_MODEL), fixed)],
            out_specs=out_specs,
            scratch_shapes=[pltpu.VMEM((tm * ROW_SPLIT, LANES), F32)] * 4 + [pltpu.SemaphoreType.DMA((2,))]),
        out_shape=out_shape,
        compiler_params=_cparams("arbitrary"),
        name="moe_combine_ln",
    )(pos1, pos2, y, h, p1, p2, g, b)


def _plan_kernel(ints_ref, cnt_ref, pos_ref, meta_ref, *, layer, chunk):
    tr = MOE_ROW_TILE
    n = ints_ref.shape[1]
    ne = MOE_EXPERTS
    tiles = (cnt_ref[...] + (tr - 1.0)) * (1.0 / tr)
    tiles = tiles.astype(I32).astype(F32)
    lower = lax.broadcasted_iota(I32, (ne, ne), 0) >= lax.broadcasted_iota(I32, (ne, ne), 1)
    ends = _dot(jnp.where(lower, 1.0, 0.0).astype(BF16), tiles.astype(BF16))
    start_col = ((ends - tiles) * tr).astype(I32)[:, 0:1]
    r8 = lax.broadcasted_iota(I32, (8, chunk), 0)
    re = lax.broadcasted_iota(I32, (ne, chunk), 0)
    for c in range(n // chunk):
        blk = ints_ref[:, c * chunk:(c + 1) * chunk]
        s1 = jnp.sum(jnp.where(re == blk[0:1], start_col, 0), 0, keepdims=True)
        s2 = jnp.sum(jnp.where(re == blk[1:2], start_col, 0), 0, keepdims=True)
        pos_ref[:, c * chunk:(c + 1) * chunk] = jnp.where(
            r8 == 0, s1 + blk[2:3], jnp.where(r8 == 1, s2 + blk[3:4], 0))
    width = meta_ref.shape[1]
    tile_id = lax.broadcasted_iota(I32, (ne, width), 1).astype(F32)
    te = jnp.sum(jnp.where(ends[:, 0:1] <= tile_id, 1, 0), 0, keepdims=True)
    te = jnp.minimum(te, ne - 1)
    expert = lax.broadcasted_iota(I32, (ne, width), 0)
    later = jnp.logical_and(expert > te, tiles[:, 0:1] > 0.0)
    nxt = jnp.min(jnp.where(later, expert, ne), 0, keepdims=True)
    nxt = jnp.where(nxt < ne, nxt + layer * ne, -1)
    n_used = ends[ne - 1:ne, 0:1].astype(I32)
    rm = lax.broadcasted_iota(I32, (8, width), 0)
    meta_ref[...] = jnp.where(rm == 0, te + layer * ne,
                              jnp.where(rm == 1, n_used, jnp.where(rm == 2, nxt, 0)))


def _plan(ints, cnt, layer, n_tiles):
    n = ints.shape[1]
    width = -(-n_tiles // LANES) * LANES
    return pl.pallas_call(
        functools.partial(_plan_kernel, layer=layer, chunk=2048),
        out_shape=[jax.ShapeDtypeStruct((8, n), I32), jax.ShapeDtypeStruct((8, width), I32)],
        compiler_params=pltpu.CompilerParams(vmem_limit_bytes=VMEM_LIMIT),
        name="moe_plan",
    )(ints, cnt)


def _moe_layer(h, hp, layer, group_w, group_b, expert_w, expert_b, gate_w, up_w, down_w, ln_g, ln_b,
               tiled_copy):
    n = h.shape[0]
    ew = jnp.transpose(expert_w, (1, 0, 2)).reshape(D_MODEL, MOE_EXPERTS)
    w_r = jnp.zeros((D_MODEL, LANES), F32).at[:, 0:MOE_GROUPS].set(group_w).at[:, 8:8 + MOE_EXPERTS].set(ew)
    b_r = jnp.zeros((LANES,), F32).at[0:MOE_GROUPS].set(group_b).at[8:8 + MOE_EXPERTS].set(expert_b.reshape(-1))
    ints, flts, cnt = _router(h, w_r, b_r.reshape(LANES, 1), 512)
    tr = MOE_ROW_TILE
    n_tiles = (2 * n) // tr + MOE_EXPERTS
    pos, meta = _plan(ints, cnt, layer, n_tiles)
    pos1, pos2 = pos[0], pos[1]

    y = _experts(hp, pos1, pos2, meta[0, :n_tiles], meta[2, :n_tiles], meta[1, :1],
                 gate_w.reshape(-1, D_MODEL, MOE_D_FF),
                 up_w.reshape(-1, D_MODEL, MOE_D_FF),
                 down_w.reshape(-1, MOE_D_FF, D_MODEL))
    return _combine(y, h, pos1, pos2, flts[0].reshape(n, 1), flts[1].reshape(n, 1),
                    ln_g.reshape(1, -1), ln_b.reshape(1, -1), 256, tiled_copy)


def _pad_heads(w, axis):
    parts = []
    h0 = 0
    for _, _, nh in ATTN_GROUPS:
        sl = [slice(None)] * w.ndim
        sl[axis] = slice(h0 * ATTN_HEAD_DIM, (h0 + nh) * ATTN_HEAD_DIM)
        part = w[tuple(sl)]
        pad = [(0, 0)] * w.ndim
        pad[axis] = (0, ATTN_SLAB - nh * ATTN_HEAD_DIM)
        parts.append(jnp.pad(part, pad))
        h0 += nh
    return parts


def _in_proj_kernel(x_ref, w_ref, wdt_ref, z_ref, xbc_ref, dt_ref, *, z_tiles):
    j = pl.program_id(1)
    acc = _dot(x_ref[...].astype(BF16), w_ref[...])

    @pl.when(j < z_tiles)
    def _():
        z_ref[...] = acc

    @pl.when(j >= z_tiles)
    def _():
        xbc_ref[...] = acc

    @pl.when(j == 0)
    def _():
        dt_ref[...] = _dot_f32ish(x_ref[...], wdt_ref[...])


def _in_proj(x, w_zx, w_dt, tm, tn):
    m, k = x.shape
    z_tiles = SSM_D_INNER // tn
    n_tiles = w_zx.shape[1] // tn
    return pl.pallas_call(
        functools.partial(_in_proj_kernel, z_tiles=z_tiles),
        grid=(m // tm, n_tiles),
        in_specs=[pl.BlockSpec((tm, k), lambda i, j: (i, 0)),
                  pl.BlockSpec((k, tn), lambda i, j: (0, j)),
                  pl.BlockSpec((k, LANES), lambda i, j: (0, 0))],
        out_specs=[pl.BlockSpec((tm, tn), lambda i, j: (i, jnp.minimum(j, z_tiles - 1))),
                   pl.BlockSpec((tm, tn), lambda i, j: (i, jnp.maximum(j - z_tiles, 0))),
                   pl.BlockSpec((tm, LANES), lambda i, j: (i, 0))],
        out_shape=[jax.ShapeDtypeStruct((m, SSM_D_INNER), F32),
                   jax.ShapeDtypeStruct((m, SSM_CONV_DIM), F32),
                   jax.ShapeDtypeStruct((m, LANES), F32)],
        compiler_params=_cparams("parallel", "arbitrary"),
        name="ssm_in_proj",
    )(x, w_zx, w_dt)


def _ssd_layer(h, in_w, conv_w, conv_b, dt_bias, a_log, d_skip, norm_w, out_w, ln_g, ln_b, bsz, seq):
    split = SSM_D_INNER + SSM_CONV_DIM
    dt_w = jnp.pad(in_w[:, split:], ((0, 0), (0, LANES - SSM_HEADS)))
    z, xbc, dt_raw = _in_proj(h, in_w[:, :split].astype(BF16), dt_w, 1024, 1024)
    pad32 = lambda v: jnp.pad(v, (0, LANES - SSM_HEADS)).reshape(1, LANES)
    y = _ssd(z, xbc, dt_raw, conv_w, conv_b.reshape(1, -1), pad32(dt_bias), pad32(a_log),
             jnp.repeat(d_skip, SSM_HEAD_DIM).reshape(1, -1), norm_w.reshape(1, -1), bsz, seq)
    return _matmul_res_ln(y, out_w.astype(BF16), h, ln_g.reshape(1, -1), ln_b.reshape(1, -1), 512)


def _attn_layer(h, h8, kv_w, q_w, o_w, rel_bias, ln_g, ln_b, bsz, seq):
    width = ATTN_HEADS * ATTN_HEAD_DIM
    w_q = _pad_heads(q_w * (ATTN_HEAD_DIM ** -0.5), 1)
    w_k = _pad_heads(kv_w[:, :width], 1)
    w_v = _pad_heads(kv_w[:, width:], 1)
    outs, lses = [], []
    h0 = 0
    for gi, (_, dil, nh) in enumerate(ATTN_GROUPS):
        w_g = jnp.concatenate([w_q[gi], w_k[gi], w_v[gi]], axis=1).astype(BF16)
        if dil == 1:
            qkv = _matmul(h, w_g, BF16, 1024, 3 * ATTN_SLAB)
        else:
            qkv = _qkv_dilated(h8, w_g, dil, max(1024, ATTN_BLOCK * dil))
        bias_p, bias_c = _group_bias(rel_bias, h0, nh, dil)
        o, lse = _attention_group(qkv, bias_p, bias_c, gi, dil, nh, bsz, seq)
        outs.extend(o)
        lses.append(lse)
        h0 += nh
    w_o = jnp.concatenate(_pad_heads(o_w, 0), axis=0).astype(BF16)
    return _attn_out(outs, lses, w_o, h, ln_g.reshape(1, -1), ln_b.reshape(1, -1), 512)


def kernel(x, ssm_in_w, ssm_conv_w, ssm_conv_b, ssm_dt_bias, ssm_a_log, ssm_d, ssm_norm_w, ssm_out_w,
           kv_w, attn_q_w, attn_o_w, rel_bias, moe_group_w, moe_group_b, moe_expert_w, moe_expert_b,
           moe_gate_w, moe_up_w, moe_down_w, ln_g, ln_b):
    bsz, seq, d = x.shape
    h = x.reshape(bsz * seq, d)
    n_ssd = DEPTH // 2
    h8 = None
    for i in range(DEPTH):
        if i < n_ssd:
            h, hp = _ssd_layer(h, ssm_in_w[i], ssm_conv_w[i], ssm_conv_b[i], ssm_dt_bias[i], ssm_a_log[i],
                               ssm_d[i], ssm_norm_w[i], ssm_out_w[i], ln_g[i, 0], ln_b[i, 0], bsz, seq)
        else:
            j = i - n_ssd
            h, hp = _attn_layer(h, h8, kv_w, attn_q_w[j], attn_o_w[j], rel_bias, ln_g[i, 0], ln_b[i, 0],
                                bsz, seq)
        feeds_attention = n_ssd <= i + 1 < DEPTH
        res = _moe_layer(h, hp, i, moe_group_w[i], moe_group_b[i], moe_expert_w[i], moe_expert_b[i],
                         moe_gate_w, moe_up_w, moe_down_w, ln_g[i, 1], ln_b[i, 1], feeds_attention)
        h = res[0]
        h8 = res[1] if feeds_attention else None
    return h.reshape(bsz, seq, d)
```

```python
import functools
import math

import numpy as np
import jax
import jax.numpy as jnp
from jax import lax
from jax.experimental import pallas as pl
from jax.experimental.pallas import tpu as pltpu

F32 = jnp.float32
BF16 = jnp.bfloat16
I32 = jnp.int32

D_MODEL = 1024
DEPTH = 2
DN_ALPHA = (2 * DEPTH) ** 0.25
LN_EPS = 1e-5
LOG2_E = math.log2(math.e)

SSM_D_INNER = 2048
SSM_HEAD_DIM = 64
SSM_HEADS = 32
SSM_GROUPS = 4
SSM_STATE = 128
SSM_CONV = 4
SSM_CHUNK = 128
SSM_CONV_DIM = SSM_D_INNER + 2 * SSM_GROUPS * SSM_STATE

ATTN_HEAD_DIM = 64
ATTN_GROUPS = ((128, 1, 6), (512, 4, 5), (2048, 16, 5))
ATTN_HEADS = 16
ATTN_BLOCK = 128
N_BUCKETS = 32
MAX_DISTANCE = 2048
ATTN_SLAB = 384
NEG_BIG = -1e30

MOE_GROUPS = 4
MOE_EPG = 8
MOE_EXPERTS = MOE_GROUPS * MOE_EPG
MOE_D_FF = 512
MOE_ROW_TILE = 256

LANES = 128
DMA_QUEUES = 2
VMEM_LIMIT = 48 * 1024 * 1024
EXPERTS_VMEM_LIMIT = 56 * 1024 * 1024


def _cparams(*sem):
    return pltpu.CompilerParams(dimension_semantics=sem, vmem_limit_bytes=VMEM_LIMIT)


def _layer_norm(x, g, b):
    mu = jnp.mean(x, -1, keepdims=True)
    xc = x - mu
    var = jnp.mean(xc * xc, -1, keepdims=True)
    return xc * lax.rsqrt(var + LN_EPS) * g + b


def _split2(x):
    hi = x.astype(BF16)
    lo = (x - hi.astype(F32)).astype(BF16)
    return hi, lo


def _dot(a, b):
    return jnp.dot(a, b, preferred_element_type=F32)


def _dot_f32ish(a, b):
    ah, al = _split2(a)
    bh, bl = _split2(b)
    return _dot(ah, bh) + _dot(al, bh) + _dot(ah, bl)


def _mm_kernel(a_ref, b_ref, o_ref):
    o_ref[...] = _dot(a_ref[...].astype(BF16), b_ref[...]).astype(o_ref.dtype)


def _matmul(a, b, out_dtype, tm, tn):
    m, k = a.shape
    nc = b.shape[1]
    return pl.pallas_call(
        _mm_kernel,
        grid=(m // tm, nc // tn),
        in_specs=[pl.BlockSpec((tm, k), lambda i, j: (i, 0)),
                  pl.BlockSpec((k, tn), lambda i, j: (0, j))],
        out_specs=pl.BlockSpec((tm, tn), lambda i, j: (i, j)),
        out_shape=jax.ShapeDtypeStruct((m, nc), out_dtype),
        compiler_params=_cparams("parallel", "parallel"),
        name="matmul",
    )(a, b)


def _mm3_kernel(a_ref, b_ref, o_ref):
    o_ref[...] = _dot_f32ish(a_ref[...], b_ref[...])


def _matmul_f32ish(a, b, tm):
    m, k = a.shape
    nc = b.shape[1]
    return pl.pallas_call(
        _mm3_kernel,
        grid=(m // tm,),
        in_specs=[pl.BlockSpec((tm, k), lambda i: (i, 0)),
                  pl.BlockSpec((k, nc), lambda i: (0, 0))],
        out_specs=pl.BlockSpec((tm, nc), lambda i: (i, 0)),
        out_shape=jax.ShapeDtypeStruct((m, nc), F32),
        compiler_params=_cparams("parallel"),
        name="matmul_f32ish",
    )(a, b)


ROW_SPLIT = D_MODEL // LANES


def _store_row_tiled(ref, val):
    rows = val.shape[0]
    for c in range(ROW_SPLIT):
        ref[pl.ds(c, rows, stride=ROW_SPLIT), :] = val[:, c * LANES:(c + 1) * LANES]


def _load_row_tiled(ref):
    rows = ref.shape[0] // ROW_SPLIT
    return jnp.concatenate([ref[pl.ds(c, rows, stride=ROW_SPLIT), :] for c in range(ROW_SPLIT)], axis=1)


HALF = D_MODEL // 2
U32 = jnp.uint32
HI16 = 0xFFFF0000


def _pack_bf16_pairs(x):
    bits = lax.bitcast_convert_type(x.astype(BF16).astype(F32), U32)
    return (bits[:, :HALF] >> 16) | (bits[:, HALF:] & U32(HI16))


def _unpack_bf16_pairs(w):
    lo = lax.bitcast_convert_type(w << 16, F32).astype(BF16)
    hi = lax.bitcast_convert_type(w & U32(HI16), F32).astype(BF16)
    return lo, hi


def _mm_ln_kernel(a_ref, w_ref, h_ref, g_ref, b_ref, o_ref, op_ref):
    acc = _dot(a_ref[...], w_ref[...])
    out = _layer_norm(DN_ALPHA * h_ref[...] + acc, g_ref[...], b_ref[...])
    o_ref[...] = out
    op_ref[...] = _pack_bf16_pairs(out)


def _matmul_res_ln(a, w, h, g, b, tm):
    m, k = a.shape
    d = w.shape[1]
    return pl.pallas_call(
        _mm_ln_kernel,
        grid=(m // tm,),
        in_specs=[pl.BlockSpec((tm, k), lambda i: (i, 0)),
                  pl.BlockSpec((k, d), lambda i: (0, 0)),
                  pl.BlockSpec((tm, d), lambda i: (i, 0)),
                  pl.BlockSpec((1, d), lambda i: (0, 0)),
                  pl.BlockSpec((1, d), lambda i: (0, 0))],
        out_specs=[pl.BlockSpec((tm, d), lambda i: (i, 0)),
                   pl.BlockSpec((tm, HALF), lambda i: (i, 0))],
        out_shape=[jax.ShapeDtypeStruct((m, d), F32),
                   jax.ShapeDtypeStruct((m, HALF), U32)],
        compiler_params=_cparams("parallel"),
        name="matmul_res_ln",
    )(a, w, h, g, b)


SSD_CHUNKS_PER_STEP = 2


def _ssd_kernel(z_ref, xbc_ref, dt_ref, cw_ref, cb_ref, dtb_ref, alog_ref, dsk_ref, nw_ref,
                y_ref, xe_ref, st_ref):
    @pl.when(pl.program_id(1) == 0)
    def _():
        xe_ref[0:8, :] = jnp.zeros((8, SSM_CONV_DIM), F32)
        st_ref[...] = jnp.zeros_like(st_ref)

    for sub in range(SSD_CHUNKS_PER_STEP):
        _ssd_chunk(pl.ds(sub * SSM_CHUNK, SSM_CHUNK), z_ref, xbc_ref, dt_ref, cw_ref, cb_ref, dtb_ref,
                   alog_ref, dsk_ref, nw_ref, y_ref, xe_ref, st_ref)


def _ssd_chunk(rows, z_ref, xbc_ref, dt_ref, cw_ref, cb_ref, dtb_ref, alog_ref, dsk_ref, nw_ref,
               y_ref, xe_ref, st_ref):
    q = SSM_CHUNK
    u = xbc_ref[rows, :]
    xe_ref[8:8 + q, :] = u
    w = cw_ref[...]
    conv = (cb_ref[...] + w[3:4] * u + w[2:3] * xe_ref[7:7 + q, :]
            + w[1:2] * xe_ref[6:6 + q, :] + w[0:1] * xe_ref[5:5 + q, :])
    xe_ref[0:8, :] = xe_ref[q:q + 8, :]
    act = conv * jax.nn.sigmoid(conv)

    pre = dt_ref[rows, :] + dtb_ref[...]
    dt = jnp.maximum(pre, 0.0) + jnp.log(1.0 + jnp.exp(-jnp.abs(pre)))
    adt = dt * (-jnp.exp(alog_ref[...]) * LOG2_E)

    row = lax.broadcasted_iota(I32, (q, q), 0)
    col = lax.broadcasted_iota(I32, (q, q), 1)
    tril = row >= col
    tri_b = jnp.where(tril, 1.0, 0.0).astype(BF16)
    a_hi = adt.astype(BF16)
    r1 = adt - a_hi.astype(F32)
    a_mid = r1.astype(BF16)
    a_lo = (r1 - a_mid.astype(F32)).astype(BF16)
    acs = _dot(tri_b, a_hi) + _dot(tri_b, a_mid) + _dot(tri_b, a_lo)
    acs_t = acs.T
    acs_dt_t = acs_t - jnp.log(dt.T) * LOG2_E
    eacs = jnp.exp2(acs)
    left = col < SSM_HEAD_DIM

    for g in range(SSM_GROUPS):
        b0 = SSM_D_INNER + g * SSM_STATE
        c0 = SSM_D_INNER + SSM_GROUPS * SSM_STATE + g * SSM_STATE
        bm = act[:, b0:b0 + SSM_STATE]
        cm = act[:, c0:c0 + SSM_STATE].astype(BF16)
        cb = lax.dot_general(cm, bm.astype(BF16), (((1,), (1,)), ((), ())),
                             preferred_element_type=F32)
        bm_t = bm.T
        gs = g * 512
        y_off = _dot(cm, st_ref[:, gs:gs + 512].astype(BF16))
        slabs = []
        for pr in range(4):
            ha = g * 8 + pr * 2
            hb = ha + 1
            cs = gs + pr * LANES
            x2 = act[:, cs:cs + LANES]
            x2b = x2.astype(BF16)
            ys, ups = [], []
            for h in (ha, hb):
                a_col = acs[:, h:h + 1]
                a_src = acs_dt_t[h:h + 1, :]
                decay = jnp.where(tril, jnp.exp2(a_col - a_src), 0.0)
                ys.append(_dot((cb * decay).astype(BF16), x2b))
                to_end = jnp.exp2(acs_t[h:h + 1, q - 1:q] - a_src)
                ups.append(_dot((bm_t * to_end).astype(BF16), x2b))
            y_diag = jnp.where(left, ys[0], ys[1])
            upd = jnp.where(left, ups[0], ups[1])
            e2 = jnp.where(left, eacs[:, ha:ha + 1], eacs[:, hb:hb + 1])
            cd = jnp.where(left[0:1, :], eacs[q - 1:q, ha:ha + 1], eacs[q - 1:q, hb:hb + 1])
            y2 = y_diag + y_off[:, pr * LANES:(pr + 1) * LANES] * e2 + dsk_ref[:, cs:cs + LANES] * x2
            st_ref[:, cs:cs + LANES] = st_ref[:, cs:cs + LANES] * cd + upd
            slabs.append(y2)
        yg = jnp.concatenate(slabs, axis=1)
        zg = z_ref[rows, gs:gs + 512]
        yg = yg * (zg * jax.nn.sigmoid(zg))
        ms = jnp.mean(yg * yg, -1, keepdims=True)
        y_ref[rows, gs:gs + 512] = (yg * lax.rsqrt(ms + LN_EPS) * nw_ref[:, gs:gs + 512]).astype(y_ref.dtype)


def _ssd(z, xbc, dt_raw, conv_w, conv_b, dt_bias, a_log, d_rep, norm_w, bsz, seq):
    n = z.shape[0]
    q = SSM_CHUNK * SSD_CHUNKS_PER_STEP
    nchunk = seq // q
    tok = lambda b, c: (b * nchunk + c, 0)
    fixed = lambda b, c: (0, 0)
    return pl.pallas_call(
        _ssd_kernel,
        grid=(bsz, nchunk),
        in_specs=[pl.BlockSpec((q, SSM_D_INNER), tok),
                  pl.BlockSpec((q, SSM_CONV_DIM), tok),
                  pl.BlockSpec((q, LANES), tok),
                  pl.BlockSpec((SSM_CONV, SSM_CONV_DIM), fixed),
                  pl.BlockSpec((1, SSM_CONV_DIM), fixed),
                  pl.BlockSpec((1, LANES), fixed),
                  pl.BlockSpec((1, LANES), fixed),
                  pl.BlockSpec((1, SSM_D_INNER), fixed),
                  pl.BlockSpec((1, SSM_D_INNER), fixed)],
        out_specs=pl.BlockSpec((q, SSM_D_INNER), tok),
        out_shape=jax.ShapeDtypeStruct((n, SSM_D_INNER), BF16),
        scratch_shapes=[pltpu.VMEM((SSM_CHUNK + 8, SSM_CONV_DIM), F32),
                        pltpu.VMEM((SSM_STATE, SSM_D_INNER), F32)],
        compiler_params=_cparams("parallel", "arbitrary"),
        name="ssd_chunk",
    )(z, xbc, dt_raw, conv_w, conv_b, dt_bias, a_log, d_rep, norm_w)


def _qkv_dilated_kernel(h8_ref, w_ref, o_ref, *, dil):
    span = ATTN_BLOCK * dil
    per_dot = 4
    rows = []
    for blk in range(o_ref.shape[0] // span):
        for r in range(dil):
            first = (blk * span + r) * ROW_SPLIT
            rows.append(jnp.concatenate(
                [h8_ref[pl.ds(first + c, ATTN_BLOCK, stride=ROW_SPLIT * dil), :] for c in range(ROW_SPLIT)],
                axis=1).astype(BF16))
    for k in range(0, len(rows), per_dot):
        x = jnp.concatenate(rows[k:k + per_dot], axis=0)
        o_ref[k * ATTN_BLOCK:(k + per_dot) * ATTN_BLOCK, :] = _dot(x, w_ref[...]).astype(o_ref.dtype)


def _qkv_dilated(h8, w, dil, tm):
    n = h8.shape[0] // ROW_SPLIT
    nc = w.shape[1]
    return pl.pallas_call(
        functools.partial(_qkv_dilated_kernel, dil=dil),
        grid=(n // tm,),
        in_specs=[pl.BlockSpec((tm * ROW_SPLIT, LANES), lambda i: (i, 0)),
                  pl.BlockSpec((D_MODEL, nc), lambda i: (0, 0))],
        out_specs=pl.BlockSpec((tm, nc), lambda i: (i, 0)),
        out_shape=jax.ShapeDtypeStruct((n, nc), BF16),
        compiler_params=_cparams("parallel"),
        name=f"qkv_dil{dil}",
    )(h8, w)


def _attn_kernel(*refs, nh, dil, has_prev):
    if has_prev:
        q_ref, kp_ref, kc_ref, vp_ref, vc_ref, bp_ref, bc_ref = refs[:7]
        out_refs = refs[7:]
        first_pen = jnp.where(pl.program_id(1) > 0, 0.0, NEG_BIG)
    else:
        q_ref, kc_ref, vc_ref, bc_ref = refs[:4]
        out_refs = refs[4:]
    s_scr, p_scr, max_scr, den_scr = out_refs[-4:]
    o_refs, lse_ref = out_refs[:-5], out_refs[-5]
    ones = jnp.ones((s_scr.shape[2], LANES), BF16)
    lane = lax.broadcasted_iota(I32, (ATTN_BLOCK, LANES), 1)
    left = lane < ATTN_HEAD_DIM
    zero = jnp.zeros((), BF16)
    nt = (((1,), (1,)), ((), ()))

    def residue(r, carry):
        rows = pl.ds(pl.multiple_of(r * ATTN_BLOCK, ATTN_BLOCK), ATTN_BLOCK)
        dst = pl.ds(r, ATTN_BLOCK, stride=dil)
        for hh in range(nh):
            cols = pl.ds(hh // 2 * LANES, LANES)
            qm = jnp.where(left if hh % 2 == 0 else ~left, q_ref[rows, cols], zero)
            s_c = lax.dot_general(qm, kc_ref[rows, cols], nt, preferred_element_type=F32) + bc_ref[hh]
            if has_prev:
                s_p = lax.dot_general(qm, kp_ref[rows, cols], nt, preferred_element_type=F32)
                s_scr[hh, :, 0:ATTN_BLOCK] = s_p + (bp_ref[hh] + first_pen)
                s_scr[hh, :, ATTN_BLOCK:] = s_c
            else:
                s_scr[hh] = s_c
        for hh in range(nh):
            s = s_scr[hh]
            m = jnp.max(s, -1, keepdims=True)
            p = jnp.exp(s - m)
            p_scr[hh] = p.astype(BF16)
            max_scr[hh] = m
            if not has_prev:
                den_scr[hh] = jnp.sum(p, -1, keepdims=True)
        lse_sum = jnp.zeros((ATTN_BLOCK, 1), F32)
        for pr in range(ATTN_SLAB // LANES):
            cols = pl.ds(pr * LANES, LANES)
            halves = []
            for hh in (pr * 2, pr * 2 + 1):
                if hh >= nh:
                    halves.append(jnp.zeros((ATTN_BLOCK, LANES), F32))
                    continue
                if has_prev:
                    den = _dot(p_scr[hh], ones)
                    den1 = den[:, 0:1]
                    o = (_dot(p_scr[hh, :, 0:ATTN_BLOCK], vp_ref[rows, cols])
                         + _dot(p_scr[hh, :, ATTN_BLOCK:], vc_ref[rows, cols]))
                else:
                    den = den1 = den_scr[hh]
                    o = _dot(p_scr[hh], vc_ref[rows, cols])
                halves.append(o * (1.0 / den))
                lse_sum = lse_sum + (max_scr[hh] + jnp.log(den1))
            o_refs[pr][dst, :] = jnp.where(left, halves[0], halves[1])
        lse_ref[dst, :] = jnp.broadcast_to(lse_sum * (1.0 / nh), (ATTN_BLOCK, LANES))
        return carry

    lax.fori_loop(0, dil, residue, 0)


def _attention_group(qkv, bias_p, bias_c, gi, dil, nh, bsz, seq):
    span = ATTN_BLOCK * dil
    nb = seq // span
    has_prev = nb > 1
    blk = (span, ATTN_SLAB)
    cur = lambda which: (lambda b, n: (b * nb + n, which))
    prev = lambda which: (lambda b, n: (b * nb + jnp.maximum(n - 1, 0), which))
    fixed = lambda b, n: (0, 0, 0)
    tab = pl.BlockSpec((nh, ATTN_BLOCK, ATTN_BLOCK), fixed)
    if has_prev:
        in_specs = [pl.BlockSpec(blk, cur(0)), pl.BlockSpec(blk, prev(1)), pl.BlockSpec(blk, cur(1)),
                    pl.BlockSpec(blk, prev(2)), pl.BlockSpec(blk, cur(2)), tab, tab]
        args = (qkv, qkv, qkv, qkv, qkv, bias_p, bias_c)
    else:
        in_specs = [pl.BlockSpec(blk, cur(0)), pl.BlockSpec(blk, cur(1)), pl.BlockSpec(blk, cur(2)), tab]
        args = (qkv, qkv, qkv, bias_c)
    n_out = ATTN_SLAB // LANES + 1
    keys = 2 * ATTN_BLOCK if has_prev else ATTN_BLOCK
    outs = pl.pallas_call(
        functools.partial(_attn_kernel, nh=nh, dil=dil, has_prev=has_prev),
        grid=(bsz, nb),
        in_specs=in_specs,
        out_specs=[pl.BlockSpec((span, LANES), lambda b, n: (b * nb + n, 0))] * n_out,
        out_shape=[jax.ShapeDtypeStruct((bsz * seq, LANES), F32)] * n_out,
        scratch_shapes=[pltpu.VMEM((nh, ATTN_BLOCK, keys), F32),
                        pltpu.VMEM((nh, ATTN_BLOCK, keys), BF16),
                        pltpu.VMEM((nh, ATTN_BLOCK, 1), F32),
                        pltpu.VMEM((nh, ATTN_BLOCK, 1), F32)],
        compiler_params=_cparams("parallel", "parallel"),
        name=f"dilated_attn_g{gi}",
    )(*args)
    return outs[:-1], outs[-1]


def _attn_out_kernel(*refs):
    n_pairs = ATTN_SLAB // LANES
    n_groups = len(ATTN_GROUPS)
    o_refs = refs[:n_groups * n_pairs]
    l_refs = refs[n_groups * n_pairs:n_groups * (n_pairs + 1)]
    w_ref, h_ref, g_ref, b_ref, out_ref, outp_ref = refs[n_groups * (n_pairs + 1):]
    ls = [r[:, 0:1] for r in l_refs]
    m = jnp.maximum(jnp.maximum(ls[0], ls[1]), ls[2])
    es = [jnp.exp(v - m) for v in ls]
    scale = n_groups / (es[0] + es[1] + es[2])
    slabs = []
    for gi in range(n_groups):
        wt = es[gi] * scale
        slabs.extend((r[...] * wt).astype(BF16) for r in o_refs[gi * n_pairs:(gi + 1) * n_pairs])
    acc = _dot(jnp.concatenate(slabs, axis=1), w_ref[...])
    out = _layer_norm(DN_ALPHA * h_ref[...] + acc, g_ref[...], b_ref[...])
    out_ref[...] = out
    outp_ref[...] = _pack_bf16_pairs(out)


def _attn_out(outs, lses, w, h, g, b, tm):
    n = h.shape[0]
    tok = lambda i: (i, 0)
    fixed2 = lambda i: (0, 0)
    return pl.pallas_call(
        _attn_out_kernel,
        grid=(n // tm,),
        in_specs=[pl.BlockSpec((tm, LANES), tok)] * (len(outs) + len(lses))
                 + [pl.BlockSpec((len(ATTN_GROUPS) * ATTN_SLAB, D_MODEL), lambda i: (0, 0)),
                    pl.BlockSpec((tm, D_MODEL), tok),
                    pl.BlockSpec((1, D_MODEL), fixed2), pl.BlockSpec((1, D_MODEL), fixed2)],
        out_specs=[pl.BlockSpec((tm, D_MODEL), tok), pl.BlockSpec((tm, HALF), tok)],
        out_shape=[jax.ShapeDtypeStruct((n, D_MODEL), F32),
                   jax.ShapeDtypeStruct((n, HALF), U32)],
        compiler_params=_cparams("parallel"),
        name="attn_out_ln",
    )(*outs, *lses, w, h, g, b)


def _t5_bucket(dist):
    max_exact = N_BUCKETS // 2
    n = np.maximum(dist, 1).astype(np.float64)
    large = max_exact + (np.log(n / max_exact) / np.log(MAX_DISTANCE / max_exact)
                         * (N_BUCKETS - max_exact)).astype(np.int32)
    large = np.minimum(large, N_BUCKETS - 1)
    return np.where(dist < max_exact, dist, large).astype(np.int32)


def _group_bias(rel_bias, h0, nh, dil):
    qi = np.arange(ATTN_BLOCK)[:, None]
    ki = np.arange(ATTN_BLOCK)[None, :]
    tabs = []
    for delta, band in ((qi + ATTN_BLOCK - ki, ki >= qi), (qi - ki, ki <= qi)):
        bucket = _t5_bucket(np.clip(delta, 0, None) * dil)
        onehot = (bucket[..., None] == np.arange(N_BUCKETS)).astype(np.float32)
        t = jnp.einsum("qkb,bh->hqk", onehot, rel_bias[:, h0:h0 + nh], precision=lax.Precision.HIGHEST)
        tabs.append(jnp.where(band[None], t, NEG_BIG).astype(F32))
    return tabs


def _router_kernel(h_ref, w_ref, b_ref, ints_ref, flts_ref, cnt_ref, carry_ref, *, tm):
    @pl.when(pl.program_id(0) == 0)
    def _():
        carry_ref[...] = jnp.zeros_like(carry_ref)

    lt = _dot_f32ish(h_ref[...], w_ref[...]).T + b_ref[...]
    gl = lt[0:MOE_GROUPS]
    r4 = lax.broadcasted_iota(I32, (MOE_GROUPS, tm), 0)
    gmax = jnp.max(gl, 0, keepdims=True)
    gidx = jnp.min(jnp.where(gl == gmax, r4, MOE_GROUPS), 0, keepdims=True)
    gval = 1.0 / jnp.sum(jnp.exp(gl - gmax), 0, keepdims=True)

    esel = jnp.zeros((MOE_EPG, tm), F32)
    for g in range(MOE_GROUPS):
        esel = jnp.where(gidx == g, lt[8 + g * MOE_EPG:8 + (g + 1) * MOE_EPG], esel)
    r8 = lax.broadcasted_iota(I32, (MOE_EPG, tm), 0)
    v1 = jnp.max(esel, 0, keepdims=True)
    i1 = jnp.min(jnp.where(esel == v1, r8, MOE_EPG), 0, keepdims=True)
    rest = jnp.where(r8 == i1, -jnp.inf, esel)
    v2 = jnp.max(rest, 0, keepdims=True)
    i2 = jnp.min(jnp.where(rest == v2, r8, MOE_EPG), 0, keepdims=True)
    t = jnp.exp(v2 - v1)
    p1 = gval / (1.0 + t)
    p2 = p1 * t
    e1 = gidx * MOE_EPG + i1
    e2 = gidx * MOE_EPG + i2

    r32 = lax.broadcasted_iota(I32, (MOE_EXPERTS, tm), 0)
    oh1 = r32 == e1
    oh2 = r32 == e2
    oh = jnp.where(oh1 | oh2, 1.0, 0.0)
    tr = lax.broadcasted_iota(I32, (tm, tm), 0)
    tc = lax.broadcasted_iota(I32, (tm, tm), 1)
    before = jnp.where(tr < tc, 1.0, 0.0).astype(BF16)
    base = _dot(oh.astype(BF16), before) + carry_ref[:, 0:1]
    rank1 = jnp.sum(jnp.where(oh1, base, 0.0), 0, keepdims=True).astype(I32)
    rank2 = jnp.sum(jnp.where(oh2, base, 0.0), 0, keepdims=True).astype(I32)
    carry_ref[...] = carry_ref[...] + jnp.sum(oh, 1, keepdims=True)
    cnt_ref[...] = carry_ref[...]

    ints_ref[...] = jnp.where(r8 == 0, e1, jnp.where(r8 == 1, e2, jnp.where(r8 == 2, rank1,
                              jnp.where(r8 == 3, rank2, 0))))
    flts_ref[...] = jnp.where(r8 == 0, p1, jnp.where(r8 == 1, p2, 0.0))


def _router(h, w_r, b_r, tm):
    n = h.shape[0]
    return pl.pallas_call(
        functools.partial(_router_kernel, tm=tm),
        grid=(n // tm,),
        in_specs=[pl.BlockSpec((tm, D_MODEL), lambda i: (i, 0)),
                  pl.BlockSpec((D_MODEL, LANES), lambda i: (0, 0)),
                  pl.BlockSpec((LANES, 1), lambda i: (0, 0))],
        out_specs=[pl.BlockSpec((8, tm), lambda i: (0, i)),
                   pl.BlockSpec((8, tm), lambda i: (0, i)),
                   pl.BlockSpec((MOE_EXPERTS, LANES), lambda i: (0, 0))],
        out_shape=[jax.ShapeDtypeStruct((8, n), I32),
                   jax.ShapeDtypeStruct((8, n), F32),
                   jax.ShapeDtypeStruct((MOE_EXPERTS, LANES), F32)],
        scratch_shapes=[pltpu.VMEM((MOE_EXPERTS, LANES), F32)],
        compiler_params=_cparams("arbitrary"),
        name="moe_router",
    )(h, w_r, b_r)


def _row_copy(src_ref, s, dst_ref, d, sem):
    def first(r):
        return r * ROW_SPLIT if isinstance(r, int) else pl.multiple_of(r * ROW_SPLIT, ROW_SPLIT)

    return pltpu.make_async_copy(src_ref.at[pl.ds(first(s), ROW_SPLIT), :],
                                 dst_ref.at[pl.ds(first(d), ROW_SPLIT), :], sem)


def _tile_wait(src_ref, dst_ref, sem):
    pltpu.make_async_copy(src_ref.at[pl.ds(0, dst_ref.shape[0]), :], dst_ref, sem).wait()


def _experts_kernel(pos1_ref, pos2_ref, te_ref, nxt_ref, na_ref, hp_ref, zeros_ref, wg_ref, wu_ref, wd_ref,
                    y_ref, xa, xb, wg_f, wu_f, wd_f, wg_s, wu_s, wd_s, row_tok, slot_ref, sem, wsem):
    i = pl.program_id(0)
    n_tiles = pl.num_programs(0)
    tr = MOE_ROW_TILE
    active = i < na_ref[0]
    changed = jnp.logical_or(i == 0, te_ref[i] != te_ref[jnp.maximum(i - 1, 0)])

    def weight_copies(expert, slot):
        return [pltpu.make_async_copy(src.at[expert], dst.at[slot], wsem.at[slot])
                for src, dst in ((wg_ref, wg_f), (wu_ref, wu_f), (wd_ref, wd_f))]

    @pl.when(i == 0)
    def _():
        slot_ref[0] = 0
        for c in weight_copies(te_ref[0], 0):
            c.start()
        clear = pltpu.make_async_copy(zeros_ref, row_tok, sem)
        clear.start()
        clear.wait()

        def invert(t, carry):
            row_tok[pos1_ref[t]] = t
            row_tok[pos2_ref[t]] = t
            return carry

        lax.fori_loop(0, pos1_ref.shape[0], invert, 0, unroll=16)

        def pick(r, carry):
            xa[pl.ds(r, 1), :] = hp_ref[pl.ds(row_tok[r], 1), :]
            return carry

        lax.fori_loop(0, tr, pick, 0, unroll=8)

    @pl.when(jnp.logical_and(active, changed))
    def _():
        slot = slot_ref[0]
        for c in weight_copies(te_ref[i], slot):
            c.wait()
        following = nxt_ref[i]

        @pl.when(following >= 0)
        def _():
            for c in weight_copies(following, 1 - slot):
                c.start()

        wg_s[...] = wg_f[slot].astype(BF16)
        wu_s[...] = wu_f[slot].astype(BF16)
        wd_s[...] = wd_f[slot].astype(BF16)
        slot_ref[0] = 1 - slot

    def step(cur, nxt):
        base = jnp.minimum(i + 1, n_tiles - 1) * tr
        for r in range(tr):
            nxt[r:r + 1, :] = hp_ref[pl.ds(row_tok[base + r], 1), :]
        lo, hi = _unpack_bf16_pairs(cur[...])
        gate = _dot(lo, wg_s[0:HALF, :]) + _dot(hi, wg_s[HALF:, :])
        up = _dot(lo, wu_s[0:HALF, :]) + _dot(hi, wu_s[HALF:, :])
        hid = (gate * jax.nn.sigmoid(gate) * up).astype(BF16)
        _store_row_tiled(y_ref, _dot(hid, wd_s[...]))

    even = i % 2 == 0

    @pl.when(jnp.logical_and(active, even))
    def _():
        step(xa, xb)

    @pl.when(jnp.logical_and(active, jnp.logical_not(even)))
    def _():
        step(xb, xa)

    @pl.when(jnp.logical_not(active))
    def _():
        y_ref[...] = jnp.zeros_like(y_ref)


def _experts(hp, pos1, pos2, tile_expert, next_expert, n_active, wg, wu, wd):
    tr = MOE_ROW_TILE
    n_tiles = tile_expert.shape[0]
    rows = n_tiles * tr
    hbm = pl.BlockSpec(memory_space=pl.ANY)
    return pl.pallas_call(
        _experts_kernel,
        grid_spec=pltpu.PrefetchScalarGridSpec(
            num_scalar_prefetch=5,
            grid=(n_tiles,),
            in_specs=[pl.BlockSpec(memory_space=pltpu.VMEM), hbm, hbm, hbm, hbm],
            out_specs=pl.BlockSpec((tr * ROW_SPLIT, LANES), lambda i, *_: (i, 0)),
            scratch_shapes=[pltpu.VMEM((tr, HALF), U32),
                            pltpu.VMEM((tr, HALF), U32),
                            pltpu.VMEM((2, D_MODEL, MOE_D_FF), F32),
                            pltpu.VMEM((2, D_MODEL, MOE_D_FF), F32),
                            pltpu.VMEM((2, MOE_D_FF, D_MODEL), F32),
                            pltpu.VMEM((D_MODEL, MOE_D_FF), BF16),
                            pltpu.VMEM((D_MODEL, MOE_D_FF), BF16),
                            pltpu.VMEM((MOE_D_FF, D_MODEL), BF16),
                            pltpu.SMEM((rows,), I32),
                            pltpu.SMEM((1,), I32),
                            pltpu.SemaphoreType.DMA(()),
                            pltpu.SemaphoreType.DMA((2,))]),
        out_shape=jax.ShapeDtypeStruct((rows * ROW_SPLIT, LANES), F32),
        compiler_params=pltpu.CompilerParams(dimension_semantics=("arbitrary",),
                                             vmem_limit_bytes=EXPERTS_VMEM_LIMIT),
        name="moe_experts",
    )(pos1, pos2, tile_expert, next_expert, n_active, hp, jnp.zeros((rows,), I32), wg, wu, wd)


def _combine_kernel(pos1_ref, pos2_ref, y_ref, h_ref, p1_ref, p2_ref, g_ref, b_ref, *rest, tm, tiled_copy):
    out_ref = rest[0]
    a1, a2, b1, b2, sem = rest[-5:]
    i = pl.program_id(0)
    last = pl.num_programs(0) - 1

    @pl.when(i == 0)
    def _():
        def issue(t, carry):
            _row_copy(y_ref, pos1_ref[t], a1, t, sem.at[0]).start()
            _row_copy(y_ref, pos2_ref[t], a2, t, sem.at[0]).start()
            return carry

        lax.fori_loop(0, tm, issue, 0)

    def step(c1, c2, cur_sem, n1, n2, nxt_sem):
        _tile_wait(y_ref, c1, cur_sem)
        _tile_wait(y_ref, c2, cur_sem)
        base = jnp.minimum(i + 1, last) * tm
        for t in range(tm):
            _row_copy(y_ref, pos1_ref[base + t], n1, t, nxt_sem).start(priority=0)
            _row_copy(y_ref, pos2_ref[base + t], n2, t, nxt_sem).start(priority=DMA_QUEUES - 1)
        ffn = p1_ref[...] * _load_row_tiled(c1) + p2_ref[...] * _load_row_tiled(c2)
        out = _layer_norm(DN_ALPHA * h_ref[...] + ffn, g_ref[...], b_ref[...])
        out_ref[...] = out
        if tiled_copy:
            _store_row_tiled(rest[1], out)

        @pl.when(i == last)
        def _():
            _tile_wait(y_ref, n1, nxt_sem)
            _tile_wait(y_ref, n2, nxt_sem)

    @pl.when(i % 2 == 0)
    def _():
        step(a1, a2, sem.at[0], b1, b2, sem.at[1])

    @pl.when(i % 2 == 1)
    def _():
        step(b1, b2, sem.at[1], a1, a2, sem.at[0])


def _combine(y, h, pos1, pos2, p1, p2, g, b, tm, tiled_copy):
    n = h.shape[0]
    tok = lambda i, a, c: (i, 0)
    fixed = lambda i, a, c: (0, 0)
    out_specs = [pl.BlockSpec((tm, D_MODEL), tok)]
    out_shape = [jax.ShapeDtypeStruct((n, D_MODEL), F32)]
    if tiled_copy:
        out_specs.append(pl.BlockSpec((tm * ROW_SPLIT, LANES), tok))
        out_shape.append(jax.ShapeDtypeStruct((n * ROW_SPLIT, LANES), F32))
    return pl.pallas_call(
        functools.partial(_combine_kernel, tm=tm, tiled_copy=tiled_copy),
        grid_spec=pltpu.PrefetchScalarGridSpec(
            num_scalar_prefetch=2,
            grid=(n // tm,),
            in_specs=[pl.BlockSpec(memory_space=pl.ANY),
                      pl.BlockSpec((tm, D_MODEL), tok),
                      pl.BlockSpec((tm, 1), tok), pl.BlockSpec((tm, 1), tok),
                      pl.BlockSpec((1, D_MODEL), fixed), pl.BlockSpec((1, D_MODEL), fixed)],
            out_specs=out_specs,
            scratch_shapes=[pltpu.VMEM((tm * ROW_SPLIT, LANES), F32)] * 4 + [pltpu.SemaphoreType.DMA((2,))]),
        out_shape=out_shape,
        compiler_params=_cparams("arbitrary"),
        name="moe_combine_ln",
    )(pos1, pos2, y, h, p1, p2, g, b)


def _plan_kernel(ints_ref, cnt_ref, pos_ref, meta_ref, *, layer, chunk):
    tr = MOE_ROW_TILE
    n = ints_ref.shape[1]
    ne = MOE_EXPERTS
    tiles = (cnt_ref[...] + (tr - 1.0)) * (1.0 / tr)
    tiles = tiles.astype(I32).astype(F32)
    lower = lax.broadcasted_iota(I32, (ne, ne), 0) >= lax.broadcasted_iota(I32, (ne, ne), 1)
    ends = _dot(jnp.where(lower, 1.0, 0.0).astype(BF16), tiles.astype(BF16))
    start_col = ((ends - tiles) * tr).astype(I32)[:, 0:1]
    r8 = lax.broadcasted_iota(I32, (8, chunk), 0)
    re = lax.broadcasted_iota(I32, (ne, chunk), 0)
    for c in range(n // chunk):
        blk = ints_ref[:, c * chunk:(c + 1) * chunk]
        s1 = jnp.sum(jnp.where(re == blk[0:1], start_col, 0), 0, keepdims=True)
        s2 = jnp.sum(jnp.where(re == blk[1:2], start_col, 0), 0, keepdims=True)
        pos_ref[:, c * chunk:(c + 1) * chunk] = jnp.where(
            r8 == 0, s1 + blk[2:3], jnp.where(r8 == 1, s2 + blk[3:4], 0))
    width = meta_ref.shape[1]
    tile_id = lax.broadcasted_iota(I32, (ne, width), 1).astype(F32)
    te = jnp.sum(jnp.where(ends[:, 0:1] <= tile_id, 1, 0), 0, keepdims=True)
    te = jnp.minimum(te, ne - 1)
    expert = lax.broadcasted_iota(I32, (ne, width), 0)
    later = jnp.logical_and(expert > te, tiles[:, 0:1] > 0.0)
    nxt = jnp.min(jnp.where(later, expert, ne), 0, keepdims=True)
    nxt = jnp.where(nxt < ne, nxt + layer * ne, -1)
    n_used = ends[ne - 1:ne, 0:1].astype(I32)
    rm = lax.broadcasted_iota(I32, (8, width), 0)
    meta_ref[...] = jnp.where(rm == 0, te + layer * ne,
                              jnp.where(rm == 1, n_used, jnp.where(rm == 2, nxt, 0)))


def _plan(ints, cnt, layer, n_tiles):
    n = ints.shape[1]
    width = -(-n_tiles // LANES) * LANES
    return pl.pallas_call(
        functools.partial(_plan_kernel, layer=layer, chunk=2048),
        out_shape=[jax.ShapeDtypeStruct((8, n), I32), jax.ShapeDtypeStruct((8, width), I32)],
        compiler_params=pltpu.CompilerParams(vmem_limit_bytes=VMEM_LIMIT),
        name="moe_plan",
    )(ints, cnt)


def _moe_layer(h, hp, layer, group_w, group_b, expert_w, expert_b, gate_w, up_w, down_w, ln_g, ln_b,
               tiled_copy):
    n = h.shape[0]
    ew = jnp.transpose(expert_w, (1, 0, 2)).reshape(D_MODEL, MOE_EXPERTS)
    w_r = jnp.zeros((D_MODEL, LANES), F32).at[:, 0:MOE_GROUPS].set(group_w).at[:, 8:8 + MOE_EXPERTS].set(ew)
    b_r = jnp.zeros((LANES,), F32).at[0:MOE_GROUPS].set(group_b).at[8:8 + MOE_EXPERTS].set(expert_b.reshape(-1))
    ints, flts, cnt = _router(h, w_r, b_r.reshape(LANES, 1), 512)
    tr = MOE_ROW_TILE
    n_tiles = (2 * n) // tr + MOE_EXPERTS
    pos, meta = _plan(ints, cnt, layer, n_tiles)
    pos1, pos2 = pos[0], pos[1]

    y = _experts(hp, pos1, pos2, meta[0, :n_tiles], meta[2, :n_tiles], meta[1, :1],
                 gate_w.reshape(-1, D_MODEL, MOE_D_FF),
                 up_w.reshape(-1, D_MODEL, MOE_D_FF),
                 down_w.reshape(-1, MOE_D_FF, D_MODEL))
    return _combine(y, h, pos1, pos2, flts[0].reshape(n, 1), flts[1].reshape(n, 1),
                    ln_g.reshape(1, -1), ln_b.reshape(1, -1), 512, tiled_copy)


def _pad_heads(w, axis):
    parts = []
    h0 = 0
    for _, _, nh in ATTN_GROUPS:
        sl = [slice(None)] * w.ndim
        sl[axis] = slice(h0 * ATTN_HEAD_DIM, (h0 + nh) * ATTN_HEAD_DIM)
        part = w[tuple(sl)]
        pad = [(0, 0)] * w.ndim
        pad[axis] = (0, ATTN_SLAB - nh * ATTN_HEAD_DIM)
        parts.append(jnp.pad(part, pad))
        h0 += nh
    return parts


def _in_proj_kernel(x_ref, w_ref, wdt_ref, z_ref, xbc_ref, dt_ref, *, z_tiles):
    j = pl.program_id(1)
    acc = _dot(x_ref[...].astype(BF16), w_ref[...])

    @pl.when(j < z_tiles)
    def _():
        z_ref[...] = acc

    @pl.when(j >= z_tiles)
    def _():
        xbc_ref[...] = acc

    @pl.when(j == 0)
    def _():
        dt_ref[...] = _dot_f32ish(x_ref[...], wdt_ref[...])


def _in_proj(x, w_zx, w_dt, tm, tn):
    m, k = x.shape
    z_tiles = SSM_D_INNER // tn
    n_tiles = w_zx.shape[1] // tn
    return pl.pallas_call(
        functools.partial(_in_proj_kernel, z_tiles=z_tiles),
        grid=(m // tm, n_tiles),
        in_specs=[pl.BlockSpec((tm, k), lambda i, j: (i, 0)),
                  pl.BlockSpec((k, tn), lambda i, j: (0, j)),
                  pl.BlockSpec((k, LANES), lambda i, j: (0, 0))],
        out_specs=[pl.BlockSpec((tm, tn), lambda i, j: (i, jnp.minimum(j, z_tiles - 1))),
                   pl.BlockSpec((tm, tn), lambda i, j: (i, jnp.maximum(j - z_tiles, 0))),
                   pl.BlockSpec((tm, LANES), lambda i, j: (i, 0))],
        out_shape=[jax.ShapeDtypeStruct((m, SSM_D_INNER), F32),
                   jax.ShapeDtypeStruct((m, SSM_CONV_DIM), F32),
                   jax.ShapeDtypeStruct((m, LANES), F32)],
        compiler_params=_cparams("parallel", "arbitrary"),
        name="ssm_in_proj",
    )(x, w_zx, w_dt)


def _ssd_layer(h, in_w, conv_w, conv_b, dt_bias, a_log, d_skip, norm_w, out_w, ln_g, ln_b, bsz, seq):
    split = SSM_D_INNER + SSM_CONV_DIM
    dt_w = jnp.pad(in_w[:, split:], ((0, 0), (0, LANES - SSM_HEADS)))
    z, xbc, dt_raw = _in_proj(h, in_w[:, :split].astype(BF16), dt_w, 1024, 1024)
    pad32 = lambda v: jnp.pad(v, (0, LANES - SSM_HEADS)).reshape(1, LANES)
    y = _ssd(z, xbc, dt_raw, conv_w, conv_b.reshape(1, -1), pad32(dt_bias), pad32(a_log),
             jnp.repeat(d_skip, SSM_HEAD_DIM).reshape(1, -1), norm_w.reshape(1, -1), bsz, seq)
    return _matmul_res_ln(y, out_w.astype(BF16), h, ln_g.reshape(1, -1), ln_b.reshape(1, -1), 512)


def _attn_layer(h, h8, kv_w, q_w, o_w, rel_bias, ln_g, ln_b, bsz, seq):
    width = ATTN_HEADS * ATTN_HEAD_DIM
    w_q = _pad_heads(q_w * (ATTN_HEAD_DIM ** -0.5), 1)
    w_k = _pad_heads(kv_w[:, :width], 1)
    w_v = _pad_heads(kv_w[:, width:], 1)
    outs, lses = [], []
    h0 = 0
    for gi, (_, dil, nh) in enumerate(ATTN_GROUPS):
        w_g = jnp.concatenate([w_q[gi], w_k[gi], w_v[gi]], axis=1).astype(BF16)
        if dil == 1:
            qkv = _matmul(h, w_g, BF16, 1024, 3 * ATTN_SLAB)
        else:
            qkv = _qkv_dilated(h8, w_g, dil, max(1024, ATTN_BLOCK * dil))
        bias_p, bias_c = _group_bias(rel_bias, h0, nh, dil)
        o, lse = _attention_group(qkv, bias_p, bias_c, gi, dil, nh, bsz, seq)
        outs.extend(o)
        lses.append(lse)
        h0 += nh
    w_o = jnp.concatenate(_pad_heads(o_w, 0), axis=0).astype(BF16)
    return _attn_out(outs, lses, w_o, h, ln_g.reshape(1, -1), ln_b.reshape(1, -1), 512)


def kernel(x, ssm_in_w, ssm_conv_w, ssm_conv_b, ssm_dt_bias, ssm_a_log, ssm_d, ssm_norm_w, ssm_out_w,
           kv_w, attn_q_w, attn_o_w, rel_bias, moe_group_w, moe_group_b, moe_expert_w, moe_expert_b,
           moe_gate_w, moe_up_w, moe_down_w, ln_g, ln_b):
    bsz, seq, d = x.shape
    h = x.reshape(bsz * seq, d)
    n_ssd = DEPTH // 2
    h8 = None
    for i in range(DEPTH):
        if i < n_ssd:
            h, hp = _ssd_layer(h, ssm_in_w[i], ssm_conv_w[i], ssm_conv_b[i], ssm_dt_bias[i], ssm_a_log[i],
                               ssm_d[i], ssm_norm_w[i], ssm_out_w[i], ln_g[i, 0], ln_b[i, 0], bsz, seq)
        else:
            j = i - n_ssd
            h, hp = _attn_layer(h, h8, kv_w, attn_q_w[j], attn_o_w[j], rel_bias, ln_g[i, 0], ln_b[i, 0],
                                bsz, seq)
        feeds_attention = n_ssd <= i + 1 < DEPTH
        res = _moe_layer(h, hp, i, moe_group_w[i], moe_group_b[i], moe_expert_w[i], moe_expert_b[i],
                         moe_gate_w, moe_up_w, moe_down_w, ln_g[i, 1], ln_b[i, 1], feeds_attention)
        h = res[0]
        h8 = res[1] if feeds_attention else None
    return h.reshape(bsz, seq, d)
```

```python
import functools
import math

import numpy as np
import jax
import jax.numpy as jnp
from jax import lax
from jax.experimental import pallas as pl
from jax.experimental.pallas import tpu as pltpu

F32 = jnp.float32
BF16 = jnp.bfloat16
I32 = jnp.int32

D_MODEL = 1024
DEPTH = 2
DN_ALPHA = (2 * DEPTH) ** 0.25
LN_EPS = 1e-5
LOG2_E = math.log2(math.e)

SSM_D_INNER = 2048
SSM_HEAD_DIM = 64
SSM_HEADS = 32
SSM_GROUPS = 4
SSM_STATE = 128
SSM_CONV = 4
SSM_CHUNK = 128
SSM_CONV_DIM = SSM_D_INNER + 2 * SSM_GROUPS * SSM_STATE

ATTN_HEAD_DIM = 64
ATTN_GROUPS = ((128, 1, 6), (512, 4, 5), (2048, 16, 5))
ATTN_HEADS = 16
ATTN_BLOCK = 128
N_BUCKETS = 32
MAX_DISTANCE = 2048
ATTN_SLAB = 384
NEG_BIG = -1e30

MOE_GROUPS = 4
MOE_EPG = 8
MOE_EXPERTS = MOE_GROUPS * MOE_EPG
MOE_D_FF = 512
MOE_ROW_TILE = 256
COMBINE_TILE = 512

LANES = 128
DMA_QUEUES = 2
VMEM_LIMIT = 48 * 1024 * 1024
EXPERTS_VMEM_LIMIT = 56 * 1024 * 1024


def _cparams(*sem):
    return pltpu.CompilerParams(dimension_semantics=sem, vmem_limit_bytes=VMEM_LIMIT)


def _layer_norm(x, g, b):
    mu = jnp.mean(x, -1, keepdims=True)
    xc = x - mu
    var = jnp.mean(xc * xc, -1, keepdims=True)
    return xc * lax.rsqrt(var + LN_EPS) * g + b


def _split2(x):
    hi = x.astype(BF16)
    lo = (x - hi.astype(F32)).astype(BF16)
    return hi, lo


def _dot(a, b):
    return jnp.dot(a, b, preferred_element_type=F32)


def _dot_f32ish(a, b):
    ah, al = _split2(a)
    bh, bl = _split2(b)
    return _dot(ah, bh) + _dot(al, bh) + _dot(ah, bl)


def _mm_kernel(a_ref, b_ref, o_ref):
    o_ref[...] = _dot(a_ref[...].astype(BF16), b_ref[...]).astype(o_ref.dtype)


def _matmul(a, b, out_dtype, tm, tn):
    m, k = a.shape
    nc = b.shape[1]
    return pl.pallas_call(
        _mm_kernel,
        grid=(m // tm, nc // tn),
        in_specs=[pl.BlockSpec((tm, k), lambda i, j: (i, 0)),
                  pl.BlockSpec((k, tn), lambda i, j: (0, j))],
        out_specs=pl.BlockSpec((tm, tn), lambda i, j: (i, j)),
        out_shape=jax.ShapeDtypeStruct((m, nc), out_dtype),
        compiler_params=_cparams("parallel", "parallel"),
        name="matmul",
    )(a, b)


def _mm3_kernel(a_ref, b_ref, o_ref):
    o_ref[...] = _dot_f32ish(a_ref[...], b_ref[...])


def _matmul_f32ish(a, b, tm):
    m, k = a.shape
    nc = b.shape[1]
    return pl.pallas_call(
        _mm3_kernel,
        grid=(m // tm,),
        in_specs=[pl.BlockSpec((tm, k), lambda i: (i, 0)),
                  pl.BlockSpec((k, nc), lambda i: (0, 0))],
        out_specs=pl.BlockSpec((tm, nc), lambda i: (i, 0)),
        out_shape=jax.ShapeDtypeStruct((m, nc), F32),
        compiler_params=_cparams("parallel"),
        name="matmul_f32ish",
    )(a, b)


ROW_SPLIT = D_MODEL // LANES


def _store_row_tiled(ref, val):
    rows = val.shape[0]
    for c in range(ROW_SPLIT):
        ref[pl.ds(c, rows, stride=ROW_SPLIT), :] = val[:, c * LANES:(c + 1) * LANES]


def _load_row_tiled(ref):
    rows = ref.shape[0] // ROW_SPLIT
    return jnp.concatenate([ref[pl.ds(c, rows, stride=ROW_SPLIT), :] for c in range(ROW_SPLIT)], axis=1)


HALF = D_MODEL // 2
U32 = jnp.uint32
HI16 = 0xFFFF0000


def _pack_bf16_pairs(x):
    bits = lax.bitcast_convert_type(x.astype(BF16).astype(F32), U32)
    return (bits[:, :HALF] >> 16) | (bits[:, HALF:] & U32(HI16))


def _unpack_bf16_pairs(w):
    lo = lax.bitcast_convert_type(w << 16, F32).astype(BF16)
    hi = lax.bitcast_convert_type(w & U32(HI16), F32).astype(BF16)
    return lo, hi


def _mm_ln_kernel(a_ref, w_ref, h_ref, g_ref, b_ref, o_ref, op_ref):
    acc = _dot(a_ref[...], w_ref[...])
    out = _layer_norm(DN_ALPHA * h_ref[...] + acc, g_ref[...], b_ref[...])
    o_ref[...] = out
    op_ref[...] = _pack_bf16_pairs(out)


def _matmul_res_ln(a, w, h, g, b, tm):
    m, k = a.shape
    d = w.shape[1]
    return pl.pallas_call(
        _mm_ln_kernel,
        grid=(m // tm,),
        in_specs=[pl.BlockSpec((tm, k), lambda i: (i, 0)),
                  pl.BlockSpec((k, d), lambda i: (0, 0)),
                  pl.BlockSpec((tm, d), lambda i: (i, 0)),
                  pl.BlockSpec((1, d), lambda i: (0, 0)),
                  pl.BlockSpec((1, d), lambda i: (0, 0))],
        out_specs=[pl.BlockSpec((tm, d), lambda i: (i, 0)),
                   pl.BlockSpec((tm, HALF), lambda i: (i, 0))],
        out_shape=[jax.ShapeDtypeStruct((m, d), F32),
                   jax.ShapeDtypeStruct((m, HALF), U32)],
        compiler_params=_cparams("parallel"),
        name="matmul_res_ln",
    )(a, w, h, g, b)


SSD_CHUNKS_PER_STEP = 2


def _ssd_kernel(z_ref, xbc_ref, dt_ref, cw_ref, cb_ref, dtb_ref, alog_ref, dsk_ref, nw_ref,
                y_ref, xe_ref, st_ref):
    @pl.when(pl.program_id(1) == 0)
    def _():
        xe_ref[0:8, :] = jnp.zeros((8, SSM_CONV_DIM), F32)
        st_ref[...] = jnp.zeros_like(st_ref)

    for sub in range(SSD_CHUNKS_PER_STEP):
        _ssd_chunk(pl.ds(sub * SSM_CHUNK, SSM_CHUNK), z_ref, xbc_ref, dt_ref, cw_ref, cb_ref, dtb_ref,
                   alog_ref, dsk_ref, nw_ref, y_ref, xe_ref, st_ref)


def _ssd_chunk(rows, z_ref, xbc_ref, dt_ref, cw_ref, cb_ref, dtb_ref, alog_ref, dsk_ref, nw_ref,
               y_ref, xe_ref, st_ref):
    q = SSM_CHUNK
    u = xbc_ref[rows, :]
    xe_ref[8:8 + q, :] = u
    w = cw_ref[...]
    conv = (cb_ref[...] + w[3:4] * u + w[2:3] * xe_ref[7:7 + q, :]
            + w[1:2] * xe_ref[6:6 + q, :] + w[0:1] * xe_ref[5:5 + q, :])
    xe_ref[0:8, :] = xe_ref[q:q + 8, :]
    act = conv * jax.nn.sigmoid(conv)

    pre = dt_ref[rows, :] + dtb_ref[...]
    dt = jnp.maximum(pre, 0.0) + jnp.log(1.0 + jnp.exp(-jnp.abs(pre)))
    adt = dt * (-jnp.exp(alog_ref[...]) * LOG2_E)

    row = lax.broadcasted_iota(I32, (q, q), 0)
    col = lax.broadcasted_iota(I32, (q, q), 1)
    tril = row >= col
    tri_b = jnp.where(tril, 1.0, 0.0).astype(BF16)
    a_hi = adt.astype(BF16)
    r1 = adt - a_hi.astype(F32)
    a_mid = r1.astype(BF16)
    a_lo = (r1 - a_mid.astype(F32)).astype(BF16)
    acs = _dot(tri_b, a_hi) + _dot(tri_b, a_mid) + _dot(tri_b, a_lo)
    acs_t = acs.T
    acs_dt_t = acs_t - jnp.log(dt.T) * LOG2_E
    eacs = jnp.exp2(acs)
    left = col < SSM_HEAD_DIM

    for g in range(SSM_GROUPS):
        b0 = SSM_D_INNER + g * SSM_STATE
        c0 = SSM_D_INNER + SSM_GROUPS * SSM_STATE + g * SSM_STATE
        bm = act[:, b0:b0 + SSM_STATE]
        cm = act[:, c0:c0 + SSM_STATE].astype(BF16)
        cb = lax.dot_general(cm, bm.astype(BF16), (((1,), (1,)), ((), ())),
                             preferred_element_type=F32)
        bm_t = bm.T
        gs = g * 512
        y_off = _dot(cm, st_ref[:, gs:gs + 512].astype(BF16))
        slabs = []
        for pr in range(4):
            ha = g * 8 + pr * 2
            hb = ha + 1
            cs = gs + pr * LANES
            x2 = act[:, cs:cs + LANES]
            x2b = x2.astype(BF16)
            ys, ups = [], []
            for h in (ha, hb):
                a_col = acs[:, h:h + 1]
                a_src = acs_dt_t[h:h + 1, :]
                decay = jnp.where(tril, jnp.exp2(a_col - a_src), 0.0)
                ys.append(_dot((cb * decay).astype(BF16), x2b))
                to_end = jnp.exp2(acs_t[h:h + 1, q - 1:q] - a_src)
                ups.append(_dot((bm_t * to_end).astype(BF16), x2b))
            y_diag = jnp.where(left, ys[0], ys[1])
            upd = jnp.where(left, ups[0], ups[1])
            e2 = jnp.where(left, eacs[:, ha:ha + 1], eacs[:, hb:hb + 1])
            cd = jnp.where(left[0:1, :], eacs[q - 1:q, ha:ha + 1], eacs[q - 1:q, hb:hb + 1])
            y2 = y_diag + y_off[:, pr * LANES:(pr + 1) * LANES] * e2 + dsk_ref[:, cs:cs + LANES] * x2
            st_ref[:, cs:cs + LANES] = st_ref[:, cs:cs + LANES] * cd + upd
            slabs.append(y2)
        yg = jnp.concatenate(slabs, axis=1)
        zg = z_ref[rows, gs:gs + 512]
        yg = yg * (zg * jax.nn.sigmoid(zg))
        ms = jnp.mean(yg * yg, -1, keepdims=True)
        y_ref[rows, gs:gs + 512] = (yg * lax.rsqrt(ms + LN_EPS) * nw_ref[:, gs:gs + 512]).astype(y_ref.dtype)


def _ssd(z, xbc, dt_raw, conv_w, conv_b, dt_bias, a_log, d_rep, norm_w, bsz, seq):
    n = z.shape[0]
    q = SSM_CHUNK * SSD_CHUNKS_PER_STEP
    nchunk = seq // q
    tok = lambda b, c: (b * nchunk + c, 0)
    fixed = lambda b, c: (0, 0)
    return pl.pallas_call(
        _ssd_kernel,
        grid=(bsz, nchunk),
        in_specs=[pl.BlockSpec((q, SSM_D_INNER), tok),
                  pl.BlockSpec((q, SSM_CONV_DIM), tok),
                  pl.BlockSpec((q, LANES), tok),
                  pl.BlockSpec((SSM_CONV, SSM_CONV_DIM), fixed),
                  pl.BlockSpec((1, SSM_CONV_DIM), fixed),
                  pl.BlockSpec((1, LANES), fixed),
                  pl.BlockSpec((1, LANES), fixed),
                  pl.BlockSpec((1, SSM_D_INNER), fixed),
                  pl.BlockSpec((1, SSM_D_INNER), fixed)],
        out_specs=pl.BlockSpec((q, SSM_D_INNER), tok),
        out_shape=jax.ShapeDtypeStruct((n, SSM_D_INNER), BF16),
        scratch_shapes=[pltpu.VMEM((SSM_CHUNK + 8, SSM_CONV_DIM), F32),
                        pltpu.VMEM((SSM_STATE, SSM_D_INNER), F32)],
        compiler_params=_cparams("parallel", "arbitrary"),
        name="ssd_chunk",
    )(z, xbc, dt_raw, conv_w, conv_b, dt_bias, a_log, d_rep, norm_w)


def _residue_major_pieces(h8_ref, n_tokens, dil):
    span = ATTN_BLOCK * dil
    pieces = []
    for blk in range(n_tokens // span):
        for r in range(dil):
            first = (blk * span + r) * ROW_SPLIT
            pieces.append(jnp.concatenate(
                [h8_ref[pl.ds(first + c, ATTN_BLOCK, stride=ROW_SPLIT * dil), :] for c in range(ROW_SPLIT)],
                axis=1).astype(BF16))
    return pieces


def _qkv_dilated_kernel(h8_ref, w_ref, o_ref, *, dil):
    per_dot = 4
    rows = _residue_major_pieces(h8_ref, o_ref.shape[0], dil)
    for k in range(0, len(rows), per_dot):
        x = jnp.concatenate(rows[k:k + per_dot], axis=0)
        o_ref[k * ATTN_BLOCK:(k + per_dot) * ATTN_BLOCK, :] = _dot(x, w_ref[...]).astype(o_ref.dtype)


def _qkv_dilated(h8, w, dil, tm):
    n = h8.shape[0] // ROW_SPLIT
    nc = w.shape[1]
    return pl.pallas_call(
        functools.partial(_qkv_dilated_kernel, dil=dil),
        grid=(n // tm,),
        in_specs=[pl.BlockSpec((tm * ROW_SPLIT, LANES), lambda i: (i, 0)),
                  pl.BlockSpec((D_MODEL, nc), lambda i: (0, 0))],
        out_specs=pl.BlockSpec((tm, nc), lambda i: (i, 0)),
        out_shape=jax.ShapeDtypeStruct((n, nc), BF16),
        compiler_params=_cparams("parallel"),
        name=f"qkv_dil{dil}",
    )(h8, w)


def _attn_kernel(*refs, nh, dil, has_prev):
    if has_prev:
        q_ref, kp_ref, kc_ref, vp_ref, vc_ref, bp_ref, bc_ref = refs[:7]
        out_refs = refs[7:]
        first_pen = jnp.where(pl.program_id(1) > 0, 0.0, NEG_BIG)
    else:
        q_ref, kc_ref, vc_ref, bc_ref = refs[:4]
        out_refs = refs[4:]
    s_scr, p_scr, max_scr, den_scr = out_refs[-4:]
    o_refs, lse_ref = out_refs[:-5], out_refs[-5]
    ones = jnp.ones((s_scr.shape[2], LANES), BF16)
    lane = lax.broadcasted_iota(I32, (ATTN_BLOCK, LANES), 1)
    left = lane < ATTN_HEAD_DIM
    zero = jnp.zeros((), BF16)
    nt = (((1,), (1,)), ((), ()))

    def residue(r, carry):
        rows = pl.ds(pl.multiple_of(r * ATTN_BLOCK, ATTN_BLOCK), ATTN_BLOCK)
        dst = pl.ds(r, ATTN_BLOCK, stride=dil)
        for hh in range(nh):
            cols = pl.ds(hh // 2 * LANES, LANES)
            qm = jnp.where(left if hh % 2 == 0 else ~left, q_ref[rows, cols], zero)
            s_c = lax.dot_general(qm, kc_ref[rows, cols], nt, preferred_element_type=F32) + bc_ref[hh]
            if has_prev:
                s_p = lax.dot_general(qm, kp_ref[rows, cols], nt, preferred_element_type=F32)
                s_scr[hh, :, 0:ATTN_BLOCK] = s_p + (bp_ref[hh] + first_pen)
                s_scr[hh, :, ATTN_BLOCK:] = s_c
            else:
                s_scr[hh] = s_c
        for hh in range(nh):
            s = s_scr[hh]
            m = jnp.max(s, -1, keepdims=True)
            p = jnp.exp(s - m)
            p_scr[hh] = p.astype(BF16)
            max_scr[hh] = m
            if not has_prev:
                den_scr[hh] = jnp.sum(p, -1, keepdims=True)
        lse_sum = jnp.zeros((ATTN_BLOCK, 1), F32)
        for pr in range(ATTN_SLAB // LANES):
            cols = pl.ds(pr * LANES, LANES)
            halves = []
            for hh in (pr * 2, pr * 2 + 1):
                if hh >= nh:
                    halves.append(jnp.zeros((ATTN_BLOCK, LANES), F32))
                    continue
                if has_prev:
                    den = _dot(p_scr[hh], ones)
                    den1 = den[:, 0:1]
                    o = (_dot(p_scr[hh, :, 0:ATTN_BLOCK], vp_ref[rows, cols])
                         + _dot(p_scr[hh, :, ATTN_BLOCK:], vc_ref[rows, cols]))
                else:
                    den = den1 = den_scr[hh]
                    o = _dot(p_scr[hh], vc_ref[rows, cols])
                halves.append(o * (1.0 / den))
                lse_sum = lse_sum + (max_scr[hh] + jnp.log(den1))
            o_refs[pr][dst, :] = jnp.where(left, halves[0], halves[1])
        lse_ref[dst, :] = jnp.broadcast_to(lse_sum * (1.0 / nh), (ATTN_BLOCK, LANES))
        return carry

    lax.fori_loop(0, dil, residue, 0)


def _attention_group(qkv, bias_p, bias_c, gi, dil, nh, bsz, seq):
    span = ATTN_BLOCK * dil
    nb = seq // span
    has_prev = nb > 1
    blk = (span, ATTN_SLAB)
    cur = lambda which: (lambda b, n: (b * nb + n, which))
    prev = lambda which: (lambda b, n: (b * nb + jnp.maximum(n - 1, 0), which))
    fixed = lambda b, n: (0, 0, 0)
    tab = pl.BlockSpec((nh, ATTN_BLOCK, ATTN_BLOCK), fixed)
    if has_prev:
        in_specs = [pl.BlockSpec(blk, cur(0)), pl.BlockSpec(blk, prev(1)), pl.BlockSpec(blk, cur(1)),
                    pl.BlockSpec(blk, prev(2)), pl.BlockSpec(blk, cur(2)), tab, tab]
        args = (qkv, qkv, qkv, qkv, qkv, bias_p, bias_c)
    else:
        in_specs = [pl.BlockSpec(blk, cur(0)), pl.BlockSpec(blk, cur(1)), pl.BlockSpec(blk, cur(2)), tab]
        args = (qkv, qkv, qkv, bias_c)
    n_out = ATTN_SLAB // LANES + 1
    keys = 2 * ATTN_BLOCK if has_prev else ATTN_BLOCK
    outs = pl.pallas_call(
        functools.partial(_attn_kernel, nh=nh, dil=dil, has_prev=has_prev),
        grid=(bsz, nb),
        in_specs=in_specs,
        out_specs=[pl.BlockSpec((span, LANES), lambda b, n: (b * nb + n, 0))] * n_out,
        out_shape=[jax.ShapeDtypeStruct((bsz * seq, LANES), F32)] * n_out,
        scratch_shapes=[pltpu.VMEM((nh, ATTN_BLOCK, keys), F32),
                        pltpu.VMEM((nh, ATTN_BLOCK, keys), BF16),
                        pltpu.VMEM((nh, ATTN_BLOCK, 1), F32),
                        pltpu.VMEM((nh, ATTN_BLOCK, 1), F32)],
        compiler_params=_cparams("parallel", "parallel"),
        name=f"dilated_attn_g{gi}",
    )(*args)
    return outs[:-1], outs[-1]


def _attn_out_kernel(*refs):
    n_pairs = ATTN_SLAB // LANES
    n_groups = len(ATTN_GROUPS)
    o_refs = refs[:n_groups * n_pairs]
    l_refs = refs[n_groups * n_pairs:n_groups * (n_pairs + 1)]
    w_ref, h_ref, g_ref, b_ref, out_ref, outp_ref = refs[n_groups * (n_pairs + 1):]
    ls = [r[:, 0:1] for r in l_refs]
    m = jnp.maximum(jnp.maximum(ls[0], ls[1]), ls[2])
    es = [jnp.exp(v - m) for v in ls]
    scale = n_groups / (es[0] + es[1] + es[2])
    slabs = []
    for gi in range(n_groups):
        wt = es[gi] * scale
        slabs.extend((r[...] * wt).astype(BF16) for r in o_refs[gi * n_pairs:(gi + 1) * n_pairs])
    acc = _dot(jnp.concatenate(slabs, axis=1), w_ref[...])
    out = _layer_norm(DN_ALPHA * h_ref[...] + acc, g_ref[...], b_ref[...])
    out_ref[...] = out
    outp_ref[...] = _pack_bf16_pairs(out)


def _attn_out(outs, lses, w, h, g, b, tm):
    n = h.shape[0]
    tok = lambda i: (i, 0)
    fixed2 = lambda i: (0, 0)
    return pl.pallas_call(
        _attn_out_kernel,
        grid=(n // tm,),
        in_specs=[pl.BlockSpec((tm, LANES), tok)] * (len(outs) + len(lses))
                 + [pl.BlockSpec((len(ATTN_GROUPS) * ATTN_SLAB, D_MODEL), lambda i: (0, 0)),
                    pl.BlockSpec((tm, D_MODEL), tok),
                    pl.BlockSpec((1, D_MODEL), fixed2), pl.BlockSpec((1, D_MODEL), fixed2)],
        out_specs=[pl.BlockSpec((tm, D_MODEL), tok), pl.BlockSpec((tm, HALF), tok)],
        out_shape=[jax.ShapeDtypeStruct((n, D_MODEL), F32),
                   jax.ShapeDtypeStruct((n, HALF), U32)],
        compiler_params=_cparams("parallel"),
        name="attn_out_ln",
    )(*outs, *lses, w, h, g, b)


def _t5_bucket(dist):
    max_exact = N_BUCKETS // 2
    n = np.maximum(dist, 1).astype(np.float64)
    large = max_exact + (np.log(n / max_exact) / np.log(MAX_DISTANCE / max_exact)
                         * (N_BUCKETS - max_exact)).astype(np.int32)
    large = np.minimum(large, N_BUCKETS - 1)
    return np.where(dist < max_exact, dist, large).astype(np.int32)


def _group_bias(rel_bias, h0, nh, dil):
    qi = np.arange(ATTN_BLOCK)[:, None]
    ki = np.arange(ATTN_BLOCK)[None, :]
    tabs = []
    for delta, band in ((qi + ATTN_BLOCK - ki, ki >= qi), (qi - ki, ki <= qi)):
        bucket = _t5_bucket(np.clip(delta, 0, None) * dil)
        onehot = (bucket[..., None] == np.arange(N_BUCKETS)).astype(np.float32)
        t = jnp.einsum("qkb,bh->hqk", onehot, rel_bias[:, h0:h0 + nh], precision=lax.Precision.HIGHEST)
        tabs.append(jnp.where(band[None], t, NEG_BIG).astype(F32))
    return tabs


def _router_kernel(h_ref, w_ref, b_ref, ints_ref, flts_ref, cnt_ref, carry_ref, *, tm):
    @pl.when(pl.program_id(0) == 0)
    def _():
        carry_ref[...] = jnp.zeros_like(carry_ref)

    lt = _dot_f32ish(h_ref[...], w_ref[...]).T + b_ref[...]
    gl = lt[0:MOE_GROUPS]
    r4 = lax.broadcasted_iota(I32, (MOE_GROUPS, tm), 0)
    gmax = jnp.max(gl, 0, keepdims=True)
    gidx = jnp.min(jnp.where(gl == gmax, r4, MOE_GROUPS), 0, keepdims=True)
    gval = 1.0 / jnp.sum(jnp.exp(gl - gmax), 0, keepdims=True)

    esel = jnp.zeros((MOE_EPG, tm), F32)
    for g in range(MOE_GROUPS):
        esel = jnp.where(gidx == g, lt[8 + g * MOE_EPG:8 + (g + 1) * MOE_EPG], esel)
    r8 = lax.broadcasted_iota(I32, (MOE_EPG, tm), 0)
    v1 = jnp.max(esel, 0, keepdims=True)
    i1 = jnp.min(jnp.where(esel == v1, r8, MOE_EPG), 0, keepdims=True)
    rest = jnp.where(r8 == i1, -jnp.inf, esel)
    v2 = jnp.max(rest, 0, keepdims=True)
    i2 = jnp.min(jnp.where(rest == v2, r8, MOE_EPG), 0, keepdims=True)
    t = jnp.exp(v2 - v1)
    p1 = gval / (1.0 + t)
    p2 = p1 * t
    e1 = gidx * MOE_EPG + i1
    e2 = gidx * MOE_EPG + i2

    r32 = lax.broadcasted_iota(I32, (MOE_EXPERTS, tm), 0)
    oh1 = r32 == e1
    oh2 = r32 == e2
    oh = jnp.where(oh1 | oh2, 1.0, 0.0)
    tr = lax.broadcasted_iota(I32, (tm, tm), 0)
    tc = lax.broadcasted_iota(I32, (tm, tm), 1)
    before = jnp.where(tr < tc, 1.0, 0.0).astype(BF16)
    base = _dot(oh.astype(BF16), before) + carry_ref[:, 0:1]
    rank1 = jnp.sum(jnp.where(oh1, base, 0.0), 0, keepdims=True).astype(I32)
    rank2 = jnp.sum(jnp.where(oh2, base, 0.0), 0, keepdims=True).astype(I32)
    carry_ref[...] = carry_ref[...] + jnp.sum(oh, 1, keepdims=True)
    cnt_ref[...] = carry_ref[...]

    ints_ref[...] = jnp.where(r8 == 0, e1, jnp.where(r8 == 1, e2, jnp.where(r8 == 2, rank1,
                              jnp.where(r8 == 3, rank2, 0))))
    flts_ref[...] = jnp.where(r8 == 0, p1, jnp.where(r8 == 1, p2, 0.0))


def _router(h, w_r, b_r, tm):
    n = h.shape[0]
    return pl.pallas_call(
        functools.partial(_router_kernel, tm=tm),
        grid=(n // tm,),
        in_specs=[pl.BlockSpec((tm, D_MODEL), lambda i: (i, 0)),
                  pl.BlockSpec((D_MODEL, LANES), lambda i: (0, 0)),
                  pl.BlockSpec((LANES, 1), lambda i: (0, 0))],
        out_specs=[pl.BlockSpec((8, tm), lambda i: (0, i)),
                   pl.BlockSpec((8, tm), lambda i: (0, i)),
                   pl.BlockSpec((MOE_EXPERTS, LANES), lambda i: (0, 0))],
        out_shape=[jax.ShapeDtypeStruct((8, n), I32),
                   jax.ShapeDtypeStruct((8, n), F32),
                   jax.ShapeDtypeStruct((MOE_EXPERTS, LANES), F32)],
        scratch_shapes=[pltpu.VMEM((MOE_EXPERTS, LANES), F32)],
        compiler_params=_cparams("arbitrary"),
        name="moe_router",
    )(h, w_r, b_r)


def _row_copy(src_ref, s, dst_ref, d, sem):
    def first(r):
        return r * ROW_SPLIT if isinstance(r, int) else pl.multiple_of(r * ROW_SPLIT, ROW_SPLIT)

    return pltpu.make_async_copy(src_ref.at[pl.ds(first(s), ROW_SPLIT), :],
                                 dst_ref.at[pl.ds(first(d), ROW_SPLIT), :], sem)


def _tile_wait(src_ref, dst_ref, sem):
    pltpu.make_async_copy(src_ref.at[pl.ds(0, dst_ref.shape[0]), :], dst_ref, sem).wait()


def _experts_kernel(pos1_ref, pos2_ref, te_ref, nxt_ref, na_ref, hp_ref, zeros_ref, wg_ref, wu_ref, wd_ref,
                    y_ref, xa, xb, wg_f, wu_f, wd_f, wg_s, wu_s, wd_s, row_tok, slot_ref, sem, wsem):
    i = pl.program_id(0)
    n_tiles = pl.num_programs(0)
    tr = MOE_ROW_TILE
    active = i < na_ref[0]
    changed = jnp.logical_or(i == 0, te_ref[i] != te_ref[jnp.maximum(i - 1, 0)])

    def weight_copies(expert, slot):
        return [pltpu.make_async_copy(src.at[expert], dst.at[slot], wsem.at[slot])
                for src, dst in ((wg_ref, wg_f), (wu_ref, wu_f), (wd_ref, wd_f))]

    @pl.when(i == 0)
    def _():
        slot_ref[0] = 0
        for c in weight_copies(te_ref[0], 0):
            c.start()
        clear = pltpu.make_async_copy(zeros_ref, row_tok, sem)
        clear.start()
        clear.wait()

        def invert(t, carry):
            row_tok[pos1_ref[t]] = t
            row_tok[pos2_ref[t]] = t
            return carry

        lax.fori_loop(0, pos1_ref.shape[0], invert, 0, unroll=16)

        def pick(r, carry):
            xa[pl.ds(r, 1), :] = hp_ref[pl.ds(row_tok[r], 1), :]
            return carry

        lax.fori_loop(0, tr, pick, 0, unroll=8)

    @pl.when(jnp.logical_and(active, changed))
    def _():
        slot = slot_ref[0]
        for c in weight_copies(te_ref[i], slot):
            c.wait()
        following = nxt_ref[i]

        @pl.when(following >= 0)
        def _():
            for c in weight_copies(following, 1 - slot):
                c.start()

        wg_s[...] = wg_f[slot].astype(BF16)
        wu_s[...] = wu_f[slot].astype(BF16)
        wd_s[...] = wd_f[slot].astype(BF16)
        slot_ref[0] = 1 - slot

    def step(cur, nxt):
        base = jnp.minimum(i + 1, n_tiles - 1) * tr
        for r in range(tr):
            nxt[r:r + 1, :] = hp_ref[pl.ds(row_tok[base + r], 1), :]
        lo, hi = _unpack_bf16_pairs(cur[...])
        gate = _dot(lo, wg_s[0:HALF, :]) + _dot(hi, wg_s[HALF:, :])
        up = _dot(lo, wu_s[0:HALF, :]) + _dot(hi, wu_s[HALF:, :])
        hid = (gate * jax.nn.sigmoid(gate) * up).astype(BF16)
        _store_row_tiled(y_ref, _dot(hid, wd_s[...]))

    even = i % 2 == 0

    @pl.when(jnp.logical_and(active, even))
    def _():
        step(xa, xb)

    @pl.when(jnp.logical_and(active, jnp.logical_not(even)))
    def _():
        step(xb, xa)

    @pl.when(jnp.logical_not(active))
    def _():
        y_ref[...] = jnp.zeros_like(y_ref)


def _experts(hp, pos1, pos2, tile_expert, next_expert, n_active, wg, wu, wd):
    tr = MOE_ROW_TILE
    n_tiles = tile_expert.shape[0]
    rows = n_tiles * tr
    hbm = pl.BlockSpec(memory_space=pl.ANY)
    return pl.pallas_call(
        _experts_kernel,
        grid_spec=pltpu.PrefetchScalarGridSpec(
            num_scalar_prefetch=5,
            grid=(n_tiles,),
            in_specs=[pl.BlockSpec(memory_space=pltpu.VMEM), hbm, hbm, hbm, hbm],
            out_specs=pl.BlockSpec((tr * ROW_SPLIT, LANES), lambda i, *_: (i, 0)),
            scratch_shapes=[pltpu.VMEM((tr, HALF), U32),
                            pltpu.VMEM((tr, HALF), U32),
                            pltpu.VMEM((2, D_MODEL, MOE_D_FF), F32),
                            pltpu.VMEM((2, D_MODEL, MOE_D_FF), F32),
                            pltpu.VMEM((2, MOE_D_FF, D_MODEL), F32),
                            pltpu.VMEM((D_MODEL, MOE_D_FF), BF16),
                            pltpu.VMEM((D_MODEL, MOE_D_FF), BF16),
                            pltpu.VMEM((MOE_D_FF, D_MODEL), BF16),
                            pltpu.SMEM((rows,), I32),
                            pltpu.SMEM((1,), I32),
                            pltpu.SemaphoreType.DMA(()),
                            pltpu.SemaphoreType.DMA((2,))]),
        out_shape=jax.ShapeDtypeStruct((rows * ROW_SPLIT, LANES), F32),
        compiler_params=pltpu.CompilerParams(dimension_semantics=("arbitrary",),
                                             vmem_limit_bytes=EXPERTS_VMEM_LIMIT),
        name="moe_experts",
    )(pos1, pos2, tile_expert, next_expert, n_active, hp, jnp.zeros((rows,), I32), wg, wu, wd)


def _combine_kernel(pos1_ref, pos2_ref, y_ref, h_ref, p1_ref, p2_ref, g_ref, b_ref, *rest, tm, tiled_copy,
                    qkv_dils):
    n_qkv = len(qkv_dils)
    w_refs, rest = rest[:n_qkv], rest[n_qkv:]
    out_ref = rest[0]
    qkv_refs = rest[2:2 + n_qkv]
    a1, a2, b1, b2, sem = rest[-5:]
    i = pl.program_id(0)
    last = pl.num_programs(0) - 1

    @pl.when(i == 0)
    def _():
        def issue(t, carry):
            _row_copy(y_ref, pos1_ref[t], a1, t, sem.at[0]).start()
            _row_copy(y_ref, pos2_ref[t], a2, t, sem.at[0]).start()
            return carry

        lax.fori_loop(0, tm, issue, 0)

    def step(c1, c2, cur_sem, n1, n2, nxt_sem):
        _tile_wait(y_ref, c1, cur_sem)
        _tile_wait(y_ref, c2, cur_sem)
        base = jnp.minimum(i + 1, last) * tm
        for t in range(tm):
            _row_copy(y_ref, pos1_ref[base + t], n1, t, nxt_sem).start(priority=0)
            _row_copy(y_ref, pos2_ref[base + t], n2, t, nxt_sem).start(priority=DMA_QUEUES - 1)
        ffn = p1_ref[...] * _load_row_tiled(c1) + p2_ref[...] * _load_row_tiled(c2)
        out = _layer_norm(DN_ALPHA * h_ref[...] + ffn, g_ref[...], b_ref[...])
        out_ref[...] = out
        if tiled_copy:
            _store_row_tiled(rest[1], out)
        for w_ref, q_ref, dil in zip(w_refs, qkv_refs, qkv_dils):
            if dil == 1:
                x = out.astype(BF16)
            else:
                x = jnp.concatenate(_residue_major_pieces(rest[1], tm, dil), axis=0)
            q_ref[...] = _dot(x, w_ref[...]).astype(q_ref.dtype)

        @pl.when(i == last)
        def _():
            _tile_wait(y_ref, n1, nxt_sem)
            _tile_wait(y_ref, n2, nxt_sem)

    @pl.when(i % 2 == 0)
    def _():
        step(a1, a2, sem.at[0], b1, b2, sem.at[1])

    @pl.when(i % 2 == 1)
    def _():
        step(b1, b2, sem.at[1], a1, a2, sem.at[0])


def _combine(y, h, pos1, pos2, p1, p2, g, b, tm, tiled_copy, qkv=()):
    n = h.shape[0]
    tok = lambda i, a, c: (i, 0)
    fixed = lambda i, a, c: (0, 0)
    out_specs = [pl.BlockSpec((tm, D_MODEL), tok)]
    out_shape = [jax.ShapeDtypeStruct((n, D_MODEL), F32)]
    if tiled_copy:
        out_specs.append(pl.BlockSpec((tm * ROW_SPLIT, LANES), tok))
        out_shape.append(jax.ShapeDtypeStruct((n * ROW_SPLIT, LANES), F32))
    w_specs = []
    for w, dil in qkv:
        assert tiled_copy and tm % (ATTN_BLOCK * dil) == 0
        w_specs.append(pl.BlockSpec(w.shape, fixed))
        out_specs.append(pl.BlockSpec((tm, w.shape[1]), tok))
        out_shape.append(jax.ShapeDtypeStruct((n, w.shape[1]), BF16))
    return pl.pallas_call(
        functools.partial(_combine_kernel, tm=tm, tiled_copy=tiled_copy,
                          qkv_dils=tuple(dil for _, dil in qkv)),
        grid_spec=pltpu.PrefetchScalarGridSpec(
            num_scalar_prefetch=2,
            grid=(n // tm,),
            in_specs=[pl.BlockSpec(memory_space=pl.ANY),
                      pl.BlockSpec((tm, D_MODEL), tok),
                      pl.BlockSpec((tm, 1), tok), pl.BlockSpec((tm, 1), tok),
                      pl.BlockSpec((1, D_MODEL), fixed), pl.BlockSpec((1, D_MODEL), fixed)] + w_specs,
            out_specs=out_specs,
            scratch_shapes=[pltpu.VMEM((tm * ROW_SPLIT, LANES), F32)] * 4 + [pltpu.SemaphoreType.DMA((2,))]),
        out_shape=out_shape,
        compiler_params=_cparams("arbitrary"),
        name="moe_combine_ln",
    )(pos1, pos2, y, h, p1, p2, g, b, *[w for w, _ in qkv])


def _plan_kernel(ints_ref, cnt_ref, pos_ref, meta_ref, *, layer, chunk):
    tr = MOE_ROW_TILE
    n = ints_ref.shape[1]
    ne = MOE_EXPERTS
    tiles = (cnt_ref[...] + (tr - 1.0)) * (1.0 / tr)
    tiles = tiles.astype(I32).astype(F32)
    lower = lax.broadcasted_iota(I32, (ne, ne), 0) >= lax.broadcasted_iota(I32, (ne, ne), 1)
    ends = _dot(jnp.where(lower, 1.0, 0.0).astype(BF16), tiles.astype(BF16))
    start_col = ((ends - tiles) * tr).astype(I32)[:, 0:1]
    r8 = lax.broadcasted_iota(I32, (8, chunk), 0)
    re = lax.broadcasted_iota(I32, (ne, chunk), 0)
    for c in range(n // chunk):
        blk = ints_ref[:, c * chunk:(c + 1) * chunk]
        s1 = jnp.sum(jnp.where(re == blk[0:1], start_col, 0), 0, keepdims=True)
        s2 = jnp.sum(jnp.where(re == blk[1:2], start_col, 0), 0, keepdims=True)
        pos_ref[:, c * chunk:(c + 1) * chunk] = jnp.where(
            r8 == 0, s1 + blk[2:3], jnp.where(r8 == 1, s2 + blk[3:4], 0))
    width = meta_ref.shape[1]
    tile_id = lax.broadcasted_iota(I32, (ne, width), 1).astype(F32)
    te = jnp.sum(jnp.where(ends[:, 0:1] <= tile_id, 1, 0), 0, keepdims=True)
    te = jnp.minimum(te, ne - 1)
    expert = lax.broadcasted_iota(I32, (ne, width), 0)
    later = jnp.logical_and(expert > te, tiles[:, 0:1] > 0.0)
    nxt = jnp.min(jnp.where(later, expert, ne), 0, keepdims=True)
    nxt = jnp.where(nxt < ne, nxt + layer * ne, -1)
    n_used = ends[ne - 1:ne, 0:1].astype(I32)
    rm = lax.broadcasted_iota(I32, (8, width), 0)
    meta_ref[...] = jnp.where(rm == 0, te + layer * ne,
                              jnp.where(rm == 1, n_used, jnp.where(rm == 2, nxt, 0)))


def _plan(ints, cnt, layer, n_tiles):
    n = ints.shape[1]
    width = -(-n_tiles // LANES) * LANES
    return pl.pallas_call(
        functools.partial(_plan_kernel, layer=layer, chunk=2048),
        out_shape=[jax.ShapeDtypeStruct((8, n), I32), jax.ShapeDtypeStruct((8, width), I32)],
        compiler_params=pltpu.CompilerParams(vmem_limit_bytes=VMEM_LIMIT),
        name="moe_plan",
    )(ints, cnt)


def _moe_layer(h, hp, layer, group_w, group_b, expert_w, expert_b, gate_w, up_w, down_w, ln_g, ln_b,
               next_qkv_w):
    n = h.shape[0]
    ew = jnp.transpose(expert_w, (1, 0, 2)).reshape(D_MODEL, MOE_EXPERTS)
    w_r = jnp.zeros((D_MODEL, LANES), F32).at[:, 0:MOE_GROUPS].set(group_w).at[:, 8:8 + MOE_EXPERTS].set(ew)
    b_r = jnp.zeros((LANES,), F32).at[0:MOE_GROUPS].set(group_b).at[8:8 + MOE_EXPERTS].set(expert_b.reshape(-1))
    ints, flts, cnt = _router(h, w_r, b_r.reshape(LANES, 1), 512)
    tr = MOE_ROW_TILE
    n_tiles = (2 * n) // tr + MOE_EXPERTS
    pos, meta = _plan(ints, cnt, layer, n_tiles)
    pos1, pos2 = pos[0], pos[1]

    y = _experts(hp, pos1, pos2, meta[0, :n_tiles], meta[2, :n_tiles], meta[1, :1],
                 gate_w.reshape(-1, D_MODEL, MOE_D_FF),
                 up_w.reshape(-1, D_MODEL, MOE_D_FF),
                 down_w.reshape(-1, MOE_D_FF, D_MODEL))
    tm = COMBINE_TILE
    fused = []
    if next_qkv_w is not None:
        fused = [gi for gi, (_, dil, _) in enumerate(ATTN_GROUPS) if tm % (ATTN_BLOCK * dil) == 0]
    res = _combine(y, h, pos1, pos2, flts[0].reshape(n, 1), flts[1].reshape(n, 1),
                   ln_g.reshape(1, -1), ln_b.reshape(1, -1), tm, next_qkv_w is not None,
                   [(next_qkv_w[gi], ATTN_GROUPS[gi][1]) for gi in fused])
    if next_qkv_w is None:
        return res[0], None, {}
    return res[0], res[1], dict(zip(fused, res[2:]))


def _pad_heads(w, axis):
    parts = []
    h0 = 0
    for _, _, nh in ATTN_GROUPS:
        sl = [slice(None)] * w.ndim
        sl[axis] = slice(h0 * ATTN_HEAD_DIM, (h0 + nh) * ATTN_HEAD_DIM)
        part = w[tuple(sl)]
        pad = [(0, 0)] * w.ndim
        pad[axis] = (0, ATTN_SLAB - nh * ATTN_HEAD_DIM)
        parts.append(jnp.pad(part, pad))
        h0 += nh
    return parts


def _in_proj_kernel(x_ref, w_ref, wdt_ref, z_ref, xbc_ref, dt_ref, *, z_tiles):
    j = pl.program_id(1)
    acc = _dot(x_ref[...].astype(BF16), w_ref[...])

    @pl.when(j < z_tiles)
    def _():
        z_ref[...] = acc

    @pl.when(j >= z_tiles)
    def _():
        xbc_ref[...] = acc

    @pl.when(j == 0)
    def _():
        dt_ref[...] = _dot_f32ish(x_ref[...], wdt_ref[...])


def _in_proj(x, w_zx, w_dt, tm, tn):
    m, k = x.shape
    z_tiles = SSM_D_INNER // tn
    n_tiles = w_zx.shape[1] // tn
    return pl.pallas_call(
        functools.partial(_in_proj_kernel, z_tiles=z_tiles),
        grid=(m // tm, n_tiles),
        in_specs=[pl.BlockSpec((tm, k), lambda i, j: (i, 0)),
                  pl.BlockSpec((k, tn), lambda i, j: (0, j)),
                  pl.BlockSpec((k, LANES), lambda i, j: (0, 0))],
        out_specs=[pl.BlockSpec((tm, tn), lambda i, j: (i, jnp.minimum(j, z_tiles - 1))),
                   pl.BlockSpec((tm, tn), lambda i, j: (i, jnp.maximum(j - z_tiles, 0))),
                   pl.BlockSpec((tm, LANES), lambda i, j: (i, 0))],
        out_shape=[jax.ShapeDtypeStruct((m, SSM_D_INNER), F32),
                   jax.ShapeDtypeStruct((m, SSM_CONV_DIM), F32),
                   jax.ShapeDtypeStruct((m, LANES), F32)],
        compiler_params=_cparams("parallel", "arbitrary"),
        name="ssm_in_proj",
    )(x, w_zx, w_dt)


def _ssd_layer(h, in_w, conv_w, conv_b, dt_bias, a_log, d_skip, norm_w, out_w, ln_g, ln_b, bsz, seq):
    split = SSM_D_INNER + SSM_CONV_DIM
    dt_w = jnp.pad(in_w[:, split:], ((0, 0), (0, LANES - SSM_HEADS)))
    z, xbc, dt_raw = _in_proj(h, in_w[:, :split].astype(BF16), dt_w, 1024, 1024)
    pad32 = lambda v: jnp.pad(v, (0, LANES - SSM_HEADS)).reshape(1, LANES)
    y = _ssd(z, xbc, dt_raw, conv_w, conv_b.reshape(1, -1), pad32(dt_bias), pad32(a_log),
             jnp.repeat(d_skip, SSM_HEAD_DIM).reshape(1, -1), norm_w.reshape(1, -1), bsz, seq)
    return _matmul_res_ln(y, out_w.astype(BF16), h, ln_g.reshape(1, -1), ln_b.reshape(1, -1), 512)


def _qkv_group_weights(kv_w, q_w):
    width = ATTN_HEADS * ATTN_HEAD_DIM
    w_q = _pad_heads(q_w * (ATTN_HEAD_DIM ** -0.5), 1)
    w_k = _pad_heads(kv_w[:, :width], 1)
    w_v = _pad_heads(kv_w[:, width:], 1)
    return [jnp.concatenate([w_q[gi], w_k[gi], w_v[gi]], axis=1).astype(BF16) for gi in range(len(ATTN_GROUPS))]


def _attn_layer(h, h8, qkv_w, qkv_done, o_w, rel_bias, ln_g, ln_b, bsz, seq):
    outs, lses = [], []
    h0 = 0
    for gi, (_, dil, nh) in enumerate(ATTN_GROUPS):
        if gi in qkv_done:
            qkv = qkv_done[gi]
        elif dil == 1:
            qkv = _matmul(h, qkv_w[gi], BF16, 1024, 3 * ATTN_SLAB)
        else:
            qkv = _qkv_dilated(h8, qkv_w[gi], dil, max(1024, ATTN_BLOCK * dil))
        bias_p, bias_c = _group_bias(rel_bias, h0, nh, dil)
        o, lse = _attention_group(qkv, bias_p, bias_c, gi, dil, nh, bsz, seq)
        outs.extend(o)
        lses.append(lse)
        h0 += nh
    w_o = jnp.concatenate(_pad_heads(o_w, 0), axis=0).astype(BF16)
    return _attn_out(outs, lses, w_o, h, ln_g.reshape(1, -1), ln_b.reshape(1, -1), 512)


def kernel(x, ssm_in_w, ssm_conv_w, ssm_conv_b, ssm_dt_bias, ssm_a_log, ssm_d, ssm_norm_w, ssm_out_w,
           kv_w, attn_q_w, attn_o_w, rel_bias, moe_group_w, moe_group_b, moe_expert_w, moe_expert_b,
           moe_gate_w, moe_up_w, moe_down_w, ln_g, ln_b):
    bsz, seq, d = x.shape
    h = x.reshape(bsz * seq, d)
    n_ssd = DEPTH // 2
    h8, qkv_done = None, {}
    for i in range(DEPTH):
        if i < n_ssd:
            h, hp = _ssd_layer(h, ssm_in_w[i], ssm_conv_w[i], ssm_conv_b[i], ssm_dt_bias[i], ssm_a_log[i],
                               ssm_d[i], ssm_norm_w[i], ssm_out_w[i], ln_g[i, 0], ln_b[i, 0], bsz, seq)
        else:
            j = i - n_ssd
            h, hp = _attn_layer(h, h8, _qkv_group_weights(kv_w, attn_q_w[j]), qkv_done, attn_o_w[j],
                                rel_bias, ln_g[i, 0], ln_b[i, 0], bsz, seq)
        nxt = i + 1
        next_qkv_w = _qkv_group_weights(kv_w, attn_q_w[nxt - n_ssd]) if n_ssd <= nxt < DEPTH else None
        h, h8, qkv_done = _moe_layer(h, hp, i, moe_group_w[i], moe_group_b[i], moe_expert_w[i],
                                     moe_expert_b[i], moe_gate_w, moe_up_w, moe_down_w,
                                     ln_g[i, 1], ln_b[i, 1], next_qkv_w)
    return h.reshape(bsz, seq, d)
```

```python
import functools
import math

import numpy as np
import jax
import jax.numpy as jnp
from jax import lax
from jax.experimental import pallas as pl
from jax.experimental.pallas import tpu as pltpu

F32 = jnp.float32
BF16 = jnp.bfloat16
I32 = jnp.int32

D_MODEL = 1024
DEPTH = 2
DN_ALPHA = (2 * DEPTH) ** 0.25
LN_EPS = 1e-5
LOG2_E = math.log2(math.e)

SSM_D_INNER = 2048
SSM_HEAD_DIM = 64
SSM_HEADS = 32
SSM_GROUPS = 4
SSM_STATE = 128
SSM_CONV = 4
SSM_CHUNK = 128
SSM_CONV_DIM = SSM_D_INNER + 2 * SSM_GROUPS * SSM_STATE

ATTN_HEAD_DIM = 64
ATTN_GROUPS = ((128, 1, 6), (512, 4, 5), (2048, 16, 5))
ATTN_HEADS = 16
ATTN_BLOCK = 128
N_BUCKETS = 32
MAX_DISTANCE = 2048
ATTN_SLAB = 384
NEG_BIG = -1e30

MOE_GROUPS = 4
MOE_EPG = 8
MOE_EXPERTS = MOE_GROUPS * MOE_EPG
MOE_D_FF = 512
MOE_ROW_TILE = 256
COMBINE_TILE = 512

LANES = 128
DMA_QUEUES = 2
VMEM_LIMIT = 48 * 1024 * 1024
EXPERTS_VMEM_LIMIT = 56 * 1024 * 1024


def _cparams(*sem):
    return pltpu.CompilerParams(dimension_semantics=sem, vmem_limit_bytes=VMEM_LIMIT)


def _layer_norm(x, g, b):
    mu = jnp.mean(x, -1, keepdims=True)
    xc = x - mu
    var = jnp.mean(xc * xc, -1, keepdims=True)
    return xc * lax.rsqrt(var + LN_EPS) * g + b


def _split2(x):
    hi = x.astype(BF16)
    lo = (x - hi.astype(F32)).astype(BF16)
    return hi, lo


def _dot(a, b):
    return jnp.dot(a, b, preferred_element_type=F32)


def _dot_f32ish(a, b):
    ah, al = _split2(a)
    bh, bl = _split2(b)
    return _dot(ah, bh) + _dot(al, bh) + _dot(ah, bl)


def _mm_kernel(a_ref, b_ref, o_ref):
    o_ref[...] = _dot(a_ref[...].astype(BF16), b_ref[...]).astype(o_ref.dtype)


def _matmul(a, b, out_dtype, tm, tn):
    m, k = a.shape
    nc = b.shape[1]
    return pl.pallas_call(
        _mm_kernel,
        grid=(m // tm, nc // tn),
        in_specs=[pl.BlockSpec((tm, k), lambda i, j: (i, 0)),
                  pl.BlockSpec((k, tn), lambda i, j: (0, j))],
        out_specs=pl.BlockSpec((tm, tn), lambda i, j: (i, j)),
        out_shape=jax.ShapeDtypeStruct((m, nc), out_dtype),
        compiler_params=_cparams("parallel", "parallel"),
        name="matmul",
    )(a, b)


def _mm3_kernel(a_ref, b_ref, o_ref):
    o_ref[...] = _dot_f32ish(a_ref[...], b_ref[...])


def _matmul_f32ish(a, b, tm):
    m, k = a.shape
    nc = b.shape[1]
    return pl.pallas_call(
        _mm3_kernel,
        grid=(m // tm,),
        in_specs=[pl.BlockSpec((tm, k), lambda i: (i, 0)),
                  pl.BlockSpec((k, nc), lambda i: (0, 0))],
        out_specs=pl.BlockSpec((tm, nc), lambda i: (i, 0)),
        out_shape=jax.ShapeDtypeStruct((m, nc), F32),
        compiler_params=_cparams("parallel"),
        name="matmul_f32ish",
    )(a, b)


ROW_SPLIT = D_MODEL // LANES


def _store_row_tiled(ref, val):
    rows = val.shape[0]
    for c in range(ROW_SPLIT):
        ref[pl.ds(c, rows, stride=ROW_SPLIT), :] = val[:, c * LANES:(c + 1) * LANES]


def _load_row_tiled(ref):
    rows = ref.shape[0] // ROW_SPLIT
    return jnp.concatenate([ref[pl.ds(c, rows, stride=ROW_SPLIT), :] for c in range(ROW_SPLIT)], axis=1)


HALF = D_MODEL // 2
U32 = jnp.uint32
HI16 = 0xFFFF0000


def _pack_bf16_pairs(x):
    bits = lax.bitcast_convert_type(x.astype(BF16).astype(F32), U32)
    return (bits[:, :HALF] >> 16) | (bits[:, HALF:] & U32(HI16))


def _unpack_bf16_pairs(w):
    lo = lax.bitcast_convert_type(w << 16, F32).astype(BF16)
    hi = lax.bitcast_convert_type(w & U32(HI16), F32).astype(BF16)
    return lo, hi


def _mm_ln_kernel(a_ref, w_ref, h_ref, g_ref, b_ref, o_ref, op_ref):
    acc = _dot(a_ref[...], w_ref[...])
    out = _layer_norm(DN_ALPHA * h_ref[...] + acc, g_ref[...], b_ref[...])
    o_ref[...] = out
    op_ref[...] = _pack_bf16_pairs(out)


def _matmul_res_ln(a, w, h, g, b, tm):
    m, k = a.shape
    d = w.shape[1]
    return pl.pallas_call(
        _mm_ln_kernel,
        grid=(m // tm,),
        in_specs=[pl.BlockSpec((tm, k), lambda i: (i, 0)),
                  pl.BlockSpec((k, d), lambda i: (0, 0)),
                  pl.BlockSpec((tm, d), lambda i: (i, 0)),
                  pl.BlockSpec((1, d), lambda i: (0, 0)),
                  pl.BlockSpec((1, d), lambda i: (0, 0))],
        out_specs=[pl.BlockSpec((tm, d), lambda i: (i, 0)),
                   pl.BlockSpec((tm, HALF), lambda i: (i, 0))],
        out_shape=[jax.ShapeDtypeStruct((m, d), F32),
                   jax.ShapeDtypeStruct((m, HALF), U32)],
        compiler_params=_cparams("parallel"),
        name="matmul_res_ln",
    )(a, w, h, g, b)


SSD_CHUNKS_PER_STEP = 2


def _ssd_kernel(z_ref, xbc_ref, dt_ref, cw_ref, cb_ref, dtb_ref, alog_ref, dsk_ref, nw_ref,
                y_ref, xe_ref, st_ref):
    @pl.when(pl.program_id(1) == 0)
    def _():
        xe_ref[:, 0:8, :] = jnp.zeros((xe_ref.shape[0], 8, LANES), F32)
        st_ref[...] = jnp.zeros_like(st_ref)

    for sub in range(SSD_CHUNKS_PER_STEP):
        _ssd_chunk(pl.ds(sub * SSM_CHUNK, SSM_CHUNK), z_ref, xbc_ref, dt_ref, cw_ref, cb_ref, dtb_ref,
                   alog_ref, dsk_ref, nw_ref, y_ref, xe_ref, st_ref)


def _ssd_chunk(rows, z_ref, xbc_ref, dt_ref, cw_ref, cb_ref, dtb_ref, alog_ref, dsk_ref, nw_ref,
               y_ref, xe_ref, st_ref):
    q = SSM_CHUNK
    w = cw_ref[...]
    bias = cb_ref[...]
    act = []
    for c in range(SSM_CONV_DIM // LANES):
        cols = slice(c * LANES, (c + 1) * LANES)
        u = xbc_ref[rows, cols]
        xe_ref[c, 8:8 + q, :] = u
        conv = (bias[:, cols] + w[3:4, cols] * u + w[2:3, cols] * xe_ref[c, 7:7 + q, :]
                + w[1:2, cols] * xe_ref[c, 6:6 + q, :] + w[0:1, cols] * xe_ref[c, 5:5 + q, :])
        xe_ref[c, 0:8, :] = xe_ref[c, q:q + 8, :]
        act.append(conv * jax.nn.sigmoid(conv))

    pre = dt_ref[rows, :] + dtb_ref[...]
    dt = jnp.maximum(pre, 0.0) + jnp.log(1.0 + jnp.exp(-jnp.abs(pre)))
    adt = dt * (-jnp.exp(alog_ref[...]) * LOG2_E)

    row = lax.broadcasted_iota(I32, (q, q), 0)
    col = lax.broadcasted_iota(I32, (q, q), 1)
    tril = row >= col
    tri_b = jnp.where(tril, 1.0, 0.0).astype(BF16)
    a_hi = adt.astype(BF16)
    r1 = adt - a_hi.astype(F32)
    a_mid = r1.astype(BF16)
    a_lo = (r1 - a_mid.astype(F32)).astype(BF16)
    acs = _dot(tri_b, a_hi) + _dot(tri_b, a_mid) + _dot(tri_b, a_lo)
    acs_t = acs.T
    acs_dt_t = acs_t - jnp.log(dt.T) * LOG2_E
    eacs = jnp.exp2(acs)
    left = col < SSM_HEAD_DIM

    for g in range(SSM_GROUPS):
        b0 = SSM_D_INNER + g * SSM_STATE
        c0 = SSM_D_INNER + SSM_GROUPS * SSM_STATE + g * SSM_STATE
        bm = act[b0 // LANES]
        cm = act[c0 // LANES].astype(BF16)
        cb = lax.dot_general(cm, bm.astype(BF16), (((1,), (1,)), ((), ())),
                             preferred_element_type=F32)
        bm_t = bm.T
        gs = g * 512
        y_off = _dot(cm, st_ref[:, gs:gs + 512].astype(BF16))
        slabs = []
        for pr in range(4):
            ha = g * 8 + pr * 2
            hb = ha + 1
            cs = gs + pr * LANES
            x2 = act[cs // LANES]
            x2b = x2.astype(BF16)
            ys, ups = [], []
            for h in (ha, hb):
                a_col = acs[:, h:h + 1]
                a_src = acs_dt_t[h:h + 1, :]
                decay = jnp.where(tril, jnp.exp2(a_col - a_src), 0.0)
                ys.append(_dot((cb * decay).astype(BF16), x2b))
                to_end = jnp.exp2(acs_t[h:h + 1, q - 1:q] - a_src)
                ups.append(_dot((bm_t * to_end).astype(BF16), x2b))
            y_diag = jnp.where(left, ys[0], ys[1])
            upd = jnp.where(left, ups[0], ups[1])
            e2 = jnp.where(left, eacs[:, ha:ha + 1], eacs[:, hb:hb + 1])
            cd = jnp.where(left[0:1, :], eacs[q - 1:q, ha:ha + 1], eacs[q - 1:q, hb:hb + 1])
            y2 = y_diag + y_off[:, pr * LANES:(pr + 1) * LANES] * e2 + dsk_ref[:, cs:cs + LANES] * x2
            st_ref[:, cs:cs + LANES] = st_ref[:, cs:cs + LANES] * cd + upd
            slabs.append(y2)
        yg = jnp.concatenate(slabs, axis=1)
        zg = z_ref[rows, gs:gs + 512]
        yg = yg * (zg * jax.nn.sigmoid(zg))
        ms = jnp.mean(yg * yg, -1, keepdims=True)
        y_ref[rows, gs:gs + 512] = (yg * lax.rsqrt(ms + LN_EPS) * nw_ref[:, gs:gs + 512]).astype(y_ref.dtype)


def _ssd(z, xbc, dt_raw, conv_w, conv_b, dt_bias, a_log, d_rep, norm_w, bsz, seq):
    n = z.shape[0]
    q = SSM_CHUNK * SSD_CHUNKS_PER_STEP
    nchunk = seq // q
    tok = lambda b, c: (b * nchunk + c, 0)
    fixed = lambda b, c: (0, 0)
    return pl.pallas_call(
        _ssd_kernel,
        grid=(bsz, nchunk),
        in_specs=[pl.BlockSpec((q, SSM_D_INNER), tok),
                  pl.BlockSpec((q, SSM_CONV_DIM), tok),
                  pl.BlockSpec((q, LANES), tok),
                  pl.BlockSpec((SSM_CONV, SSM_CONV_DIM), fixed),
                  pl.BlockSpec((1, SSM_CONV_DIM), fixed),
                  pl.BlockSpec((1, LANES), fixed),
                  pl.BlockSpec((1, LANES), fixed),
                  pl.BlockSpec((1, SSM_D_INNER), fixed),
                  pl.BlockSpec((1, SSM_D_INNER), fixed)],
        out_specs=pl.BlockSpec((q, SSM_D_INNER), tok),
        out_shape=jax.ShapeDtypeStruct((n, SSM_D_INNER), BF16),
        scratch_shapes=[pltpu.VMEM((SSM_CONV_DIM // LANES, SSM_CHUNK + 8, LANES), F32),
                        pltpu.VMEM((SSM_STATE, SSM_D_INNER), F32)],
        compiler_params=_cparams("parallel", "arbitrary"),
        name="ssd_chunk",
    )(z, xbc, dt_raw, conv_w, conv_b, dt_bias, a_log, d_rep, norm_w)


def _residue_major_pieces(h8_ref, n_tokens, dil):
    span = ATTN_BLOCK * dil
    pieces = []
    for blk in range(n_tokens // span):
        for r in range(dil):
            first = (blk * span + r) * ROW_SPLIT
            pieces.append(jnp.concatenate(
                [h8_ref[pl.ds(first + c, ATTN_BLOCK, stride=ROW_SPLIT * dil), :] for c in range(ROW_SPLIT)],
                axis=1).astype(BF16))
    return pieces


def _qkv_dilated_kernel(h8_ref, w_ref, o_ref, *, dil):
    per_dot = 4
    rows = _residue_major_pieces(h8_ref, o_ref.shape[0], dil)
    for k in range(0, len(rows), per_dot):
        x = jnp.concatenate(rows[k:k + per_dot], axis=0)
        o_ref[k * ATTN_BLOCK:(k + per_dot) * ATTN_BLOCK, :] = _dot(x, w_ref[...]).astype(o_ref.dtype)


def _qkv_dilated(h8, w, dil, tm):
    n = h8.shape[0] // ROW_SPLIT
    nc = w.shape[1]
    return pl.pallas_call(
        functools.partial(_qkv_dilated_kernel, dil=dil),
        grid=(n // tm,),
        in_specs=[pl.BlockSpec((tm * ROW_SPLIT, LANES), lambda i: (i, 0)),
                  pl.BlockSpec((D_MODEL, nc), lambda i: (0, 0))],
        out_specs=pl.BlockSpec((tm, nc), lambda i: (i, 0)),
        out_shape=jax.ShapeDtypeStruct((n, nc), BF16),
        compiler_params=_cparams("parallel"),
        name=f"qkv_dil{dil}",
    )(h8, w)


def _attn_kernel(*refs, nh, dil, has_prev):
    if has_prev:
        q_ref, kp_ref, kc_ref, vp_ref, vc_ref, bp_ref, bc_ref = refs[:7]
        out_refs = refs[7:]
        first_pen = jnp.where(pl.program_id(1) > 0, 0.0, NEG_BIG)
    else:
        q_ref, kc_ref, vc_ref, bc_ref = refs[:4]
        out_refs = refs[4:]
    s_scr, p_scr, max_scr, den_scr = out_refs[-4:]
    o_refs, lse_ref = out_refs[:-5], out_refs[-5]
    ones = jnp.ones((s_scr.shape[2], LANES), BF16)
    lane = lax.broadcasted_iota(I32, (ATTN_BLOCK, LANES), 1)
    left = lane < ATTN_HEAD_DIM
    zero = jnp.zeros((), BF16)
    nt = (((1,), (1,)), ((), ()))

    def residue(r, carry):
        rows = pl.ds(pl.multiple_of(r * ATTN_BLOCK, ATTN_BLOCK), ATTN_BLOCK)
        dst = pl.ds(r, ATTN_BLOCK, stride=dil)
        for hh in range(nh):
            cols = pl.ds(hh // 2 * LANES, LANES)
            qm = jnp.where(left if hh % 2 == 0 else ~left, q_ref[rows, cols], zero)
            s_c = lax.dot_general(qm, kc_ref[rows, cols], nt, preferred_element_type=F32) + bc_ref[hh]
            if has_prev:
                s_p = lax.dot_general(qm, kp_ref[rows, cols], nt, preferred_element_type=F32)
                s_scr[hh, :, 0:ATTN_BLOCK] = s_p + (bp_ref[hh] + first_pen)
                s_scr[hh, :, ATTN_BLOCK:] = s_c
            else:
                s_scr[hh] = s_c
        for hh in range(nh):
            s = s_scr[hh]
            m = jnp.max(s, -1, keepdims=True)
            p = jnp.exp(s - m)
            p_scr[hh] = p.astype(BF16)
            max_scr[hh] = m
            if not has_prev:
                den_scr[hh] = jnp.sum(p, -1, keepdims=True)
        lse_sum = jnp.zeros((ATTN_BLOCK, 1), F32)
        for pr in range(ATTN_SLAB // LANES):
            cols = pl.ds(pr * LANES, LANES)
            halves = []
            for hh in (pr * 2, pr * 2 + 1):
                if hh >= nh:
                    halves.append(jnp.zeros((ATTN_BLOCK, LANES), F32))
                    continue
                if has_prev:
                    den = _dot(p_scr[hh], ones)
                    den1 = den[:, 0:1]
                    o = (_dot(p_scr[hh, :, 0:ATTN_BLOCK], vp_ref[rows, cols])
                         + _dot(p_scr[hh, :, ATTN_BLOCK:], vc_ref[rows, cols]))
                else:
                    den = den1 = den_scr[hh]
                    o = _dot(p_scr[hh], vc_ref[rows, cols])
                halves.append(o * (1.0 / den))
                lse_sum = lse_sum + (max_scr[hh] + jnp.log(den1))
            o_refs[pr][dst, :] = jnp.where(left, halves[0], halves[1])
        lse_ref[dst, :] = jnp.broadcast_to(lse_sum * (1.0 / nh), (ATTN_BLOCK, LANES))
        return carry

    lax.fori_loop(0, dil, residue, 0)


def _attention_group(qkv, bias_p, bias_c, gi, dil, nh, bsz, seq):
    span = ATTN_BLOCK * dil
    nb = seq // span
    has_prev = nb > 1
    blk = (span, ATTN_SLAB)
    cur = lambda which: (lambda b, n: (b * nb + n, which))
    prev = lambda which: (lambda b, n: (b * nb + jnp.maximum(n - 1, 0), which))
    fixed = lambda b, n: (0, 0, 0)
    tab = pl.BlockSpec((nh, ATTN_BLOCK, ATTN_BLOCK), fixed)
    if has_prev:
        in_specs = [pl.BlockSpec(blk, cur(0)), pl.BlockSpec(blk, prev(1)), pl.BlockSpec(blk, cur(1)),
                    pl.BlockSpec(blk, prev(2)), pl.BlockSpec(blk, cur(2)), tab, tab]
        args = (qkv, qkv, qkv, qkv, qkv, bias_p, bias_c)
    else:
        in_specs = [pl.BlockSpec(blk, cur(0)), pl.BlockSpec(blk, cur(1)), pl.BlockSpec(blk, cur(2)), tab]
        args = (qkv, qkv, qkv, bias_c)
    n_out = ATTN_SLAB // LANES + 1
    keys = 2 * ATTN_BLOCK if has_prev else ATTN_BLOCK
    outs = pl.pallas_call(
        functools.partial(_attn_kernel, nh=nh, dil=dil, has_prev=has_prev),
        grid=(bsz, nb),
        in_specs=in_specs,
        out_specs=[pl.BlockSpec((span, LANES), lambda b, n: (b * nb + n, 0))] * n_out,
        out_shape=[jax.ShapeDtypeStruct((bsz * seq, LANES), F32)] * n_out,
        scratch_shapes=[pltpu.VMEM((nh, ATTN_BLOCK, keys), F32),
                        pltpu.VMEM((nh, ATTN_BLOCK, keys), BF16),
                        pltpu.VMEM((nh, ATTN_BLOCK, 1), F32),
                        pltpu.VMEM((nh, ATTN_BLOCK, 1), F32)],
        compiler_params=_cparams("parallel", "parallel"),
        name=f"dilated_attn_g{gi}",
    )(*args)
    return outs[:-1], outs[-1]


def _attn_out_kernel(*refs):
    n_pairs = ATTN_SLAB // LANES
    n_groups = len(ATTN_GROUPS)
    o_refs = refs[:n_groups * n_pairs]
    l_refs = refs[n_groups * n_pairs:n_groups * (n_pairs + 1)]
    w_ref, h_ref, g_ref, b_ref, out_ref, outp_ref = refs[n_groups * (n_pairs + 1):]
    ls = [r[:, 0:1] for r in l_refs]
    m = jnp.maximum(jnp.maximum(ls[0], ls[1]), ls[2])
    es = [jnp.exp(v - m) for v in ls]
    scale = n_groups / (es[0] + es[1] + es[2])
    slabs = []
    for gi in range(n_groups):
        wt = es[gi] * scale
        slabs.extend((r[...] * wt).astype(BF16) for r in o_refs[gi * n_pairs:(gi + 1) * n_pairs])
    acc = _dot(jnp.concatenate(slabs, axis=1), w_ref[...])
    out = _layer_norm(DN_ALPHA * h_ref[...] + acc, g_ref[...], b_ref[...])
    out_ref[...] = out
    outp_ref[...] = _pack_bf16_pairs(out)


def _attn_out(outs, lses, w, h, g, b, tm):
    n = h.shape[0]
    tok = lambda i: (i, 0)
    fixed2 = lambda i: (0, 0)
    return pl.pallas_call(
        _attn_out_kernel,
        grid=(n // tm,),
        in_specs=[pl.BlockSpec((tm, LANES), tok)] * (len(outs) + len(lses))
                 + [pl.BlockSpec((len(ATTN_GROUPS) * ATTN_SLAB, D_MODEL), lambda i: (0, 0)),
                    pl.BlockSpec((tm, D_MODEL), tok),
                    pl.BlockSpec((1, D_MODEL), fixed2), pl.BlockSpec((1, D_MODEL), fixed2)],
        out_specs=[pl.BlockSpec((tm, D_MODEL), tok), pl.BlockSpec((tm, HALF), tok)],
        out_shape=[jax.ShapeDtypeStruct((n, D_MODEL), F32),
                   jax.ShapeDtypeStruct((n, HALF), U32)],
        compiler_params=_cparams("parallel"),
        name="attn_out_ln",
    )(*outs, *lses, w, h, g, b)


def _t5_bucket(dist):
    max_exact = N_BUCKETS // 2
    n = np.maximum(dist, 1).astype(np.float64)
    large = max_exact + (np.log(n / max_exact) / np.log(MAX_DISTANCE / max_exact)
                         * (N_BUCKETS - max_exact)).astype(np.int32)
    large = np.minimum(large, N_BUCKETS - 1)
    return np.where(dist < max_exact, dist, large).astype(np.int32)


def _group_bias(rel_bias, h0, nh, dil):
    qi = np.arange(ATTN_BLOCK)[:, None]
    ki = np.arange(ATTN_BLOCK)[None, :]
    tabs = []
    for delta, band in ((qi + ATTN_BLOCK - ki, ki >= qi), (qi - ki, ki <= qi)):
        bucket = _t5_bucket(np.clip(delta, 0, None) * dil)
        onehot = (bucket[..., None] == np.arange(N_BUCKETS)).astype(np.float32)
        t = jnp.einsum("qkb,bh->hqk", onehot, rel_bias[:, h0:h0 + nh], precision=lax.Precision.HIGHEST)
        tabs.append(jnp.where(band[None], t, NEG_BIG).astype(F32))
    return tabs


def _router_kernel(h_ref, w_ref, b_ref, ints_ref, flts_ref, cnt_ref, carry_ref, before_ref, *, tm):
    @pl.when(pl.program_id(0) == 0)
    def _():
        carry_ref[...] = jnp.zeros_like(carry_ref)
        tr = lax.broadcasted_iota(I32, (tm, tm), 0)
        tc = lax.broadcasted_iota(I32, (tm, tm), 1)
        before_ref[...] = jnp.where(tr < tc, 1.0, 0.0).astype(BF16)

    lt = _dot_f32ish(h_ref[...], w_ref[...]).T + b_ref[...]
    gl = lt[0:MOE_GROUPS]
    r4 = lax.broadcasted_iota(I32, (MOE_GROUPS, tm), 0)
    gmax = jnp.max(gl, 0, keepdims=True)
    gidx = jnp.min(jnp.where(gl == gmax, r4, MOE_GROUPS), 0, keepdims=True)
    gval = 1.0 / jnp.sum(jnp.exp(gl - gmax), 0, keepdims=True)

    esel = jnp.zeros((MOE_EPG, tm), F32)
    for g in range(MOE_GROUPS):
        esel = jnp.where(gidx == g, lt[8 + g * MOE_EPG:8 + (g + 1) * MOE_EPG], esel)
    r8 = lax.broadcasted_iota(I32, (MOE_EPG, tm), 0)
    v1 = jnp.max(esel, 0, keepdims=True)
    i1 = jnp.min(jnp.where(esel == v1, r8, MOE_EPG), 0, keepdims=True)
    rest = jnp.where(r8 == i1, -jnp.inf, esel)
    v2 = jnp.max(rest, 0, keepdims=True)
    i2 = jnp.min(jnp.where(rest == v2, r8, MOE_EPG), 0, keepdims=True)
    t = jnp.exp(v2 - v1)
    p1 = gval / (1.0 + t)
    p2 = p1 * t
    e1 = gidx * MOE_EPG + i1
    e2 = gidx * MOE_EPG + i2

    r32 = lax.broadcasted_iota(I32, (MOE_EXPERTS, tm), 0)
    oh1 = r32 == e1
    oh2 = r32 == e2
    oh = jnp.where(oh1 | oh2, 1.0, 0.0)
    base = _dot(oh.astype(BF16), before_ref[...]) + carry_ref[:, 0:1]
    rank1 = jnp.sum(jnp.where(oh1, base, 0.0), 0, keepdims=True).astype(I32)
    rank2 = jnp.sum(jnp.where(oh2, base, 0.0), 0, keepdims=True).astype(I32)
    carry_ref[...] = carry_ref[...] + jnp.sum(oh, 1, keepdims=True)
    cnt_ref[...] = carry_ref[...]

    ints_ref[...] = jnp.where(r8 == 0, e1, jnp.where(r8 == 1, e2, jnp.where(r8 == 2, rank1,
                              jnp.where(r8 == 3, rank2, 0))))
    flts_ref[...] = jnp.where(r8 == 0, p1, jnp.where(r8 == 1, p2, 0.0))


def _router(h, w_r, b_r, tm):
    n = h.shape[0]
    return pl.pallas_call(
        functools.partial(_router_kernel, tm=tm),
        grid=(n // tm,),
        in_specs=[pl.BlockSpec((tm, D_MODEL), lambda i: (i, 0)),
                  pl.BlockSpec((D_MODEL, LANES), lambda i: (0, 0)),
                  pl.BlockSpec((LANES, 1), lambda i: (0, 0))],
        out_specs=[pl.BlockSpec((8, tm), lambda i: (0, i)),
                   pl.BlockSpec((8, tm), lambda i: (0, i)),
                   pl.BlockSpec((MOE_EXPERTS, LANES), lambda i: (0, 0))],
        out_shape=[jax.ShapeDtypeStruct((8, n), I32),
                   jax.ShapeDtypeStruct((8, n), F32),
                   jax.ShapeDtypeStruct((MOE_EXPERTS, LANES), F32)],
        scratch_shapes=[pltpu.VMEM((MOE_EXPERTS, LANES), F32), pltpu.VMEM((tm, tm), BF16)],
        compiler_params=_cparams("arbitrary"),
        name="moe_router",
    )(h, w_r, b_r)


def _row_copy(src_ref, s, dst_ref, d, sem):
    def first(r):
        return r * ROW_SPLIT if isinstance(r, int) else pl.multiple_of(r * ROW_SPLIT, ROW_SPLIT)

    return pltpu.make_async_copy(src_ref.at[pl.ds(first(s), ROW_SPLIT), :],
                                 dst_ref.at[pl.ds(first(d), ROW_SPLIT), :], sem)


def _tile_wait(src_ref, dst_ref, sem):
    pltpu.make_async_copy(src_ref.at[pl.ds(0, dst_ref.shape[0]), :], dst_ref, sem).wait()


def _experts_kernel(pos1_ref, pos2_ref, te_ref, nxt_ref, na_ref, hp_ref, zeros_ref, wg_ref, wu_ref, wd_ref,
                    y_ref, xa, xb, wg_f, wu_f, wd_f, wg_s, wu_s, wd_s, row_tok, slot_ref, sem, wsem):
    i = pl.program_id(0)
    n_tiles = pl.num_programs(0)
    tr = MOE_ROW_TILE
    active = i < na_ref[0]
    changed = jnp.logical_or(i == 0, te_ref[i] != te_ref[jnp.maximum(i - 1, 0)])

    def weight_copies(expert, slot):
        return [pltpu.make_async_copy(src.at[expert], dst.at[slot], wsem.at[slot])
                for src, dst in ((wg_ref, wg_f), (wu_ref, wu_f), (wd_ref, wd_f))]

    @pl.when(i == 0)
    def _():
        slot_ref[0] = 0
        for c in weight_copies(te_ref[0], 0):
            c.start()
        clear = pltpu.make_async_copy(zeros_ref, row_tok, sem)
        clear.start()
        clear.wait()

        def invert(t, carry):
            row_tok[pos1_ref[t]] = t
            row_tok[pos2_ref[t]] = t
            return carry

        lax.fori_loop(0, pos1_ref.shape[0], invert, 0, unroll=16)

        def pick(r, carry):
            xa[pl.ds(r, 1), :] = hp_ref[pl.ds(row_tok[r], 1), :]
            return carry

        lax.fori_loop(0, tr, pick, 0, unroll=8)

    @pl.when(jnp.logical_and(active, changed))
    def _():
        slot = slot_ref[0]
        for c in weight_copies(te_ref[i], slot):
            c.wait()
        following = nxt_ref[i]

        @pl.when(following >= 0)
        def _():
            for c in weight_copies(following, 1 - slot):
                c.start()

        wg_s[...] = wg_f[slot].astype(BF16)
        wu_s[...] = wu_f[slot].astype(BF16)
        wd_s[...] = wd_f[slot].astype(BF16)
        slot_ref[0] = 1 - slot

    def step(cur, nxt):
        base = jnp.minimum(i + 1, n_tiles - 1) * tr
        for r in range(tr):
            nxt[r:r + 1, :] = hp_ref[pl.ds(row_tok[base + r], 1), :]
        lo, hi = _unpack_bf16_pairs(cur[...])
        gate = _dot(lo, wg_s[0:HALF, :]) + _dot(hi, wg_s[HALF:, :])
        up = _dot(lo, wu_s[0:HALF, :]) + _dot(hi, wu_s[HALF:, :])
        hid = (gate * jax.nn.sigmoid(gate) * up).astype(BF16)
        _store_row_tiled(y_ref, _dot(hid, wd_s[...]))

    even = i % 2 == 0

    @pl.when(jnp.logical_and(active, even))
    def _():
        step(xa, xb)

    @pl.when(jnp.logical_and(active, jnp.logical_not(even)))
    def _():
        step(xb, xa)

    @pl.when(jnp.logical_not(active))
    def _():
        y_ref[...] = jnp.zeros_like(y_ref)


def _experts(hp, pos1, pos2, tile_expert, next_expert, n_active, wg, wu, wd):
    tr = MOE_ROW_TILE
    n_tiles = tile_expert.shape[0]
    rows = n_tiles * tr
    hbm = pl.BlockSpec(memory_space=pl.ANY)
    return pl.pallas_call(
        _experts_kernel,
        grid_spec=pltpu.PrefetchScalarGridSpec(
            num_scalar_prefetch=5,
            grid=(n_tiles,),
            in_specs=[pl.BlockSpec(memory_space=pltpu.VMEM), hbm, hbm, hbm, hbm],
            out_specs=pl.BlockSpec((tr * ROW_SPLIT, LANES), lambda i, *_: (i, 0)),
            scratch_shapes=[pltpu.VMEM((tr, HALF), U32),
                            pltpu.VMEM((tr, HALF), U32),
                            pltpu.VMEM((2, D_MODEL, MOE_D_FF), F32),
                            pltpu.VMEM((2, D_MODEL, MOE_D_FF), F32),
                            pltpu.VMEM((2, MOE_D_FF, D_MODEL), F32),
                            pltpu.VMEM((D_MODEL, MOE_D_FF), BF16),
                            pltpu.VMEM((D_MODEL, MOE_D_FF), BF16),
                            pltpu.VMEM((MOE_D_FF, D_MODEL), BF16),
                            pltpu.SMEM((rows,), I32),
                            pltpu.SMEM((1,), I32),
                            pltpu.SemaphoreType.DMA(()),
                            pltpu.SemaphoreType.DMA((2,))]),
        out_shape=jax.ShapeDtypeStruct((rows * ROW_SPLIT, LANES), F32),
        compiler_params=pltpu.CompilerParams(dimension_semantics=("arbitrary",),
                                             vmem_limit_bytes=EXPERTS_VMEM_LIMIT),
        name="moe_experts",
    )(pos1, pos2, tile_expert, next_expert, n_active, hp, jnp.zeros((rows,), I32), wg, wu, wd)


def _combine_kernel(pos1_ref, pos2_ref, y_ref, h_ref, p1_ref, p2_ref, g_ref, b_ref, *rest, tm, tiled_copy):
    out_ref = rest[0]
    a1, a2, b1, b2, sem = rest[-5:]
    i = pl.program_id(0)
    last = pl.num_programs(0) - 1

    @pl.when(i == 0)
    def _():
        def issue(t, carry):
            _row_copy(y_ref, pos1_ref[t], a1, t, sem.at[0]).start()
            _row_copy(y_ref, pos2_ref[t], a2, t, sem.at[0]).start()
            return carry

        lax.fori_loop(0, tm, issue, 0)

    def step(c1, c2, cur_sem, n1, n2, nxt_sem):
        _tile_wait(y_ref, c1, cur_sem)
        _tile_wait(y_ref, c2, cur_sem)
        base = jnp.minimum(i + 1, last) * tm
        for t in range(tm):
            _row_copy(y_ref, pos1_ref[base + t], n1, t, nxt_sem).start(priority=0)
            _row_copy(y_ref, pos2_ref[base + t], n2, t, nxt_sem).start(priority=DMA_QUEUES - 1)
        ffn = p1_ref[...] * _load_row_tiled(c1) + p2_ref[...] * _load_row_tiled(c2)
        out = _layer_norm(DN_ALPHA * h_ref[...] + ffn, g_ref[...], b_ref[...])
        out_ref[...] = out
        if tiled_copy:
            _store_row_tiled(rest[1], out)

        @pl.when(i == last)
        def _():
            _tile_wait(y_ref, n1, nxt_sem)
            _tile_wait(y_ref, n2, nxt_sem)

    @pl.when(i % 2 == 0)
    def _():
        step(a1, a2, sem.at[0], b1, b2, sem.at[1])

    @pl.when(i % 2 == 1)
    def _():
        step(b1, b2, sem.at[1], a1, a2, sem.at[0])


def _combine(y, h, pos1, pos2, p1, p2, g, b, tm, tiled_copy):
    n = h.shape[0]
    tok = lambda i, a, c: (i, 0)
    fixed = lambda i, a, c: (0, 0)
    out_specs = [pl.BlockSpec((tm, D_MODEL), tok)]
    out_shape = [jax.ShapeDtypeStruct((n, D_MODEL), F32)]
    if tiled_copy:
        out_specs.append(pl.BlockSpec((tm * ROW_SPLIT, LANES), tok))
        out_shape.append(jax.ShapeDtypeStruct((n * ROW_SPLIT, LANES), F32))
    return pl.pallas_call(
        functools.partial(_combine_kernel, tm=tm, tiled_copy=tiled_copy),
        grid_spec=pltpu.PrefetchScalarGridSpec(
            num_scalar_prefetch=2,
            grid=(n // tm,),
            in_specs=[pl.BlockSpec(memory_space=pl.ANY),
                      pl.BlockSpec((tm, D_MODEL), tok),
                      pl.BlockSpec((tm, 1), tok), pl.BlockSpec((tm, 1), tok),
                      pl.BlockSpec((1, D_MODEL), fixed), pl.BlockSpec((1, D_MODEL), fixed)],
            out_specs=out_specs,
            scratch_shapes=[pltpu.VMEM((tm * ROW_SPLIT, LANES), F32)] * 4 + [pltpu.SemaphoreType.DMA((2,))]),
        out_shape=out_shape,
        compiler_params=_cparams("arbitrary"),
        name="moe_combine_ln",
    )(pos1, pos2, y, h, p1, p2, g, b)


def _plan_kernel(ints_ref, cnt_ref, pos_ref, meta_ref, *, layer, chunk):
    tr = MOE_ROW_TILE
    n = ints_ref.shape[1]
    ne = MOE_EXPERTS
    tiles = (cnt_ref[...] + (tr - 1.0)) * (1.0 / tr)
    tiles = tiles.astype(I32).astype(F32)
    lower = lax.broadcasted_iota(I32, (ne, ne), 0) >= lax.broadcasted_iota(I32, (ne, ne), 1)
    ends = _dot(jnp.where(lower, 1.0, 0.0).astype(BF16), tiles.astype(BF16))
    start_col = ((ends - tiles) * tr).astype(I32)[:, 0:1]
    r8 = lax.broadcasted_iota(I32, (8, chunk), 0)
    re = lax.broadcasted_iota(I32, (ne, chunk), 0)
    for c in range(n // chunk):
        blk = ints_ref[:, c * chunk:(c + 1) * chunk]
        s1 = jnp.sum(jnp.where(re == blk[0:1], start_col, 0), 0, keepdims=True)
        s2 = jnp.sum(jnp.where(re == blk[1:2], start_col, 0), 0, keepdims=True)
        pos_ref[:, c * chunk:(c + 1) * chunk] = jnp.where(
            r8 == 0, s1 + blk[2:3], jnp.where(r8 == 1, s2 + blk[3:4], 0))
    width = meta_ref.shape[1]
    tile_id = lax.broadcasted_iota(I32, (ne, width), 1).astype(F32)
    te = jnp.sum(jnp.where(ends[:, 0:1] <= tile_id, 1, 0), 0, keepdims=True)
    te = jnp.minimum(te, ne - 1)
    expert = lax.broadcasted_iota(I32, (ne, width), 0)
    later = jnp.logical_and(expert > te, tiles[:, 0:1] > 0.0)
    nxt = jnp.min(jnp.where(later, expert, ne), 0, keepdims=True)
    nxt = jnp.where(nxt < ne, nxt + layer * ne, -1)
    n_used = ends[ne - 1:ne, 0:1].astype(I32)
    rm = lax.broadcasted_iota(I32, (8, width), 0)
    meta_ref[...] = jnp.where(rm == 0, te + layer * ne,
                              jnp.where(rm == 1, n_used, jnp.where(rm == 2, nxt, 0)))


def _plan(ints, cnt, layer, n_tiles):
    n = ints.shape[1]
    width = -(-n_tiles // LANES) * LANES
    return pl.pallas_call(
        functools.partial(_plan_kernel, layer=layer, chunk=2048),
        out_shape=[jax.ShapeDtypeStruct((8, n), I32), jax.ShapeDtypeStruct((8, width), I32)],
        compiler_params=pltpu.CompilerParams(vmem_limit_bytes=VMEM_LIMIT),
        name="moe_plan",
    )(ints, cnt)


def _moe_layer(h, hp, layer, group_w, group_b, expert_w, expert_b, gate_w, up_w, down_w, ln_g, ln_b,
               tiled_copy):
    n = h.shape[0]
    ew = jnp.transpose(expert_w, (1, 0, 2)).reshape(D_MODEL, MOE_EXPERTS)
    w_r = jnp.zeros((D_MODEL, LANES), F32).at[:, 0:MOE_GROUPS].set(group_w).at[:, 8:8 + MOE_EXPERTS].set(ew)
    b_r = jnp.zeros((LANES,), F32).at[0:MOE_GROUPS].set(group_b).at[8:8 + MOE_EXPERTS].set(expert_b.reshape(-1))
    ints, flts, cnt = _router(h, w_r, b_r.reshape(LANES, 1), 512)
    tr = MOE_ROW_TILE
    n_tiles = (2 * n) // tr + MOE_EXPERTS
    pos, meta = _plan(ints, cnt, layer, n_tiles)
    pos1, pos2 = pos[0], pos[1]

    y = _experts(hp, pos1, pos2, meta[0, :n_tiles], meta[2, :n_tiles], meta[1, :1],
                 gate_w.reshape(-1, D_MODEL, MOE_D_FF),
                 up_w.reshape(-1, D_MODEL, MOE_D_FF),
                 down_w.reshape(-1, MOE_D_FF, D_MODEL))
    return _combine(y, h, pos1, pos2, flts[0].reshape(n, 1), flts[1].reshape(n, 1),
                    ln_g.reshape(1, -1), ln_b.reshape(1, -1), COMBINE_TILE, tiled_copy)


def _pad_heads(w, axis):
    parts = []
    h0 = 0
    for _, _, nh in ATTN_GROUPS:
        sl = [slice(None)] * w.ndim
        sl[axis] = slice(h0 * ATTN_HEAD_DIM, (h0 + nh) * ATTN_HEAD_DIM)
        part = w[tuple(sl)]
        pad = [(0, 0)] * w.ndim
        pad[axis] = (0, ATTN_SLAB - nh * ATTN_HEAD_DIM)
        parts.append(jnp.pad(part, pad))
        h0 += nh
    return parts


def _in_proj_kernel(x_ref, w_ref, wdt_ref, z_ref, xbc_ref, dt_ref, *, z_tiles):
    j = pl.program_id(1)
    acc = _dot(x_ref[...].astype(BF16), w_ref[...])

    @pl.when(j < z_tiles)
    def _():
        z_ref[...] = acc

    @pl.when(j >= z_tiles)
    def _():
        xbc_ref[...] = acc

    @pl.when(j == 0)
    def _():
        dt_ref[...] = _dot_f32ish(x_ref[...], wdt_ref[...])


def _in_proj(x, w_zx, w_dt, tm, tn):
    m, k = x.shape
    z_tiles = SSM_D_INNER // tn
    n_tiles = w_zx.shape[1] // tn
    return pl.pallas_call(
        functools.partial(_in_proj_kernel, z_tiles=z_tiles),
        grid=(m // tm, n_tiles),
        in_specs=[pl.BlockSpec((tm, k), lambda i, j: (i, 0)),
                  pl.BlockSpec((k, tn), lambda i, j: (0, j)),
                  pl.BlockSpec((k, LANES), lambda i, j: (0, 0))],
        out_specs=[pl.BlockSpec((tm, tn), lambda i, j: (i, jnp.minimum(j, z_tiles - 1))),
                   pl.BlockSpec((tm, tn), lambda i, j: (i, jnp.maximum(j - z_tiles, 0))),
                   pl.BlockSpec((tm, LANES), lambda i, j: (i, 0))],
        out_shape=[jax.ShapeDtypeStruct((m, SSM_D_INNER), F32),
                   jax.ShapeDtypeStruct((m, SSM_CONV_DIM), F32),
                   jax.ShapeDtypeStruct((m, LANES), F32)],
        compiler_params=_cparams("parallel", "arbitrary"),
        name="ssm_in_proj",
    )(x, w_zx, w_dt)


def _ssd_layer(h, in_w, conv_w, conv_b, dt_bias, a_log, d_skip, norm_w, out_w, ln_g, ln_b, bsz, seq):
    split = SSM_D_INNER + SSM_CONV_DIM
    dt_w = jnp.pad(in_w[:, split:], ((0, 0), (0, LANES - SSM_HEADS)))
    z, xbc, dt_raw = _in_proj(h, in_w[:, :split].astype(BF16), dt_w, 1024, 1024)
    pad32 = lambda v: jnp.pad(v, (0, LANES - SSM_HEADS)).reshape(1, LANES)
    y = _ssd(z, xbc, dt_raw, conv_w, conv_b.reshape(1, -1), pad32(dt_bias), pad32(a_log),
             jnp.repeat(d_skip, SSM_HEAD_DIM).reshape(1, -1), norm_w.reshape(1, -1), bsz, seq)
    return _matmul_res_ln(y, out_w.astype(BF16), h, ln_g.reshape(1, -1), ln_b.reshape(1, -1), 512)


def _qkv_group_weights(kv_w, q_w):
    width = ATTN_HEADS * ATTN_HEAD_DIM
    w_q = _pad_heads(q_w * (ATTN_HEAD_DIM ** -0.5), 1)
    w_k = _pad_heads(kv_w[:, :width], 1)
    w_v = _pad_heads(kv_w[:, width:], 1)
    return [jnp.concatenate([w_q[gi], w_k[gi], w_v[gi]], axis=1).astype(BF16) for gi in range(len(ATTN_GROUPS))]


def _attn_layer(h, h8, qkv_w, o_w, rel_bias, ln_g, ln_b, bsz, seq):
    outs, lses = [], []
    h0 = 0
    for gi, (_, dil, nh) in enumerate(ATTN_GROUPS):
        if dil == 1:
            qkv = _matmul(h, qkv_w[gi], BF16, 1024, 3 * ATTN_SLAB)
        else:
            qkv = _qkv_dilated(h8, qkv_w[gi], dil, max(1024, ATTN_BLOCK * dil))
        bias_p, bias_c = _group_bias(rel_bias, h0, nh, dil)
        o, lse = _attention_group(qkv, bias_p, bias_c, gi, dil, nh, bsz, seq)
        outs.extend(o)
        lses.append(lse)
        h0 += nh
    w_o = jnp.concatenate(_pad_heads(o_w, 0), axis=0).astype(BF16)
    return _attn_out(outs, lses, w_o, h, ln_g.reshape(1, -1), ln_b.reshape(1, -1), 512)


def kernel(x, ssm_in_w, ssm_conv_w, ssm_conv_b, ssm_dt_bias, ssm_a_log, ssm_d, ssm_norm_w, ssm_out_w,
           kv_w, attn_q_w, attn_o_w, rel_bias, moe_group_w, moe_group_b, moe_expert_w, moe_expert_b,
           moe_gate_w, moe_up_w, moe_down_w, ln_g, ln_b):
    bsz, seq, d = x.shape
    h = x.reshape(bsz * seq, d)
    n_ssd = DEPTH // 2
    h8 = None
    for i in range(DEPTH):
        if i < n_ssd:
            h, hp = _ssd_layer(h, ssm_in_w[i], ssm_conv_w[i], ssm_conv_b[i], ssm_dt_bias[i], ssm_a_log[i],
                               ssm_d[i], ssm_norm_w[i], ssm_out_w[i], ln_g[i, 0], ln_b[i, 0], bsz, seq)
        else:
            j = i - n_ssd
            h, hp = _attn_layer(h, h8, _qkv_group_weights(kv_w, attn_q_w[j]), attn_o_w[j], rel_bias,
                                ln_g[i, 0], ln_b[i, 0], bsz, seq)
        feeds_attention = n_ssd <= i + 1 < DEPTH
        res = _moe_layer(h, hp, i, moe_group_w[i], moe_group_b[i], moe_expert_w[i], moe_expert_b[i],
                         moe_gate_w, moe_up_w, moe_down_w, ln_g[i, 1], ln_b[i, 1], feeds_attention)
        h = res[0]
        h8 = res[1] if feeds_attention else None
    return h.reshape(bsz, seq, d)
```

```python
import functools
import math

import numpy as np
import jax
import jax.numpy as jnp
from jax import lax
from jax.experimental import pallas as pl
from jax.experimental.pallas import tpu as pltpu

F32 = jnp.float32
BF16 = jnp.bfloat16
I32 = jnp.int32

D_MODEL = 1024
DEPTH = 2
DN_ALPHA = (2 * DEPTH) ** 0.25
LN_EPS = 1e-5
LOG2_E = math.log2(math.e)

SSM_D_INNER = 2048
SSM_HEAD_DIM = 64
SSM_HEADS = 32
SSM_GROUPS = 4
SSM_STATE = 128
SSM_CONV = 4
SSM_CHUNK = 128
SSM_CONV_DIM = SSM_D_INNER + 2 * SSM_GROUPS * SSM_STATE

ATTN_HEAD_DIM = 64
ATTN_GROUPS = ((128, 1, 6), (512, 4, 5), (2048, 16, 5))
ATTN_HEADS = 16
ATTN_BLOCK = 128
N_BUCKETS = 32
MAX_DISTANCE = 2048
ATTN_SLAB = 384
NEG_BIG = -1e30

MOE_GROUPS = 4
MOE_EPG = 8
MOE_EXPERTS = MOE_GROUPS * MOE_EPG
MOE_D_FF = 512
MOE_ROW_TILE = 256
COMBINE_TILE = 512

LANES = 128
DMA_QUEUES = 2
VMEM_LIMIT = 48 * 1024 * 1024
EXPERTS_VMEM_LIMIT = 56 * 1024 * 1024


def _cparams(*sem):
    return pltpu.CompilerParams(dimension_semantics=sem, vmem_limit_bytes=VMEM_LIMIT)


def _layer_norm(x, g, b):
    mu = jnp.mean(x, -1, keepdims=True)
    xc = x - mu
    var = jnp.mean(xc * xc, -1, keepdims=True)
    return xc * lax.rsqrt(var + LN_EPS) * g + b


def _split2(x):
    hi = x.astype(BF16)
    lo = (x - hi.astype(F32)).astype(BF16)
    return hi, lo


def _dot(a, b):
    return jnp.dot(a, b, preferred_element_type=F32)


def _dot_f32ish(a, b):
    ah, al = _split2(a)
    bh, bl = _split2(b)
    return _dot(ah, bh) + _dot(al, bh) + _dot(ah, bl)


def _mm_kernel(a_ref, b_ref, o_ref):
    o_ref[...] = _dot(a_ref[...].astype(BF16), b_ref[...]).astype(o_ref.dtype)


def _matmul(a, b, out_dtype, tm, tn):
    m, k = a.shape
    nc = b.shape[1]
    return pl.pallas_call(
        _mm_kernel,
        grid=(m // tm, nc // tn),
        in_specs=[pl.BlockSpec((tm, k), lambda i, j: (i, 0)),
                  pl.BlockSpec((k, tn), lambda i, j: (0, j))],
        out_specs=pl.BlockSpec((tm, tn), lambda i, j: (i, j)),
        out_shape=jax.ShapeDtypeStruct((m, nc), out_dtype),
        compiler_params=_cparams("parallel", "parallel"),
        name="matmul",
    )(a, b)


def _mm3_kernel(a_ref, b_ref, o_ref):
    o_ref[...] = _dot_f32ish(a_ref[...], b_ref[...])


def _matmul_f32ish(a, b, tm):
    m, k = a.shape
    nc = b.shape[1]
    return pl.pallas_call(
        _mm3_kernel,
        grid=(m // tm,),
        in_specs=[pl.BlockSpec((tm, k), lambda i: (i, 0)),
                  pl.BlockSpec((k, nc), lambda i: (0, 0))],
        out_specs=pl.BlockSpec((tm, nc), lambda i: (i, 0)),
        out_shape=jax.ShapeDtypeStruct((m, nc), F32),
        compiler_params=_cparams("parallel"),
        name="matmul_f32ish",
    )(a, b)


ROW_SPLIT = D_MODEL // LANES


def _store_row_tiled(ref, val):
    rows = val.shape[0]
    for c in range(ROW_SPLIT):
        ref[pl.ds(c, rows, stride=ROW_SPLIT), :] = val[:, c * LANES:(c + 1) * LANES]


def _load_row_tiled(ref):
    rows = ref.shape[0] // ROW_SPLIT
    return jnp.concatenate([ref[pl.ds(c, rows, stride=ROW_SPLIT), :] for c in range(ROW_SPLIT)], axis=1)


HALF = D_MODEL // 2
U32 = jnp.uint32
HI16 = 0xFFFF0000


def _pack_bf16_pairs(x):
    bits = lax.bitcast_convert_type(x.astype(BF16).astype(F32), U32)
    return (bits[:, :HALF] >> 16) | (bits[:, HALF:] & U32(HI16))


def _unpack_bf16_pairs(w):
    lo = lax.bitcast_convert_type(w << 16, F32).astype(BF16)
    hi = lax.bitcast_convert_type(w & U32(HI16), F32).astype(BF16)
    return lo, hi


def _mm_ln_kernel(a_ref, w_ref, h_ref, g_ref, b_ref, o_ref, op_ref):
    acc = _dot(a_ref[...], w_ref[...])
    out = _layer_norm(DN_ALPHA * h_ref[...] + acc, g_ref[...], b_ref[...])
    o_ref[...] = out
    op_ref[...] = _pack_bf16_pairs(out)


def _matmul_res_ln(a, w, h, g, b, tm):
    m, k = a.shape
    d = w.shape[1]
    return pl.pallas_call(
        _mm_ln_kernel,
        grid=(m // tm,),
        in_specs=[pl.BlockSpec((tm, k), lambda i: (i, 0)),
                  pl.BlockSpec((k, d), lambda i: (0, 0)),
                  pl.BlockSpec((tm, d), lambda i: (i, 0)),
                  pl.BlockSpec((1, d), lambda i: (0, 0)),
                  pl.BlockSpec((1, d), lambda i: (0, 0))],
        out_specs=[pl.BlockSpec((tm, d), lambda i: (i, 0)),
                   pl.BlockSpec((tm, HALF), lambda i: (i, 0))],
        out_shape=[jax.ShapeDtypeStruct((m, d), F32),
                   jax.ShapeDtypeStruct((m, HALF), U32)],
        compiler_params=_cparams("parallel"),
        name="matmul_res_ln",
    )(a, w, h, g, b)


SSD_CHUNKS_PER_STEP = 2


def _ssd_kernel(z_ref, xbc_ref, dt_ref, cw_ref, cb_ref, dtb_ref, alog_ref, dsk_ref, nw_ref,
                y_ref, xe_ref, st_ref):
    @pl.when(pl.program_id(1) == 0)
    def _():
        xe_ref[:, 0:8, :] = jnp.zeros((xe_ref.shape[0], 8, LANES), F32)
        st_ref[...] = jnp.zeros_like(st_ref)

    for sub in range(SSD_CHUNKS_PER_STEP):
        _ssd_chunk(pl.ds(sub * SSM_CHUNK, SSM_CHUNK), z_ref, xbc_ref, dt_ref, cw_ref, cb_ref, dtb_ref,
                   alog_ref, dsk_ref, nw_ref, y_ref, xe_ref, st_ref)


def _ssd_chunk(rows, z_ref, xbc_ref, dt_ref, cw_ref, cb_ref, dtb_ref, alog_ref, dsk_ref, nw_ref,
               y_ref, xe_ref, st_ref):
    q = SSM_CHUNK
    w = cw_ref[...]
    bias = cb_ref[...]
    act = []
    for c in range(SSM_CONV_DIM // LANES):
        cols = slice(c * LANES, (c + 1) * LANES)
        u = xbc_ref[rows, cols]
        xe_ref[c, 8:8 + q, :] = u
        conv = (bias[:, cols] + w[3:4, cols] * u + w[2:3, cols] * xe_ref[c, 7:7 + q, :]
                + w[1:2, cols] * xe_ref[c, 6:6 + q, :] + w[0:1, cols] * xe_ref[c, 5:5 + q, :])
        xe_ref[c, 0:8, :] = xe_ref[c, q:q + 8, :]
        act.append(conv * jax.nn.sigmoid(conv))

    pre = dt_ref[rows, :] + dtb_ref[...]
    dt = jnp.maximum(pre, 0.0) + jnp.log(1.0 + jnp.exp(-jnp.abs(pre)))
    adt = dt * (-jnp.exp(alog_ref[...]) * LOG2_E)

    row = lax.broadcasted_iota(I32, (q, q), 0)
    col = lax.broadcasted_iota(I32, (q, q), 1)
    tril = row >= col
    tri_b = jnp.where(tril, 1.0, 0.0).astype(BF16)
    a_hi = adt.astype(BF16)
    r1 = adt - a_hi.astype(F32)
    a_mid = r1.astype(BF16)
    a_lo = (r1 - a_mid.astype(F32)).astype(BF16)
    acs = _dot(tri_b, a_hi) + _dot(tri_b, a_mid) + _dot(tri_b, a_lo)
    acs_t = acs.T
    acs_dt_t = acs_t - jnp.log(dt.T) * LOG2_E
    eacs = jnp.exp2(acs)
    left = col < SSM_HEAD_DIM

    for g in range(SSM_GROUPS):
        b0 = SSM_D_INNER + g * SSM_STATE
        c0 = SSM_D_INNER + SSM_GROUPS * SSM_STATE + g * SSM_STATE
        bm = act[b0 // LANES]
        cm = act[c0 // LANES].astype(BF16)
        cb = lax.dot_general(cm, bm.astype(BF16), (((1,), (1,)), ((), ())),
                             preferred_element_type=F32)
        bm_t = bm.T
        gs = g * 512
        y_off = _dot(cm, st_ref[:, gs:gs + 512].astype(BF16))
        slabs = []
        for pr in range(4):
            ha = g * 8 + pr * 2
            hb = ha + 1
            cs = gs + pr * LANES
            x2 = act[cs // LANES]
            x2b = x2.astype(BF16)
            ys, ups = [], []
            for h in (ha, hb):
                a_col = acs[:, h:h + 1]
                a_src = acs_dt_t[h:h + 1, :]
                decay = jnp.where(tril, jnp.exp2(a_col - a_src), 0.0)
                ys.append(_dot((cb * decay).astype(BF16), x2b))
                to_end = jnp.exp2(acs_t[h:h + 1, q - 1:q] - a_src)
                ups.append(_dot((bm_t * to_end).astype(BF16), x2b))
            y_diag = jnp.where(left, ys[0], ys[1])
            upd = jnp.where(left, ups[0], ups[1])
            e2 = jnp.where(left, eacs[:, ha:ha + 1], eacs[:, hb:hb + 1])
            cd = jnp.where(left[0:1, :], eacs[q - 1:q, ha:ha + 1], eacs[q - 1:q, hb:hb + 1])
            y2 = y_diag + y_off[:, pr * LANES:(pr + 1) * LANES] * e2 + dsk_ref[:, cs:cs + LANES] * x2
            st_ref[:, cs:cs + LANES] = st_ref[:, cs:cs + LANES] * cd + upd
            slabs.append(y2)
        yg = jnp.concatenate(slabs, axis=1)
        zg = z_ref[rows, gs:gs + 512]
        yg = yg * (zg * jax.nn.sigmoid(zg))
        ms = jnp.mean(yg * yg, -1, keepdims=True)
        y_ref[rows, gs:gs + 512] = (yg * lax.rsqrt(ms + LN_EPS) * nw_ref[:, gs:gs + 512]).astype(y_ref.dtype)


def _ssd(z, xbc, dt_raw, conv_w, conv_b, dt_bias, a_log, d_rep, norm_w, bsz, seq):
    n = z.shape[0]
    q = SSM_CHUNK * SSD_CHUNKS_PER_STEP
    nchunk = seq // q
    tok = lambda b, c: (b * nchunk + c, 0)
    fixed = lambda b, c: (0, 0)
    return pl.pallas_call(
        _ssd_kernel,
        grid=(bsz, nchunk),
        in_specs=[pl.BlockSpec((q, SSM_D_INNER), tok),
                  pl.BlockSpec((q, SSM_CONV_DIM), tok),
                  pl.BlockSpec((q, LANES), tok),
                  pl.BlockSpec((SSM_CONV, SSM_CONV_DIM), fixed),
                  pl.BlockSpec((1, SSM_CONV_DIM), fixed),
                  pl.BlockSpec((1, LANES), fixed),
                  pl.BlockSpec((1, LANES), fixed),
                  pl.BlockSpec((1, SSM_D_INNER), fixed),
                  pl.BlockSpec((1, SSM_D_INNER), fixed)],
        out_specs=pl.BlockSpec((q, SSM_D_INNER), tok),
        out_shape=jax.ShapeDtypeStruct((n, SSM_D_INNER), BF16),
        scratch_shapes=[pltpu.VMEM((SSM_CONV_DIM // LANES, SSM_CHUNK + 8, LANES), F32),
                        pltpu.VMEM((SSM_STATE, SSM_D_INNER), F32)],
        compiler_params=_cparams("parallel", "arbitrary"),
        name="ssd_chunk",
    )(z, xbc, dt_raw, conv_w, conv_b, dt_bias, a_log, d_rep, norm_w)


def _residue_major_pieces(h8_ref, n_tokens, dil):
    span = ATTN_BLOCK * dil
    pieces = []
    for blk in range(n_tokens // span):
        for r in range(dil):
            first = (blk * span + r) * ROW_SPLIT
            pieces.append(jnp.concatenate(
                [h8_ref[pl.ds(first + c, ATTN_BLOCK, stride=ROW_SPLIT * dil), :] for c in range(ROW_SPLIT)],
                axis=1).astype(BF16))
    return pieces


def _qkv_dilated_kernel(h8_ref, w_ref, o_ref, *, dil):
    per_dot = 4
    rows = _residue_major_pieces(h8_ref, o_ref.shape[0], dil)
    for k in range(0, len(rows), per_dot):
        x = jnp.concatenate(rows[k:k + per_dot], axis=0)
        o_ref[k * ATTN_BLOCK:(k + per_dot) * ATTN_BLOCK, :] = _dot(x, w_ref[...]).astype(o_ref.dtype)


def _qkv_dilated(h8, w, dil, tm):
    n = h8.shape[0] // ROW_SPLIT
    nc = w.shape[1]
    return pl.pallas_call(
        functools.partial(_qkv_dilated_kernel, dil=dil),
        grid=(n // tm,),
        in_specs=[pl.BlockSpec((tm * ROW_SPLIT, LANES), lambda i: (i, 0)),
                  pl.BlockSpec((D_MODEL, nc), lambda i: (0, 0))],
        out_specs=pl.BlockSpec((tm, nc), lambda i: (i, 0)),
        out_shape=jax.ShapeDtypeStruct((n, nc), BF16),
        compiler_params=_cparams("parallel"),
        name=f"qkv_dil{dil}",
    )(h8, w)


def _attn_kernel(*refs, nh, dil, has_prev, blocks):
    span = ATTN_BLOCK * dil
    if has_prev:
        q_ref, kp_ref, kc_ref, vp_ref, vc_ref, bp_ref, bc_ref = refs[:7]
        out_refs = refs[7:]
        first_pen = jnp.where(pl.program_id(1) > 0, 0.0, NEG_BIG)
    else:
        q_ref, kc_ref, vc_ref, bc_ref = refs[:4]
        out_refs = refs[4:]
    s_scr, p_scr, max_scr, den_scr = out_refs[-4:]
    o_refs, lse_ref = out_refs[:-5], out_refs[-5]
    ones = jnp.ones((s_scr.shape[2], LANES), BF16)
    lane = lax.broadcasted_iota(I32, (ATTN_BLOCK, LANES), 1)
    left = lane < ATTN_HEAD_DIM
    zero = jnp.zeros((), BF16)
    nt = (((1,), (1,)), ((), ()))

    def residue(blk, r):
        rows = pl.ds(pl.multiple_of(blk * span + r * ATTN_BLOCK, ATTN_BLOCK), ATTN_BLOCK)
        dst = pl.ds(blk * span + r, ATTN_BLOCK, stride=dil)
        if has_prev and blk == 0:
            kprev, vprev, pen = kp_ref, vp_ref, first_pen
            prows = pl.ds(pl.multiple_of(r * ATTN_BLOCK, ATTN_BLOCK), ATTN_BLOCK)
        elif has_prev:
            kprev, vprev, pen = kc_ref, vc_ref, 0.0
            prows = pl.ds(pl.multiple_of((blk - 1) * span + r * ATTN_BLOCK, ATTN_BLOCK), ATTN_BLOCK)
        for hh in range(nh):
            cols = pl.ds(hh // 2 * LANES, LANES)
            qm = jnp.where(left if hh % 2 == 0 else ~left, q_ref[rows, cols], zero)
            s_c = lax.dot_general(qm, kc_ref[rows, cols], nt, preferred_element_type=F32) + bc_ref[hh]
            if has_prev:
                s_p = lax.dot_general(qm, kprev[prows, cols], nt, preferred_element_type=F32)
                s_scr[hh, :, 0:ATTN_BLOCK] = s_p + (bp_ref[hh] + pen)
                s_scr[hh, :, ATTN_BLOCK:] = s_c
            else:
                s_scr[hh] = s_c
        for hh in range(nh):
            s = s_scr[hh]
            m = jnp.max(s, -1, keepdims=True)
            p = jnp.exp(s - m)
            p_scr[hh] = p.astype(BF16)
            max_scr[hh] = m
            if not has_prev:
                den_scr[hh] = jnp.sum(p, -1, keepdims=True)
        lse_sum = jnp.zeros((ATTN_BLOCK, 1), F32)
        for pr in range(ATTN_SLAB // LANES):
            cols = pl.ds(pr * LANES, LANES)
            halves = []
            for hh in (pr * 2, pr * 2 + 1):
                if hh >= nh:
                    halves.append(jnp.zeros((ATTN_BLOCK, LANES), F32))
                    continue
                if has_prev:
                    den = _dot(p_scr[hh], ones)
                    den1 = den[:, 0:1]
                    o = (_dot(p_scr[hh, :, 0:ATTN_BLOCK], vprev[prows, cols])
                         + _dot(p_scr[hh, :, ATTN_BLOCK:], vc_ref[rows, cols]))
                else:
                    den = den1 = den_scr[hh]
                    o = _dot(p_scr[hh], vc_ref[rows, cols])
                halves.append(o * (1.0 / den))
                lse_sum = lse_sum + (max_scr[hh] + jnp.log(den1))
            o_refs[pr][dst, :] = jnp.where(left, halves[0], halves[1])
        lse_ref[dst, :] = jnp.broadcast_to(lse_sum * (1.0 / nh), (ATTN_BLOCK, LANES))

    for blk in range(blocks):
        def body(r, carry, blk=blk):
            residue(blk, r)
            return carry

        lax.fori_loop(0, dil, body, 0)


def _attention_group(qkv, bias_p, bias_c, gi, dil, nh, bsz, seq):
    span = ATTN_BLOCK * dil
    nb = seq // span
    has_prev = nb > 1
    blocks = 2 if span == ATTN_BLOCK else 1
    steps = nb // blocks
    blk = (blocks * span, ATTN_SLAB)
    cur = lambda which: (lambda b, n: (b * steps + n, which))
    prev = lambda which: (lambda b, n: (b * nb + jnp.maximum(n * blocks - 1, 0), which))
    fixed = lambda b, n: (0, 0, 0)
    tab = pl.BlockSpec((nh, ATTN_BLOCK, ATTN_BLOCK), fixed)
    if has_prev:
        pblk = (span, ATTN_SLAB)
        in_specs = [pl.BlockSpec(blk, cur(0)), pl.BlockSpec(pblk, prev(1)), pl.BlockSpec(blk, cur(1)),
                    pl.BlockSpec(pblk, prev(2)), pl.BlockSpec(blk, cur(2)), tab, tab]
        args = (qkv, qkv, qkv, qkv, qkv, bias_p, bias_c)
    else:
        in_specs = [pl.BlockSpec(blk, cur(0)), pl.BlockSpec(blk, cur(1)), pl.BlockSpec(blk, cur(2)), tab]
        args = (qkv, qkv, qkv, bias_c)
    n_out = ATTN_SLAB // LANES + 1
    keys = 2 * ATTN_BLOCK if has_prev else ATTN_BLOCK
    outs = pl.pallas_call(
        functools.partial(_attn_kernel, nh=nh, dil=dil, has_prev=has_prev, blocks=blocks),
        grid=(bsz, steps),
        in_specs=in_specs,
        out_specs=[pl.BlockSpec((blocks * span, LANES), lambda b, n: (b * steps + n, 0))] * n_out,
        out_shape=[jax.ShapeDtypeStruct((bsz * seq, LANES), F32)] * n_out,
        scratch_shapes=[pltpu.VMEM((nh, ATTN_BLOCK, keys), F32),
                        pltpu.VMEM((nh, ATTN_BLOCK, keys), BF16),
                        pltpu.VMEM((nh, ATTN_BLOCK, 1), F32),
                        pltpu.VMEM((nh, ATTN_BLOCK, 1), F32)],
        compiler_params=_cparams("parallel", "parallel"),
        name=f"dilated_attn_g{gi}",
    )(*args)
    return outs[:-1], outs[-1]


def _attn_out_kernel(*refs):
    n_pairs = ATTN_SLAB // LANES
    n_groups = len(ATTN_GROUPS)
    o_refs = refs[:n_groups * n_pairs]
    l_refs = refs[n_groups * n_pairs:n_groups * (n_pairs + 1)]
    w_ref, h_ref, g_ref, b_ref, out_ref, outp_ref = refs[n_groups * (n_pairs + 1):]
    ls = [r[:, 0:1] for r in l_refs]
    m = jnp.maximum(jnp.maximum(ls[0], ls[1]), ls[2])
    es = [jnp.exp(v - m) for v in ls]
    scale = n_groups / (es[0] + es[1] + es[2])
    slabs = []
    for gi in range(n_groups):
        wt = es[gi] * scale
        slabs.extend((r[...] * wt).astype(BF16) for r in o_refs[gi * n_pairs:(gi + 1) * n_pairs])
    acc = _dot(jnp.concatenate(slabs, axis=1), w_ref[...])
    out = _layer_norm(DN_ALPHA * h_ref[...] + acc, g_ref[...], b_ref[...])
    out_ref[...] = out
    outp_ref[...] = _pack_bf16_pairs(out)


def _attn_out(outs, lses, w, h, g, b, tm):
    n = h.shape[0]
    tok = lambda i: (i, 0)
    fixed2 = lambda i: (0, 0)
    return pl.pallas_call(
        _attn_out_kernel,
        grid=(n // tm,),
        in_specs=[pl.BlockSpec((tm, LANES), tok)] * (len(outs) + len(lses))
                 + [pl.BlockSpec((len(ATTN_GROUPS) * ATTN_SLAB, D_MODEL), lambda i: (0, 0)),
                    pl.BlockSpec((tm, D_MODEL), tok),
                    pl.BlockSpec((1, D_MODEL), fixed2), pl.BlockSpec((1, D_MODEL), fixed2)],
        out_specs=[pl.BlockSpec((tm, D_MODEL), tok), pl.BlockSpec((tm, HALF), tok)],
        out_shape=[jax.ShapeDtypeStruct((n, D_MODEL), F32),
                   jax.ShapeDtypeStruct((n, HALF), U32)],
        compiler_params=_cparams("parallel"),
        name="attn_out_ln",
    )(*outs, *lses, w, h, g, b)


def _t5_bucket(dist):
    max_exact = N_BUCKETS // 2
    n = np.maximum(dist, 1).astype(np.float64)
    large = max_exact + (np.log(n / max_exact) / np.log(MAX_DISTANCE / max_exact)
                         * (N_BUCKETS - max_exact)).astype(np.int32)
    large = np.minimum(large, N_BUCKETS - 1)
    return np.where(dist < max_exact, dist, large).astype(np.int32)


def _group_bias(rel_bias, h0, nh, dil):
    qi = np.arange(ATTN_BLOCK)[:, None]
    ki = np.arange(ATTN_BLOCK)[None, :]
    tabs = []
    for delta, band in ((qi + ATTN_BLOCK - ki, ki >= qi), (qi - ki, ki <= qi)):
        bucket = _t5_bucket(np.clip(delta, 0, None) * dil)
        onehot = (bucket[..., None] == np.arange(N_BUCKETS)).astype(np.float32)
        t = jnp.einsum("qkb,bh->hqk", onehot, rel_bias[:, h0:h0 + nh], precision=lax.Precision.HIGHEST)
        tabs.append(jnp.where(band[None], t, NEG_BIG).astype(F32))
    return tabs


def _router_kernel(h_ref, w_ref, b_ref, ints_ref, flts_ref, cnt_ref, carry_ref, before_ref, *, tm):
    @pl.when(pl.program_id(0) == 0)
    def _():
        carry_ref[...] = jnp.zeros_like(carry_ref)
        tr = lax.broadcasted_iota(I32, (tm, tm), 0)
        tc = lax.broadcasted_iota(I32, (tm, tm), 1)
        before_ref[...] = jnp.where(tr < tc, 1.0, 0.0).astype(BF16)

    lt = _dot_f32ish(h_ref[...], w_ref[...]).T + b_ref[...]
    gl = lt[0:MOE_GROUPS]
    r4 = lax.broadcasted_iota(I32, (MOE_GROUPS, tm), 0)
    gmax = jnp.max(gl, 0, keepdims=True)
    gidx = jnp.min(jnp.where(gl == gmax, r4, MOE_GROUPS), 0, keepdims=True)
    gval = 1.0 / jnp.sum(jnp.exp(gl - gmax), 0, keepdims=True)

    esel = jnp.zeros((MOE_EPG, tm), F32)
    for g in range(MOE_GROUPS):
        esel = jnp.where(gidx == g, lt[8 + g * MOE_EPG:8 + (g + 1) * MOE_EPG], esel)
    r8 = lax.broadcasted_iota(I32, (MOE_EPG, tm), 0)
    v1 = jnp.max(esel, 0, keepdims=True)
    i1 = jnp.min(jnp.where(esel == v1, r8, MOE_EPG), 0, keepdims=True)
    rest = jnp.where(r8 == i1, -jnp.inf, esel)
    v2 = jnp.max(rest, 0, keepdims=True)
    i2 = jnp.min(jnp.where(rest == v2, r8, MOE_EPG), 0, keepdims=True)
    t = jnp.exp(v2 - v1)
    p1 = gval / (1.0 + t)
    p2 = p1 * t
    e1 = gidx * MOE_EPG + i1
    e2 = gidx * MOE_EPG + i2

    r32 = lax.broadcasted_iota(I32, (MOE_EXPERTS, tm), 0)
    oh1 = r32 == e1
    oh2 = r32 == e2
    oh = jnp.where(oh1 | oh2, 1.0, 0.0)
    base = _dot(oh.astype(BF16), before_ref[...]) + carry_ref[:, 0:1]
    rank1 = jnp.sum(jnp.where(oh1, base, 0.0), 0, keepdims=True).astype(I32)
    rank2 = jnp.sum(jnp.where(oh2, base, 0.0), 0, keepdims=True).astype(I32)
    carry_ref[...] = carry_ref[...] + jnp.sum(oh, 1, keepdims=True)
    cnt_ref[...] = carry_ref[...]

    ints_ref[...] = jnp.where(r8 == 0, e1, jnp.where(r8 == 1, e2, jnp.where(r8 == 2, rank1,
                              jnp.where(r8 == 3, rank2, 0))))
    flts_ref[...] = jnp.where(r8 == 0, p1, jnp.where(r8 == 1, p2, 0.0))


def _router(h, w_r, b_r, tm):
    n = h.shape[0]
    return pl.pallas_call(
        functools.partial(_router_kernel, tm=tm),
        grid=(n // tm,),
        in_specs=[pl.BlockSpec((tm, D_MODEL), lambda i: (i, 0)),
                  pl.BlockSpec((D_MODEL, LANES), lambda i: (0, 0)),
                  pl.BlockSpec((LANES, 1), lambda i: (0, 0))],
        out_specs=[pl.BlockSpec((8, tm), lambda i: (0, i)),
                   pl.BlockSpec((8, tm), lambda i: (0, i)),
                   pl.BlockSpec((MOE_EXPERTS, LANES), lambda i: (0, 0))],
        out_shape=[jax.ShapeDtypeStruct((8, n), I32),
                   jax.ShapeDtypeStruct((8, n), F32),
                   jax.ShapeDtypeStruct((MOE_EXPERTS, LANES), F32)],
        scratch_shapes=[pltpu.VMEM((MOE_EXPERTS, LANES), F32), pltpu.VMEM((tm, tm), BF16)],
        compiler_params=_cparams("arbitrary"),
        name="moe_router",
    )(h, w_r, b_r)


def _row_copy(src_ref, s, dst_ref, d, sem):
    def first(r):
        return r * ROW_SPLIT if isinstance(r, int) else pl.multiple_of(r * ROW_SPLIT, ROW_SPLIT)

    return pltpu.make_async_copy(src_ref.at[pl.ds(first(s), ROW_SPLIT), :],
                                 dst_ref.at[pl.ds(first(d), ROW_SPLIT), :], sem)


def _tile_wait(src_ref, dst_ref, sem):
    pltpu.make_async_copy(src_ref.at[pl.ds(0, dst_ref.shape[0]), :], dst_ref, sem).wait()


def _experts_kernel(pos1_ref, pos2_ref, te_ref, nxt_ref, na_ref, hp_hbm, zeros_ref, wg_ref, wu_ref, wd_ref,
                    y_ref, hp_ref, xa, xb, wg_f, wu_f, wd_f, wg_s, wu_s, wd_s, row_tok, slot_ref,
                    sem, hsem, wsem):
    i = pl.program_id(0)
    n_tiles = pl.num_programs(0)
    tr = MOE_ROW_TILE
    active = i < na_ref[0]
    changed = jnp.logical_or(i == 0, te_ref[i] != te_ref[jnp.maximum(i - 1, 0)])

    def weight_copies(expert, slot):
        return [pltpu.make_async_copy(src.at[expert], dst.at[slot], wsem.at[slot])
                for src, dst in ((wg_ref, wg_f), (wu_ref, wu_f), (wd_ref, wd_f))]

    @pl.when(i == 0)
    def _():
        slot_ref[0] = 0
        rows_in = pltpu.make_async_copy(hp_hbm, hp_ref, hsem)
        rows_in.start()
        for c in weight_copies(te_ref[0], 0):
            c.start()
        clear = pltpu.make_async_copy(zeros_ref, row_tok, sem)
        clear.start()
        clear.wait()

        def invert(t, carry):
            row_tok[pos1_ref[t]] = t
            row_tok[pos2_ref[t]] = t
            return carry

        lax.fori_loop(0, pos1_ref.shape[0], invert, 0, unroll=16)
        rows_in.wait()

        def pick(r, carry):
            xa[pl.ds(r, 1), :] = hp_ref[pl.ds(row_tok[r], 1), :]
            return carry

        lax.fori_loop(0, tr, pick, 0, unroll=8)

    @pl.when(jnp.logical_and(active, changed))
    def _():
        slot = slot_ref[0]
        for c in weight_copies(te_ref[i], slot):
            c.wait()
        following = nxt_ref[i]

        @pl.when(following >= 0)
        def _():
            for c in weight_copies(following, 1 - slot):
                c.start()

        wg_s[...] = wg_f[slot].astype(BF16)
        wu_s[...] = wu_f[slot].astype(BF16)
        wd_s[...] = wd_f[slot].astype(BF16)
        slot_ref[0] = 1 - slot

    def step(cur, nxt):
        base = jnp.minimum(i + 1, n_tiles - 1) * tr
        for r in range(tr):
            nxt[r:r + 1, :] = hp_ref[pl.ds(row_tok[base + r], 1), :]
        lo, hi = _unpack_bf16_pairs(cur[...])
        gate = _dot(lo, wg_s[0:HALF, :]) + _dot(hi, wg_s[HALF:, :])
        up = _dot(lo, wu_s[0:HALF, :]) + _dot(hi, wu_s[HALF:, :])
        hid = (gate * jax.nn.sigmoid(gate) * up).astype(BF16)
        _store_row_tiled(y_ref, _dot(hid, wd_s[...]))

    even = i % 2 == 0

    @pl.when(jnp.logical_and(active, even))
    def _():
        step(xa, xb)

    @pl.when(jnp.logical_and(active, jnp.logical_not(even)))
    def _():
        step(xb, xa)

    @pl.when(jnp.logical_not(active))
    def _():
        y_ref[...] = jnp.zeros_like(y_ref)


def _experts(hp, pos1, pos2, tile_expert, next_expert, n_active, wg, wu, wd):
    tr = MOE_ROW_TILE
    n_tiles = tile_expert.shape[0]
    rows = n_tiles * tr
    hbm = pl.BlockSpec(memory_space=pl.ANY)
    return pl.pallas_call(
        _experts_kernel,
        grid_spec=pltpu.PrefetchScalarGridSpec(
            num_scalar_prefetch=5,
            grid=(n_tiles,),
            in_specs=[hbm, hbm, hbm, hbm, hbm],
            out_specs=pl.BlockSpec((tr * ROW_SPLIT, LANES), lambda i, *_: (i, 0)),
            scratch_shapes=[pltpu.VMEM(hp.shape, U32),
                            pltpu.VMEM((tr, HALF), U32),
                            pltpu.VMEM((tr, HALF), U32),
                            pltpu.VMEM((2, D_MODEL, MOE_D_FF), F32),
                            pltpu.VMEM((2, D_MODEL, MOE_D_FF), F32),
                            pltpu.VMEM((2, MOE_D_FF, D_MODEL), F32),
                            pltpu.VMEM((D_MODEL, MOE_D_FF), BF16),
                            pltpu.VMEM((D_MODEL, MOE_D_FF), BF16),
                            pltpu.VMEM((MOE_D_FF, D_MODEL), BF16),
                            pltpu.SMEM((rows,), I32),
                            pltpu.SMEM((1,), I32),
                            pltpu.SemaphoreType.DMA(()),
                            pltpu.SemaphoreType.DMA(()),
                            pltpu.SemaphoreType.DMA((2,))]),
        out_shape=jax.ShapeDtypeStruct((rows * ROW_SPLIT, LANES), F32),
        compiler_params=pltpu.CompilerParams(dimension_semantics=("arbitrary",),
                                             vmem_limit_bytes=EXPERTS_VMEM_LIMIT),
        name="moe_experts",
    )(pos1, pos2, tile_expert, next_expert, n_active, hp, jnp.zeros((rows,), I32), wg, wu, wd)


def _combine_kernel(pos1_ref, pos2_ref, y_ref, h_ref, p1_ref, p2_ref, g_ref, b_ref, *rest, tm, tiled_copy):
    out_ref = rest[0]
    a1, a2, b1, b2, sem = rest[-5:]
    i = pl.program_id(0)
    last = pl.num_programs(0) - 1

    @pl.when(i == 0)
    def _():
        def issue(t, carry):
            _row_copy(y_ref, pos1_ref[t], a1, t, sem.at[0]).start()
            _row_copy(y_ref, pos2_ref[t], a2, t, sem.at[0]).start()
            return carry

        lax.fori_loop(0, tm, issue, 0)

    def step(c1, c2, cur_sem, n1, n2, nxt_sem):
        _tile_wait(y_ref, c1, cur_sem)
        _tile_wait(y_ref, c2, cur_sem)
        base = jnp.minimum(i + 1, last) * tm
        for t in range(tm):
            _row_copy(y_ref, pos1_ref[base + t], n1, t, nxt_sem).start(priority=0)
            _row_copy(y_ref, pos2_ref[base + t], n2, t, nxt_sem).start(priority=DMA_QUEUES - 1)
        ffn = p1_ref[...] * _load_row_tiled(c1) + p2_ref[...] * _load_row_tiled(c2)
        out = _layer_norm(DN_ALPHA * h_ref[...] + ffn, g_ref[...], b_ref[...])
        out_ref[...] = out
        if tiled_copy:
            _store_row_tiled(rest[1], out)

        @pl.when(i == last)
        def _():
            _tile_wait(y_ref, n1, nxt_sem)
            _tile_wait(y_ref, n2, nxt_sem)

    @pl.when(i % 2 == 0)
    def _():
        step(a1, a2, sem.at[0], b1, b2, sem.at[1])

    @pl.when(i % 2 == 1)
    def _():
        step(b1, b2, sem.at[1], a1, a2, sem.at[0])


def _combine(y, h, pos1, pos2, p1, p2, g, b, tm, tiled_copy):
    n = h.shape[0]
    tok = lambda i, a, c: (i, 0)
    fixed = lambda i, a, c: (0, 0)
    out_specs = [pl.BlockSpec((tm, D_MODEL), tok)]
    out_shape = [jax.ShapeDtypeStruct((n, D_MODEL), F32)]
    if tiled_copy:
        out_specs.append(pl.BlockSpec((tm * ROW_SPLIT, LANES), tok))
        out_shape.append(jax.ShapeDtypeStruct((n * ROW_SPLIT, LANES), F32))
    return pl.pallas_call(
        functools.partial(_combine_kernel, tm=tm, tiled_copy=tiled_copy),
        grid_spec=pltpu.PrefetchScalarGridSpec(
            num_scalar_prefetch=2,
            grid=(n // tm,),
            in_specs=[pl.BlockSpec(memory_space=pl.ANY),
                      pl.BlockSpec((tm, D_MODEL), tok),
                      pl.BlockSpec((tm, 1), tok), pl.BlockSpec((tm, 1), tok),
                      pl.BlockSpec((1, D_MODEL), fixed), pl.BlockSpec((1, D_MODEL), fixed)],
            out_specs=out_specs,
            scratch_shapes=[pltpu.VMEM((tm * ROW_SPLIT, LANES), F32)] * 4 + [pltpu.SemaphoreType.DMA((2,))]),
        out_shape=out_shape,
        compiler_params=_cparams("arbitrary"),
        name="moe_combine_ln",
    )(pos1, pos2, y, h, p1, p2, g, b)


def _plan_kernel(ints_ref, cnt_ref, pos_ref, meta_ref, *, layer, chunk):
    tr = MOE_ROW_TILE
    n = ints_ref.shape[1]
    ne = MOE_EXPERTS
    tiles = (cnt_ref[...] + (tr - 1.0)) * (1.0 / tr)
    tiles = tiles.astype(I32).astype(F32)
    lower = lax.broadcasted_iota(I32, (ne, ne), 0) >= lax.broadcasted_iota(I32, (ne, ne), 1)
    ends = _dot(jnp.where(lower, 1.0, 0.0).astype(BF16), tiles.astype(BF16))
    start_col = ((ends - tiles) * tr).astype(I32)[:, 0:1]
    r8 = lax.broadcasted_iota(I32, (8, chunk), 0)
    re = lax.broadcasted_iota(I32, (ne, chunk), 0)
    for c in range(n // chunk):
        blk = ints_ref[:, c * chunk:(c + 1) * chunk]
        s1 = jnp.sum(jnp.where(re == blk[0:1], start_col, 0), 0, keepdims=True)
        s2 = jnp.sum(jnp.where(re == blk[1:2], start_col, 0), 0, keepdims=True)
        pos_ref[:, c * chunk:(c + 1) * chunk] = jnp.where(
            r8 == 0, s1 + blk[2:3], jnp.where(r8 == 1, s2 + blk[3:4], 0))
    width = meta_ref.shape[1]
    tile_id = lax.broadcasted_iota(I32, (ne, width), 1).astype(F32)
    te = jnp.sum(jnp.where(ends[:, 0:1] <= tile_id, 1, 0), 0, keepdims=True)
    te = jnp.minimum(te, ne - 1)
    expert = lax.broadcasted_iota(I32, (ne, width), 0)
    later = jnp.logical_and(expert > te, tiles[:, 0:1] > 0.0)
    nxt = jnp.min(jnp.where(later, expert, ne), 0, keepdims=True)
    nxt = jnp.where(nxt < ne, nxt + layer * ne, -1)
    n_used = ends[ne - 1:ne, 0:1].astype(I32)
    rm = lax.broadcasted_iota(I32, (8, width), 0)
    meta_ref[...] = jnp.where(rm == 0, te + layer * ne,
                              jnp.where(rm == 1, n_used, jnp.where(rm == 2, nxt, 0)))


def _plan(ints, cnt, layer, n_tiles):
    n = ints.shape[1]
    width = -(-n_tiles // LANES) * LANES
    return pl.pallas_call(
        functools.partial(_plan_kernel, layer=layer, chunk=2048),
        out_shape=[jax.ShapeDtypeStruct((8, n), I32), jax.ShapeDtypeStruct((8, width), I32)],
        compiler_params=pltpu.CompilerParams(vmem_limit_bytes=VMEM_LIMIT),
        name="moe_plan",
    )(ints, cnt)


def _moe_layer(h, hp, layer, group_w, group_b, expert_w, expert_b, gate_w, up_w, down_w, ln_g, ln_b,
               tiled_copy):
    n = h.shape[0]
    ew = jnp.transpose(expert_w, (1, 0, 2)).reshape(D_MODEL, MOE_EXPERTS)
    w_r = jnp.zeros((D_MODEL, LANES), F32).at[:, 0:MOE_GROUPS].set(group_w).at[:, 8:8 + MOE_EXPERTS].set(ew)
    b_r = jnp.zeros((LANES,), F32).at[0:MOE_GROUPS].set(group_b).at[8:8 + MOE_EXPERTS].set(expert_b.reshape(-1))
    ints, flts, cnt = _router(h, w_r, b_r.reshape(LANES, 1), 512)
    tr = MOE_ROW_TILE
    n_tiles = (2 * n) // tr + MOE_EXPERTS
    pos, meta = _plan(ints, cnt, layer, n_tiles)
    pos1, pos2 = pos[0], pos[1]

    y = _experts(hp, pos1, pos2, meta[0, :n_tiles], meta[2, :n_tiles], meta[1, :1],
                 gate_w.reshape(-1, D_MODEL, MOE_D_FF),
                 up_w.reshape(-1, D_MODEL, MOE_D_FF),
                 down_w.reshape(-1, MOE_D_FF, D_MODEL))
    return _combine(y, h, pos1, pos2, flts[0].reshape(n, 1), flts[1].reshape(n, 1),
                    ln_g.reshape(1, -1), ln_b.reshape(1, -1), COMBINE_TILE, tiled_copy)


def _pad_heads(w, axis):
    parts = []
    h0 = 0
    for _, _, nh in ATTN_GROUPS:
        sl = [slice(None)] * w.ndim
        sl[axis] = slice(h0 * ATTN_HEAD_DIM, (h0 + nh) * ATTN_HEAD_DIM)
        part = w[tuple(sl)]
        pad = [(0, 0)] * w.ndim
        pad[axis] = (0, ATTN_SLAB - nh * ATTN_HEAD_DIM)
        parts.append(jnp.pad(part, pad))
        h0 += nh
    return parts


def _in_proj_kernel(x_ref, w_ref, wdt_ref, z_ref, xbc_ref, dt_ref, *, z_tiles):
    j = pl.program_id(1)
    acc = _dot(x_ref[...].astype(BF16), w_ref[...])

    @pl.when(j < z_tiles)
    def _():
        z_ref[...] = acc

    @pl.when(j >= z_tiles)
    def _():
        xbc_ref[...] = acc

    @pl.when(j == 0)
    def _():
        dt_ref[...] = _dot_f32ish(x_ref[...], wdt_ref[...])


def _in_proj(x, w_zx, w_dt, tm, tn):
    m, k = x.shape
    z_tiles = SSM_D_INNER // tn
    n_tiles = w_zx.shape[1] // tn
    return pl.pallas_call(
        functools.partial(_in_proj_kernel, z_tiles=z_tiles),
        grid=(m // tm, n_tiles),
        in_specs=[pl.BlockSpec((tm, k), lambda i, j: (i, 0)),
                  pl.BlockSpec((k, tn), lambda i, j: (0, j)),
                  pl.BlockSpec((k, LANES), lambda i, j: (0, 0))],
        out_specs=[pl.BlockSpec((tm, tn), lambda i, j: (i, jnp.minimum(j, z_tiles - 1))),
                   pl.BlockSpec((tm, tn), lambda i, j: (i, jnp.maximum(j - z_tiles, 0))),
                   pl.BlockSpec((tm, LANES), lambda i, j: (i, 0))],
        out_shape=[jax.ShapeDtypeStruct((m, SSM_D_INNER), F32),
                   jax.ShapeDtypeStruct((m, SSM_CONV_DIM), F32),
                   jax.ShapeDtypeStruct((m, LANES), F32)],
        compiler_params=_cparams("parallel", "arbitrary"),
        name="ssm_in_proj",
    )(x, w_zx, w_dt)


def _ssd_layer(h, in_w, conv_w, conv_b, dt_bias, a_log, d_skip, norm_w, out_w, ln_g, ln_b, bsz, seq):
    split = SSM_D_INNER + SSM_CONV_DIM
    dt_w = jnp.pad(in_w[:, split:], ((0, 0), (0, LANES - SSM_HEADS)))
    z, xbc, dt_raw = _in_proj(h, in_w[:, :split].astype(BF16), dt_w, 1024, 1024)
    pad32 = lambda v: jnp.pad(v, (0, LANES - SSM_HEADS)).reshape(1, LANES)
    y = _ssd(z, xbc, dt_raw, conv_w, conv_b.reshape(1, -1), pad32(dt_bias), pad32(a_log),
             jnp.repeat(d_skip, SSM_HEAD_DIM).reshape(1, -1), norm_w.reshape(1, -1), bsz, seq)
    return _matmul_res_ln(y, out_w.astype(BF16), h, ln_g.reshape(1, -1), ln_b.reshape(1, -1), 1024)


def _qkv_group_weights(kv_w, q_w):
    width = ATTN_HEADS * ATTN_HEAD_DIM
    w_q = _pad_heads(q_w * (ATTN_HEAD_DIM ** -0.5), 1)
    w_k = _pad_heads(kv_w[:, :width], 1)
    w_v = _pad_heads(kv_w[:, width:], 1)
    return [jnp.concatenate([w_q[gi], w_k[gi], w_v[gi]], axis=1).astype(BF16) for gi in range(len(ATTN_GROUPS))]


def _attn_layer(h, h8, qkv_w, o_w, rel_bias, ln_g, ln_b, bsz, seq):
    outs, lses = [], []
    h0 = 0
    for gi, (_, dil, nh) in enumerate(ATTN_GROUPS):
        if dil == 1:
            qkv = _matmul(h, qkv_w[gi], BF16, 1024, 3 * ATTN_SLAB)
        else:
            qkv = _qkv_dilated(h8, qkv_w[gi], dil, max(1024, ATTN_BLOCK * dil))
        bias_p, bias_c = _group_bias(rel_bias, h0, nh, dil)
        o, lse = _attention_group(qkv, bias_p, bias_c, gi, dil, nh, bsz, seq)
        outs.extend(o)
        lses.append(lse)
        h0 += nh
    w_o = jnp.concatenate(_pad_heads(o_w, 0), axis=0).astype(BF16)
    return _attn_out(outs, lses, w_o, h, ln_g.reshape(1, -1), ln_b.reshape(1, -1), 1024)


def kernel(x, ssm_in_w, ssm_conv_w, ssm_conv_b, ssm_dt_bias, ssm_a_log, ssm_d, ssm_norm_w, ssm_out_w,
           kv_w, attn_q_w, attn_o_w, rel_bias, moe_group_w, moe_group_b, moe_expert_w, moe_expert_b,
           moe_gate_w, moe_up_w, moe_down_w, ln_g, ln_b):
    bsz, seq, d = x.shape
    h = x.reshape(bsz * seq, d)
    n_ssd = DEPTH // 2
    h8 = None
    for i in range(DEPTH):
        if i < n_ssd:
            h, hp = _ssd_layer(h, ssm_in_w[i], ssm_conv_w[i], ssm_conv_b[i], ssm_dt_bias[i], ssm_a_log[i],
                               ssm_d[i], ssm_norm_w[i], ssm_out_w[i], ln_g[i, 0], ln_b[i, 0], bsz, seq)
        else:
            j = i - n_ssd
            h, hp = _attn_layer(h, h8, _qkv_group_weights(kv_w, attn_q_w[j]), attn_o_w[j], rel_bias,
                                ln_g[i, 0], ln_b[i, 0], bsz, seq)
        feeds_attention = n_ssd <= i + 1 < DEPTH
        res = _moe_layer(h, hp, i, moe_group_w[i], moe_group_b[i], moe_expert_w[i], moe_expert_b[i],
                         moe_gate_w, moe_up_w, moe_down_w, ln_g[i, 1], ln_b[i, 1], feeds_attention)
        h = res[0]
        h8 = res[1] if feeds_attention else None
    return h.reshape(bsz, seq, d)
```

```python
import functools
import math

import numpy as np
import jax
import jax.numpy as jnp
from jax import lax
from jax.experimental import pallas as pl
from jax.experimental.pallas import tpu as pltpu

F32 = jnp.float32
BF16 = jnp.bfloat16
I32 = jnp.int32

D_MODEL = 1024
DEPTH = 2
DN_ALPHA = (2 * DEPTH) ** 0.25
LN_EPS = 1e-5
LOG2_E = math.log2(math.e)

SSM_D_INNER = 2048
SSM_HEAD_DIM = 64
SSM_HEADS = 32
SSM_GROUPS = 4
SSM_STATE = 128
SSM_CONV = 4
SSM_CHUNK = 128
SSM_CONV_DIM = SSM_D_INNER + 2 * SSM_GROUPS * SSM_STATE

ATTN_HEAD_DIM = 64
ATTN_GROUPS = ((128, 1, 6), (512, 4, 5), (2048, 16, 5))
ATTN_HEADS = 16
ATTN_BLOCK = 128
N_BUCKETS = 32
MAX_DISTANCE = 2048
ATTN_SLAB = 384
NEG_BIG = -1e30

MOE_GROUPS = 4
MOE_EPG = 8
MOE_EXPERTS = MOE_GROUPS * MOE_EPG
MOE_D_FF = 512
MOE_ROW_TILE = 256
COMBINE_TILE = 512

LANES = 128
DMA_QUEUES = 2
VMEM_LIMIT = 48 * 1024 * 1024
EXPERTS_VMEM_LIMIT = 56 * 1024 * 1024


def _cparams(*sem):
    return pltpu.CompilerParams(dimension_semantics=sem, vmem_limit_bytes=VMEM_LIMIT)


def _layer_norm(x, g, b):
    mu = jnp.mean(x, -1, keepdims=True)
    xc = x - mu
    var = jnp.mean(xc * xc, -1, keepdims=True)
    return xc * lax.rsqrt(var + LN_EPS) * g + b


def _split2(x):
    hi = x.astype(BF16)
    lo = (x - hi.astype(F32)).astype(BF16)
    return hi, lo


def _dot(a, b):
    return jnp.dot(a, b, preferred_element_type=F32)


def _dot_f32ish(a, b):
    ah, al = _split2(a)
    bh, bl = _split2(b)
    return _dot(ah, bh) + _dot(al, bh) + _dot(ah, bl)


def _mm_kernel(a_ref, b_ref, o_ref):
    o_ref[...] = _dot(a_ref[...].astype(BF16), b_ref[...]).astype(o_ref.dtype)


def _matmul(a, b, out_dtype, tm, tn):
    m, k = a.shape
    nc = b.shape[1]
    return pl.pallas_call(
        _mm_kernel,
        grid=(m // tm, nc // tn),
        in_specs=[pl.BlockSpec((tm, k), lambda i, j: (i, 0)),
                  pl.BlockSpec((k, tn), lambda i, j: (0, j))],
        out_specs=pl.BlockSpec((tm, tn), lambda i, j: (i, j)),
        out_shape=jax.ShapeDtypeStruct((m, nc), out_dtype),
        compiler_params=_cparams("parallel", "parallel"),
        name="matmul",
    )(a, b)


def _mm3_kernel(a_ref, b_ref, o_ref):
    o_ref[...] = _dot_f32ish(a_ref[...], b_ref[...])


def _matmul_f32ish(a, b, tm):
    m, k = a.shape
    nc = b.shape[1]
    return pl.pallas_call(
        _mm3_kernel,
        grid=(m // tm,),
        in_specs=[pl.BlockSpec((tm, k), lambda i: (i, 0)),
                  pl.BlockSpec((k, nc), lambda i: (0, 0))],
        out_specs=pl.BlockSpec((tm, nc), lambda i: (i, 0)),
        out_shape=jax.ShapeDtypeStruct((m, nc), F32),
        compiler_params=_cparams("parallel"),
        name="matmul_f32ish",
    )(a, b)


ROW_SPLIT = D_MODEL // LANES


def _store_row_tiled(ref, val):
    rows = val.shape[0]
    for c in range(ROW_SPLIT):
        ref[pl.ds(c, rows, stride=ROW_SPLIT), :] = val[:, c * LANES:(c + 1) * LANES]


def _load_row_tiled(ref):
    rows = ref.shape[0] // ROW_SPLIT
    return jnp.concatenate([ref[pl.ds(c, rows, stride=ROW_SPLIT), :] for c in range(ROW_SPLIT)], axis=1)


HALF = D_MODEL // 2
U32 = jnp.uint32
HI16 = 0xFFFF0000


def _pack_bf16_pairs(x):
    bits = lax.bitcast_convert_type(x.astype(BF16).astype(F32), U32)
    return (bits[:, :HALF] >> 16) | (bits[:, HALF:] & U32(HI16))


def _unpack_bf16_pairs(w):
    lo = lax.bitcast_convert_type(w << 16, F32).astype(BF16)
    hi = lax.bitcast_convert_type(w & U32(HI16), F32).astype(BF16)
    return lo, hi


def _mm_ln_kernel(a_ref, w_ref, h_ref, g_ref, b_ref, o_ref, op_ref):
    acc = _dot(a_ref[...], w_ref[...])
    out = _layer_norm(DN_ALPHA * h_ref[...] + acc, g_ref[...], b_ref[...])
    o_ref[...] = out
    op_ref[...] = _pack_bf16_pairs(out)


def _matmul_res_ln(a, w, h, g, b, tm):
    m, k = a.shape
    d = w.shape[1]
    return pl.pallas_call(
        _mm_ln_kernel,
        grid=(m // tm,),
        in_specs=[pl.BlockSpec((tm, k), lambda i: (i, 0)),
                  pl.BlockSpec((k, d), lambda i: (0, 0)),
                  pl.BlockSpec((tm, d), lambda i: (i, 0)),
                  pl.BlockSpec((1, d), lambda i: (0, 0)),
                  pl.BlockSpec((1, d), lambda i: (0, 0))],
        out_specs=[pl.BlockSpec((tm, d), lambda i: (i, 0)),
                   pl.BlockSpec((tm, HALF), lambda i: (i, 0))],
        out_shape=[jax.ShapeDtypeStruct((m, d), F32),
                   jax.ShapeDtypeStruct((m, HALF), U32)],
        compiler_params=_cparams("parallel"),
        name="matmul_res_ln",
    )(a, w, h, g, b)


SSD_CHUNKS_PER_STEP = 2


def _ssd_kernel(z_ref, xbc_ref, dt_ref, cw_ref, cb_ref, dtb_ref, alog_ref, dsk_ref, nw_ref,
                y_ref, xe_ref, st_ref):
    @pl.when(pl.program_id(1) == 0)
    def _():
        xe_ref[:, 0:8, :] = jnp.zeros((xe_ref.shape[0], 8, LANES), F32)
        st_ref[...] = jnp.zeros_like(st_ref)

    for sub in range(SSD_CHUNKS_PER_STEP):
        _ssd_chunk(pl.ds(sub * SSM_CHUNK, SSM_CHUNK), z_ref, xbc_ref, dt_ref, cw_ref, cb_ref, dtb_ref,
                   alog_ref, dsk_ref, nw_ref, y_ref, xe_ref, st_ref)


def _ssd_chunk(rows, z_ref, xbc_ref, dt_ref, cw_ref, cb_ref, dtb_ref, alog_ref, dsk_ref, nw_ref,
               y_ref, xe_ref, st_ref):
    q = SSM_CHUNK
    w = cw_ref[...]
    bias = cb_ref[...]
    act = []
    for c in range(SSM_CONV_DIM // LANES):
        cols = slice(c * LANES, (c + 1) * LANES)
        u = xbc_ref[rows, cols]
        xe_ref[c, 8:8 + q, :] = u
        conv = (bias[:, cols] + w[3:4, cols] * u + w[2:3, cols] * xe_ref[c, 7:7 + q, :]
                + w[1:2, cols] * xe_ref[c, 6:6 + q, :] + w[0:1, cols] * xe_ref[c, 5:5 + q, :])
        xe_ref[c, 0:8, :] = xe_ref[c, q:q + 8, :]
        act.append(conv * jax.nn.sigmoid(conv))

    pre = dt_ref[rows, :] + dtb_ref[...]
    dt = jnp.maximum(pre, 0.0) + jnp.log(1.0 + jnp.exp(-jnp.abs(pre)))
    adt = dt * (-jnp.exp(alog_ref[...]) * LOG2_E)

    row = lax.broadcasted_iota(I32, (q, q), 0)
    col = lax.broadcasted_iota(I32, (q, q), 1)
    tril = row >= col
    tri_b = jnp.where(tril, 1.0, 0.0).astype(BF16)
    a_hi = adt.astype(BF16)
    r1 = adt - a_hi.astype(F32)
    a_mid = r1.astype(BF16)
    a_lo = (r1 - a_mid.astype(F32)).astype(BF16)
    acs = _dot(tri_b, a_hi) + _dot(tri_b, a_mid) + _dot(tri_b, a_lo)
    acs_t = acs.T
    acs_dt_t = acs_t - jnp.log(dt.T) * LOG2_E
    eacs = jnp.exp2(acs)
    left = col < SSM_HEAD_DIM

    for g in range(SSM_GROUPS):
        b0 = SSM_D_INNER + g * SSM_STATE
        c0 = SSM_D_INNER + SSM_GROUPS * SSM_STATE + g * SSM_STATE
        bm = act[b0 // LANES]
        cm = act[c0 // LANES].astype(BF16)
        cb = lax.dot_general(cm, bm.astype(BF16), (((1,), (1,)), ((), ())),
                             preferred_element_type=F32)
        bm_t = bm.T
        gs = g * 512
        y_off = _dot(cm, st_ref[:, gs:gs + 512].astype(BF16))
        slabs = []
        for pr in range(4):
            ha = g * 8 + pr * 2
            hb = ha + 1
            cs = gs + pr * LANES
            x2 = act[cs // LANES]
            x2b = x2.astype(BF16)
            ys, ups = [], []
            for h in (ha, hb):
                a_col = acs[:, h:h + 1]
                a_src = acs_dt_t[h:h + 1, :]
                decay = jnp.where(tril, jnp.exp2(a_col - a_src), 0.0)
                ys.append(_dot((cb * decay).astype(BF16), x2b))
                to_end = jnp.exp2(acs_t[h:h + 1, q - 1:q] - a_src)
                ups.append(_dot((bm_t * to_end).astype(BF16), x2b))
            y_diag = jnp.where(left, ys[0], ys[1])
            upd = jnp.where(left, ups[0], ups[1])
            e2 = jnp.where(left, eacs[:, ha:ha + 1], eacs[:, hb:hb + 1])
            cd = jnp.where(left[0:1, :], eacs[q - 1:q, ha:ha + 1], eacs[q - 1:q, hb:hb + 1])
            y2 = y_diag + y_off[:, pr * LANES:(pr + 1) * LANES] * e2 + dsk_ref[:, cs:cs + LANES] * x2
            st_ref[:, cs:cs + LANES] = st_ref[:, cs:cs + LANES] * cd + upd
            slabs.append(y2)
        yg = jnp.concatenate(slabs, axis=1)
        zg = z_ref[rows, gs:gs + 512]
        yg = yg * (zg * jax.nn.sigmoid(zg))
        ms = jnp.mean(yg * yg, -1, keepdims=True)
        y_ref[rows, gs:gs + 512] = (yg * lax.rsqrt(ms + LN_EPS) * nw_ref[:, gs:gs + 512]).astype(y_ref.dtype)


def _ssd(z, xbc, dt_raw, conv_w, conv_b, dt_bias, a_log, d_rep, norm_w, bsz, seq):
    n = z.shape[0]
    q = SSM_CHUNK * SSD_CHUNKS_PER_STEP
    nchunk = seq // q
    tok = lambda b, c: (b * nchunk + c, 0)
    fixed = lambda b, c: (0, 0)
    return pl.pallas_call(
        _ssd_kernel,
        grid=(bsz, nchunk),
        in_specs=[pl.BlockSpec((q, SSM_D_INNER), tok),
                  pl.BlockSpec((q, SSM_CONV_DIM), tok),
                  pl.BlockSpec((q, LANES), tok),
                  pl.BlockSpec((SSM_CONV, SSM_CONV_DIM), fixed),
                  pl.BlockSpec((1, SSM_CONV_DIM), fixed),
                  pl.BlockSpec((1, LANES), fixed),
                  pl.BlockSpec((1, LANES), fixed),
                  pl.BlockSpec((1, SSM_D_INNER), fixed),
                  pl.BlockSpec((1, SSM_D_INNER), fixed)],
        out_specs=pl.BlockSpec((q, SSM_D_INNER), tok),
        out_shape=jax.ShapeDtypeStruct((n, SSM_D_INNER), BF16),
        scratch_shapes=[pltpu.VMEM((SSM_CONV_DIM // LANES, SSM_CHUNK + 8, LANES), F32),
                        pltpu.VMEM((SSM_STATE, SSM_D_INNER), F32)],
        compiler_params=_cparams("parallel", "arbitrary"),
        name="ssd_chunk",
    )(z, xbc, dt_raw, conv_w, conv_b, dt_bias, a_log, d_rep, norm_w)


def _residue_major_pieces(h8_ref, n_tokens, dil):
    span = ATTN_BLOCK * dil
    pieces = []
    for blk in range(n_tokens // span):
        for r in range(dil):
            first = (blk * span + r) * ROW_SPLIT
            pieces.append(jnp.concatenate(
                [h8_ref[pl.ds(first + c, ATTN_BLOCK, stride=ROW_SPLIT * dil), :] for c in range(ROW_SPLIT)],
                axis=1).astype(BF16))
    return pieces


def _qkv_dilated_kernel(h8_ref, w_ref, o_ref, *, dil):
    per_dot = 4
    rows = _residue_major_pieces(h8_ref, o_ref.shape[0], dil)
    for k in range(0, len(rows), per_dot):
        x = jnp.concatenate(rows[k:k + per_dot], axis=0)
        o_ref[k * ATTN_BLOCK:(k + per_dot) * ATTN_BLOCK, :] = _dot(x, w_ref[...]).astype(o_ref.dtype)


def _qkv_dilated(h8, w, dil, tm):
    n = h8.shape[0] // ROW_SPLIT
    nc = w.shape[1]
    return pl.pallas_call(
        functools.partial(_qkv_dilated_kernel, dil=dil),
        grid=(n // tm,),
        in_specs=[pl.BlockSpec((tm * ROW_SPLIT, LANES), lambda i: (i, 0)),
                  pl.BlockSpec((D_MODEL, nc), lambda i: (0, 0))],
        out_specs=pl.BlockSpec((tm, nc), lambda i: (i, 0)),
        out_shape=jax.ShapeDtypeStruct((n, nc), BF16),
        compiler_params=_cparams("parallel"),
        name=f"qkv_dil{dil}",
    )(h8, w)


def _attn_kernel(*refs, nh, dil, has_prev, blocks):
    span = ATTN_BLOCK * dil
    if has_prev:
        q_ref, kp_ref, kc_ref, vp_ref, vc_ref, bp_ref, bc_ref = refs[:7]
        out_refs = refs[7:]
        first_pen = jnp.where(pl.program_id(1) > 0, 0.0, NEG_BIG)
    else:
        q_ref, kc_ref, vc_ref, bc_ref = refs[:4]
        out_refs = refs[4:]
    s_scr, p_scr, max_scr, den_scr = out_refs[-4:]
    o_refs, lse_ref = out_refs[:-5], out_refs[-5]
    ones = jnp.ones((s_scr.shape[2], LANES), BF16)
    lane = lax.broadcasted_iota(I32, (ATTN_BLOCK, LANES), 1)
    left = lane < ATTN_HEAD_DIM
    zero = jnp.zeros((), BF16)
    nt = (((1,), (1,)), ((), ()))

    def residue(blk, r):
        rows = pl.ds(pl.multiple_of(blk * span + r * ATTN_BLOCK, ATTN_BLOCK), ATTN_BLOCK)
        dst = pl.ds(blk * span + r, ATTN_BLOCK, stride=dil)
        if has_prev and blk == 0:
            kprev, vprev, pen = kp_ref, vp_ref, first_pen
            prows = pl.ds(pl.multiple_of(r * ATTN_BLOCK, ATTN_BLOCK), ATTN_BLOCK)
        elif has_prev:
            kprev, vprev, pen = kc_ref, vc_ref, 0.0
            prows = pl.ds(pl.multiple_of((blk - 1) * span + r * ATTN_BLOCK, ATTN_BLOCK), ATTN_BLOCK)
        for hh in range(nh):
            cols = pl.ds(hh // 2 * LANES, LANES)
            qm = jnp.where(left if hh % 2 == 0 else ~left, q_ref[rows, cols], zero)
            s_c = lax.dot_general(qm, kc_ref[rows, cols], nt, preferred_element_type=F32) + bc_ref[hh]
            if has_prev:
                s_p = lax.dot_general(qm, kprev[prows, cols], nt, preferred_element_type=F32)
                s_scr[hh, :, 0:ATTN_BLOCK] = s_p + (bp_ref[hh] + pen)
                s_scr[hh, :, ATTN_BLOCK:] = s_c
            else:
                s_scr[hh] = s_c
        for hh in range(nh):
            s = s_scr[hh]
            m = jnp.max(s, -1, keepdims=True)
            p = jnp.exp(s - m)
            p_scr[hh] = p.astype(BF16)
            max_scr[hh] = m
            if not has_prev:
                den_scr[hh] = jnp.sum(p, -1, keepdims=True)
        lse_sum = jnp.zeros((ATTN_BLOCK, 1), F32)
        for pr in range(ATTN_SLAB // LANES):
            cols = pl.ds(pr * LANES, LANES)
            halves = []
            for hh in (pr * 2, pr * 2 + 1):
                if hh >= nh:
                    halves.append(jnp.zeros((ATTN_BLOCK, LANES), F32))
                    continue
                if has_prev:
                    den = _dot(p_scr[hh], ones)
                    den1 = den[:, 0:1]
                    o = (_dot(p_scr[hh, :, 0:ATTN_BLOCK], vprev[prows, cols])
                         + _dot(p_scr[hh, :, ATTN_BLOCK:], vc_ref[rows, cols]))
                else:
                    den = den1 = den_scr[hh]
                    o = _dot(p_scr[hh], vc_ref[rows, cols])
                halves.append(o * (1.0 / den))
                lse_sum = lse_sum + (max_scr[hh] + jnp.log(den1))
            o_refs[pr][dst, :] = jnp.where(left, halves[0], halves[1])
        lse_ref[dst, :] = jnp.broadcast_to(lse_sum * (1.0 / nh), (ATTN_BLOCK, LANES))

    for blk in range(blocks):
        def body(r, carry, blk=blk):
            residue(blk, r)
            return carry

        lax.fori_loop(0, dil, body, 0)


def _attention_group(qkv, bias_p, bias_c, gi, dil, nh, bsz, seq):
    span = ATTN_BLOCK * dil
    nb = seq // span
    has_prev = nb > 1
    blocks = max(1, 512 // span)
    steps = nb // blocks
    blk = (blocks * span, ATTN_SLAB)
    cur = lambda which: (lambda b, n: (b * steps + n, which))
    prev = lambda which: (lambda b, n: (b * nb + jnp.maximum(n * blocks - 1, 0), which))
    fixed = lambda b, n: (0, 0, 0)
    tab = pl.BlockSpec((nh, ATTN_BLOCK, ATTN_BLOCK), fixed)
    if has_prev:
        pblk = (span, ATTN_SLAB)
        in_specs = [pl.BlockSpec(blk, cur(0)), pl.BlockSpec(pblk, prev(1)), pl.BlockSpec(blk, cur(1)),
                    pl.BlockSpec(pblk, prev(2)), pl.BlockSpec(blk, cur(2)), tab, tab]
        args = (qkv, qkv, qkv, qkv, qkv, bias_p, bias_c)
    else:
        in_specs = [pl.BlockSpec(blk, cur(0)), pl.BlockSpec(blk, cur(1)), pl.BlockSpec(blk, cur(2)), tab]
        args = (qkv, qkv, qkv, bias_c)
    n_out = ATTN_SLAB // LANES + 1
    keys = 2 * ATTN_BLOCK if has_prev else ATTN_BLOCK
    outs = pl.pallas_call(
        functools.partial(_attn_kernel, nh=nh, dil=dil, has_prev=has_prev, blocks=blocks),
        grid=(bsz, steps),
        in_specs=in_specs,
        out_specs=[pl.BlockSpec((blocks * span, LANES), lambda b, n: (b * steps + n, 0))] * n_out,
        out_shape=[jax.ShapeDtypeStruct((bsz * seq, LANES), F32)] * n_out,
        scratch_shapes=[pltpu.VMEM((nh, ATTN_BLOCK, keys), F32),
                        pltpu.VMEM((nh, ATTN_BLOCK, keys), BF16),
                        pltpu.VMEM((nh, ATTN_BLOCK, 1), F32),
                        pltpu.VMEM((nh, ATTN_BLOCK, 1), F32)],
        compiler_params=_cparams("parallel", "parallel"),
        name=f"dilated_attn_g{gi}",
    )(*args)
    return outs[:-1], outs[-1]


def _attn_out_kernel(*refs):
    n_pairs = ATTN_SLAB // LANES
    n_groups = len(ATTN_GROUPS)
    o_refs = refs[:n_groups * n_pairs]
    l_refs = refs[n_groups * n_pairs:n_groups * (n_pairs + 1)]
    w_ref, h_ref, g_ref, b_ref, out_ref, outp_ref = refs[n_groups * (n_pairs + 1):]
    ls = [r[:, 0:1] for r in l_refs]
    m = jnp.maximum(jnp.maximum(ls[0], ls[1]), ls[2])
    es = [jnp.exp(v - m) for v in ls]
    scale = n_groups / (es[0] + es[1] + es[2])
    slabs = []
    for gi in range(n_groups):
        wt = es[gi] * scale
        slabs.extend((r[...] * wt).astype(BF16) for r in o_refs[gi * n_pairs:(gi + 1) * n_pairs])
    acc = _dot(jnp.concatenate(slabs, axis=1), w_ref[...])
    out = _layer_norm(DN_ALPHA * h_ref[...] + acc, g_ref[...], b_ref[...])
    out_ref[...] = out
    outp_ref[...] = _pack_bf16_pairs(out)


def _attn_out(outs, lses, w, h, g, b, tm):
    n = h.shape[0]
    tok = lambda i: (i, 0)
    fixed2 = lambda i: (0, 0)
    return pl.pallas_call(
        _attn_out_kernel,
        grid=(n // tm,),
        in_specs=[pl.BlockSpec((tm, LANES), tok)] * (len(outs) + len(lses))
                 + [pl.BlockSpec((len(ATTN_GROUPS) * ATTN_SLAB, D_MODEL), lambda i: (0, 0)),
                    pl.BlockSpec((tm, D_MODEL), tok),
                    pl.BlockSpec((1, D_MODEL), fixed2), pl.BlockSpec((1, D_MODEL), fixed2)],
        out_specs=[pl.BlockSpec((tm, D_MODEL), tok), pl.BlockSpec((tm, HALF), tok)],
        out_shape=[jax.ShapeDtypeStruct((n, D_MODEL), F32),
                   jax.ShapeDtypeStruct((n, HALF), U32)],
        compiler_params=_cparams("parallel"),
        name="attn_out_ln",
    )(*outs, *lses, w, h, g, b)


def _t5_bucket(dist):
    max_exact = N_BUCKETS // 2
    n = np.maximum(dist, 1).astype(np.float64)
    large = max_exact + (np.log(n / max_exact) / np.log(MAX_DISTANCE / max_exact)
                         * (N_BUCKETS - max_exact)).astype(np.int32)
    large = np.minimum(large, N_BUCKETS - 1)
    return np.where(dist < max_exact, dist, large).astype(np.int32)


def _group_bias(rel_bias, h0, nh, dil):
    qi = np.arange(ATTN_BLOCK)[:, None]
    ki = np.arange(ATTN_BLOCK)[None, :]
    tabs = []
    for delta, band in ((qi + ATTN_BLOCK - ki, ki >= qi), (qi - ki, ki <= qi)):
        bucket = _t5_bucket(np.clip(delta, 0, None) * dil)
        onehot = (bucket[..., None] == np.arange(N_BUCKETS)).astype(np.float32)
        t = jnp.einsum("qkb,bh->hqk", onehot, rel_bias[:, h0:h0 + nh], precision=lax.Precision.HIGHEST)
        tabs.append(jnp.where(band[None], t, NEG_BIG).astype(F32))
    return tabs


def _router_kernel(h_ref, w_ref, b_ref, ints_ref, flts_ref, cnt_ref, carry_ref, before_ref, *, tm):
    @pl.when(pl.program_id(0) == 0)
    def _():
        carry_ref[...] = jnp.zeros_like(carry_ref)
        tr = lax.broadcasted_iota(I32, (tm, tm), 0)
        tc = lax.broadcasted_iota(I32, (tm, tm), 1)
        before_ref[...] = jnp.where(tr < tc, 1.0, 0.0).astype(BF16)

    lt = _dot_f32ish(h_ref[...], w_ref[...]).T + b_ref[...]
    gl = lt[0:MOE_GROUPS]
    r4 = lax.broadcasted_iota(I32, (MOE_GROUPS, tm), 0)
    gmax = jnp.max(gl, 0, keepdims=True)
    gidx = jnp.min(jnp.where(gl == gmax, r4, MOE_GROUPS), 0, keepdims=True)
    gval = 1.0 / jnp.sum(jnp.exp(gl - gmax), 0, keepdims=True)

    esel = jnp.zeros((MOE_EPG, tm), F32)
    for g in range(MOE_GROUPS):
        esel = jnp.where(gidx == g, lt[8 + g * MOE_EPG:8 + (g + 1) * MOE_EPG], esel)
    r8 = lax.broadcasted_iota(I32, (MOE_EPG, tm), 0)
    v1 = jnp.max(esel, 0, keepdims=True)
    i1 = jnp.min(jnp.where(esel == v1, r8, MOE_EPG), 0, keepdims=True)
    rest = jnp.where(r8 == i1, -jnp.inf, esel)
    v2 = jnp.max(rest, 0, keepdims=True)
    i2 = jnp.min(jnp.where(rest == v2, r8, MOE_EPG), 0, keepdims=True)
    t = jnp.exp(v2 - v1)
    p1 = gval / (1.0 + t)
    p2 = p1 * t
    e1 = gidx * MOE_EPG + i1
    e2 = gidx * MOE_EPG + i2

    r32 = lax.broadcasted_iota(I32, (MOE_EXPERTS, tm), 0)
    oh1 = r32 == e1
    oh2 = r32 == e2
    oh = jnp.where(oh1 | oh2, 1.0, 0.0)
    base = _dot(oh.astype(BF16), before_ref[...]) + carry_ref[:, 0:1]
    rank1 = jnp.sum(jnp.where(oh1, base, 0.0), 0, keepdims=True).astype(I32)
    rank2 = jnp.sum(jnp.where(oh2, base, 0.0), 0, keepdims=True).astype(I32)
    carry_ref[...] = carry_ref[...] + jnp.sum(oh, 1, keepdims=True)
    cnt_ref[...] = carry_ref[...]

    ints_ref[...] = jnp.where(r8 == 0, e1, jnp.where(r8 == 1, e2, jnp.where(r8 == 2, rank1,
                              jnp.where(r8 == 3, rank2, 0))))
    r128 = lax.broadcasted_iota(I32, (LANES, tm), 0)
    flts_ref[...] = jnp.where(r128 == 0, p1, jnp.where(r128 == 1, p2, 0.0)).T


def _router(h, w_r, b_r, tm):
    n = h.shape[0]
    return pl.pallas_call(
        functools.partial(_router_kernel, tm=tm),
        grid=(n // tm,),
        in_specs=[pl.BlockSpec((tm, D_MODEL), lambda i: (i, 0)),
                  pl.BlockSpec((D_MODEL, LANES), lambda i: (0, 0)),
                  pl.BlockSpec((LANES, 1), lambda i: (0, 0))],
        out_specs=[pl.BlockSpec((8, tm), lambda i: (0, i)),
                   pl.BlockSpec((tm, LANES), lambda i: (i, 0)),
                   pl.BlockSpec((MOE_EXPERTS, LANES), lambda i: (0, 0))],
        out_shape=[jax.ShapeDtypeStruct((8, n), I32),
                   jax.ShapeDtypeStruct((n, LANES), F32),
                   jax.ShapeDtypeStruct((MOE_EXPERTS, LANES), F32)],
        scratch_shapes=[pltpu.VMEM((MOE_EXPERTS, LANES), F32), pltpu.VMEM((tm, tm), BF16)],
        compiler_params=_cparams("arbitrary"),
        name="moe_router",
    )(h, w_r, b_r)


def _row_copy(src_ref, s, dst_ref, d, sem):
    def first(r):
        return r * ROW_SPLIT if isinstance(r, int) else pl.multiple_of(r * ROW_SPLIT, ROW_SPLIT)

    return pltpu.make_async_copy(src_ref.at[pl.ds(first(s), ROW_SPLIT), :],
                                 dst_ref.at[pl.ds(first(d), ROW_SPLIT), :], sem)


def _tile_wait(src_ref, dst_ref, sem):
    pltpu.make_async_copy(src_ref.at[pl.ds(0, dst_ref.shape[0]), :], dst_ref, sem).wait()


def _experts_kernel(pos1_ref, pos2_ref, te_ref, nxt_ref, na_ref, hp_hbm, zeros_ref, wg_ref, wu_ref, wd_ref,
                    y_ref, hp_ref, xa, xb, wg_f, wu_f, wd_f, wg_s, wu_s, wd_s, row_tok, slot_ref,
                    sem, hsem, wsem):
    i = pl.program_id(0)
    n_tiles = pl.num_programs(0)
    tr = MOE_ROW_TILE
    active = i < na_ref[0]
    changed = jnp.logical_or(i == 0, te_ref[i] != te_ref[jnp.maximum(i - 1, 0)])

    def weight_copies(expert, slot):
        return [pltpu.make_async_copy(src.at[expert], dst.at[slot], wsem.at[slot])
                for src, dst in ((wg_ref, wg_f), (wu_ref, wu_f), (wd_ref, wd_f))]

    @pl.when(i == 0)
    def _():
        slot_ref[0] = 0
        rows_in = pltpu.make_async_copy(hp_hbm, hp_ref, hsem)
        rows_in.start()
        for c in weight_copies(te_ref[0], 0):
            c.start()
        clear = pltpu.make_async_copy(zeros_ref, row_tok, sem)
        clear.start()
        clear.wait()

        def invert(t, carry):
            row_tok[pos1_ref[t]] = t
            row_tok[pos2_ref[t]] = t
            return carry

        lax.fori_loop(0, pos1_ref.shape[0], invert, 0, unroll=16)
        rows_in.wait()

        def pick(r, carry):
            xa[pl.ds(r, 1), :] = hp_ref[pl.ds(row_tok[r], 1), :]
            return carry

        lax.fori_loop(0, tr, pick, 0, unroll=8)

    @pl.when(jnp.logical_and(active, changed))
    def _():
        slot = slot_ref[0]
        for c in weight_copies(te_ref[i], slot):
            c.wait()
        following = nxt_ref[i]

        @pl.when(following >= 0)
        def _():
            for c in weight_copies(following, 1 - slot):
                c.start()

        wg_s[...] = wg_f[slot].astype(BF16)
        wu_s[...] = wu_f[slot].astype(BF16)
        wd_s[...] = wd_f[slot].astype(BF16)
        slot_ref[0] = 1 - slot

    def step(cur, nxt):
        base = jnp.minimum(i + 1, n_tiles - 1) * tr
        for r in range(tr):
            nxt[r:r + 1, :] = hp_ref[pl.ds(row_tok[base + r], 1), :]
        lo, hi = _unpack_bf16_pairs(cur[...])
        gate = _dot(lo, wg_s[0:HALF, :]) + _dot(hi, wg_s[HALF:, :])
        up = _dot(lo, wu_s[0:HALF, :]) + _dot(hi, wu_s[HALF:, :])
        hid = (gate * jax.nn.sigmoid(gate) * up).astype(BF16)
        _store_row_tiled(y_ref, _dot(hid, wd_s[...]))

    even = i % 2 == 0

    @pl.when(jnp.logical_and(active, even))
    def _():
        step(xa, xb)

    @pl.when(jnp.logical_and(active, jnp.logical_not(even)))
    def _():
        step(xb, xa)

    @pl.when(jnp.logical_not(active))
    def _():
        y_ref[...] = jnp.zeros_like(y_ref)


def _experts(hp, pos1, pos2, tile_expert, next_expert, n_active, wg, wu, wd):
    tr = MOE_ROW_TILE
    n_tiles = tile_expert.shape[0]
    rows = n_tiles * tr
    hbm = pl.BlockSpec(memory_space=pl.ANY)
    return pl.pallas_call(
        _experts_kernel,
        grid_spec=pltpu.PrefetchScalarGridSpec(
            num_scalar_prefetch=5,
            grid=(n_tiles,),
            in_specs=[hbm, hbm, hbm, hbm, hbm],
            out_specs=pl.BlockSpec((tr * ROW_SPLIT, LANES), lambda i, *_: (i, 0)),
            scratch_shapes=[pltpu.VMEM(hp.shape, U32),
                            pltpu.VMEM((tr, HALF), U32),
                            pltpu.VMEM((tr, HALF), U32),
                            pltpu.VMEM((2, D_MODEL, MOE_D_FF), F32),
                            pltpu.VMEM((2, D_MODEL, MOE_D_FF), F32),
                            pltpu.VMEM((2, MOE_D_FF, D_MODEL), F32),
                            pltpu.VMEM((D_MODEL, MOE_D_FF), BF16),
                            pltpu.VMEM((D_MODEL, MOE_D_FF), BF16),
                            pltpu.VMEM((MOE_D_FF, D_MODEL), BF16),
                            pltpu.SMEM((rows,), I32),
                            pltpu.SMEM((1,), I32),
                            pltpu.SemaphoreType.DMA(()),
                            pltpu.SemaphoreType.DMA(()),
                            pltpu.SemaphoreType.DMA((2,))]),
        out_shape=jax.ShapeDtypeStruct((rows * ROW_SPLIT, LANES), F32),
        compiler_params=pltpu.CompilerParams(dimension_semantics=("arbitrary",),
                                             vmem_limit_bytes=EXPERTS_VMEM_LIMIT),
        name="moe_experts",
    )(pos1, pos2, tile_expert, next_expert, n_active, hp, jnp.zeros((rows,), I32), wg, wu, wd)


def _combine_kernel(pos1_ref, pos2_ref, y_ref, h_ref, p_ref, g_ref, b_ref, *rest, tm, tiled_copy):
    out_ref = rest[0]
    a1, a2, b1, b2, sem = rest[-5:]
    i = pl.program_id(0)
    last = pl.num_programs(0) - 1

    @pl.when(i == 0)
    def _():
        def issue(t, carry):
            _row_copy(y_ref, pos1_ref[t], a1, t, sem.at[0]).start()
            _row_copy(y_ref, pos2_ref[t], a2, t, sem.at[0]).start()
            return carry

        lax.fori_loop(0, tm, issue, 0)

    def step(c1, c2, cur_sem, n1, n2, nxt_sem):
        _tile_wait(y_ref, c1, cur_sem)
        _tile_wait(y_ref, c2, cur_sem)
        base = jnp.minimum(i + 1, last) * tm
        for t in range(tm):
            _row_copy(y_ref, pos1_ref[base + t], n1, t, nxt_sem).start(priority=0)
            _row_copy(y_ref, pos2_ref[base + t], n2, t, nxt_sem).start(priority=DMA_QUEUES - 1)
        ffn = p_ref[:, 0:1] * _load_row_tiled(c1) + p_ref[:, 1:2] * _load_row_tiled(c2)
        out = _layer_norm(DN_ALPHA * h_ref[...] + ffn, g_ref[...], b_ref[...])
        out_ref[...] = out
        if tiled_copy:
            _store_row_tiled(rest[1], out)

        @pl.when(i == last)
        def _():
            _tile_wait(y_ref, n1, nxt_sem)
            _tile_wait(y_ref, n2, nxt_sem)

    @pl.when(i % 2 == 0)
    def _():
        step(a1, a2, sem.at[0], b1, b2, sem.at[1])

    @pl.when(i % 2 == 1)
    def _():
        step(b1, b2, sem.at[1], a1, a2, sem.at[0])


def _combine(y, h, pos1, pos2, gates, g, b, tm, tiled_copy):
    n = h.shape[0]
    tok = lambda i, a, c: (i, 0)
    fixed = lambda i, a, c: (0, 0)
    out_specs = [pl.BlockSpec((tm, D_MODEL), tok)]
    out_shape = [jax.ShapeDtypeStruct((n, D_MODEL), F32)]
    if tiled_copy:
        out_specs.append(pl.BlockSpec((tm * ROW_SPLIT, LANES), tok))
        out_shape.append(jax.ShapeDtypeStruct((n * ROW_SPLIT, LANES), F32))
    return pl.pallas_call(
        functools.partial(_combine_kernel, tm=tm, tiled_copy=tiled_copy),
        grid_spec=pltpu.PrefetchScalarGridSpec(
            num_scalar_prefetch=2,
            grid=(n // tm,),
            in_specs=[pl.BlockSpec(memory_space=pl.ANY),
                      pl.BlockSpec((tm, D_MODEL), tok),
                      pl.BlockSpec((tm, LANES), tok),
                      pl.BlockSpec((1, D_MODEL), fixed), pl.BlockSpec((1, D_MODEL), fixed)],
            out_specs=out_specs,
            scratch_shapes=[pltpu.VMEM((tm * ROW_SPLIT, LANES), F32)] * 4 + [pltpu.SemaphoreType.DMA((2,))]),
        out_shape=out_shape,
        compiler_params=_cparams("arbitrary"),
        name="moe_combine_ln",
    )(pos1, pos2, y, h, gates, g, b)


def _plan_kernel(ints_ref, cnt_ref, pos_ref, meta_ref, *, layer, chunk):
    tr = MOE_ROW_TILE
    n = ints_ref.shape[1]
    ne = MOE_EXPERTS
    tiles = (cnt_ref[...] + (tr - 1.0)) * (1.0 / tr)
    tiles = tiles.astype(I32).astype(F32)
    lower = lax.broadcasted_iota(I32, (ne, ne), 0) >= lax.broadcasted_iota(I32, (ne, ne), 1)
    ends = _dot(jnp.where(lower, 1.0, 0.0).astype(BF16), tiles.astype(BF16))
    start_col = ((ends - tiles) * tr).astype(I32)[:, 0:1]
    r8 = lax.broadcasted_iota(I32, (8, chunk), 0)
    re = lax.broadcasted_iota(I32, (ne, chunk), 0)
    for c in range(n // chunk):
        blk = ints_ref[:, c * chunk:(c + 1) * chunk]
        s1 = jnp.sum(jnp.where(re == blk[0:1], start_col, 0), 0, keepdims=True)
        s2 = jnp.sum(jnp.where(re == blk[1:2], start_col, 0), 0, keepdims=True)
        pos_ref[:, c * chunk:(c + 1) * chunk] = jnp.where(
            r8 == 0, s1 + blk[2:3], jnp.where(r8 == 1, s2 + blk[3:4], 0))
    width = meta_ref.shape[1]
    tile_id = lax.broadcasted_iota(I32, (ne, width), 1).astype(F32)
    te = jnp.sum(jnp.where(ends[:, 0:1] <= tile_id, 1, 0), 0, keepdims=True)
    te = jnp.minimum(te, ne - 1)
    expert = lax.broadcasted_iota(I32, (ne, width), 0)
    later = jnp.logical_and(expert > te, tiles[:, 0:1] > 0.0)
    nxt = jnp.min(jnp.where(later, expert, ne), 0, keepdims=True)
    nxt = jnp.where(nxt < ne, nxt + layer * ne, -1)
    n_used = ends[ne - 1:ne, 0:1].astype(I32)
    rm = lax.broadcasted_iota(I32, (8, width), 0)
    meta_ref[...] = jnp.where(rm == 0, te + layer * ne,
                              jnp.where(rm == 1, n_used, jnp.where(rm == 2, nxt, 0)))


def _plan(ints, cnt, layer, n_tiles):
    n = ints.shape[1]
    width = -(-n_tiles // LANES) * LANES
    return pl.pallas_call(
        functools.partial(_plan_kernel, layer=layer, chunk=2048),
        out_shape=[jax.ShapeDtypeStruct((8, n), I32), jax.ShapeDtypeStruct((8, width), I32)],
        compiler_params=pltpu.CompilerParams(vmem_limit_bytes=VMEM_LIMIT),
        name="moe_plan",
    )(ints, cnt)


def _moe_layer(h, hp, layer, group_w, group_b, expert_w, expert_b, gate_w, up_w, down_w, ln_g, ln_b,
               tiled_copy):
    n = h.shape[0]
    ew = jnp.transpose(expert_w, (1, 0, 2)).reshape(D_MODEL, MOE_EXPERTS)
    w_r = jnp.zeros((D_MODEL, LANES), F32).at[:, 0:MOE_GROUPS].set(group_w).at[:, 8:8 + MOE_EXPERTS].set(ew)
    b_r = jnp.zeros((LANES,), F32).at[0:MOE_GROUPS].set(group_b).at[8:8 + MOE_EXPERTS].set(expert_b.reshape(-1))
    ints, flts, cnt = _router(h, w_r, b_r.reshape(LANES, 1), 512)
    tr = MOE_ROW_TILE
    n_tiles = (2 * n) // tr + MOE_EXPERTS
    pos, meta = _plan(ints, cnt, layer, n_tiles)
    pos1, pos2 = pos[0], pos[1]

    y = _experts(hp, pos1, pos2, meta[0, :n_tiles], meta[2, :n_tiles], meta[1, :1],
                 gate_w.reshape(-1, D_MODEL, MOE_D_FF),
                 up_w.reshape(-1, D_MODEL, MOE_D_FF),
                 down_w.reshape(-1, MOE_D_FF, D_MODEL))
    return _combine(y, h, pos1, pos2, flts,
                    ln_g.reshape(1, -1), ln_b.reshape(1, -1), COMBINE_TILE, tiled_copy)


def _pad_heads(w, axis):
    parts = []
    h0 = 0
    for _, _, nh in ATTN_GROUPS:
        sl = [slice(None)] * w.ndim
        sl[axis] = slice(h0 * ATTN_HEAD_DIM, (h0 + nh) * ATTN_HEAD_DIM)
        part = w[tuple(sl)]
        pad = [(0, 0)] * w.ndim
        pad[axis] = (0, ATTN_SLAB - nh * ATTN_HEAD_DIM)
        parts.append(jnp.pad(part, pad))
        h0 += nh
    return parts


def _in_proj_kernel(x_ref, w_ref, wdt_ref, z_ref, xbc_ref, dt_ref, *, z_tiles):
    j = pl.program_id(1)
    acc = _dot(x_ref[...].astype(BF16), w_ref[...])

    @pl.when(j < z_tiles)
    def _():
        z_ref[...] = acc

    @pl.when(j >= z_tiles)
    def _():
        xbc_ref[...] = acc

    @pl.when(j == 0)
    def _():
        dt_ref[...] = _dot_f32ish(x_ref[...], wdt_ref[...])


def _in_proj(x, w_zx, w_dt, tm, tn):
    m, k = x.shape
    z_tiles = SSM_D_INNER // tn
    n_tiles = (SSM_D_INNER + SSM_CONV_DIM) // tn
    return pl.pallas_call(
        functools.partial(_in_proj_kernel, z_tiles=z_tiles),
        grid=(m // tm, n_tiles),
        in_specs=[pl.BlockSpec((tm, k), lambda i, j: (i, 0)),
                  pl.BlockSpec((k, tn), lambda i, j: (0, j)),
                  pl.BlockSpec((k, LANES), lambda i, j: (0, 0))],
        out_specs=[pl.BlockSpec((tm, tn), lambda i, j: (i, jnp.minimum(j, z_tiles - 1))),
                   pl.BlockSpec((tm, tn), lambda i, j: (i, jnp.maximum(j - z_tiles, 0))),
                   pl.BlockSpec((tm, LANES), lambda i, j: (i, 0))],
        out_shape=[jax.ShapeDtypeStruct((m, SSM_D_INNER), F32),
                   jax.ShapeDtypeStruct((m, SSM_CONV_DIM), F32),
                   jax.ShapeDtypeStruct((m, LANES), F32)],
        compiler_params=_cparams("parallel", "arbitrary"),
        name="ssm_in_proj",
    )(x, w_zx, w_dt)


def _ssd_layer(h, in_w, conv_w, conv_b, dt_bias, a_log, d_skip, norm_w, out_w, ln_g, ln_b, bsz, seq):
    split = SSM_D_INNER + SSM_CONV_DIM
    dt_w = jnp.pad(in_w[:, split:], ((0, 0), (0, LANES - SSM_HEADS)))
    z, xbc, dt_raw = _in_proj(h, in_w.astype(BF16), dt_w, 1024, 1024)
    pad32 = lambda v: jnp.pad(v, (0, LANES - SSM_HEADS)).reshape(1, LANES)
    y = _ssd(z, xbc, dt_raw, conv_w, conv_b.reshape(1, -1), pad32(dt_bias), pad32(a_log),
             jnp.repeat(d_skip, SSM_HEAD_DIM).reshape(1, -1), norm_w.reshape(1, -1), bsz, seq)
    return _matmul_res_ln(y, out_w.astype(BF16), h, ln_g.reshape(1, -1), ln_b.reshape(1, -1), 1024)


def _qkv_group_weights(kv_w, q_w):
    width = ATTN_HEADS * ATTN_HEAD_DIM
    w_q = _pad_heads(q_w * (ATTN_HEAD_DIM ** -0.5), 1)
    w_k = _pad_heads(kv_w[:, :width], 1)
    w_v = _pad_heads(kv_w[:, width:], 1)
    return [jnp.concatenate([w_q[gi], w_k[gi], w_v[gi]], axis=1).astype(BF16) for gi in range(len(ATTN_GROUPS))]


def _attn_layer(h, h8, qkv_w, o_w, rel_bias, ln_g, ln_b, bsz, seq):
    outs, lses = [], []
    h0 = 0
    for gi, (_, dil, nh) in enumerate(ATTN_GROUPS):
        if dil == 1:
            qkv = _matmul(h, qkv_w[gi], BF16, 1024, 3 * ATTN_SLAB)
        else:
            qkv = _qkv_dilated(h8, qkv_w[gi], dil, max(1024, ATTN_BLOCK * dil))
        bias_p, bias_c = _group_bias(rel_bias, h0, nh, dil)
        o, lse = _attention_group(qkv, bias_p, bias_c, gi, dil, nh, bsz, seq)
        outs.extend(o)
        lses.append(lse)
        h0 += nh
    w_o = jnp.concatenate(_pad_heads(o_w, 0), axis=0).astype(BF16)
    return _attn_out(outs, lses, w_o, h, ln_g.reshape(1, -1), ln_b.reshape(1, -1), 1024)


def kernel(x, ssm_in_w, ssm_conv_w, ssm_conv_b, ssm_dt_bias, ssm_a_log, ssm_d, ssm_norm_w, ssm_out_w,
           kv_w, attn_q_w, attn_o_w, rel_bias, moe_group_w, moe_group_b, moe_expert_w, moe_expert_b,
           moe_gate_w, moe_up_w, moe_down_w, ln_g, ln_b):
    bsz, seq, d = x.shape
    h = x.reshape(bsz * seq, d)
    n_ssd = DEPTH // 2
    h8 = None
    for i in range(DEPTH):
        if i < n_ssd:
            h, hp = _ssd_layer(h, ssm_in_w[i], ssm_conv_w[i], ssm_conv_b[i], ssm_dt_bias[i], ssm_a_log[i],
                               ssm_d[i], ssm_norm_w[i], ssm_out_w[i], ln_g[i, 0], ln_b[i, 0], bsz, seq)
        else:
            j = i - n_ssd
            h, hp = _attn_layer(h, h8, _qkv_group_weights(kv_w, attn_q_w[j]), attn_o_w[j], rel_bias,
                                ln_g[i, 0], ln_b[i, 0], bsz, seq)
        feeds_attention = n_ssd <= i + 1 < DEPTH
        res = _moe_layer(h, hp, i, moe_group_w[i], moe_group_b[i], moe_expert_w[i], moe_expert_b[i],
                         moe_gate_w, moe_up_w, moe_down_w, ln_g[i, 1], ln_b[i, 1], feeds_attention)
        h = res[0]
        h8 = res[1] if feeds_attention else None
    return h.reshape(bsz, seq, d)
```

```python
import functools
import math

import numpy as np
import jax
import jax.numpy as jnp
from jax import lax
from jax.experimental import pallas as pl
from jax.experimental.pallas import tpu as pltpu

F32 = jnp.float32
BF16 = jnp.bfloat16
I32 = jnp.int32

D_MODEL = 1024
DEPTH = 2
DN_ALPHA = (2 * DEPTH) ** 0.25
LN_EPS = 1e-5
LOG2_E = math.log2(math.e)

SSM_D_INNER = 2048
SSM_HEAD_DIM = 64
SSM_HEADS = 32
SSM_GROUPS = 4
SSM_STATE = 128
SSM_CONV = 4
SSM_CHUNK = 128
SSM_CONV_DIM = SSM_D_INNER + 2 * SSM_GROUPS * SSM_STATE

ATTN_HEAD_DIM = 64
ATTN_GROUPS = ((128, 1, 6), (512, 4, 5), (2048, 16, 5))
ATTN_HEADS = 16
ATTN_BLOCK = 128
N_BUCKETS = 32
MAX_DISTANCE = 2048
ATTN_SLAB = 384
NEG_BIG = -1e30

MOE_GROUPS = 4
MOE_EPG = 8
MOE_EXPERTS = MOE_GROUPS * MOE_EPG
MOE_D_FF = 512
MOE_ROW_TILE = 256
COMBINE_TILE = 512

LANES = 128
DMA_QUEUES = 2
VMEM_LIMIT = 48 * 1024 * 1024
EXPERTS_VMEM_LIMIT = 56 * 1024 * 1024
IN_PROJ_VMEM_LIMIT = 56 * 1024 * 1024


def _cparams(*sem):
    return pltpu.CompilerParams(dimension_semantics=sem, vmem_limit_bytes=VMEM_LIMIT)


def _layer_norm(x, g, b):
    mu = jnp.mean(x, -1, keepdims=True)
    xc = x - mu
    var = jnp.mean(xc * xc, -1, keepdims=True)
    return xc * lax.rsqrt(var + LN_EPS) * g + b


def _split2(x):
    hi = x.astype(BF16)
    lo = (x - hi.astype(F32)).astype(BF16)
    return hi, lo


def _dot(a, b):
    return jnp.dot(a, b, preferred_element_type=F32)


def _dot_f32ish(a, b):
    ah, al = _split2(a)
    bh, bl = _split2(b)
    return _dot(ah, bh) + _dot(al, bh) + _dot(ah, bl)


def _mm_kernel(a_ref, b_ref, o_ref):
    o_ref[...] = _dot(a_ref[...].astype(BF16), b_ref[...]).astype(o_ref.dtype)


def _matmul(a, b, out_dtype, tm, tn):
    m, k = a.shape
    nc = b.shape[1]
    return pl.pallas_call(
        _mm_kernel,
        grid=(m // tm, nc // tn),
        in_specs=[pl.BlockSpec((tm, k), lambda i, j: (i, 0)),
                  pl.BlockSpec((k, tn), lambda i, j: (0, j))],
        out_specs=pl.BlockSpec((tm, tn), lambda i, j: (i, j)),
        out_shape=jax.ShapeDtypeStruct((m, nc), out_dtype),
        compiler_params=_cparams("parallel", "parallel"),
        name="matmul",
    )(a, b)


def _mm3_kernel(a_ref, b_ref, o_ref):
    o_ref[...] = _dot_f32ish(a_ref[...], b_ref[...])


def _matmul_f32ish(a, b, tm):
    m, k = a.shape
    nc = b.shape[1]
    return pl.pallas_call(
        _mm3_kernel,
        grid=(m // tm,),
        in_specs=[pl.BlockSpec((tm, k), lambda i: (i, 0)),
                  pl.BlockSpec((k, nc), lambda i: (0, 0))],
        out_specs=pl.BlockSpec((tm, nc), lambda i: (i, 0)),
        out_shape=jax.ShapeDtypeStruct((m, nc), F32),
        compiler_params=_cparams("parallel"),
        name="matmul_f32ish",
    )(a, b)


ROW_SPLIT = D_MODEL // LANES


def _store_row_tiled(ref, val):
    rows = val.shape[0]
    for c in range(ROW_SPLIT):
        ref[pl.ds(c, rows, stride=ROW_SPLIT), :] = val[:, c * LANES:(c + 1) * LANES]


def _load_row_tiled(ref):
    rows = ref.shape[0] // ROW_SPLIT
    return jnp.concatenate([ref[pl.ds(c, rows, stride=ROW_SPLIT), :] for c in range(ROW_SPLIT)], axis=1)


HALF = D_MODEL // 2
U32 = jnp.uint32
HI16 = 0xFFFF0000


def _pack_bf16_pairs(x):
    bits = lax.bitcast_convert_type(x.astype(BF16).astype(F32), U32)
    return (bits[:, :HALF] >> 16) | (bits[:, HALF:] & U32(HI16))


def _unpack_bf16_pairs(w):
    lo = lax.bitcast_convert_type(w << 16, F32).astype(BF16)
    hi = lax.bitcast_convert_type(w & U32(HI16), F32).astype(BF16)
    return lo, hi


def _mm_ln_kernel(a_ref, w_ref, h_ref, g_ref, b_ref, o_ref, op_ref):
    acc = _dot(a_ref[...], w_ref[...])
    out = _layer_norm(DN_ALPHA * h_ref[...] + acc, g_ref[...], b_ref[...])
    o_ref[...] = out
    op_ref[...] = _pack_bf16_pairs(out)


def _matmul_res_ln(a, w, h, g, b, tm):
    m, k = a.shape
    d = w.shape[1]
    return pl.pallas_call(
        _mm_ln_kernel,
        grid=(m // tm,),
        in_specs=[pl.BlockSpec((tm, k), lambda i: (i, 0)),
                  pl.BlockSpec((k, d), lambda i: (0, 0)),
                  pl.BlockSpec((tm, d), lambda i: (i, 0)),
                  pl.BlockSpec((1, d), lambda i: (0, 0)),
                  pl.BlockSpec((1, d), lambda i: (0, 0))],
        out_specs=[pl.BlockSpec((tm, d), lambda i: (i, 0)),
                   pl.BlockSpec((tm, HALF), lambda i: (i, 0))],
        out_shape=[jax.ShapeDtypeStruct((m, d), F32),
                   jax.ShapeDtypeStruct((m, HALF), U32)],
        compiler_params=_cparams("parallel"),
        name="matmul_res_ln",
    )(a, w, h, g, b)


SSD_CHUNKS_PER_STEP = 2


def _ssd_kernel(z_ref, xbc_ref, dt_ref, cw_ref, cb_ref, dtb_ref, alog_ref, dsk_ref, nw_ref,
                y_ref, xe_ref, st_ref):
    @pl.when(pl.program_id(1) == 0)
    def _():
        xe_ref[:, 0:8, :] = jnp.zeros((xe_ref.shape[0], 8, LANES), F32)
        st_ref[...] = jnp.zeros_like(st_ref)

    for sub in range(SSD_CHUNKS_PER_STEP):
        _ssd_chunk(pl.ds(sub * SSM_CHUNK, SSM_CHUNK), z_ref, xbc_ref, dt_ref, cw_ref, cb_ref, dtb_ref,
                   alog_ref, dsk_ref, nw_ref, y_ref, xe_ref, st_ref)


def _ssd_chunk(rows, z_ref, xbc_ref, dt_ref, cw_ref, cb_ref, dtb_ref, alog_ref, dsk_ref, nw_ref,
               y_ref, xe_ref, st_ref):
    q = SSM_CHUNK
    w = cw_ref[...]
    bias = cb_ref[...]
    act = []
    for c in range(SSM_CONV_DIM // LANES):
        cols = slice(c * LANES, (c + 1) * LANES)
        u = xbc_ref[rows, cols].astype(F32)
        xe_ref[c, 8:8 + q, :] = u
        conv = (bias[:, cols] + w[3:4, cols] * u + w[2:3, cols] * xe_ref[c, 7:7 + q, :]
                + w[1:2, cols] * xe_ref[c, 6:6 + q, :] + w[0:1, cols] * xe_ref[c, 5:5 + q, :])
        xe_ref[c, 0:8, :] = xe_ref[c, q:q + 8, :]
        act.append(conv * jax.nn.sigmoid(conv))

    pre = dt_ref[rows, :] + dtb_ref[...]
    dt = jnp.maximum(pre, 0.0) + jnp.log(1.0 + jnp.exp(-jnp.abs(pre)))
    adt = dt * (-jnp.exp(alog_ref[...]) * LOG2_E)

    row = lax.broadcasted_iota(I32, (q, q), 0)
    col = lax.broadcasted_iota(I32, (q, q), 1)
    tril = row >= col
    tri_b = jnp.where(tril, 1.0, 0.0).astype(BF16)
    a_hi = adt.astype(BF16)
    r1 = adt - a_hi.astype(F32)
    a_mid = r1.astype(BF16)
    a_lo = (r1 - a_mid.astype(F32)).astype(BF16)
    acs = _dot(tri_b, a_hi) + _dot(tri_b, a_mid) + _dot(tri_b, a_lo)
    acs_t = acs.T
    acs_dt_t = acs_t - jnp.log(dt.T) * LOG2_E
    eacs = jnp.exp2(acs)
    left = col < SSM_HEAD_DIM

    for g in range(SSM_GROUPS):
        b0 = SSM_D_INNER + g * SSM_STATE
        c0 = SSM_D_INNER + SSM_GROUPS * SSM_STATE + g * SSM_STATE
        bm = act[b0 // LANES]
        cm = act[c0 // LANES].astype(BF16)
        cb = lax.dot_general(cm, bm.astype(BF16), (((1,), (1,)), ((), ())),
                             preferred_element_type=F32)
        bm_t = bm.T
        gs = g * 512
        y_off = _dot(cm, st_ref[:, gs:gs + 512].astype(BF16))
        slabs = []
        for pr in range(4):
            ha = g * 8 + pr * 2
            hb = ha + 1
            cs = gs + pr * LANES
            x2 = act[cs // LANES]
            x2b = x2.astype(BF16)
            ys, ups = [], []
            for h in (ha, hb):
                a_col = acs[:, h:h + 1]
                a_src = acs_dt_t[h:h + 1, :]
                decay = jnp.where(tril, jnp.exp2(a_col - a_src), 0.0)
                ys.append(_dot((cb * decay).astype(BF16), x2b))
                to_end = jnp.exp2(acs_t[h:h + 1, q - 1:q] - a_src)
                ups.append(_dot((bm_t * to_end).astype(BF16), x2b))
            y_diag = jnp.where(left, ys[0], ys[1])
            upd = jnp.where(left, ups[0], ups[1])
            e2 = jnp.where(left, eacs[:, ha:ha + 1], eacs[:, hb:hb + 1])
            cd = jnp.where(left[0:1, :], eacs[q - 1:q, ha:ha + 1], eacs[q - 1:q, hb:hb + 1])
            y2 = y_diag + y_off[:, pr * LANES:(pr + 1) * LANES] * e2 + dsk_ref[:, cs:cs + LANES] * x2
            st_ref[:, cs:cs + LANES] = st_ref[:, cs:cs + LANES] * cd + upd
            slabs.append(y2)
        yg = jnp.concatenate(slabs, axis=1)
        zg = z_ref[rows, gs:gs + 512].astype(F32)
        yg = yg * (zg * jax.nn.sigmoid(zg))
        ms = jnp.mean(yg * yg, -1, keepdims=True)
        y_ref[rows, gs:gs + 512] = (yg * lax.rsqrt(ms + LN_EPS) * nw_ref[:, gs:gs + 512]).astype(y_ref.dtype)


def _ssd(z, xbc, dt_raw, conv_w, conv_b, dt_bias, a_log, d_rep, norm_w, bsz, seq):
    n = z.shape[0]
    q = SSM_CHUNK * SSD_CHUNKS_PER_STEP
    nchunk = seq // q
    tok = lambda b, c: (b * nchunk + c, 0)
    fixed = lambda b, c: (0, 0)
    return pl.pallas_call(
        _ssd_kernel,
        grid=(bsz, nchunk),
        in_specs=[pl.BlockSpec((q, SSM_D_INNER), tok),
                  pl.BlockSpec((q, SSM_CONV_DIM), tok),
                  pl.BlockSpec((q, LANES), tok),
                  pl.BlockSpec((SSM_CONV, SSM_CONV_DIM), fixed),
                  pl.BlockSpec((1, SSM_CONV_DIM), fixed),
                  pl.BlockSpec((1, LANES), fixed),
                  pl.BlockSpec((1, LANES), fixed),
                  pl.BlockSpec((1, SSM_D_INNER), fixed),
                  pl.BlockSpec((1, SSM_D_INNER), fixed)],
        out_specs=pl.BlockSpec((q, SSM_D_INNER), tok),
        out_shape=jax.ShapeDtypeStruct((n, SSM_D_INNER), BF16),
        scratch_shapes=[pltpu.VMEM((SSM_CONV_DIM // LANES, SSM_CHUNK + 8, LANES), F32),
                        pltpu.VMEM((SSM_STATE, SSM_D_INNER), F32)],
        compiler_params=_cparams("parallel", "arbitrary"),
        name="ssd_chunk",
    )(z, xbc, dt_raw, conv_w, conv_b, dt_bias, a_log, d_rep, norm_w)


def _residue_major_pieces(h8_ref, n_tokens, dil):
    span = ATTN_BLOCK * dil
    pieces = []
    for blk in range(n_tokens // span):
        for r in range(dil):
            first = (blk * span + r) * ROW_SPLIT
            pieces.append(jnp.concatenate(
                [h8_ref[pl.ds(first + c, ATTN_BLOCK, stride=ROW_SPLIT * dil), :] for c in range(ROW_SPLIT)],
                axis=1).astype(BF16))
    return pieces


def _qkv_dilated_kernel(h8_ref, w_ref, o_ref, *, dil):
    per_dot = 4
    rows = _residue_major_pieces(h8_ref, o_ref.shape[0], dil)
    for k in range(0, len(rows), per_dot):
        x = jnp.concatenate(rows[k:k + per_dot], axis=0)
        o_ref[k * ATTN_BLOCK:(k + per_dot) * ATTN_BLOCK, :] = _dot(x, w_ref[...]).astype(o_ref.dtype)


def _qkv_dilated(h8, w, dil, tm):
    n = h8.shape[0] // ROW_SPLIT
    nc = w.shape[1]
    return pl.pallas_call(
        functools.partial(_qkv_dilated_kernel, dil=dil),
        grid=(n // tm,),
        in_specs=[pl.BlockSpec((tm * ROW_SPLIT, LANES), lambda i: (i, 0)),
                  pl.BlockSpec((D_MODEL, nc), lambda i: (0, 0))],
        out_specs=pl.BlockSpec((tm, nc), lambda i: (i, 0)),
        out_shape=jax.ShapeDtypeStruct((n, nc), BF16),
        compiler_params=_cparams("parallel"),
        name=f"qkv_dil{dil}",
    )(h8, w)


def _attn_kernel(*refs, nh, dil, has_prev, blocks):
    span = ATTN_BLOCK * dil
    if has_prev:
        q_ref, kp_ref, kc_ref, vp_ref, vc_ref, bp_ref, bc_ref = refs[:7]
        out_refs = refs[7:]
        first_pen = jnp.where(pl.program_id(1) > 0, 0.0, NEG_BIG)
    else:
        q_ref, kc_ref, vc_ref, bc_ref = refs[:4]
        out_refs = refs[4:]
    s_scr, p_scr, max_scr, den_scr = out_refs[-4:]
    o_refs, lse_ref = out_refs[:-5], out_refs[-5]
    ones = jnp.ones((s_scr.shape[2], LANES), BF16)
    lane = lax.broadcasted_iota(I32, (ATTN_BLOCK, LANES), 1)
    left = lane < ATTN_HEAD_DIM
    zero = jnp.zeros((), BF16)
    nt = (((1,), (1,)), ((), ()))

    def residue(blk, r):
        rows = pl.ds(pl.multiple_of(blk * span + r * ATTN_BLOCK, ATTN_BLOCK), ATTN_BLOCK)
        dst = pl.ds(blk * span + r, ATTN_BLOCK, stride=dil)
        if has_prev and blk == 0:
            kprev, vprev, pen = kp_ref, vp_ref, first_pen
            prows = pl.ds(pl.multiple_of(r * ATTN_BLOCK, ATTN_BLOCK), ATTN_BLOCK)
        elif has_prev:
            kprev, vprev, pen = kc_ref, vc_ref, 0.0
            prows = pl.ds(pl.multiple_of((blk - 1) * span + r * ATTN_BLOCK, ATTN_BLOCK), ATTN_BLOCK)
        for hh in range(nh):
            cols = pl.ds(hh // 2 * LANES, LANES)
            qm = jnp.where(left if hh % 2 == 0 else ~left, q_ref[rows, cols], zero)
            s_c = lax.dot_general(qm, kc_ref[rows, cols], nt, preferred_element_type=F32) + bc_ref[hh]
            if has_prev:
                s_p = lax.dot_general(qm, kprev[prows, cols], nt, preferred_element_type=F32)
                s_scr[hh, :, 0:ATTN_BLOCK] = s_p + (bp_ref[hh] + pen)
                s_scr[hh, :, ATTN_BLOCK:] = s_c
            else:
                s_scr[hh] = s_c
        for hh in range(nh):
            s = s_scr[hh]
            m = jnp.max(s, -1, keepdims=True)
            p = jnp.exp(s - m)
            p_scr[hh] = p.astype(BF16)
            max_scr[hh] = m
            if not has_prev:
                den_scr[hh] = jnp.sum(p, -1, keepdims=True)
        lse_sum = jnp.zeros((ATTN_BLOCK, 1), F32)
        for pr in range(ATTN_SLAB // LANES):
            cols = pl.ds(pr * LANES, LANES)
            halves = []
            for hh in (pr * 2, pr * 2 + 1):
                if hh >= nh:
                    halves.append(jnp.zeros((ATTN_BLOCK, LANES), F32))
                    continue
                if has_prev:
                    den = _dot(p_scr[hh], ones)
                    den1 = den[:, 0:1]
                    o = (_dot(p_scr[hh, :, 0:ATTN_BLOCK], vprev[prows, cols])
                         + _dot(p_scr[hh, :, ATTN_BLOCK:], vc_ref[rows, cols]))
                else:
                    den = den1 = den_scr[hh]
                    o = _dot(p_scr[hh], vc_ref[rows, cols])
                halves.append(o * (1.0 / den))
                lse_sum = lse_sum + (max_scr[hh] + jnp.log(den1))
            o_refs[pr][dst, :] = jnp.where(left, halves[0], halves[1])
        lse_ref[dst, :] = jnp.broadcast_to(lse_sum * (1.0 / nh), (ATTN_BLOCK, LANES))

    for blk in range(blocks):
        def body(r, carry, blk=blk):
            residue(blk, r)
            return carry

        lax.fori_loop(0, dil, body, 0)


def _attention_group(qkv, bias_p, bias_c, gi, dil, nh, bsz, seq):
    span = ATTN_BLOCK * dil
    nb = seq // span
    has_prev = nb > 1
    blocks = max(1, 512 // span)
    steps = nb // blocks
    blk = (blocks * span, ATTN_SLAB)
    cur = lambda which: (lambda b, n: (b * steps + n, which))
    prev = lambda which: (lambda b, n: (b * nb + jnp.maximum(n * blocks - 1, 0), which))
    fixed = lambda b, n: (0, 0, 0)
    tab = pl.BlockSpec((nh, ATTN_BLOCK, ATTN_BLOCK), fixed)
    if has_prev:
        pblk = (span, ATTN_SLAB)
        in_specs = [pl.BlockSpec(blk, cur(0)), pl.BlockSpec(pblk, prev(1)), pl.BlockSpec(blk, cur(1)),
                    pl.BlockSpec(pblk, prev(2)), pl.BlockSpec(blk, cur(2)), tab, tab]
        args = (qkv, qkv, qkv, qkv, qkv, bias_p, bias_c)
    else:
        in_specs = [pl.BlockSpec(blk, cur(0)), pl.BlockSpec(blk, cur(1)), pl.BlockSpec(blk, cur(2)), tab]
        args = (qkv, qkv, qkv, bias_c)
    n_out = ATTN_SLAB // LANES + 1
    keys = 2 * ATTN_BLOCK if has_prev else ATTN_BLOCK
    outs = pl.pallas_call(
        functools.partial(_attn_kernel, nh=nh, dil=dil, has_prev=has_prev, blocks=blocks),
        grid=(bsz, steps),
        in_specs=in_specs,
        out_specs=[pl.BlockSpec((blocks * span, LANES), lambda b, n: (b * steps + n, 0))] * n_out,
        out_shape=[jax.ShapeDtypeStruct((bsz * seq, LANES), F32)] * n_out,
        scratch_shapes=[pltpu.VMEM((nh, ATTN_BLOCK, keys), F32),
                        pltpu.VMEM((nh, ATTN_BLOCK, keys), BF16),
                        pltpu.VMEM((nh, ATTN_BLOCK, 1), F32),
                        pltpu.VMEM((nh, ATTN_BLOCK, 1), F32)],
        compiler_params=_cparams("parallel", "parallel"),
        name=f"dilated_attn_g{gi}",
    )(*args)
    return outs[:-1], outs[-1]


def _attn_out_kernel(*refs):
    n_pairs = ATTN_SLAB // LANES
    n_groups = len(ATTN_GROUPS)
    o_refs = refs[:n_groups * n_pairs]
    l_refs = refs[n_groups * n_pairs:n_groups * (n_pairs + 1)]
    w_ref, h_ref, g_ref, b_ref, out_ref, outp_ref = refs[n_groups * (n_pairs + 1):]
    ls = [r[:, 0:1] for r in l_refs]
    m = jnp.maximum(jnp.maximum(ls[0], ls[1]), ls[2])
    es = [jnp.exp(v - m) for v in ls]
    scale = n_groups / (es[0] + es[1] + es[2])
    slabs = []
    for gi in range(n_groups):
        wt = es[gi] * scale
        slabs.extend((r[...] * wt).astype(BF16) for r in o_refs[gi * n_pairs:(gi + 1) * n_pairs])
    acc = _dot(jnp.concatenate(slabs, axis=1), w_ref[...])
    out = _layer_norm(DN_ALPHA * h_ref[...] + acc, g_ref[...], b_ref[...])
    out_ref[...] = out
    outp_ref[...] = _pack_bf16_pairs(out)


def _attn_out(outs, lses, w, h, g, b, tm):
    n = h.shape[0]
    tok = lambda i: (i, 0)
    fixed2 = lambda i: (0, 0)
    return pl.pallas_call(
        _attn_out_kernel,
        grid=(n // tm,),
        in_specs=[pl.BlockSpec((tm, LANES), tok)] * (len(outs) + len(lses))
                 + [pl.BlockSpec((len(ATTN_GROUPS) * ATTN_SLAB, D_MODEL), lambda i: (0, 0)),
                    pl.BlockSpec((tm, D_MODEL), tok),
                    pl.BlockSpec((1, D_MODEL), fixed2), pl.BlockSpec((1, D_MODEL), fixed2)],
        out_specs=[pl.BlockSpec((tm, D_MODEL), tok), pl.BlockSpec((tm, HALF), tok)],
        out_shape=[jax.ShapeDtypeStruct((n, D_MODEL), F32),
                   jax.ShapeDtypeStruct((n, HALF), U32)],
        compiler_params=_cparams("parallel"),
        name="attn_out_ln",
    )(*outs, *lses, w, h, g, b)


def _t5_bucket(dist):
    max_exact = N_BUCKETS // 2
    n = np.maximum(dist, 1).astype(np.float64)
    large = max_exact + (np.log(n / max_exact) / np.log(MAX_DISTANCE / max_exact)
                         * (N_BUCKETS - max_exact)).astype(np.int32)
    large = np.minimum(large, N_BUCKETS - 1)
    return np.where(dist < max_exact, dist, large).astype(np.int32)


def _group_bias(rel_bias, h0, nh, dil):
    qi = np.arange(ATTN_BLOCK)[:, None]
    ki = np.arange(ATTN_BLOCK)[None, :]
    tabs = []
    for delta, band in ((qi + ATTN_BLOCK - ki, ki >= qi), (qi - ki, ki <= qi)):
        bucket = _t5_bucket(np.clip(delta, 0, None) * dil)
        onehot = (bucket[..., None] == np.arange(N_BUCKETS)).astype(np.float32)
        t = jnp.einsum("qkb,bh->hqk", onehot, rel_bias[:, h0:h0 + nh], precision=lax.Precision.HIGHEST)
        tabs.append(jnp.where(band[None], t, NEG_BIG).astype(F32))
    return tabs


def _router_kernel(h_ref, w_ref, b_ref, ints_ref, flts_ref, cnt_ref, carry_ref, before_ref, *, tm):
    @pl.when(pl.program_id(0) == 0)
    def _():
        carry_ref[...] = jnp.zeros_like(carry_ref)
        tr = lax.broadcasted_iota(I32, (tm, tm), 0)
        tc = lax.broadcasted_iota(I32, (tm, tm), 1)
        before_ref[...] = jnp.where(tr < tc, 1.0, 0.0).astype(BF16)

    lt = _dot_f32ish(h_ref[...], w_ref[...]).T + b_ref[...]
    gl = lt[0:MOE_GROUPS]
    r4 = lax.broadcasted_iota(I32, (MOE_GROUPS, tm), 0)
    gmax = jnp.max(gl, 0, keepdims=True)
    gidx = jnp.min(jnp.where(gl == gmax, r4, MOE_GROUPS), 0, keepdims=True)
    gval = 1.0 / jnp.sum(jnp.exp(gl - gmax), 0, keepdims=True)

    esel = jnp.zeros((MOE_EPG, tm), F32)
    for g in range(MOE_GROUPS):
        esel = jnp.where(gidx == g, lt[8 + g * MOE_EPG:8 + (g + 1) * MOE_EPG], esel)
    r8 = lax.broadcasted_iota(I32, (MOE_EPG, tm), 0)
    v1 = jnp.max(esel, 0, keepdims=True)
    i1 = jnp.min(jnp.where(esel == v1, r8, MOE_EPG), 0, keepdims=True)
    rest = jnp.where(r8 == i1, -jnp.inf, esel)
    v2 = jnp.max(rest, 0, keepdims=True)
    i2 = jnp.min(jnp.where(rest == v2, r8, MOE_EPG), 0, keepdims=True)
    t = jnp.exp(v2 - v1)
    p1 = gval / (1.0 + t)
    p2 = p1 * t
    e1 = gidx * MOE_EPG + i1
    e2 = gidx * MOE_EPG + i2

    r32 = lax.broadcasted_iota(I32, (MOE_EXPERTS, tm), 0)
    oh1 = r32 == e1
    oh2 = r32 == e2
    oh = jnp.where(oh1 | oh2, 1.0, 0.0)
    base = _dot(oh.astype(BF16), before_ref[...]) + carry_ref[:, 0:1]
    rank1 = jnp.sum(jnp.where(oh1, base, 0.0), 0, keepdims=True).astype(I32)
    rank2 = jnp.sum(jnp.where(oh2, base, 0.0), 0, keepdims=True).astype(I32)
    carry_ref[...] = carry_ref[...] + jnp.sum(oh, 1, keepdims=True)
    cnt_ref[...] = carry_ref[...]

    ints_ref[...] = jnp.where(r8 == 0, e1, jnp.where(r8 == 1, e2, jnp.where(r8 == 2, rank1,
                              jnp.where(r8 == 3, rank2, 0))))
    r128 = lax.broadcasted_iota(I32, (LANES, tm), 0)
    flts_ref[...] = jnp.where(r128 == 0, p1, jnp.where(r128 == 1, p2, 0.0)).T


def _router(h, w_r, b_r, tm):
    n = h.shape[0]
    return pl.pallas_call(
        functools.partial(_router_kernel, tm=tm),
        grid=(n // tm,),
        in_specs=[pl.BlockSpec((tm, D_MODEL), lambda i: (i, 0)),
                  pl.BlockSpec((D_MODEL, LANES), lambda i: (0, 0)),
                  pl.BlockSpec((LANES, 1), lambda i: (0, 0))],
        out_specs=[pl.BlockSpec((8, tm), lambda i: (0, i)),
                   pl.BlockSpec((tm, LANES), lambda i: (i, 0)),
                   pl.BlockSpec((MOE_EXPERTS, LANES), lambda i: (0, 0))],
        out_shape=[jax.ShapeDtypeStruct((8, n), I32),
                   jax.ShapeDtypeStruct((n, LANES), F32),
                   jax.ShapeDtypeStruct((MOE_EXPERTS, LANES), F32)],
        scratch_shapes=[pltpu.VMEM((MOE_EXPERTS, LANES), F32), pltpu.VMEM((tm, tm), BF16)],
        compiler_params=_cparams("arbitrary"),
        name="moe_router",
    )(h, w_r, b_r)


def _row_copy(src_ref, s, dst_ref, d, sem):
    def first(r):
        return r * ROW_SPLIT if isinstance(r, int) else pl.multiple_of(r * ROW_SPLIT, ROW_SPLIT)

    return pltpu.make_async_copy(src_ref.at[pl.ds(first(s), ROW_SPLIT), :],
                                 dst_ref.at[pl.ds(first(d), ROW_SPLIT), :], sem)


def _tile_wait(src_ref, dst_ref, sem):
    pltpu.make_async_copy(src_ref.at[pl.ds(0, dst_ref.shape[0]), :], dst_ref, sem).wait()


def _experts_kernel(pos1_ref, pos2_ref, te_ref, nxt_ref, na_ref, hp_hbm, zeros_ref, wg_ref, wu_ref, wd_ref,
                    y_ref, hp_ref, xa, xb, wg_f, wu_f, wd_f, wg_s, wu_s, wd_s, row_tok, slot_ref,
                    sem, hsem, wsem):
    i = pl.program_id(0)
    n_tiles = pl.num_programs(0)
    tr = MOE_ROW_TILE
    active = i < na_ref[0]
    changed = jnp.logical_or(i == 0, te_ref[i] != te_ref[jnp.maximum(i - 1, 0)])

    def weight_copies(expert, slot):
        return [pltpu.make_async_copy(src.at[expert], dst.at[slot], wsem.at[slot])
                for src, dst in ((wg_ref, wg_f), (wu_ref, wu_f), (wd_ref, wd_f))]

    @pl.when(i == 0)
    def _():
        slot_ref[0] = 0
        rows_in = pltpu.make_async_copy(hp_hbm, hp_ref, hsem)
        rows_in.start()
        for c in weight_copies(te_ref[0], 0):
            c.start()
        clear = pltpu.make_async_copy(zeros_ref, row_tok, sem)
        clear.start()
        clear.wait()

        def invert(t, carry):
            row_tok[pos1_ref[t]] = t
            row_tok[pos2_ref[t]] = t
            return carry

        lax.fori_loop(0, pos1_ref.shape[0], invert, 0, unroll=16)
        rows_in.wait()

        def pick(r, carry):
            xa[pl.ds(r, 1), :] = hp_ref[pl.ds(row_tok[r], 1), :]
            return carry

        lax.fori_loop(0, tr, pick, 0, unroll=8)

    @pl.when(jnp.logical_and(active, changed))
    def _():
        slot = slot_ref[0]
        for c in weight_copies(te_ref[i], slot):
            c.wait()
        following = nxt_ref[i]

        @pl.when(following >= 0)
        def _():
            for c in weight_copies(following, 1 - slot):
                c.start()

        wg_s[...] = wg_f[slot].astype(BF16)
        wu_s[...] = wu_f[slot].astype(BF16)
        wd_s[...] = wd_f[slot].astype(BF16)
        slot_ref[0] = 1 - slot

    def step(cur, nxt):
        base = jnp.minimum(i + 1, n_tiles - 1) * tr
        for r in range(tr):
            nxt[r:r + 1, :] = hp_ref[pl.ds(row_tok[base + r], 1), :]
        lo, hi = _unpack_bf16_pairs(cur[...])
        gate = _dot(lo, wg_s[0:HALF, :]) + _dot(hi, wg_s[HALF:, :])
        up = _dot(lo, wu_s[0:HALF, :]) + _dot(hi, wu_s[HALF:, :])
        hid = (gate * jax.nn.sigmoid(gate) * up).astype(BF16)
        _store_row_tiled(y_ref, _dot(hid, wd_s[...]))

    even = i % 2 == 0

    @pl.when(jnp.logical_and(active, even))
    def _():
        step(xa, xb)

    @pl.when(jnp.logical_and(active, jnp.logical_not(even)))
    def _():
        step(xb, xa)

    @pl.when(jnp.logical_not(active))
    def _():
        y_ref[...] = jnp.zeros_like(y_ref)


def _experts(hp, pos1, pos2, tile_expert, next_expert, n_active, wg, wu, wd):
    tr = MOE_ROW_TILE
    n_tiles = tile_expert.shape[0]
    rows = n_tiles * tr
    hbm = pl.BlockSpec(memory_space=pl.ANY)
    return pl.pallas_call(
        _experts_kernel,
        grid_spec=pltpu.PrefetchScalarGridSpec(
            num_scalar_prefetch=5,
            grid=(n_tiles,),
            in_specs=[hbm, hbm, hbm, hbm, hbm],
            out_specs=pl.BlockSpec((tr * ROW_SPLIT, LANES), lambda i, *_: (i, 0)),
            scratch_shapes=[pltpu.VMEM(hp.shape, U32),
                            pltpu.VMEM((tr, HALF), U32),
                            pltpu.VMEM((tr, HALF), U32),
                            pltpu.VMEM((2, D_MODEL, MOE_D_FF), F32),
                            pltpu.VMEM((2, D_MODEL, MOE_D_FF), F32),
                            pltpu.VMEM((2, MOE_D_FF, D_MODEL), F32),
                            pltpu.VMEM((D_MODEL, MOE_D_FF), BF16),
                            pltpu.VMEM((D_MODEL, MOE_D_FF), BF16),
                            pltpu.VMEM((MOE_D_FF, D_MODEL), BF16),
                            pltpu.SMEM((rows,), I32),
                            pltpu.SMEM((1,), I32),
                            pltpu.SemaphoreType.DMA(()),
                            pltpu.SemaphoreType.DMA(()),
                            pltpu.SemaphoreType.DMA((2,))]),
        out_shape=jax.ShapeDtypeStruct((rows * ROW_SPLIT, LANES), F32),
        compiler_params=pltpu.CompilerParams(dimension_semantics=("arbitrary",),
                                             vmem_limit_bytes=EXPERTS_VMEM_LIMIT),
        name="moe_experts",
    )(pos1, pos2, tile_expert, next_expert, n_active, hp, jnp.zeros((rows,), I32), wg, wu, wd)


def _combine_kernel(pos1_ref, pos2_ref, y_ref, h_ref, p_ref, g_ref, b_ref, *rest, tm, tiled_copy):
    out_ref = rest[0]
    a1, a2, b1, b2, sem = rest[-5:]
    i = pl.program_id(0)
    last = pl.num_programs(0) - 1

    @pl.when(i == 0)
    def _():
        def issue(t, carry):
            _row_copy(y_ref, pos1_ref[t], a1, t, sem.at[0]).start()
            _row_copy(y_ref, pos2_ref[t], a2, t, sem.at[0]).start()
            return carry

        lax.fori_loop(0, tm, issue, 0)

    def step(c1, c2, cur_sem, n1, n2, nxt_sem):
        _tile_wait(y_ref, c1, cur_sem)
        _tile_wait(y_ref, c2, cur_sem)
        base = jnp.minimum(i + 1, last) * tm
        for t in range(tm):
            _row_copy(y_ref, pos1_ref[base + t], n1, t, nxt_sem).start(priority=0)
            _row_copy(y_ref, pos2_ref[base + t], n2, t, nxt_sem).start(priority=DMA_QUEUES - 1)
        ffn = p_ref[:, 0:1] * _load_row_tiled(c1) + p_ref[:, 1:2] * _load_row_tiled(c2)
        out = _layer_norm(DN_ALPHA * h_ref[...] + ffn, g_ref[...], b_ref[...])
        out_ref[...] = out
        if tiled_copy:
            _store_row_tiled(rest[1], out)

        @pl.when(i == last)
        def _():
            _tile_wait(y_ref, n1, nxt_sem)
            _tile_wait(y_ref, n2, nxt_sem)

    @pl.when(i % 2 == 0)
    def _():
        step(a1, a2, sem.at[0], b1, b2, sem.at[1])

    @pl.when(i % 2 == 1)
    def _():
        step(b1, b2, sem.at[1], a1, a2, sem.at[0])


def _combine(y, h, pos1, pos2, gates, g, b, tm, tiled_copy):
    n = h.shape[0]
    tok = lambda i, a, c: (i, 0)
    fixed = lambda i, a, c: (0, 0)
    out_specs = [pl.BlockSpec((tm, D_MODEL), tok)]
    out_shape = [jax.ShapeDtypeStruct((n, D_MODEL), F32)]
    if tiled_copy:
        out_specs.append(pl.BlockSpec((tm * ROW_SPLIT, LANES), tok))
        out_shape.append(jax.ShapeDtypeStruct((n * ROW_SPLIT, LANES), F32))
    return pl.pallas_call(
        functools.partial(_combine_kernel, tm=tm, tiled_copy=tiled_copy),
        grid_spec=pltpu.PrefetchScalarGridSpec(
            num_scalar_prefetch=2,
            grid=(n // tm,),
            in_specs=[pl.BlockSpec(memory_space=pl.ANY),
                      pl.BlockSpec((tm, D_MODEL), tok),
                      pl.BlockSpec((tm, LANES), tok),
                      pl.BlockSpec((1, D_MODEL), fixed), pl.BlockSpec((1, D_MODEL), fixed)],
            out_specs=out_specs,
            scratch_shapes=[pltpu.VMEM((tm * ROW_SPLIT, LANES), F32)] * 4 + [pltpu.SemaphoreType.DMA((2,))]),
        out_shape=out_shape,
        compiler_params=_cparams("arbitrary"),
        name="moe_combine_ln",
    )(pos1, pos2, y, h, gates, g, b)


def _plan_kernel(ints_ref, cnt_ref, pos_ref, meta_ref, *, layer, chunk):
    tr = MOE_ROW_TILE
    n = ints_ref.shape[1]
    ne = MOE_EXPERTS
    tiles = (cnt_ref[...] + (tr - 1.0)) * (1.0 / tr)
    tiles = tiles.astype(I32).astype(F32)
    lower = lax.broadcasted_iota(I32, (ne, ne), 0) >= lax.broadcasted_iota(I32, (ne, ne), 1)
    ends = _dot(jnp.where(lower, 1.0, 0.0).astype(BF16), tiles.astype(BF16))
    start_col = ((ends - tiles) * tr).astype(I32)[:, 0:1]
    r8 = lax.broadcasted_iota(I32, (8, chunk), 0)
    re = lax.broadcasted_iota(I32, (ne, chunk), 0)
    for c in range(n // chunk):
        blk = ints_ref[:, c * chunk:(c + 1) * chunk]
        s1 = jnp.sum(jnp.where(re == blk[0:1], start_col, 0), 0, keepdims=True)
        s2 = jnp.sum(jnp.where(re == blk[1:2], start_col, 0), 0, keepdims=True)
        pos_ref[:, c * chunk:(c + 1) * chunk] = jnp.where(
            r8 == 0, s1 + blk[2:3], jnp.where(r8 == 1, s2 + blk[3:4], 0))
    width = meta_ref.shape[1]
    tile_id = lax.broadcasted_iota(I32, (ne, width), 1).astype(F32)
    te = jnp.sum(jnp.where(ends[:, 0:1] <= tile_id, 1, 0), 0, keepdims=True)
    te = jnp.minimum(te, ne - 1)
    expert = lax.broadcasted_iota(I32, (ne, width), 0)
    later = jnp.logical_and(expert > te, tiles[:, 0:1] > 0.0)
    nxt = jnp.min(jnp.where(later, expert, ne), 0, keepdims=True)
    nxt = jnp.where(nxt < ne, nxt + layer * ne, -1)
    n_used = ends[ne - 1:ne, 0:1].astype(I32)
    rm = lax.broadcasted_iota(I32, (8, width), 0)
    meta_ref[...] = jnp.where(rm == 0, te + layer * ne,
                              jnp.where(rm == 1, n_used, jnp.where(rm == 2, nxt, 0)))


def _plan(ints, cnt, layer, n_tiles):
    n = ints.shape[1]
    width = -(-n_tiles // LANES) * LANES
    return pl.pallas_call(
        functools.partial(_plan_kernel, layer=layer, chunk=2048),
        out_shape=[jax.ShapeDtypeStruct((8, n), I32), jax.ShapeDtypeStruct((8, width), I32)],
        compiler_params=pltpu.CompilerParams(vmem_limit_bytes=VMEM_LIMIT),
        name="moe_plan",
    )(ints, cnt)


def _moe_layer(h, hp, layer, group_w, group_b, expert_w, expert_b, gate_w, up_w, down_w, ln_g, ln_b,
               tiled_copy):
    n = h.shape[0]
    ew = jnp.transpose(expert_w, (1, 0, 2)).reshape(D_MODEL, MOE_EXPERTS)
    w_r = jnp.zeros((D_MODEL, LANES), F32).at[:, 0:MOE_GROUPS].set(group_w).at[:, 8:8 + MOE_EXPERTS].set(ew)
    b_r = jnp.zeros((LANES,), F32).at[0:MOE_GROUPS].set(group_b).at[8:8 + MOE_EXPERTS].set(expert_b.reshape(-1))
    ints, flts, cnt = _router(h, w_r, b_r.reshape(LANES, 1), 512)
    tr = MOE_ROW_TILE
    n_tiles = (2 * n) // tr + MOE_EXPERTS
    pos, meta = _plan(ints, cnt, layer, n_tiles)
    pos1, pos2 = pos[0], pos[1]

    y = _experts(hp, pos1, pos2, meta[0, :n_tiles], meta[2, :n_tiles], meta[1, :1],
                 gate_w.reshape(-1, D_MODEL, MOE_D_FF),
                 up_w.reshape(-1, D_MODEL, MOE_D_FF),
                 down_w.reshape(-1, MOE_D_FF, D_MODEL))
    return _combine(y, h, pos1, pos2, flts,
                    ln_g.reshape(1, -1), ln_b.reshape(1, -1), COMBINE_TILE, tiled_copy)


def _pad_heads(w, axis):
    parts = []
    h0 = 0
    for _, _, nh in ATTN_GROUPS:
        sl = [slice(None)] * w.ndim
        sl[axis] = slice(h0 * ATTN_HEAD_DIM, (h0 + nh) * ATTN_HEAD_DIM)
        part = w[tuple(sl)]
        pad = [(0, 0)] * w.ndim
        pad[axis] = (0, ATTN_SLAB - nh * ATTN_HEAD_DIM)
        parts.append(jnp.pad(part, pad))
        h0 += nh
    return parts


def _in_proj_kernel(x_ref, w_ref, wdt_ref, z_ref, xbc_ref, dt_ref, *, z_tiles):
    j = pl.program_id(1)
    acc = _dot(x_ref[...].astype(BF16), w_ref[...])

    @pl.when(j < z_tiles)
    def _():
        z_ref[...] = acc.astype(z_ref.dtype)

    @pl.when(j >= z_tiles)
    def _():
        xbc_ref[...] = acc.astype(xbc_ref.dtype)

    @pl.when(j == 0)
    def _():
        dt_ref[...] = _dot_f32ish(x_ref[...], wdt_ref[...])


def _in_proj(x, w_zx, w_dt, tm, tn):
    m, k = x.shape
    z_tiles = SSM_D_INNER // tn
    n_tiles = (SSM_D_INNER + SSM_CONV_DIM) // tn
    return pl.pallas_call(
        functools.partial(_in_proj_kernel, z_tiles=z_tiles),
        grid=(m // tm, n_tiles),
        in_specs=[pl.BlockSpec((tm, k), lambda i, j: (i, 0)),
                  pl.BlockSpec((k, tn), lambda i, j: (0, j)),
                  pl.BlockSpec((k, LANES), lambda i, j: (0, 0))],
        out_specs=[pl.BlockSpec((tm, tn), lambda i, j: (i, jnp.minimum(j, z_tiles - 1))),
                   pl.BlockSpec((tm, tn), lambda i, j: (i, jnp.maximum(j - z_tiles, 0))),
                   pl.BlockSpec((tm, LANES), lambda i, j: (i, 0))],
        out_shape=[jax.ShapeDtypeStruct((m, SSM_D_INNER), BF16),
                   jax.ShapeDtypeStruct((m, SSM_CONV_DIM), BF16),
                   jax.ShapeDtypeStruct((m, LANES), F32)],
        compiler_params=pltpu.CompilerParams(dimension_semantics=("parallel", "arbitrary"),
                                             vmem_limit_bytes=IN_PROJ_VMEM_LIMIT),
        name="ssm_in_proj",
    )(x, w_zx, w_dt)


def _ssd_layer(h, in_w, conv_w, conv_b, dt_bias, a_log, d_skip, norm_w, out_w, ln_g, ln_b, bsz, seq):
    split = SSM_D_INNER + SSM_CONV_DIM
    dt_w = jnp.pad(in_w[:, split:], ((0, 0), (0, LANES - SSM_HEADS)))
    z, xbc, dt_raw = _in_proj(h, in_w.astype(BF16), dt_w, 2048, 1024)
    pad32 = lambda v: jnp.pad(v, (0, LANES - SSM_HEADS)).reshape(1, LANES)
    y = _ssd(z, xbc, dt_raw, conv_w, conv_b.reshape(1, -1), pad32(dt_bias), pad32(a_log),
             jnp.repeat(d_skip, SSM_HEAD_DIM).reshape(1, -1), norm_w.reshape(1, -1), bsz, seq)
    return _matmul_res_ln(y, out_w.astype(BF16), h, ln_g.reshape(1, -1), ln_b.reshape(1, -1), 1024)


def _qkv_group_weights(kv_w, q_w):
    width = ATTN_HEADS * ATTN_HEAD_DIM
    w_q = _pad_heads(q_w * (ATTN_HEAD_DIM ** -0.5), 1)
    w_k = _pad_heads(kv_w[:, :width], 1)
    w_v = _pad_heads(kv_w[:, width:], 1)
    return [jnp.concatenate([w_q[gi], w_k[gi], w_v[gi]], axis=1).astype(BF16) for gi in range(len(ATTN_GROUPS))]


def _attn_layer(h, h8, qkv_w, o_w, rel_bias, ln_g, ln_b, bsz, seq):
    outs, lses = [], []
    h0 = 0
    for gi, (_, dil, nh) in enumerate(ATTN_GROUPS):
        if dil == 1:
            qkv = _matmul(h, qkv_w[gi], BF16, 1024, 3 * ATTN_SLAB)
        else:
            qkv = _qkv_dilated(h8, qkv_w[gi], dil, max(1024, ATTN_BLOCK * dil))
        bias_p, bias_c = _group_bias(rel_bias, h0, nh, dil)
        o, lse = _attention_group(qkv, bias_p, bias_c, gi, dil, nh, bsz, seq)
        outs.extend(o)
        lses.append(lse)
        h0 += nh
    w_o = jnp.concatenate(_pad_heads(o_w, 0), axis=0).astype(BF16)
    return _attn_out(outs, lses, w_o, h, ln_g.reshape(1, -1), ln_b.reshape(1, -1), 1024)


def kernel(x, ssm_in_w, ssm_conv_w, ssm_conv_b, ssm_dt_bias, ssm_a_log, ssm_d, ssm_norm_w, ssm_out_w,
           kv_w, attn_q_w, attn_o_w, rel_bias, moe_group_w, moe_group_b, moe_expert_w, moe_expert_b,
           moe_gate_w, moe_up_w, moe_down_w, ln_g, ln_b):
    bsz, seq, d = x.shape
    h = x.reshape(bsz * seq, d)
    n_ssd = DEPTH // 2
    h8 = None
    for i in range(DEPTH):
        if i < n_ssd:
            h, hp = _ssd_layer(h, ssm_in_w[i], ssm_conv_w[i], ssm_conv_b[i], ssm_dt_bias[i], ssm_a_log[i],
                               ssm_d[i], ssm_norm_w[i], ssm_out_w[i], ln_g[i, 0], ln_b[i, 0], bsz, seq)
        else:
            j = i - n_ssd
            h, hp = _attn_layer(h, h8, _qkv_group_weights(kv_w, attn_q_w[j]), attn_o_w[j], rel_bias,
                                ln_g[i, 0], ln_b[i, 0], bsz, seq)
        feeds_attention = n_ssd <= i + 1 < DEPTH
        res = _moe_layer(h, hp, i, moe_group_w[i], moe_group_b[i], moe_expert_w[i], moe_expert_b[i],
                         moe_gate_w, moe_up_w, moe_down_w, ln_g[i, 1], ln_b[i, 1], feeds_attention)
        h = res[0]
        h8 = res[1] if feeds_attention else None
    return h.reshape(bsz, seq, d)
```

```python
import functools
import math

import numpy as np
import jax
import jax.numpy as jnp
from jax import lax
from jax.experimental import pallas as pl
from jax.experimental.pallas import tpu as pltpu

F32 = jnp.float32
BF16 = jnp.bfloat16
I32 = jnp.int32

D_MODEL = 1024
DEPTH = 2
DN_ALPHA = (2 * DEPTH) ** 0.25
LN_EPS = 1e-5
LOG2_E = math.log2(math.e)

SSM_D_INNER = 2048
SSM_HEAD_DIM = 64
SSM_HEADS = 32
SSM_GROUPS = 4
SSM_STATE = 128
SSM_CONV = 4
SSM_CHUNK = 128
SSM_CONV_DIM = SSM_D_INNER + 2 * SSM_GROUPS * SSM_STATE

ATTN_HEAD_DIM = 64
ATTN_GROUPS = ((128, 1, 6), (512, 4, 5), (2048, 16, 5))
ATTN_HEADS = 16
ATTN_BLOCK = 128
N_BUCKETS = 32
MAX_DISTANCE = 2048
ATTN_SLAB = 384
NEG_BIG = -1e30

MOE_GROUPS = 4
MOE_EPG = 8
MOE_EXPERTS = MOE_GROUPS * MOE_EPG
MOE_D_FF = 512
MOE_ROW_TILE = 512
COMBINE_TILE = 512

LANES = 128
DMA_QUEUES = 2
VMEM_LIMIT = 48 * 1024 * 1024
EXPERTS_VMEM_LIMIT = 56 * 1024 * 1024


def _cparams(*sem):
    return pltpu.CompilerParams(dimension_semantics=sem, vmem_limit_bytes=VMEM_LIMIT)


def _layer_norm(x, g, b):
    mu = jnp.mean(x, -1, keepdims=True)
    xc = x - mu
    var = jnp.mean(xc * xc, -1, keepdims=True)
    return xc * lax.rsqrt(var + LN_EPS) * g + b


def _split2(x):
    hi = x.astype(BF16)
    lo = (x - hi.astype(F32)).astype(BF16)
    return hi, lo


def _dot(a, b):
    return jnp.dot(a, b, preferred_element_type=F32)


def _dot_f32ish(a, b):
    ah, al = _split2(a)
    bh, bl = _split2(b)
    return _dot(ah, bh) + _dot(al, bh) + _dot(ah, bl)


def _mm_kernel(a_ref, b_ref, o_ref):
    o_ref[...] = _dot(a_ref[...].astype(BF16), b_ref[...]).astype(o_ref.dtype)


def _matmul(a, b, out_dtype, tm, tn):
    m, k = a.shape
    nc = b.shape[1]
    return pl.pallas_call(
        _mm_kernel,
        grid=(m // tm, nc // tn),
        in_specs=[pl.BlockSpec((tm, k), lambda i, j: (i, 0)),
                  pl.BlockSpec((k, tn), lambda i, j: (0, j))],
        out_specs=pl.BlockSpec((tm, tn), lambda i, j: (i, j)),
        out_shape=jax.ShapeDtypeStruct((m, nc), out_dtype),
        compiler_params=_cparams("parallel", "parallel"),
        name="matmul",
    )(a, b)


def _mm3_kernel(a_ref, b_ref, o_ref):
    o_ref[...] = _dot_f32ish(a_ref[...], b_ref[...])


def _matmul_f32ish(a, b, tm):
    m, k = a.shape
    nc = b.shape[1]
    return pl.pallas_call(
        _mm3_kernel,
        grid=(m // tm,),
        in_specs=[pl.BlockSpec((tm, k), lambda i: (i, 0)),
                  pl.BlockSpec((k, nc), lambda i: (0, 0))],
        out_specs=pl.BlockSpec((tm, nc), lambda i: (i, 0)),
        out_shape=jax.ShapeDtypeStruct((m, nc), F32),
        compiler_params=_cparams("parallel"),
        name="matmul_f32ish",
    )(a, b)


ROW_SPLIT = D_MODEL // LANES


def _store_row_tiled(ref, val):
    rows = val.shape[0]
    for c in range(ROW_SPLIT):
        ref[pl.ds(c, rows, stride=ROW_SPLIT), :] = val[:, c * LANES:(c + 1) * LANES]


def _load_row_tiled(ref):
    rows = ref.shape[0] // ROW_SPLIT
    return jnp.concatenate([ref[pl.ds(c, rows, stride=ROW_SPLIT), :] for c in range(ROW_SPLIT)], axis=1)


HALF = D_MODEL // 2
U32 = jnp.uint32
HI16 = 0xFFFF0000


def _pack_bf16_pairs(x):
    bits = lax.bitcast_convert_type(x.astype(BF16).astype(F32), U32)
    return (bits[:, :HALF] >> 16) | (bits[:, HALF:] & U32(HI16))


def _unpack_bf16_pairs(w):
    lo = lax.bitcast_convert_type(w << 16, F32).astype(BF16)
    hi = lax.bitcast_convert_type(w & U32(HI16), F32).astype(BF16)
    return lo, hi


def _mm_ln_kernel(a_ref, w_ref, h_ref, g_ref, b_ref, o_ref, op_ref):
    acc = _dot(a_ref[...], w_ref[...])
    out = _layer_norm(DN_ALPHA * h_ref[...] + acc, g_ref[...], b_ref[...])
    o_ref[...] = out
    op_ref[...] = _pack_bf16_pairs(out)


def _matmul_res_ln(a, w, h, g, b, tm):
    m, k = a.shape
    d = w.shape[1]
    return pl.pallas_call(
        _mm_ln_kernel,
        grid=(m // tm,),
        in_specs=[pl.BlockSpec((tm, k), lambda i: (i, 0)),
                  pl.BlockSpec((k, d), lambda i: (0, 0)),
                  pl.BlockSpec((tm, d), lambda i: (i, 0)),
                  pl.BlockSpec((1, d), lambda i: (0, 0)),
                  pl.BlockSpec((1, d), lambda i: (0, 0))],
        out_specs=[pl.BlockSpec((tm, d), lambda i: (i, 0)),
                   pl.BlockSpec((tm, HALF), lambda i: (i, 0))],
        out_shape=[jax.ShapeDtypeStruct((m, d), F32),
                   jax.ShapeDtypeStruct((m, HALF), U32)],
        compiler_params=_cparams("parallel"),
        name="matmul_res_ln",
    )(a, w, h, g, b)


SSD_CHUNKS_PER_STEP = 2


def _ssd_kernel(z_ref, xbc_ref, dt_ref, cw_ref, cb_ref, dtb_ref, alog_ref, dsk_ref, nw_ref,
                y_ref, xe_ref, st_ref):
    @pl.when(pl.program_id(1) == 0)
    def _():
        xe_ref[:, 0:8, :] = jnp.zeros((xe_ref.shape[0], 8, LANES), F32)
        st_ref[...] = jnp.zeros_like(st_ref)

    for sub in range(SSD_CHUNKS_PER_STEP):
        _ssd_chunk(pl.ds(sub * SSM_CHUNK, SSM_CHUNK), z_ref, xbc_ref, dt_ref, cw_ref, cb_ref, dtb_ref,
                   alog_ref, dsk_ref, nw_ref, y_ref, xe_ref, st_ref)


def _ssd_chunk(rows, z_ref, xbc_ref, dt_ref, cw_ref, cb_ref, dtb_ref, alog_ref, dsk_ref, nw_ref,
               y_ref, xe_ref, st_ref):
    q = SSM_CHUNK
    w = cw_ref[...]
    bias = cb_ref[...]
    act = []
    for c in range(SSM_CONV_DIM // LANES):
        cols = slice(c * LANES, (c + 1) * LANES)
        u = xbc_ref[rows, cols]
        xe_ref[c, 8:8 + q, :] = u
        conv = (bias[:, cols] + w[3:4, cols] * u + w[2:3, cols] * xe_ref[c, 7:7 + q, :]
                + w[1:2, cols] * xe_ref[c, 6:6 + q, :] + w[0:1, cols] * xe_ref[c, 5:5 + q, :])
        xe_ref[c, 0:8, :] = xe_ref[c, q:q + 8, :]
        act.append(conv * jax.nn.sigmoid(conv))

    pre = dt_ref[rows, :] + dtb_ref[...]
    dt = jnp.maximum(pre, 0.0) + jnp.log(1.0 + jnp.exp(-jnp.abs(pre)))
    adt = dt * (-jnp.exp(alog_ref[...]) * LOG2_E)

    row = lax.broadcasted_iota(I32, (q, q), 0)
    col = lax.broadcasted_iota(I32, (q, q), 1)
    tril = row >= col
    tri_b = jnp.where(tril, 1.0, 0.0).astype(BF16)
    a_hi = adt.astype(BF16)
    r1 = adt - a_hi.astype(F32)
    a_mid = r1.astype(BF16)
    a_lo = (r1 - a_mid.astype(F32)).astype(BF16)
    acs = _dot(tri_b, a_hi) + _dot(tri_b, a_mid) + _dot(tri_b, a_lo)
    acs_t = acs.T
    acs_dt_t = acs_t - jnp.log(dt.T) * LOG2_E
    eacs = jnp.exp2(acs)
    left = col < SSM_HEAD_DIM

    for g in range(SSM_GROUPS):
        b0 = SSM_D_INNER + g * SSM_STATE
        c0 = SSM_D_INNER + SSM_GROUPS * SSM_STATE + g * SSM_STATE
        bm = act[b0 // LANES]
        cm = act[c0 // LANES].astype(BF16)
        cb = lax.dot_general(cm, bm.astype(BF16), (((1,), (1,)), ((), ())),
                             preferred_element_type=F32)
        bm_t = bm.T
        gs = g * 512
        y_off = _dot(cm, st_ref[:, gs:gs + 512].astype(BF16))
        slabs = []
        for pr in range(4):
            ha = g * 8 + pr * 2
            hb = ha + 1
            cs = gs + pr * LANES
            x2 = act[cs // LANES]
            x2b = x2.astype(BF16)
            ys, ups = [], []
            for h in (ha, hb):
                a_col = acs[:, h:h + 1]
                a_src = acs_dt_t[h:h + 1, :]
                decay = jnp.where(tril, jnp.exp2(a_col - a_src), 0.0)
                ys.append(_dot((cb * decay).astype(BF16), x2b))
                to_end = jnp.exp2(acs_t[h:h + 1, q - 1:q] - a_src)
                ups.append(_dot((bm_t * to_end).astype(BF16), x2b))
            y_diag = jnp.where(left, ys[0], ys[1])
            upd = jnp.where(left, ups[0], ups[1])
            e2 = jnp.where(left, eacs[:, ha:ha + 1], eacs[:, hb:hb + 1])
            cd = jnp.where(left[0:1, :], eacs[q - 1:q, ha:ha + 1], eacs[q - 1:q, hb:hb + 1])
            y2 = y_diag + y_off[:, pr * LANES:(pr + 1) * LANES] * e2 + dsk_ref[:, cs:cs + LANES] * x2
            st_ref[:, cs:cs + LANES] = st_ref[:, cs:cs + LANES] * cd + upd
            slabs.append(y2)
        yg = jnp.concatenate(slabs, axis=1)
        zg = z_ref[rows, gs:gs + 512]
        yg = yg * (zg * jax.nn.sigmoid(zg))
        ms = jnp.mean(yg * yg, -1, keepdims=True)
        y_ref[rows, gs:gs + 512] = (yg * lax.rsqrt(ms + LN_EPS) * nw_ref[:, gs:gs + 512]).astype(y_ref.dtype)


def _ssd(z, xbc, dt_raw, conv_w, conv_b, dt_bias, a_log, d_rep, norm_w, bsz, seq):
    n = z.shape[0]
    q = SSM_CHUNK * SSD_CHUNKS_PER_STEP
    nchunk = seq // q
    tok = lambda b, c: (b * nchunk + c, 0)
    fixed = lambda b, c: (0, 0)
    return pl.pallas_call(
        _ssd_kernel,
        grid=(bsz, nchunk),
        in_specs=[pl.BlockSpec((q, SSM_D_INNER), tok),
                  pl.BlockSpec((q, SSM_CONV_DIM), tok),
                  pl.BlockSpec((q, LANES), tok),
                  pl.BlockSpec((SSM_CONV, SSM_CONV_DIM), fixed),
                  pl.BlockSpec((1, SSM_CONV_DIM), fixed),
                  pl.BlockSpec((1, LANES), fixed),
                  pl.BlockSpec((1, LANES), fixed),
                  pl.BlockSpec((1, SSM_D_INNER), fixed),
                  pl.BlockSpec((1, SSM_D_INNER), fixed)],
        out_specs=pl.BlockSpec((q, SSM_D_INNER), tok),
        out_shape=jax.ShapeDtypeStruct((n, SSM_D_INNER), BF16),
        scratch_shapes=[pltpu.VMEM((SSM_CONV_DIM // LANES, SSM_CHUNK + 8, LANES), F32),
                        pltpu.VMEM((SSM_STATE, SSM_D_INNER), F32)],
        compiler_params=_cparams("parallel", "arbitrary"),
        name="ssd_chunk",
    )(z, xbc, dt_raw, conv_w, conv_b, dt_bias, a_log, d_rep, norm_w)


def _residue_major_pieces(h8_ref, n_tokens, dil):
    span = ATTN_BLOCK * dil
    pieces = []
    for blk in range(n_tokens // span):
        for r in range(dil):
            first = (blk * span + r) * ROW_SPLIT
            pieces.append(jnp.concatenate(
                [h8_ref[pl.ds(first + c, ATTN_BLOCK, stride=ROW_SPLIT * dil), :] for c in range(ROW_SPLIT)],
                axis=1).astype(BF16))
    return pieces


def _qkv_dilated_kernel(h8_ref, w_ref, o_ref, *, dil):
    per_dot = 4
    rows = _residue_major_pieces(h8_ref, o_ref.shape[0], dil)
    for k in range(0, len(rows), per_dot):
        x = jnp.concatenate(rows[k:k + per_dot], axis=0)
        o_ref[k * ATTN_BLOCK:(k + per_dot) * ATTN_BLOCK, :] = _dot(x, w_ref[...]).astype(o_ref.dtype)


def _qkv_dilated(h8, w, dil, tm):
    n = h8.shape[0] // ROW_SPLIT
    nc = w.shape[1]
    return pl.pallas_call(
        functools.partial(_qkv_dilated_kernel, dil=dil),
        grid=(n // tm,),
        in_specs=[pl.BlockSpec((tm * ROW_SPLIT, LANES), lambda i: (i, 0)),
                  pl.BlockSpec((D_MODEL, nc), lambda i: (0, 0))],
        out_specs=pl.BlockSpec((tm, nc), lambda i: (i, 0)),
        out_shape=jax.ShapeDtypeStruct((n, nc), BF16),
        compiler_params=_cparams("parallel"),
        name=f"qkv_dil{dil}",
    )(h8, w)


def _attn_kernel(*refs, nh, dil, has_prev, blocks):
    span = ATTN_BLOCK * dil
    if has_prev:
        q_ref, kp_ref, kc_ref, vp_ref, vc_ref, bp_ref, bc_ref = refs[:7]
        out_refs = refs[7:]
        first_pen = jnp.where(pl.program_id(1) > 0, 0.0, NEG_BIG)
    else:
        q_ref, kc_ref, vc_ref, bc_ref = refs[:4]
        out_refs = refs[4:]
    s_scr, p_scr, max_scr, den_scr = out_refs[-4:]
    o_refs, lse_ref = out_refs[:-5], out_refs[-5]
    ones = jnp.ones((s_scr.shape[2], LANES), BF16)
    lane = lax.broadcasted_iota(I32, (ATTN_BLOCK, LANES), 1)
    left = lane < ATTN_HEAD_DIM
    zero = jnp.zeros((), BF16)
    nt = (((1,), (1,)), ((), ()))

    def residue(blk, r):
        rows = pl.ds(pl.multiple_of(blk * span + r * ATTN_BLOCK, ATTN_BLOCK), ATTN_BLOCK)
        dst = pl.ds(blk * span + r, ATTN_BLOCK, stride=dil)
        if has_prev and blk == 0:
            kprev, vprev, pen = kp_ref, vp_ref, first_pen
            prows = pl.ds(pl.multiple_of(r * ATTN_BLOCK, ATTN_BLOCK), ATTN_BLOCK)
        elif has_prev:
            kprev, vprev, pen = kc_ref, vc_ref, 0.0
            prows = pl.ds(pl.multiple_of((blk - 1) * span + r * ATTN_BLOCK, ATTN_BLOCK), ATTN_BLOCK)
        for hh in range(nh):
            cols = pl.ds(hh // 2 * LANES, LANES)
            qm = jnp.where(left if hh % 2 == 0 else ~left, q_ref[rows, cols], zero)
            s_c = lax.dot_general(qm, kc_ref[rows, cols], nt, preferred_element_type=F32) + bc_ref[hh]
            if has_prev:
                s_p = lax.dot_general(qm, kprev[prows, cols], nt, preferred_element_type=F32)
                s_scr[hh, :, 0:ATTN_BLOCK] = s_p + (bp_ref[hh] + pen)
                s_scr[hh, :, ATTN_BLOCK:] = s_c
            else:
                s_scr[hh] = s_c
        for hh in range(nh):
            s = s_scr[hh]
            m = jnp.max(s, -1, keepdims=True)
            p = jnp.exp(s - m)
            p_scr[hh] = p.astype(BF16)
            max_scr[hh] = m
            if not has_prev:
                den_scr[hh] = jnp.sum(p, -1, keepdims=True)
        lse_sum = jnp.zeros((ATTN_BLOCK, 1), F32)
        for pr in range(ATTN_SLAB // LANES):
            cols = pl.ds(pr * LANES, LANES)
            halves = []
            for hh in (pr * 2, pr * 2 + 1):
                if hh >= nh:
                    halves.append(jnp.zeros((ATTN_BLOCK, LANES), F32))
                    continue
                if has_prev:
                    den = _dot(p_scr[hh], ones)
                    den1 = den[:, 0:1]
                    o = (_dot(p_scr[hh, :, 0:ATTN_BLOCK], vprev[prows, cols])
                         + _dot(p_scr[hh, :, ATTN_BLOCK:], vc_ref[rows, cols]))
                else:
                    den = den1 = den_scr[hh]
                    o = _dot(p_scr[hh], vc_ref[rows, cols])
                halves.append(o * (1.0 / den))
                lse_sum = lse_sum + (max_scr[hh] + jnp.log(den1))
            o_refs[pr][dst, :] = jnp.where(left, halves[0], halves[1])
        lse_ref[dst, :] = jnp.broadcast_to(lse_sum * (1.0 / nh), (ATTN_BLOCK, LANES))

    for blk in range(blocks):
        def body(r, carry, blk=blk):
            residue(blk, r)
            return carry

        lax.fori_loop(0, dil, body, 0)


def _attention_group(qkv, bias_p, bias_c, gi, dil, nh, bsz, seq):
    span = ATTN_BLOCK * dil
    nb = seq // span
    has_prev = nb > 1
    blocks = max(1, 512 // span)
    steps = nb // blocks
    blk = (blocks * span, ATTN_SLAB)
    cur = lambda which: (lambda b, n: (b * steps + n, which))
    prev = lambda which: (lambda b, n: (b * nb + jnp.maximum(n * blocks - 1, 0), which))
    fixed = lambda b, n: (0, 0, 0)
    tab = pl.BlockSpec((nh, ATTN_BLOCK, ATTN_BLOCK), fixed)
    if has_prev:
        pblk = (span, ATTN_SLAB)
        in_specs = [pl.BlockSpec(blk, cur(0)), pl.BlockSpec(pblk, prev(1)), pl.BlockSpec(blk, cur(1)),
                    pl.BlockSpec(pblk, prev(2)), pl.BlockSpec(blk, cur(2)), tab, tab]
        args = (qkv, qkv, qkv, qkv, qkv, bias_p, bias_c)
    else:
        in_specs = [pl.BlockSpec(blk, cur(0)), pl.BlockSpec(blk, cur(1)), pl.BlockSpec(blk, cur(2)), tab]
        args = (qkv, qkv, qkv, bias_c)
    n_out = ATTN_SLAB // LANES + 1
    keys = 2 * ATTN_BLOCK if has_prev else ATTN_BLOCK
    outs = pl.pallas_call(
        functools.partial(_attn_kernel, nh=nh, dil=dil, has_prev=has_prev, blocks=blocks),
        grid=(bsz, steps),
        in_specs=in_specs,
        out_specs=[pl.BlockSpec((blocks * span, LANES), lambda b, n: (b * steps + n, 0))] * n_out,
        out_shape=[jax.ShapeDtypeStruct((bsz * seq, LANES), F32)] * n_out,
        scratch_shapes=[pltpu.VMEM((nh, ATTN_BLOCK, keys), F32),
                        pltpu.VMEM((nh, ATTN_BLOCK, keys), BF16),
                        pltpu.VMEM((nh, ATTN_BLOCK, 1), F32),
                        pltpu.VMEM((nh, ATTN_BLOCK, 1), F32)],
        compiler_params=_cparams("parallel", "parallel"),
        name=f"dilated_attn_g{gi}",
    )(*args)
    return outs[:-1], outs[-1]


def _attn_out_kernel(*refs):
    n_pairs = ATTN_SLAB // LANES
    n_groups = len(ATTN_GROUPS)
    o_refs = refs[:n_groups * n_pairs]
    l_refs = refs[n_groups * n_pairs:n_groups * (n_pairs + 1)]
    w_ref, h_ref, g_ref, b_ref, out_ref, outp_ref = refs[n_groups * (n_pairs + 1):]
    ls = [r[:, 0:1] for r in l_refs]
    m = jnp.maximum(jnp.maximum(ls[0], ls[1]), ls[2])
    es = [jnp.exp(v - m) for v in ls]
    scale = n_groups / (es[0] + es[1] + es[2])
    slabs = []
    for gi in range(n_groups):
        wt = es[gi] * scale
        slabs.extend((r[...] * wt).astype(BF16) for r in o_refs[gi * n_pairs:(gi + 1) * n_pairs])
    acc = _dot(jnp.concatenate(slabs, axis=1), w_ref[...])
    out = _layer_norm(DN_ALPHA * h_ref[...] + acc, g_ref[...], b_ref[...])
    out_ref[...] = out
    outp_ref[...] = _pack_bf16_pairs(out)


def _attn_out(outs, lses, w, h, g, b, tm):
    n = h.shape[0]
    tok = lambda i: (i, 0)
    fixed2 = lambda i: (0, 0)
    return pl.pallas_call(
        _attn_out_kernel,
        grid=(n // tm,),
        in_specs=[pl.BlockSpec((tm, LANES), tok)] * (len(outs) + len(lses))
                 + [pl.BlockSpec((len(ATTN_GROUPS) * ATTN_SLAB, D_MODEL), lambda i: (0, 0)),
                    pl.BlockSpec((tm, D_MODEL), tok),
                    pl.BlockSpec((1, D_MODEL), fixed2), pl.BlockSpec((1, D_MODEL), fixed2)],
        out_specs=[pl.BlockSpec((tm, D_MODEL), tok), pl.BlockSpec((tm, HALF), tok)],
        out_shape=[jax.ShapeDtypeStruct((n, D_MODEL), F32),
                   jax.ShapeDtypeStruct((n, HALF), U32)],
        compiler_params=_cparams("parallel"),
        name="attn_out_ln",
    )(*outs, *lses, w, h, g, b)


def _t5_bucket(dist):
    max_exact = N_BUCKETS // 2
    n = np.maximum(dist, 1).astype(np.float64)
    large = max_exact + (np.log(n / max_exact) / np.log(MAX_DISTANCE / max_exact)
                         * (N_BUCKETS - max_exact)).astype(np.int32)
    large = np.minimum(large, N_BUCKETS - 1)
    return np.where(dist < max_exact, dist, large).astype(np.int32)


def _group_bias(rel_bias, h0, nh, dil):
    qi = np.arange(ATTN_BLOCK)[:, None]
    ki = np.arange(ATTN_BLOCK)[None, :]
    tabs = []
    for delta, band in ((qi + ATTN_BLOCK - ki, ki >= qi), (qi - ki, ki <= qi)):
        bucket = _t5_bucket(np.clip(delta, 0, None) * dil)
        onehot = (bucket[..., None] == np.arange(N_BUCKETS)).astype(np.float32)
        t = jnp.einsum("qkb,bh->hqk", onehot, rel_bias[:, h0:h0 + nh], precision=lax.Precision.HIGHEST)
        tabs.append(jnp.where(band[None], t, NEG_BIG).astype(F32))
    return tabs


def _router_kernel(h_ref, w_ref, b_ref, ints_ref, flts_ref, cnt_ref, carry_ref, before_ref, *, tm):
    @pl.when(pl.program_id(0) == 0)
    def _():
        carry_ref[...] = jnp.zeros_like(carry_ref)
        tr = lax.broadcasted_iota(I32, (tm, tm), 0)
        tc = lax.broadcasted_iota(I32, (tm, tm), 1)
        before_ref[...] = jnp.where(tr < tc, 1.0, 0.0).astype(BF16)

    lt = _dot_f32ish(h_ref[...], w_ref[...]).T + b_ref[...]
    gl = lt[0:MOE_GROUPS]
    r4 = lax.broadcasted_iota(I32, (MOE_GROUPS, tm), 0)
    gmax = jnp.max(gl, 0, keepdims=True)
    gidx = jnp.min(jnp.where(gl == gmax, r4, MOE_GROUPS), 0, keepdims=True)
    gval = 1.0 / jnp.sum(jnp.exp(gl - gmax), 0, keepdims=True)

    esel = jnp.zeros((MOE_EPG, tm), F32)
    for g in range(MOE_GROUPS):
        esel = jnp.where(gidx == g, lt[8 + g * MOE_EPG:8 + (g + 1) * MOE_EPG], esel)
    r8 = lax.broadcasted_iota(I32, (MOE_EPG, tm), 0)
    v1 = jnp.max(esel, 0, keepdims=True)
    i1 = jnp.min(jnp.where(esel == v1, r8, MOE_EPG), 0, keepdims=True)
    rest = jnp.where(r8 == i1, -jnp.inf, esel)
    v2 = jnp.max(rest, 0, keepdims=True)
    i2 = jnp.min(jnp.where(rest == v2, r8, MOE_EPG), 0, keepdims=True)
    t = jnp.exp(v2 - v1)
    p1 = gval / (1.0 + t)
    p2 = p1 * t
    e1 = gidx * MOE_EPG + i1
    e2 = gidx * MOE_EPG + i2

    r32 = lax.broadcasted_iota(I32, (MOE_EXPERTS, tm), 0)
    oh1 = r32 == e1
    oh2 = r32 == e2
    oh = jnp.where(oh1 | oh2, 1.0, 0.0)
    base = _dot(oh.astype(BF16), before_ref[...]) + carry_ref[:, 0:1]
    rank1 = jnp.sum(jnp.where(oh1, base, 0.0), 0, keepdims=True).astype(I32)
    rank2 = jnp.sum(jnp.where(oh2, base, 0.0), 0, keepdims=True).astype(I32)
    carry_ref[...] = carry_ref[...] + jnp.sum(oh, 1, keepdims=True)
    cnt_ref[...] = carry_ref[...]

    ints_ref[...] = jnp.where(r8 == 0, e1, jnp.where(r8 == 1, e2, jnp.where(r8 == 2, rank1,
                              jnp.where(r8 == 3, rank2, 0))))
    r128 = lax.broadcasted_iota(I32, (LANES, tm), 0)
    flts_ref[...] = jnp.where(r128 == 0, p1, jnp.where(r128 == 1, p2, 0.0)).T


def _router(h, w_r, b_r, tm):
    n = h.shape[0]
    return pl.pallas_call(
        functools.partial(_router_kernel, tm=tm),
        grid=(n // tm,),
        in_specs=[pl.BlockSpec((tm, D_MODEL), lambda i: (i, 0)),
                  pl.BlockSpec((D_MODEL, LANES), lambda i: (0, 0)),
                  pl.BlockSpec((LANES, 1), lambda i: (0, 0))],
        out_specs=[pl.BlockSpec((8, tm), lambda i: (0, i)),
                   pl.BlockSpec((tm, LANES), lambda i: (i, 0)),
                   pl.BlockSpec((MOE_EXPERTS, LANES), lambda i: (0, 0))],
        out_shape=[jax.ShapeDtypeStruct((8, n), I32),
                   jax.ShapeDtypeStruct((n, LANES), F32),
                   jax.ShapeDtypeStruct((MOE_EXPERTS, LANES), F32)],
        scratch_shapes=[pltpu.VMEM((MOE_EXPERTS, LANES), F32), pltpu.VMEM((tm, tm), BF16)],
        compiler_params=_cparams("arbitrary"),
        name="moe_router",
    )(h, w_r, b_r)


def _row_copy(src_ref, s, dst_ref, d, sem):
    def first(r):
        return r * ROW_SPLIT if isinstance(r, int) else pl.multiple_of(r * ROW_SPLIT, ROW_SPLIT)

    return pltpu.make_async_copy(src_ref.at[pl.ds(first(s), ROW_SPLIT), :],
                                 dst_ref.at[pl.ds(first(d), ROW_SPLIT), :], sem)


def _tile_wait(src_ref, dst_ref, sem):
    pltpu.make_async_copy(src_ref.at[pl.ds(0, dst_ref.shape[0]), :], dst_ref, sem).wait()


def _experts_kernel(pos1_ref, pos2_ref, te_ref, nxt_ref, na_ref, hp_hbm, zeros_ref, wg_ref, wu_ref, wd_ref,
                    y_ref, hp_ref, xa, xb, wg_f, wu_f, wd_f, wg_s, wu_s, wd_s, row_tok, slot_ref,
                    sem, hsem, wsem):
    i = pl.program_id(0)
    n_tiles = pl.num_programs(0)
    tr = MOE_ROW_TILE
    active = i < na_ref[0]
    changed = jnp.logical_or(i == 0, te_ref[i] != te_ref[jnp.maximum(i - 1, 0)])

    def weight_copies(expert, slot):
        return [pltpu.make_async_copy(src.at[expert], dst.at[slot], wsem.at[slot])
                for src, dst in ((wg_ref, wg_f), (wu_ref, wu_f), (wd_ref, wd_f))]

    @pl.when(i == 0)
    def _():
        slot_ref[0] = 0
        rows_in = pltpu.make_async_copy(hp_hbm, hp_ref, hsem)
        rows_in.start()
        for c in weight_copies(te_ref[0], 0):
            c.start()
        clear = pltpu.make_async_copy(zeros_ref, row_tok, sem)
        clear.start()
        clear.wait()

        def invert(t, carry):
            row_tok[pos1_ref[t]] = t
            row_tok[pos2_ref[t]] = t
            return carry

        lax.fori_loop(0, pos1_ref.shape[0], invert, 0, unroll=16)
        rows_in.wait()

        def pick(r, carry):
            xa[pl.ds(r, 1), :] = hp_ref[pl.ds(row_tok[r], 1), :]
            return carry

        lax.fori_loop(0, tr, pick, 0, unroll=8)

    @pl.when(jnp.logical_and(active, changed))
    def _():
        slot = slot_ref[0]
        for c in weight_copies(te_ref[i], slot):
            c.wait()
        following = nxt_ref[i]

        @pl.when(following >= 0)
        def _():
            for c in weight_copies(following, 1 - slot):
                c.start()

        wg_s[...] = wg_f[slot].astype(BF16)
        wu_s[...] = wu_f[slot].astype(BF16)
        wd_s[...] = wd_f[slot].astype(BF16)
        slot_ref[0] = 1 - slot

    def step(cur, nxt):
        base = jnp.minimum(i + 1, n_tiles - 1) * tr
        for r in range(tr):
            nxt[r:r + 1, :] = hp_ref[pl.ds(row_tok[base + r], 1), :]
        lo, hi = _unpack_bf16_pairs(cur[...])
        gate = _dot(lo, wg_s[0:HALF, :]) + _dot(hi, wg_s[HALF:, :])
        up = _dot(lo, wu_s[0:HALF, :]) + _dot(hi, wu_s[HALF:, :])
        hid = (gate * jax.nn.sigmoid(gate) * up).astype(BF16)
        _store_row_tiled(y_ref, _dot(hid, wd_s[...]))

    even = i % 2 == 0

    @pl.when(jnp.logical_and(active, even))
    def _():
        step(xa, xb)

    @pl.when(jnp.logical_and(active, jnp.logical_not(even)))
    def _():
        step(xb, xa)

    @pl.when(jnp.logical_not(active))
    def _():
        y_ref[...] = jnp.zeros_like(y_ref)


def _experts(hp, pos1, pos2, tile_expert, next_expert, n_active, wg, wu, wd):
    tr = MOE_ROW_TILE
    n_tiles = tile_expert.shape[0]
    rows = n_tiles * tr
    hbm = pl.BlockSpec(memory_space=pl.ANY)
    return pl.pallas_call(
        _experts_kernel,
        grid_spec=pltpu.PrefetchScalarGridSpec(
            num_scalar_prefetch=5,
            grid=(n_tiles,),
            in_specs=[hbm, hbm, hbm, hbm, hbm],
            out_specs=pl.BlockSpec((tr * ROW_SPLIT, LANES), lambda i, *_: (i, 0)),
            scratch_shapes=[pltpu.VMEM(hp.shape, U32),
                            pltpu.VMEM((tr, HALF), U32),
                            pltpu.VMEM((tr, HALF), U32),
                            pltpu.VMEM((2, D_MODEL, MOE_D_FF), F32),
                            pltpu.VMEM((2, D_MODEL, MOE_D_FF), F32),
                            pltpu.VMEM((2, MOE_D_FF, D_MODEL), F32),
                            pltpu.VMEM((D_MODEL, MOE_D_FF), BF16),
                            pltpu.VMEM((D_MODEL, MOE_D_FF), BF16),
                            pltpu.VMEM((MOE_D_FF, D_MODEL), BF16),
                            pltpu.SMEM((rows,), I32),
                            pltpu.SMEM((1,), I32),
                            pltpu.SemaphoreType.DMA(()),
                            pltpu.SemaphoreType.DMA(()),
                            pltpu.SemaphoreType.DMA((2,))]),
        out_shape=jax.ShapeDtypeStruct((rows * ROW_SPLIT, LANES), F32),
        compiler_params=pltpu.CompilerParams(dimension_semantics=("arbitrary",),
                                             vmem_limit_bytes=EXPERTS_VMEM_LIMIT),
        name="moe_experts",
    )(pos1, pos2, tile_expert, next_expert, n_active, hp, jnp.zeros((rows,), I32), wg, wu, wd)


def _combine_kernel(pos1_ref, pos2_ref, y_ref, h_ref, p_ref, g_ref, b_ref, *rest, tm, tiled_copy):
    out_ref = rest[0]
    a1, a2, b1, b2, sem = rest[-5:]
    i = pl.program_id(0)
    last = pl.num_programs(0) - 1

    @pl.when(i == 0)
    def _():
        def issue(t, carry):
            _row_copy(y_ref, pos1_ref[t], a1, t, sem.at[0]).start()
            _row_copy(y_ref, pos2_ref[t], a2, t, sem.at[0]).start()
            return carry

        lax.fori_loop(0, tm, issue, 0)

    def step(c1, c2, cur_sem, n1, n2, nxt_sem):
        _tile_wait(y_ref, c1, cur_sem)
        _tile_wait(y_ref, c2, cur_sem)
        base = jnp.minimum(i + 1, last) * tm
        for t in range(tm):
            _row_copy(y_ref, pos1_ref[base + t], n1, t, nxt_sem).start(priority=0)
            _row_copy(y_ref, pos2_ref[base + t], n2, t, nxt_sem).start(priority=DMA_QUEUES - 1)
        ffn = p_ref[:, 0:1] * _load_row_tiled(c1) + p_ref[:, 1:2] * _load_row_tiled(c2)
        out = _layer_norm(DN_ALPHA * h_ref[...] + ffn, g_ref[...], b_ref[...])
        out_ref[...] = out
        if tiled_copy:
            _store_row_tiled(rest[1], out)

        @pl.when(i == last)
        def _():
            _tile_wait(y_ref, n1, nxt_sem)
            _tile_wait(y_ref, n2, nxt_sem)

    @pl.when(i % 2 == 0)
    def _():
        step(a1, a2, sem.at[0], b1, b2, sem.at[1])

    @pl.when(i % 2 == 1)
    def _():
        step(b1, b2, sem.at[1], a1, a2, sem.at[0])


def _combine(y, h, pos1, pos2, gates, g, b, tm, tiled_copy):
    n = h.shape[0]
    tok = lambda i, a, c: (i, 0)
    fixed = lambda i, a, c: (0, 0)
    out_specs = [pl.BlockSpec((tm, D_MODEL), tok)]
    out_shape = [jax.ShapeDtypeStruct((n, D_MODEL), F32)]
    if tiled_copy:
        out_specs.append(pl.BlockSpec((tm * ROW_SPLIT, LANES), tok))
        out_shape.append(jax.ShapeDtypeStruct((n * ROW_SPLIT, LANES), F32))
    return pl.pallas_call(
        functools.partial(_combine_kernel, tm=tm, tiled_copy=tiled_copy),
        grid_spec=pltpu.PrefetchScalarGridSpec(
            num_scalar_prefetch=2,
            grid=(n // tm,),
            in_specs=[pl.BlockSpec(memory_space=pl.ANY),
                      pl.BlockSpec((tm, D_MODEL), tok),
                      pl.BlockSpec((tm, LANES), tok),
                      pl.BlockSpec((1, D_MODEL), fixed), pl.BlockSpec((1, D_MODEL), fixed)],
            out_specs=out_specs,
            scratch_shapes=[pltpu.VMEM((tm * ROW_SPLIT, LANES), F32)] * 4 + [pltpu.SemaphoreType.DMA((2,))]),
        out_shape=out_shape,
        compiler_params=_cparams("arbitrary"),
        name="moe_combine_ln",
    )(pos1, pos2, y, h, gates, g, b)


def _plan_kernel(ints_ref, cnt_ref, pos_ref, meta_ref, *, layer, chunk):
    tr = MOE_ROW_TILE
    n = ints_ref.shape[1]
    ne = MOE_EXPERTS
    tiles = (cnt_ref[...] + (tr - 1.0)) * (1.0 / tr)
    tiles = tiles.astype(I32).astype(F32)
    lower = lax.broadcasted_iota(I32, (ne, ne), 0) >= lax.broadcasted_iota(I32, (ne, ne), 1)
    ends = _dot(jnp.where(lower, 1.0, 0.0).astype(BF16), tiles.astype(BF16))
    start_col = ((ends - tiles) * tr).astype(I32)[:, 0:1]
    r8 = lax.broadcasted_iota(I32, (8, chunk), 0)
    re = lax.broadcasted_iota(I32, (ne, chunk), 0)
    for c in range(n // chunk):
        blk = ints_ref[:, c * chunk:(c + 1) * chunk]
        s1 = jnp.sum(jnp.where(re == blk[0:1], start_col, 0), 0, keepdims=True)
        s2 = jnp.sum(jnp.where(re == blk[1:2], start_col, 0), 0, keepdims=True)
        pos_ref[:, c * chunk:(c + 1) * chunk] = jnp.where(
            r8 == 0, s1 + blk[2:3], jnp.where(r8 == 1, s2 + blk[3:4], 0))
    width = meta_ref.shape[1]
    tile_id = lax.broadcasted_iota(I32, (ne, width), 1).astype(F32)
    te = jnp.sum(jnp.where(ends[:, 0:1] <= tile_id, 1, 0), 0, keepdims=True)
    te = jnp.minimum(te, ne - 1)
    expert = lax.broadcasted_iota(I32, (ne, width), 0)
    later = jnp.logical_and(expert > te, tiles[:, 0:1] > 0.0)
    nxt = jnp.min(jnp.where(later, expert, ne), 0, keepdims=True)
    nxt = jnp.where(nxt < ne, nxt + layer * ne, -1)
    n_used = ends[ne - 1:ne, 0:1].astype(I32)
    rm = lax.broadcasted_iota(I32, (8, width), 0)
    meta_ref[...] = jnp.where(rm == 0, te + layer * ne,
                              jnp.where(rm == 1, n_used, jnp.where(rm == 2, nxt, 0)))


def _plan(ints, cnt, layer, n_tiles):
    n = ints.shape[1]
    width = -(-n_tiles // LANES) * LANES
    return pl.pallas_call(
        functools.partial(_plan_kernel, layer=layer, chunk=2048),
        out_shape=[jax.ShapeDtypeStruct((8, n), I32), jax.ShapeDtypeStruct((8, width), I32)],
        compiler_params=pltpu.CompilerParams(vmem_limit_bytes=VMEM_LIMIT),
        name="moe_plan",
    )(ints, cnt)


def _moe_layer(h, hp, layer, group_w, group_b, expert_w, expert_b, gate_w, up_w, down_w, ln_g, ln_b,
               tiled_copy):
    n = h.shape[0]
    ew = jnp.transpose(expert_w, (1, 0, 2)).reshape(D_MODEL, MOE_EXPERTS)
    w_r = jnp.zeros((D_MODEL, LANES), F32).at[:, 0:MOE_GROUPS].set(group_w).at[:, 8:8 + MOE_EXPERTS].set(ew)
    b_r = jnp.zeros((LANES,), F32).at[0:MOE_GROUPS].set(group_b).at[8:8 + MOE_EXPERTS].set(expert_b.reshape(-1))
    ints, flts, cnt = _router(h, w_r, b_r.reshape(LANES, 1), 512)
    tr = MOE_ROW_TILE
    n_tiles = (2 * n) // tr + MOE_EXPERTS
    pos, meta = _plan(ints, cnt, layer, n_tiles)
    pos1, pos2 = pos[0], pos[1]

    y = _experts(hp, pos1, pos2, meta[0, :n_tiles], meta[2, :n_tiles], meta[1, :1],
                 gate_w.reshape(-1, D_MODEL, MOE_D_FF),
                 up_w.reshape(-1, D_MODEL, MOE_D_FF),
                 down_w.reshape(-1, MOE_D_FF, D_MODEL))
    return _combine(y, h, pos1, pos2, flts,
                    ln_g.reshape(1, -1), ln_b.reshape(1, -1), COMBINE_TILE, tiled_copy)


def _pad_heads(w, axis):
    parts = []
    h0 = 0
    for _, _, nh in ATTN_GROUPS:
        sl = [slice(None)] * w.ndim
        sl[axis] = slice(h0 * ATTN_HEAD_DIM, (h0 + nh) * ATTN_HEAD_DIM)
        part = w[tuple(sl)]
        pad = [(0, 0)] * w.ndim
        pad[axis] = (0, ATTN_SLAB - nh * ATTN_HEAD_DIM)
        parts.append(jnp.pad(part, pad))
        h0 += nh
    return parts


def _in_proj_kernel(x_ref, w_ref, wdt_ref, z_ref, xbc_ref, dt_ref, *, z_tiles):
    j = pl.program_id(1)
    acc = _dot(x_ref[...].astype(BF16), w_ref[...])

    @pl.when(j < z_tiles)
    def _():
        z_ref[...] = acc

    @pl.when(j >= z_tiles)
    def _():
        xbc_ref[...] = acc

    @pl.when(j == 0)
    def _():
        dt_ref[...] = _dot_f32ish(x_ref[...], wdt_ref[...])


def _in_proj(x, w_zx, w_dt, tm, tn):
    m, k = x.shape
    z_tiles = SSM_D_INNER // tn
    n_tiles = (SSM_D_INNER + SSM_CONV_DIM) // tn
    return pl.pallas_call(
        functools.partial(_in_proj_kernel, z_tiles=z_tiles),
        grid=(m // tm, n_tiles),
        in_specs=[pl.BlockSpec((tm, k), lambda i, j: (i, 0)),
                  pl.BlockSpec((k, tn), lambda i, j: (0, j)),
                  pl.BlockSpec((k, LANES), lambda i, j: (0, 0))],
        out_specs=[pl.BlockSpec((tm, tn), lambda i, j: (i, jnp.minimum(j, z_tiles - 1))),
                   pl.BlockSpec((tm, tn), lambda i, j: (i, jnp.maximum(j - z_tiles, 0))),
                   pl.BlockSpec((tm, LANES), lambda i, j: (i, 0))],
        out_shape=[jax.ShapeDtypeStruct((m, SSM_D_INNER), F32),
                   jax.ShapeDtypeStruct((m, SSM_CONV_DIM), F32),
                   jax.ShapeDtypeStruct((m, LANES), F32)],
        compiler_params=_cparams("parallel", "arbitrary"),
        name="ssm_in_proj",
    )(x, w_zx, w_dt)


def _ssd_layer(h, in_w, conv_w, conv_b, dt_bias, a_log, d_skip, norm_w, out_w, ln_g, ln_b, bsz, seq):
    split = SSM_D_INNER + SSM_CONV_DIM
    dt_w = jnp.pad(in_w[:, split:], ((0, 0), (0, LANES - SSM_HEADS)))
    z, xbc, dt_raw = _in_proj(h, in_w.astype(BF16), dt_w, 1024, 1024)
    pad32 = lambda v: jnp.pad(v, (0, LANES - SSM_HEADS)).reshape(1, LANES)
    y = _ssd(z, xbc, dt_raw, conv_w, conv_b.reshape(1, -1), pad32(dt_bias), pad32(a_log),
             jnp.repeat(d_skip, SSM_HEAD_DIM).reshape(1, -1), norm_w.reshape(1, -1), bsz, seq)
    return _matmul_res_ln(y, out_w.astype(BF16), h, ln_g.reshape(1, -1), ln_b.reshape(1, -1), 1024)


def _qkv_group_weights(kv_w, q_w):
    width = ATTN_HEADS * ATTN_HEAD_DIM
    w_q = _pad_heads(q_w * (ATTN_HEAD_DIM ** -0.5), 1)
    w_k = _pad_heads(kv_w[:, :width], 1)
    w_v = _pad_heads(kv_w[:, width:], 1)
    return [jnp.concatenate([w_q[gi], w_k[gi], w_v[gi]], axis=1).astype(BF16) for gi in range(len(ATTN_GROUPS))]


def _attn_layer(h, h8, qkv_w, o_w, rel_bias, ln_g, ln_b, bsz, seq):
    outs, lses = [], []
    h0 = 0
    for gi, (_, dil, nh) in enumerate(ATTN_GROUPS):
        if dil == 1:
            qkv = _matmul(h, qkv_w[gi], BF16, 1024, 3 * ATTN_SLAB)
        else:
            qkv = _qkv_dilated(h8, qkv_w[gi], dil, max(1024, ATTN_BLOCK * dil))
        bias_p, bias_c = _group_bias(rel_bias, h0, nh, dil)
        o, lse = _attention_group(qkv, bias_p, bias_c, gi, dil, nh, bsz, seq)
        outs.extend(o)
        lses.append(lse)
        h0 += nh
    w_o = jnp.concatenate(_pad_heads(o_w, 0), axis=0).astype(BF16)
    return _attn_out(outs, lses, w_o, h, ln_g.reshape(1, -1), ln_b.reshape(1, -1), 1024)


def kernel(x, ssm_in_w, ssm_conv_w, ssm_conv_b, ssm_dt_bias, ssm_a_log, ssm_d, ssm_norm_w, ssm_out_w,
           kv_w, attn_q_w, attn_o_w, rel_bias, moe_group_w, moe_group_b, moe_expert_w, moe_expert_b,
           moe_gate_w, moe_up_w, moe_down_w, ln_g, ln_b):
    bsz, seq, d = x.shape
    h = x.reshape(bsz * seq, d)
    n_ssd = DEPTH // 2
    h8 = None
    for i in range(DEPTH):
        if i < n_ssd:
            h, hp = _ssd_layer(h, ssm_in_w[i], ssm_conv_w[i], ssm_conv_b[i], ssm_dt_bias[i], ssm_a_log[i],
                               ssm_d[i], ssm_norm_w[i], ssm_out_w[i], ln_g[i, 0], ln_b[i, 0], bsz, seq)
        else:
            j = i - n_ssd
            h, hp = _attn_layer(h, h8, _qkv_group_weights(kv_w, attn_q_w[j]), attn_o_w[j], rel_bias,
                                ln_g[i, 0], ln_b[i, 0], bsz, seq)
        feeds_attention = n_ssd <= i + 1 < DEPTH
        res = _moe_layer(h, hp, i, moe_group_w[i], moe_group_b[i], moe_expert_w[i], moe_expert_b[i],
                         moe_gate_w, moe_up_w, moe_down_w, ln_g[i, 1], ln_b[i, 1], feeds_attention)
        h = res[0]
        h8 = res[1] if feeds_attention else None
    return h.reshape(bsz, seq, d)
```

```python
import functools
import math

import numpy as np
import jax
import jax.numpy as jnp
from jax import lax
from jax.experimental import pallas as pl
from jax.experimental.pallas import tpu as pltpu

F32 = jnp.float32
BF16 = jnp.bfloat16
I32 = jnp.int32

D_MODEL = 1024
DEPTH = 2
DN_ALPHA = (2 * DEPTH) ** 0.25
LN_EPS = 1e-5
LOG2_E = math.log2(math.e)

SSM_D_INNER = 2048
SSM_HEAD_DIM = 64
SSM_HEADS = 32
SSM_GROUPS = 4
SSM_STATE = 128
SSM_CONV = 4
SSM_CHUNK = 128
SSM_CONV_DIM = SSM_D_INNER + 2 * SSM_GROUPS * SSM_STATE

ATTN_HEAD_DIM = 64
ATTN_GROUPS = ((128, 1, 6), (512, 4, 5), (2048, 16, 5))
ATTN_HEADS = 16
ATTN_BLOCK = 128
N_BUCKETS = 32
MAX_DISTANCE = 2048
ATTN_SLAB = 384
NEG_BIG = -1e30

MOE_GROUPS = 4
MOE_EPG = 8
MOE_EXPERTS = MOE_GROUPS * MOE_EPG
MOE_D_FF = 512
MOE_ROW_TILE = 256
COMBINE_TILE = 512

LANES = 128
DMA_QUEUES = 2
VMEM_LIMIT = 48 * 1024 * 1024
EXPERTS_VMEM_LIMIT = 56 * 1024 * 1024


def _cparams(*sem):
    return pltpu.CompilerParams(dimension_semantics=sem, vmem_limit_bytes=VMEM_LIMIT)


def _layer_norm(x, g, b):
    mu = jnp.mean(x, -1, keepdims=True)
    xc = x - mu
    var = jnp.mean(xc * xc, -1, keepdims=True)
    return xc * lax.rsqrt(var + LN_EPS) * g + b


def _split2(x):
    hi = x.astype(BF16)
    lo = (x - hi.astype(F32)).astype(BF16)
    return hi, lo


def _dot(a, b):
    return jnp.dot(a, b, preferred_element_type=F32)


def _dot_f32ish(a, b):
    ah, al = _split2(a)
    bh, bl = _split2(b)
    return _dot(ah, bh) + _dot(al, bh) + _dot(ah, bl)


def _mm_kernel(a_ref, b_ref, o_ref):
    o_ref[...] = _dot(a_ref[...].astype(BF16), b_ref[...]).astype(o_ref.dtype)


def _matmul(a, b, out_dtype, tm, tn):
    m, k = a.shape
    nc = b.shape[1]
    return pl.pallas_call(
        _mm_kernel,
        grid=(m // tm, nc // tn),
        in_specs=[pl.BlockSpec((tm, k), lambda i, j: (i, 0)),
                  pl.BlockSpec((k, tn), lambda i, j: (0, j))],
        out_specs=pl.BlockSpec((tm, tn), lambda i, j: (i, j)),
        out_shape=jax.ShapeDtypeStruct((m, nc), out_dtype),
        compiler_params=_cparams("parallel", "parallel"),
        name="matmul",
    )(a, b)


def _mm3_kernel(a_ref, b_ref, o_ref):
    o_ref[...] = _dot_f32ish(a_ref[...], b_ref[...])


def _matmul_f32ish(a, b, tm):
    m, k = a.shape
    nc = b.shape[1]
    return pl.pallas_call(
        _mm3_kernel,
        grid=(m // tm,),
        in_specs=[pl.BlockSpec((tm, k), lambda i: (i, 0)),
                  pl.BlockSpec((k, nc), lambda i: (0, 0))],
        out_specs=pl.BlockSpec((tm, nc), lambda i: (i, 0)),
        out_shape=jax.ShapeDtypeStruct((m, nc), F32),
        compiler_params=_cparams("parallel"),
        name="matmul_f32ish",
    )(a, b)


ROW_SPLIT = D_MODEL // LANES


def _store_row_tiled(ref, val):
    rows = val.shape[0]
    for c in range(ROW_SPLIT):
        ref[pl.ds(c, rows, stride=ROW_SPLIT), :] = val[:, c * LANES:(c + 1) * LANES]


def _load_row_tiled(ref):
    rows = ref.shape[0] // ROW_SPLIT
    return jnp.concatenate([ref[pl.ds(c, rows, stride=ROW_SPLIT), :] for c in range(ROW_SPLIT)], axis=1)


HALF = D_MODEL // 2
U32 = jnp.uint32
HI16 = 0xFFFF0000


def _pack_bf16_pairs(x):
    bits = lax.bitcast_convert_type(x.astype(BF16).astype(F32), U32)
    return (bits[:, :HALF] >> 16) | (bits[:, HALF:] & U32(HI16))


def _unpack_bf16_pairs(w):
    lo = lax.bitcast_convert_type(w << 16, F32).astype(BF16)
    hi = lax.bitcast_convert_type(w & U32(HI16), F32).astype(BF16)
    return lo, hi


def _mm_ln_kernel(a_ref, w_ref, h_ref, g_ref, b_ref, o_ref, op_ref):
    acc = _dot(a_ref[...], w_ref[...])
    out = _layer_norm(DN_ALPHA * h_ref[...] + acc, g_ref[...], b_ref[...])
    o_ref[...] = out
    op_ref[...] = _pack_bf16_pairs(out)


def _matmul_res_ln(a, w, h, g, b, tm):
    m, k = a.shape
    d = w.shape[1]
    return pl.pallas_call(
        _mm_ln_kernel,
        grid=(m // tm,),
        in_specs=[pl.BlockSpec((tm, k), lambda i: (i, 0)),
                  pl.BlockSpec((k, d), lambda i: (0, 0)),
                  pl.BlockSpec((tm, d), lambda i: (i, 0)),
                  pl.BlockSpec((1, d), lambda i: (0, 0)),
                  pl.BlockSpec((1, d), lambda i: (0, 0))],
        out_specs=[pl.BlockSpec((tm, d), lambda i: (i, 0)),
                   pl.BlockSpec((tm, HALF), lambda i: (i, 0))],
        out_shape=[jax.ShapeDtypeStruct((m, d), F32),
                   jax.ShapeDtypeStruct((m, HALF), U32)],
        compiler_params=_cparams("parallel"),
        name="matmul_res_ln",
    )(a, w, h, g, b)


SSD_CHUNKS_PER_STEP = 2


def _ssd_kernel(z_ref, xbc_ref, dt_ref, cw_ref, cb_ref, dtb_ref, alog_ref, dsk_ref, nw_ref,
                y_ref, xe_ref, st_ref):
    @pl.when(pl.program_id(1) == 0)
    def _():
        xe_ref[:, 0:8, :] = jnp.zeros((xe_ref.shape[0], 8, LANES), F32)
        st_ref[...] = jnp.zeros_like(st_ref)

    for sub in range(SSD_CHUNKS_PER_STEP):
        _ssd_chunk(pl.ds(sub * SSM_CHUNK, SSM_CHUNK), z_ref, xbc_ref, dt_ref, cw_ref, cb_ref, dtb_ref,
                   alog_ref, dsk_ref, nw_ref, y_ref, xe_ref, st_ref)


def _ssd_chunk(rows, z_ref, xbc_ref, dt_ref, cw_ref, cb_ref, dtb_ref, alog_ref, dsk_ref, nw_ref,
               y_ref, xe_ref, st_ref):
    q = SSM_CHUNK
    w = cw_ref[...]
    bias = cb_ref[...]
    act = []
    for c in range(SSM_CONV_DIM // LANES):
        cols = slice(c * LANES, (c + 1) * LANES)
        u = xbc_ref[rows, cols]
        xe_ref[c, 8:8 + q, :] = u
        conv = (bias[:, cols] + w[3:4, cols] * u + w[2:3, cols] * xe_ref[c, 7:7 + q, :]
                + w[1:2, cols] * xe_ref[c, 6:6 + q, :] + w[0:1, cols] * xe_ref[c, 5:5 + q, :])
        xe_ref[c, 0:8, :] = xe_ref[c, q:q + 8, :]
        act.append(conv * jax.nn.sigmoid(conv))

    pre = dt_ref[rows, :] + dtb_ref[...]
    dt = jnp.maximum(pre, 0.0) + jnp.log(1.0 + jnp.exp(-jnp.abs(pre)))
    adt = dt * (-jnp.exp(alog_ref[...]) * LOG2_E)

    row = lax.broadcasted_iota(I32, (q, q), 0)
    col = lax.broadcasted_iota(I32, (q, q), 1)
    tril = row >= col
    tri_b = jnp.where(tril, 1.0, 0.0).astype(BF16)
    a_hi = adt.astype(BF16)
    r1 = adt - a_hi.astype(F32)
    a_mid = r1.astype(BF16)
    a_lo = (r1 - a_mid.astype(F32)).astype(BF16)
    acs = _dot(tri_b, a_hi) + _dot(tri_b, a_mid) + _dot(tri_b, a_lo)
    acs_t = acs.T
    acs_dt_t = acs_t - jnp.log(dt.T) * LOG2_E
    eacs = jnp.exp2(acs)
    left = col < SSM_HEAD_DIM

    for g in range(SSM_GROUPS):
        b0 = SSM_D_INNER + g * SSM_STATE
        c0 = SSM_D_INNER + SSM_GROUPS * SSM_STATE + g * SSM_STATE
        bm = act[b0 // LANES]
        cm = act[c0 // LANES].astype(BF16)
        cb = lax.dot_general(cm, bm.astype(BF16), (((1,), (1,)), ((), ())),
                             preferred_element_type=F32)
        bm_t = bm.T
        gs = g * 512
        y_off = _dot(cm, st_ref[:, gs:gs + 512].astype(BF16))
        slabs = []
        for pr in range(4):
            ha = g * 8 + pr * 2
            hb = ha + 1
            cs = gs + pr * LANES
            x2 = act[cs // LANES]
            x2b = x2.astype(BF16)
            ys, ups = [], []
            for h in (ha, hb):
                a_col = acs[:, h:h + 1]
                a_src = acs_dt_t[h:h + 1, :]
                decay = jnp.where(tril, jnp.exp2(a_col - a_src), 0.0)
                ys.append(_dot((cb * decay).astype(BF16), x2b))
                to_end = jnp.exp2(acs_t[h:h + 1, q - 1:q] - a_src)
                ups.append(_dot((bm_t * to_end).astype(BF16), x2b))
            y_diag = jnp.where(left, ys[0], ys[1])
            upd = jnp.where(left, ups[0], ups[1])
            e2 = jnp.where(left, eacs[:, ha:ha + 1], eacs[:, hb:hb + 1])
            cd = jnp.where(left[0:1, :], eacs[q - 1:q, ha:ha + 1], eacs[q - 1:q, hb:hb + 1])
            y2 = y_diag + y_off[:, pr * LANES:(pr + 1) * LANES] * e2 + dsk_ref[:, cs:cs + LANES] * x2
            st_ref[:, cs:cs + LANES] = st_ref[:, cs:cs + LANES] * cd + upd
            slabs.append(y2)
        yg = jnp.concatenate(slabs, axis=1)
        zg = z_ref[rows, gs:gs + 512]
        yg = yg * (zg * jax.nn.sigmoid(zg))
        ms = jnp.mean(yg * yg, -1, keepdims=True)
        y_ref[rows, gs:gs + 512] = (yg * lax.rsqrt(ms + LN_EPS) * nw_ref[:, gs:gs + 512]).astype(y_ref.dtype)


def _ssd(z, xbc, dt_raw, conv_w, conv_b, dt_bias, a_log, d_rep, norm_w, bsz, seq):
    n = z.shape[0]
    q = SSM_CHUNK * SSD_CHUNKS_PER_STEP
    nchunk = seq // q
    tok = lambda b, c: (b * nchunk + c, 0)
    fixed = lambda b, c: (0, 0)
    return pl.pallas_call(
        _ssd_kernel,
        grid=(bsz, nchunk),
        in_specs=[pl.BlockSpec((q, SSM_D_INNER), tok),
                  pl.BlockSpec((q, SSM_CONV_DIM), tok),
                  pl.BlockSpec((q, LANES), tok),
                  pl.BlockSpec((SSM_CONV, SSM_CONV_DIM), fixed),
                  pl.BlockSpec((1, SSM_CONV_DIM), fixed),
                  pl.BlockSpec((1, LANES), fixed),
                  pl.BlockSpec((1, LANES), fixed),
                  pl.BlockSpec((1, SSM_D_INNER), fixed),
                  pl.BlockSpec((1, SSM_D_INNER), fixed)],
        out_specs=pl.BlockSpec((q, SSM_D_INNER), tok),
        out_shape=jax.ShapeDtypeStruct((n, SSM_D_INNER), BF16),
        scratch_shapes=[pltpu.VMEM((SSM_CONV_DIM // LANES, SSM_CHUNK + 8, LANES), F32),
                        pltpu.VMEM((SSM_STATE, SSM_D_INNER), F32)],
        compiler_params=_cparams("parallel", "arbitrary"),
        name="ssd_chunk",
    )(z, xbc, dt_raw, conv_w, conv_b, dt_bias, a_log, d_rep, norm_w)


def _residue_major_pieces(h8_ref, n_tokens, dil):
    span = ATTN_BLOCK * dil
    pieces = []
    for blk in range(n_tokens // span):
        for r in range(dil):
            first = (blk * span + r) * ROW_SPLIT
            pieces.append(jnp.concatenate(
                [h8_ref[pl.ds(first + c, ATTN_BLOCK, stride=ROW_SPLIT * dil), :] for c in range(ROW_SPLIT)],
                axis=1).astype(BF16))
    return pieces


def _qkv_dilated_kernel(h8_ref, w_ref, o_ref, *, dil):
    per_dot = 4
    rows = _residue_major_pieces(h8_ref, o_ref.shape[0], dil)
    for k in range(0, len(rows), per_dot):
        x = jnp.concatenate(rows[k:k + per_dot], axis=0)
        o_ref[k * ATTN_BLOCK:(k + per_dot) * ATTN_BLOCK, :] = _dot(x, w_ref[...]).astype(o_ref.dtype)


def _qkv_dilated(h8, w, dil, tm):
    n = h8.shape[0] // ROW_SPLIT
    nc = w.shape[1]
    return pl.pallas_call(
        functools.partial(_qkv_dilated_kernel, dil=dil),
        grid=(n // tm,),
        in_specs=[pl.BlockSpec((tm * ROW_SPLIT, LANES), lambda i: (i, 0)),
                  pl.BlockSpec((D_MODEL, nc), lambda i: (0, 0))],
        out_specs=pl.BlockSpec((tm, nc), lambda i: (i, 0)),
        out_shape=jax.ShapeDtypeStruct((n, nc), BF16),
        compiler_params=_cparams("parallel"),
        name=f"qkv_dil{dil}",
    )(h8, w)


def _attn_kernel(*refs, nh, dil, has_prev, blocks):
    span = ATTN_BLOCK * dil
    if has_prev:
        q_ref, kp_ref, kc_ref, vp_ref, vc_ref, bp_ref, bc_ref = refs[:7]
        out_refs = refs[7:]
        first_pen = jnp.where(pl.program_id(1) > 0, 0.0, NEG_BIG)
    else:
        q_ref, kc_ref, vc_ref, bc_ref = refs[:4]
        out_refs = refs[4:]
    s_scr, p_scr, max_scr, den_scr = out_refs[-4:]
    o_refs, lse_ref = out_refs[:-5], out_refs[-5]
    ones = jnp.ones((s_scr.shape[2], LANES), BF16)
    lane = lax.broadcasted_iota(I32, (ATTN_BLOCK, LANES), 1)
    left = lane < ATTN_HEAD_DIM
    zero = jnp.zeros((), BF16)
    nt = (((1,), (1,)), ((), ()))

    def residue(blk, r):
        rows = pl.ds(pl.multiple_of(blk * span + r * ATTN_BLOCK, ATTN_BLOCK), ATTN_BLOCK)
        dst = pl.ds(blk * span + r, ATTN_BLOCK, stride=dil)
        if has_prev and blk == 0:
            kprev, vprev, pen = kp_ref, vp_ref, first_pen
            prows = pl.ds(pl.multiple_of(r * ATTN_BLOCK, ATTN_BLOCK), ATTN_BLOCK)
        elif has_prev:
            kprev, vprev, pen = kc_ref, vc_ref, 0.0
            prows = pl.ds(pl.multiple_of((blk - 1) * span + r * ATTN_BLOCK, ATTN_BLOCK), ATTN_BLOCK)
        for hh in range(nh):
            cols = pl.ds(hh // 2 * LANES, LANES)
            qm = jnp.where(left if hh % 2 == 0 else ~left, q_ref[rows, cols], zero)
            s_c = lax.dot_general(qm, kc_ref[rows, cols], nt, preferred_element_type=F32) + bc_ref[hh]
            if has_prev:
                s_p = lax.dot_general(qm, kprev[prows, cols], nt, preferred_element_type=F32)
                s_scr[hh, :, 0:ATTN_BLOCK] = s_p + (bp_ref[hh] + pen)
                s_scr[hh, :, ATTN_BLOCK:] = s_c
            else:
                s_scr[hh] = s_c
        for hh in range(nh):
            s = s_scr[hh]
            m = jnp.max(s, -1, keepdims=True)
            p = jnp.exp(s - m)
            p_scr[hh] = p.astype(BF16)
            max_scr[hh] = m
            if not has_prev:
                den_scr[hh] = jnp.sum(p, -1, keepdims=True)
        lse_sum = jnp.zeros((ATTN_BLOCK, 1), F32)
        for pr in range(ATTN_SLAB // LANES):
            cols = pl.ds(pr * LANES, LANES)
            halves = []
            for hh in (pr * 2, pr * 2 + 1):
                if hh >= nh:
                    halves.append(jnp.zeros((ATTN_BLOCK, LANES), F32))
                    continue
                if has_prev:
                    den = _dot(p_scr[hh], ones)
                    den1 = den[:, 0:1]
                    o = (_dot(p_scr[hh, :, 0:ATTN_BLOCK], vprev[prows, cols])
                         + _dot(p_scr[hh, :, ATTN_BLOCK:], vc_ref[rows, cols]))
                else:
                    den = den1 = den_scr[hh]
                    o = _dot(p_scr[hh], vc_ref[rows, cols])
                halves.append(o * (1.0 / den))
                lse_sum = lse_sum + (max_scr[hh] + jnp.log(den1))
            o_refs[pr][dst, :] = jnp.where(left, halves[0], halves[1])
        lse_ref[dst, :] = jnp.broadcast_to(lse_sum * (1.0 / nh), (ATTN_BLOCK, LANES))

    for blk in range(blocks):
        def body(r, carry, blk=blk):
            residue(blk, r)
            return carry

        lax.fori_loop(0, dil, body, 0)


def _attention_group(qkv, bias_p, bias_c, gi, dil, nh, bsz, seq):
    span = ATTN_BLOCK * dil
    nb = seq // span
    has_prev = nb > 1
    blocks = max(1, 512 // span)
    steps = nb // blocks
    blk = (blocks * span, ATTN_SLAB)
    cur = lambda which: (lambda b, n: (b * steps + n, which))
    prev = lambda which: (lambda b, n: (b * nb + jnp.maximum(n * blocks - 1, 0), which))
    fixed = lambda b, n: (0, 0, 0)
    tab = pl.BlockSpec((nh, ATTN_BLOCK, ATTN_BLOCK), fixed)
    if has_prev:
        pblk = (span, ATTN_SLAB)
        in_specs = [pl.BlockSpec(blk, cur(0)), pl.BlockSpec(pblk, prev(1)), pl.BlockSpec(blk, cur(1)),
                    pl.BlockSpec(pblk, prev(2)), pl.BlockSpec(blk, cur(2)), tab, tab]
        args = (qkv, qkv, qkv, qkv, qkv, bias_p, bias_c)
    else:
        in_specs = [pl.BlockSpec(blk, cur(0)), pl.BlockSpec(blk, cur(1)), pl.BlockSpec(blk, cur(2)), tab]
        args = (qkv, qkv, qkv, bias_c)
    n_out = ATTN_SLAB // LANES + 1
    keys = 2 * ATTN_BLOCK if has_prev else ATTN_BLOCK
    outs = pl.pallas_call(
        functools.partial(_attn_kernel, nh=nh, dil=dil, has_prev=has_prev, blocks=blocks),
        grid=(bsz, steps),
        in_specs=in_specs,
        out_specs=[pl.BlockSpec((blocks * span, LANES), lambda b, n: (b * steps + n, 0))] * n_out,
        out_shape=[jax.ShapeDtypeStruct((bsz * seq, LANES), F32)] * n_out,
        scratch_shapes=[pltpu.VMEM((nh, ATTN_BLOCK, keys), F32),
                        pltpu.VMEM((nh, ATTN_BLOCK, keys), BF16),
                        pltpu.VMEM((nh, ATTN_BLOCK, 1), F32),
                        pltpu.VMEM((nh, ATTN_BLOCK, 1), F32)],
        compiler_params=_cparams("parallel", "parallel"),
        name=f"dilated_attn_g{gi}",
    )(*args)
    return outs[:-1], outs[-1]


def _attn_out_kernel(*refs):
    n_pairs = ATTN_SLAB // LANES
    n_groups = len(ATTN_GROUPS)
    o_refs = refs[:n_groups * n_pairs]
    l_refs = refs[n_groups * n_pairs:n_groups * (n_pairs + 1)]
    w_ref, h_ref, g_ref, b_ref, out_ref, outp_ref = refs[n_groups * (n_pairs + 1):]
    ls = [r[:, 0:1] for r in l_refs]
    m = jnp.maximum(jnp.maximum(ls[0], ls[1]), ls[2])
    es = [jnp.exp(v - m) for v in ls]
    scale = n_groups / (es[0] + es[1] + es[2])
    slabs = []
    for gi in range(n_groups):
        wt = es[gi] * scale
        slabs.extend((r[...] * wt).astype(BF16) for r in o_refs[gi * n_pairs:(gi + 1) * n_pairs])
    acc = _dot(jnp.concatenate(slabs, axis=1), w_ref[...])
    out = _layer_norm(DN_ALPHA * h_ref[...] + acc, g_ref[...], b_ref[...])
    out_ref[...] = out
    outp_ref[...] = _pack_bf16_pairs(out)


def _attn_out(outs, lses, w, h, g, b, tm):
    n = h.shape[0]
    tok = lambda i: (i, 0)
    fixed2 = lambda i: (0, 0)
    return pl.pallas_call(
        _attn_out_kernel,
        grid=(n // tm,),
        in_specs=[pl.BlockSpec((tm, LANES), tok)] * (len(outs) + len(lses))
                 + [pl.BlockSpec((len(ATTN_GROUPS) * ATTN_SLAB, D_MODEL), lambda i: (0, 0)),
                    pl.BlockSpec((tm, D_MODEL), tok),
                    pl.BlockSpec((1, D_MODEL), fixed2), pl.BlockSpec((1, D_MODEL), fixed2)],
        out_specs=[pl.BlockSpec((tm, D_MODEL), tok), pl.BlockSpec((tm, HALF), tok)],
        out_shape=[jax.ShapeDtypeStruct((n, D_MODEL), F32),
                   jax.ShapeDtypeStruct((n, HALF), U32)],
        compiler_params=_cparams("parallel"),
        name="attn_out_ln",
    )(*outs, *lses, w, h, g, b)


def _t5_bucket(dist):
    max_exact = N_BUCKETS // 2
    n = np.maximum(dist, 1).astype(np.float64)
    large = max_exact + (np.log(n / max_exact) / np.log(MAX_DISTANCE / max_exact)
                         * (N_BUCKETS - max_exact)).astype(np.int32)
    large = np.minimum(large, N_BUCKETS - 1)
    return np.where(dist < max_exact, dist, large).astype(np.int32)


def _group_bias(rel_bias, h0, nh, dil):
    qi = np.arange(ATTN_BLOCK)[:, None]
    ki = np.arange(ATTN_BLOCK)[None, :]
    tabs = []
    for delta, band in ((qi + ATTN_BLOCK - ki, ki >= qi), (qi - ki, ki <= qi)):
        bucket = _t5_bucket(np.clip(delta, 0, None) * dil)
        onehot = (bucket[..., None] == np.arange(N_BUCKETS)).astype(np.float32)
        t = jnp.einsum("qkb,bh->hqk", onehot, rel_bias[:, h0:h0 + nh], precision=lax.Precision.HIGHEST)
        tabs.append(jnp.where(band[None], t, NEG_BIG).astype(F32))
    return tabs


def _router_kernel(h_ref, w_ref, b_ref, ints_ref, flts_ref, cnt_ref, carry_ref, before_ref, *, tm):
    @pl.when(pl.program_id(0) == 0)
    def _():
        carry_ref[...] = jnp.zeros_like(carry_ref)
        tr = lax.broadcasted_iota(I32, (tm, tm), 0)
        tc = lax.broadcasted_iota(I32, (tm, tm), 1)
        before_ref[...] = jnp.where(tr < tc, 1.0, 0.0).astype(BF16)

    lt = _dot_f32ish(h_ref[...], w_ref[...]).T + b_ref[...]
    gl = lt[0:MOE_GROUPS]
    r4 = lax.broadcasted_iota(I32, (MOE_GROUPS, tm), 0)
    gmax = jnp.max(gl, 0, keepdims=True)
    gidx = jnp.min(jnp.where(gl == gmax, r4, MOE_GROUPS), 0, keepdims=True)
    gval = 1.0 / jnp.sum(jnp.exp(gl - gmax), 0, keepdims=True)

    esel = jnp.zeros((MOE_EPG, tm), F32)
    for g in range(MOE_GROUPS):
        esel = jnp.where(gidx == g, lt[8 + g * MOE_EPG:8 + (g + 1) * MOE_EPG], esel)
    r8 = lax.broadcasted_iota(I32, (MOE_EPG, tm), 0)
    v1 = jnp.max(esel, 0, keepdims=True)
    i1 = jnp.min(jnp.where(esel == v1, r8, MOE_EPG), 0, keepdims=True)
    rest = jnp.where(r8 == i1, -jnp.inf, esel)
    v2 = jnp.max(rest, 0, keepdims=True)
    i2 = jnp.min(jnp.where(rest == v2, r8, MOE_EPG), 0, keepdims=True)
    t = jnp.exp(v2 - v1)
    p1 = gval / (1.0 + t)
    p2 = p1 * t
    e1 = gidx * MOE_EPG + i1
    e2 = gidx * MOE_EPG + i2

    r32 = lax.broadcasted_iota(I32, (MOE_EXPERTS, tm), 0)
    oh1 = r32 == e1
    oh2 = r32 == e2
    oh = jnp.where(oh1 | oh2, 1.0, 0.0)
    base = _dot(oh.astype(BF16), before_ref[...]) + carry_ref[:, 0:1]
    rank1 = jnp.sum(jnp.where(oh1, base, 0.0), 0, keepdims=True).astype(I32)
    rank2 = jnp.sum(jnp.where(oh2, base, 0.0), 0, keepdims=True).astype(I32)
    carry_ref[...] = carry_ref[...] + jnp.sum(oh, 1, keepdims=True)
    cnt_ref[...] = carry_ref[...]

    ints_ref[...] = jnp.where(r8 == 0, e1, jnp.where(r8 == 1, e2, jnp.where(r8 == 2, rank1,
                              jnp.where(r8 == 3, rank2, 0))))
    r128 = lax.broadcasted_iota(I32, (LANES, tm), 0)
    flts_ref[...] = jnp.where(r128 == 0, p1, jnp.where(r128 == 1, p2, 0.0)).T


def _router(h, w_r, b_r, tm):
    n = h.shape[0]
    return pl.pallas_call(
        functools.partial(_router_kernel, tm=tm),
        grid=(n // tm,),
        in_specs=[pl.BlockSpec((tm, D_MODEL), lambda i: (i, 0)),
                  pl.BlockSpec((D_MODEL, LANES), lambda i: (0, 0)),
                  pl.BlockSpec((LANES, 1), lambda i: (0, 0))],
        out_specs=[pl.BlockSpec((8, tm), lambda i: (0, i)),
                   pl.BlockSpec((tm, LANES), lambda i: (i, 0)),
                   pl.BlockSpec((MOE_EXPERTS, LANES), lambda i: (0, 0))],
        out_shape=[jax.ShapeDtypeStruct((8, n), I32),
                   jax.ShapeDtypeStruct((n, LANES), F32),
                   jax.ShapeDtypeStruct((MOE_EXPERTS, LANES), F32)],
        scratch_shapes=[pltpu.VMEM((MOE_EXPERTS, LANES), F32), pltpu.VMEM((tm, tm), BF16)],
        compiler_params=_cparams("arbitrary"),
        name="moe_router",
    )(h, w_r, b_r)


def _row_copy(src_ref, s, dst_ref, d, sem):
    def first(r):
        return r * ROW_SPLIT if isinstance(r, int) else pl.multiple_of(r * ROW_SPLIT, ROW_SPLIT)

    return pltpu.make_async_copy(src_ref.at[pl.ds(first(s), ROW_SPLIT), :],
                                 dst_ref.at[pl.ds(first(d), ROW_SPLIT), :], sem)


def _tile_wait(src_ref, dst_ref, sem):
    pltpu.make_async_copy(src_ref.at[pl.ds(0, dst_ref.shape[0]), :], dst_ref, sem).wait()


def _experts_kernel(pos1_ref, pos2_ref, te_ref, nxt_ref, na_ref, hp_hbm, zeros_ref, wg_ref, wu_ref, wd_ref,
                    y_ref, hp_ref, xa, xb, wg_f, wu_f, wd_f, wg_s, wu_s, wd_s, row_tok, slot_ref,
                    sem, hsem, wsem):
    i = pl.program_id(0)
    n_tiles = pl.num_programs(0)
    tr = MOE_ROW_TILE
    active = i < na_ref[0]
    changed = jnp.logical_or(i == 0, te_ref[i] != te_ref[jnp.maximum(i - 1, 0)])

    def weight_copies(expert, slot):
        return [pltpu.make_async_copy(src.at[expert], dst.at[slot], wsem.at[slot])
                for src, dst in ((wg_ref, wg_f), (wu_ref, wu_f), (wd_ref, wd_f))]

    @pl.when(i == 0)
    def _():
        slot_ref[0] = 0
        rows_in = pltpu.make_async_copy(hp_hbm, hp_ref, hsem)
        rows_in.start()
        for c in weight_copies(te_ref[0], 0):
            c.start()
        clear = pltpu.make_async_copy(zeros_ref, row_tok, sem)
        clear.start()
        clear.wait()

        def invert(t, carry):
            row_tok[pos1_ref[t]] = t
            row_tok[pos2_ref[t]] = t
            return carry

        lax.fori_loop(0, pos1_ref.shape[0], invert, 0, unroll=16)
        rows_in.wait()

        def pick(r, carry):
            xa[pl.ds(r, 1), :] = hp_ref[pl.ds(row_tok[r], 1), :]
            return carry

        lax.fori_loop(0, tr, pick, 0, unroll=8)

    @pl.when(jnp.logical_and(active, changed))
    def _():
        slot = slot_ref[0]
        for c in weight_copies(te_ref[i], slot):
            c.wait()
        following = nxt_ref[i]

        @pl.when(following >= 0)
        def _():
            for c in weight_copies(following, 1 - slot):
                c.start()

        wg_s[...] = wg_f[slot].astype(BF16)
        wu_s[...] = wu_f[slot].astype(BF16)
        wd_s[...] = wd_f[slot].astype(BF16)
        slot_ref[0] = 1 - slot

    def step(cur, nxt):
        base = jnp.minimum(i + 1, n_tiles - 1) * tr
        for r in range(tr):
            nxt[r:r + 1, :] = hp_ref[pl.ds(row_tok[base + r], 1), :]
        lo, hi = _unpack_bf16_pairs(cur[...])
        gate = _dot(lo, wg_s[0:HALF, :]) + _dot(hi, wg_s[HALF:, :])
        up = _dot(lo, wu_s[0:HALF, :]) + _dot(hi, wu_s[HALF:, :])
        hid = (gate * jax.nn.sigmoid(gate) * up).astype(BF16)
        _store_row_tiled(y_ref, _dot(hid, wd_s[...]))

    even = i % 2 == 0

    @pl.when(jnp.logical_and(active, even))
    def _():
        step(xa, xb)

    @pl.when(jnp.logical_and(active, jnp.logical_not(even)))
    def _():
        step(xb, xa)

    @pl.when(jnp.logical_not(active))
    def _():
        y_ref[...] = jnp.zeros_like(y_ref)


def _experts(hp, pos1, pos2, tile_expert, next_expert, n_active, wg, wu, wd):
    tr = MOE_ROW_TILE
    n_tiles = tile_expert.shape[0]
    rows = n_tiles * tr
    hbm = pl.BlockSpec(memory_space=pl.ANY)
    return pl.pallas_call(
        _experts_kernel,
        grid_spec=pltpu.PrefetchScalarGridSpec(
            num_scalar_prefetch=5,
            grid=(n_tiles,),
            in_specs=[hbm, hbm, hbm, hbm, hbm],
            out_specs=pl.BlockSpec((tr * ROW_SPLIT, LANES), lambda i, *_: (i, 0)),
            scratch_shapes=[pltpu.VMEM(hp.shape, U32),
                            pltpu.VMEM((tr, HALF), U32),
                            pltpu.VMEM((tr, HALF), U32),
                            pltpu.VMEM((2, D_MODEL, MOE_D_FF), F32),
                            pltpu.VMEM((2, D_MODEL, MOE_D_FF), F32),
                            pltpu.VMEM((2, MOE_D_FF, D_MODEL), F32),
                            pltpu.VMEM((D_MODEL, MOE_D_FF), BF16),
                            pltpu.VMEM((D_MODEL, MOE_D_FF), BF16),
                            pltpu.VMEM((MOE_D_FF, D_MODEL), BF16),
                            pltpu.SMEM((rows,), I32),
                            pltpu.SMEM((1,), I32),
                            pltpu.SemaphoreType.DMA(()),
                            pltpu.SemaphoreType.DMA(()),
                            pltpu.SemaphoreType.DMA((2,))]),
        out_shape=jax.ShapeDtypeStruct((rows * ROW_SPLIT, LANES), F32),
        compiler_params=pltpu.CompilerParams(dimension_semantics=("arbitrary",),
                                             vmem_limit_bytes=EXPERTS_VMEM_LIMIT),
        name="moe_experts",
    )(pos1, pos2, tile_expert, next_expert, n_active, hp, jnp.zeros((rows,), I32), wg, wu, wd)


def _combine_kernel(pos1_ref, pos2_ref, y_ref, h_ref, p_ref, g_ref, b_ref, *rest, tm, tiled_copy):
    out_ref = rest[0]
    a1, a2, b1, b2, sem = rest[-5:]
    i = pl.program_id(0)
    last = pl.num_programs(0) - 1

    @pl.when(i == 0)
    def _():
        def issue(t, carry):
            _row_copy(y_ref, pos1_ref[t], a1, t, sem.at[0]).start()
            _row_copy(y_ref, pos2_ref[t], a2, t, sem.at[0]).start()
            return carry

        lax.fori_loop(0, tm, issue, 0)

    def step(c1, c2, cur_sem, n1, n2, nxt_sem):
        _tile_wait(y_ref, c1, cur_sem)
        _tile_wait(y_ref, c2, cur_sem)
        base = jnp.minimum(i + 1, last) * tm
        for t in range(tm):
            _row_copy(y_ref, pos1_ref[base + t], n1, t, nxt_sem).start(priority=0)
            _row_copy(y_ref, pos2_ref[base + t], n2, t, nxt_sem).start(priority=DMA_QUEUES - 1)
        ffn = p_ref[:, 0:1] * _load_row_tiled(c1) + p_ref[:, 1:2] * _load_row_tiled(c2)
        out = _layer_norm(DN_ALPHA * h_ref[...] + ffn, g_ref[...], b_ref[...])
        out_ref[...] = out
        if tiled_copy:
            _store_row_tiled(rest[1], out)

        @pl.when(i == last)
        def _():
            _tile_wait(y_ref, n1, nxt_sem)
            _tile_wait(y_ref, n2, nxt_sem)

    @pl.when(i % 2 == 0)
    def _():
        step(a1, a2, sem.at[0], b1, b2, sem.at[1])

    @pl.when(i % 2 == 1)
    def _():
        step(b1, b2, sem.at[1], a1, a2, sem.at[0])


def _combine(y, h, pos1, pos2, gates, g, b, tm, tiled_copy):
    n = h.shape[0]
    tok = lambda i, a, c: (i, 0)
    fixed = lambda i, a, c: (0, 0)
    out_specs = [pl.BlockSpec((tm, D_MODEL), tok)]
    out_shape = [jax.ShapeDtypeStruct((n, D_MODEL), F32)]
    if tiled_copy:
        out_specs.append(pl.BlockSpec((tm * ROW_SPLIT, LANES), tok))
        out_shape.append(jax.ShapeDtypeStruct((n * ROW_SPLIT, LANES), F32))
    return pl.pallas_call(
        functools.partial(_combine_kernel, tm=tm, tiled_copy=tiled_copy),
        grid_spec=pltpu.PrefetchScalarGridSpec(
            num_scalar_prefetch=2,
            grid=(n // tm,),
            in_specs=[pl.BlockSpec(memory_space=pl.ANY),
                      pl.BlockSpec((tm, D_MODEL), tok),
                      pl.BlockSpec((tm, LANES), tok),
                      pl.BlockSpec((1, D_MODEL), fixed), pl.BlockSpec((1, D_MODEL), fixed)],
            out_specs=out_specs,
            scratch_shapes=[pltpu.VMEM((tm * ROW_SPLIT, LANES), F32)] * 4 + [pltpu.SemaphoreType.DMA((2,))]),
        out_shape=out_shape,
        compiler_params=_cparams("arbitrary"),
        name="moe_combine_ln",
    )(pos1, pos2, y, h, gates, g, b)


def _plan_kernel(ints_ref, cnt_ref, pos_ref, meta_ref, *, layer, chunk):
    tr = MOE_ROW_TILE
    n = ints_ref.shape[1]
    ne = MOE_EXPERTS
    tiles = (cnt_ref[...] + (tr - 1.0)) * (1.0 / tr)
    tiles = tiles.astype(I32).astype(F32)
    lower = lax.broadcasted_iota(I32, (ne, ne), 0) >= lax.broadcasted_iota(I32, (ne, ne), 1)
    ends = _dot(jnp.where(lower, 1.0, 0.0).astype(BF16), tiles.astype(BF16))
    start_col = ((ends - tiles) * tr).astype(I32)[:, 0:1]
    r8 = lax.broadcasted_iota(I32, (8, chunk), 0)
    re = lax.broadcasted_iota(I32, (ne, chunk), 0)
    for c in range(n // chunk):
        blk = ints_ref[:, c * chunk:(c + 1) * chunk]
        s1 = jnp.sum(jnp.where(re == blk[0:1], start_col, 0), 0, keepdims=True)
        s2 = jnp.sum(jnp.where(re == blk[1:2], start_col, 0), 0, keepdims=True)
        pos_ref[:, c * chunk:(c + 1) * chunk] = jnp.where(
            r8 == 0, s1 + blk[2:3], jnp.where(r8 == 1, s2 + blk[3:4], 0))
    width = meta_ref.shape[1]
    tile_id = lax.broadcasted_iota(I32, (ne, width), 1).astype(F32)
    te = jnp.sum(jnp.where(ends[:, 0:1] <= tile_id, 1, 0), 0, keepdims=True)
    te = jnp.minimum(te, ne - 1)
    expert = lax.broadcasted_iota(I32, (ne, width), 0)
    later = jnp.logical_and(expert > te, tiles[:, 0:1] > 0.0)
    nxt = jnp.min(jnp.where(later, expert, ne), 0, keepdims=True)
    nxt = jnp.where(nxt < ne, nxt + layer * ne, -1)
    n_used = ends[ne - 1:ne, 0:1].astype(I32)
    rm = lax.broadcasted_iota(I32, (8, width), 0)
    meta_ref[...] = jnp.where(rm == 0, te + layer * ne,
                              jnp.where(rm == 1, n_used, jnp.where(rm == 2, nxt, 0)))


def _plan(ints, cnt, layer, n_tiles):
    n = ints.shape[1]
    width = -(-n_tiles // LANES) * LANES
    return pl.pallas_call(
        functools.partial(_plan_kernel, layer=layer, chunk=2048),
        out_shape=[jax.ShapeDtypeStruct((8, n), I32), jax.ShapeDtypeStruct((8, width), I32)],
        compiler_params=pltpu.CompilerParams(vmem_limit_bytes=VMEM_LIMIT),
        name="moe_plan",
    )(ints, cnt)


def _moe_layer(h, hp, layer, group_w, group_b, expert_w, expert_b, gate_w, up_w, down_w, ln_g, ln_b,
               tiled_copy):
    n = h.shape[0]
    ew = jnp.transpose(expert_w, (1, 0, 2)).reshape(D_MODEL, MOE_EXPERTS)
    w_r = jnp.zeros((D_MODEL, LANES), F32).at[:, 0:MOE_GROUPS].set(group_w).at[:, 8:8 + MOE_EXPERTS].set(ew)
    b_r = jnp.zeros((LANES,), F32).at[0:MOE_GROUPS].set(group_b).at[8:8 + MOE_EXPERTS].set(expert_b.reshape(-1))
    ints, flts, cnt = _router(h, w_r, b_r.reshape(LANES, 1), 512)
    tr = MOE_ROW_TILE
    n_tiles = (2 * n) // tr + MOE_EXPERTS
    pos, meta = _plan(ints, cnt, layer, n_tiles)
    pos1, pos2 = pos[0], pos[1]

    y = _experts(hp, pos1, pos2, meta[0, :n_tiles], meta[2, :n_tiles], meta[1, :1],
                 gate_w.reshape(-1, D_MODEL, MOE_D_FF),
                 up_w.reshape(-1, D_MODEL, MOE_D_FF),
                 down_w.reshape(-1, MOE_D_FF, D_MODEL))
    return _combine(y, h, pos1, pos2, flts,
                    ln_g.reshape(1, -1), ln_b.reshape(1, -1), COMBINE_TILE, tiled_copy)


def _pad_heads(w, axis):
    parts = []
    h0 = 0
    for _, _, nh in ATTN_GROUPS:
        sl = [slice(None)] * w.ndim
        sl[axis] = slice(h0 * ATTN_HEAD_DIM, (h0 + nh) * ATTN_HEAD_DIM)
        part = w[tuple(sl)]
        pad = [(0, 0)] * w.ndim
        pad[axis] = (0, ATTN_SLAB - nh * ATTN_HEAD_DIM)
        parts.append(jnp.pad(part, pad))
        h0 += nh
    return parts


def _in_proj_kernel(x_ref, w_ref, wdt_ref, z_ref, xbc_ref, dt_ref, *, z_tiles):
    j = pl.program_id(1)
    tn = z_ref.shape[1]
    acc = _dot(x_ref[...].astype(BF16), w_ref[:, pl.ds(pl.multiple_of(j * tn, tn), tn)])

    @pl.when(j < z_tiles)
    def _():
        z_ref[...] = acc

    @pl.when(j >= z_tiles)
    def _():
        xbc_ref[...] = acc

    @pl.when(j == 0)
    def _():
        dt_ref[...] = _dot_f32ish(x_ref[...], wdt_ref[...])


def _in_proj(x, w_zx, w_dt, tm, tn):
    m, k = x.shape
    z_tiles = SSM_D_INNER // tn
    n_tiles = (SSM_D_INNER + SSM_CONV_DIM) // tn
    return pl.pallas_call(
        functools.partial(_in_proj_kernel, z_tiles=z_tiles),
        grid=(m // tm, n_tiles),
        in_specs=[pl.BlockSpec((tm, k), lambda i, j: (i, 0)),
                  pl.BlockSpec(memory_space=pltpu.VMEM),
                  pl.BlockSpec((k, LANES), lambda i, j: (0, 0))],
        out_specs=[pl.BlockSpec((tm, tn), lambda i, j: (i, jnp.minimum(j, z_tiles - 1))),
                   pl.BlockSpec((tm, tn), lambda i, j: (i, jnp.maximum(j - z_tiles, 0))),
                   pl.BlockSpec((tm, LANES), lambda i, j: (i, 0))],
        out_shape=[jax.ShapeDtypeStruct((m, SSM_D_INNER), F32),
                   jax.ShapeDtypeStruct((m, SSM_CONV_DIM), F32),
                   jax.ShapeDtypeStruct((m, LANES), F32)],
        compiler_params=_cparams("parallel", "arbitrary"),
        name="ssm_in_proj",
    )(x, w_zx, w_dt)


def _ssd_layer(h, in_w, conv_w, conv_b, dt_bias, a_log, d_skip, norm_w, out_w, ln_g, ln_b, bsz, seq):
    split = SSM_D_INNER + SSM_CONV_DIM
    dt_w = jnp.pad(in_w[:, split:], ((0, 0), (0, LANES - SSM_HEADS)))
    z, xbc, dt_raw = _in_proj(h, in_w.astype(BF16), dt_w, 1024, 1024)
    pad32 = lambda v: jnp.pad(v, (0, LANES - SSM_HEADS)).reshape(1, LANES)
    y = _ssd(z, xbc, dt_raw, conv_w, conv_b.reshape(1, -1), pad32(dt_bias), pad32(a_log),
             jnp.repeat(d_skip, SSM_HEAD_DIM).reshape(1, -1), norm_w.reshape(1, -1), bsz, seq)
    return _matmul_res_ln(y, out_w.astype(BF16), h, ln_g.reshape(1, -1), ln_b.reshape(1, -1), 1024)


def _qkv_group_weights(kv_w, q_w):
    width = ATTN_HEADS * ATTN_HEAD_DIM
    w_q = _pad_heads(q_w * (ATTN_HEAD_DIM ** -0.5), 1)
    w_k = _pad_heads(kv_w[:, :width], 1)
    w_v = _pad_heads(kv_w[:, width:], 1)
    return [jnp.concatenate([w_q[gi], w_k[gi], w_v[gi]], axis=1).astype(BF16) for gi in range(len(ATTN_GROUPS))]


def _attn_layer(h, h8, qkv_w, o_w, rel_bias, ln_g, ln_b, bsz, seq):
    outs, lses = [], []
    h0 = 0
    for gi, (_, dil, nh) in enumerate(ATTN_GROUPS):
        if dil == 1:
            qkv = _matmul(h, qkv_w[gi], BF16, 1024, 3 * ATTN_SLAB)
        else:
            qkv = _qkv_dilated(h8, qkv_w[gi], dil, max(1024, ATTN_BLOCK * dil))
        bias_p, bias_c = _group_bias(rel_bias, h0, nh, dil)
        o, lse = _attention_group(qkv, bias_p, bias_c, gi, dil, nh, bsz, seq)
        outs.extend(o)
        lses.append(lse)
        h0 += nh
    w_o = jnp.concatenate(_pad_heads(o_w, 0), axis=0).astype(BF16)
    return _attn_out(outs, lses, w_o, h, ln_g.reshape(1, -1), ln_b.reshape(1, -1), 1024)


def kernel(x, ssm_in_w, ssm_conv_w, ssm_conv_b, ssm_dt_bias, ssm_a_log, ssm_d, ssm_norm_w, ssm_out_w,
           kv_w, attn_q_w, attn_o_w, rel_bias, moe_group_w, moe_group_b, moe_expert_w, moe_expert_b,
           moe_gate_w, moe_up_w, moe_down_w, ln_g, ln_b):
    bsz, seq, d = x.shape
    h = x.reshape(bsz * seq, d)
    n_ssd = DEPTH // 2
    h8 = None
    for i in range(DEPTH):
        if i < n_ssd:
            h, hp = _ssd_layer(h, ssm_in_w[i], ssm_conv_w[i], ssm_conv_b[i], ssm_dt_bias[i], ssm_a_log[i],
                               ssm_d[i], ssm_norm_w[i], ssm_out_w[i], ln_g[i, 0], ln_b[i, 0], bsz, seq)
        else:
            j = i - n_ssd
            h, hp = _attn_layer(h, h8, _qkv_group_weights(kv_w, attn_q_w[j]), attn_o_w[j], rel_bias,
                                ln_g[i, 0], ln_b[i, 0], bsz, seq)
        feeds_attention = n_ssd <= i + 1 < DEPTH
        res = _moe_layer(h, hp, i, moe_group_w[i], moe_group_b[i], moe_expert_w[i], moe_expert_b[i],
                         moe_gate_w, moe_up_w, moe_down_w, ln_g[i, 1], ln_b[i, 1], feeds_attention)
        h = res[0]
        h8 = res[1] if feeds_attention else None
    return h.reshape(bsz, seq, d)
```

```python
import functools
import math

import numpy as np
import jax
import jax.numpy as jnp
from jax import lax
from jax.experimental import pallas as pl
from jax.experimental.pallas import tpu as pltpu

F32 = jnp.float32
BF16 = jnp.bfloat16
I32 = jnp.int32

D_MODEL = 1024
DEPTH = 2
DN_ALPHA = (2 * DEPTH) ** 0.25
LN_EPS = 1e-5
LOG2_E = math.log2(math.e)

SSM_D_INNER = 2048
SSM_HEAD_DIM = 64
SSM_HEADS = 32
SSM_GROUPS = 4
SSM_STATE = 128
SSM_CONV = 4
SSM_CHUNK = 128
SSM_CONV_DIM = SSM_D_INNER + 2 * SSM_GROUPS * SSM_STATE

ATTN_HEAD_DIM = 64
ATTN_GROUPS = ((128, 1, 6), (512, 4, 5), (2048, 16, 5))
ATTN_HEADS = 16
ATTN_BLOCK = 128
N_BUCKETS = 32
MAX_DISTANCE = 2048
ATTN_SLAB = 384
NEG_BIG = -1e30
ATTN_RESIDUES_PER_PASS = 2

MOE_GROUPS = 4
MOE_EPG = 8
MOE_EXPERTS = MOE_GROUPS * MOE_EPG
MOE_D_FF = 512
MOE_ROW_TILE = 256
COMBINE_TILE = 512

LANES = 128
DMA_QUEUES = 2
VMEM_LIMIT = 48 * 1024 * 1024
EXPERTS_VMEM_LIMIT = 56 * 1024 * 1024


def _cparams(*sem):
    return pltpu.CompilerParams(dimension_semantics=sem, vmem_limit_bytes=VMEM_LIMIT)


def _layer_norm(x, g, b):
    mu = jnp.mean(x, -1, keepdims=True)
    xc = x - mu
    var = jnp.mean(xc * xc, -1, keepdims=True)
    return xc * lax.rsqrt(var + LN_EPS) * g + b


def _split2(x):
    hi = x.astype(BF16)
    lo = (x - hi.astype(F32)).astype(BF16)
    return hi, lo


def _dot(a, b):
    return jnp.dot(a, b, preferred_element_type=F32)


def _dot_f32ish(a, b):
    ah, al = _split2(a)
    bh, bl = _split2(b)
    return _dot(ah, bh) + _dot(al, bh) + _dot(ah, bl)


def _mm_kernel(a_ref, b_ref, o_ref):
    o_ref[...] = _dot(a_ref[...].astype(BF16), b_ref[...]).astype(o_ref.dtype)


def _matmul(a, b, out_dtype, tm, tn):
    m, k = a.shape
    nc = b.shape[1]
    return pl.pallas_call(
        _mm_kernel,
        grid=(m // tm, nc // tn),
        in_specs=[pl.BlockSpec((tm, k), lambda i, j: (i, 0)),
                  pl.BlockSpec((k, tn), lambda i, j: (0, j))],
        out_specs=pl.BlockSpec((tm, tn), lambda i, j: (i, j)),
        out_shape=jax.ShapeDtypeStruct((m, nc), out_dtype),
        compiler_params=_cparams("parallel", "parallel"),
        name="matmul",
    )(a, b)


ROW_SPLIT = D_MODEL // LANES


def _store_row_tiled(ref, val):
    rows = val.shape[0]
    for c in range(ROW_SPLIT):
        ref[pl.ds(c, rows, stride=ROW_SPLIT), :] = val[:, c * LANES:(c + 1) * LANES]


def _load_row_tiled(ref):
    rows = ref.shape[0] // ROW_SPLIT
    return jnp.concatenate([ref[pl.ds(c, rows, stride=ROW_SPLIT), :] for c in range(ROW_SPLIT)], axis=1)


HALF = D_MODEL // 2
U32 = jnp.uint32
HI16 = 0xFFFF0000


def _pack_bf16_pairs(x):
    bits = lax.bitcast_convert_type(x.astype(BF16).astype(F32), U32)
    return (bits[:, :HALF] >> 16) | (bits[:, HALF:] & U32(HI16))


def _unpack_bf16_pairs(w):
    lo = lax.bitcast_convert_type(w << 16, F32).astype(BF16)
    hi = lax.bitcast_convert_type(w & U32(HI16), F32).astype(BF16)
    return lo, hi


def _mm_ln_kernel(a_ref, w_ref, h_ref, g_ref, b_ref, o_ref, op_ref):
    acc = _dot(a_ref[...], w_ref[...])
    out = _layer_norm(DN_ALPHA * h_ref[...] + acc, g_ref[...], b_ref[...])
    o_ref[...] = out
    op_ref[...] = _pack_bf16_pairs(out)


def _matmul_res_ln(a, w, h, g, b, tm):
    m, k = a.shape
    d = w.shape[1]
    return pl.pallas_call(
        _mm_ln_kernel,
        grid=(m // tm,),
        in_specs=[pl.BlockSpec((tm, k), lambda i: (i, 0)),
                  pl.BlockSpec((k, d), lambda i: (0, 0)),
                  pl.BlockSpec((tm, d), lambda i: (i, 0)),
                  pl.BlockSpec((1, d), lambda i: (0, 0)),
                  pl.BlockSpec((1, d), lambda i: (0, 0))],
        out_specs=[pl.BlockSpec((tm, d), lambda i: (i, 0)),
                   pl.BlockSpec((tm, HALF), lambda i: (i, 0))],
        out_shape=[jax.ShapeDtypeStruct((m, d), F32),
                   jax.ShapeDtypeStruct((m, HALF), U32)],
        compiler_params=_cparams("parallel"),
        name="matmul_res_ln",
    )(a, w, h, g, b)


SSD_CHUNKS_PER_STEP = 2


def _ssd_kernel(z_ref, xbc_ref, dt_ref, cw_ref, cb_ref, dtb_ref, alog_ref, dsk_ref, nw_ref,
                y_ref, xe_ref, st_ref):
    @pl.when(pl.program_id(1) == 0)
    def _():
        xe_ref[:, 0:8, :] = jnp.zeros((xe_ref.shape[0], 8, LANES), F32)
        st_ref[...] = jnp.zeros_like(st_ref)

    for sub in range(SSD_CHUNKS_PER_STEP):
        _ssd_chunk(pl.ds(sub * SSM_CHUNK, SSM_CHUNK), z_ref, xbc_ref, dt_ref, cw_ref, cb_ref, dtb_ref,
                   alog_ref, dsk_ref, nw_ref, y_ref, xe_ref, st_ref)


def _ssd_chunk(rows, z_ref, xbc_ref, dt_ref, cw_ref, cb_ref, dtb_ref, alog_ref, dsk_ref, nw_ref,
               y_ref, xe_ref, st_ref):
    q = SSM_CHUNK
    w = cw_ref[...]
    bias = cb_ref[...]
    act = []
    for c in range(SSM_CONV_DIM // LANES):
        cols = slice(c * LANES, (c + 1) * LANES)
        u = xbc_ref[rows, cols]
        xe_ref[c, 8:8 + q, :] = u
        conv = (bias[:, cols] + w[3:4, cols] * u + w[2:3, cols] * xe_ref[c, 7:7 + q, :]
                + w[1:2, cols] * xe_ref[c, 6:6 + q, :] + w[0:1, cols] * xe_ref[c, 5:5 + q, :])
        xe_ref[c, 0:8, :] = xe_ref[c, q:q + 8, :]
        act.append(conv * jax.nn.sigmoid(conv))

    pre = dt_ref[rows, :] + dtb_ref[...]
    dt = jnp.maximum(pre, 0.0) + jnp.log(1.0 + jnp.exp(-jnp.abs(pre)))
    adt = dt * (-jnp.exp(alog_ref[...]) * LOG2_E)

    row = lax.broadcasted_iota(I32, (q, q), 0)
    col = lax.broadcasted_iota(I32, (q, q), 1)
    tril = row >= col
    tri_b = jnp.where(tril, 1.0, 0.0).astype(BF16)
    a_hi = adt.astype(BF16)
    r1 = adt - a_hi.astype(F32)
    a_mid = r1.astype(BF16)
    a_lo = (r1 - a_mid.astype(F32)).astype(BF16)
    acs = _dot(tri_b, a_hi) + _dot(tri_b, a_mid) + _dot(tri_b, a_lo)
    acs_t = acs.T
    acs_dt_t = acs_t - jnp.log(dt.T) * LOG2_E
    eacs = jnp.exp2(acs)
    left = col < SSM_HEAD_DIM

    for g in range(SSM_GROUPS):
        b0 = SSM_D_INNER + g * SSM_STATE
        c0 = SSM_D_INNER + SSM_GROUPS * SSM_STATE + g * SSM_STATE
        bm = act[b0 // LANES]
        cm = act[c0 // LANES].astype(BF16)
        cb = lax.dot_general(cm, bm.astype(BF16), (((1,), (1,)), ((), ())),
                             preferred_element_type=F32)
        bm_t = bm.T
        gs = g * 512
        y_off = _dot(cm, st_ref[:, gs:gs + 512].astype(BF16))
        slabs = []
        for pr in range(4):
            ha = g * 8 + pr * 2
            hb = ha + 1
            cs = gs + pr * LANES
            x2 = act[cs // LANES]
            x2b = x2.astype(BF16)
            ys, ups = [], []
            for h in (ha, hb):
                a_col = acs[:, h:h + 1]
                a_src = acs_dt_t[h:h + 1, :]
                decay = jnp.where(tril, jnp.exp2(a_col - a_src), 0.0)
                ys.append(_dot((cb * decay).astype(BF16), x2b))
                to_end = jnp.exp2(acs_t[h:h + 1, q - 1:q] - a_src)
                ups.append(_dot((bm_t * to_end).astype(BF16), x2b))
            y_diag = jnp.where(left, ys[0], ys[1])
            upd = jnp.where(left, ups[0], ups[1])
            e2 = jnp.where(left, eacs[:, ha:ha + 1], eacs[:, hb:hb + 1])
            cd = jnp.where(left[0:1, :], eacs[q - 1:q, ha:ha + 1], eacs[q - 1:q, hb:hb + 1])
            y2 = y_diag + y_off[:, pr * LANES:(pr + 1) * LANES] * e2 + dsk_ref[:, cs:cs + LANES] * x2
            st_ref[:, cs:cs + LANES] = st_ref[:, cs:cs + LANES] * cd + upd
            slabs.append(y2)
        yg = jnp.concatenate(slabs, axis=1)
        zg = z_ref[rows, gs:gs + 512]
        yg = yg * (zg * jax.nn.sigmoid(zg))
        ms = jnp.mean(yg * yg, -1, keepdims=True)
        y_ref[rows, gs:gs + 512] = (yg * lax.rsqrt(ms + LN_EPS) * nw_ref[:, gs:gs + 512]).astype(y_ref.dtype)


def _ssd(z, xbc, dt_raw, conv_w, conv_b, dt_bias, a_log, d_rep, norm_w, bsz, seq):
    n = z.shape[0]
    q = SSM_CHUNK * SSD_CHUNKS_PER_STEP
    nchunk = seq // q
    tok = lambda b, c: (b * nchunk + c, 0)
    fixed = lambda b, c: (0, 0)
    return pl.pallas_call(
        _ssd_kernel,
        grid=(bsz, nchunk),
        in_specs=[pl.BlockSpec((q, SSM_D_INNER), tok),
                  pl.BlockSpec((q, SSM_CONV_DIM), tok),
                  pl.BlockSpec((q, LANES), tok),
                  pl.BlockSpec((SSM_CONV, SSM_CONV_DIM), fixed),
                  pl.BlockSpec((1, SSM_CONV_DIM), fixed),
                  pl.BlockSpec((1, LANES), fixed),
                  pl.BlockSpec((1, LANES), fixed),
                  pl.BlockSpec((1, SSM_D_INNER), fixed),
                  pl.BlockSpec((1, SSM_D_INNER), fixed)],
        out_specs=pl.BlockSpec((q, SSM_D_INNER), tok),
        out_shape=jax.ShapeDtypeStruct((n, SSM_D_INNER), BF16),
        scratch_shapes=[pltpu.VMEM((SSM_CONV_DIM // LANES, SSM_CHUNK + 8, LANES), F32),
                        pltpu.VMEM((SSM_STATE, SSM_D_INNER), F32)],
        compiler_params=_cparams("parallel", "arbitrary"),
        name="ssd_chunk",
    )(z, xbc, dt_raw, conv_w, conv_b, dt_bias, a_log, d_rep, norm_w)


def _residue_major_pieces(h8_ref, n_tokens, dil):
    span = ATTN_BLOCK * dil
    pieces = []
    for blk in range(n_tokens // span):
        for r in range(dil):
            first = (blk * span + r) * ROW_SPLIT
            pieces.append(jnp.concatenate(
                [h8_ref[pl.ds(first + c, ATTN_BLOCK, stride=ROW_SPLIT * dil), :] for c in range(ROW_SPLIT)],
                axis=1).astype(BF16))
    return pieces


def _qkv_dilated_kernel(h8_ref, w_ref, o_ref, *, dil):
    per_dot = 4
    rows = _residue_major_pieces(h8_ref, o_ref.shape[0], dil)
    for k in range(0, len(rows), per_dot):
        x = jnp.concatenate(rows[k:k + per_dot], axis=0)
        o_ref[k * ATTN_BLOCK:(k + per_dot) * ATTN_BLOCK, :] = _dot(x, w_ref[...]).astype(o_ref.dtype)


def _qkv_dilated(h8, w, dil, tm):
    n = h8.shape[0] // ROW_SPLIT
    nc = w.shape[1]
    return pl.pallas_call(
        functools.partial(_qkv_dilated_kernel, dil=dil),
        grid=(n // tm,),
        in_specs=[pl.BlockSpec((tm * ROW_SPLIT, LANES), lambda i: (i, 0)),
                  pl.BlockSpec((D_MODEL, nc), lambda i: (0, 0))],
        out_specs=pl.BlockSpec((tm, nc), lambda i: (i, 0)),
        out_shape=jax.ShapeDtypeStruct((n, nc), BF16),
        compiler_params=_cparams("parallel"),
        name=f"qkv_dil{dil}",
    )(h8, w)


def _attn_kernel(*refs, nh, dil, has_prev, blocks):
    span = ATTN_BLOCK * dil
    if has_prev:
        q_ref, kp_ref, kc_ref, vp_ref, vc_ref, bp_ref, bc_ref = refs[:7]
        out_refs = refs[7:]
        first_pen = jnp.where(pl.program_id(1) > 0, 0.0, NEG_BIG)
    else:
        q_ref, kc_ref, vc_ref, bc_ref = refs[:4]
        out_refs = refs[4:]
    s_scr, p_scr, max_scr, den_scr = out_refs[-4:]
    o_refs, lse_ref = out_refs[:-5], out_refs[-5]
    ones = jnp.ones((s_scr.shape[2], LANES), BF16)
    lane = lax.broadcasted_iota(I32, (ATTN_BLOCK, LANES), 1)
    left = lane < ATTN_HEAD_DIM
    zero = jnp.zeros((), BF16)
    nt = (((1,), (1,)), ((), ()))

    def residues(blk, rs):
        info = []
        for slot, r in enumerate(rs):
            rows = pl.ds(pl.multiple_of(blk * span + r * ATTN_BLOCK, ATTN_BLOCK), ATTN_BLOCK)
            dst = pl.ds(blk * span + r, ATTN_BLOCK, stride=dil)
            kprev = vprev = prows = pen = None
            if has_prev and blk == 0:
                kprev, vprev, pen = kp_ref, vp_ref, first_pen
                prows = pl.ds(pl.multiple_of(r * ATTN_BLOCK, ATTN_BLOCK), ATTN_BLOCK)
            elif has_prev:
                kprev, vprev, pen = kc_ref, vc_ref, 0.0
                prows = pl.ds(pl.multiple_of((blk - 1) * span + r * ATTN_BLOCK, ATTN_BLOCK), ATTN_BLOCK)
            info.append((slot * nh, rows, dst, kprev, vprev, prows, pen))
        for base, rows, dst, kprev, vprev, prows, pen in info:
            for hh in range(nh):
                cols = pl.ds(hh // 2 * LANES, LANES)
                qm = jnp.where(left if hh % 2 == 0 else ~left, q_ref[rows, cols], zero)
                s_c = lax.dot_general(qm, kc_ref[rows, cols], nt, preferred_element_type=F32) + bc_ref[hh]
                if has_prev:
                    s_p = lax.dot_general(qm, kprev[prows, cols], nt, preferred_element_type=F32)
                    s_scr[base + hh, :, 0:ATTN_BLOCK] = s_p + (bp_ref[hh] + pen)
                    s_scr[base + hh, :, ATTN_BLOCK:] = s_c
                else:
                    s_scr[base + hh] = s_c
        for base, *_ in info:
            for hh in range(nh):
                s = s_scr[base + hh]
                m = jnp.max(s, -1, keepdims=True)
                p = jnp.exp(s - m)
                p_scr[base + hh] = p.astype(BF16)
                max_scr[base + hh] = m
                if not has_prev:
                    den_scr[base + hh] = jnp.sum(p, -1, keepdims=True)
        for base, rows, dst, kprev, vprev, prows, pen in info:
            lse_sum = jnp.zeros((ATTN_BLOCK, 1), F32)
            for pr in range(ATTN_SLAB // LANES):
                cols = pl.ds(pr * LANES, LANES)
                halves = []
                for hh in (pr * 2, pr * 2 + 1):
                    if hh >= nh:
                        halves.append(jnp.zeros((ATTN_BLOCK, LANES), F32))
                        continue
                    if has_prev:
                        den = _dot(p_scr[base + hh], ones)
                        den1 = den[:, 0:1]
                        o = (_dot(p_scr[base + hh, :, 0:ATTN_BLOCK], vprev[prows, cols])
                             + _dot(p_scr[base + hh, :, ATTN_BLOCK:], vc_ref[rows, cols]))
                    else:
                        den = den1 = den_scr[base + hh]
                        o = _dot(p_scr[base + hh], vc_ref[rows, cols])
                    halves.append(o * (1.0 / den))
                    lse_sum = lse_sum + (max_scr[base + hh] + jnp.log(den1))
                o_refs[pr][dst, :] = jnp.where(left, halves[0], halves[1])
            lse_ref[dst, :] = jnp.broadcast_to(lse_sum * (1.0 / nh), (ATTN_BLOCK, LANES))

    per_pass = min(dil, ATTN_RESIDUES_PER_PASS)
    for blk in range(blocks):
        def body(it, carry, blk=blk):
            residues(blk, [it * per_pass + k for k in range(per_pass)])
            return carry

        lax.fori_loop(0, dil // per_pass, body, 0)


def _attention_group(qkv, bias_p, bias_c, gi, dil, nh, bsz, seq):
    span = ATTN_BLOCK * dil
    nb = seq // span
    has_prev = nb > 1
    blocks = max(1, 512 // span)
    steps = nb // blocks
    blk = (blocks * span, ATTN_SLAB)
    cur = lambda which: (lambda b, n: (b * steps + n, which))
    prev = lambda which: (lambda b, n: (b * nb + jnp.maximum(n * blocks - 1, 0), which))
    fixed = lambda b, n: (0, 0, 0)
    tab = pl.BlockSpec((nh, ATTN_BLOCK, ATTN_BLOCK), fixed)
    if has_prev:
        pblk = (span, ATTN_SLAB)
        in_specs = [pl.BlockSpec(blk, cur(0)), pl.BlockSpec(pblk, prev(1)), pl.BlockSpec(blk, cur(1)),
                    pl.BlockSpec(pblk, prev(2)), pl.BlockSpec(blk, cur(2)), tab, tab]
        args = (qkv, qkv, qkv, qkv, qkv, bias_p, bias_c)
    else:
        in_specs = [pl.BlockSpec(blk, cur(0)), pl.BlockSpec(blk, cur(1)), pl.BlockSpec(blk, cur(2)), tab]
        args = (qkv, qkv, qkv, bias_c)
    n_out = ATTN_SLAB // LANES + 1
    keys = 2 * ATTN_BLOCK if has_prev else ATTN_BLOCK
    slots = min(dil, ATTN_RESIDUES_PER_PASS)
    outs = pl.pallas_call(
        functools.partial(_attn_kernel, nh=nh, dil=dil, has_prev=has_prev, blocks=blocks),
        grid=(bsz, steps),
        in_specs=in_specs,
        out_specs=[pl.BlockSpec((blocks * span, LANES), lambda b, n: (b * steps + n, 0))] * n_out,
        out_shape=[jax.ShapeDtypeStruct((bsz * seq, LANES), F32)] * n_out,
        scratch_shapes=[pltpu.VMEM((slots * nh, ATTN_BLOCK, keys), F32),
                        pltpu.VMEM((slots * nh, ATTN_BLOCK, keys), BF16),
                        pltpu.VMEM((slots * nh, ATTN_BLOCK, 1), F32),
                        pltpu.VMEM((slots * nh, ATTN_BLOCK, 1), F32)],
        compiler_params=_cparams("parallel", "parallel"),
        name=f"dilated_attn_g{gi}",
    )(*args)
    return outs[:-1], outs[-1]


def _attn_out_kernel(*refs):
    n_pairs = ATTN_SLAB // LANES
    n_groups = len(ATTN_GROUPS)
    o_refs = refs[:n_groups * n_pairs]
    l_refs = refs[n_groups * n_pairs:n_groups * (n_pairs + 1)]
    w_ref, h_ref, g_ref, b_ref, out_ref, outp_ref = refs[n_groups * (n_pairs + 1):]
    ls = [r[:, 0:1] for r in l_refs]
    m = jnp.maximum(jnp.maximum(ls[0], ls[1]), ls[2])
    es = [jnp.exp(v - m) for v in ls]
    scale = n_groups / (es[0] + es[1] + es[2])
    slabs = []
    for gi in range(n_groups):
        wt = es[gi] * scale
        slabs.extend((r[...] * wt).astype(BF16) for r in o_refs[gi * n_pairs:(gi + 1) * n_pairs])
    acc = _dot(jnp.concatenate(slabs, axis=1), w_ref[...])
    out = _layer_norm(DN_ALPHA * h_ref[...] + acc, g_ref[...], b_ref[...])
    out_ref[...] = out
    outp_ref[...] = _pack_bf16_pairs(out)


def _attn_out(outs, lses, w, h, g, b, tm):
    n = h.shape[0]
    tok = lambda i: (i, 0)
    fixed2 = lambda i: (0, 0)
    return pl.pallas_call(
        _attn_out_kernel,
        grid=(n // tm,),
        in_specs=[pl.BlockSpec((tm, LANES), tok)] * (len(outs) + len(lses))
                 + [pl.BlockSpec((len(ATTN_GROUPS) * ATTN_SLAB, D_MODEL), lambda i: (0, 0)),
                    pl.BlockSpec((tm, D_MODEL), tok),
                    pl.BlockSpec((1, D_MODEL), fixed2), pl.BlockSpec((1, D_MODEL), fixed2)],
        out_specs=[pl.BlockSpec((tm, D_MODEL), tok), pl.BlockSpec((tm, HALF), tok)],
        out_shape=[jax.ShapeDtypeStruct((n, D_MODEL), F32),
                   jax.ShapeDtypeStruct((n, HALF), U32)],
        compiler_params=_cparams("parallel"),
        name="attn_out_ln",
    )(*outs, *lses, w, h, g, b)


def _t5_bucket(dist):
    max_exact = N_BUCKETS // 2
    n = np.maximum(dist, 1).astype(np.float64)
    large = max_exact + (np.log(n / max_exact) / np.log(MAX_DISTANCE / max_exact)
                         * (N_BUCKETS - max_exact)).astype(np.int32)
    large = np.minimum(large, N_BUCKETS - 1)
    return np.where(dist < max_exact, dist, large).astype(np.int32)


def _group_bias(rel_bias, h0, nh, dil):
    qi = np.arange(ATTN_BLOCK)[:, None]
    ki = np.arange(ATTN_BLOCK)[None, :]
    tabs = []
    for delta, band in ((qi + ATTN_BLOCK - ki, ki >= qi), (qi - ki, ki <= qi)):
        bucket = _t5_bucket(np.clip(delta, 0, None) * dil)
        onehot = (bucket[..., None] == np.arange(N_BUCKETS)).astype(np.float32)
        t = jnp.einsum("qkb,bh->hqk", onehot, rel_bias[:, h0:h0 + nh], precision=lax.Precision.HIGHEST)
        tabs.append(jnp.where(band[None], t, NEG_BIG).astype(F32))
    return tabs


def _router_kernel(h_ref, w_ref, b_ref, ints_ref, flts_ref, cnt_ref, carry_ref, before_ref, *, tm):
    @pl.when(pl.program_id(0) == 0)
    def _():
        carry_ref[...] = jnp.zeros_like(carry_ref)
        tr = lax.broadcasted_iota(I32, (tm, tm), 0)
        tc = lax.broadcasted_iota(I32, (tm, tm), 1)
        before_ref[...] = jnp.where(tr < tc, 1.0, 0.0).astype(BF16)

    lt = _dot_f32ish(h_ref[...], w_ref[...]).T + b_ref[...]
    gl = lt[0:MOE_GROUPS]
    r4 = lax.broadcasted_iota(I32, (MOE_GROUPS, tm), 0)
    gmax = jnp.max(gl, 0, keepdims=True)
    gidx = jnp.min(jnp.where(gl == gmax, r4, MOE_GROUPS), 0, keepdims=True)
    gval = 1.0 / jnp.sum(jnp.exp(gl - gmax), 0, keepdims=True)

    esel = jnp.zeros((MOE_EPG, tm), F32)
    for g in range(MOE_GROUPS):
        esel = jnp.where(gidx == g, lt[8 + g * MOE_EPG:8 + (g + 1) * MOE_EPG], esel)
    r8 = lax.broadcasted_iota(I32, (MOE_EPG, tm), 0)
    v1 = jnp.max(esel, 0, keepdims=True)
    i1 = jnp.min(jnp.where(esel == v1, r8, MOE_EPG), 0, keepdims=True)
    rest = jnp.where(r8 == i1, -jnp.inf, esel)
    v2 = jnp.max(rest, 0, keepdims=True)
    i2 = jnp.min(jnp.where(rest == v2, r8, MOE_EPG), 0, keepdims=True)
    t = jnp.exp(v2 - v1)
    p1 = gval / (1.0 + t)
    p2 = p1 * t
    e1 = gidx * MOE_EPG + i1
    e2 = gidx * MOE_EPG + i2

    r32 = lax.broadcasted_iota(I32, (MOE_EXPERTS, tm), 0)
    oh1 = r32 == e1
    oh2 = r32 == e2
    oh = jnp.where(oh1 | oh2, 1.0, 0.0)
    base = _dot(oh.astype(BF16), before_ref[...]) + carry_ref[:, 0:1]
    rank1 = jnp.sum(jnp.where(oh1, base, 0.0), 0, keepdims=True).astype(I32)
    rank2 = jnp.sum(jnp.where(oh2, base, 0.0), 0, keepdims=True).astype(I32)
    carry_ref[...] = carry_ref[...] + jnp.sum(oh, 1, keepdims=True)
    cnt_ref[...] = carry_ref[...]

    ints_ref[...] = jnp.where(r8 == 0, e1, jnp.where(r8 == 1, e2, jnp.where(r8 == 2, rank1,
                              jnp.where(r8 == 3, rank2, 0))))
    r128 = lax.broadcasted_iota(I32, (LANES, tm), 0)
    flts_ref[...] = jnp.where(r128 == 0, p1, jnp.where(r128 == 1, p2, 0.0)).T


def _router(h, w_r, b_r, tm):
    n = h.shape[0]
    return pl.pallas_call(
        functools.partial(_router_kernel, tm=tm),
        grid=(n // tm,),
        in_specs=[pl.BlockSpec((tm, D_MODEL), lambda i: (i, 0)),
                  pl.BlockSpec((D_MODEL, LANES), lambda i: (0, 0)),
                  pl.BlockSpec((LANES, 1), lambda i: (0, 0))],
        out_specs=[pl.BlockSpec((8, tm), lambda i: (0, i)),
                   pl.BlockSpec((tm, LANES), lambda i: (i, 0)),
                   pl.BlockSpec((MOE_EXPERTS, LANES), lambda i: (0, 0))],
        out_shape=[jax.ShapeDtypeStruct((8, n), I32),
                   jax.ShapeDtypeStruct((n, LANES), F32),
                   jax.ShapeDtypeStruct((MOE_EXPERTS, LANES), F32)],
        scratch_shapes=[pltpu.VMEM((MOE_EXPERTS, LANES), F32), pltpu.VMEM((tm, tm), BF16)],
        compiler_params=_cparams("arbitrary"),
        name="moe_router",
    )(h, w_r, b_r)


def _row_copy(src_ref, s, dst_ref, d, sem):
    def first(r):
        return r * ROW_SPLIT if isinstance(r, int) else pl.multiple_of(r * ROW_SPLIT, ROW_SPLIT)

    return pltpu.make_async_copy(src_ref.at[pl.ds(first(s), ROW_SPLIT), :],
                                 dst_ref.at[pl.ds(first(d), ROW_SPLIT), :], sem)


def _tile_wait(src_ref, dst_ref, sem):
    pltpu.make_async_copy(src_ref.at[pl.ds(0, dst_ref.shape[0]), :], dst_ref, sem).wait()


def _experts_kernel(pos1_ref, pos2_ref, te_ref, nxt_ref, na_ref, hp_hbm, zeros_ref, wg_ref, wu_ref, wd_ref,
                    y_ref, hp_ref, xa, xb, wg_f, wu_f, wd_f, wg_s, wu_s, wd_s, row_tok, slot_ref,
                    sem, hsem, wsem):
    i = pl.program_id(0)
    n_tiles = pl.num_programs(0)
    tr = MOE_ROW_TILE
    active = i < na_ref[0]
    changed = jnp.logical_or(i == 0, te_ref[i] != te_ref[jnp.maximum(i - 1, 0)])

    def weight_copies(expert, slot):
        return [pltpu.make_async_copy(src.at[expert], dst.at[slot], wsem.at[slot])
                for src, dst in ((wg_ref, wg_f), (wu_ref, wu_f), (wd_ref, wd_f))]

    @pl.when(i == 0)
    def _():
        slot_ref[0] = 0
        rows_in = pltpu.make_async_copy(hp_hbm, hp_ref, hsem)
        rows_in.start()
        for c in weight_copies(te_ref[0], 0):
            c.start()
        clear = pltpu.make_async_copy(zeros_ref, row_tok, sem)
        clear.start()
        clear.wait()

        def invert(t, carry):
            row_tok[pos1_ref[t]] = t
            row_tok[pos2_ref[t]] = t
            return carry

        lax.fori_loop(0, pos1_ref.shape[0], invert, 0, unroll=16)
        rows_in.wait()

        def pick(r, carry):
            xa[pl.ds(r, 1), :] = hp_ref[pl.ds(row_tok[r], 1), :]
            return carry

        lax.fori_loop(0, tr, pick, 0, unroll=8)

    @pl.when(jnp.logical_and(active, changed))
    def _():
        slot = slot_ref[0]
        for c in weight_copies(te_ref[i], slot):
            c.wait()
        following = nxt_ref[i]

        @pl.when(following >= 0)
        def _():
            for c in weight_copies(following, 1 - slot):
                c.start()

        wg_s[...] = wg_f[slot].astype(BF16)
        wu_s[...] = wu_f[slot].astype(BF16)
        wd_s[...] = wd_f[slot].astype(BF16)
        slot_ref[0] = 1 - slot

    def step(cur, nxt):
        base = jnp.minimum(i + 1, n_tiles - 1) * tr
        for r in range(tr):
            nxt[r:r + 1, :] = hp_ref[pl.ds(row_tok[base + r], 1), :]
        lo, hi = _unpack_bf16_pairs(cur[...])
        gate = _dot(lo, wg_s[0:HALF, :]) + _dot(hi, wg_s[HALF:, :])
        up = _dot(lo, wu_s[0:HALF, :]) + _dot(hi, wu_s[HALF:, :])
        hid = (gate * jax.nn.sigmoid(gate) * up).astype(BF16)
        _store_row_tiled(y_ref, _dot(hid, wd_s[...]))

    even = i % 2 == 0

    @pl.when(jnp.logical_and(active, even))
    def _():
        step(xa, xb)

    @pl.when(jnp.logical_and(active, jnp.logical_not(even)))
    def _():
        step(xb, xa)

    @pl.when(jnp.logical_not(active))
    def _():
        y_ref[...] = jnp.zeros_like(y_ref)


def _experts(hp, pos1, pos2, tile_expert, next_expert, n_active, wg, wu, wd):
    tr = MOE_ROW_TILE
    n_tiles = tile_expert.shape[0]
    rows = n_tiles * tr
    hbm = pl.BlockSpec(memory_space=pl.ANY)
    return pl.pallas_call(
        _experts_kernel,
        grid_spec=pltpu.PrefetchScalarGridSpec(
            num_scalar_prefetch=5,
            grid=(n_tiles,),
            in_specs=[hbm, hbm, hbm, hbm, hbm],
            out_specs=pl.BlockSpec((tr * ROW_SPLIT, LANES), lambda i, *_: (i, 0)),
            scratch_shapes=[pltpu.VMEM(hp.shape, U32),
                            pltpu.VMEM((tr, HALF), U32),
                            pltpu.VMEM((tr, HALF), U32),
                            pltpu.VMEM((2, D_MODEL, MOE_D_FF), F32),
                            pltpu.VMEM((2, D_MODEL, MOE_D_FF), F32),
                            pltpu.VMEM((2, MOE_D_FF, D_MODEL), F32),
                            pltpu.VMEM((D_MODEL, MOE_D_FF), BF16),
                            pltpu.VMEM((D_MODEL, MOE_D_FF), BF16),
                            pltpu.VMEM((MOE_D_FF, D_MODEL), BF16),
                            pltpu.SMEM((rows,), I32),
                            pltpu.SMEM((1,), I32),
                            pltpu.SemaphoreType.DMA(()),
                            pltpu.SemaphoreType.DMA(()),
                            pltpu.SemaphoreType.DMA((2,))]),
        out_shape=jax.ShapeDtypeStruct((rows * ROW_SPLIT, LANES), F32),
        compiler_params=pltpu.CompilerParams(dimension_semantics=("arbitrary",),
                                             vmem_limit_bytes=EXPERTS_VMEM_LIMIT),
        name="moe_experts",
    )(pos1, pos2, tile_expert, next_expert, n_active, hp, jnp.zeros((rows,), I32), wg, wu, wd)


def _combine_kernel(pos1_ref, pos2_ref, y_ref, h_ref, p_ref, g_ref, b_ref, *rest, tm, tiled_copy):
    out_ref = rest[0]
    a1, a2, b1, b2, sem = rest[-5:]
    i = pl.program_id(0)
    last = pl.num_programs(0) - 1

    @pl.when(i == 0)
    def _():
        def issue(t, carry):
            _row_copy(y_ref, pos1_ref[t], a1, t, sem.at[0]).start()
            _row_copy(y_ref, pos2_ref[t], a2, t, sem.at[0]).start()
            return carry

        lax.fori_loop(0, tm, issue, 0)

    def step(c1, c2, cur_sem, n1, n2, nxt_sem):
        _tile_wait(y_ref, c1, cur_sem)
        _tile_wait(y_ref, c2, cur_sem)
        base = jnp.minimum(i + 1, last) * tm
        for t in range(tm):
            _row_copy(y_ref, pos1_ref[base + t], n1, t, nxt_sem).start(priority=0)
            _row_copy(y_ref, pos2_ref[base + t], n2, t, nxt_sem).start(priority=DMA_QUEUES - 1)
        ffn = p_ref[:, 0:1] * _load_row_tiled(c1) + p_ref[:, 1:2] * _load_row_tiled(c2)
        out = _layer_norm(DN_ALPHA * h_ref[...] + ffn, g_ref[...], b_ref[...])
        out_ref[...] = out
        if tiled_copy:
            _store_row_tiled(rest[1], out)

        @pl.when(i == last)
        def _():
            _tile_wait(y_ref, n1, nxt_sem)
            _tile_wait(y_ref, n2, nxt_sem)

    @pl.when(i % 2 == 0)
    def _():
        step(a1, a2, sem.at[0], b1, b2, sem.at[1])

    @pl.when(i % 2 == 1)
    def _():
        step(b1, b2, sem.at[1], a1, a2, sem.at[0])


def _combine(y, h, pos1, pos2, gates, g, b, tm, tiled_copy):
    n = h.shape[0]
    tok = lambda i, a, c: (i, 0)
    fixed = lambda i, a, c: (0, 0)
    out_specs = [pl.BlockSpec((tm, D_MODEL), tok)]
    out_shape = [jax.ShapeDtypeStruct((n, D_MODEL), F32)]
    if tiled_copy:
        out_specs.append(pl.BlockSpec((tm * ROW_SPLIT, LANES), tok))
        out_shape.append(jax.ShapeDtypeStruct((n * ROW_SPLIT, LANES), F32))
    return pl.pallas_call(
        functools.partial(_combine_kernel, tm=tm, tiled_copy=tiled_copy),
        grid_spec=pltpu.PrefetchScalarGridSpec(
            num_scalar_prefetch=2,
            grid=(n // tm,),
            in_specs=[pl.BlockSpec(memory_space=pl.ANY),
                      pl.BlockSpec((tm, D_MODEL), tok),
                      pl.BlockSpec((tm, LANES), tok),
                      pl.BlockSpec((1, D_MODEL), fixed), pl.BlockSpec((1, D_MODEL), fixed)],
            out_specs=out_specs,
            scratch_shapes=[pltpu.VMEM((tm * ROW_SPLIT, LANES), F32)] * 4 + [pltpu.SemaphoreType.DMA((2,))]),
        out_shape=out_shape,
        compiler_params=_cparams("arbitrary"),
        name="moe_combine_ln",
    )(pos1, pos2, y, h, gates, g, b)


def _plan_kernel(ints_ref, cnt_ref, pos_ref, meta_ref, *, layer, chunk):
    tr = MOE_ROW_TILE
    n = ints_ref.shape[1]
    ne = MOE_EXPERTS
    tiles = (cnt_ref[...] + (tr - 1.0)) * (1.0 / tr)
    tiles = tiles.astype(I32).astype(F32)
    lower = lax.broadcasted_iota(I32, (ne, ne), 0) >= lax.broadcasted_iota(I32, (ne, ne), 1)
    ends = _dot(jnp.where(lower, 1.0, 0.0).astype(BF16), tiles.astype(BF16))
    start_col = ((ends - tiles) * tr).astype(I32)[:, 0:1]
    r8 = lax.broadcasted_iota(I32, (8, chunk), 0)
    re = lax.broadcasted_iota(I32, (ne, chunk), 0)
    for c in range(n // chunk):
        blk = ints_ref[:, c * chunk:(c + 1) * chunk]
        s1 = jnp.sum(jnp.where(re == blk[0:1], start_col, 0), 0, keepdims=True)
        s2 = jnp.sum(jnp.where(re == blk[1:2], start_col, 0), 0, keepdims=True)
        pos_ref[:, c * chunk:(c + 1) * chunk] = jnp.where(
            r8 == 0, s1 + blk[2:3], jnp.where(r8 == 1, s2 + blk[3:4], 0))
    width = meta_ref.shape[1]
    tile_id = lax.broadcasted_iota(I32, (ne, width), 1).astype(F32)
    te = jnp.sum(jnp.where(ends[:, 0:1] <= tile_id, 1, 0), 0, keepdims=True)
    te = jnp.minimum(te, ne - 1)
    expert = lax.broadcasted_iota(I32, (ne, width), 0)
    later = jnp.logical_and(expert > te, tiles[:, 0:1] > 0.0)
    nxt = jnp.min(jnp.where(later, expert, ne), 0, keepdims=True)
    nxt = jnp.where(nxt < ne, nxt + layer * ne, -1)
    n_used = ends[ne - 1:ne, 0:1].astype(I32)
    rm = lax.broadcasted_iota(I32, (8, width), 0)
    meta_ref[...] = jnp.where(rm == 0, te + layer * ne,
                              jnp.where(rm == 1, n_used, jnp.where(rm == 2, nxt, 0)))


def _plan(ints, cnt, layer, n_tiles):
    n = ints.shape[1]
    width = -(-n_tiles // LANES) * LANES
    return pl.pallas_call(
        functools.partial(_plan_kernel, layer=layer, chunk=2048),
        out_shape=[jax.ShapeDtypeStruct((8, n), I32), jax.ShapeDtypeStruct((8, width), I32)],
        compiler_params=pltpu.CompilerParams(vmem_limit_bytes=VMEM_LIMIT),
        name="moe_plan",
    )(ints, cnt)


def _moe_layer(h, hp, layer, group_w, group_b, expert_w, expert_b, gate_w, up_w, down_w, ln_g, ln_b,
               tiled_copy):
    n = h.shape[0]
    ew = jnp.transpose(expert_w, (1, 0, 2)).reshape(D_MODEL, MOE_EXPERTS)
    w_r = jnp.zeros((D_MODEL, LANES), F32).at[:, 0:MOE_GROUPS].set(group_w).at[:, 8:8 + MOE_EXPERTS].set(ew)
    b_r = jnp.zeros((LANES,), F32).at[0:MOE_GROUPS].set(group_b).at[8:8 + MOE_EXPERTS].set(expert_b.reshape(-1))
    ints, flts, cnt = _router(h, w_r, b_r.reshape(LANES, 1), 512)
    tr = MOE_ROW_TILE
    n_tiles = (2 * n) // tr + MOE_EXPERTS
    pos, meta = _plan(ints, cnt, layer, n_tiles)
    pos1, pos2 = pos[0], pos[1]

    y = _experts(hp, pos1, pos2, meta[0, :n_tiles], meta[2, :n_tiles], meta[1, :1],
                 gate_w.reshape(-1, D_MODEL, MOE_D_FF),
                 up_w.reshape(-1, D_MODEL, MOE_D_FF),
                 down_w.reshape(-1, MOE_D_FF, D_MODEL))
    return _combine(y, h, pos1, pos2, flts,
                    ln_g.reshape(1, -1), ln_b.reshape(1, -1), COMBINE_TILE, tiled_copy)


def _pad_heads(w, axis):
    parts = []
    h0 = 0
    for _, _, nh in ATTN_GROUPS:
        sl = [slice(None)] * w.ndim
        sl[axis] = slice(h0 * ATTN_HEAD_DIM, (h0 + nh) * ATTN_HEAD_DIM)
        part = w[tuple(sl)]
        pad = [(0, 0)] * w.ndim
        pad[axis] = (0, ATTN_SLAB - nh * ATTN_HEAD_DIM)
        parts.append(jnp.pad(part, pad))
        h0 += nh
    return parts


def _in_proj_kernel(x_ref, w_ref, wdt_ref, z_ref, xbc_ref, dt_ref, *, z_tiles):
    j = pl.program_id(1)
    tn = z_ref.shape[1]
    acc = _dot(x_ref[...].astype(BF16), w_ref[:, pl.ds(pl.multiple_of(j * tn, tn), tn)])

    @pl.when(j < z_tiles)
    def _():
        z_ref[...] = acc

    @pl.when(j >= z_tiles)
    def _():
        xbc_ref[...] = acc

    @pl.when(j == 0)
    def _():
        dt_ref[...] = _dot_f32ish(x_ref[...], wdt_ref[...])


def _in_proj(x, w_zx, w_dt, tm, tn):
    m, k = x.shape
    z_tiles = SSM_D_INNER // tn
    n_tiles = (SSM_D_INNER + SSM_CONV_DIM) // tn
    return pl.pallas_call(
        functools.partial(_in_proj_kernel, z_tiles=z_tiles),
        grid=(m // tm, n_tiles),
        in_specs=[pl.BlockSpec((tm, k), lambda i, j: (i, 0)),
                  pl.BlockSpec(memory_space=pltpu.VMEM),
                  pl.BlockSpec((k, LANES), lambda i, j: (0, 0))],
        out_specs=[pl.BlockSpec((tm, tn), lambda i, j: (i, jnp.minimum(j, z_tiles - 1))),
                   pl.BlockSpec((tm, tn), lambda i, j: (i, jnp.maximum(j - z_tiles, 0))),
                   pl.BlockSpec((tm, LANES), lambda i, j: (i, 0))],
        out_shape=[jax.ShapeDtypeStruct((m, SSM_D_INNER), F32),
                   jax.ShapeDtypeStruct((m, SSM_CONV_DIM), F32),
                   jax.ShapeDtypeStruct((m, LANES), F32)],
        compiler_params=_cparams("parallel", "arbitrary"),
        name="ssm_in_proj",
    )(x, w_zx, w_dt)


def _ssd_layer(h, in_w, conv_w, conv_b, dt_bias, a_log, d_skip, norm_w, out_w, ln_g, ln_b, bsz, seq):
    split = SSM_D_INNER + SSM_CONV_DIM
    dt_w = jnp.pad(in_w[:, split:], ((0, 0), (0, LANES - SSM_HEADS)))
    z, xbc, dt_raw = _in_proj(h, in_w.astype(BF16), dt_w, 1024, 1024)
    pad32 = lambda v: jnp.pad(v, (0, LANES - SSM_HEADS)).reshape(1, LANES)
    y = _ssd(z, xbc, dt_raw, conv_w, conv_b.reshape(1, -1), pad32(dt_bias), pad32(a_log),
             jnp.repeat(d_skip, SSM_HEAD_DIM).reshape(1, -1), norm_w.reshape(1, -1), bsz, seq)
    return _matmul_res_ln(y, out_w.astype(BF16), h, ln_g.reshape(1, -1), ln_b.reshape(1, -1), 1024)


def _qkv_group_weights(kv_w, q_w):
    width = ATTN_HEADS * ATTN_HEAD_DIM
    w_q = _pad_heads(q_w * (ATTN_HEAD_DIM ** -0.5), 1)
    w_k = _pad_heads(kv_w[:, :width], 1)
    w_v = _pad_heads(kv_w[:, width:], 1)
    return [jnp.concatenate([w_q[gi], w_k[gi], w_v[gi]], axis=1).astype(BF16) for gi in range(len(ATTN_GROUPS))]


def _attn_layer(h, h8, qkv_w, o_w, rel_bias, ln_g, ln_b, bsz, seq):
    outs, lses = [], []
    h0 = 0
    for gi, (_, dil, nh) in enumerate(ATTN_GROUPS):
        if dil == 1:
            qkv = _matmul(h, qkv_w[gi], BF16, 1024, 3 * ATTN_SLAB)
        else:
            qkv = _qkv_dilated(h8, qkv_w[gi], dil, max(1024, ATTN_BLOCK * dil))
        bias_p, bias_c = _group_bias(rel_bias, h0, nh, dil)
        o, lse = _attention_group(qkv, bias_p, bias_c, gi, dil, nh, bsz, seq)
        outs.extend(o)
        lses.append(lse)
        h0 += nh
    w_o = jnp.concatenate(_pad_heads(o_w, 0), axis=0).astype(BF16)
    return _attn_out(outs, lses, w_o, h, ln_g.reshape(1, -1), ln_b.reshape(1, -1), 1024)


def kernel(x, ssm_in_w, ssm_conv_w, ssm_conv_b, ssm_dt_bias, ssm_a_log, ssm_d, ssm_norm_w, ssm_out_w,
           kv_w, attn_q_w, attn_o_w, rel_bias, moe_group_w, moe_group_b, moe_expert_w, moe_expert_b,
           moe_gate_w, moe_up_w, moe_down_w, ln_g, ln_b):
    bsz, seq, d = x.shape
    h = x.reshape(bsz * seq, d)
    n_ssd = DEPTH // 2
    h8 = None
    for i in range(DEPTH):
        if i < n_ssd:
            h, hp = _ssd_layer(h, ssm_in_w[i], ssm_conv_w[i], ssm_conv_b[i], ssm_dt_bias[i], ssm_a_log[i],
                               ssm_d[i], ssm_norm_w[i], ssm_out_w[i], ln_g[i, 0], ln_b[i, 0], bsz, seq)
        else:
            j = i - n_ssd
            h, hp = _attn_layer(h, h8, _qkv_group_weights(kv_w, attn_q_w[j]), attn_o_w[j], rel_bias,
                                ln_g[i, 0], ln_b[i, 0], bsz, seq)
        feeds_attention = n_ssd <= i + 1 < DEPTH
        res = _moe_layer(h, hp, i, moe_group_w[i], moe_group_b[i], moe_expert_w[i], moe_expert_b[i],
                         moe_gate_w, moe_up_w, moe_down_w, ln_g[i, 1], ln_b[i, 1], feeds_attention)
        h = res[0]
        h8 = res[1] if feeds_attention else None
    return h.reshape(bsz, seq, d)
```

```python
import functools
import math

import numpy as np
import jax
import jax.numpy as jnp
from jax import lax
from jax.experimental import pallas as pl
from jax.experimental.pallas import tpu as pltpu

F32 = jnp.float32
BF16 = jnp.bfloat16
I32 = jnp.int32

D_MODEL = 1024
DEPTH = 2
DN_ALPHA = (2 * DEPTH) ** 0.25
LN_EPS = 1e-5
LOG2_E = math.log2(math.e)

SSM_D_INNER = 2048
SSM_HEAD_DIM = 64
SSM_HEADS = 32
SSM_GROUPS = 4
SSM_STATE = 128
SSM_CONV = 4
SSM_CHUNK = 128
SSM_CONV_DIM = SSM_D_INNER + 2 * SSM_GROUPS * SSM_STATE

ATTN_HEAD_DIM = 64
ATTN_GROUPS = ((128, 1, 6), (512, 4, 5), (2048, 16, 5))
ATTN_HEADS = 16
ATTN_BLOCK = 128
N_BUCKETS = 32
MAX_DISTANCE = 2048
ATTN_SLAB = 384
NEG_BIG = -1e30
ATTN_ITEMS_PER_PASS = 4

MOE_GROUPS = 4
MOE_EPG = 8
MOE_EXPERTS = MOE_GROUPS * MOE_EPG
MOE_D_FF = 512
MOE_ROW_TILE = 256
COMBINE_TILE = 512

LANES = 128
DMA_QUEUES = 2
VMEM_LIMIT = 48 * 1024 * 1024
EXPERTS_VMEM_LIMIT = 56 * 1024 * 1024


def _cparams(*sem):
    return pltpu.CompilerParams(dimension_semantics=sem, vmem_limit_bytes=VMEM_LIMIT)


def _layer_norm(x, g, b):
    mu = jnp.mean(x, -1, keepdims=True)
    xc = x - mu
    var = jnp.mean(xc * xc, -1, keepdims=True)
    return xc * lax.rsqrt(var + LN_EPS) * g + b


def _split2(x):
    hi = x.astype(BF16)
    lo = (x - hi.astype(F32)).astype(BF16)
    return hi, lo


def _dot(a, b):
    return jnp.dot(a, b, preferred_element_type=F32)


def _dot_f32ish(a, b):
    ah, al = _split2(a)
    bh, bl = _split2(b)
    return _dot(ah, bh) + _dot(al, bh) + _dot(ah, bl)


def _mm_kernel(a_ref, b_ref, o_ref):
    o_ref[...] = _dot(a_ref[...].astype(BF16), b_ref[...]).astype(o_ref.dtype)


def _matmul(a, b, out_dtype, tm, tn):
    m, k = a.shape
    nc = b.shape[1]
    return pl.pallas_call(
        _mm_kernel,
        grid=(m // tm, nc // tn),
        in_specs=[pl.BlockSpec((tm, k), lambda i, j: (i, 0)),
                  pl.BlockSpec((k, tn), lambda i, j: (0, j))],
        out_specs=pl.BlockSpec((tm, tn), lambda i, j: (i, j)),
        out_shape=jax.ShapeDtypeStruct((m, nc), out_dtype),
        compiler_params=_cparams("parallel", "parallel"),
        name="matmul",
    )(a, b)


ROW_SPLIT = D_MODEL // LANES


def _store_row_tiled(ref, val):
    rows = val.shape[0]
    for c in range(ROW_SPLIT):
        ref[pl.ds(c, rows, stride=ROW_SPLIT), :] = val[:, c * LANES:(c + 1) * LANES]


def _load_row_tiled(ref):
    rows = ref.shape[0] // ROW_SPLIT
    return jnp.concatenate([ref[pl.ds(c, rows, stride=ROW_SPLIT), :] for c in range(ROW_SPLIT)], axis=1)


HALF = D_MODEL // 2
U32 = jnp.uint32
HI16 = 0xFFFF0000


def _pack_bf16_pairs(x):
    bits = lax.bitcast_convert_type(x.astype(BF16).astype(F32), U32)
    return (bits[:, :HALF] >> 16) | (bits[:, HALF:] & U32(HI16))


def _unpack_bf16_pairs(w):
    lo = lax.bitcast_convert_type(w << 16, F32).astype(BF16)
    hi = lax.bitcast_convert_type(w & U32(HI16), F32).astype(BF16)
    return lo, hi


def _mm_ln_kernel(a_ref, w_ref, h_ref, g_ref, b_ref, o_ref, op_ref):
    acc = _dot(a_ref[...], w_ref[...])
    out = _layer_norm(DN_ALPHA * h_ref[...] + acc, g_ref[...], b_ref[...])
    o_ref[...] = out
    op_ref[...] = _pack_bf16_pairs(out)


def _matmul_res_ln(a, w, h, g, b, tm):
    m, k = a.shape
    d = w.shape[1]
    return pl.pallas_call(
        _mm_ln_kernel,
        grid=(m // tm,),
        in_specs=[pl.BlockSpec((tm, k), lambda i: (i, 0)),
                  pl.BlockSpec((k, d), lambda i: (0, 0)),
                  pl.BlockSpec((tm, d), lambda i: (i, 0)),
                  pl.BlockSpec((1, d), lambda i: (0, 0)),
                  pl.BlockSpec((1, d), lambda i: (0, 0))],
        out_specs=[pl.BlockSpec((tm, d), lambda i: (i, 0)),
                   pl.BlockSpec((tm, HALF), lambda i: (i, 0))],
        out_shape=[jax.ShapeDtypeStruct((m, d), F32),
                   jax.ShapeDtypeStruct((m, HALF), U32)],
        compiler_params=_cparams("parallel"),
        name="matmul_res_ln",
    )(a, w, h, g, b)


SSD_CHUNKS_PER_STEP = 2


def _ssd_kernel(z_ref, xbc_ref, dt_ref, cw_ref, cb_ref, dtb_ref, alog_ref, dsk_ref, nw_ref,
                y_ref, xe_ref, st_ref):
    @pl.when(pl.program_id(1) == 0)
    def _():
        xe_ref[:, 0:8, :] = jnp.zeros((xe_ref.shape[0], 8, LANES), F32)
        st_ref[...] = jnp.zeros_like(st_ref)

    for sub in range(SSD_CHUNKS_PER_STEP):
        _ssd_chunk(pl.ds(sub * SSM_CHUNK, SSM_CHUNK), z_ref, xbc_ref, dt_ref, cw_ref, cb_ref, dtb_ref,
                   alog_ref, dsk_ref, nw_ref, y_ref, xe_ref, st_ref)


def _ssd_chunk(rows, z_ref, xbc_ref, dt_ref, cw_ref, cb_ref, dtb_ref, alog_ref, dsk_ref, nw_ref,
               y_ref, xe_ref, st_ref):
    q = SSM_CHUNK
    w = cw_ref[...]
    bias = cb_ref[...]
    act = []
    for c in range(SSM_CONV_DIM // LANES):
        cols = slice(c * LANES, (c + 1) * LANES)
        u = xbc_ref[rows, cols]
        xe_ref[c, 8:8 + q, :] = u
        conv = (bias[:, cols] + w[3:4, cols] * u + w[2:3, cols] * xe_ref[c, 7:7 + q, :]
                + w[1:2, cols] * xe_ref[c, 6:6 + q, :] + w[0:1, cols] * xe_ref[c, 5:5 + q, :])
        xe_ref[c, 0:8, :] = xe_ref[c, q:q + 8, :]
        act.append(conv * jax.nn.sigmoid(conv))

    pre = dt_ref[rows, :] + dtb_ref[...]
    dt = jnp.maximum(pre, 0.0) + jnp.log(1.0 + jnp.exp(-jnp.abs(pre)))
    adt = dt * (-jnp.exp(alog_ref[...]) * LOG2_E)

    row = lax.broadcasted_iota(I32, (q, q), 0)
    col = lax.broadcasted_iota(I32, (q, q), 1)
    tril = row >= col
    tri_b = jnp.where(tril, 1.0, 0.0).astype(BF16)
    a_hi = adt.astype(BF16)
    r1 = adt - a_hi.astype(F32)
    a_mid = r1.astype(BF16)
    a_lo = (r1 - a_mid.astype(F32)).astype(BF16)
    acs = _dot(tri_b, a_hi) + _dot(tri_b, a_mid) + _dot(tri_b, a_lo)
    acs_t = acs.T
    acs_dt_t = acs_t - jnp.log(dt.T) * LOG2_E
    eacs = jnp.exp2(acs)
    left = col < SSM_HEAD_DIM

    for g in range(SSM_GROUPS):
        b0 = SSM_D_INNER + g * SSM_STATE
        c0 = SSM_D_INNER + SSM_GROUPS * SSM_STATE + g * SSM_STATE
        bm = act[b0 // LANES]
        cm = act[c0 // LANES].astype(BF16)
        cb = lax.dot_general(cm, bm.astype(BF16), (((1,), (1,)), ((), ())),
                             preferred_element_type=F32)
        bm_t = bm.T
        gs = g * 512
        y_off = _dot(cm, st_ref[:, gs:gs + 512].astype(BF16))
        slabs = []
        for pr in range(4):
            ha = g * 8 + pr * 2
            hb = ha + 1
            cs = gs + pr * LANES
            x2 = act[cs // LANES]
            x2b = x2.astype(BF16)
            ys, ups = [], []
            for h in (ha, hb):
                a_col = acs[:, h:h + 1]
                a_src = acs_dt_t[h:h + 1, :]
                decay = jnp.where(tril, jnp.exp2(a_col - a_src), 0.0)
                ys.append(_dot((cb * decay).astype(BF16), x2b))
                to_end = jnp.exp2(acs_t[h:h + 1, q - 1:q] - a_src)
                ups.append(_dot((bm_t * to_end).astype(BF16), x2b))
            y_diag = jnp.where(left, ys[0], ys[1])
            upd = jnp.where(left, ups[0], ups[1])
            e2 = jnp.where(left, eacs[:, ha:ha + 1], eacs[:, hb:hb + 1])
            cd = jnp.where(left[0:1, :], eacs[q - 1:q, ha:ha + 1], eacs[q - 1:q, hb:hb + 1])
            y2 = y_diag + y_off[:, pr * LANES:(pr + 1) * LANES] * e2 + dsk_ref[:, cs:cs + LANES] * x2
            st_ref[:, cs:cs + LANES] = st_ref[:, cs:cs + LANES] * cd + upd
            slabs.append(y2)
        yg = jnp.concatenate(slabs, axis=1)
        zg = z_ref[rows, gs:gs + 512]
        yg = yg * (zg * jax.nn.sigmoid(zg))
        ms = jnp.mean(yg * yg, -1, keepdims=True)
        y_ref[rows, gs:gs + 512] = (yg * lax.rsqrt(ms + LN_EPS) * nw_ref[:, gs:gs + 512]).astype(y_ref.dtype)


def _ssd(z, xbc, dt_raw, conv_w, conv_b, dt_bias, a_log, d_rep, norm_w, bsz, seq):
    n = z.shape[0]
    q = SSM_CHUNK * SSD_CHUNKS_PER_STEP
    nchunk = seq // q
    tok = lambda b, c: (b * nchunk + c, 0)
    fixed = lambda b, c: (0, 0)
    return pl.pallas_call(
        _ssd_kernel,
        grid=(bsz, nchunk),
        in_specs=[pl.BlockSpec((q, SSM_D_INNER), tok),
                  pl.BlockSpec((q, SSM_CONV_DIM), tok),
                  pl.BlockSpec((q, LANES), tok),
                  pl.BlockSpec((SSM_CONV, SSM_CONV_DIM), fixed),
                  pl.BlockSpec((1, SSM_CONV_DIM), fixed),
                  pl.BlockSpec((1, LANES), fixed),
                  pl.BlockSpec((1, LANES), fixed),
                  pl.BlockSpec((1, SSM_D_INNER), fixed),
                  pl.BlockSpec((1, SSM_D_INNER), fixed)],
        out_specs=pl.BlockSpec((q, SSM_D_INNER), tok),
        out_shape=jax.ShapeDtypeStruct((n, SSM_D_INNER), BF16),
        scratch_shapes=[pltpu.VMEM((SSM_CONV_DIM // LANES, SSM_CHUNK + 8, LANES), F32),
                        pltpu.VMEM((SSM_STATE, SSM_D_INNER), F32)],
        compiler_params=_cparams("parallel", "arbitrary"),
        name="ssd_chunk",
    )(z, xbc, dt_raw, conv_w, conv_b, dt_bias, a_log, d_rep, norm_w)


def _residue_major_pieces(h8_ref, n_tokens, dil):
    span = ATTN_BLOCK * dil
    pieces = []
    for blk in range(n_tokens // span):
        for r in range(dil):
            first = (blk * span + r) * ROW_SPLIT
            pieces.append(jnp.concatenate(
                [h8_ref[pl.ds(first + c, ATTN_BLOCK, stride=ROW_SPLIT * dil), :] for c in range(ROW_SPLIT)],
                axis=1).astype(BF16))
    return pieces


def _qkv_dilated_kernel(h8_ref, w_ref, o_ref, *, dil):
    per_dot = 4
    rows = _residue_major_pieces(h8_ref, o_ref.shape[0], dil)
    for k in range(0, len(rows), per_dot):
        x = jnp.concatenate(rows[k:k + per_dot], axis=0)
        o_ref[k * ATTN_BLOCK:(k + per_dot) * ATTN_BLOCK, :] = _dot(x, w_ref[...]).astype(o_ref.dtype)


def _qkv_dilated(h8, w, dil, tm):
    n = h8.shape[0] // ROW_SPLIT
    nc = w.shape[1]
    return pl.pallas_call(
        functools.partial(_qkv_dilated_kernel, dil=dil),
        grid=(n // tm,),
        in_specs=[pl.BlockSpec((tm * ROW_SPLIT, LANES), lambda i: (i, 0)),
                  pl.BlockSpec((D_MODEL, nc), lambda i: (0, 0))],
        out_specs=pl.BlockSpec((tm, nc), lambda i: (i, 0)),
        out_shape=jax.ShapeDtypeStruct((n, nc), BF16),
        compiler_params=_cparams("parallel"),
        name=f"qkv_dil{dil}",
    )(h8, w)


def _attn_kernel(*refs, nh, dil, has_prev, blocks, per_pass):
    span = ATTN_BLOCK * dil
    if has_prev:
        q_ref, kp_ref, kc_ref, vp_ref, vc_ref, bp_ref, bc_ref = refs[:7]
        out_refs = refs[7:]
        first_pen = jnp.where(pl.program_id(1) > 0, 0.0, NEG_BIG)
    else:
        q_ref, kc_ref, vc_ref, bc_ref = refs[:4]
        out_refs = refs[4:]
    s_scr, p_scr, max_scr, den_scr = out_refs[-4:]
    o_refs, lse_ref = out_refs[:-5], out_refs[-5]
    ones = jnp.ones((s_scr.shape[2], LANES), BF16)
    lane = lax.broadcasted_iota(I32, (ATTN_BLOCK, LANES), 1)
    left = lane < ATTN_HEAD_DIM
    zero = jnp.zeros((), BF16)
    nt = (((1,), (1,)), ((), ()))

    def block_start(row):
        return row if isinstance(row, int) else pl.multiple_of(row, ATTN_BLOCK)

    def work(items):
        info = []
        for slot, (blk, r) in enumerate(items):
            rows = pl.ds(block_start(blk * span + r * ATTN_BLOCK), ATTN_BLOCK)
            dst = pl.ds(blk * span + r, ATTN_BLOCK, stride=dil)
            kprev = vprev = prows = pen = None
            if has_prev and blk == 0:
                kprev, vprev, pen = kp_ref, vp_ref, first_pen
                prows = pl.ds(block_start(r * ATTN_BLOCK), ATTN_BLOCK)
            elif has_prev:
                kprev, vprev, pen = kc_ref, vc_ref, 0.0
                prows = pl.ds(block_start((blk - 1) * span + r * ATTN_BLOCK), ATTN_BLOCK)
            info.append((slot * nh, rows, dst, kprev, vprev, prows, pen))
        for base, rows, dst, kprev, vprev, prows, pen in info:
            for hh in range(nh):
                cols = pl.ds(hh // 2 * LANES, LANES)
                qm = jnp.where(left if hh % 2 == 0 else ~left, q_ref[rows, cols], zero)
                s_c = lax.dot_general(qm, kc_ref[rows, cols], nt, preferred_element_type=F32) + bc_ref[hh]
                if has_prev:
                    s_p = lax.dot_general(qm, kprev[prows, cols], nt, preferred_element_type=F32)
                    s_scr[base + hh, :, 0:ATTN_BLOCK] = s_p + (bp_ref[hh] + pen)
                    s_scr[base + hh, :, ATTN_BLOCK:] = s_c
                else:
                    s_scr[base + hh] = s_c
        for base, *_ in info:
            for hh in range(nh):
                s = s_scr[base + hh]
                m = jnp.max(s, -1, keepdims=True)
                p = jnp.exp(s - m)
                p_scr[base + hh] = p.astype(BF16)
                max_scr[base + hh] = m
                if not has_prev:
                    den_scr[base + hh] = jnp.sum(p, -1, keepdims=True)
        for base, rows, dst, kprev, vprev, prows, pen in info:
            lse_sum = jnp.zeros((ATTN_BLOCK, 1), F32)
            for pr in range(ATTN_SLAB // LANES):
                cols = pl.ds(pr * LANES, LANES)
                halves = []
                for hh in (pr * 2, pr * 2 + 1):
                    if hh >= nh:
                        halves.append(jnp.zeros((ATTN_BLOCK, LANES), F32))
                        continue
                    if has_prev:
                        den = _dot(p_scr[base + hh], ones)
                        den1 = den[:, 0:1]
                        o = (_dot(p_scr[base + hh, :, 0:ATTN_BLOCK], vprev[prows, cols])
                             + _dot(p_scr[base + hh, :, ATTN_BLOCK:], vc_ref[rows, cols]))
                    else:
                        den = den1 = den_scr[base + hh]
                        o = _dot(p_scr[base + hh], vc_ref[rows, cols])
                    halves.append(o * (1.0 / den))
                    lse_sum = lse_sum + (max_scr[base + hh] + jnp.log(den1))
                o_refs[pr][dst, :] = jnp.where(left, halves[0], halves[1])
            lse_ref[dst, :] = jnp.broadcast_to(lse_sum * (1.0 / nh), (ATTN_BLOCK, LANES))

    if dil == 1:
        for b0 in range(0, blocks, per_pass):
            work([(b, 0) for b in range(b0, min(blocks, b0 + per_pass))])
    else:
        for blk in range(blocks):
            def body(it, carry, blk=blk):
                work([(blk, it * per_pass + k) for k in range(per_pass)])
                return carry

            lax.fori_loop(0, dil // per_pass, body, 0)


def _attention_group(qkv, bias_p, bias_c, gi, dil, nh, bsz, seq):
    span = ATTN_BLOCK * dil
    nb = seq // span
    has_prev = nb > 1
    blocks = max(1, 512 // span)
    steps = nb // blocks
    blk = (blocks * span, ATTN_SLAB)
    cur = lambda which: (lambda b, n: (b * steps + n, which))
    prev = lambda which: (lambda b, n: (b * nb + jnp.maximum(n * blocks - 1, 0), which))
    fixed = lambda b, n: (0, 0, 0)
    tab = pl.BlockSpec((nh, ATTN_BLOCK, ATTN_BLOCK), fixed)
    if has_prev:
        pblk = (span, ATTN_SLAB)
        in_specs = [pl.BlockSpec(blk, cur(0)), pl.BlockSpec(pblk, prev(1)), pl.BlockSpec(blk, cur(1)),
                    pl.BlockSpec(pblk, prev(2)), pl.BlockSpec(blk, cur(2)), tab, tab]
        args = (qkv, qkv, qkv, qkv, qkv, bias_p, bias_c)
    else:
        in_specs = [pl.BlockSpec(blk, cur(0)), pl.BlockSpec(blk, cur(1)), pl.BlockSpec(blk, cur(2)), tab]
        args = (qkv, qkv, qkv, bias_c)
    n_out = ATTN_SLAB // LANES + 1
    keys = 2 * ATTN_BLOCK if has_prev else ATTN_BLOCK
    slots = min(dil * blocks, ATTN_ITEMS_PER_PASS if has_prev else ATTN_ITEMS_PER_PASS // 2)
    outs = pl.pallas_call(
        functools.partial(_attn_kernel, nh=nh, dil=dil, has_prev=has_prev, blocks=blocks, per_pass=slots),
        grid=(bsz, steps),
        in_specs=in_specs,
        out_specs=[pl.BlockSpec((blocks * span, LANES), lambda b, n: (b * steps + n, 0))] * n_out,
        out_shape=[jax.ShapeDtypeStruct((bsz * seq, LANES), F32)] * n_out,
        scratch_shapes=[pltpu.VMEM((slots * nh, ATTN_BLOCK, keys), F32),
                        pltpu.VMEM((slots * nh, ATTN_BLOCK, keys), BF16),
                        pltpu.VMEM((slots * nh, ATTN_BLOCK, 1), F32),
                        pltpu.VMEM((slots * nh, ATTN_BLOCK, 1), F32)],
        compiler_params=_cparams("parallel", "parallel"),
        name=f"dilated_attn_g{gi}",
    )(*args)
    return outs[:-1], outs[-1]


def _attn_out_kernel(*refs):
    n_pairs = ATTN_SLAB // LANES
    n_groups = len(ATTN_GROUPS)
    o_refs = refs[:n_groups * n_pairs]
    l_refs = refs[n_groups * n_pairs:n_groups * (n_pairs + 1)]
    w_ref, h_ref, g_ref, b_ref, out_ref, outp_ref = refs[n_groups * (n_pairs + 1):]
    ls = [r[:, 0:1] for r in l_refs]
    m = jnp.maximum(jnp.maximum(ls[0], ls[1]), ls[2])
    es = [jnp.exp(v - m) for v in ls]
    scale = n_groups / (es[0] + es[1] + es[2])
    slabs = []
    for gi in range(n_groups):
        wt = es[gi] * scale
        slabs.extend((r[...] * wt).astype(BF16) for r in o_refs[gi * n_pairs:(gi + 1) * n_pairs])
    acc = _dot(jnp.concatenate(slabs, axis=1), w_ref[...])
    out = _layer_norm(DN_ALPHA * h_ref[...] + acc, g_ref[...], b_ref[...])
    out_ref[...] = out
    outp_ref[...] = _pack_bf16_pairs(out)


def _attn_out(outs, lses, w, h, g, b, tm):
    n = h.shape[0]
    tok = lambda i: (i, 0)
    fixed2 = lambda i: (0, 0)
    return pl.pallas_call(
        _attn_out_kernel,
        grid=(n // tm,),
        in_specs=[pl.BlockSpec((tm, LANES), tok)] * (len(outs) + len(lses))
                 + [pl.BlockSpec((len(ATTN_GROUPS) * ATTN_SLAB, D_MODEL), lambda i: (0, 0)),
                    pl.BlockSpec((tm, D_MODEL), tok),
                    pl.BlockSpec((1, D_MODEL), fixed2), pl.BlockSpec((1, D_MODEL), fixed2)],
        out_specs=[pl.BlockSpec((tm, D_MODEL), tok), pl.BlockSpec((tm, HALF), tok)],
        out_shape=[jax.ShapeDtypeStruct((n, D_MODEL), F32),
                   jax.ShapeDtypeStruct((n, HALF), U32)],
        compiler_params=_cparams("parallel"),
        name="attn_out_ln",
    )(*outs, *lses, w, h, g, b)


def _t5_bucket(dist):
    max_exact = N_BUCKETS // 2
    n = np.maximum(dist, 1).astype(np.float64)
    large = max_exact + (np.log(n / max_exact) / np.log(MAX_DISTANCE / max_exact)
                         * (N_BUCKETS - max_exact)).astype(np.int32)
    large = np.minimum(large, N_BUCKETS - 1)
    return np.where(dist < max_exact, dist, large).astype(np.int32)


def _group_bias(rel_bias, h0, nh, dil):
    qi = np.arange(ATTN_BLOCK)[:, None]
    ki = np.arange(ATTN_BLOCK)[None, :]
    tabs = []
    for delta, band in ((qi + ATTN_BLOCK - ki, ki >= qi), (qi - ki, ki <= qi)):
        bucket = _t5_bucket(np.clip(delta, 0, None) * dil)
        onehot = (bucket[..., None] == np.arange(N_BUCKETS)).astype(np.float32)
        t = jnp.einsum("qkb,bh->hqk", onehot, rel_bias[:, h0:h0 + nh], precision=lax.Precision.HIGHEST)
        tabs.append(jnp.where(band[None], t, NEG_BIG).astype(F32))
    return tabs


def _router_kernel(h_ref, w_ref, b_ref, ints_ref, flts_ref, cnt_ref, carry_ref, before_ref, *, tm):
    @pl.when(pl.program_id(0) == 0)
    def _():
        carry_ref[...] = jnp.zeros_like(carry_ref)
        tr = lax.broadcasted_iota(I32, (tm, tm), 0)
        tc = lax.broadcasted_iota(I32, (tm, tm), 1)
        before_ref[...] = jnp.where(tr < tc, 1.0, 0.0).astype(BF16)

    lt = _dot_f32ish(h_ref[...], w_ref[...]).T + b_ref[...]
    gl = lt[0:MOE_GROUPS]
    r4 = lax.broadcasted_iota(I32, (MOE_GROUPS, tm), 0)
    gmax = jnp.max(gl, 0, keepdims=True)
    gidx = jnp.min(jnp.where(gl == gmax, r4, MOE_GROUPS), 0, keepdims=True)
    gval = 1.0 / jnp.sum(jnp.exp(gl - gmax), 0, keepdims=True)

    esel = jnp.zeros((MOE_EPG, tm), F32)
    for g in range(MOE_GROUPS):
        esel = jnp.where(gidx == g, lt[8 + g * MOE_EPG:8 + (g + 1) * MOE_EPG], esel)
    r8 = lax.broadcasted_iota(I32, (MOE_EPG, tm), 0)
    v1 = jnp.max(esel, 0, keepdims=True)
    i1 = jnp.min(jnp.where(esel == v1, r8, MOE_EPG), 0, keepdims=True)
    rest = jnp.where(r8 == i1, -jnp.inf, esel)
    v2 = jnp.max(rest, 0, keepdims=True)
    i2 = jnp.min(jnp.where(rest == v2, r8, MOE_EPG), 0, keepdims=True)
    t = jnp.exp(v2 - v1)
    p1 = gval / (1.0 + t)
    p2 = p1 * t
    e1 = gidx * MOE_EPG + i1
    e2 = gidx * MOE_EPG + i2

    r32 = lax.broadcasted_iota(I32, (MOE_EXPERTS, tm), 0)
    oh1 = r32 == e1
    oh2 = r32 == e2
    oh = jnp.where(oh1 | oh2, 1.0, 0.0)
    base = _dot(oh.astype(BF16), before_ref[...]) + carry_ref[:, 0:1]
    rank1 = jnp.sum(jnp.where(oh1, base, 0.0), 0, keepdims=True).astype(I32)
    rank2 = jnp.sum(jnp.where(oh2, base, 0.0), 0, keepdims=True).astype(I32)
    carry_ref[...] = carry_ref[...] + jnp.sum(oh, 1, keepdims=True)
    cnt_ref[...] = carry_ref[...]

    ints_ref[...] = jnp.where(r8 == 0, e1, jnp.where(r8 == 1, e2, jnp.where(r8 == 2, rank1,
                              jnp.where(r8 == 3, rank2, 0))))
    r128 = lax.broadcasted_iota(I32, (LANES, tm), 0)
    flts_ref[...] = jnp.where(r128 == 0, p1, jnp.where(r128 == 1, p2, 0.0)).T


def _router(h, w_r, b_r, tm):
    n = h.shape[0]
    return pl.pallas_call(
        functools.partial(_router_kernel, tm=tm),
        grid=(n // tm,),
        in_specs=[pl.BlockSpec((tm, D_MODEL), lambda i: (i, 0)),
                  pl.BlockSpec((D_MODEL, LANES), lambda i: (0, 0)),
                  pl.BlockSpec((LANES, 1), lambda i: (0, 0))],
        out_specs=[pl.BlockSpec((8, tm), lambda i: (0, i)),
                   pl.BlockSpec((tm, LANES), lambda i: (i, 0)),
                   pl.BlockSpec((MOE_EXPERTS, LANES), lambda i: (0, 0))],
        out_shape=[jax.ShapeDtypeStruct((8, n), I32),
                   jax.ShapeDtypeStruct((n, LANES), F32),
                   jax.ShapeDtypeStruct((MOE_EXPERTS, LANES), F32)],
        scratch_shapes=[pltpu.VMEM((MOE_EXPERTS, LANES), F32), pltpu.VMEM((tm, tm), BF16)],
        compiler_params=_cparams("arbitrary"),
        name="moe_router",
    )(h, w_r, b_r)


def _row_copy(src_ref, s, dst_ref, d, sem):
    def first(r):
        return r * ROW_SPLIT if isinstance(r, int) else pl.multiple_of(r * ROW_SPLIT, ROW_SPLIT)

    return pltpu.make_async_copy(src_ref.at[pl.ds(first(s), ROW_SPLIT), :],
                                 dst_ref.at[pl.ds(first(d), ROW_SPLIT), :], sem)


def _tile_wait(src_ref, dst_ref, sem):
    pltpu.make_async_copy(src_ref.at[pl.ds(0, dst_ref.shape[0]), :], dst_ref, sem).wait()


def _experts_kernel(pos1_ref, pos2_ref, te_ref, nxt_ref, na_ref, hp_hbm, zeros_ref, wg_ref, wu_ref, wd_ref,
                    y_ref, hp_ref, xa, xb, wg_f, wu_f, wd_f, wg_s, wu_s, wd_s, row_tok, slot_ref,
                    sem, hsem, wsem):
    i = pl.program_id(0)
    n_tiles = pl.num_programs(0)
    tr = MOE_ROW_TILE
    active = i < na_ref[0]
    changed = jnp.logical_or(i == 0, te_ref[i] != te_ref[jnp.maximum(i - 1, 0)])

    def weight_copies(expert, slot):
        return [pltpu.make_async_copy(src.at[expert], dst.at[slot], wsem.at[slot])
                for src, dst in ((wg_ref, wg_f), (wu_ref, wu_f), (wd_ref, wd_f))]

    @pl.when(i == 0)
    def _():
        slot_ref[0] = 0
        rows_in = pltpu.make_async_copy(hp_hbm, hp_ref, hsem)
        rows_in.start()
        for c in weight_copies(te_ref[0], 0):
            c.start()
        clear = pltpu.make_async_copy(zeros_ref, row_tok, sem)
        clear.start()
        clear.wait()

        def invert(t, carry):
            row_tok[pos1_ref[t]] = t
            row_tok[pos2_ref[t]] = t
            return carry

        lax.fori_loop(0, pos1_ref.shape[0], invert, 0, unroll=16)
        rows_in.wait()

        def pick(r, carry):
            xa[pl.ds(r, 1), :] = hp_ref[pl.ds(row_tok[r], 1), :]
            return carry

        lax.fori_loop(0, tr, pick, 0, unroll=8)

    @pl.when(jnp.logical_and(active, changed))
    def _():
        slot = slot_ref[0]
        for c in weight_copies(te_ref[i], slot):
            c.wait()
        following = nxt_ref[i]

        @pl.when(following >= 0)
        def _():
            for c in weight_copies(following, 1 - slot):
                c.start()

        wg_s[...] = wg_f[slot].astype(BF16)
        wu_s[...] = wu_f[slot].astype(BF16)
        wd_s[...] = wd_f[slot].astype(BF16)
        slot_ref[0] = 1 - slot

    def step(cur, nxt):
        base = jnp.minimum(i + 1, n_tiles - 1) * tr
        for r in range(tr):
            nxt[r:r + 1, :] = hp_ref[pl.ds(row_tok[base + r], 1), :]
        lo, hi = _unpack_bf16_pairs(cur[...])
        gate = _dot(lo, wg_s[0:HALF, :]) + _dot(hi, wg_s[HALF:, :])
        up = _dot(lo, wu_s[0:HALF, :]) + _dot(hi, wu_s[HALF:, :])
        hid = (gate * jax.nn.sigmoid(gate) * up).astype(BF16)
        _store_row_tiled(y_ref, _dot(hid, wd_s[...]))

    even = i % 2 == 0

    @pl.when(jnp.logical_and(active, even))
    def _():
        step(xa, xb)

    @pl.when(jnp.logical_and(active, jnp.logical_not(even)))
    def _():
        step(xb, xa)

    @pl.when(jnp.logical_not(active))
    def _():
        y_ref[...] = jnp.zeros_like(y_ref)


def _experts(hp, pos1, pos2, tile_expert, next_expert, n_active, wg, wu, wd):
    tr = MOE_ROW_TILE
    n_tiles = tile_expert.shape[0]
    rows = n_tiles * tr
    hbm = pl.BlockSpec(memory_space=pl.ANY)
    return pl.pallas_call(
        _experts_kernel,
        grid_spec=pltpu.PrefetchScalarGridSpec(
            num_scalar_prefetch=5,
            grid=(n_tiles,),
            in_specs=[hbm, hbm, hbm, hbm, hbm],
            out_specs=pl.BlockSpec((tr * ROW_SPLIT, LANES), lambda i, *_: (i, 0)),
            scratch_shapes=[pltpu.VMEM(hp.shape, U32),
                            pltpu.VMEM((tr, HALF), U32),
                            pltpu.VMEM((tr, HALF), U32),
                            pltpu.VMEM((2, D_MODEL, MOE_D_FF), F32),
                            pltpu.VMEM((2, D_MODEL, MOE_D_FF), F32),
                            pltpu.VMEM((2, MOE_D_FF, D_MODEL), F32),
                            pltpu.VMEM((D_MODEL, MOE_D_FF), BF16),
                            pltpu.VMEM((D_MODEL, MOE_D_FF), BF16),
                            pltpu.VMEM((MOE_D_FF, D_MODEL), BF16),
                            pltpu.SMEM((rows,), I32),
                            pltpu.SMEM((1,), I32),
                            pltpu.SemaphoreType.DMA(()),
                            pltpu.SemaphoreType.DMA(()),
                            pltpu.SemaphoreType.DMA((2,))]),
        out_shape=jax.ShapeDtypeStruct((rows * ROW_SPLIT, LANES), F32),
        compiler_params=pltpu.CompilerParams(dimension_semantics=("arbitrary",),
                                             vmem_limit_bytes=EXPERTS_VMEM_LIMIT),
        name="moe_experts",
    )(pos1, pos2, tile_expert, next_expert, n_active, hp, jnp.zeros((rows,), I32), wg, wu, wd)


def _combine_kernel(pos1_ref, pos2_ref, y_ref, h_ref, p_ref, g_ref, b_ref, *rest, tm, tiled_copy):
    out_ref = rest[0]
    a1, a2, b1, b2, sem = rest[-5:]
    i = pl.program_id(0)
    last = pl.num_programs(0) - 1

    @pl.when(i == 0)
    def _():
        def issue(t, carry):
            _row_copy(y_ref, pos1_ref[t], a1, t, sem.at[0]).start()
            _row_copy(y_ref, pos2_ref[t], a2, t, sem.at[0]).start()
            return carry

        lax.fori_loop(0, tm, issue, 0)

    def step(c1, c2, cur_sem, n1, n2, nxt_sem):
        _tile_wait(y_ref, c1, cur_sem)
        _tile_wait(y_ref, c2, cur_sem)
        base = jnp.minimum(i + 1, last) * tm
        for t in range(tm):
            _row_copy(y_ref, pos1_ref[base + t], n1, t, nxt_sem).start(priority=0)
            _row_copy(y_ref, pos2_ref[base + t], n2, t, nxt_sem).start(priority=DMA_QUEUES - 1)
        ffn = p_ref[:, 0:1] * _load_row_tiled(c1) + p_ref[:, 1:2] * _load_row_tiled(c2)
        out = _layer_norm(DN_ALPHA * h_ref[...] + ffn, g_ref[...], b_ref[...])
        out_ref[...] = out
        if tiled_copy:
            _store_row_tiled(rest[1], out)

        @pl.when(i == last)
        def _():
            _tile_wait(y_ref, n1, nxt_sem)
            _tile_wait(y_ref, n2, nxt_sem)

    @pl.when(i % 2 == 0)
    def _():
        step(a1, a2, sem.at[0], b1, b2, sem.at[1])

    @pl.when(i % 2 == 1)
    def _():
        step(b1, b2, sem.at[1], a1, a2, sem.at[0])


def _combine(y, h, pos1, pos2, gates, g, b, tm, tiled_copy):
    n = h.shape[0]
    tok = lambda i, a, c: (i, 0)
    fixed = lambda i, a, c: (0, 0)
    out_specs = [pl.BlockSpec((tm, D_MODEL), tok)]
    out_shape = [jax.ShapeDtypeStruct((n, D_MODEL), F32)]
    if tiled_copy:
        out_specs.append(pl.BlockSpec((tm * ROW_SPLIT, LANES), tok))
        out_shape.append(jax.ShapeDtypeStruct((n * ROW_SPLIT, LANES), F32))
    return pl.pallas_call(
        functools.partial(_combine_kernel, tm=tm, tiled_copy=tiled_copy),
        grid_spec=pltpu.PrefetchScalarGridSpec(
            num_scalar_prefetch=2,
            grid=(n // tm,),
            in_specs=[pl.BlockSpec(memory_space=pl.ANY),
                      pl.BlockSpec((tm, D_MODEL), tok),
                      pl.BlockSpec((tm, LANES), tok),
                      pl.BlockSpec((1, D_MODEL), fixed), pl.BlockSpec((1, D_MODEL), fixed)],
            out_specs=out_specs,
            scratch_shapes=[pltpu.VMEM((tm * ROW_SPLIT, LANES), F32)] * 4 + [pltpu.SemaphoreType.DMA((2,))]),
        out_shape=out_shape,
        compiler_params=_cparams("arbitrary"),
        name="moe_combine_ln",
    )(pos1, pos2, y, h, gates, g, b)


def _plan_kernel(ints_ref, cnt_ref, pos_ref, meta_ref, *, layer, chunk):
    tr = MOE_ROW_TILE
    n = ints_ref.shape[1]
    ne = MOE_EXPERTS
    tiles = (cnt_ref[...] + (tr - 1.0)) * (1.0 / tr)
    tiles = tiles.astype(I32).astype(F32)
    lower = lax.broadcasted_iota(I32, (ne, ne), 0) >= lax.broadcasted_iota(I32, (ne, ne), 1)
    ends = _dot(jnp.where(lower, 1.0, 0.0).astype(BF16), tiles.astype(BF16))
    start_col = ((ends - tiles) * tr).astype(I32)[:, 0:1]
    r8 = lax.broadcasted_iota(I32, (8, chunk), 0)
    re = lax.broadcasted_iota(I32, (ne, chunk), 0)
    for c in range(n // chunk):
        blk = ints_ref[:, c * chunk:(c + 1) * chunk]
        s1 = jnp.sum(jnp.where(re == blk[0:1], start_col, 0), 0, keepdims=True)
        s2 = jnp.sum(jnp.where(re == blk[1:2], start_col, 0), 0, keepdims=True)
        pos_ref[:, c * chunk:(c + 1) * chunk] = jnp.where(
            r8 == 0, s1 + blk[2:3], jnp.where(r8 == 1, s2 + blk[3:4], 0))
    width = meta_ref.shape[1]
    tile_id = lax.broadcasted_iota(I32, (ne, width), 1).astype(F32)
    te = jnp.sum(jnp.where(ends[:, 0:1] <= tile_id, 1, 0), 0, keepdims=True)
    te = jnp.minimum(te, ne - 1)
    expert = lax.broadcasted_iota(I32, (ne, width), 0)
    later = jnp.logical_and(expert > te, tiles[:, 0:1] > 0.0)
    nxt = jnp.min(jnp.where(later, expert, ne), 0, keepdims=True)
    nxt = jnp.where(nxt < ne, nxt + layer * ne, -1)
    n_used = ends[ne - 1:ne, 0:1].astype(I32)
    rm = lax.broadcasted_iota(I32, (8, width), 0)
    meta_ref[...] = jnp.where(rm == 0, te + layer * ne,
                              jnp.where(rm == 1, n_used, jnp.where(rm == 2, nxt, 0)))


def _plan(ints, cnt, layer, n_tiles):
    n = ints.shape[1]
    width = -(-n_tiles // LANES) * LANES
    return pl.pallas_call(
        functools.partial(_plan_kernel, layer=layer, chunk=2048),
        out_shape=[jax.ShapeDtypeStruct((8, n), I32), jax.ShapeDtypeStruct((8, width), I32)],
        compiler_params=pltpu.CompilerParams(vmem_limit_bytes=VMEM_LIMIT),
        name="moe_plan",
    )(ints, cnt)


def _moe_layer(h, hp, layer, group_w, group_b, expert_w, expert_b, gate_w, up_w, down_w, ln_g, ln_b,
               tiled_copy):
    n = h.shape[0]
    ew = jnp.transpose(expert_w, (1, 0, 2)).reshape(D_MODEL, MOE_EXPERTS)
    w_r = jnp.zeros((D_MODEL, LANES), F32).at[:, 0:MOE_GROUPS].set(group_w).at[:, 8:8 + MOE_EXPERTS].set(ew)
    b_r = jnp.zeros((LANES,), F32).at[0:MOE_GROUPS].set(group_b).at[8:8 + MOE_EXPERTS].set(expert_b.reshape(-1))
    ints, flts, cnt = _router(h, w_r, b_r.reshape(LANES, 1), 512)
    tr = MOE_ROW_TILE
    n_tiles = (2 * n) // tr + MOE_EXPERTS
    pos, meta = _plan(ints, cnt, layer, n_tiles)
    pos1, pos2 = pos[0], pos[1]

    y = _experts(hp, pos1, pos2, meta[0, :n_tiles], meta[2, :n_tiles], meta[1, :1],
                 gate_w.reshape(-1, D_MODEL, MOE_D_FF),
                 up_w.reshape(-1, D_MODEL, MOE_D_FF),
                 down_w.reshape(-1, MOE_D_FF, D_MODEL))
    return _combine(y, h, pos1, pos2, flts,
                    ln_g.reshape(1, -1), ln_b.reshape(1, -1), COMBINE_TILE, tiled_copy)


def _pad_heads(w, axis):
    parts = []
    h0 = 0
    for _, _, nh in ATTN_GROUPS:
        sl = [slice(None)] * w.ndim
        sl[axis] = slice(h0 * ATTN_HEAD_DIM, (h0 + nh) * ATTN_HEAD_DIM)
        part = w[tuple(sl)]
        pad = [(0, 0)] * w.ndim
        pad[axis] = (0, ATTN_SLAB - nh * ATTN_HEAD_DIM)
        parts.append(jnp.pad(part, pad))
        h0 += nh
    return parts


def _in_proj_kernel(x_ref, w_ref, wdt_ref, z_ref, xbc_ref, dt_ref, *, z_tiles):
    j = pl.program_id(1)
    tn = z_ref.shape[1]
    acc = _dot(x_ref[...].astype(BF16), w_ref[:, pl.ds(pl.multiple_of(j * tn, tn), tn)])

    @pl.when(j < z_tiles)
    def _():
        z_ref[...] = acc

    @pl.when(j >= z_tiles)
    def _():
        xbc_ref[...] = acc

    @pl.when(j == 0)
    def _():
        dt_ref[...] = _dot_f32ish(x_ref[...], wdt_ref[...])


def _in_proj(x, w_zx, w_dt, tm, tn):
    m, k = x.shape
    z_tiles = SSM_D_INNER // tn
    n_tiles = (SSM_D_INNER + SSM_CONV_DIM) // tn
    return pl.pallas_call(
        functools.partial(_in_proj_kernel, z_tiles=z_tiles),
        grid=(m // tm, n_tiles),
        in_specs=[pl.BlockSpec((tm, k), lambda i, j: (i, 0)),
                  pl.BlockSpec(memory_space=pltpu.VMEM),
                  pl.BlockSpec((k, LANES), lambda i, j: (0, 0))],
        out_specs=[pl.BlockSpec((tm, tn), lambda i, j: (i, jnp.minimum(j, z_tiles - 1))),
                   pl.BlockSpec((tm, tn), lambda i, j: (i, jnp.maximum(j - z_tiles, 0))),
                   pl.BlockSpec((tm, LANES), lambda i, j: (i, 0))],
        out_shape=[jax.ShapeDtypeStruct((m, SSM_D_INNER), F32),
                   jax.ShapeDtypeStruct((m, SSM_CONV_DIM), F32),
                   jax.ShapeDtypeStruct((m, LANES), F32)],
        compiler_params=_cparams("parallel", "arbitrary"),
        name="ssm_in_proj",
    )(x, w_zx, w_dt)


def _ssd_layer(h, in_w, conv_w, conv_b, dt_bias, a_log, d_skip, norm_w, out_w, ln_g, ln_b, bsz, seq):
    split = SSM_D_INNER + SSM_CONV_DIM
    dt_w = jnp.pad(in_w[:, split:], ((0, 0), (0, LANES - SSM_HEADS)))
    z, xbc, dt_raw = _in_proj(h, in_w.astype(BF16), dt_w, 1024, 1024)
    pad32 = lambda v: jnp.pad(v, (0, LANES - SSM_HEADS)).reshape(1, LANES)
    y = _ssd(z, xbc, dt_raw, conv_w, conv_b.reshape(1, -1), pad32(dt_bias), pad32(a_log),
             jnp.repeat(d_skip, SSM_HEAD_DIM).reshape(1, -1), norm_w.reshape(1, -1), bsz, seq)
    return _matmul_res_ln(y, out_w.astype(BF16), h, ln_g.reshape(1, -1), ln_b.reshape(1, -1), 1024)


def _qkv_group_weights(kv_w, q_w):
    width = ATTN_HEADS * ATTN_HEAD_DIM
    w_q = _pad_heads(q_w * (ATTN_HEAD_DIM ** -0.5), 1)
    w_k = _pad_heads(kv_w[:, :width], 1)
    w_v = _pad_heads(kv_w[:, width:], 1)
    return [jnp.concatenate([w_q[gi], w_k[gi], w_v[gi]], axis=1).astype(BF16) for gi in range(len(ATTN_GROUPS))]


def _attn_layer(h, h8, qkv_w, o_w, rel_bias, ln_g, ln_b, bsz, seq):
    outs, lses = [], []
    h0 = 0
    for gi, (_, dil, nh) in enumerate(ATTN_GROUPS):
        if dil == 1:
            qkv = _matmul(h, qkv_w[gi], BF16, 1024, 3 * ATTN_SLAB)
        else:
            qkv = _qkv_dilated(h8, qkv_w[gi], dil, max(1024, ATTN_BLOCK * dil))
        bias_p, bias_c = _group_bias(rel_bias, h0, nh, dil)
        o, lse = _attention_group(qkv, bias_p, bias_c, gi, dil, nh, bsz, seq)
        outs.extend(o)
        lses.append(lse)
        h0 += nh
    w_o = jnp.concatenate(_pad_heads(o_w, 0), axis=0).astype(BF16)
    return _attn_out(outs, lses, w_o, h, ln_g.reshape(1, -1), ln_b.reshape(1, -1), 1024)


def kernel(x, ssm_in_w, ssm_conv_w, ssm_conv_b, ssm_dt_bias, ssm_a_log, ssm_d, ssm_norm_w, ssm_out_w,
           kv_w, attn_q_w, attn_o_w, rel_bias, moe_group_w, moe_group_b, moe_expert_w, moe_expert_b,
           moe_gate_w, moe_up_w, moe_down_w, ln_g, ln_b):
    bsz, seq, d = x.shape
    h = x.reshape(bsz * seq, d)
    n_ssd = DEPTH // 2
    h8 = None
    for i in range(DEPTH):
        if i < n_ssd:
            h, hp = _ssd_layer(h, ssm_in_w[i], ssm_conv_w[i], ssm_conv_b[i], ssm_dt_bias[i], ssm_a_log[i],
                               ssm_d[i], ssm_norm_w[i], ssm_out_w[i], ln_g[i, 0], ln_b[i, 0], bsz, seq)
        else:
            j = i - n_ssd
            h, hp = _attn_layer(h, h8, _qkv_group_weights(kv_w, attn_q_w[j]), attn_o_w[j], rel_bias,
                                ln_g[i, 0], ln_b[i, 0], bsz, seq)
        feeds_attention = n_ssd <= i + 1 < DEPTH
        res = _moe_layer(h, hp, i, moe_group_w[i], moe_group_b[i], moe_expert_w[i], moe_expert_b[i],
                         moe_gate_w, moe_up_w, moe_down_w, ln_g[i, 1], ln_b[i, 1], feeds_attention)
        h = res[0]
        h8 = res[1] if feeds_attention else None
    return h.reshape(bsz, seq, d)
```

```python
import functools
import math

import numpy as np
import jax
import jax.numpy as jnp
from jax import lax
from jax.experimental import pallas as pl
from jax.experimental.pallas import tpu as pltpu

F32 = jnp.float32
BF16 = jnp.bfloat16
I32 = jnp.int32

D_MODEL = 1024
DEPTH = 2
DN_ALPHA = (2 * DEPTH) ** 0.25
LN_EPS = 1e-5
LOG2_E = math.log2(math.e)

SSM_D_INNER = 2048
SSM_HEAD_DIM = 64
SSM_HEADS = 32
SSM_GROUPS = 4
SSM_STATE = 128
SSM_CONV = 4
SSM_CHUNK = 128
SSM_CONV_DIM = SSM_D_INNER + 2 * SSM_GROUPS * SSM_STATE

ATTN_HEAD_DIM = 64
ATTN_GROUPS = ((128, 1, 6), (512, 4, 5), (2048, 16, 5))
ATTN_HEADS = 16
ATTN_BLOCK = 128
N_BUCKETS = 32
MAX_DISTANCE = 2048
ATTN_SLAB = 384
NEG_BIG = -1e30
ATTN_ITEMS_PER_PASS = 4

MOE_GROUPS = 4
MOE_EPG = 8
MOE_EXPERTS = MOE_GROUPS * MOE_EPG
MOE_D_FF = 512
MOE_ROW_TILE = 256
COMBINE_TILE = 512

LANES = 128
DMA_QUEUES = 2
VMEM_LIMIT = 48 * 1024 * 1024
EXPERTS_VMEM_LIMIT = 56 * 1024 * 1024


def _cparams(*sem):
    return pltpu.CompilerParams(dimension_semantics=sem, vmem_limit_bytes=VMEM_LIMIT)


def _layer_norm(x, g, b):
    mu = jnp.mean(x, -1, keepdims=True)
    xc = x - mu
    var = jnp.mean(xc * xc, -1, keepdims=True)
    return xc * lax.rsqrt(var + LN_EPS) * g + b


def _split2(x):
    hi = x.astype(BF16)
    lo = (x - hi.astype(F32)).astype(BF16)
    return hi, lo


def _dot(a, b):
    return jnp.dot(a, b, preferred_element_type=F32)


def _dot_f32ish(a, b):
    ah, al = _split2(a)
    bh, bl = _split2(b)
    return _dot(ah, bh) + _dot(al, bh) + _dot(ah, bl)


def _mm_kernel(a_ref, b_ref, o_ref):
    o_ref[...] = _dot(a_ref[...].astype(BF16), b_ref[...]).astype(o_ref.dtype)


def _matmul(a, b, out_dtype, tm, tn):
    m, k = a.shape
    nc = b.shape[1]
    return pl.pallas_call(
        _mm_kernel,
        grid=(m // tm, nc // tn),
        in_specs=[pl.BlockSpec((tm, k), lambda i, j: (i, 0)),
                  pl.BlockSpec((k, tn), lambda i, j: (0, j))],
        out_specs=pl.BlockSpec((tm, tn), lambda i, j: (i, j)),
        out_shape=jax.ShapeDtypeStruct((m, nc), out_dtype),
        compiler_params=_cparams("parallel", "parallel"),
        name="matmul",
    )(a, b)


ROW_SPLIT = D_MODEL // LANES


def _store_row_tiled(ref, val):
    rows = val.shape[0]
    for c in range(ROW_SPLIT):
        ref[pl.ds(c, rows, stride=ROW_SPLIT), :] = val[:, c * LANES:(c + 1) * LANES]


def _load_row_tiled(ref):
    rows = ref.shape[0] // ROW_SPLIT
    return jnp.concatenate([ref[pl.ds(c, rows, stride=ROW_SPLIT), :] for c in range(ROW_SPLIT)], axis=1)


HALF = D_MODEL // 2
U32 = jnp.uint32
HI16 = 0xFFFF0000


def _pack_bf16_pairs(x):
    bits = lax.bitcast_convert_type(x.astype(BF16).astype(F32), U32)
    return (bits[:, :HALF] >> 16) | (bits[:, HALF:] & U32(HI16))


def _unpack_bf16_pairs(w):
    lo = lax.bitcast_convert_type(w << 16, F32).astype(BF16)
    hi = lax.bitcast_convert_type(w & U32(HI16), F32).astype(BF16)
    return lo, hi


def _mm_ln_kernel(a_ref, w_ref, h_ref, g_ref, b_ref, o_ref, op_ref):
    acc = _dot(a_ref[...], w_ref[...])
    out = _layer_norm(DN_ALPHA * h_ref[...] + acc, g_ref[...], b_ref[...])
    o_ref[...] = out
    op_ref[...] = _pack_bf16_pairs(out)


def _matmul_res_ln(a, w, h, g, b, tm):
    m, k = a.shape
    d = w.shape[1]
    return pl.pallas_call(
        _mm_ln_kernel,
        grid=(m // tm,),
        in_specs=[pl.BlockSpec((tm, k), lambda i: (i, 0)),
                  pl.BlockSpec((k, d), lambda i: (0, 0)),
                  pl.BlockSpec((tm, d), lambda i: (i, 0)),
                  pl.BlockSpec((1, d), lambda i: (0, 0)),
                  pl.BlockSpec((1, d), lambda i: (0, 0))],
        out_specs=[pl.BlockSpec((tm, d), lambda i: (i, 0)),
                   pl.BlockSpec((tm, HALF), lambda i: (i, 0))],
        out_shape=[jax.ShapeDtypeStruct((m, d), F32),
                   jax.ShapeDtypeStruct((m, HALF), U32)],
        compiler_params=_cparams("parallel"),
        name="matmul_res_ln",
    )(a, w, h, g, b)


SSD_CHUNKS_PER_STEP = 2


def _ssd_kernel(z_ref, xbc_ref, dt_ref, cw_ref, cb_ref, dtb_ref, alog_ref, dsk_ref, nw_ref,
                y_ref, xe_ref, st_ref):
    @pl.when(pl.program_id(1) == 0)
    def _():
        xe_ref[:, 0:8, :] = jnp.zeros((xe_ref.shape[0], 8, LANES), F32)
        st_ref[...] = jnp.zeros_like(st_ref)

    for sub in range(SSD_CHUNKS_PER_STEP):
        _ssd_chunk(pl.ds(sub * SSM_CHUNK, SSM_CHUNK), z_ref, xbc_ref, dt_ref, cw_ref, cb_ref, dtb_ref,
                   alog_ref, dsk_ref, nw_ref, y_ref, xe_ref, st_ref)


def _ssd_chunk(rows, z_ref, xbc_ref, dt_ref, cw_ref, cb_ref, dtb_ref, alog_ref, dsk_ref, nw_ref,
               y_ref, xe_ref, st_ref):
    q = SSM_CHUNK
    w = cw_ref[...]
    bias = cb_ref[...]
    act = []
    for c in range(SSM_CONV_DIM // LANES):
        cols = slice(c * LANES, (c + 1) * LANES)
        u = xbc_ref[rows, cols]
        xe_ref[c, 8:8 + q, :] = u
        conv = (bias[:, cols] + w[3:4, cols] * u + w[2:3, cols] * xe_ref[c, 7:7 + q, :]
                + w[1:2, cols] * xe_ref[c, 6:6 + q, :] + w[0:1, cols] * xe_ref[c, 5:5 + q, :])
        xe_ref[c, 0:8, :] = xe_ref[c, q:q + 8, :]
        act.append(conv * jax.nn.sigmoid(conv))

    pre = dt_ref[rows, :] + dtb_ref[...]
    dt = jnp.maximum(pre, 0.0) + jnp.log(1.0 + jnp.exp(-jnp.abs(pre)))
    adt = dt * (-jnp.exp(alog_ref[...]) * LOG2_E)

    row = lax.broadcasted_iota(I32, (q, q), 0)
    col = lax.broadcasted_iota(I32, (q, q), 1)
    tril = row >= col
    tri_b = jnp.where(tril, 1.0, 0.0).astype(BF16)
    a_hi = adt.astype(BF16)
    r1 = adt - a_hi.astype(F32)
    a_mid = r1.astype(BF16)
    a_lo = (r1 - a_mid.astype(F32)).astype(BF16)
    acs = _dot(tri_b, a_hi) + _dot(tri_b, a_mid) + _dot(tri_b, a_lo)
    acs_t = acs.T
    acs_dt_t = acs_t - jnp.log(dt.T) * LOG2_E
    eacs = jnp.exp2(acs)
    left = col < SSM_HEAD_DIM

    for g in range(SSM_GROUPS):
        b0 = SSM_D_INNER + g * SSM_STATE
        c0 = SSM_D_INNER + SSM_GROUPS * SSM_STATE + g * SSM_STATE
        bm = act[b0 // LANES]
        cm = act[c0 // LANES].astype(BF16)
        cb = lax.dot_general(cm, bm.astype(BF16), (((1,), (1,)), ((), ())),
                             preferred_element_type=F32)
        bm_t = bm.T
        gs = g * 512
        y_off = _dot(cm, st_ref[:, gs:gs + 512].astype(BF16))
        slabs = []
        for pr in range(4):
            ha = g * 8 + pr * 2
            hb = ha + 1
            cs = gs + pr * LANES
            x2 = act[cs // LANES]
            x2b = x2.astype(BF16)
            ys, ups = [], []
            for h in (ha, hb):
                a_col = acs[:, h:h + 1]
                a_src = acs_dt_t[h:h + 1, :]
                decay = jnp.where(tril, jnp.exp2(a_col - a_src), 0.0)
                ys.append(_dot((cb * decay).astype(BF16), x2b))
                to_end = jnp.exp2(acs_t[h:h + 1, q - 1:q] - a_src)
                ups.append(_dot((bm_t * to_end).astype(BF16), x2b))
            y_diag = jnp.where(left, ys[0], ys[1])
            upd = jnp.where(left, ups[0], ups[1])
            e2 = jnp.where(left, eacs[:, ha:ha + 1], eacs[:, hb:hb + 1])
            cd = jnp.where(left[0:1, :], eacs[q - 1:q, ha:ha + 1], eacs[q - 1:q, hb:hb + 1])
            y2 = y_diag + y_off[:, pr * LANES:(pr + 1) * LANES] * e2 + dsk_ref[:, cs:cs + LANES] * x2
            st_ref[:, cs:cs + LANES] = st_ref[:, cs:cs + LANES] * cd + upd
            slabs.append(y2)
        yg = jnp.concatenate(slabs, axis=1)
        zg = z_ref[rows, gs:gs + 512]
        yg = yg * (zg * jax.nn.sigmoid(zg))
        ms = jnp.mean(yg * yg, -1, keepdims=True)
        y_ref[rows, gs:gs + 512] = (yg * lax.rsqrt(ms + LN_EPS) * nw_ref[:, gs:gs + 512]).astype(y_ref.dtype)


def _ssd(z, xbc, dt_raw, conv_w, conv_b, dt_bias, a_log, d_rep, norm_w, bsz, seq):
    n = z.shape[0]
    q = SSM_CHUNK * SSD_CHUNKS_PER_STEP
    nchunk = seq // q
    tok = lambda b, c: (b * nchunk + c, 0)
    fixed = lambda b, c: (0, 0)
    return pl.pallas_call(
        _ssd_kernel,
        grid=(bsz, nchunk),
        in_specs=[pl.BlockSpec((q, SSM_D_INNER), tok),
                  pl.BlockSpec((q, SSM_CONV_DIM), tok),
                  pl.BlockSpec((q, LANES), tok),
                  pl.BlockSpec((SSM_CONV, SSM_CONV_DIM), fixed),
                  pl.BlockSpec((1, SSM_CONV_DIM), fixed),
                  pl.BlockSpec((1, LANES), fixed),
                  pl.BlockSpec((1, LANES), fixed),
                  pl.BlockSpec((1, SSM_D_INNER), fixed),
                  pl.BlockSpec((1, SSM_D_INNER), fixed)],
        out_specs=pl.BlockSpec((q, SSM_D_INNER), tok),
        out_shape=jax.ShapeDtypeStruct((n, SSM_D_INNER), BF16),
        scratch_shapes=[pltpu.VMEM((SSM_CONV_DIM // LANES, SSM_CHUNK + 8, LANES), F32),
                        pltpu.VMEM((SSM_STATE, SSM_D_INNER), F32)],
        compiler_params=_cparams("parallel", "arbitrary"),
        name="ssd_chunk",
    )(z, xbc, dt_raw, conv_w, conv_b, dt_bias, a_log, d_rep, norm_w)


def _residue_major_pieces(h8_ref, n_tokens, dil):
    span = ATTN_BLOCK * dil
    pieces = []
    for blk in range(n_tokens // span):
        for r in range(dil):
            first = (blk * span + r) * ROW_SPLIT
            pieces.append(jnp.concatenate(
                [h8_ref[pl.ds(first + c, ATTN_BLOCK, stride=ROW_SPLIT * dil), :] for c in range(ROW_SPLIT)],
                axis=1).astype(BF16))
    return pieces


def _qkv_dilated_kernel(h8_ref, w_ref, o_ref, *, dil):
    per_dot = 4
    rows = _residue_major_pieces(h8_ref, o_ref.shape[0], dil)
    for k in range(0, len(rows), per_dot):
        x = jnp.concatenate(rows[k:k + per_dot], axis=0)
        o_ref[k * ATTN_BLOCK:(k + per_dot) * ATTN_BLOCK, :] = _dot(x, w_ref[...]).astype(o_ref.dtype)


def _qkv_dilated(h8, w, dil, tm):
    n = h8.shape[0] // ROW_SPLIT
    nc = w.shape[1]
    return pl.pallas_call(
        functools.partial(_qkv_dilated_kernel, dil=dil),
        grid=(n // tm,),
        in_specs=[pl.BlockSpec((tm * ROW_SPLIT, LANES), lambda i: (i, 0)),
                  pl.BlockSpec((D_MODEL, nc), lambda i: (0, 0))],
        out_specs=pl.BlockSpec((tm, nc), lambda i: (i, 0)),
        out_shape=jax.ShapeDtypeStruct((n, nc), BF16),
        compiler_params=_cparams("parallel"),
        name=f"qkv_dil{dil}",
    )(h8, w)


def _attn_kernel(*refs, nh, dil, has_prev, blocks, per_pass):
    span = ATTN_BLOCK * dil
    if has_prev:
        q_ref, kp_ref, kc_ref, vp_ref, vc_ref, bp_ref, bc_ref = refs[:7]
        out_refs = refs[7:]
        first_pen = jnp.where(pl.program_id(1) > 0, 0.0, NEG_BIG)
    else:
        q_ref, kc_ref, vc_ref, bc_ref = refs[:4]
        out_refs = refs[4:]
    s_scr, p_scr, max_scr = out_refs[-3:]
    o_refs, lse_ref = out_refs[:-4], out_refs[-4]
    ones = jnp.ones((s_scr.shape[2], LANES), BF16)
    lane = lax.broadcasted_iota(I32, (ATTN_BLOCK, LANES), 1)
    left = lane < ATTN_HEAD_DIM
    zero = jnp.zeros((), BF16)
    nt = (((1,), (1,)), ((), ()))

    def block_start(row):
        return row if isinstance(row, int) else pl.multiple_of(row, ATTN_BLOCK)

    def work(items):
        info = []
        for slot, (blk, r) in enumerate(items):
            rows = pl.ds(block_start(blk * span + r * ATTN_BLOCK), ATTN_BLOCK)
            dst = pl.ds(blk * span + r, ATTN_BLOCK, stride=dil)
            kprev = vprev = prows = pen = None
            if has_prev and blk == 0:
                kprev, vprev, pen = kp_ref, vp_ref, first_pen
                prows = pl.ds(block_start(r * ATTN_BLOCK), ATTN_BLOCK)
            elif has_prev:
                kprev, vprev, pen = kc_ref, vc_ref, 0.0
                prows = pl.ds(block_start((blk - 1) * span + r * ATTN_BLOCK), ATTN_BLOCK)
            info.append((slot * nh, rows, dst, kprev, vprev, prows, pen))
        for base, rows, dst, kprev, vprev, prows, pen in info:
            for hh in range(nh):
                cols = pl.ds(hh // 2 * LANES, LANES)
                qm = jnp.where(left if hh % 2 == 0 else ~left, q_ref[rows, cols], zero)
                s_c = lax.dot_general(qm, kc_ref[rows, cols], nt, preferred_element_type=F32) + bc_ref[hh]
                if has_prev:
                    s_p = lax.dot_general(qm, kprev[prows, cols], nt, preferred_element_type=F32)
                    s_scr[base + hh, :, 0:ATTN_BLOCK] = s_p + (bp_ref[hh] + pen)
                    s_scr[base + hh, :, ATTN_BLOCK:] = s_c
                else:
                    s_scr[base + hh] = s_c
        for base, *_ in info:
            for hh in range(nh):
                s = s_scr[base + hh]
                m = jnp.max(s, -1, keepdims=True)
                p = jnp.exp(s - m)
                p_scr[base + hh] = p.astype(BF16)
                max_scr[base + hh] = m
        for base, rows, dst, kprev, vprev, prows, pen in info:
            lse_sum = jnp.zeros((ATTN_BLOCK, 1), F32)
            for pr in range(ATTN_SLAB // LANES):
                cols = pl.ds(pr * LANES, LANES)
                halves = []
                for hh in (pr * 2, pr * 2 + 1):
                    if hh >= nh:
                        halves.append(jnp.zeros((ATTN_BLOCK, LANES), F32))
                        continue
                    den = _dot(p_scr[base + hh], ones)
                    den1 = den[:, 0:1]
                    if has_prev:
                        o = (_dot(p_scr[base + hh, :, 0:ATTN_BLOCK], vprev[prows, cols])
                             + _dot(p_scr[base + hh, :, ATTN_BLOCK:], vc_ref[rows, cols]))
                    else:
                        o = _dot(p_scr[base + hh], vc_ref[rows, cols])
                    halves.append(o * (1.0 / den))
                    lse_sum = lse_sum + (max_scr[base + hh] + jnp.log(den1))
                o_refs[pr][dst, :] = jnp.where(left, halves[0], halves[1])
            lse_ref[dst, :] = jnp.broadcast_to(lse_sum * (1.0 / nh), (ATTN_BLOCK, LANES))

    if dil == 1:
        for b0 in range(0, blocks, per_pass):
            work([(b, 0) for b in range(b0, min(blocks, b0 + per_pass))])
    else:
        for blk in range(blocks):
            def body(it, carry, blk=blk):
                work([(blk, it * per_pass + k) for k in range(per_pass)])
                return carry

            lax.fori_loop(0, dil // per_pass, body, 0)


def _attention_group(qkv, bias_p, bias_c, gi, dil, nh, bsz, seq):
    span = ATTN_BLOCK * dil
    nb = seq // span
    has_prev = nb > 1
    blocks = max(1, 512 // span)
    steps = nb // blocks
    blk = (blocks * span, ATTN_SLAB)
    cur = lambda which: (lambda b, n: (b * steps + n, which))
    prev = lambda which: (lambda b, n: (b * nb + jnp.maximum(n * blocks - 1, 0), which))
    fixed = lambda b, n: (0, 0, 0)
    tab = pl.BlockSpec((nh, ATTN_BLOCK, ATTN_BLOCK), fixed)
    if has_prev:
        pblk = (span, ATTN_SLAB)
        in_specs = [pl.BlockSpec(blk, cur(0)), pl.BlockSpec(pblk, prev(1)), pl.BlockSpec(blk, cur(1)),
                    pl.BlockSpec(pblk, prev(2)), pl.BlockSpec(blk, cur(2)), tab, tab]
        args = (qkv, qkv, qkv, qkv, qkv, bias_p, bias_c)
    else:
        in_specs = [pl.BlockSpec(blk, cur(0)), pl.BlockSpec(blk, cur(1)), pl.BlockSpec(blk, cur(2)), tab]
        args = (qkv, qkv, qkv, bias_c)
    n_out = ATTN_SLAB // LANES + 1
    keys = 2 * ATTN_BLOCK if has_prev else ATTN_BLOCK
    slots = min(dil * blocks, ATTN_ITEMS_PER_PASS)
    outs = pl.pallas_call(
        functools.partial(_attn_kernel, nh=nh, dil=dil, has_prev=has_prev, blocks=blocks, per_pass=slots),
        grid=(bsz, steps),
        in_specs=in_specs,
        out_specs=[pl.BlockSpec((blocks * span, LANES), lambda b, n: (b * steps + n, 0))] * n_out,
        out_shape=[jax.ShapeDtypeStruct((bsz * seq, LANES), F32)] * n_out,
        scratch_shapes=[pltpu.VMEM((slots * nh, ATTN_BLOCK, keys), F32),
                        pltpu.VMEM((slots * nh, ATTN_BLOCK, keys), BF16),
                        pltpu.VMEM((slots * nh, ATTN_BLOCK, 1), F32)],
        compiler_params=_cparams("parallel", "parallel"),
        name=f"dilated_attn_g{gi}",
    )(*args)
    return outs[:-1], outs[-1]


def _attn_out_kernel(*refs):
    n_pairs = ATTN_SLAB // LANES
    n_groups = len(ATTN_GROUPS)
    o_refs = refs[:n_groups * n_pairs]
    l_refs = refs[n_groups * n_pairs:n_groups * (n_pairs + 1)]
    w_ref, h_ref, g_ref, b_ref, out_ref, outp_ref = refs[n_groups * (n_pairs + 1):]
    ls = [r[:, 0:1] for r in l_refs]
    m = jnp.maximum(jnp.maximum(ls[0], ls[1]), ls[2])
    es = [jnp.exp(v - m) for v in ls]
    scale = n_groups / (es[0] + es[1] + es[2])
    slabs = []
    for gi in range(n_groups):
        wt = es[gi] * scale
        slabs.extend((r[...] * wt).astype(BF16) for r in o_refs[gi * n_pairs:(gi + 1) * n_pairs])
    acc = _dot(jnp.concatenate(slabs, axis=1), w_ref[...])
    out = _layer_norm(DN_ALPHA * h_ref[...] + acc, g_ref[...], b_ref[...])
    out_ref[...] = out
    outp_ref[...] = _pack_bf16_pairs(out)


def _attn_out(outs, lses, w, h, g, b, tm):
    n = h.shape[0]
    tok = lambda i: (i, 0)
    fixed2 = lambda i: (0, 0)
    return pl.pallas_call(
        _attn_out_kernel,
        grid=(n // tm,),
        in_specs=[pl.BlockSpec((tm, LANES), tok)] * (len(outs) + len(lses))
                 + [pl.BlockSpec((len(ATTN_GROUPS) * ATTN_SLAB, D_MODEL), lambda i: (0, 0)),
                    pl.BlockSpec((tm, D_MODEL), tok),
                    pl.BlockSpec((1, D_MODEL), fixed2), pl.BlockSpec((1, D_MODEL), fixed2)],
        out_specs=[pl.BlockSpec((tm, D_MODEL), tok), pl.BlockSpec((tm, HALF), tok)],
        out_shape=[jax.ShapeDtypeStruct((n, D_MODEL), F32),
                   jax.ShapeDtypeStruct((n, HALF), U32)],
        compiler_params=_cparams("parallel"),
        name="attn_out_ln",
    )(*outs, *lses, w, h, g, b)


def _t5_bucket(dist):
    max_exact = N_BUCKETS // 2
    n = np.maximum(dist, 1).astype(np.float64)
    large = max_exact + (np.log(n / max_exact) / np.log(MAX_DISTANCE / max_exact)
                         * (N_BUCKETS - max_exact)).astype(np.int32)
    large = np.minimum(large, N_BUCKETS - 1)
    return np.where(dist < max_exact, dist, large).astype(np.int32)


def _group_bias(rel_bias, h0, nh, dil):
    qi = np.arange(ATTN_BLOCK)[:, None]
    ki = np.arange(ATTN_BLOCK)[None, :]
    tabs = []
    for delta, band in ((qi + ATTN_BLOCK - ki, ki >= qi), (qi - ki, ki <= qi)):
        bucket = _t5_bucket(np.clip(delta, 0, None) * dil)
        onehot = (bucket[..., None] == np.arange(N_BUCKETS)).astype(np.float32)
        t = jnp.einsum("qkb,bh->hqk", onehot, rel_bias[:, h0:h0 + nh], precision=lax.Precision.HIGHEST)
        tabs.append(jnp.where(band[None], t, NEG_BIG).astype(F32))
    return tabs


def _router_kernel(h_ref, w_ref, b_ref, ints_ref, flts_ref, cnt_ref, carry_ref, before_ref, *, tm):
    @pl.when(pl.program_id(0) == 0)
    def _():
        carry_ref[...] = jnp.zeros_like(carry_ref)
        tr = lax.broadcasted_iota(I32, (tm, tm), 0)
        tc = lax.broadcasted_iota(I32, (tm, tm), 1)
        before_ref[...] = jnp.where(tr < tc, 1.0, 0.0).astype(BF16)

    lt = _dot_f32ish(h_ref[...], w_ref[...]).T + b_ref[...]
    gl = lt[0:MOE_GROUPS]
    r4 = lax.broadcasted_iota(I32, (MOE_GROUPS, tm), 0)
    gmax = jnp.max(gl, 0, keepdims=True)
    gidx = jnp.min(jnp.where(gl == gmax, r4, MOE_GROUPS), 0, keepdims=True)
    gval = 1.0 / jnp.sum(jnp.exp(gl - gmax), 0, keepdims=True)

    esel = jnp.zeros((MOE_EPG, tm), F32)
    for g in range(MOE_GROUPS):
        esel = jnp.where(gidx == g, lt[8 + g * MOE_EPG:8 + (g + 1) * MOE_EPG], esel)
    r8 = lax.broadcasted_iota(I32, (MOE_EPG, tm), 0)
    v1 = jnp.max(esel, 0, keepdims=True)
    i1 = jnp.min(jnp.where(esel == v1, r8, MOE_EPG), 0, keepdims=True)
    rest = jnp.where(r8 == i1, -jnp.inf, esel)
    v2 = jnp.max(rest, 0, keepdims=True)
    i2 = jnp.min(jnp.where(rest == v2, r8, MOE_EPG), 0, keepdims=True)
    t = jnp.exp(v2 - v1)
    p1 = gval / (1.0 + t)
    p2 = p1 * t
    e1 = gidx * MOE_EPG + i1
    e2 = gidx * MOE_EPG + i2

    r32 = lax.broadcasted_iota(I32, (MOE_EXPERTS, tm), 0)
    oh1 = r32 == e1
    oh2 = r32 == e2
    oh = jnp.where(oh1 | oh2, 1.0, 0.0)
    base = _dot(oh.astype(BF16), before_ref[...]) + carry_ref[:, 0:1]
    rank1 = jnp.sum(jnp.where(oh1, base, 0.0), 0, keepdims=True).astype(I32)
    rank2 = jnp.sum(jnp.where(oh2, base, 0.0), 0, keepdims=True).astype(I32)
    carry_ref[...] = carry_ref[...] + jnp.sum(oh, 1, keepdims=True)
    cnt_ref[...] = carry_ref[...]

    ints_ref[...] = jnp.where(r8 == 0, e1, jnp.where(r8 == 1, e2, jnp.where(r8 == 2, rank1,
                              jnp.where(r8 == 3, rank2, 0))))
    r128 = lax.broadcasted_iota(I32, (LANES, tm), 0)
    flts_ref[...] = jnp.where(r128 == 0, p1, jnp.where(r128 == 1, p2, 0.0)).T


def _router(h, w_r, b_r, tm):
    n = h.shape[0]
    return pl.pallas_call(
        functools.partial(_router_kernel, tm=tm),
        grid=(n // tm,),
        in_specs=[pl.BlockSpec((tm, D_MODEL), lambda i: (i, 0)),
                  pl.BlockSpec((D_MODEL, LANES), lambda i: (0, 0)),
                  pl.BlockSpec((LANES, 1), lambda i: (0, 0))],
        out_specs=[pl.BlockSpec((8, tm), lambda i: (0, i)),
                   pl.BlockSpec((tm, LANES), lambda i: (i, 0)),
                   pl.BlockSpec((MOE_EXPERTS, LANES), lambda i: (0, 0))],
        out_shape=[jax.ShapeDtypeStruct((8, n), I32),
                   jax.ShapeDtypeStruct((n, LANES), F32),
                   jax.ShapeDtypeStruct((MOE_EXPERTS, LANES), F32)],
        scratch_shapes=[pltpu.VMEM((MOE_EXPERTS, LANES), F32), pltpu.VMEM((tm, tm), BF16)],
        compiler_params=_cparams("arbitrary"),
        name="moe_router",
    )(h, w_r, b_r)


def _row_copy(src_ref, s, dst_ref, d, sem):
    def first(r):
        return r * ROW_SPLIT if isinstance(r, int) else pl.multiple_of(r * ROW_SPLIT, ROW_SPLIT)

    return pltpu.make_async_copy(src_ref.at[pl.ds(first(s), ROW_SPLIT), :],
                                 dst_ref.at[pl.ds(first(d), ROW_SPLIT), :], sem)


def _tile_wait(src_ref, dst_ref, sem):
    pltpu.make_async_copy(src_ref.at[pl.ds(0, dst_ref.shape[0]), :], dst_ref, sem).wait()


def _experts_kernel(pos1_ref, pos2_ref, te_ref, nxt_ref, na_ref, hp_hbm, zeros_ref, wg_ref, wu_ref, wd_ref,
                    y_ref, hp_ref, xa, xb, wg_f, wu_f, wd_f, wg_s, wu_s, wd_s, row_tok, slot_ref,
                    sem, hsem, wsem):
    i = pl.program_id(0)
    n_tiles = pl.num_programs(0)
    tr = MOE_ROW_TILE
    active = i < na_ref[0]
    changed = jnp.logical_or(i == 0, te_ref[i] != te_ref[jnp.maximum(i - 1, 0)])

    def weight_copies(expert, slot):
        return [pltpu.make_async_copy(src.at[expert], dst.at[slot], wsem.at[slot])
                for src, dst in ((wg_ref, wg_f), (wu_ref, wu_f), (wd_ref, wd_f))]

    @pl.when(i == 0)
    def _():
        slot_ref[0] = 0
        rows_in = pltpu.make_async_copy(hp_hbm, hp_ref, hsem)
        rows_in.start()
        for c in weight_copies(te_ref[0], 0):
            c.start()
        clear = pltpu.make_async_copy(zeros_ref, row_tok, sem)
        clear.start()
        clear.wait()

        def invert(t, carry):
            row_tok[pos1_ref[t]] = t
            row_tok[pos2_ref[t]] = t
            return carry

        lax.fori_loop(0, pos1_ref.shape[0], invert, 0, unroll=16)
        rows_in.wait()

        def pick(r, carry):
            xa[pl.ds(r, 1), :] = hp_ref[pl.ds(row_tok[r], 1), :]
            return carry

        lax.fori_loop(0, tr, pick, 0, unroll=8)

    @pl.when(jnp.logical_and(active, changed))
    def _():
        slot = slot_ref[0]
        for c in weight_copies(te_ref[i], slot):
            c.wait()
        following = nxt_ref[i]

        @pl.when(following >= 0)
        def _():
            for c in weight_copies(following, 1 - slot):
                c.start()

        wg_s[...] = wg_f[slot].astype(BF16)
        wu_s[...] = wu_f[slot].astype(BF16)
        wd_s[...] = wd_f[slot].astype(BF16)
        slot_ref[0] = 1 - slot

    def step(cur, nxt):
        base = jnp.minimum(i + 1, n_tiles - 1) * tr
        for r in range(tr):
            nxt[r:r + 1, :] = hp_ref[pl.ds(row_tok[base + r], 1), :]
        lo, hi = _unpack_bf16_pairs(cur[...])
        gate = _dot(lo, wg_s[0:HALF, :]) + _dot(hi, wg_s[HALF:, :])
        up = _dot(lo, wu_s[0:HALF, :]) + _dot(hi, wu_s[HALF:, :])
        hid = (gate * jax.nn.sigmoid(gate) * up).astype(BF16)
        _store_row_tiled(y_ref, _dot(hid, wd_s[...]))

    even = i % 2 == 0

    @pl.when(jnp.logical_and(active, even))
    def _():
        step(xa, xb)

    @pl.when(jnp.logical_and(active, jnp.logical_not(even)))
    def _():
        step(xb, xa)

    @pl.when(jnp.logical_not(active))
    def _():
        y_ref[...] = jnp.zeros_like(y_ref)


def _experts(hp, pos1, pos2, tile_expert, next_expert, n_active, wg, wu, wd):
    tr = MOE_ROW_TILE
    n_tiles = tile_expert.shape[0]
    rows = n_tiles * tr
    hbm = pl.BlockSpec(memory_space=pl.ANY)
    return pl.pallas_call(
        _experts_kernel,
        grid_spec=pltpu.PrefetchScalarGridSpec(
            num_scalar_prefetch=5,
            grid=(n_tiles,),
            in_specs=[hbm, hbm, hbm, hbm, hbm],
            out_specs=pl.BlockSpec((tr * ROW_SPLIT, LANES), lambda i, *_: (i, 0)),
            scratch_shapes=[pltpu.VMEM(hp.shape, U32),
                            pltpu.VMEM((tr, HALF), U32),
                            pltpu.VMEM((tr, HALF), U32),
                            pltpu.VMEM((2, D_MODEL, MOE_D_FF), F32),
                            pltpu.VMEM((2, D_MODEL, MOE_D_FF), F32),
                            pltpu.VMEM((2, MOE_D_FF, D_MODEL), F32),
                            pltpu.VMEM((D_MODEL, MOE_D_FF), BF16),
                            pltpu.VMEM((D_MODEL, MOE_D_FF), BF16),
                            pltpu.VMEM((MOE_D_FF, D_MODEL), BF16),
                            pltpu.SMEM((rows,), I32),
                            pltpu.SMEM((1,), I32),
                            pltpu.SemaphoreType.DMA(()),
                            pltpu.SemaphoreType.DMA(()),
                            pltpu.SemaphoreType.DMA((2,))]),
        out_shape=jax.ShapeDtypeStruct((rows * ROW_SPLIT, LANES), F32),
        compiler_params=pltpu.CompilerParams(dimension_semantics=("arbitrary",),
                                             vmem_limit_bytes=EXPERTS_VMEM_LIMIT),
        name="moe_experts",
    )(pos1, pos2, tile_expert, next_expert, n_active, hp, jnp.zeros((rows,), I32), wg, wu, wd)


def _combine_kernel(pos1_ref, pos2_ref, y_ref, h_ref, p_ref, g_ref, b_ref, *rest, tm, tiled_copy):
    out_ref = rest[0]
    a1, a2, b1, b2, sem = rest[-5:]
    i = pl.program_id(0)
    last = pl.num_programs(0) - 1

    @pl.when(i == 0)
    def _():
        def issue(t, carry):
            _row_copy(y_ref, pos1_ref[t], a1, t, sem.at[0]).start()
            _row_copy(y_ref, pos2_ref[t], a2, t, sem.at[0]).start()
            return carry

        lax.fori_loop(0, tm, issue, 0)

    def step(c1, c2, cur_sem, n1, n2, nxt_sem):
        _tile_wait(y_ref, c1, cur_sem)
        _tile_wait(y_ref, c2, cur_sem)
        base = jnp.minimum(i + 1, last) * tm
        for t in range(tm):
            _row_copy(y_ref, pos1_ref[base + t], n1, t, nxt_sem).start(priority=0)
            _row_copy(y_ref, pos2_ref[base + t], n2, t, nxt_sem).start(priority=DMA_QUEUES - 1)
        ffn = p_ref[:, 0:1] * _load_row_tiled(c1) + p_ref[:, 1:2] * _load_row_tiled(c2)
        out = _layer_norm(DN_ALPHA * h_ref[...] + ffn, g_ref[...], b_ref[...])
        out_ref[...] = out
        if tiled_copy:
            _store_row_tiled(rest[1], out)

        @pl.when(i == last)
        def _():
            _tile_wait(y_ref, n1, nxt_sem)
            _tile_wait(y_ref, n2, nxt_sem)

    @pl.when(i % 2 == 0)
    def _():
        step(a1, a2, sem.at[0], b1, b2, sem.at[1])

    @pl.when(i % 2 == 1)
    def _():
        step(b1, b2, sem.at[1], a1, a2, sem.at[0])


def _combine(y, h, pos1, pos2, gates, g, b, tm, tiled_copy):
    n = h.shape[0]
    tok = lambda i, a, c: (i, 0)
    fixed = lambda i, a, c: (0, 0)
    out_specs = [pl.BlockSpec((tm, D_MODEL), tok)]
    out_shape = [jax.ShapeDtypeStruct((n, D_MODEL), F32)]
    if tiled_copy:
        out_specs.append(pl.BlockSpec((tm * ROW_SPLIT, LANES), tok))
        out_shape.append(jax.ShapeDtypeStruct((n * ROW_SPLIT, LANES), F32))
    return pl.pallas_call(
        functools.partial(_combine_kernel, tm=tm, tiled_copy=tiled_copy),
        grid_spec=pltpu.PrefetchScalarGridSpec(
            num_scalar_prefetch=2,
            grid=(n // tm,),
            in_specs=[pl.BlockSpec(memory_space=pl.ANY),
                      pl.BlockSpec((tm, D_MODEL), tok),
                      pl.BlockSpec((tm, LANES), tok),
                      pl.BlockSpec((1, D_MODEL), fixed), pl.BlockSpec((1, D_MODEL), fixed)],
            out_specs=out_specs,
            scratch_shapes=[pltpu.VMEM((tm * ROW_SPLIT, LANES), F32)] * 4 + [pltpu.SemaphoreType.DMA((2,))]),
        out_shape=out_shape,
        compiler_params=_cparams("arbitrary"),
        name="moe_combine_ln",
    )(pos1, pos2, y, h, gates, g, b)


def _plan_kernel(ints_ref, cnt_ref, pos_ref, meta_ref, *, layer, chunk):
    tr = MOE_ROW_TILE
    n = ints_ref.shape[1]
    ne = MOE_EXPERTS
    tiles = (cnt_ref[...] + (tr - 1.0)) * (1.0 / tr)
    tiles = tiles.astype(I32).astype(F32)
    lower = lax.broadcasted_iota(I32, (ne, ne), 0) >= lax.broadcasted_iota(I32, (ne, ne), 1)
    ends = _dot(jnp.where(lower, 1.0, 0.0).astype(BF16), tiles.astype(BF16))
    start_col = ((ends - tiles) * tr).astype(I32)[:, 0:1]
    r8 = lax.broadcasted_iota(I32, (8, chunk), 0)
    re = lax.broadcasted_iota(I32, (ne, chunk), 0)
    for c in range(n // chunk):
        blk = ints_ref[:, c * chunk:(c + 1) * chunk]
        s1 = jnp.sum(jnp.where(re == blk[0:1], start_col, 0), 0, keepdims=True)
        s2 = jnp.sum(jnp.where(re == blk[1:2], start_col, 0), 0, keepdims=True)
        pos_ref[:, c * chunk:(c + 1) * chunk] = jnp.where(
            r8 == 0, s1 + blk[2:3], jnp.where(r8 == 1, s2 + blk[3:4], 0))
    width = meta_ref.shape[1]
    tile_id = lax.broadcasted_iota(I32, (ne, width), 1).astype(F32)
    te = jnp.sum(jnp.where(ends[:, 0:1] <= tile_id, 1, 0), 0, keepdims=True)
    te = jnp.minimum(te, ne - 1)
    expert = lax.broadcasted_iota(I32, (ne, width), 0)
    later = jnp.logical_and(expert > te, tiles[:, 0:1] > 0.0)
    nxt = jnp.min(jnp.where(later, expert, ne), 0, keepdims=True)
    nxt = jnp.where(nxt < ne, nxt + layer * ne, -1)
    n_used = ends[ne - 1:ne, 0:1].astype(I32)
    rm = lax.broadcasted_iota(I32, (8, width), 0)
    meta_ref[...] = jnp.where(rm == 0, te + layer * ne,
                              jnp.where(rm == 1, n_used, jnp.where(rm == 2, nxt, 0)))


def _plan(ints, cnt, layer, n_tiles):
    n = ints.shape[1]
    width = -(-n_tiles // LANES) * LANES
    return pl.pallas_call(
        functools.partial(_plan_kernel, layer=layer, chunk=2048),
        out_shape=[jax.ShapeDtypeStruct((8, n), I32), jax.ShapeDtypeStruct((8, width), I32)],
        compiler_params=pltpu.CompilerParams(vmem_limit_bytes=VMEM_LIMIT),
        name="moe_plan",
    )(ints, cnt)


def _moe_layer(h, hp, layer, group_w, group_b, expert_w, expert_b, gate_w, up_w, down_w, ln_g, ln_b,
               tiled_copy):
    n = h.shape[0]
    ew = jnp.transpose(expert_w, (1, 0, 2)).reshape(D_MODEL, MOE_EXPERTS)
    w_r = jnp.zeros((D_MODEL, LANES), F32).at[:, 0:MOE_GROUPS].set(group_w).at[:, 8:8 + MOE_EXPERTS].set(ew)
    b_r = jnp.zeros((LANES,), F32).at[0:MOE_GROUPS].set(group_b).at[8:8 + MOE_EXPERTS].set(expert_b.reshape(-1))
    ints, flts, cnt = _router(h, w_r, b_r.reshape(LANES, 1), 512)
    tr = MOE_ROW_TILE
    n_tiles = (2 * n) // tr + MOE_EXPERTS
    pos, meta = _plan(ints, cnt, layer, n_tiles)
    pos1, pos2 = pos[0], pos[1]

    y = _experts(hp, pos1, pos2, meta[0, :n_tiles], meta[2, :n_tiles], meta[1, :1],
                 gate_w.reshape(-1, D_MODEL, MOE_D_FF),
                 up_w.reshape(-1, D_MODEL, MOE_D_FF),
                 down_w.reshape(-1, MOE_D_FF, D_MODEL))
    return _combine(y, h, pos1, pos2, flts,
                    ln_g.reshape(1, -1), ln_b.reshape(1, -1), COMBINE_TILE, tiled_copy)


def _pad_heads(w, axis):
    parts = []
    h0 = 0
    for _, _, nh in ATTN_GROUPS:
        sl = [slice(None)] * w.ndim
        sl[axis] = slice(h0 * ATTN_HEAD_DIM, (h0 + nh) * ATTN_HEAD_DIM)
        part = w[tuple(sl)]
        pad = [(0, 0)] * w.ndim
        pad[axis] = (0, ATTN_SLAB - nh * ATTN_HEAD_DIM)
        parts.append(jnp.pad(part, pad))
        h0 += nh
    return parts


def _in_proj_kernel(x_ref, w_ref, wdt_ref, z_ref, xbc_ref, dt_ref, *, z_tiles):
    j = pl.program_id(1)
    tn = z_ref.shape[1]
    acc = _dot(x_ref[...].astype(BF16), w_ref[:, pl.ds(pl.multiple_of(j * tn, tn), tn)])

    @pl.when(j < z_tiles)
    def _():
        z_ref[...] = acc

    @pl.when(j >= z_tiles)
    def _():
        xbc_ref[...] = acc

    @pl.when(j == 0)
    def _():
        dt_ref[...] = _dot_f32ish(x_ref[...], wdt_ref[...])


def _in_proj(x, w_zx, w_dt, tm, tn):
    m, k = x.shape
    z_tiles = SSM_D_INNER // tn
    n_tiles = (SSM_D_INNER + SSM_CONV_DIM) // tn
    return pl.pallas_call(
        functools.partial(_in_proj_kernel, z_tiles=z_tiles),
        grid=(m // tm, n_tiles),
        in_specs=[pl.BlockSpec((tm, k), lambda i, j: (i, 0)),
                  pl.BlockSpec(memory_space=pltpu.VMEM),
                  pl.BlockSpec((k, LANES), lambda i, j: (0, 0))],
        out_specs=[pl.BlockSpec((tm, tn), lambda i, j: (i, jnp.minimum(j, z_tiles - 1))),
                   pl.BlockSpec((tm, tn), lambda i, j: (i, jnp.maximum(j - z_tiles, 0))),
                   pl.BlockSpec((tm, LANES), lambda i, j: (i, 0))],
        out_shape=[jax.ShapeDtypeStruct((m, SSM_D_INNER), F32),
                   jax.ShapeDtypeStruct((m, SSM_CONV_DIM), F32),
                   jax.ShapeDtypeStruct((m, LANES), F32)],
        compiler_params=_cparams("parallel", "arbitrary"),
        name="ssm_in_proj",
    )(x, w_zx, w_dt)


def _ssd_layer(h, in_w, conv_w, conv_b, dt_bias, a_log, d_skip, norm_w, out_w, ln_g, ln_b, bsz, seq):
    split = SSM_D_INNER + SSM_CONV_DIM
    dt_w = jnp.pad(in_w[:, split:], ((0, 0), (0, LANES - SSM_HEADS)))
    z, xbc, dt_raw = _in_proj(h, in_w.astype(BF16), dt_w, 1024, 1024)
    pad32 = lambda v: jnp.pad(v, (0, LANES - SSM_HEADS)).reshape(1, LANES)
    y = _ssd(z, xbc, dt_raw, conv_w, conv_b.reshape(1, -1), pad32(dt_bias), pad32(a_log),
             jnp.repeat(d_skip, SSM_HEAD_DIM).reshape(1, -1), norm_w.reshape(1, -1), bsz, seq)
    return _matmul_res_ln(y, out_w.astype(BF16), h, ln_g.reshape(1, -1), ln_b.reshape(1, -1), 1024)


def _qkv_group_weights(kv_w, q_w):
    width = ATTN_HEADS * ATTN_HEAD_DIM
    w_q = _pad_heads(q_w * (ATTN_HEAD_DIM ** -0.5), 1)
    w_k = _pad_heads(kv_w[:, :width], 1)
    w_v = _pad_heads(kv_w[:, width:], 1)
    return [jnp.concatenate([w_q[gi], w_k[gi], w_v[gi]], axis=1).astype(BF16) for gi in range(len(ATTN_GROUPS))]


def _attn_layer(h, h8, qkv_w, o_w, rel_bias, ln_g, ln_b, bsz, seq):
    outs, lses = [], []
    h0 = 0
    for gi, (_, dil, nh) in enumerate(ATTN_GROUPS):
        if dil == 1:
            qkv = _matmul(h, qkv_w[gi], BF16, 1024, 3 * ATTN_SLAB)
        else:
            qkv = _qkv_dilated(h8, qkv_w[gi], dil, max(1024, ATTN_BLOCK * dil))
        bias_p, bias_c = _group_bias(rel_bias, h0, nh, dil)
        o, lse = _attention_group(qkv, bias_p, bias_c, gi, dil, nh, bsz, seq)
        outs.extend(o)
        lses.append(lse)
        h0 += nh
    w_o = jnp.concatenate(_pad_heads(o_w, 0), axis=0).astype(BF16)
    return _attn_out(outs, lses, w_o, h, ln_g.reshape(1, -1), ln_b.reshape(1, -1), 1024)


def kernel(x, ssm_in_w, ssm_conv_w, ssm_conv_b, ssm_dt_bias, ssm_a_log, ssm_d, ssm_norm_w, ssm_out_w,
           kv_w, attn_q_w, attn_o_w, rel_bias, moe_group_w, moe_group_b, moe_expert_w, moe_expert_b,
           moe_gate_w, moe_up_w, moe_down_w, ln_g, ln_b):
    bsz, seq, d = x.shape
    h = x.reshape(bsz * seq, d)
    n_ssd = DEPTH // 2
    h8 = None
    for i in range(DEPTH):
        if i < n_ssd:
            h, hp = _ssd_layer(h, ssm_in_w[i], ssm_conv_w[i], ssm_conv_b[i], ssm_dt_bias[i], ssm_a_log[i],
                               ssm_d[i], ssm_norm_w[i], ssm_out_w[i], ln_g[i, 0], ln_b[i, 0], bsz, seq)
        else:
            j = i - n_ssd
            h, hp = _attn_layer(h, h8, _qkv_group_weights(kv_w, attn_q_w[j]), attn_o_w[j], rel_bias,
                                ln_g[i, 0], ln_b[i, 0], bsz, seq)
        feeds_attention = n_ssd <= i + 1 < DEPTH
        res = _moe_layer(h, hp, i, moe_group_w[i], moe_group_b[i], moe_expert_w[i], moe_expert_b[i],
                         moe_gate_w, moe_up_w, moe_down_w, ln_g[i, 1], ln_b[i, 1], feeds_attention)
        h = res[0]
        h8 = res[1] if feeds_attention else None
    return h.reshape(bsz, seq, d)
```

```python
import functools
import math

import numpy as np
import jax
import jax.numpy as jnp
from jax import lax
from jax.experimental import pallas as pl
from jax.experimental.pallas import tpu as pltpu

F32 = jnp.float32
BF16 = jnp.bfloat16
I32 = jnp.int32

D_MODEL = 1024
DEPTH = 2
DN_ALPHA = (2 * DEPTH) ** 0.25
LN_EPS = 1e-5
LOG2_E = math.log2(math.e)

SSM_D_INNER = 2048
SSM_HEAD_DIM = 64
SSM_HEADS = 32
SSM_GROUPS = 4
SSM_STATE = 128
SSM_CONV = 4
SSM_CHUNK = 128
SSM_CONV_DIM = SSM_D_INNER + 2 * SSM_GROUPS * SSM_STATE

ATTN_HEAD_DIM = 64
ATTN_GROUPS = ((128, 1, 6), (512, 4, 5), (2048, 16, 5))
ATTN_HEADS = 16
ATTN_BLOCK = 128
N_BUCKETS = 32
MAX_DISTANCE = 2048
ATTN_SLAB = 384
NEG_BIG = -1e30
ATTN_ITEMS_PER_PASS = 8

MOE_GROUPS = 4
MOE_EPG = 8
MOE_EXPERTS = MOE_GROUPS * MOE_EPG
MOE_D_FF = 512
MOE_ROW_TILE = 256
COMBINE_TILE = 512

LANES = 128
DMA_QUEUES = 2
VMEM_LIMIT = 48 * 1024 * 1024
EXPERTS_VMEM_LIMIT = 56 * 1024 * 1024


def _cparams(*sem):
    return pltpu.CompilerParams(dimension_semantics=sem, vmem_limit_bytes=VMEM_LIMIT)


def _layer_norm(x, g, b):
    mu = jnp.mean(x, -1, keepdims=True)
    xc = x - mu
    var = jnp.mean(xc * xc, -1, keepdims=True)
    return xc * lax.rsqrt(var + LN_EPS) * g + b


def _split2(x):
    hi = x.astype(BF16)
    lo = (x - hi.astype(F32)).astype(BF16)
    return hi, lo


def _dot(a, b):
    return jnp.dot(a, b, preferred_element_type=F32)


def _dot_f32ish(a, b):
    ah, al = _split2(a)
    bh, bl = _split2(b)
    return _dot(ah, bh) + _dot(al, bh) + _dot(ah, bl)


def _mm_kernel(a_ref, b_ref, o_ref):
    o_ref[...] = _dot(a_ref[...].astype(BF16), b_ref[...]).astype(o_ref.dtype)


def _matmul(a, b, out_dtype, tm, tn):
    m, k = a.shape
    nc = b.shape[1]
    return pl.pallas_call(
        _mm_kernel,
        grid=(m // tm, nc // tn),
        in_specs=[pl.BlockSpec((tm, k), lambda i, j: (i, 0)),
                  pl.BlockSpec((k, tn), lambda i, j: (0, j))],
        out_specs=pl.BlockSpec((tm, tn), lambda i, j: (i, j)),
        out_shape=jax.ShapeDtypeStruct((m, nc), out_dtype),
        compiler_params=_cparams("parallel", "parallel"),
        name="matmul",
    )(a, b)


ROW_SPLIT = D_MODEL // LANES


def _store_row_tiled(ref, val):
    rows = val.shape[0]
    for c in range(ROW_SPLIT):
        ref[pl.ds(c, rows, stride=ROW_SPLIT), :] = val[:, c * LANES:(c + 1) * LANES]


def _load_row_tiled(ref):
    rows = ref.shape[0] // ROW_SPLIT
    return jnp.concatenate([ref[pl.ds(c, rows, stride=ROW_SPLIT), :] for c in range(ROW_SPLIT)], axis=1)


HALF = D_MODEL // 2
U32 = jnp.uint32
HI16 = 0xFFFF0000


def _pack_bf16_pairs(x):
    bits = lax.bitcast_convert_type(x.astype(BF16).astype(F32), U32)
    return (bits[:, :HALF] >> 16) | (bits[:, HALF:] & U32(HI16))


def _unpack_bf16_pairs(w):
    lo = lax.bitcast_convert_type(w << 16, F32).astype(BF16)
    hi = lax.bitcast_convert_type(w & U32(HI16), F32).astype(BF16)
    return lo, hi


def _mm_ln_kernel(a_ref, w_ref, h_ref, g_ref, b_ref, o_ref, op_ref):
    acc = _dot(a_ref[...], w_ref[...])
    out = _layer_norm(DN_ALPHA * h_ref[...] + acc, g_ref[...], b_ref[...])
    o_ref[...] = out
    op_ref[...] = _pack_bf16_pairs(out)


def _matmul_res_ln(a, w, h, g, b, tm):
    m, k = a.shape
    d = w.shape[1]
    return pl.pallas_call(
        _mm_ln_kernel,
        grid=(m // tm,),
        in_specs=[pl.BlockSpec((tm, k), lambda i: (i, 0)),
                  pl.BlockSpec((k, d), lambda i: (0, 0)),
                  pl.BlockSpec((tm, d), lambda i: (i, 0)),
                  pl.BlockSpec((1, d), lambda i: (0, 0)),
                  pl.BlockSpec((1, d), lambda i: (0, 0))],
        out_specs=[pl.BlockSpec((tm, d), lambda i: (i, 0)),
                   pl.BlockSpec((tm, HALF), lambda i: (i, 0))],
        out_shape=[jax.ShapeDtypeStruct((m, d), F32),
                   jax.ShapeDtypeStruct((m, HALF), U32)],
        compiler_params=_cparams("parallel"),
        name="matmul_res_ln",
    )(a, w, h, g, b)


SSD_CHUNKS_PER_STEP = 2


def _ssd_kernel(z_ref, xbc_ref, dt_ref, cw_ref, cb_ref, dtb_ref, alog_ref, dsk_ref, nw_ref,
                y_ref, xe_ref, st_ref):
    @pl.when(pl.program_id(1) == 0)
    def _():
        xe_ref[:, 0:8, :] = jnp.zeros((xe_ref.shape[0], 8, LANES), F32)
        st_ref[...] = jnp.zeros_like(st_ref)

    for sub in range(SSD_CHUNKS_PER_STEP):
        _ssd_chunk(pl.ds(sub * SSM_CHUNK, SSM_CHUNK), z_ref, xbc_ref, dt_ref, cw_ref, cb_ref, dtb_ref,
                   alog_ref, dsk_ref, nw_ref, y_ref, xe_ref, st_ref)


def _ssd_chunk(rows, z_ref, xbc_ref, dt_ref, cw_ref, cb_ref, dtb_ref, alog_ref, dsk_ref, nw_ref,
               y_ref, xe_ref, st_ref):
    q = SSM_CHUNK
    w = cw_ref[...]
    bias = cb_ref[...]
    act = []
    for c in range(SSM_CONV_DIM // LANES):
        cols = slice(c * LANES, (c + 1) * LANES)
        u = xbc_ref[rows, cols]
        xe_ref[c, 8:8 + q, :] = u
        conv = (bias[:, cols] + w[3:4, cols] * u + w[2:3, cols] * xe_ref[c, 7:7 + q, :]
                + w[1:2, cols] * xe_ref[c, 6:6 + q, :] + w[0:1, cols] * xe_ref[c, 5:5 + q, :])
        xe_ref[c, 0:8, :] = xe_ref[c, q:q + 8, :]
        act.append(conv * jax.nn.sigmoid(conv))

    pre = dt_ref[rows, :] + dtb_ref[...]
    dt = jnp.maximum(pre, 0.0) + jnp.log(1.0 + jnp.exp(-jnp.abs(pre)))
    adt = dt * (-jnp.exp(alog_ref[...]) * LOG2_E)

    row = lax.broadcasted_iota(I32, (q, q), 0)
    col = lax.broadcasted_iota(I32, (q, q), 1)
    tril = row >= col
    tri_b = jnp.where(tril, 1.0, 0.0).astype(BF16)
    a_hi = adt.astype(BF16)
    r1 = adt - a_hi.astype(F32)
    a_mid = r1.astype(BF16)
    a_lo = (r1 - a_mid.astype(F32)).astype(BF16)
    acs = _dot(tri_b, a_hi) + _dot(tri_b, a_mid) + _dot(tri_b, a_lo)
    acs_t = acs.T
    acs_dt_t = acs_t - jnp.log(dt.T) * LOG2_E
    eacs = jnp.exp2(acs)
    left = col < SSM_HEAD_DIM

    for g in range(SSM_GROUPS):
        b0 = SSM_D_INNER + g * SSM_STATE
        c0 = SSM_D_INNER + SSM_GROUPS * SSM_STATE + g * SSM_STATE
        bm = act[b0 // LANES]
        cm = act[c0 // LANES].astype(BF16)
        cb = lax.dot_general(cm, bm.astype(BF16), (((1,), (1,)), ((), ())),
                             preferred_element_type=F32)
        bm_t = bm.T
        gs = g * 512
        y_off = _dot(cm, st_ref[:, gs:gs + 512].astype(BF16))
        slabs = []
        for pr in range(4):
            ha = g * 8 + pr * 2
            hb = ha + 1
            cs = gs + pr * LANES
            x2 = act[cs // LANES]
            x2b = x2.astype(BF16)
            ys, ups = [], []
            for h in (ha, hb):
                a_col = acs[:, h:h + 1]
                a_src = acs_dt_t[h:h + 1, :]
                decay = jnp.where(tril, jnp.exp2(a_col - a_src), 0.0)
                ys.append(_dot((cb * decay).astype(BF16), x2b))
                to_end = jnp.exp2(acs_t[h:h + 1, q - 1:q] - a_src)
                ups.append(_dot((bm_t * to_end).astype(BF16), x2b))
            y_diag = jnp.where(left, ys[0], ys[1])
            upd = jnp.where(left, ups[0], ups[1])
            e2 = jnp.where(left, eacs[:, ha:ha + 1], eacs[:, hb:hb + 1])
            cd = jnp.where(left[0:1, :], eacs[q - 1:q, ha:ha + 1], eacs[q - 1:q, hb:hb + 1])
            y2 = y_diag + y_off[:, pr * LANES:(pr + 1) * LANES] * e2 + dsk_ref[:, cs:cs + LANES] * x2
            st_ref[:, cs:cs + LANES] = st_ref[:, cs:cs + LANES] * cd + upd
            slabs.append(y2)
        yg = jnp.concatenate(slabs, axis=1)
        zg = z_ref[rows, gs:gs + 512]
        yg = yg * (zg * jax.nn.sigmoid(zg))
        ms = jnp.mean(yg * yg, -1, keepdims=True)
        y_ref[rows, gs:gs + 512] = (yg * lax.rsqrt(ms + LN_EPS) * nw_ref[:, gs:gs + 512]).astype(y_ref.dtype)


def _ssd(z, xbc, dt_raw, conv_w, conv_b, dt_bias, a_log, d_rep, norm_w, bsz, seq):
    n = z.shape[0]
    q = SSM_CHUNK * SSD_CHUNKS_PER_STEP
    nchunk = seq // q
    tok = lambda b, c: (b * nchunk + c, 0)
    fixed = lambda b, c: (0, 0)
    return pl.pallas_call(
        _ssd_kernel,
        grid=(bsz, nchunk),
        in_specs=[pl.BlockSpec((q, SSM_D_INNER), tok),
                  pl.BlockSpec((q, SSM_CONV_DIM), tok),
                  pl.BlockSpec((q, LANES), tok),
                  pl.BlockSpec((SSM_CONV, SSM_CONV_DIM), fixed),
                  pl.BlockSpec((1, SSM_CONV_DIM), fixed),
                  pl.BlockSpec((1, LANES), fixed),
                  pl.BlockSpec((1, LANES), fixed),
                  pl.BlockSpec((1, SSM_D_INNER), fixed),
                  pl.BlockSpec((1, SSM_D_INNER), fixed)],
        out_specs=pl.BlockSpec((q, SSM_D_INNER), tok),
        out_shape=jax.ShapeDtypeStruct((n, SSM_D_INNER), BF16),
        scratch_shapes=[pltpu.VMEM((SSM_CONV_DIM // LANES, SSM_CHUNK + 8, LANES), F32),
                        pltpu.VMEM((SSM_STATE, SSM_D_INNER), F32)],
        compiler_params=_cparams("parallel", "arbitrary"),
        name="ssd_chunk",
    )(z, xbc, dt_raw, conv_w, conv_b, dt_bias, a_log, d_rep, norm_w)


def _residue_major_pieces(h8_ref, n_tokens, dil):
    span = ATTN_BLOCK * dil
    pieces = []
    for blk in range(n_tokens // span):
        for r in range(dil):
            first = (blk * span + r) * ROW_SPLIT
            pieces.append(jnp.concatenate(
                [h8_ref[pl.ds(first + c, ATTN_BLOCK, stride=ROW_SPLIT * dil), :] for c in range(ROW_SPLIT)],
                axis=1).astype(BF16))
    return pieces


def _qkv_dilated_kernel(h8_ref, w_ref, o_ref, *, dil):
    per_dot = 4
    rows = _residue_major_pieces(h8_ref, o_ref.shape[0], dil)
    for k in range(0, len(rows), per_dot):
        x = jnp.concatenate(rows[k:k + per_dot], axis=0)
        o_ref[k * ATTN_BLOCK:(k + per_dot) * ATTN_BLOCK, :] = _dot(x, w_ref[...]).astype(o_ref.dtype)


def _qkv_dilated(h8, w, dil, tm):
    n = h8.shape[0] // ROW_SPLIT
    nc = w.shape[1]
    return pl.pallas_call(
        functools.partial(_qkv_dilated_kernel, dil=dil),
        grid=(n // tm,),
        in_specs=[pl.BlockSpec((tm * ROW_SPLIT, LANES), lambda i: (i, 0)),
                  pl.BlockSpec((D_MODEL, nc), lambda i: (0, 0))],
        out_specs=pl.BlockSpec((tm, nc), lambda i: (i, 0)),
        out_shape=jax.ShapeDtypeStruct((n, nc), BF16),
        compiler_params=_cparams("parallel"),
        name=f"qkv_dil{dil}",
    )(h8, w)


def _attn_kernel(*refs, nh, dil, has_prev, blocks, per_pass):
    span = ATTN_BLOCK * dil
    if has_prev:
        q_ref, kp_ref, kc_ref, vp_ref, vc_ref, bp_ref, bc_ref = refs[:7]
        out_refs = refs[7:]
        first_pen = jnp.where(pl.program_id(1) > 0, 0.0, NEG_BIG)
    else:
        q_ref, kc_ref, vc_ref, bc_ref = refs[:4]
        out_refs = refs[4:]
    s_scr, p_scr, max_scr = out_refs[-3:]
    o_refs, lse_ref = out_refs[:-4], out_refs[-4]
    ones = jnp.ones((s_scr.shape[2], LANES), BF16)
    lane = lax.broadcasted_iota(I32, (ATTN_BLOCK, LANES), 1)
    left = lane < ATTN_HEAD_DIM
    zero = jnp.zeros((), BF16)
    nt = (((1,), (1,)), ((), ()))

    def block_start(row):
        return row if isinstance(row, int) else pl.multiple_of(row, ATTN_BLOCK)

    def work(items):
        info = []
        for slot, (blk, r) in enumerate(items):
            rows = pl.ds(block_start(blk * span + r * ATTN_BLOCK), ATTN_BLOCK)
            dst = pl.ds(blk * span + r, ATTN_BLOCK, stride=dil)
            kprev = vprev = prows = pen = None
            if has_prev and blk == 0:
                kprev, vprev, pen = kp_ref, vp_ref, first_pen
                prows = pl.ds(block_start(r * ATTN_BLOCK), ATTN_BLOCK)
            elif has_prev:
                kprev, vprev, pen = kc_ref, vc_ref, 0.0
                prows = pl.ds(block_start((blk - 1) * span + r * ATTN_BLOCK), ATTN_BLOCK)
            info.append((slot * nh, rows, dst, kprev, vprev, prows, pen))
        for base, rows, dst, kprev, vprev, prows, pen in info:
            for hh in range(nh):
                cols = pl.ds(hh // 2 * LANES, LANES)
                qm = jnp.where(left if hh % 2 == 0 else ~left, q_ref[rows, cols], zero)
                s_c = lax.dot_general(qm, kc_ref[rows, cols], nt, preferred_element_type=F32) + bc_ref[hh]
                if has_prev:
                    s_p = lax.dot_general(qm, kprev[prows, cols], nt, preferred_element_type=F32)
                    s_scr[base + hh, :, 0:ATTN_BLOCK] = s_p + (bp_ref[hh] + pen)
                    s_scr[base + hh, :, ATTN_BLOCK:] = s_c
                else:
                    s_scr[base + hh] = s_c
        for base, *_ in info:
            for hh in range(nh):
                s = s_scr[base + hh]
                m = jnp.max(s, -1, keepdims=True)
                p = jnp.exp(s - m)
                p_scr[base + hh] = p.astype(BF16)
                max_scr[base + hh] = m
        for base, rows, dst, kprev, vprev, prows, pen in info:
            lse_sum = jnp.zeros((ATTN_BLOCK, 1), F32)
            for pr in range(ATTN_SLAB // LANES):
                cols = pl.ds(pr * LANES, LANES)
                halves = []
                for hh in (pr * 2, pr * 2 + 1):
                    if hh >= nh:
                        halves.append(jnp.zeros((ATTN_BLOCK, LANES), F32))
                        continue
                    den = _dot(p_scr[base + hh], ones)
                    den1 = den[:, 0:1]
                    if has_prev:
                        o = (_dot(p_scr[base + hh, :, 0:ATTN_BLOCK], vprev[prows, cols])
                             + _dot(p_scr[base + hh, :, ATTN_BLOCK:], vc_ref[rows, cols]))
                    else:
                        o = _dot(p_scr[base + hh], vc_ref[rows, cols])
                    halves.append(o * (1.0 / den))
                    lse_sum = lse_sum + (max_scr[base + hh] + jnp.log(den1))
                o_refs[pr][dst, :] = jnp.where(left, halves[0], halves[1])
            lse_ref[dst, :] = jnp.broadcast_to(lse_sum * (1.0 / nh), (ATTN_BLOCK, LANES))

    if dil == 1:
        for b0 in range(0, blocks, per_pass):
            work([(b, 0) for b in range(b0, min(blocks, b0 + per_pass))])
    else:
        res_per_pass = min(dil, per_pass)
        blocks_per_pass = max(1, per_pass // res_per_pass)
        for b0 in range(0, blocks, blocks_per_pass):
            def body(it, carry, b0=b0):
                work([(b, it * res_per_pass + k) for b in range(b0, min(blocks, b0 + blocks_per_pass))
                      for k in range(res_per_pass)])
                return carry

            lax.fori_loop(0, dil // res_per_pass, body, 0)


def _attention_group(qkv, bias_p, bias_c, gi, dil, nh, bsz, seq):
    span = ATTN_BLOCK * dil
    nb = seq // span
    has_prev = nb > 1
    blocks = max(1, 1024 // span)
    steps = nb // blocks
    blk = (blocks * span, ATTN_SLAB)
    cur = lambda which: (lambda b, n: (b * steps + n, which))
    prev = lambda which: (lambda b, n: (b * nb + jnp.maximum(n * blocks - 1, 0), which))
    fixed = lambda b, n: (0, 0, 0)
    tab = pl.BlockSpec((nh, ATTN_BLOCK, ATTN_BLOCK), fixed)
    if has_prev:
        pblk = (span, ATTN_SLAB)
        in_specs = [pl.BlockSpec(blk, cur(0)), pl.BlockSpec(pblk, prev(1)), pl.BlockSpec(blk, cur(1)),
                    pl.BlockSpec(pblk, prev(2)), pl.BlockSpec(blk, cur(2)), tab, tab]
        args = (qkv, qkv, qkv, qkv, qkv, bias_p, bias_c)
    else:
        in_specs = [pl.BlockSpec(blk, cur(0)), pl.BlockSpec(blk, cur(1)), pl.BlockSpec(blk, cur(2)), tab]
        args = (qkv, qkv, qkv, bias_c)
    n_out = ATTN_SLAB // LANES + 1
    keys = 2 * ATTN_BLOCK if has_prev else ATTN_BLOCK
    slots = min(dil * blocks, ATTN_ITEMS_PER_PASS)
    outs = pl.pallas_call(
        functools.partial(_attn_kernel, nh=nh, dil=dil, has_prev=has_prev, blocks=blocks, per_pass=slots),
        grid=(bsz, steps),
        in_specs=in_specs,
        out_specs=[pl.BlockSpec((blocks * span, LANES), lambda b, n: (b * steps + n, 0))] * n_out,
        out_shape=[jax.ShapeDtypeStruct((bsz * seq, LANES), F32)] * n_out,
        scratch_shapes=[pltpu.VMEM((slots * nh, ATTN_BLOCK, keys), F32),
                        pltpu.VMEM((slots * nh, ATTN_BLOCK, keys), BF16),
                        pltpu.VMEM((slots * nh, ATTN_BLOCK, 1), F32)],
        compiler_params=_cparams("parallel", "parallel"),
        name=f"dilated_attn_g{gi}",
    )(*args)
    return outs[:-1], outs[-1]


def _attn_out_kernel(*refs):
    n_pairs = ATTN_SLAB // LANES
    n_groups = len(ATTN_GROUPS)
    o_refs = refs[:n_groups * n_pairs]
    l_refs = refs[n_groups * n_pairs:n_groups * (n_pairs + 1)]
    w_ref, h_ref, g_ref, b_ref, out_ref, outp_ref = refs[n_groups * (n_pairs + 1):]
    ls = [r[:, 0:1] for r in l_refs]
    m = jnp.maximum(jnp.maximum(ls[0], ls[1]), ls[2])
    es = [jnp.exp(v - m) for v in ls]
    scale = n_groups / (es[0] + es[1] + es[2])
    slabs = []
    for gi in range(n_groups):
        wt = es[gi] * scale
        slabs.extend((r[...] * wt).astype(BF16) for r in o_refs[gi * n_pairs:(gi + 1) * n_pairs])
    acc = _dot(jnp.concatenate(slabs, axis=1), w_ref[...])
    out = _layer_norm(DN_ALPHA * h_ref[...] + acc, g_ref[...], b_ref[...])
    out_ref[...] = out
    outp_ref[...] = _pack_bf16_pairs(out)


def _attn_out(outs, lses, w, h, g, b, tm):
    n = h.shape[0]
    tok = lambda i: (i, 0)
    fixed2 = lambda i: (0, 0)
    return pl.pallas_call(
        _attn_out_kernel,
        grid=(n // tm,),
        in_specs=[pl.BlockSpec((tm, LANES), tok)] * (len(outs) + len(lses))
                 + [pl.BlockSpec((len(ATTN_GROUPS) * ATTN_SLAB, D_MODEL), lambda i: (0, 0)),
                    pl.BlockSpec((tm, D_MODEL), tok),
                    pl.BlockSpec((1, D_MODEL), fixed2), pl.BlockSpec((1, D_MODEL), fixed2)],
        out_specs=[pl.BlockSpec((tm, D_MODEL), tok), pl.BlockSpec((tm, HALF), tok)],
        out_shape=[jax.ShapeDtypeStruct((n, D_MODEL), F32),
                   jax.ShapeDtypeStruct((n, HALF), U32)],
        compiler_params=_cparams("parallel"),
        name="attn_out_ln",
    )(*outs, *lses, w, h, g, b)


def _t5_bucket(dist):
    max_exact = N_BUCKETS // 2
    n = np.maximum(dist, 1).astype(np.float64)
    large = max_exact + (np.log(n / max_exact) / np.log(MAX_DISTANCE / max_exact)
                         * (N_BUCKETS - max_exact)).astype(np.int32)
    large = np.minimum(large, N_BUCKETS - 1)
    return np.where(dist < max_exact, dist, large).astype(np.int32)


def _group_bias(rel_bias, h0, nh, dil):
    qi = np.arange(ATTN_BLOCK)[:, None]
    ki = np.arange(ATTN_BLOCK)[None, :]
    tabs = []
    for delta, band in ((qi + ATTN_BLOCK - ki, ki >= qi), (qi - ki, ki <= qi)):
        bucket = _t5_bucket(np.clip(delta, 0, None) * dil)
        onehot = (bucket[..., None] == np.arange(N_BUCKETS)).astype(np.float32)
        t = jnp.einsum("qkb,bh->hqk", onehot, rel_bias[:, h0:h0 + nh], precision=lax.Precision.HIGHEST)
        tabs.append(jnp.where(band[None], t, NEG_BIG).astype(F32))
    return tabs


def _router_kernel(h_ref, w_ref, b_ref, ints_ref, flts_ref, cnt_ref, carry_ref, before_ref, *, tm):
    @pl.when(pl.program_id(0) == 0)
    def _():
        carry_ref[...] = jnp.zeros_like(carry_ref)
        tr = lax.broadcasted_iota(I32, (tm, tm), 0)
        tc = lax.broadcasted_iota(I32, (tm, tm), 1)
        before_ref[...] = jnp.where(tr < tc, 1.0, 0.0).astype(BF16)

    lt = _dot_f32ish(h_ref[...], w_ref[...]).T + b_ref[...]
    gl = lt[0:MOE_GROUPS]
    r4 = lax.broadcasted_iota(I32, (MOE_GROUPS, tm), 0)
    gmax = jnp.max(gl, 0, keepdims=True)
    gidx = jnp.min(jnp.where(gl == gmax, r4, MOE_GROUPS), 0, keepdims=True)
    gval = 1.0 / jnp.sum(jnp.exp(gl - gmax), 0, keepdims=True)

    esel = jnp.zeros((MOE_EPG, tm), F32)
    for g in range(MOE_GROUPS):
        esel = jnp.where(gidx == g, lt[8 + g * MOE_EPG:8 + (g + 1) * MOE_EPG], esel)
    r8 = lax.broadcasted_iota(I32, (MOE_EPG, tm), 0)
    v1 = jnp.max(esel, 0, keepdims=True)
    i1 = jnp.min(jnp.where(esel == v1, r8, MOE_EPG), 0, keepdims=True)
    rest = jnp.where(r8 == i1, -jnp.inf, esel)
    v2 = jnp.max(rest, 0, keepdims=True)
    i2 = jnp.min(jnp.where(rest == v2, r8, MOE_EPG), 0, keepdims=True)
    t = jnp.exp(v2 - v1)
    p1 = gval / (1.0 + t)
    p2 = p1 * t
    e1 = gidx * MOE_EPG + i1
    e2 = gidx * MOE_EPG + i2

    r32 = lax.broadcasted_iota(I32, (MOE_EXPERTS, tm), 0)
    oh1 = r32 == e1
    oh2 = r32 == e2
    oh = jnp.where(oh1 | oh2, 1.0, 0.0)
    base = _dot(oh.astype(BF16), before_ref[...]) + carry_ref[:, 0:1]
    rank1 = jnp.sum(jnp.where(oh1, base, 0.0), 0, keepdims=True).astype(I32)
    rank2 = jnp.sum(jnp.where(oh2, base, 0.0), 0, keepdims=True).astype(I32)
    carry_ref[...] = carry_ref[...] + jnp.sum(oh, 1, keepdims=True)
    cnt_ref[...] = carry_ref[...]

    ints_ref[...] = jnp.where(r8 == 0, e1, jnp.where(r8 == 1, e2, jnp.where(r8 == 2, rank1,
                              jnp.where(r8 == 3, rank2, 0))))
    r128 = lax.broadcasted_iota(I32, (LANES, tm), 0)
    flts_ref[...] = jnp.where(r128 == 0, p1, jnp.where(r128 == 1, p2, 0.0)).T


def _router(h, w_r, b_r, tm):
    n = h.shape[0]
    return pl.pallas_call(
        functools.partial(_router_kernel, tm=tm),
        grid=(n // tm,),
        in_specs=[pl.BlockSpec((tm, D_MODEL), lambda i: (i, 0)),
                  pl.BlockSpec((D_MODEL, LANES), lambda i: (0, 0)),
                  pl.BlockSpec((LANES, 1), lambda i: (0, 0))],
        out_specs=[pl.BlockSpec((8, tm), lambda i: (0, i)),
                   pl.BlockSpec((tm, LANES), lambda i: (i, 0)),
                   pl.BlockSpec((MOE_EXPERTS, LANES), lambda i: (0, 0))],
        out_shape=[jax.ShapeDtypeStruct((8, n), I32),
                   jax.ShapeDtypeStruct((n, LANES), F32),
                   jax.ShapeDtypeStruct((MOE_EXPERTS, LANES), F32)],
        scratch_shapes=[pltpu.VMEM((MOE_EXPERTS, LANES), F32), pltpu.VMEM((tm, tm), BF16)],
        compiler_params=_cparams("arbitrary"),
        name="moe_router",
    )(h, w_r, b_r)


def _row_copy(src_ref, s, dst_ref, d, sem):
    def first(r):
        return r * ROW_SPLIT if isinstance(r, int) else pl.multiple_of(r * ROW_SPLIT, ROW_SPLIT)

    return pltpu.make_async_copy(src_ref.at[pl.ds(first(s), ROW_SPLIT), :],
                                 dst_ref.at[pl.ds(first(d), ROW_SPLIT), :], sem)


def _tile_wait(src_ref, dst_ref, sem):
    pltpu.make_async_copy(src_ref.at[pl.ds(0, dst_ref.shape[0]), :], dst_ref, sem).wait()


def _experts_kernel(pos1_ref, pos2_ref, te_ref, nxt_ref, na_ref, hp_hbm, zeros_ref, wg_ref, wu_ref, wd_ref,
                    y_ref, hp_ref, xa, xb, wg_f, wu_f, wd_f, wg_s, wu_s, wd_s, row_tok, slot_ref,
                    sem, hsem, wsem):
    i = pl.program_id(0)
    n_tiles = pl.num_programs(0)
    tr = MOE_ROW_TILE
    active = i < na_ref[0]
    changed = jnp.logical_or(i == 0, te_ref[i] != te_ref[jnp.maximum(i - 1, 0)])

    def weight_copies(expert, slot):
        return [pltpu.make_async_copy(src.at[expert], dst.at[slot], wsem.at[slot])
                for src, dst in ((wg_ref, wg_f), (wu_ref, wu_f), (wd_ref, wd_f))]

    @pl.when(i == 0)
    def _():
        slot_ref[0] = 0
        rows_in = pltpu.make_async_copy(hp_hbm, hp_ref, hsem)
        rows_in.start()
        for c in weight_copies(te_ref[0], 0):
            c.start()
        clear = pltpu.make_async_copy(zeros_ref, row_tok, sem)
        clear.start()
        clear.wait()

        def invert(t, carry):
            row_tok[pos1_ref[t]] = t
            row_tok[pos2_ref[t]] = t
            return carry

        lax.fori_loop(0, pos1_ref.shape[0], invert, 0, unroll=16)
        rows_in.wait()

        def pick(r, carry):
            xa[pl.ds(r, 1), :] = hp_ref[pl.ds(row_tok[r], 1), :]
            return carry

        lax.fori_loop(0, tr, pick, 0, unroll=8)

    @pl.when(jnp.logical_and(active, changed))
    def _():
        slot = slot_ref[0]
        for c in weight_copies(te_ref[i], slot):
            c.wait()
        following = nxt_ref[i]

        @pl.when(following >= 0)
        def _():
            for c in weight_copies(following, 1 - slot):
                c.start()

        wg_s[...] = wg_f[slot].astype(BF16)
        wu_s[...] = wu_f[slot].astype(BF16)
        wd_s[...] = wd_f[slot].astype(BF16)
        slot_ref[0] = 1 - slot

    def step(cur, nxt):
        base = jnp.minimum(i + 1, n_tiles - 1) * tr
        for r in range(tr):
            nxt[r:r + 1, :] = hp_ref[pl.ds(row_tok[base + r], 1), :]
        lo, hi = _unpack_bf16_pairs(cur[...])
        gate = _dot(lo, wg_s[0:HALF, :]) + _dot(hi, wg_s[HALF:, :])
        up = _dot(lo, wu_s[0:HALF, :]) + _dot(hi, wu_s[HALF:, :])
        hid = (gate * jax.nn.sigmoid(gate) * up).astype(BF16)
        _store_row_tiled(y_ref, _dot(hid, wd_s[...]))

    even = i % 2 == 0

    @pl.when(jnp.logical_and(active, even))
    def _():
        step(xa, xb)

    @pl.when(jnp.logical_and(active, jnp.logical_not(even)))
    def _():
        step(xb, xa)

    @pl.when(jnp.logical_not(active))
    def _():
        y_ref[...] = jnp.zeros_like(y_ref)


def _experts(hp, pos1, pos2, tile_expert, next_expert, n_active, wg, wu, wd):
    tr = MOE_ROW_TILE
    n_tiles = tile_expert.shape[0]
    rows = n_tiles * tr
    hbm = pl.BlockSpec(memory_space=pl.ANY)
    return pl.pallas_call(
        _experts_kernel,
        grid_spec=pltpu.PrefetchScalarGridSpec(
            num_scalar_prefetch=5,
            grid=(n_tiles,),
            in_specs=[hbm, hbm, hbm, hbm, hbm],
            out_specs=pl.BlockSpec((tr * ROW_SPLIT, LANES), lambda i, *_: (i, 0)),
            scratch_shapes=[pltpu.VMEM(hp.shape, U32),
                            pltpu.VMEM((tr, HALF), U32),
                            pltpu.VMEM((tr, HALF), U32),
                            pltpu.VMEM((2, D_MODEL, MOE_D_FF), F32),
                            pltpu.VMEM((2, D_MODEL, MOE_D_FF), F32),
                            pltpu.VMEM((2, MOE_D_FF, D_MODEL), F32),
                            pltpu.VMEM((D_MODEL, MOE_D_FF), BF16),
                            pltpu.VMEM((D_MODEL, MOE_D_FF), BF16),
                            pltpu.VMEM((MOE_D_FF, D_MODEL), BF16),
                            pltpu.SMEM((rows,), I32),
                            pltpu.SMEM((1,), I32),
                            pltpu.SemaphoreType.DMA(()),
                            pltpu.SemaphoreType.DMA(()),
                            pltpu.SemaphoreType.DMA((2,))]),
        out_shape=jax.ShapeDtypeStruct((rows * ROW_SPLIT, LANES), F32),
        compiler_params=pltpu.CompilerParams(dimension_semantics=("arbitrary",),
                                             vmem_limit_bytes=EXPERTS_VMEM_LIMIT),
        name="moe_experts",
    )(pos1, pos2, tile_expert, next_expert, n_active, hp, jnp.zeros((rows,), I32), wg, wu, wd)


def _combine_kernel(pos1_ref, pos2_ref, y_ref, h_ref, p_ref, g_ref, b_ref, *rest, tm, tiled_copy):
    out_ref = rest[0]
    a1, a2, b1, b2, sem = rest[-5:]
    i = pl.program_id(0)
    last = pl.num_programs(0) - 1

    @pl.when(i == 0)
    def _():
        def issue(t, carry):
            _row_copy(y_ref, pos1_ref[t], a1, t, sem.at[0]).start()
            _row_copy(y_ref, pos2_ref[t], a2, t, sem.at[0]).start()
            return carry

        lax.fori_loop(0, tm, issue, 0)

    def step(c1, c2, cur_sem, n1, n2, nxt_sem):
        _tile_wait(y_ref, c1, cur_sem)
        _tile_wait(y_ref, c2, cur_sem)
        base = jnp.minimum(i + 1, last) * tm
        for t in range(tm):
            _row_copy(y_ref, pos1_ref[base + t], n1, t, nxt_sem).start(priority=0)
            _row_copy(y_ref, pos2_ref[base + t], n2, t, nxt_sem).start(priority=DMA_QUEUES - 1)
        ffn = p_ref[:, 0:1] * _load_row_tiled(c1) + p_ref[:, 1:2] * _load_row_tiled(c2)
        out = _layer_norm(DN_ALPHA * h_ref[...] + ffn, g_ref[...], b_ref[...])
        out_ref[...] = out
        if tiled_copy:
            _store_row_tiled(rest[1], out)

        @pl.when(i == last)
        def _():
            _tile_wait(y_ref, n1, nxt_sem)
            _tile_wait(y_ref, n2, nxt_sem)

    @pl.when(i % 2 == 0)
    def _():
        step(a1, a2, sem.at[0], b1, b2, sem.at[1])

    @pl.when(i % 2 == 1)
    def _():
        step(b1, b2, sem.at[1], a1, a2, sem.at[0])


def _combine(y, h, pos1, pos2, gates, g, b, tm, tiled_copy):
    n = h.shape[0]
    tok = lambda i, a, c: (i, 0)
    fixed = lambda i, a, c: (0, 0)
    out_specs = [pl.BlockSpec((tm, D_MODEL), tok)]
    out_shape = [jax.ShapeDtypeStruct((n, D_MODEL), F32)]
    if tiled_copy:
        out_specs.append(pl.BlockSpec((tm * ROW_SPLIT, LANES), tok))
        out_shape.append(jax.ShapeDtypeStruct((n * ROW_SPLIT, LANES), F32))
    return pl.pallas_call(
        functools.partial(_combine_kernel, tm=tm, tiled_copy=tiled_copy),
        grid_spec=pltpu.PrefetchScalarGridSpec(
            num_scalar_prefetch=2,
            grid=(n // tm,),
            in_specs=[pl.BlockSpec(memory_space=pl.ANY),
                      pl.BlockSpec((tm, D_MODEL), tok),
                      pl.BlockSpec((tm, LANES), tok),
                      pl.BlockSpec((1, D_MODEL), fixed), pl.BlockSpec((1, D_MODEL), fixed)],
            out_specs=out_specs,
            scratch_shapes=[pltpu.VMEM((tm * ROW_SPLIT, LANES), F32)] * 4 + [pltpu.SemaphoreType.DMA((2,))]),
        out_shape=out_shape,
        compiler_params=_cparams("arbitrary"),
        name="moe_combine_ln",
    )(pos1, pos2, y, h, gates, g, b)


def _plan_kernel(ints_ref, cnt_ref, pos_ref, meta_ref, *, layer, chunk):
    tr = MOE_ROW_TILE
    n = ints_ref.shape[1]
    ne = MOE_EXPERTS
    tiles = (cnt_ref[...] + (tr - 1.0)) * (1.0 / tr)
    tiles = tiles.astype(I32).astype(F32)
    lower = lax.broadcasted_iota(I32, (ne, ne), 0) >= lax.broadcasted_iota(I32, (ne, ne), 1)
    ends = _dot(jnp.where(lower, 1.0, 0.0).astype(BF16), tiles.astype(BF16))
    start_col = ((ends - tiles) * tr).astype(I32)[:, 0:1]
    r8 = lax.broadcasted_iota(I32, (8, chunk), 0)
    re = lax.broadcasted_iota(I32, (ne, chunk), 0)
    for c in range(n // chunk):
        blk = ints_ref[:, c * chunk:(c + 1) * chunk]
        s1 = jnp.sum(jnp.where(re == blk[0:1], start_col, 0), 0, keepdims=True)
        s2 = jnp.sum(jnp.where(re == blk[1:2], start_col, 0), 0, keepdims=True)
        pos_ref[:, c * chunk:(c + 1) * chunk] = jnp.where(
            r8 == 0, s1 + blk[2:3], jnp.where(r8 == 1, s2 + blk[3:4], 0))
    width = meta_ref.shape[1]
    tile_id = lax.broadcasted_iota(I32, (ne, width), 1).astype(F32)
    te = jnp.sum(jnp.where(ends[:, 0:1] <= tile_id, 1, 0), 0, keepdims=True)
    te = jnp.minimum(te, ne - 1)
    expert = lax.broadcasted_iota(I32, (ne, width), 0)
    later = jnp.logical_and(expert > te, tiles[:, 0:1] > 0.0)
    nxt = jnp.min(jnp.where(later, expert, ne), 0, keepdims=True)
    nxt = jnp.where(nxt < ne, nxt + layer * ne, -1)
    n_used = ends[ne - 1:ne, 0:1].astype(I32)
    rm = lax.broadcasted_iota(I32, (8, width), 0)
    meta_ref[...] = jnp.where(rm == 0, te + layer * ne,
                              jnp.where(rm == 1, n_used, jnp.where(rm == 2, nxt, 0)))


def _plan(ints, cnt, layer, n_tiles):
    n = ints.shape[1]
    width = -(-n_tiles // LANES) * LANES
    return pl.pallas_call(
        functools.partial(_plan_kernel, layer=layer, chunk=2048),
        out_shape=[jax.ShapeDtypeStruct((8, n), I32), jax.ShapeDtypeStruct((8, width), I32)],
        compiler_params=pltpu.CompilerParams(vmem_limit_bytes=VMEM_LIMIT),
        name="moe_plan",
    )(ints, cnt)


def _moe_layer(h, hp, layer, group_w, group_b, expert_w, expert_b, gate_w, up_w, down_w, ln_g, ln_b,
               tiled_copy):
    n = h.shape[0]
    ew = jnp.transpose(expert_w, (1, 0, 2)).reshape(D_MODEL, MOE_EXPERTS)
    w_r = jnp.zeros((D_MODEL, LANES), F32).at[:, 0:MOE_GROUPS].set(group_w).at[:, 8:8 + MOE_EXPERTS].set(ew)
    b_r = jnp.zeros((LANES,), F32).at[0:MOE_GROUPS].set(group_b).at[8:8 + MOE_EXPERTS].set(expert_b.reshape(-1))
    ints, flts, cnt = _router(h, w_r, b_r.reshape(LANES, 1), 512)
    tr = MOE_ROW_TILE
    n_tiles = (2 * n) // tr + MOE_EXPERTS
    pos, meta = _plan(ints, cnt, layer, n_tiles)
    pos1, pos2 = pos[0], pos[1]

    y = _experts(hp, pos1, pos2, meta[0, :n_tiles], meta[2, :n_tiles], meta[1, :1],
                 gate_w.reshape(-1, D_MODEL, MOE_D_FF),
                 up_w.reshape(-1, D_MODEL, MOE_D_FF),
                 down_w.reshape(-1, MOE_D_FF, D_MODEL))
    return _combine(y, h, pos1, pos2, flts,
                    ln_g.reshape(1, -1), ln_b.reshape(1, -1), COMBINE_TILE, tiled_copy)


def _pad_heads(w, axis):
    parts = []
    h0 = 0
    for _, _, nh in ATTN_GROUPS:
        sl = [slice(None)] * w.ndim
        sl[axis] = slice(h0 * ATTN_HEAD_DIM, (h0 + nh) * ATTN_HEAD_DIM)
        part = w[tuple(sl)]
        pad = [(0, 0)] * w.ndim
        pad[axis] = (0, ATTN_SLAB - nh * ATTN_HEAD_DIM)
        parts.append(jnp.pad(part, pad))
        h0 += nh
    return parts


def _in_proj_kernel(x_ref, w_ref, wdt_ref, z_ref, xbc_ref, dt_ref, *, z_tiles):
    j = pl.program_id(1)
    tn = z_ref.shape[1]
    acc = _dot(x_ref[...].astype(BF16), w_ref[:, pl.ds(pl.multiple_of(j * tn, tn), tn)])

    @pl.when(j < z_tiles)
    def _():
        z_ref[...] = acc

    @pl.when(j >= z_tiles)
    def _():
        xbc_ref[...] = acc

    @pl.when(j == 0)
    def _():
        dt_ref[...] = _dot_f32ish(x_ref[...], wdt_ref[...])


def _in_proj(x, w_zx, w_dt, tm, tn):
    m, k = x.shape
    z_tiles = SSM_D_INNER // tn
    n_tiles = (SSM_D_INNER + SSM_CONV_DIM) // tn
    return pl.pallas_call(
        functools.partial(_in_proj_kernel, z_tiles=z_tiles),
        grid=(m // tm, n_tiles),
        in_specs=[pl.BlockSpec((tm, k), lambda i, j: (i, 0)),
                  pl.BlockSpec(memory_space=pltpu.VMEM),
                  pl.BlockSpec((k, LANES), lambda i, j: (0, 0))],
        out_specs=[pl.BlockSpec((tm, tn), lambda i, j: (i, jnp.minimum(j, z_tiles - 1))),
                   pl.BlockSpec((tm, tn), lambda i, j: (i, jnp.maximum(j - z_tiles, 0))),
                   pl.BlockSpec((tm, LANES), lambda i, j: (i, 0))],
        out_shape=[jax.ShapeDtypeStruct((m, SSM_D_INNER), F32),
                   jax.ShapeDtypeStruct((m, SSM_CONV_DIM), F32),
                   jax.ShapeDtypeStruct((m, LANES), F32)],
        compiler_params=_cparams("parallel", "arbitrary"),
        name="ssm_in_proj",
    )(x, w_zx, w_dt)


def _ssd_layer(h, in_w, conv_w, conv_b, dt_bias, a_log, d_skip, norm_w, out_w, ln_g, ln_b, bsz, seq):
    split = SSM_D_INNER + SSM_CONV_DIM
    dt_w = jnp.pad(in_w[:, split:], ((0, 0), (0, LANES - SSM_HEADS)))
    z, xbc, dt_raw = _in_proj(h, in_w.astype(BF16), dt_w, 1024, 1024)
    pad32 = lambda v: jnp.pad(v, (0, LANES - SSM_HEADS)).reshape(1, LANES)
    y = _ssd(z, xbc, dt_raw, conv_w, conv_b.reshape(1, -1), pad32(dt_bias), pad32(a_log),
             jnp.repeat(d_skip, SSM_HEAD_DIM).reshape(1, -1), norm_w.reshape(1, -1), bsz, seq)
    return _matmul_res_ln(y, out_w.astype(BF16), h, ln_g.reshape(1, -1), ln_b.reshape(1, -1), 1024)


def _qkv_group_weights(kv_w, q_w):
    width = ATTN_HEADS * ATTN_HEAD_DIM
    w_q = _pad_heads(q_w * (ATTN_HEAD_DIM ** -0.5), 1)
    w_k = _pad_heads(kv_w[:, :width], 1)
    w_v = _pad_heads(kv_w[:, width:], 1)
    return [jnp.concatenate([w_q[gi], w_k[gi], w_v[gi]], axis=1).astype(BF16) for gi in range(len(ATTN_GROUPS))]


def _attn_layer(h, h8, qkv_w, o_w, rel_bias, ln_g, ln_b, bsz, seq):
    outs, lses = [], []
    h0 = 0
    for gi, (_, dil, nh) in enumerate(ATTN_GROUPS):
        if dil == 1:
            qkv = _matmul(h, qkv_w[gi], BF16, 1024, 3 * ATTN_SLAB)
        else:
            qkv = _qkv_dilated(h8, qkv_w[gi], dil, max(1024, ATTN_BLOCK * dil))
        bias_p, bias_c = _group_bias(rel_bias, h0, nh, dil)
        o, lse = _attention_group(qkv, bias_p, bias_c, gi, dil, nh, bsz, seq)
        outs.extend(o)
        lses.append(lse)
        h0 += nh
    w_o = jnp.concatenate(_pad_heads(o_w, 0), axis=0).astype(BF16)
    return _attn_out(outs, lses, w_o, h, ln_g.reshape(1, -1), ln_b.reshape(1, -1), 1024)


def kernel(x, ssm_in_w, ssm_conv_w, ssm_conv_b, ssm_dt_bias, ssm_a_log, ssm_d, ssm_norm_w, ssm_out_w,
           kv_w, attn_q_w, attn_o_w, rel_bias, moe_group_w, moe_group_b, moe_expert_w, moe_expert_b,
           moe_gate_w, moe_up_w, moe_down_w, ln_g, ln_b):
    bsz, seq, d = x.shape
    h = x.reshape(bsz * seq, d)
    n_ssd = DEPTH // 2
    h8 = None
    for i in range(DEPTH):
        if i < n_ssd:
            h, hp = _ssd_layer(h, ssm_in_w[i], ssm_conv_w[i], ssm_conv_b[i], ssm_dt_bias[i], ssm_a_log[i],
                               ssm_d[i], ssm_norm_w[i], ssm_out_w[i], ln_g[i, 0], ln_b[i, 0], bsz, seq)
        else:
            j = i - n_ssd
            h, hp = _attn_layer(h, h8, _qkv_group_weights(kv_w, attn_q_w[j]), attn_o_w[j], rel_bias,
                                ln_g[i, 0], ln_b[i, 0], bsz, seq)
        feeds_attention = n_ssd <= i + 1 < DEPTH
        res = _moe_layer(h, hp, i, moe_group_w[i], moe_group_b[i], moe_expert_w[i], moe_expert_b[i],
                         moe_gate_w, moe_up_w, moe_down_w, ln_g[i, 1], ln_b[i, 1], feeds_attention)
        h = res[0]
        h8 = res[1] if feeds_attention else None
    return h.reshape(bsz, seq, d)
```

```python
import functools
import math

import numpy as np
import jax
import jax.numpy as jnp
from jax import lax
from jax.experimental import pallas as pl
from jax.experimental.pallas import tpu as pltpu

F32 = jnp.float32
BF16 = jnp.bfloat16
I32 = jnp.int32

D_MODEL = 1024
DEPTH = 2
DN_ALPHA = (2 * DEPTH) ** 0.25
LN_EPS = 1e-5
LOG2_E = math.log2(math.e)

SSM_D_INNER = 2048
SSM_HEAD_DIM = 64
SSM_HEADS = 32
SSM_GROUPS = 4
SSM_STATE = 128
SSM_CONV = 4
SSM_CHUNK = 128
SSM_CONV_DIM = SSM_D_INNER + 2 * SSM_GROUPS * SSM_STATE

ATTN_HEAD_DIM = 64
ATTN_GROUPS = ((128, 1, 6), (512, 4, 5), (2048, 16, 5))
ATTN_HEADS = 16
ATTN_BLOCK = 128
N_BUCKETS = 32
MAX_DISTANCE = 2048
ATTN_SLAB = 384
NEG_BIG = -1e30
ATTN_ITEMS_PER_PASS = 8

MOE_GROUPS = 4
MOE_EPG = 8
MOE_EXPERTS = MOE_GROUPS * MOE_EPG
MOE_D_FF = 512
MOE_ROW_TILE = 256
COMBINE_TILE = 1024

LANES = 128
DMA_QUEUES = 2
VMEM_LIMIT = 48 * 1024 * 1024
EXPERTS_VMEM_LIMIT = 56 * 1024 * 1024


def _cparams(*sem):
    return pltpu.CompilerParams(dimension_semantics=sem, vmem_limit_bytes=VMEM_LIMIT)


def _layer_norm(x, g, b):
    mu = jnp.mean(x, -1, keepdims=True)
    xc = x - mu
    var = jnp.mean(xc * xc, -1, keepdims=True)
    return xc * lax.rsqrt(var + LN_EPS) * g + b


def _split2(x):
    hi = x.astype(BF16)
    lo = (x - hi.astype(F32)).astype(BF16)
    return hi, lo


def _dot(a, b):
    return jnp.dot(a, b, preferred_element_type=F32)


def _dot_f32ish(a, b):
    ah, al = _split2(a)
    bh, bl = _split2(b)
    return _dot(ah, bh) + _dot(al, bh) + _dot(ah, bl)


def _mm_kernel(a_ref, b_ref, o_ref):
    o_ref[...] = _dot(a_ref[...].astype(BF16), b_ref[...]).astype(o_ref.dtype)


def _matmul(a, b, out_dtype, tm, tn):
    m, k = a.shape
    nc = b.shape[1]
    return pl.pallas_call(
        _mm_kernel,
        grid=(m // tm, nc // tn),
        in_specs=[pl.BlockSpec((tm, k), lambda i, j: (i, 0)),
                  pl.BlockSpec((k, tn), lambda i, j: (0, j))],
        out_specs=pl.BlockSpec((tm, tn), lambda i, j: (i, j)),
        out_shape=jax.ShapeDtypeStruct((m, nc), out_dtype),
        compiler_params=_cparams("parallel", "parallel"),
        name="matmul",
    )(a, b)


ROW_SPLIT = D_MODEL // LANES


def _store_row_tiled(ref, val):
    rows = val.shape[0]
    for c in range(ROW_SPLIT):
        ref[pl.ds(c, rows, stride=ROW_SPLIT), :] = val[:, c * LANES:(c + 1) * LANES]


def _load_row_tiled(ref):
    rows = ref.shape[0] // ROW_SPLIT
    return jnp.concatenate([ref[pl.ds(c, rows, stride=ROW_SPLIT), :] for c in range(ROW_SPLIT)], axis=1)


HALF = D_MODEL // 2
U32 = jnp.uint32
HI16 = 0xFFFF0000


def _pack_bf16_pairs(x):
    bits = lax.bitcast_convert_type(x.astype(BF16).astype(F32), U32)
    return (bits[:, :HALF] >> 16) | (bits[:, HALF:] & U32(HI16))


def _unpack_bf16_pairs(w):
    lo = lax.bitcast_convert_type(w << 16, F32).astype(BF16)
    hi = lax.bitcast_convert_type(w & U32(HI16), F32).astype(BF16)
    return lo, hi


def _mm_ln_kernel(a_ref, w_ref, h_ref, g_ref, b_ref, o_ref, op_ref):
    acc = _dot(a_ref[...], w_ref[...])
    out = _layer_norm(DN_ALPHA * h_ref[...] + acc, g_ref[...], b_ref[...])
    o_ref[...] = out
    op_ref[...] = _pack_bf16_pairs(out)


def _matmul_res_ln(a, w, h, g, b, tm):
    m, k = a.shape
    d = w.shape[1]
    return pl.pallas_call(
        _mm_ln_kernel,
        grid=(m // tm,),
        in_specs=[pl.BlockSpec((tm, k), lambda i: (i, 0)),
                  pl.BlockSpec((k, d), lambda i: (0, 0)),
                  pl.BlockSpec((tm, d), lambda i: (i, 0)),
                  pl.BlockSpec((1, d), lambda i: (0, 0)),
                  pl.BlockSpec((1, d), lambda i: (0, 0))],
        out_specs=[pl.BlockSpec((tm, d), lambda i: (i, 0)),
                   pl.BlockSpec((tm, HALF), lambda i: (i, 0))],
        out_shape=[jax.ShapeDtypeStruct((m, d), F32),
                   jax.ShapeDtypeStruct((m, HALF), U32)],
        compiler_params=_cparams("parallel"),
        name="matmul_res_ln",
    )(a, w, h, g, b)


SSD_CHUNKS_PER_STEP = 2


def _ssd_kernel(z_ref, xbc_ref, dt_ref, cw_ref, cb_ref, dtb_ref, alog_ref, dsk_ref, nw_ref,
                y_ref, xe_ref, st_ref):
    @pl.when(pl.program_id(1) == 0)
    def _():
        xe_ref[:, 0:8, :] = jnp.zeros((xe_ref.shape[0], 8, LANES), F32)
        st_ref[...] = jnp.zeros_like(st_ref)

    for sub in range(SSD_CHUNKS_PER_STEP):
        _ssd_chunk(pl.ds(sub * SSM_CHUNK, SSM_CHUNK), z_ref, xbc_ref, dt_ref, cw_ref, cb_ref, dtb_ref,
                   alog_ref, dsk_ref, nw_ref, y_ref, xe_ref, st_ref)


def _ssd_chunk(rows, z_ref, xbc_ref, dt_ref, cw_ref, cb_ref, dtb_ref, alog_ref, dsk_ref, nw_ref,
               y_ref, xe_ref, st_ref):
    q = SSM_CHUNK
    w = cw_ref[...]
    bias = cb_ref[...]
    act = []
    for c in range(SSM_CONV_DIM // LANES):
        cols = slice(c * LANES, (c + 1) * LANES)
        u = xbc_ref[rows, cols]
        xe_ref[c, 8:8 + q, :] = u
        conv = (bias[:, cols] + w[3:4, cols] * u + w[2:3, cols] * xe_ref[c, 7:7 + q, :]
                + w[1:2, cols] * xe_ref[c, 6:6 + q, :] + w[0:1, cols] * xe_ref[c, 5:5 + q, :])
        xe_ref[c, 0:8, :] = xe_ref[c, q:q + 8, :]
        act.append(conv * jax.nn.sigmoid(conv))

    pre = dt_ref[rows, :] + dtb_ref[...]
    dt = jnp.maximum(pre, 0.0) + jnp.log(1.0 + jnp.exp(-jnp.abs(pre)))
    adt = dt * (-jnp.exp(alog_ref[...]) * LOG2_E)

    row = lax.broadcasted_iota(I32, (q, q), 0)
    col = lax.broadcasted_iota(I32, (q, q), 1)
    tril = row >= col
    tri_b = jnp.where(tril, 1.0, 0.0).astype(BF16)
    a_hi = adt.astype(BF16)
    r1 = adt - a_hi.astype(F32)
    a_mid = r1.astype(BF16)
    a_lo = (r1 - a_mid.astype(F32)).astype(BF16)
    acs = _dot(tri_b, a_hi) + _dot(tri_b, a_mid) + _dot(tri_b, a_lo)
    acs_t = acs.T
    acs_dt_t = acs_t - jnp.log(dt.T) * LOG2_E
    eacs = jnp.exp2(acs)
    left = col < SSM_HEAD_DIM

    for g in range(SSM_GROUPS):
        b0 = SSM_D_INNER + g * SSM_STATE
        c0 = SSM_D_INNER + SSM_GROUPS * SSM_STATE + g * SSM_STATE
        bm = act[b0 // LANES]
        cm = act[c0 // LANES].astype(BF16)
        cb = lax.dot_general(cm, bm.astype(BF16), (((1,), (1,)), ((), ())),
                             preferred_element_type=F32)
        bm_t = bm.T
        gs = g * 512
        y_off = _dot(cm, st_ref[:, gs:gs + 512].astype(BF16))
        slabs = []
        for pr in range(4):
            ha = g * 8 + pr * 2
            hb = ha + 1
            cs = gs + pr * LANES
            x2 = act[cs // LANES]
            x2b = x2.astype(BF16)
            ys, ups = [], []
            for h in (ha, hb):
                a_col = acs[:, h:h + 1]
                a_src = acs_dt_t[h:h + 1, :]
                decay = jnp.where(tril, jnp.exp2(a_col - a_src), 0.0)
                ys.append(_dot((cb * decay).astype(BF16), x2b))
                to_end = jnp.exp2(acs_t[h:h + 1, q - 1:q] - a_src)
                ups.append(_dot((bm_t * to_end).astype(BF16), x2b))
            y_diag = jnp.where(left, ys[0], ys[1])
            upd = jnp.where(left, ups[0], ups[1])
            e2 = jnp.where(left, eacs[:, ha:ha + 1], eacs[:, hb:hb + 1])
            cd = jnp.where(left[0:1, :], eacs[q - 1:q, ha:ha + 1], eacs[q - 1:q, hb:hb + 1])
            y2 = y_diag + y_off[:, pr * LANES:(pr + 1) * LANES] * e2 + dsk_ref[:, cs:cs + LANES] * x2
            st_ref[:, cs:cs + LANES] = st_ref[:, cs:cs + LANES] * cd + upd
            slabs.append(y2)
        yg = jnp.concatenate(slabs, axis=1)
        zg = z_ref[rows, gs:gs + 512]
        yg = yg * (zg * jax.nn.sigmoid(zg))
        ms = jnp.mean(yg * yg, -1, keepdims=True)
        y_ref[rows, gs:gs + 512] = (yg * lax.rsqrt(ms + LN_EPS) * nw_ref[:, gs:gs + 512]).astype(y_ref.dtype)


def _ssd(z, xbc, dt_raw, conv_w, conv_b, dt_bias, a_log, d_rep, norm_w, bsz, seq):
    n = z.shape[0]
    q = SSM_CHUNK * SSD_CHUNKS_PER_STEP
    nchunk = seq // q
    tok = lambda b, c: (b * nchunk + c, 0)
    fixed = lambda b, c: (0, 0)
    return pl.pallas_call(
        _ssd_kernel,
        grid=(bsz, nchunk),
        in_specs=[pl.BlockSpec((q, SSM_D_INNER), tok),
                  pl.BlockSpec((q, SSM_CONV_DIM), tok),
                  pl.BlockSpec((q, LANES), tok),
                  pl.BlockSpec((SSM_CONV, SSM_CONV_DIM), fixed),
                  pl.BlockSpec((1, SSM_CONV_DIM), fixed),
                  pl.BlockSpec((1, LANES), fixed),
                  pl.BlockSpec((1, LANES), fixed),
                  pl.BlockSpec((1, SSM_D_INNER), fixed),
                  pl.BlockSpec((1, SSM_D_INNER), fixed)],
        out_specs=pl.BlockSpec((q, SSM_D_INNER), tok),
        out_shape=jax.ShapeDtypeStruct((n, SSM_D_INNER), BF16),
        scratch_shapes=[pltpu.VMEM((SSM_CONV_DIM // LANES, SSM_CHUNK + 8, LANES), F32),
                        pltpu.VMEM((SSM_STATE, SSM_D_INNER), F32)],
        compiler_params=_cparams("parallel", "arbitrary"),
        name="ssd_chunk",
    )(z, xbc, dt_raw, conv_w, conv_b, dt_bias, a_log, d_rep, norm_w)


def _residue_major_pieces(h8_ref, n_tokens, dil):
    span = ATTN_BLOCK * dil
    pieces = []
    for blk in range(n_tokens // span):
        for r in range(dil):
            first = (blk * span + r) * ROW_SPLIT
            pieces.append(jnp.concatenate(
                [h8_ref[pl.ds(first + c, ATTN_BLOCK, stride=ROW_SPLIT * dil), :] for c in range(ROW_SPLIT)],
                axis=1).astype(BF16))
    return pieces


def _qkv_dilated_kernel(h8_ref, w_ref, o_ref, *, dil):
    per_dot = 4
    rows = _residue_major_pieces(h8_ref, o_ref.shape[0], dil)
    for k in range(0, len(rows), per_dot):
        x = jnp.concatenate(rows[k:k + per_dot], axis=0)
        o_ref[k * ATTN_BLOCK:(k + per_dot) * ATTN_BLOCK, :] = _dot(x, w_ref[...]).astype(o_ref.dtype)


def _qkv_dilated(h8, w, dil, tm):
    n = h8.shape[0] // ROW_SPLIT
    nc = w.shape[1]
    return pl.pallas_call(
        functools.partial(_qkv_dilated_kernel, dil=dil),
        grid=(n // tm,),
        in_specs=[pl.BlockSpec((tm * ROW_SPLIT, LANES), lambda i: (i, 0)),
                  pl.BlockSpec((D_MODEL, nc), lambda i: (0, 0))],
        out_specs=pl.BlockSpec((tm, nc), lambda i: (i, 0)),
        out_shape=jax.ShapeDtypeStruct((n, nc), BF16),
        compiler_params=_cparams("parallel"),
        name=f"qkv_dil{dil}",
    )(h8, w)


def _attn_kernel(*refs, nh, dil, has_prev, blocks, per_pass):
    span = ATTN_BLOCK * dil
    if has_prev:
        q_ref, kp_ref, kc_ref, vp_ref, vc_ref, bp_ref, bc_ref = refs[:7]
        out_refs = refs[7:]
        first_pen = jnp.where(pl.program_id(1) > 0, 0.0, NEG_BIG)
    else:
        q_ref, kc_ref, vc_ref, bc_ref = refs[:4]
        out_refs = refs[4:]
    s_scr, p_scr, max_scr = out_refs[-3:]
    o_refs, lse_ref = out_refs[:-4], out_refs[-4]
    ones = jnp.ones((s_scr.shape[2], LANES), BF16)
    lane = lax.broadcasted_iota(I32, (ATTN_BLOCK, LANES), 1)
    left = lane < ATTN_HEAD_DIM
    zero = jnp.zeros((), BF16)
    nt = (((1,), (1,)), ((), ()))

    def block_start(row):
        return row if isinstance(row, int) else pl.multiple_of(row, ATTN_BLOCK)

    def work(items):
        info = []
        for slot, (blk, r) in enumerate(items):
            rows = pl.ds(block_start(blk * span + r * ATTN_BLOCK), ATTN_BLOCK)
            dst = pl.ds(blk * span + r, ATTN_BLOCK, stride=dil)
            kprev = vprev = prows = pen = None
            if has_prev and blk == 0:
                kprev, vprev, pen = kp_ref, vp_ref, first_pen
                prows = pl.ds(block_start(r * ATTN_BLOCK), ATTN_BLOCK)
            elif has_prev:
                kprev, vprev, pen = kc_ref, vc_ref, 0.0
                prows = pl.ds(block_start((blk - 1) * span + r * ATTN_BLOCK), ATTN_BLOCK)
            info.append((slot * nh, rows, dst, kprev, vprev, prows, pen))
        for base, rows, dst, kprev, vprev, prows, pen in info:
            for hh in range(nh):
                cols = pl.ds(hh // 2 * LANES, LANES)
                qm = jnp.where(left if hh % 2 == 0 else ~left, q_ref[rows, cols], zero)
                s_c = lax.dot_general(qm, kc_ref[rows, cols], nt, preferred_element_type=F32) + bc_ref[hh]
                if has_prev:
                    s_p = lax.dot_general(qm, kprev[prows, cols], nt, preferred_element_type=F32)
                    s_scr[base + hh, :, 0:ATTN_BLOCK] = s_p + (bp_ref[hh] + pen)
                    s_scr[base + hh, :, ATTN_BLOCK:] = s_c
                else:
                    s_scr[base + hh] = s_c
        for base, *_ in info:
            for hh in range(nh):
                s = s_scr[base + hh]
                m = jnp.max(s, -1, keepdims=True)
                p = jnp.exp(s - m)
                p_scr[base + hh] = p.astype(BF16)
                max_scr[base + hh] = m
        for base, rows, dst, kprev, vprev, prows, pen in info:
            lse_sum = jnp.zeros((ATTN_BLOCK, 1), F32)
            for pr in range(ATTN_SLAB // LANES):
                cols = pl.ds(pr * LANES, LANES)
                halves = []
                for hh in (pr * 2, pr * 2 + 1):
                    if hh >= nh:
                        halves.append(jnp.zeros((ATTN_BLOCK, LANES), F32))
                        continue
                    den = _dot(p_scr[base + hh], ones)
                    den1 = den[:, 0:1]
                    if has_prev:
                        o = (_dot(p_scr[base + hh, :, 0:ATTN_BLOCK], vprev[prows, cols])
                             + _dot(p_scr[base + hh, :, ATTN_BLOCK:], vc_ref[rows, cols]))
                    else:
                        o = _dot(p_scr[base + hh], vc_ref[rows, cols])
                    halves.append(o * (1.0 / den))
                    lse_sum = lse_sum + (max_scr[base + hh] + jnp.log(den1))
                o_refs[pr][dst, :] = jnp.where(left, halves[0], halves[1])
            lse_ref[dst, :] = jnp.broadcast_to(lse_sum * (1.0 / nh), (ATTN_BLOCK, LANES))

    if dil == 1:
        for b0 in range(0, blocks, per_pass):
            work([(b, 0) for b in range(b0, min(blocks, b0 + per_pass))])
    else:
        res_per_pass = min(dil, per_pass)
        blocks_per_pass = max(1, per_pass // res_per_pass)
        for b0 in range(0, blocks, blocks_per_pass):
            def body(it, carry, b0=b0):
                work([(b, it * res_per_pass + k) for b in range(b0, min(blocks, b0 + blocks_per_pass))
                      for k in range(res_per_pass)])
                return carry

            lax.fori_loop(0, dil // res_per_pass, body, 0)


def _attention_group(qkv, bias_p, bias_c, gi, dil, nh, bsz, seq):
    span = ATTN_BLOCK * dil
    nb = seq // span
    has_prev = nb > 1
    blocks = max(1, 1024 // span)
    steps = nb // blocks
    blk = (blocks * span, ATTN_SLAB)
    cur = lambda which: (lambda b, n: (b * steps + n, which))
    prev = lambda which: (lambda b, n: (b * nb + jnp.maximum(n * blocks - 1, 0), which))
    fixed = lambda b, n: (0, 0, 0)
    tab = pl.BlockSpec((nh, ATTN_BLOCK, ATTN_BLOCK), fixed)
    if has_prev:
        pblk = (span, ATTN_SLAB)
        in_specs = [pl.BlockSpec(blk, cur(0)), pl.BlockSpec(pblk, prev(1)), pl.BlockSpec(blk, cur(1)),
                    pl.BlockSpec(pblk, prev(2)), pl.BlockSpec(blk, cur(2)), tab, tab]
        args = (qkv, qkv, qkv, qkv, qkv, bias_p, bias_c)
    else:
        in_specs = [pl.BlockSpec(blk, cur(0)), pl.BlockSpec(blk, cur(1)), pl.BlockSpec(blk, cur(2)), tab]
        args = (qkv, qkv, qkv, bias_c)
    n_out = ATTN_SLAB // LANES + 1
    keys = 2 * ATTN_BLOCK if has_prev else ATTN_BLOCK
    slots = min(dil * blocks, ATTN_ITEMS_PER_PASS)
    outs = pl.pallas_call(
        functools.partial(_attn_kernel, nh=nh, dil=dil, has_prev=has_prev, blocks=blocks, per_pass=slots),
        grid=(bsz, steps),
        in_specs=in_specs,
        out_specs=[pl.BlockSpec((blocks * span, LANES), lambda b, n: (b * steps + n, 0))] * n_out,
        out_shape=[jax.ShapeDtypeStruct((bsz * seq, LANES), F32)] * n_out,
        scratch_shapes=[pltpu.VMEM((slots * nh, ATTN_BLOCK, keys), F32),
                        pltpu.VMEM((slots * nh, ATTN_BLOCK, keys), BF16),
                        pltpu.VMEM((slots * nh, ATTN_BLOCK, 1), F32)],
        compiler_params=_cparams("parallel", "parallel"),
        name=f"dilated_attn_g{gi}",
    )(*args)
    return outs[:-1], outs[-1]


def _attn_out_kernel(*refs):
    n_pairs = ATTN_SLAB // LANES
    n_groups = len(ATTN_GROUPS)
    o_refs = refs[:n_groups * n_pairs]
    l_refs = refs[n_groups * n_pairs:n_groups * (n_pairs + 1)]
    w_ref, h_ref, g_ref, b_ref, out_ref, outp_ref = refs[n_groups * (n_pairs + 1):]
    ls = [r[:, 0:1] for r in l_refs]
    m = jnp.maximum(jnp.maximum(ls[0], ls[1]), ls[2])
    es = [jnp.exp(v - m) for v in ls]
    scale = n_groups / (es[0] + es[1] + es[2])
    slabs = []
    for gi in range(n_groups):
        wt = es[gi] * scale
        slabs.extend((r[...] * wt).astype(BF16) for r in o_refs[gi * n_pairs:(gi + 1) * n_pairs])
    acc = _dot(jnp.concatenate(slabs, axis=1), w_ref[...])
    out = _layer_norm(DN_ALPHA * h_ref[...] + acc, g_ref[...], b_ref[...])
    out_ref[...] = out
    outp_ref[...] = _pack_bf16_pairs(out)


def _attn_out(outs, lses, w, h, g, b, tm):
    n = h.shape[0]
    tok = lambda i: (i, 0)
    fixed2 = lambda i: (0, 0)
    return pl.pallas_call(
        _attn_out_kernel,
        grid=(n // tm,),
        in_specs=[pl.BlockSpec((tm, LANES), tok)] * (len(outs) + len(lses))
                 + [pl.BlockSpec((len(ATTN_GROUPS) * ATTN_SLAB, D_MODEL), lambda i: (0, 0)),
                    pl.BlockSpec((tm, D_MODEL), tok),
                    pl.BlockSpec((1, D_MODEL), fixed2), pl.BlockSpec((1, D_MODEL), fixed2)],
        out_specs=[pl.BlockSpec((tm, D_MODEL), tok), pl.BlockSpec((tm, HALF), tok)],
        out_shape=[jax.ShapeDtypeStruct((n, D_MODEL), F32),
                   jax.ShapeDtypeStruct((n, HALF), U32)],
        compiler_params=_cparams("parallel"),
        name="attn_out_ln",
    )(*outs, *lses, w, h, g, b)


def _t5_bucket(dist):
    max_exact = N_BUCKETS // 2
    n = np.maximum(dist, 1).astype(np.float64)
    large = max_exact + (np.log(n / max_exact) / np.log(MAX_DISTANCE / max_exact)
                         * (N_BUCKETS - max_exact)).astype(np.int32)
    large = np.minimum(large, N_BUCKETS - 1)
    return np.where(dist < max_exact, dist, large).astype(np.int32)


def _group_bias(rel_bias, h0, nh, dil):
    qi = np.arange(ATTN_BLOCK)[:, None]
    ki = np.arange(ATTN_BLOCK)[None, :]
    tabs = []
    for delta, band in ((qi + ATTN_BLOCK - ki, ki >= qi), (qi - ki, ki <= qi)):
        bucket = _t5_bucket(np.clip(delta, 0, None) * dil)
        onehot = (bucket[..., None] == np.arange(N_BUCKETS)).astype(np.float32)
        t = jnp.einsum("qkb,bh->hqk", onehot, rel_bias[:, h0:h0 + nh], precision=lax.Precision.HIGHEST)
        tabs.append(jnp.where(band[None], t, NEG_BIG).astype(F32))
    return tabs


def _router_kernel(h_ref, w_ref, b_ref, ints_ref, flts_ref, cnt_ref, carry_ref, before_ref, *, tm):
    @pl.when(pl.program_id(0) == 0)
    def _():
        carry_ref[...] = jnp.zeros_like(carry_ref)
        tr = lax.broadcasted_iota(I32, (tm, tm), 0)
        tc = lax.broadcasted_iota(I32, (tm, tm), 1)
        before_ref[...] = jnp.where(tr < tc, 1.0, 0.0).astype(BF16)

    lt = _dot_f32ish(h_ref[...], w_ref[...]).T + b_ref[...]
    gl = lt[0:MOE_GROUPS]
    r4 = lax.broadcasted_iota(I32, (MOE_GROUPS, tm), 0)
    gmax = jnp.max(gl, 0, keepdims=True)
    gidx = jnp.min(jnp.where(gl == gmax, r4, MOE_GROUPS), 0, keepdims=True)
    gval = 1.0 / jnp.sum(jnp.exp(gl - gmax), 0, keepdims=True)

    esel = jnp.zeros((MOE_EPG, tm), F32)
    for g in range(MOE_GROUPS):
        esel = jnp.where(gidx == g, lt[8 + g * MOE_EPG:8 + (g + 1) * MOE_EPG], esel)
    r8 = lax.broadcasted_iota(I32, (MOE_EPG, tm), 0)
    v1 = jnp.max(esel, 0, keepdims=True)
    i1 = jnp.min(jnp.where(esel == v1, r8, MOE_EPG), 0, keepdims=True)
    rest = jnp.where(r8 == i1, -jnp.inf, esel)
    v2 = jnp.max(rest, 0, keepdims=True)
    i2 = jnp.min(jnp.where(rest == v2, r8, MOE_EPG), 0, keepdims=True)
    t = jnp.exp(v2 - v1)
    p1 = gval / (1.0 + t)
    p2 = p1 * t
    e1 = gidx * MOE_EPG + i1
    e2 = gidx * MOE_EPG + i2

    r32 = lax.broadcasted_iota(I32, (MOE_EXPERTS, tm), 0)
    oh1 = r32 == e1
    oh2 = r32 == e2
    oh = jnp.where(oh1 | oh2, 1.0, 0.0)
    base = _dot(oh.astype(BF16), before_ref[...]) + carry_ref[:, 0:1]
    rank1 = jnp.sum(jnp.where(oh1, base, 0.0), 0, keepdims=True).astype(I32)
    rank2 = jnp.sum(jnp.where(oh2, base, 0.0), 0, keepdims=True).astype(I32)
    carry_ref[...] = carry_ref[...] + jnp.sum(oh, 1, keepdims=True)
    cnt_ref[...] = carry_ref[...]

    ints_ref[...] = jnp.where(r8 == 0, e1, jnp.where(r8 == 1, e2, jnp.where(r8 == 2, rank1,
                              jnp.where(r8 == 3, rank2, 0))))
    r128 = lax.broadcasted_iota(I32, (LANES, tm), 0)
    flts_ref[...] = jnp.where(r128 == 0, p1, jnp.where(r128 == 1, p2, 0.0)).T


def _router(h, w_r, b_r, tm):
    n = h.shape[0]
    return pl.pallas_call(
        functools.partial(_router_kernel, tm=tm),
        grid=(n // tm,),
        in_specs=[pl.BlockSpec((tm, D_MODEL), lambda i: (i, 0)),
                  pl.BlockSpec((D_MODEL, LANES), lambda i: (0, 0)),
                  pl.BlockSpec((LANES, 1), lambda i: (0, 0))],
        out_specs=[pl.BlockSpec((8, tm), lambda i: (0, i)),
                   pl.BlockSpec((tm, LANES), lambda i: (i, 0)),
                   pl.BlockSpec((MOE_EXPERTS, LANES), lambda i: (0, 0))],
        out_shape=[jax.ShapeDtypeStruct((8, n), I32),
                   jax.ShapeDtypeStruct((n, LANES), F32),
                   jax.ShapeDtypeStruct((MOE_EXPERTS, LANES), F32)],
        scratch_shapes=[pltpu.VMEM((MOE_EXPERTS, LANES), F32), pltpu.VMEM((tm, tm), BF16)],
        compiler_params=_cparams("arbitrary"),
        name="moe_router",
    )(h, w_r, b_r)


def _row_copy(src_ref, s, dst_ref, d, sem):
    def first(r):
        return r * ROW_SPLIT if isinstance(r, int) else pl.multiple_of(r * ROW_SPLIT, ROW_SPLIT)

    return pltpu.make_async_copy(src_ref.at[pl.ds(first(s), ROW_SPLIT), :],
                                 dst_ref.at[pl.ds(first(d), ROW_SPLIT), :], sem)


def _tile_wait(src_ref, dst_ref, sem):
    pltpu.make_async_copy(src_ref.at[pl.ds(0, dst_ref.shape[0]), :], dst_ref, sem).wait()


def _experts_kernel(pos1_ref, pos2_ref, te_ref, nxt_ref, na_ref, hp_hbm, zeros_ref, wg_ref, wu_ref, wd_ref,
                    y_ref, hp_ref, xa, xb, wg_f, wu_f, wd_f, wg_s, wu_s, wd_s, row_tok, slot_ref,
                    sem, hsem, wsem):
    i = pl.program_id(0)
    n_tiles = pl.num_programs(0)
    tr = MOE_ROW_TILE
    active = i < na_ref[0]
    changed = jnp.logical_or(i == 0, te_ref[i] != te_ref[jnp.maximum(i - 1, 0)])

    def weight_copies(expert, slot):
        return [pltpu.make_async_copy(src.at[expert], dst.at[slot], wsem.at[slot])
                for src, dst in ((wg_ref, wg_f), (wu_ref, wu_f), (wd_ref, wd_f))]

    @pl.when(i == 0)
    def _():
        slot_ref[0] = 0
        rows_in = pltpu.make_async_copy(hp_hbm, hp_ref, hsem)
        rows_in.start()
        for c in weight_copies(te_ref[0], 0):
            c.start()
        clear = pltpu.make_async_copy(zeros_ref, row_tok, sem)
        clear.start()
        clear.wait()

        def invert(t, carry):
            row_tok[pos1_ref[t]] = t
            row_tok[pos2_ref[t]] = t
            return carry

        lax.fori_loop(0, pos1_ref.shape[0], invert, 0, unroll=16)
        rows_in.wait()

        def pick(r, carry):
            xa[pl.ds(r, 1), :] = hp_ref[pl.ds(row_tok[r], 1), :]
            return carry

        lax.fori_loop(0, tr, pick, 0, unroll=8)

    @pl.when(jnp.logical_and(active, changed))
    def _():
        slot = slot_ref[0]
        for c in weight_copies(te_ref[i], slot):
            c.wait()
        following = nxt_ref[i]

        @pl.when(following >= 0)
        def _():
            for c in weight_copies(following, 1 - slot):
                c.start()

        wg_s[...] = wg_f[slot].astype(BF16)
        wu_s[...] = wu_f[slot].astype(BF16)
        wd_s[...] = wd_f[slot].astype(BF16)
        slot_ref[0] = 1 - slot

    def step(cur, nxt):
        base = jnp.minimum(i + 1, n_tiles - 1) * tr
        for r in range(tr):
            nxt[r:r + 1, :] = hp_ref[pl.ds(row_tok[base + r], 1), :]
        lo, hi = _unpack_bf16_pairs(cur[...])
        gate = _dot(lo, wg_s[0:HALF, :]) + _dot(hi, wg_s[HALF:, :])
        up = _dot(lo, wu_s[0:HALF, :]) + _dot(hi, wu_s[HALF:, :])
        hid = (gate * jax.nn.sigmoid(gate) * up).astype(BF16)
        _store_row_tiled(y_ref, _dot(hid, wd_s[...]))

    even = i % 2 == 0

    @pl.when(jnp.logical_and(active, even))
    def _():
        step(xa, xb)

    @pl.when(jnp.logical_and(active, jnp.logical_not(even)))
    def _():
        step(xb, xa)

    @pl.when(jnp.logical_not(active))
    def _():
        y_ref[...] = jnp.zeros_like(y_ref)


def _experts(hp, pos1, pos2, tile_expert, next_expert, n_active, wg, wu, wd):
    tr = MOE_ROW_TILE
    n_tiles = tile_expert.shape[0]
    rows = n_tiles * tr
    hbm = pl.BlockSpec(memory_space=pl.ANY)
    return pl.pallas_call(
        _experts_kernel,
        grid_spec=pltpu.PrefetchScalarGridSpec(
            num_scalar_prefetch=5,
            grid=(n_tiles,),
            in_specs=[hbm, hbm, hbm, hbm, hbm],
            out_specs=pl.BlockSpec((tr * ROW_SPLIT, LANES), lambda i, *_: (i, 0)),
            scratch_shapes=[pltpu.VMEM(hp.shape, U32),
                            pltpu.VMEM((tr, HALF), U32),
                            pltpu.VMEM((tr, HALF), U32),
                            pltpu.VMEM((2, D_MODEL, MOE_D_FF), F32),
                            pltpu.VMEM((2, D_MODEL, MOE_D_FF), F32),
                            pltpu.VMEM((2, MOE_D_FF, D_MODEL), F32),
                            pltpu.VMEM((D_MODEL, MOE_D_FF), BF16),
                            pltpu.VMEM((D_MODEL, MOE_D_FF), BF16),
                            pltpu.VMEM((MOE_D_FF, D_MODEL), BF16),
                            pltpu.SMEM((rows,), I32),
                            pltpu.SMEM((1,), I32),
                            pltpu.SemaphoreType.DMA(()),
                            pltpu.SemaphoreType.DMA(()),
                            pltpu.SemaphoreType.DMA((2,))]),
        out_shape=jax.ShapeDtypeStruct((rows * ROW_SPLIT, LANES), F32),
        compiler_params=pltpu.CompilerParams(dimension_semantics=("arbitrary",),
                                             vmem_limit_bytes=EXPERTS_VMEM_LIMIT),
        name="moe_experts",
    )(pos1, pos2, tile_expert, next_expert, n_active, hp, jnp.zeros((rows,), I32), wg, wu, wd)


def _combine_kernel(pos1_ref, pos2_ref, y_ref, h_ref, p_ref, g_ref, b_ref, *rest, tm, tiled_copy):
    out_ref = rest[0]
    a1, a2, b1, b2, sem = rest[-5:]
    i = pl.program_id(0)
    last = pl.num_programs(0) - 1

    @pl.when(i == 0)
    def _():
        def issue(t, carry):
            _row_copy(y_ref, pos1_ref[t], a1, t, sem.at[0]).start()
            _row_copy(y_ref, pos2_ref[t], a2, t, sem.at[0]).start()
            return carry

        lax.fori_loop(0, tm, issue, 0)

    def step(c1, c2, cur_sem, n1, n2, nxt_sem):
        _tile_wait(y_ref, c1, cur_sem)
        _tile_wait(y_ref, c2, cur_sem)
        base = jnp.minimum(i + 1, last) * tm
        for t in range(tm):
            _row_copy(y_ref, pos1_ref[base + t], n1, t, nxt_sem).start(priority=0)
            _row_copy(y_ref, pos2_ref[base + t], n2, t, nxt_sem).start(priority=DMA_QUEUES - 1)
        ffn = p_ref[:, 0:1] * _load_row_tiled(c1) + p_ref[:, 1:2] * _load_row_tiled(c2)
        out = _layer_norm(DN_ALPHA * h_ref[...] + ffn, g_ref[...], b_ref[...])
        out_ref[...] = out
        if tiled_copy:
            _store_row_tiled(rest[1], out)

        @pl.when(i == last)
        def _():
            _tile_wait(y_ref, n1, nxt_sem)
            _tile_wait(y_ref, n2, nxt_sem)

    @pl.when(i % 2 == 0)
    def _():
        step(a1, a2, sem.at[0], b1, b2, sem.at[1])

    @pl.when(i % 2 == 1)
    def _():
        step(b1, b2, sem.at[1], a1, a2, sem.at[0])


def _combine(y, h, pos1, pos2, gates, g, b, tm, tiled_copy):
    n = h.shape[0]
    tok = lambda i, a, c: (i, 0)
    fixed = lambda i, a, c: (0, 0)
    out_specs = [pl.BlockSpec((tm, D_MODEL), tok)]
    out_shape = [jax.ShapeDtypeStruct((n, D_MODEL), F32)]
    if tiled_copy:
        out_specs.append(pl.BlockSpec((tm * ROW_SPLIT, LANES), tok))
        out_shape.append(jax.ShapeDtypeStruct((n * ROW_SPLIT, LANES), F32))
    return pl.pallas_call(
        functools.partial(_combine_kernel, tm=tm, tiled_copy=tiled_copy),
        grid_spec=pltpu.PrefetchScalarGridSpec(
            num_scalar_prefetch=2,
            grid=(n // tm,),
            in_specs=[pl.BlockSpec(memory_space=pl.ANY),
                      pl.BlockSpec((tm, D_MODEL), tok),
                      pl.BlockSpec((tm, LANES), tok),
                      pl.BlockSpec((1, D_MODEL), fixed), pl.BlockSpec((1, D_MODEL), fixed)],
            out_specs=out_specs,
            scratch_shapes=[pltpu.VMEM((tm * ROW_SPLIT, LANES), F32)] * 4 + [pltpu.SemaphoreType.DMA((2,))]),
        out_shape=out_shape,
        compiler_params=_cparams("arbitrary"),
        name="moe_combine_ln",
    )(pos1, pos2, y, h, gates, g, b)


def _plan_kernel(ints_ref, cnt_ref, pos_ref, meta_ref, *, layer, chunk):
    tr = MOE_ROW_TILE
    n = ints_ref.shape[1]
    ne = MOE_EXPERTS
    tiles = (cnt_ref[...] + (tr - 1.0)) * (1.0 / tr)
    tiles = tiles.astype(I32).astype(F32)
    lower = lax.broadcasted_iota(I32, (ne, ne), 0) >= lax.broadcasted_iota(I32, (ne, ne), 1)
    ends = _dot(jnp.where(lower, 1.0, 0.0).astype(BF16), tiles.astype(BF16))
    start_col = ((ends - tiles) * tr).astype(I32)[:, 0:1]
    r8 = lax.broadcasted_iota(I32, (8, chunk), 0)
    re = lax.broadcasted_iota(I32, (ne, chunk), 0)
    for c in range(n // chunk):
        blk = ints_ref[:, c * chunk:(c + 1) * chunk]
        s1 = jnp.sum(jnp.where(re == blk[0:1], start_col, 0), 0, keepdims=True)
        s2 = jnp.sum(jnp.where(re == blk[1:2], start_col, 0), 0, keepdims=True)
        pos_ref[:, c * chunk:(c + 1) * chunk] = jnp.where(
            r8 == 0, s1 + blk[2:3], jnp.where(r8 == 1, s2 + blk[3:4], 0))
    width = meta_ref.shape[1]
    tile_id = lax.broadcasted_iota(I32, (ne, width), 1).astype(F32)
    te = jnp.sum(jnp.where(ends[:, 0:1] <= tile_id, 1, 0), 0, keepdims=True)
    te = jnp.minimum(te, ne - 1)
    expert = lax.broadcasted_iota(I32, (ne, width), 0)
    later = jnp.logical_and(expert > te, tiles[:, 0:1] > 0.0)
    nxt = jnp.min(jnp.where(later, expert, ne), 0, keepdims=True)
    nxt = jnp.where(nxt < ne, nxt + layer * ne, -1)
    n_used = ends[ne - 1:ne, 0:1].astype(I32)
    rm = lax.broadcasted_iota(I32, (8, width), 0)
    meta_ref[...] = jnp.where(rm == 0, te + layer * ne,
                              jnp.where(rm == 1, n_used, jnp.where(rm == 2, nxt, 0)))


def _plan(ints, cnt, layer, n_tiles):
    n = ints.shape[1]
    width = -(-n_tiles // LANES) * LANES
    return pl.pallas_call(
        functools.partial(_plan_kernel, layer=layer, chunk=2048),
        out_shape=[jax.ShapeDtypeStruct((8, n), I32), jax.ShapeDtypeStruct((8, width), I32)],
        compiler_params=pltpu.CompilerParams(vmem_limit_bytes=VMEM_LIMIT),
        name="moe_plan",
    )(ints, cnt)


def _moe_layer(h, hp, layer, group_w, group_b, expert_w, expert_b, gate_w, up_w, down_w, ln_g, ln_b,
               tiled_copy):
    n = h.shape[0]
    ew = jnp.transpose(expert_w, (1, 0, 2)).reshape(D_MODEL, MOE_EXPERTS)
    w_r = jnp.zeros((D_MODEL, LANES), F32).at[:, 0:MOE_GROUPS].set(group_w).at[:, 8:8 + MOE_EXPERTS].set(ew)
    b_r = jnp.zeros((LANES,), F32).at[0:MOE_GROUPS].set(group_b).at[8:8 + MOE_EXPERTS].set(expert_b.reshape(-1))
    ints, flts, cnt = _router(h, w_r, b_r.reshape(LANES, 1), 512)
    tr = MOE_ROW_TILE
    n_tiles = (2 * n) // tr + MOE_EXPERTS
    pos, meta = _plan(ints, cnt, layer, n_tiles)
    pos1, pos2 = pos[0], pos[1]

    y = _experts(hp, pos1, pos2, meta[0, :n_tiles], meta[2, :n_tiles], meta[1, :1],
                 gate_w.reshape(-1, D_MODEL, MOE_D_FF),
                 up_w.reshape(-1, D_MODEL, MOE_D_FF),
                 down_w.reshape(-1, MOE_D_FF, D_MODEL))
    return _combine(y, h, pos1, pos2, flts,
                    ln_g.reshape(1, -1), ln_b.reshape(1, -1), COMBINE_TILE, tiled_copy)


def _pad_heads(w, axis):
    parts = []
    h0 = 0
    for _, _, nh in ATTN_GROUPS:
        sl = [slice(None)] * w.ndim
        sl[axis] = slice(h0 * ATTN_HEAD_DIM, (h0 + nh) * ATTN_HEAD_DIM)
        part = w[tuple(sl)]
        pad = [(0, 0)] * w.ndim
        pad[axis] = (0, ATTN_SLAB - nh * ATTN_HEAD_DIM)
        parts.append(jnp.pad(part, pad))
        h0 += nh
    return parts


def _in_proj_kernel(x_ref, w_ref, wdt_ref, z_ref, xbc_ref, dt_ref, *, z_tiles):
    j = pl.program_id(1)
    tn = z_ref.shape[1]
    acc = _dot(x_ref[...].astype(BF16), w_ref[:, pl.ds(pl.multiple_of(j * tn, tn), tn)])

    @pl.when(j < z_tiles)
    def _():
        z_ref[...] = acc

    @pl.when(j >= z_tiles)
    def _():
        xbc_ref[...] = acc

    @pl.when(j == 0)
    def _():
        dt_ref[...] = _dot_f32ish(x_ref[...], wdt_ref[...])


def _in_proj(x, w_zx, w_dt, tm, tn):
    m, k = x.shape
    z_tiles = SSM_D_INNER // tn
    n_tiles = (SSM_D_INNER + SSM_CONV_DIM) // tn
    return pl.pallas_call(
        functools.partial(_in_proj_kernel, z_tiles=z_tiles),
        grid=(m // tm, n_tiles),
        in_specs=[pl.BlockSpec((tm, k), lambda i, j: (i, 0)),
                  pl.BlockSpec(memory_space=pltpu.VMEM),
                  pl.BlockSpec((k, LANES), lambda i, j: (0, 0))],
        out_specs=[pl.BlockSpec((tm, tn), lambda i, j: (i, jnp.minimum(j, z_tiles - 1))),
                   pl.BlockSpec((tm, tn), lambda i, j: (i, jnp.maximum(j - z_tiles, 0))),
                   pl.BlockSpec((tm, LANES), lambda i, j: (i, 0))],
        out_shape=[jax.ShapeDtypeStruct((m, SSM_D_INNER), F32),
                   jax.ShapeDtypeStruct((m, SSM_CONV_DIM), F32),
                   jax.ShapeDtypeStruct((m, LANES), F32)],
        compiler_params=_cparams("parallel", "arbitrary"),
        name="ssm_in_proj",
    )(x, w_zx, w_dt)


def _ssd_layer(h, in_w, conv_w, conv_b, dt_bias, a_log, d_skip, norm_w, out_w, ln_g, ln_b, bsz, seq):
    split = SSM_D_INNER + SSM_CONV_DIM
    dt_w = jnp.pad(in_w[:, split:], ((0, 0), (0, LANES - SSM_HEADS)))
    z, xbc, dt_raw = _in_proj(h, in_w.astype(BF16), dt_w, 1024, 1024)
    pad32 = lambda v: jnp.pad(v, (0, LANES - SSM_HEADS)).reshape(1, LANES)
    y = _ssd(z, xbc, dt_raw, conv_w, conv_b.reshape(1, -1), pad32(dt_bias), pad32(a_log),
             jnp.repeat(d_skip, SSM_HEAD_DIM).reshape(1, -1), norm_w.reshape(1, -1), bsz, seq)
    return _matmul_res_ln(y, out_w.astype(BF16), h, ln_g.reshape(1, -1), ln_b.reshape(1, -1), 1024)


def _qkv_group_weights(kv_w, q_w):
    width = ATTN_HEADS * ATTN_HEAD_DIM
    w_q = _pad_heads(q_w * (ATTN_HEAD_DIM ** -0.5), 1)
    w_k = _pad_heads(kv_w[:, :width], 1)
    w_v = _pad_heads(kv_w[:, width:], 1)
    return [jnp.concatenate([w_q[gi], w_k[gi], w_v[gi]], axis=1).astype(BF16) for gi in range(len(ATTN_GROUPS))]


def _attn_layer(h, h8, qkv_w, o_w, rel_bias, ln_g, ln_b, bsz, seq):
    outs, lses = [], []
    h0 = 0
    for gi, (_, dil, nh) in enumerate(ATTN_GROUPS):
        if dil == 1:
            qkv = _matmul(h, qkv_w[gi], BF16, 1024, 3 * ATTN_SLAB)
        else:
            qkv = _qkv_dilated(h8, qkv_w[gi], dil, max(1024, ATTN_BLOCK * dil))
        bias_p, bias_c = _group_bias(rel_bias, h0, nh, dil)
        o, lse = _attention_group(qkv, bias_p, bias_c, gi, dil, nh, bsz, seq)
        outs.extend(o)
        lses.append(lse)
        h0 += nh
    w_o = jnp.concatenate(_pad_heads(o_w, 0), axis=0).astype(BF16)
    return _attn_out(outs, lses, w_o, h, ln_g.reshape(1, -1), ln_b.reshape(1, -1), 1024)


def kernel(x, ssm_in_w, ssm_conv_w, ssm_conv_b, ssm_dt_bias, ssm_a_log, ssm_d, ssm_norm_w, ssm_out_w,
           kv_w, attn_q_w, attn_o_w, rel_bias, moe_group_w, moe_group_b, moe_expert_w, moe_expert_b,
           moe_gate_w, moe_up_w, moe_down_w, ln_g, ln_b):
    bsz, seq, d = x.shape
    h = x.reshape(bsz * seq, d)
    n_ssd = DEPTH // 2
    h8 = None
    for i in range(DEPTH):
        if i < n_ssd:
            h, hp = _ssd_layer(h, ssm_in_w[i], ssm_conv_w[i], ssm_conv_b[i], ssm_dt_bias[i], ssm_a_log[i],
                               ssm_d[i], ssm_norm_w[i], ssm_out_w[i], ln_g[i, 0], ln_b[i, 0], bsz, seq)
        else:
            j = i - n_ssd
            h, hp = _attn_layer(h, h8, _qkv_group_weights(kv_w, attn_q_w[j]), attn_o_w[j], rel_bias,
                                ln_g[i, 0], ln_b[i, 0], bsz, seq)
        feeds_attention = n_ssd <= i + 1 < DEPTH
        res = _moe_layer(h, hp, i, moe_group_w[i], moe_group_b[i], moe_expert_w[i], moe_expert_b[i],
                         moe_gate_w, moe_up_w, moe_down_w, ln_g[i, 1], ln_b[i, 1], feeds_attention)
        h = res[0]
        h8 = res[1] if feeds_attention else None
    return h.reshape(bsz, seq, d)
```

```python
import functools
import math

import numpy as np
import jax
import jax.numpy as jnp
from jax import lax
from jax.experimental import pallas as pl
from jax.experimental.pallas import tpu as pltpu

F32 = jnp.float32
BF16 = jnp.bfloat16
I32 = jnp.int32

D_MODEL = 1024
DEPTH = 2
DN_ALPHA = (2 * DEPTH) ** 0.25
LN_EPS = 1e-5
LOG2_E = math.log2(math.e)

SSM_D_INNER = 2048
SSM_HEAD_DIM = 64
SSM_HEADS = 32
SSM_GROUPS = 4
SSM_STATE = 128
SSM_CONV = 4
SSM_CHUNK = 128
SSM_CONV_DIM = SSM_D_INNER + 2 * SSM_GROUPS * SSM_STATE

ATTN_HEAD_DIM = 64
ATTN_GROUPS = ((128, 1, 6), (512, 4, 5), (2048, 16, 5))
ATTN_HEADS = 16
ATTN_BLOCK = 128
N_BUCKETS = 32
MAX_DISTANCE = 2048
ATTN_SLAB = 384
NEG_BIG = -1e30
ATTN_ITEMS_PER_PASS = 8

MOE_GROUPS = 4
MOE_EPG = 8
MOE_EXPERTS = MOE_GROUPS * MOE_EPG
MOE_D_FF = 512
MOE_ROW_TILE = 256
COMBINE_TILE = 512

LANES = 128
DMA_QUEUES = 2
VMEM_LIMIT = 48 * 1024 * 1024
EXPERTS_VMEM_LIMIT = 56 * 1024 * 1024


def _cparams(*sem):
    return pltpu.CompilerParams(dimension_semantics=sem, vmem_limit_bytes=VMEM_LIMIT)


def _layer_norm(x, g, b):
    mu = jnp.mean(x, -1, keepdims=True)
    xc = x - mu
    var = jnp.mean(xc * xc, -1, keepdims=True)
    return xc * lax.rsqrt(var + LN_EPS) * g + b


def _split2(x):
    hi = x.astype(BF16)
    lo = (x - hi.astype(F32)).astype(BF16)
    return hi, lo


def _dot(a, b):
    return jnp.dot(a, b, preferred_element_type=F32)


def _dot_f32ish(a, b):
    ah, al = _split2(a)
    bh, bl = _split2(b)
    return _dot(ah, bh) + _dot(al, bh) + _dot(ah, bl)


def _mm_kernel(a_ref, b_ref, o_ref):
    o_ref[...] = _dot(a_ref[...].astype(BF16), b_ref[...]).astype(o_ref.dtype)


def _matmul(a, b, out_dtype, tm, tn):
    m, k = a.shape
    nc = b.shape[1]
    return pl.pallas_call(
        _mm_kernel,
        grid=(m // tm, nc // tn),
        in_specs=[pl.BlockSpec((tm, k), lambda i, j: (i, 0)),
                  pl.BlockSpec((k, tn), lambda i, j: (0, j))],
        out_specs=pl.BlockSpec((tm, tn), lambda i, j: (i, j)),
        out_shape=jax.ShapeDtypeStruct((m, nc), out_dtype),
        compiler_params=_cparams("parallel", "parallel"),
        name="matmul",
    )(a, b)


ROW_SPLIT = D_MODEL // LANES


def _store_row_tiled(ref, val):
    rows = val.shape[0]
    for c in range(ROW_SPLIT):
        ref[pl.ds(c, rows, stride=ROW_SPLIT), :] = val[:, c * LANES:(c + 1) * LANES]


def _load_row_tiled(ref):
    rows = ref.shape[0] // ROW_SPLIT
    return jnp.concatenate([ref[pl.ds(c, rows, stride=ROW_SPLIT), :] for c in range(ROW_SPLIT)], axis=1)


HALF = D_MODEL // 2
U32 = jnp.uint32
HI16 = 0xFFFF0000


def _pack_bf16_pairs(x):
    bits = lax.bitcast_convert_type(x.astype(BF16).astype(F32), U32)
    return (bits[:, :HALF] >> 16) | (bits[:, HALF:] & U32(HI16))


def _unpack_bf16_pairs(w):
    lo = lax.bitcast_convert_type(w << 16, F32).astype(BF16)
    hi = lax.bitcast_convert_type(w & U32(HI16), F32).astype(BF16)
    return lo, hi


def _mm_ln_kernel(a_ref, w_ref, h_ref, g_ref, b_ref, o_ref, op_ref):
    acc = _dot(a_ref[...], w_ref[...])
    out = _layer_norm(DN_ALPHA * h_ref[...] + acc, g_ref[...], b_ref[...])
    o_ref[...] = out
    op_ref[...] = _pack_bf16_pairs(out)


def _matmul_res_ln(a, w, h, g, b, tm):
    m, k = a.shape
    d = w.shape[1]
    return pl.pallas_call(
        _mm_ln_kernel,
        grid=(m // tm,),
        in_specs=[pl.BlockSpec((tm, k), lambda i: (i, 0)),
                  pl.BlockSpec((k, d), lambda i: (0, 0)),
                  pl.BlockSpec((tm, d), lambda i: (i, 0)),
                  pl.BlockSpec((1, d), lambda i: (0, 0)),
                  pl.BlockSpec((1, d), lambda i: (0, 0))],
        out_specs=[pl.BlockSpec((tm, d), lambda i: (i, 0)),
                   pl.BlockSpec((tm, HALF), lambda i: (i, 0))],
        out_shape=[jax.ShapeDtypeStruct((m, d), F32),
                   jax.ShapeDtypeStruct((m, HALF), U32)],
        compiler_params=_cparams("parallel"),
        name="matmul_res_ln",
    )(a, w, h, g, b)


SSD_CHUNKS_PER_STEP = 2


def _ssd_kernel(z_ref, xbc_ref, dt_ref, cw_ref, cb_ref, dtb_ref, alog_ref, dsk_ref, nw_ref,
                y_ref, xe_ref, st_ref):
    @pl.when(pl.program_id(1) == 0)
    def _():
        xe_ref[:, 0:8, :] = jnp.zeros((xe_ref.shape[0], 8, LANES), F32)
        st_ref[...] = jnp.zeros_like(st_ref)

    for sub in range(SSD_CHUNKS_PER_STEP):
        _ssd_chunk(pl.ds(sub * SSM_CHUNK, SSM_CHUNK), z_ref, xbc_ref, dt_ref, cw_ref, cb_ref, dtb_ref,
                   alog_ref, dsk_ref, nw_ref, y_ref, xe_ref, st_ref)


def _ssd_chunk(rows, z_ref, xbc_ref, dt_ref, cw_ref, cb_ref, dtb_ref, alog_ref, dsk_ref, nw_ref,
               y_ref, xe_ref, st_ref):
    q = SSM_CHUNK
    w = cw_ref[...]
    bias = cb_ref[...]
    act = []
    for c in range(SSM_CONV_DIM // LANES):
        cols = slice(c * LANES, (c + 1) * LANES)
        u = xbc_ref[rows, cols]
        xe_ref[c, 8:8 + q, :] = u
        conv = (bias[:, cols] + w[3:4, cols] * u + w[2:3, cols] * xe_ref[c, 7:7 + q, :]
                + w[1:2, cols] * xe_ref[c, 6:6 + q, :] + w[0:1, cols] * xe_ref[c, 5:5 + q, :])
        xe_ref[c, 0:8, :] = xe_ref[c, q:q + 8, :]
        act.append(conv * jax.nn.sigmoid(conv))

    pre = dt_ref[rows, :] + dtb_ref[...]
    dt = jnp.maximum(pre, 0.0) + jnp.log(1.0 + jnp.exp(-jnp.abs(pre)))
    adt = dt * (-jnp.exp(alog_ref[...]) * LOG2_E)

    row = lax.broadcasted_iota(I32, (q, q), 0)
    col = lax.broadcasted_iota(I32, (q, q), 1)
    tril = row >= col
    tri_b = jnp.where(tril, 1.0, 0.0).astype(BF16)
    a_hi = adt.astype(BF16)
    r1 = adt - a_hi.astype(F32)
    a_mid = r1.astype(BF16)
    a_lo = (r1 - a_mid.astype(F32)).astype(BF16)
    acs = _dot(tri_b, a_hi) + _dot(tri_b, a_mid) + _dot(tri_b, a_lo)
    acs_t = acs.T
    acs_dt_t = acs_t - jnp.log(dt.T) * LOG2_E
    eacs = jnp.exp2(acs)
    left = col < SSM_HEAD_DIM

    for g in range(SSM_GROUPS):
        b0 = SSM_D_INNER + g * SSM_STATE
        c0 = SSM_D_INNER + SSM_GROUPS * SSM_STATE + g * SSM_STATE
        bm = act[b0 // LANES]
        cm = act[c0 // LANES].astype(BF16)
        cb = lax.dot_general(cm, bm.astype(BF16), (((1,), (1,)), ((), ())),
                             preferred_element_type=F32)
        bm_t = bm.T
        gs = g * 512
        y_off = _dot(cm, st_ref[:, gs:gs + 512].astype(BF16))
        slabs = []
        for pr in range(4):
            ha = g * 8 + pr * 2
            hb = ha + 1
            cs = gs + pr * LANES
            x2 = act[cs // LANES]
            x2b = x2.astype(BF16)
            ys, ups = [], []
            for h in (ha, hb):
                a_col = acs[:, h:h + 1]
                a_src = acs_dt_t[h:h + 1, :]
                decay = jnp.where(tril, jnp.exp2(a_col - a_src), 0.0)
                ys.append(_dot((cb * decay).astype(BF16), x2b))
                to_end = jnp.exp2(acs_t[h:h + 1, q - 1:q] - a_src)
                ups.append(_dot((bm_t * to_end).astype(BF16), x2b))
            y_diag = jnp.where(left, ys[0], ys[1])
            upd = jnp.where(left, ups[0], ups[1])
            e2 = jnp.where(left, eacs[:, ha:ha + 1], eacs[:, hb:hb + 1])
            cd = jnp.where(left[0:1, :], eacs[q - 1:q, ha:ha + 1], eacs[q - 1:q, hb:hb + 1])
            y2 = y_diag + y_off[:, pr * LANES:(pr + 1) * LANES] * e2 + dsk_ref[:, cs:cs + LANES] * x2
            st_ref[:, cs:cs + LANES] = st_ref[:, cs:cs + LANES] * cd + upd
            slabs.append(y2)
        yg = jnp.concatenate(slabs, axis=1)
        zg = z_ref[rows, gs:gs + 512]
        yg = yg * (zg * jax.nn.sigmoid(zg))
        ms = jnp.mean(yg * yg, -1, keepdims=True)
        y_ref[rows, gs:gs + 512] = (yg * lax.rsqrt(ms + LN_EPS) * nw_ref[:, gs:gs + 512]).astype(y_ref.dtype)


def _ssd(z, xbc, dt_raw, conv_w, conv_b, dt_bias, a_log, d_rep, norm_w, bsz, seq):
    n = z.shape[0]
    q = SSM_CHUNK * SSD_CHUNKS_PER_STEP
    nchunk = seq // q
    tok = lambda b, c: (b * nchunk + c, 0)
    fixed = lambda b, c: (0, 0)
    return pl.pallas_call(
        _ssd_kernel,
        grid=(bsz, nchunk),
        in_specs=[pl.BlockSpec((q, SSM_D_INNER), tok),
                  pl.BlockSpec((q, SSM_CONV_DIM), tok),
                  pl.BlockSpec((q, LANES), tok),
                  pl.BlockSpec((SSM_CONV, SSM_CONV_DIM), fixed),
                  pl.BlockSpec((1, SSM_CONV_DIM), fixed),
                  pl.BlockSpec((1, LANES), fixed),
                  pl.BlockSpec((1, LANES), fixed),
                  pl.BlockSpec((1, SSM_D_INNER), fixed),
                  pl.BlockSpec((1, SSM_D_INNER), fixed)],
        out_specs=pl.BlockSpec((q, SSM_D_INNER), tok),
        out_shape=jax.ShapeDtypeStruct((n, SSM_D_INNER), BF16),
        scratch_shapes=[pltpu.VMEM((SSM_CONV_DIM // LANES, SSM_CHUNK + 8, LANES), F32),
                        pltpu.VMEM((SSM_STATE, SSM_D_INNER), F32)],
        compiler_params=_cparams("parallel", "arbitrary"),
        name="ssd_chunk",
    )(z, xbc, dt_raw, conv_w, conv_b, dt_bias, a_log, d_rep, norm_w)


def _residue_major_pieces(h8_ref, n_tokens, dil):
    span = ATTN_BLOCK * dil
    pieces = []
    for blk in range(n_tokens // span):
        for r in range(dil):
            first = (blk * span + r) * ROW_SPLIT
            pieces.append(jnp.concatenate(
                [h8_ref[pl.ds(first + c, ATTN_BLOCK, stride=ROW_SPLIT * dil), :] for c in range(ROW_SPLIT)],
                axis=1).astype(BF16))
    return pieces


def _qkv_dilated_kernel(h8_ref, *refs, dils):
    per_dot = 4
    w_refs, o_refs = refs[:len(dils)], refs[len(dils):]
    for w_ref, o_ref, dil in zip(w_refs, o_refs, dils):
        rows = _residue_major_pieces(h8_ref, o_ref.shape[0], dil)
        for k in range(0, len(rows), per_dot):
            x = jnp.concatenate(rows[k:k + per_dot], axis=0)
            o_ref[k * ATTN_BLOCK:(k + per_dot) * ATTN_BLOCK, :] = _dot(x, w_ref[...]).astype(o_ref.dtype)


def _qkv_dilated(h8, ws, dils, tm):
    n = h8.shape[0] // ROW_SPLIT
    nc = ws[0].shape[1]
    return pl.pallas_call(
        functools.partial(_qkv_dilated_kernel, dils=tuple(dils)),
        grid=(n // tm,),
        in_specs=[pl.BlockSpec((tm * ROW_SPLIT, LANES), lambda i: (i, 0))]
                 + [pl.BlockSpec((D_MODEL, nc), lambda i: (0, 0))] * len(ws),
        out_specs=[pl.BlockSpec((tm, nc), lambda i: (i, 0))] * len(ws),
        out_shape=[jax.ShapeDtypeStruct((n, nc), BF16)] * len(ws),
        compiler_params=_cparams("parallel"),
        name="qkv_dil" + "_".join(str(d) for d in dils),
    )(h8, *ws)


def _attn_kernel(*refs, nh, dil, has_prev, blocks, per_pass):
    span = ATTN_BLOCK * dil
    if has_prev:
        q_ref, kp_ref, kc_ref, vp_ref, vc_ref, bp_ref, bc_ref = refs[:7]
        out_refs = refs[7:]
        first_pen = jnp.where(pl.program_id(1) > 0, 0.0, NEG_BIG)
    else:
        q_ref, kc_ref, vc_ref, bc_ref = refs[:4]
        out_refs = refs[4:]
    s_scr, p_scr, max_scr = out_refs[-3:]
    o_refs, lse_ref = out_refs[:-4], out_refs[-4]
    ones = jnp.ones((s_scr.shape[2], LANES), BF16)
    lane = lax.broadcasted_iota(I32, (ATTN_BLOCK, LANES), 1)
    left = lane < ATTN_HEAD_DIM
    zero = jnp.zeros((), BF16)
    nt = (((1,), (1,)), ((), ()))

    def block_start(row):
        return row if isinstance(row, int) else pl.multiple_of(row, ATTN_BLOCK)

    def work(items):
        info = []
        for slot, (blk, r) in enumerate(items):
            rows = pl.ds(block_start(blk * span + r * ATTN_BLOCK), ATTN_BLOCK)
            dst = pl.ds(blk * span + r, ATTN_BLOCK, stride=dil)
            kprev = vprev = prows = pen = None
            if has_prev and blk == 0:
                kprev, vprev, pen = kp_ref, vp_ref, first_pen
                prows = pl.ds(block_start(r * ATTN_BLOCK), ATTN_BLOCK)
            elif has_prev:
                kprev, vprev, pen = kc_ref, vc_ref, 0.0
                prows = pl.ds(block_start((blk - 1) * span + r * ATTN_BLOCK), ATTN_BLOCK)
            info.append((slot * nh, rows, dst, kprev, vprev, prows, pen))
        for base, rows, dst, kprev, vprev, prows, pen in info:
            for hh in range(nh):
                cols = pl.ds(hh // 2 * LANES, LANES)
                qm = jnp.where(left if hh % 2 == 0 else ~left, q_ref[rows, cols], zero)
                s_c = lax.dot_general(qm, kc_ref[rows, cols], nt, preferred_element_type=F32) + bc_ref[hh]
                if has_prev:
                    s_p = lax.dot_general(qm, kprev[prows, cols], nt, preferred_element_type=F32)
                    s_scr[base + hh, :, 0:ATTN_BLOCK] = s_p + (bp_ref[hh] + pen)
                    s_scr[base + hh, :, ATTN_BLOCK:] = s_c
                else:
                    s_scr[base + hh] = s_c
        for base, *_ in info:
            for hh in range(nh):
                s = s_scr[base + hh]
                m = jnp.max(s, -1, keepdims=True)
                p = jnp.exp(s - m)
                p_scr[base + hh] = p.astype(BF16)
                max_scr[base + hh] = m
        for base, rows, dst, kprev, vprev, prows, pen in info:
            lse_sum = jnp.zeros((ATTN_BLOCK, 1), F32)
            for pr in range(ATTN_SLAB // LANES):
                cols = pl.ds(pr * LANES, LANES)
                halves = []
                for hh in (pr * 2, pr * 2 + 1):
                    if hh >= nh:
                        halves.append(jnp.zeros((ATTN_BLOCK, LANES), F32))
                        continue
                    den = _dot(p_scr[base + hh], ones)
                    den1 = den[:, 0:1]
                    if has_prev:
                        o = (_dot(p_scr[base + hh, :, 0:ATTN_BLOCK], vprev[prows, cols])
                             + _dot(p_scr[base + hh, :, ATTN_BLOCK:], vc_ref[rows, cols]))
                    else:
                        o = _dot(p_scr[base + hh], vc_ref[rows, cols])
                    halves.append(o * (1.0 / den))
                    lse_sum = lse_sum + (max_scr[base + hh] + jnp.log(den1))
                o_refs[pr][dst, :] = jnp.where(left, halves[0], halves[1])
            lse_ref[dst, :] = jnp.broadcast_to(lse_sum * (1.0 / nh), (ATTN_BLOCK, LANES))

    if dil == 1:
        for b0 in range(0, blocks, per_pass):
            work([(b, 0) for b in range(b0, min(blocks, b0 + per_pass))])
    else:
        res_per_pass = min(dil, per_pass)
        blocks_per_pass = max(1, per_pass // res_per_pass)
        for b0 in range(0, blocks, blocks_per_pass):
            def body(it, carry, b0=b0):
                work([(b, it * res_per_pass + k) for b in range(b0, min(blocks, b0 + blocks_per_pass))
                      for k in range(res_per_pass)])
                return carry

            lax.fori_loop(0, dil // res_per_pass, body, 0)


def _attention_group(qkv, bias_p, bias_c, gi, dil, nh, bsz, seq):
    span = ATTN_BLOCK * dil
    nb = seq // span
    has_prev = nb > 1
    blocks = max(1, 1024 // span)
    steps = nb // blocks
    blk = (blocks * span, ATTN_SLAB)
    cur = lambda which: (lambda b, n: (b * steps + n, which))
    prev = lambda which: (lambda b, n: (b * nb + jnp.maximum(n * blocks - 1, 0), which))
    fixed = lambda b, n: (0, 0, 0)
    tab = pl.BlockSpec((nh, ATTN_BLOCK, ATTN_BLOCK), fixed)
    if has_prev:
        pblk = (span, ATTN_SLAB)
        in_specs = [pl.BlockSpec(blk, cur(0)), pl.BlockSpec(pblk, prev(1)), pl.BlockSpec(blk, cur(1)),
                    pl.BlockSpec(pblk, prev(2)), pl.BlockSpec(blk, cur(2)), tab, tab]
        args = (qkv, qkv, qkv, qkv, qkv, bias_p, bias_c)
    else:
        in_specs = [pl.BlockSpec(blk, cur(0)), pl.BlockSpec(blk, cur(1)), pl.BlockSpec(blk, cur(2)), tab]
        args = (qkv, qkv, qkv, bias_c)
    n_out = ATTN_SLAB // LANES + 1
    keys = 2 * ATTN_BLOCK if has_prev else ATTN_BLOCK
    slots = min(dil * blocks, ATTN_ITEMS_PER_PASS)
    outs = pl.pallas_call(
        functools.partial(_attn_kernel, nh=nh, dil=dil, has_prev=has_prev, blocks=blocks, per_pass=slots),
        grid=(bsz, steps),
        in_specs=in_specs,
        out_specs=[pl.BlockSpec((blocks * span, LANES), lambda b, n: (b * steps + n, 0))] * n_out,
        out_shape=[jax.ShapeDtypeStruct((bsz * seq, LANES), F32)] * n_out,
        scratch_shapes=[pltpu.VMEM((slots * nh, ATTN_BLOCK, keys), F32),
                        pltpu.VMEM((slots * nh, ATTN_BLOCK, keys), BF16),
                        pltpu.VMEM((slots * nh, ATTN_BLOCK, 1), F32)],
        compiler_params=_cparams("parallel", "parallel"),
        name=f"dilated_attn_g{gi}",
    )(*args)
    return outs[:-1], outs[-1]


def _attn_out_kernel(*refs):
    n_pairs = ATTN_SLAB // LANES
    n_groups = len(ATTN_GROUPS)
    o_refs = refs[:n_groups * n_pairs]
    l_refs = refs[n_groups * n_pairs:n_groups * (n_pairs + 1)]
    w_ref, h_ref, g_ref, b_ref, out_ref, outp_ref = refs[n_groups * (n_pairs + 1):]
    ls = [r[:, 0:1] for r in l_refs]
    m = jnp.maximum(jnp.maximum(ls[0], ls[1]), ls[2])
    es = [jnp.exp(v - m) for v in ls]
    scale = n_groups / (es[0] + es[1] + es[2])
    slabs = []
    for gi in range(n_groups):
        wt = es[gi] * scale
        slabs.extend((r[...] * wt).astype(BF16) for r in o_refs[gi * n_pairs:(gi + 1) * n_pairs])
    acc = _dot(jnp.concatenate(slabs, axis=1), w_ref[...])
    out = _layer_norm(DN_ALPHA * h_ref[...] + acc, g_ref[...], b_ref[...])
    out_ref[...] = out
    outp_ref[...] = _pack_bf16_pairs(out)


def _attn_out(outs, lses, w, h, g, b, tm):
    n = h.shape[0]
    tok = lambda i: (i, 0)
    fixed2 = lambda i: (0, 0)
    return pl.pallas_call(
        _attn_out_kernel,
        grid=(n // tm,),
        in_specs=[pl.BlockSpec((tm, LANES), tok)] * (len(outs) + len(lses))
                 + [pl.BlockSpec((len(ATTN_GROUPS) * ATTN_SLAB, D_MODEL), lambda i: (0, 0)),
                    pl.BlockSpec((tm, D_MODEL), tok),
                    pl.BlockSpec((1, D_MODEL), fixed2), pl.BlockSpec((1, D_MODEL), fixed2)],
        out_specs=[pl.BlockSpec((tm, D_MODEL), tok), pl.BlockSpec((tm, HALF), tok)],
        out_shape=[jax.ShapeDtypeStruct((n, D_MODEL), F32),
                   jax.ShapeDtypeStruct((n, HALF), U32)],
        compiler_params=_cparams("parallel"),
        name="attn_out_ln",
    )(*outs, *lses, w, h, g, b)


def _t5_bucket(dist):
    max_exact = N_BUCKETS // 2
    n = np.maximum(dist, 1).astype(np.float64)
    large = max_exact + (np.log(n / max_exact) / np.log(MAX_DISTANCE / max_exact)
                         * (N_BUCKETS - max_exact)).astype(np.int32)
    large = np.minimum(large, N_BUCKETS - 1)
    return np.where(dist < max_exact, dist, large).astype(np.int32)


def _group_bias(rel_bias, h0, nh, dil):
    qi = np.arange(ATTN_BLOCK)[:, None]
    ki = np.arange(ATTN_BLOCK)[None, :]
    tabs = []
    for delta, band in ((qi + ATTN_BLOCK - ki, ki >= qi), (qi - ki, ki <= qi)):
        bucket = _t5_bucket(np.clip(delta, 0, None) * dil)
        onehot = (bucket[..., None] == np.arange(N_BUCKETS)).astype(np.float32)
        t = jnp.einsum("qkb,bh->hqk", onehot, rel_bias[:, h0:h0 + nh], precision=lax.Precision.HIGHEST)
        tabs.append(jnp.where(band[None], t, NEG_BIG).astype(F32))
    return tabs


def _router_kernel(h_ref, w_ref, b_ref, ints_ref, flts_ref, cnt_ref, carry_ref, before_ref, *, tm):
    @pl.when(pl.program_id(0) == 0)
    def _():
        carry_ref[...] = jnp.zeros_like(carry_ref)
        tr = lax.broadcasted_iota(I32, (tm, tm), 0)
        tc = lax.broadcasted_iota(I32, (tm, tm), 1)
        before_ref[...] = jnp.where(tr < tc, 1.0, 0.0).astype(BF16)

    lt = _dot_f32ish(h_ref[...], w_ref[...]).T + b_ref[...]
    gl = lt[0:MOE_GROUPS]
    r4 = lax.broadcasted_iota(I32, (MOE_GROUPS, tm), 0)
    gmax = jnp.max(gl, 0, keepdims=True)
    gidx = jnp.min(jnp.where(gl == gmax, r4, MOE_GROUPS), 0, keepdims=True)
    gval = 1.0 / jnp.sum(jnp.exp(gl - gmax), 0, keepdims=True)

    esel = jnp.zeros((MOE_EPG, tm), F32)
    for g in range(MOE_GROUPS):
        esel = jnp.where(gidx == g, lt[8 + g * MOE_EPG:8 + (g + 1) * MOE_EPG], esel)
    r8 = lax.broadcasted_iota(I32, (MOE_EPG, tm), 0)
    v1 = jnp.max(esel, 0, keepdims=True)
    i1 = jnp.min(jnp.where(esel == v1, r8, MOE_EPG), 0, keepdims=True)
    rest = jnp.where(r8 == i1, -jnp.inf, esel)
    v2 = jnp.max(rest, 0, keepdims=True)
    i2 = jnp.min(jnp.where(rest == v2, r8, MOE_EPG), 0, keepdims=True)
    t = jnp.exp(v2 - v1)
    p1 = gval / (1.0 + t)
    p2 = p1 * t
    e1 = gidx * MOE_EPG + i1
    e2 = gidx * MOE_EPG + i2

    r32 = lax.broadcasted_iota(I32, (MOE_EXPERTS, tm), 0)
    oh1 = r32 == e1
    oh2 = r32 == e2
    oh = jnp.where(oh1 | oh2, 1.0, 0.0)
    base = _dot(oh.astype(BF16), before_ref[...]) + carry_ref[:, 0:1]
    rank1 = jnp.sum(jnp.where(oh1, base, 0.0), 0, keepdims=True).astype(I32)
    rank2 = jnp.sum(jnp.where(oh2, base, 0.0), 0, keepdims=True).astype(I32)
    carry_ref[...] = carry_ref[...] + jnp.sum(oh, 1, keepdims=True)
    cnt_ref[...] = carry_ref[...]

    ints_ref[...] = jnp.where(r8 == 0, e1, jnp.where(r8 == 1, e2, jnp.where(r8 == 2, rank1,
                              jnp.where(r8 == 3, rank2, 0))))
    r128 = lax.broadcasted_iota(I32, (LANES, tm), 0)
    flts_ref[...] = jnp.where(r128 == 0, p1, jnp.where(r128 == 1, p2, 0.0)).T


def _router(h, w_r, b_r, tm):
    n = h.shape[0]
    return pl.pallas_call(
        functools.partial(_router_kernel, tm=tm),
        grid=(n // tm,),
        in_specs=[pl.BlockSpec((tm, D_MODEL), lambda i: (i, 0)),
                  pl.BlockSpec((D_MODEL, LANES), lambda i: (0, 0)),
                  pl.BlockSpec((LANES, 1), lambda i: (0, 0))],
        out_specs=[pl.BlockSpec((8, tm), lambda i: (0, i)),
                   pl.BlockSpec((tm, LANES), lambda i: (i, 0)),
                   pl.BlockSpec((MOE_EXPERTS, LANES), lambda i: (0, 0))],
        out_shape=[jax.ShapeDtypeStruct((8, n), I32),
                   jax.ShapeDtypeStruct((n, LANES), F32),
                   jax.ShapeDtypeStruct((MOE_EXPERTS, LANES), F32)],
        scratch_shapes=[pltpu.VMEM((MOE_EXPERTS, LANES), F32), pltpu.VMEM((tm, tm), BF16)],
        compiler_params=_cparams("arbitrary"),
        name="moe_router",
    )(h, w_r, b_r)


def _row_copy(src_ref, s, dst_ref, d, sem):
    def first(r):
        return r * ROW_SPLIT if isinstance(r, int) else pl.multiple_of(r * ROW_SPLIT, ROW_SPLIT)

    return pltpu.make_async_copy(src_ref.at[pl.ds(first(s), ROW_SPLIT), :],
                                 dst_ref.at[pl.ds(first(d), ROW_SPLIT), :], sem)


def _tile_wait(src_ref, dst_ref, sem):
    pltpu.make_async_copy(src_ref.at[pl.ds(0, dst_ref.shape[0]), :], dst_ref, sem).wait()


def _experts_kernel(pos1_ref, pos2_ref, te_ref, nxt_ref, na_ref, hp_hbm, zeros_ref, wg_ref, wu_ref, wd_ref,
                    y_ref, hp_ref, xa, xb, wg_f, wu_f, wd_f, wg_s, wu_s, wd_s, row_tok, slot_ref,
                    sem, hsem, wsem):
    i = pl.program_id(0)
    n_tiles = pl.num_programs(0)
    tr = MOE_ROW_TILE
    active = i < na_ref[0]
    changed = jnp.logical_or(i == 0, te_ref[i] != te_ref[jnp.maximum(i - 1, 0)])

    def weight_copies(expert, slot):
        return [pltpu.make_async_copy(src.at[expert], dst.at[slot], wsem.at[slot])
                for src, dst in ((wg_ref, wg_f), (wu_ref, wu_f), (wd_ref, wd_f))]

    @pl.when(i == 0)
    def _():
        slot_ref[0] = 0
        rows_in = pltpu.make_async_copy(hp_hbm, hp_ref, hsem)
        rows_in.start()
        for c in weight_copies(te_ref[0], 0):
            c.start()
        clear = pltpu.make_async_copy(zeros_ref, row_tok, sem)
        clear.start()
        clear.wait()

        def invert(t, carry):
            row_tok[pos1_ref[t]] = t
            row_tok[pos2_ref[t]] = t
            return carry

        lax.fori_loop(0, pos1_ref.shape[0], invert, 0, unroll=16)
        rows_in.wait()

        def pick(r, carry):
            xa[pl.ds(r, 1), :] = hp_ref[pl.ds(row_tok[r], 1), :]
            return carry

        lax.fori_loop(0, tr, pick, 0, unroll=8)

    @pl.when(jnp.logical_and(active, changed))
    def _():
        slot = slot_ref[0]
        for c in weight_copies(te_ref[i], slot):
            c.wait()
        following = nxt_ref[i]

        @pl.when(following >= 0)
        def _():
            for c in weight_copies(following, 1 - slot):
                c.start()

        wg_s[...] = wg_f[slot].astype(BF16)
        wu_s[...] = wu_f[slot].astype(BF16)
        wd_s[...] = wd_f[slot].astype(BF16)
        slot_ref[0] = 1 - slot

    def step(cur, nxt):
        base = jnp.minimum(i + 1, n_tiles - 1) * tr
        for r in range(tr):
            nxt[r:r + 1, :] = hp_ref[pl.ds(row_tok[base + r], 1), :]
        lo, hi = _unpack_bf16_pairs(cur[...])
        gate = _dot(lo, wg_s[0:HALF, :]) + _dot(hi, wg_s[HALF:, :])
        up = _dot(lo, wu_s[0:HALF, :]) + _dot(hi, wu_s[HALF:, :])
        hid = (gate * jax.nn.sigmoid(gate) * up).astype(BF16)
        _store_row_tiled(y_ref, _dot(hid, wd_s[...]))

    even = i % 2 == 0

    @pl.when(jnp.logical_and(active, even))
    def _():
        step(xa, xb)

    @pl.when(jnp.logical_and(active, jnp.logical_not(even)))
    def _():
        step(xb, xa)

    @pl.when(jnp.logical_not(active))
    def _():
        y_ref[...] = jnp.zeros_like(y_ref)


def _experts(hp, pos1, pos2, tile_expert, next_expert, n_active, wg, wu, wd):
    tr = MOE_ROW_TILE
    n_tiles = tile_expert.shape[0]
    rows = n_tiles * tr
    hbm = pl.BlockSpec(memory_space=pl.ANY)
    return pl.pallas_call(
        _experts_kernel,
        grid_spec=pltpu.PrefetchScalarGridSpec(
            num_scalar_prefetch=5,
            grid=(n_tiles,),
            in_specs=[hbm, hbm, hbm, hbm, hbm],
            out_specs=pl.BlockSpec((tr * ROW_SPLIT, LANES), lambda i, *_: (i, 0)),
            scratch_shapes=[pltpu.VMEM(hp.shape, U32),
                            pltpu.VMEM((tr, HALF), U32),
                            pltpu.VMEM((tr, HALF), U32),
                            pltpu.VMEM((2, D_MODEL, MOE_D_FF), F32),
                            pltpu.VMEM((2, D_MODEL, MOE_D_FF), F32),
                            pltpu.VMEM((2, MOE_D_FF, D_MODEL), F32),
                            pltpu.VMEM((D_MODEL, MOE_D_FF), BF16),
                            pltpu.VMEM((D_MODEL, MOE_D_FF), BF16),
                            pltpu.VMEM((MOE_D_FF, D_MODEL), BF16),
                            pltpu.SMEM((rows,), I32),
                            pltpu.SMEM((1,), I32),
                            pltpu.SemaphoreType.DMA(()),
                            pltpu.SemaphoreType.DMA(()),
                            pltpu.SemaphoreType.DMA((2,))]),
        out_shape=jax.ShapeDtypeStruct((rows * ROW_SPLIT, LANES), F32),
        compiler_params=pltpu.CompilerParams(dimension_semantics=("arbitrary",),
                                             vmem_limit_bytes=EXPERTS_VMEM_LIMIT),
        name="moe_experts",
    )(pos1, pos2, tile_expert, next_expert, n_active, hp, jnp.zeros((rows,), I32), wg, wu, wd)


def _combine_kernel(pos1_ref, pos2_ref, y_ref, h_ref, p_ref, g_ref, b_ref, *rest, tm, tiled_copy):
    out_ref = rest[0]
    a1, a2, b1, b2, sem = rest[-5:]
    i = pl.program_id(0)
    last = pl.num_programs(0) - 1

    @pl.when(i == 0)
    def _():
        def issue(t, carry):
            _row_copy(y_ref, pos1_ref[t], a1, t, sem.at[0]).start()
            _row_copy(y_ref, pos2_ref[t], a2, t, sem.at[0]).start()
            return carry

        lax.fori_loop(0, tm, issue, 0)

    def step(c1, c2, cur_sem, n1, n2, nxt_sem):
        _tile_wait(y_ref, c1, cur_sem)
        _tile_wait(y_ref, c2, cur_sem)
        base = jnp.minimum(i + 1, last) * tm
        for t in range(tm):
            _row_copy(y_ref, pos1_ref[base + t], n1, t, nxt_sem).start(priority=0)
            _row_copy(y_ref, pos2_ref[base + t], n2, t, nxt_sem).start(priority=DMA_QUEUES - 1)
        ffn = p_ref[:, 0:1] * _load_row_tiled(c1) + p_ref[:, 1:2] * _load_row_tiled(c2)
        out = _layer_norm(DN_ALPHA * h_ref[...] + ffn, g_ref[...], b_ref[...])
        out_ref[...] = out
        if tiled_copy:
            _store_row_tiled(rest[1], out)

        @pl.when(i == last)
        def _():
            _tile_wait(y_ref, n1, nxt_sem)
            _tile_wait(y_ref, n2, nxt_sem)

    @pl.when(i % 2 == 0)
    def _():
        step(a1, a2, sem.at[0], b1, b2, sem.at[1])

    @pl.when(i % 2 == 1)
    def _():
        step(b1, b2, sem.at[1], a1, a2, sem.at[0])


def _combine(y, h, pos1, pos2, gates, g, b, tm, tiled_copy):
    n = h.shape[0]
    tok = lambda i, a, c: (i, 0)
    fixed = lambda i, a, c: (0, 0)
    out_specs = [pl.BlockSpec((tm, D_MODEL), tok)]
    out_shape = [jax.ShapeDtypeStruct((n, D_MODEL), F32)]
    if tiled_copy:
        out_specs.append(pl.BlockSpec((tm * ROW_SPLIT, LANES), tok))
        out_shape.append(jax.ShapeDtypeStruct((n * ROW_SPLIT, LANES), F32))
    return pl.pallas_call(
        functools.partial(_combine_kernel, tm=tm, tiled_copy=tiled_copy),
        grid_spec=pltpu.PrefetchScalarGridSpec(
            num_scalar_prefetch=2,
            grid=(n // tm,),
            in_specs=[pl.BlockSpec(memory_space=pl.ANY),
                      pl.BlockSpec((tm, D_MODEL), tok),
                      pl.BlockSpec((tm, LANES), tok),
                      pl.BlockSpec((1, D_MODEL), fixed), pl.BlockSpec((1, D_MODEL), fixed)],
            out_specs=out_specs,
            scratch_shapes=[pltpu.VMEM((tm * ROW_SPLIT, LANES), F32)] * 4 + [pltpu.SemaphoreType.DMA((2,))]),
        out_shape=out_shape,
        compiler_params=_cparams("arbitrary"),
        name="moe_combine_ln",
    )(pos1, pos2, y, h, gates, g, b)


def _plan_kernel(ints_ref, cnt_ref, pos_ref, meta_ref, *, layer, chunk):
    tr = MOE_ROW_TILE
    n = ints_ref.shape[1]
    ne = MOE_EXPERTS
    tiles = (cnt_ref[...] + (tr - 1.0)) * (1.0 / tr)
    tiles = tiles.astype(I32).astype(F32)
    lower = lax.broadcasted_iota(I32, (ne, ne), 0) >= lax.broadcasted_iota(I32, (ne, ne), 1)
    ends = _dot(jnp.where(lower, 1.0, 0.0).astype(BF16), tiles.astype(BF16))
    start_col = ((ends - tiles) * tr).astype(I32)[:, 0:1]
    r8 = lax.broadcasted_iota(I32, (8, chunk), 0)
    re = lax.broadcasted_iota(I32, (ne, chunk), 0)
    for c in range(n // chunk):
        blk = ints_ref[:, c * chunk:(c + 1) * chunk]
        s1 = jnp.sum(jnp.where(re == blk[0:1], start_col, 0), 0, keepdims=True)
        s2 = jnp.sum(jnp.where(re == blk[1:2], start_col, 0), 0, keepdims=True)
        pos_ref[:, c * chunk:(c + 1) * chunk] = jnp.where(
            r8 == 0, s1 + blk[2:3], jnp.where(r8 == 1, s2 + blk[3:4], 0))
    width = meta_ref.shape[1]
    tile_id = lax.broadcasted_iota(I32, (ne, width), 1).astype(F32)
    te = jnp.sum(jnp.where(ends[:, 0:1] <= tile_id, 1, 0), 0, keepdims=True)
    te = jnp.minimum(te, ne - 1)
    expert = lax.broadcasted_iota(I32, (ne, width), 0)
    later = jnp.logical_and(expert > te, tiles[:, 0:1] > 0.0)
    nxt = jnp.min(jnp.where(later, expert, ne), 0, keepdims=True)
    nxt = jnp.where(nxt < ne, nxt + layer * ne, -1)
    n_used = ends[ne - 1:ne, 0:1].astype(I32)
    rm = lax.broadcasted_iota(I32, (8, width), 0)
    meta_ref[...] = jnp.where(rm == 0, te + layer * ne,
                              jnp.where(rm == 1, n_used, jnp.where(rm == 2, nxt, 0)))


def _plan(ints, cnt, layer, n_tiles):
    n = ints.shape[1]
    width = -(-n_tiles // LANES) * LANES
    return pl.pallas_call(
        functools.partial(_plan_kernel, layer=layer, chunk=2048),
        out_shape=[jax.ShapeDtypeStruct((8, n), I32), jax.ShapeDtypeStruct((8, width), I32)],
        compiler_params=pltpu.CompilerParams(vmem_limit_bytes=VMEM_LIMIT),
        name="moe_plan",
    )(ints, cnt)


def _moe_layer(h, hp, layer, group_w, group_b, expert_w, expert_b, gate_w, up_w, down_w, ln_g, ln_b,
               tiled_copy):
    n = h.shape[0]
    ew = jnp.transpose(expert_w, (1, 0, 2)).reshape(D_MODEL, MOE_EXPERTS)
    w_r = jnp.zeros((D_MODEL, LANES), F32).at[:, 0:MOE_GROUPS].set(group_w).at[:, 8:8 + MOE_EXPERTS].set(ew)
    b_r = jnp.zeros((LANES,), F32).at[0:MOE_GROUPS].set(group_b).at[8:8 + MOE_EXPERTS].set(expert_b.reshape(-1))
    ints, flts, cnt = _router(h, w_r, b_r.reshape(LANES, 1), 512)
    tr = MOE_ROW_TILE
    n_tiles = (2 * n) // tr + MOE_EXPERTS
    pos, meta = _plan(ints, cnt, layer, n_tiles)
    pos1, pos2 = pos[0], pos[1]

    y = _experts(hp, pos1, pos2, meta[0, :n_tiles], meta[2, :n_tiles], meta[1, :1],
                 gate_w.reshape(-1, D_MODEL, MOE_D_FF),
                 up_w.reshape(-1, D_MODEL, MOE_D_FF),
                 down_w.reshape(-1, MOE_D_FF, D_MODEL))
    return _combine(y, h, pos1, pos2, flts,
                    ln_g.reshape(1, -1), ln_b.reshape(1, -1), COMBINE_TILE, tiled_copy)


def _pad_heads(w, axis):
    parts = []
    h0 = 0
    for _, _, nh in ATTN_GROUPS:
        sl = [slice(None)] * w.ndim
        sl[axis] = slice(h0 * ATTN_HEAD_DIM, (h0 + nh) * ATTN_HEAD_DIM)
        part = w[tuple(sl)]
        pad = [(0, 0)] * w.ndim
        pad[axis] = (0, ATTN_SLAB - nh * ATTN_HEAD_DIM)
        parts.append(jnp.pad(part, pad))
        h0 += nh
    return parts


def _in_proj_kernel(x_ref, w_ref, wdt_ref, z_ref, xbc_ref, dt_ref, *, z_tiles):
    j = pl.program_id(1)
    tn = z_ref.shape[1]
    acc = _dot(x_ref[...].astype(BF16), w_ref[:, pl.ds(pl.multiple_of(j * tn, tn), tn)])

    @pl.when(j < z_tiles)
    def _():
        z_ref[...] = acc

    @pl.when(j >= z_tiles)
    def _():
        xbc_ref[...] = acc

    @pl.when(j == 0)
    def _():
        dt_ref[...] = _dot_f32ish(x_ref[...], wdt_ref[...])


def _in_proj(x, w_zx, w_dt, tm, tn):
    m, k = x.shape
    z_tiles = SSM_D_INNER // tn
    n_tiles = (SSM_D_INNER + SSM_CONV_DIM) // tn
    return pl.pallas_call(
        functools.partial(_in_proj_kernel, z_tiles=z_tiles),
        grid=(m // tm, n_tiles),
        in_specs=[pl.BlockSpec((tm, k), lambda i, j: (i, 0)),
                  pl.BlockSpec(memory_space=pltpu.VMEM),
                  pl.BlockSpec((k, LANES), lambda i, j: (0, 0))],
        out_specs=[pl.BlockSpec((tm, tn), lambda i, j: (i, jnp.minimum(j, z_tiles - 1))),
                   pl.BlockSpec((tm, tn), lambda i, j: (i, jnp.maximum(j - z_tiles, 0))),
                   pl.BlockSpec((tm, LANES), lambda i, j: (i, 0))],
        out_shape=[jax.ShapeDtypeStruct((m, SSM_D_INNER), F32),
                   jax.ShapeDtypeStruct((m, SSM_CONV_DIM), F32),
                   jax.ShapeDtypeStruct((m, LANES), F32)],
        compiler_params=_cparams("parallel", "arbitrary"),
        name="ssm_in_proj",
    )(x, w_zx, w_dt)


def _ssd_layer(h, in_w, conv_w, conv_b, dt_bias, a_log, d_skip, norm_w, out_w, ln_g, ln_b, bsz, seq):
    split = SSM_D_INNER + SSM_CONV_DIM
    dt_w = jnp.pad(in_w[:, split:], ((0, 0), (0, LANES - SSM_HEADS)))
    z, xbc, dt_raw = _in_proj(h, in_w.astype(BF16), dt_w, 1024, 1024)
    pad32 = lambda v: jnp.pad(v, (0, LANES - SSM_HEADS)).reshape(1, LANES)
    y = _ssd(z, xbc, dt_raw, conv_w, conv_b.reshape(1, -1), pad32(dt_bias), pad32(a_log),
             jnp.repeat(d_skip, SSM_HEAD_DIM).reshape(1, -1), norm_w.reshape(1, -1), bsz, seq)
    return _matmul_res_ln(y, out_w.astype(BF16), h, ln_g.reshape(1, -1), ln_b.reshape(1, -1), 1024)


def _qkv_group_weights(kv_w, q_w):
    width = ATTN_HEADS * ATTN_HEAD_DIM
    w_q = _pad_heads(q_w * (ATTN_HEAD_DIM ** -0.5), 1)
    w_k = _pad_heads(kv_w[:, :width], 1)
    w_v = _pad_heads(kv_w[:, width:], 1)
    return [jnp.concatenate([w_q[gi], w_k[gi], w_v[gi]], axis=1).astype(BF16) for gi in range(len(ATTN_GROUPS))]


def _attn_layer(h, h8, qkv_w, o_w, rel_bias, ln_g, ln_b, bsz, seq):
    outs, lses = [], []
    h0 = 0
    tile = 1024
    small = [gi for gi, (_, dil, _) in enumerate(ATTN_GROUPS) if tile % (ATTN_BLOCK * dil) == 0]
    proj = dict(zip(small, _qkv_dilated(h8, [qkv_w[gi] for gi in small],
                                        [ATTN_GROUPS[gi][1] for gi in small], tile)))
    for gi, (_, dil, nh) in enumerate(ATTN_GROUPS):
        if gi in proj:
            qkv = proj[gi]
        else:
            qkv = _qkv_dilated(h8, [qkv_w[gi]], [dil], ATTN_BLOCK * dil)[0]
        bias_p, bias_c = _group_bias(rel_bias, h0, nh, dil)
        o, lse = _attention_group(qkv, bias_p, bias_c, gi, dil, nh, bsz, seq)
        outs.extend(o)
        lses.append(lse)
        h0 += nh
    w_o = jnp.concatenate(_pad_heads(o_w, 0), axis=0).astype(BF16)
    return _attn_out(outs, lses, w_o, h, ln_g.reshape(1, -1), ln_b.reshape(1, -1), 1024)


def kernel(x, ssm_in_w, ssm_conv_w, ssm_conv_b, ssm_dt_bias, ssm_a_log, ssm_d, ssm_norm_w, ssm_out_w,
           kv_w, attn_q_w, attn_o_w, rel_bias, moe_group_w, moe_group_b, moe_expert_w, moe_expert_b,
           moe_gate_w, moe_up_w, moe_down_w, ln_g, ln_b):
    bsz, seq, d = x.shape
    h = x.reshape(bsz * seq, d)
    n_ssd = DEPTH // 2
    h8 = None
    for i in range(DEPTH):
        if i < n_ssd:
            h, hp = _ssd_layer(h, ssm_in_w[i], ssm_conv_w[i], ssm_conv_b[i], ssm_dt_bias[i], ssm_a_log[i],
                               ssm_d[i], ssm_norm_w[i], ssm_out_w[i], ln_g[i, 0], ln_b[i, 0], bsz, seq)
        else:
            j = i - n_ssd
            h, hp = _attn_layer(h, h8, _qkv_group_weights(kv_w, attn_q_w[j]), attn_o_w[j], rel_bias,
                                ln_g[i, 0], ln_b[i, 0], bsz, seq)
        feeds_attention = n_ssd <= i + 1 < DEPTH
        res = _moe_layer(h, hp, i, moe_group_w[i], moe_group_b[i], moe_expert_w[i], moe_expert_b[i],
                         moe_gate_w, moe_up_w, moe_down_w, ln_g[i, 1], ln_b[i, 1], feeds_attention)
        h = res[0]
        h8 = res[1] if feeds_attention else None
    return h.reshape(bsz, seq, d)
```
